```python
import jax, jax.numpy as jnp
from jax import lax
import numpy as np

D_MODEL = 1024
BATCH = 32
SEQ = 2048
DEPTH = 1

CTX_LEN = 256
GRID_W = 64
MIX_WIDTH = D_MODEL
RWKV_WIDTH = MIX_WIDTH // 2
CONV_WIDTH = MIX_WIDTH - RWKV_WIDTH
HEAD_SIZE = 64
RWKV_HEADS = RWKV_WIDTH // HEAD_SIZE
DECAY_RANK = max(32, int(round(1.8 * D_MODEL ** 0.5 / 32)) * 32)
ICLR_RANK = max(32, int(round(1.8 * D_MODEL ** 0.5 / 32)) * 32)
GATE_RANK = max(32, int(round(0.6 * D_MODEL ** 0.8 / 32)) * 32)
CONV_KERNEL = 31
D_FF = 4 * D_MODEL
N_MOD = 6
EPS_RMS = 1e-6
EPS_LN = 1e-5
EPS_GN = 64e-5

IN_SPLITS = (RWKV_WIDTH, RWKV_WIDTH, RWKV_WIDTH, DECAY_RANK, DECAY_RANK,
             ICLR_RANK, ICLR_RANK, GATE_RANK, 2 * CONV_WIDTH)
IN_COLS = sum(IN_SPLITS)
SHIFT_COLS = sum(IN_SPLITS[:-1])
RWKV_CUTS = tuple(int(v) for v in np.cumsum(IN_SPLITS[:-1])[:-1])

kernel_name = "hymba_rwkv7_conformer_dit_block"


def rms_norm(x, g):
    xf = x.astype(jnp.float32)
    y = xf * lax.rsqrt(jnp.mean(xf * xf, axis=-1, keepdims=True) + EPS_RMS)
    return (y * g.astype(jnp.float32)).astype(x.dtype)


def modulate(h, shift, scale):
    return h * (1 + scale) + shift


def split_heads(t):
    return t.reshape(t.shape[:-1] + (RWKV_HEADS, HEAD_SIZE))


def token_shift(z, mu_prev, mu_next):
    zp = jnp.pad(z, ((0, 0), (1, 1), (0, 0)))
    mp = mu_prev.astype(jnp.float32)
    mn = mu_next.astype(jnp.float32)
    return z + mp * (zp[:, :-2] - z) + mn * (zp[:, 2:] - z)


def project_stream(h, w_in, mu_prev, mu_next):
    p = h @ w_in
    rw = token_shift(p[..., :SHIFT_COLS].astype(jnp.float32), mu_prev, mu_next)
    pieces = tuple(jnp.split(rw, RWKV_CUTS, axis=-1))
    return pieces, p[..., SHIFT_COLS:]


def wkv_scan(s0, r, decay, k, v, a_vec, b_vec, reverse):
    xs = tuple(jnp.moveaxis(t, 1, 0) for t in (r, decay, k, v, a_vec, b_vec))

    def step(s, inp):
        r_t, w_t, k_t, v_t, a_t, b_t = inp
        sa = jnp.einsum('bhvk,bhk->bhv', s, a_t)
        s = (s * w_t[:, :, None, :] + sa[..., None] * b_t[:, :, None, :]
             + v_t[..., None] * k_t[:, :, None, :])
        return s, jnp.einsum('bhvk,bhk->bhv', s, r_t)

    s_fin, ys = lax.scan(step, s0, xs, reverse=reverse)
    return s_fin, jnp.moveaxis(ys, 0, 1)


def rwkv_direction(s0, r, k, v, wd, ad, w0, w2, a0, a2, k_k, k_a, reverse):
    w_log = -jax.nn.softplus(-(w0 + jnp.tanh(wd) @ w2)) - 0.5
    decay = jnp.exp(-jnp.exp(w_log))
    iclr = jax.nn.sigmoid(a0 + ad @ a2)
    kk = split_heads(k * k_k)
    kk = kk / jnp.maximum(jnp.sqrt(jnp.sum(kk * kk, axis=-1, keepdims=True)), 1e-12)
    k_dir = split_heads(k * (1 + (iclr - 1) * k_a))
    s_fin, y = wkv_scan(s0, split_heads(r), split_heads(decay), k_dir, split_heads(v),
                        -kk, kk * split_heads(iclr), reverse)
    return s_fin, y, k_dir


def rwkv_bidir(s0_f, s0_b, pieces, decay_w0, decay_w2, iclr_a0, iclr_a2, k_k, k_a):
    r, k, v, wd_f, wd_b, ad_f, ad_b, _ = pieces
    s_f, y_f, kd_f = rwkv_direction(s0_f, r, k, v, wd_f, ad_f, decay_w0[0], decay_w2[0],
                                    iclr_a0[0], iclr_a2[0], k_k, k_a, False)
    s_b, y_b, kd_b = rwkv_direction(s0_b, r, k, v, wd_b, ad_b, decay_w0[1], decay_w2[1],
                                    iclr_a0[1], iclr_a2[1], k_k, k_a, True)
    return s_f, s_b, y_f + y_b, 0.5 * (kd_f + kd_b)


def rwkv_readout(y, k_bar, pieces, r_k, gate_w2, lnx_w, lnx_b):
    r, _, v, _, _, _, _, gd = pieces
    mu = jnp.mean(y, axis=-1, keepdims=True)
    var = jnp.mean(jnp.square(y - mu), axis=-1, keepdims=True)
    yn = ((y - mu) * lax.rsqrt(var + EPS_GN)).reshape(r.shape) * lnx_w + lnx_b
    rh = split_heads(r)
    bonus = jnp.sum(rh * k_bar * r_k, axis=-1, keepdims=True) * split_heads(v)
    g = jax.nn.sigmoid(gd) @ gate_w2
    return (yn + bonus.reshape(r.shape)) * g


def conformer_conv(pcv, n_lines, line_len, conv_w, conv_b, ln_w, ln_b):
    u = pcv[..., :CONV_WIDTH] * jax.nn.sigmoid(pcv[..., CONV_WIDTH:])
    bsz, n_tok, ch = u.shape
    lines = u.reshape(bsz * n_lines, line_len, ch)
    pad = CONV_KERNEL // 2
    y = lax.conv_general_dilated(lines, conv_w[:, None, :].astype(lines.dtype), (1,),
                                 [(pad, pad)], dimension_numbers=('NWC', 'WIO', 'NWC'),
                                 feature_group_count=ch)
    yf = (y.reshape(bsz, n_tok, ch) + conv_b).astype(jnp.float32)
    mu = jnp.mean(yf, axis=-1, keepdims=True)
    var = jnp.mean(jnp.square(yf - mu), axis=-1, keepdims=True)
    yn = (yf - mu) * lax.rsqrt(var + EPS_LN) * ln_w + ln_b
    return jax.nn.silu(yn).astype(pcv.dtype)


def sqrelu_mlp(h, w1, w2):
    return jnp.square(jax.nn.relu(h @ w1)) @ w2


def _fwd_setup_inputs(seed: int = 0) -> dict:
    key = jax.random.key(seed)
    ks = jax.random.split(key, 32)
    L, D, W, CW = DEPTH, D_MODEL, RWKV_WIDTH, CONV_WIDTH
    nrm = lambda k, shape, s: jax.random.normal(k, shape, jnp.float32) * s
    gain = lambda k, shape: 1.0 + nrm(k, shape, 0.02)
    return {
        "x": nrm(ks[0], (BATCH, SEQ, D), 1.0),
        "c": nrm(ks[1], (BATCH, D), 1.0),
        "ctx": nrm(ks[2], (BATCH, CTX_LEN, D), 1.0),
        "c_ctx": nrm(ks[3], (D,), 1.0),
        "ada_w": nrm(ks[4], (L, D, N_MOD * D), 0.5 * D ** -0.5),
        "ada_b": nrm(ks[5], (L, N_MOD * D), 0.02),
        "mix_pre_g": gain(ks[6], (L, D)),
        "mix_post_g": gain(ks[7], (L, D)),
        "mlp_pre_g": gain(ks[8], (L, D)),
        "mlp_post_g": gain(ks[9], (L, D)),
        "w_in": nrm(ks[10], (L, D, IN_COLS), D ** -0.5),
        "mu_prev": jax.random.uniform(ks[11], (L, SHIFT_COLS), jnp.float32, 0.0, 0.5),
        "mu_next": jax.random.uniform(ks[12], (L, SHIFT_COLS), jnp.float32, 0.0, 0.5),
        "decay_w0": jax.random.uniform(ks[13], (L, 2, W), jnp.float32, -6.0, 1.0),
        "decay_w2": nrm(ks[14], (L, 2, DECAY_RANK, W), 0.5 * DECAY_RANK ** -0.5),
        "iclr_a0": nrm(ks[15], (L, 2, W), 0.5),
        "iclr_a2": nrm(ks[16], (L, 2, ICLR_RANK, W), 0.5 * ICLR_RANK ** -0.5),
        "k_k": 0.85 + nrm(ks[17], (L, W), 0.02),
        "k_a": 1.0 + nrm(ks[18], (L, W), 0.02),
        "r_k": nrm(ks[19], (L, RWKV_HEADS, HEAD_SIZE), 0.1),
        "gate_w2": nrm(ks[20], (L, GATE_RANK, W), GATE_RANK ** -0.5),
        "lnx_w": gain(ks[21], (L, W)),
        "lnx_b": nrm(ks[22], (L, W), 0.02),
        "conv_w": nrm(ks[23], (L, CONV_KERNEL, CW), CONV_KERNEL ** -0.5),
        "conv_b": nrm(ks[24], (L, CW), 0.02),
        "conv_ln_w": gain(ks[25], (L, CW)),
        "conv_ln_b": nrm(ks[26], (L, CW), 0.02),
        "w_out": nrm(ks[27], (L, MIX_WIDTH, D), MIX_WIDTH ** -0.5),
        "mlp_w1": nrm(ks[28], (L, D, D_FF), D ** -0.5),
        "mlp_w2": nrm(ks[29], (L, D_FF, D), D_FF ** -0.5),
    }


def _fwd_reference(x, c, ctx, c_ctx, ada_w, ada_b, mix_pre_g, mix_post_g, mlp_pre_g, mlp_post_g,
              w_in, mu_prev, mu_next, decay_w0, decay_w2, iclr_a0, iclr_a2, k_k, k_a, r_k,
              gate_w2, lnx_w, lnx_b, conv_w, conv_b, conv_ln_w, conv_ln_b, w_out,
              mlp_w1, mlp_w2):
    n_rows = x.shape[1] // GRID_W
    s_zero = jnp.zeros((x.shape[0], RWKV_HEADS, HEAD_SIZE, HEAD_SIZE), jnp.float32)
    for l in range(DEPTH):
        update_ctx = l + 1 < DEPTH
        mod_x = jnp.split((jax.nn.silu(c) @ ada_w[l] + ada_b[l])[:, None, :], N_MOD, axis=-1)
        mod_c = jnp.split(jax.nn.silu(c_ctx) @ ada_w[l] + ada_b[l], N_MOD, axis=-1)
        rw_params = (decay_w0[l], decay_w2[l], iclr_a0[l], iclr_a2[l], k_k[l], k_a[l])
        ro_params = (r_k[l], gate_w2[l], lnx_w[l], lnx_b[l])
        cv_params = (conv_w[l], conv_b[l], conv_ln_w[l], conv_ln_b[l])

        hx = modulate(rms_norm(x, mix_pre_g[l]), mod_x[0], mod_x[1])
        hc = modulate(rms_norm(ctx, mix_pre_g[l]), mod_c[0], mod_c[1])
        px, cvx = project_stream(hx, w_in[l], mu_prev[l], mu_next[l])
        pc, cvc = project_stream(hc, w_in[l], mu_prev[l], mu_next[l])
        s_f_c, s_b_c, y_c, kbar_c = rwkv_bidir(s_zero, s_zero, pc, *rw_params)
        _, _, y_x, kbar_x = rwkv_bidir(s_f_c, s_b_c, px, *rw_params)
        mix_x = jnp.concatenate(
            [rwkv_readout(y_x, kbar_x, px, *ro_params).astype(x.dtype),
             conformer_conv(cvx, n_rows, GRID_W, *cv_params)], axis=-1) @ w_out[l]
        x = x + mod_x[2] * rms_norm(mix_x, mix_post_g[l])
        if update_ctx:
            mix_c = jnp.concatenate(
                [rwkv_readout(y_c, kbar_c, pc, *ro_params).astype(ctx.dtype),
                 conformer_conv(cvc, 1, ctx.shape[1], *cv_params)], axis=-1) @ w_out[l]
            ctx = ctx + mod_c[2] * rms_norm(mix_c, mix_post_g[l])

        hx = modulate(rms_norm(x, mlp_pre_g[l]), mod_x[3], mod_x[4])
        x = x + mod_x[5] * rms_norm(sqrelu_mlp(hx, mlp_w1[l], mlp_w2[l]), mlp_post_g[l])
        if update_ctx:
            hc = modulate(rms_norm(ctx, mlp_pre_g[l]), mod_c[3], mod_c[4])
            ctx = ctx + mod_c[5] * rms_norm(sqrelu_mlp(hc, mlp_w1[l], mlp_w2[l]), mlp_post_g[l])
    return x


import jax as _jax
import jax.numpy as _jnp

TWIN_FORMAT = 'train_step'
FWD_PARAMS = ['x', 'c', 'ctx', 'c_ctx', 'ada_w', 'ada_b', 'mix_pre_g', 'mix_post_g', 'mlp_pre_g', 'mlp_post_g', 'w_in', 'mu_prev', 'mu_next', 'decay_w0', 'decay_w2', 'iclr_a0', 'iclr_a2', 'k_k', 'k_a', 'r_k', 'gate_w2', 'lnx_w', 'lnx_b', 'conv_w', 'conv_b', 'conv_ln_w', 'conv_ln_b', 'w_out', 'mlp_w1', 'mlp_w2']
TWIN_WEIGHTS = ['c_ctx', 'ada_w', 'ada_b', 'mix_pre_g', 'mix_post_g', 'mlp_pre_g', 'mlp_post_g', 'w_in', 'mu_prev', 'mu_next', 'decay_w0', 'decay_w2', 'iclr_a0', 'iclr_a2', 'k_k', 'k_a', 'r_k', 'gate_w2', 'lnx_w', 'lnx_b', 'conv_w', 'conv_b', 'conv_ln_w', 'conv_ln_b', 'w_out', 'mlp_w1', 'mlp_w2']
TWIN_DIFF_INPUT = 'x'
TWIN_INPUTS = ['x', 'c', 'ctx', 'c_ctx', 'ada_w', 'ada_b', 'mix_pre_g', 'mix_post_g', 'mlp_pre_g', 'mlp_post_g', 'w_in', 'mu_prev', 'mu_next', 'decay_w0', 'decay_w2', 'iclr_a0', 'iclr_a2', 'k_k', 'k_a', 'r_k', 'gate_w2', 'lnx_w', 'lnx_b', 'conv_w', 'conv_b', 'conv_ln_w', 'conv_ln_b', 'w_out', 'mlp_w1', 'mlp_w2', 'loss_target', 'm_c_ctx', 'm_ada_w', 'm_ada_b', 'm_mix_pre_g', 'm_mix_post_g', 'm_mlp_pre_g', 'm_mlp_post_g', 'm_w_in', 'm_mu_prev', 'm_mu_next', 'm_decay_w0', 'm_decay_w2', 'm_iclr_a0', 'm_iclr_a2', 'm_k_k', 'm_k_a', 'm_r_k', 'm_gate_w2', 'm_lnx_w', 'm_lnx_b', 'm_conv_w', 'm_conv_b', 'm_conv_ln_w', 'm_conv_ln_b', 'm_w_out', 'm_mlp_w1', 'm_mlp_w2', 'v_c_ctx', 'v_ada_w', 'v_ada_b', 'v_mix_pre_g', 'v_mix_post_g', 'v_mlp_pre_g', 'v_mlp_post_g', 'v_w_in', 'v_mu_prev', 'v_mu_next', 'v_decay_w0', 'v_decay_w2', 'v_iclr_a0', 'v_iclr_a2', 'v_k_k', 'v_k_a', 'v_r_k', 'v_gate_w2', 'v_lnx_w', 'v_lnx_b', 'v_conv_w', 'v_conv_b', 'v_conv_ln_w', 'v_conv_ln_b', 'v_w_out', 'v_mlp_w1', 'v_mlp_w2']
TWIN_OUTPUTS = ['loss', 'grad_x', 'grad_c_ctx', 'grad_ada_w', 'grad_ada_b', 'grad_mix_pre_g', 'grad_mix_post_g', 'grad_mlp_pre_g', 'grad_mlp_post_g', 'grad_w_in', 'grad_mu_prev', 'grad_mu_next', 'grad_decay_w0', 'grad_decay_w2', 'grad_iclr_a0', 'grad_iclr_a2', 'grad_k_k', 'grad_k_a', 'grad_r_k', 'grad_gate_w2', 'grad_lnx_w', 'grad_lnx_b', 'grad_conv_w', 'grad_conv_b', 'grad_conv_ln_w', 'grad_conv_ln_b', 'grad_w_out', 'grad_mlp_w1', 'grad_mlp_w2', 'delta_c_ctx', 'delta_ada_w', 'delta_ada_b', 'delta_mix_pre_g', 'delta_mix_post_g', 'delta_mlp_pre_g', 'delta_mlp_post_g', 'delta_w_in', 'delta_mu_prev', 'delta_mu_next', 'delta_decay_w0', 'delta_decay_w2', 'delta_iclr_a0', 'delta_iclr_a2', 'delta_k_k', 'delta_k_a', 'delta_r_k', 'delta_gate_w2', 'delta_lnx_w', 'delta_lnx_b', 'delta_conv_w', 'delta_conv_b', 'delta_conv_ln_w', 'delta_conv_ln_b', 'delta_w_out', 'delta_mlp_w1', 'delta_mlp_w2', 'new_m_c_ctx', 'new_m_ada_w', 'new_m_ada_b', 'new_m_mix_pre_g', 'new_m_mix_post_g', 'new_m_mlp_pre_g', 'new_m_mlp_post_g', 'new_m_w_in', 'new_m_mu_prev', 'new_m_mu_next', 'new_m_decay_w0', 'new_m_decay_w2', 'new_m_iclr_a0', 'new_m_iclr_a2', 'new_m_k_k', 'new_m_k_a', 'new_m_r_k', 'new_m_gate_w2', 'new_m_lnx_w', 'new_m_lnx_b', 'new_m_conv_w', 'new_m_conv_b', 'new_m_conv_ln_w', 'new_m_conv_ln_b', 'new_m_w_out', 'new_m_mlp_w1', 'new_m_mlp_w2', 'new_v_c_ctx', 'new_v_ada_w', 'new_v_ada_b', 'new_v_mix_pre_g', 'new_v_mix_post_g', 'new_v_mlp_pre_g', 'new_v_mlp_post_g', 'new_v_w_in', 'new_v_mu_prev', 'new_v_mu_next', 'new_v_decay_w0', 'new_v_decay_w2', 'new_v_iclr_a0', 'new_v_iclr_a2', 'new_v_k_k', 'new_v_k_a', 'new_v_r_k', 'new_v_gate_w2', 'new_v_lnx_w', 'new_v_lnx_b', 'new_v_conv_w', 'new_v_conv_b', 'new_v_conv_ln_w', 'new_v_conv_ln_b', 'new_v_w_out', 'new_v_mlp_w1', 'new_v_mlp_w2']
TWIN_LEAF_KINDS = {'loss': 'loss', 'grad_x': 'grad_x', 'grad_c_ctx': 'grad_w', 'grad_ada_w': 'grad_w', 'grad_ada_b': 'grad_w', 'grad_mix_pre_g': 'grad_w', 'grad_mix_post_g': 'grad_w', 'grad_mlp_pre_g': 'grad_w', 'grad_mlp_post_g': 'grad_w', 'grad_w_in': 'grad_w', 'grad_mu_prev': 'grad_w', 'grad_mu_next': 'grad_w', 'grad_decay_w0': 'grad_w', 'grad_decay_w2': 'grad_w', 'grad_iclr_a0': 'grad_w', 'grad_iclr_a2': 'grad_w', 'grad_k_k': 'grad_w', 'grad_k_a': 'grad_w', 'grad_r_k': 'grad_w', 'grad_gate_w2': 'grad_w', 'grad_lnx_w': 'grad_w', 'grad_lnx_b': 'grad_w', 'grad_conv_w': 'grad_w', 'grad_conv_b': 'grad_w', 'grad_conv_ln_w': 'grad_w', 'grad_conv_ln_b': 'grad_w', 'grad_w_out': 'grad_w', 'grad_mlp_w1': 'grad_w', 'grad_mlp_w2': 'grad_w', 'delta_c_ctx': 'delta_w', 'delta_ada_w': 'delta_w', 'delta_ada_b': 'delta_w', 'delta_mix_pre_g': 'delta_w', 'delta_mix_post_g': 'delta_w', 'delta_mlp_pre_g': 'delta_w', 'delta_mlp_post_g': 'delta_w', 'delta_w_in': 'delta_w', 'delta_mu_prev': 'delta_w', 'delta_mu_next': 'delta_w', 'delta_decay_w0': 'delta_w', 'delta_decay_w2': 'delta_w', 'delta_iclr_a0': 'delta_w', 'delta_iclr_a2': 'delta_w', 'delta_k_k': 'delta_w', 'delta_k_a': 'delta_w', 'delta_r_k': 'delta_w', 'delta_gate_w2': 'delta_w', 'delta_lnx_w': 'delta_w', 'delta_lnx_b': 'delta_w', 'delta_conv_w': 'delta_w', 'delta_conv_b': 'delta_w', 'delta_conv_ln_w': 'delta_w', 'delta_conv_ln_b': 'delta_w', 'delta_w_out': 'delta_w', 'delta_mlp_w1': 'delta_w', 'delta_mlp_w2': 'delta_w', 'new_m_c_ctx': 'new_m', 'new_m_ada_w': 'new_m', 'new_m_ada_b': 'new_m', 'new_m_mix_pre_g': 'new_m', 'new_m_mix_post_g': 'new_m', 'new_m_mlp_pre_g': 'new_m', 'new_m_mlp_post_g': 'new_m', 'new_m_w_in': 'new_m', 'new_m_mu_prev': 'new_m', 'new_m_mu_next': 'new_m', 'new_m_decay_w0': 'new_m', 'new_m_decay_w2': 'new_m', 'new_m_iclr_a0': 'new_m', 'new_m_iclr_a2': 'new_m', 'new_m_k_k': 'new_m', 'new_m_k_a': 'new_m', 'new_m_r_k': 'new_m', 'new_m_gate_w2': 'new_m', 'new_m_lnx_w': 'new_m', 'new_m_lnx_b': 'new_m', 'new_m_conv_w': 'new_m', 'new_m_conv_b': 'new_m', 'new_m_conv_ln_w': 'new_m', 'new_m_conv_ln_b': 'new_m', 'new_m_w_out': 'new_m', 'new_m_mlp_w1': 'new_m', 'new_m_mlp_w2': 'new_m', 'new_v_c_ctx': 'new_v', 'new_v_ada_w': 'new_v', 'new_v_ada_b': 'new_v', 'new_v_mix_pre_g': 'new_v', 'new_v_mix_post_g': 'new_v', 'new_v_mlp_pre_g': 'new_v', 'new_v_mlp_post_g': 'new_v', 'new_v_w_in': 'new_v', 'new_v_mu_prev': 'new_v', 'new_v_mu_next': 'new_v', 'new_v_decay_w0': 'new_v', 'new_v_decay_w2': 'new_v', 'new_v_iclr_a0': 'new_v', 'new_v_iclr_a2': 'new_v', 'new_v_k_k': 'new_v', 'new_v_k_a': 'new_v', 'new_v_r_k': 'new_v', 'new_v_gate_w2': 'new_v', 'new_v_lnx_w': 'new_v', 'new_v_lnx_b': 'new_v', 'new_v_conv_w': 'new_v', 'new_v_conv_b': 'new_v', 'new_v_conv_ln_w': 'new_v', 'new_v_conv_ln_b': 'new_v', 'new_v_w_out': 'new_v', 'new_v_mlp_w1': 'new_v', 'new_v_mlp_w2': 'new_v'}


def _forward(args):
    return _fwd_reference(*[args[k] for k in FWD_PARAMS])


def _output_shape():
    out = _jax.eval_shape(lambda: _forward(_fwd_setup_inputs(0)))
    return out.shape, out.dtype

N_MICROBATCH = 1
ADAM_LR = 0.001
ADAM_B1 = 0.9
ADAM_B2 = 0.999
ADAM_EPS = 1e-08
ADAM_WD = 0.01
ADAM_STEP = 10
PER_EXAMPLE_BATCH_AXIS = {'x': 0, 'c': 0, 'ctx': 0, 'loss_target': 0}
SHARED_INPUTS = []
_WEIGHT_DTYPES = {'c_ctx': _jnp.float32, 'ada_w': _jnp.float32, 'ada_b': _jnp.float32, 'mix_pre_g': _jnp.float32, 'mix_post_g': _jnp.float32, 'mlp_pre_g': _jnp.float32, 'mlp_post_g': _jnp.float32, 'w_in': _jnp.float32, 'mu_prev': _jnp.float32, 'mu_next': _jnp.float32, 'decay_w0': _jnp.float32, 'decay_w2': _jnp.float32, 'iclr_a0': _jnp.float32, 'iclr_a2': _jnp.float32, 'k_k': _jnp.float32, 'k_a': _jnp.float32, 'r_k': _jnp.float32, 'gate_w2': _jnp.float32, 'lnx_w': _jnp.float32, 'lnx_b': _jnp.float32, 'conv_w': _jnp.float32, 'conv_b': _jnp.float32, 'conv_ln_w': _jnp.float32, 'conv_ln_b': _jnp.float32, 'w_out': _jnp.float32, 'mlp_w1': _jnp.float32, 'mlp_w2': _jnp.float32}
MOMENT_SCALE = {'c_ctx': 3.486293e-02, 'ada_w': 3.594579e+00, 'ada_b': 6.774195e+00, 'mix_pre_g': 2.424447e-01, 'mix_post_g': 7.463289e+00, 'mlp_pre_g': 2.031605e-01, 'mlp_post_g': 7.966108e+00, 'w_in': 2.366553e-01, 'mu_prev': 2.903510e-01, 'mu_next': 3.049621e-01, 'decay_w0': 7.800639e-02, 'decay_w2': 1.629194e-02, 'iclr_a0': 4.080541e-02, 'iclr_a2': 3.816421e-02, 'k_k': 1.279467e+00, 'k_a': 1.239431e+00, 'r_k': 2.694377e-01, 'gate_w2': 3.064231e-01, 'lnx_w': 5.982874e-01, 'lnx_b': 1.540745e+00, 'conv_w': 2.828413e-01, 'conv_b': 2.845445e+00, 'conv_ln_w': 9.161957e-01, 'conv_ln_b': 1.573157e+00, 'w_out': 4.755222e-01, 'mlp_w1': 1.604005e-01, 'mlp_w2': 8.059219e-01}


def _to_microbatches(a, axis):
    t = _jnp.moveaxis(a, axis, 0)
    t = t.reshape((N_MICROBATCH, t.shape[0] // N_MICROBATCH) + t.shape[1:])
    return _jnp.moveaxis(t, 1, axis + 1)


def setup_inputs(seed: int = 0) -> dict:
    inp = _fwd_setup_inputs(seed)
    key = _jax.random.fold_in(_jax.random.key(seed), 7919)
    shape, _ = _output_shape()
    out = dict(inp)
    out["loss_target"] = _jax.random.normal(_jax.random.fold_in(key, 0), shape, _jnp.float32)
    for i, name in enumerate(TWIN_WEIGHTS):
        w = inp[name].astype(_jnp.float32)
        if MOMENT_SCALE is None:
            s = _jnp.sqrt(_jnp.mean(_jnp.square(w)) + 1e-30)
        else:
            s = MOMENT_SCALE[name]
        km, kv = _jax.random.split(_jax.random.fold_in(key, i + 1))
        out[name] = w
        out["m_" + name] = s * _jax.random.normal(km, w.shape, _jnp.float32)
        out["v_" + name] = (s * s) * _jax.random.uniform(kv, w.shape, _jnp.float32, 0.5, 1.5)
    if N_MICROBATCH > 1:
        for name, axis in PER_EXAMPLE_BATCH_AXIS.items():
            out[name] = _to_microbatches(out[name], axis)
    return {'x': out['x'], 'c': out['c'], 'ctx': out['ctx'], 'c_ctx': out['c_ctx'], 'ada_w': out['ada_w'], 'ada_b': out['ada_b'], 'mix_pre_g': out['mix_pre_g'], 'mix_post_g': out['mix_post_g'], 'mlp_pre_g': out['mlp_pre_g'], 'mlp_post_g': out['mlp_post_g'], 'w_in': out['w_in'], 'mu_prev': out['mu_prev'], 'mu_next': out['mu_next'], 'decay_w0': out['decay_w0'], 'decay_w2': out['decay_w2'], 'iclr_a0': out['iclr_a0'], 'iclr_a2': out['iclr_a2'], 'k_k': out['k_k'], 'k_a': out['k_a'], 'r_k': out['r_k'], 'gate_w2': out['gate_w2'], 'lnx_w': out['lnx_w'], 'lnx_b': out['lnx_b'], 'conv_w': out['conv_w'], 'conv_b': out['conv_b'], 'conv_ln_w': out['conv_ln_w'], 'conv_ln_b': out['conv_ln_b'], 'w_out': out['w_out'], 'mlp_w1': out['mlp_w1'], 'mlp_w2': out['mlp_w2'], 'loss_target': out['loss_target'], 'm_c_ctx': out['m_c_ctx'], 'm_ada_w': out['m_ada_w'], 'm_ada_b': out['m_ada_b'], 'm_mix_pre_g': out['m_mix_pre_g'], 'm_mix_post_g': out['m_mix_post_g'], 'm_mlp_pre_g': out['m_mlp_pre_g'], 'm_mlp_post_g': out['m_mlp_post_g'], 'm_w_in': out['m_w_in'], 'm_mu_prev': out['m_mu_prev'], 'm_mu_next': out['m_mu_next'], 'm_decay_w0': out['m_decay_w0'], 'm_decay_w2': out['m_decay_w2'], 'm_iclr_a0': out['m_iclr_a0'], 'm_iclr_a2': out['m_iclr_a2'], 'm_k_k': out['m_k_k'], 'm_k_a': out['m_k_a'], 'm_r_k': out['m_r_k'], 'm_gate_w2': out['m_gate_w2'], 'm_lnx_w': out['m_lnx_w'], 'm_lnx_b': out['m_lnx_b'], 'm_conv_w': out['m_conv_w'], 'm_conv_b': out['m_conv_b'], 'm_conv_ln_w': out['m_conv_ln_w'], 'm_conv_ln_b': out['m_conv_ln_b'], 'm_w_out': out['m_w_out'], 'm_mlp_w1': out['m_mlp_w1'], 'm_mlp_w2': out['m_mlp_w2'], 'v_c_ctx': out['v_c_ctx'], 'v_ada_w': out['v_ada_w'], 'v_ada_b': out['v_ada_b'], 'v_mix_pre_g': out['v_mix_pre_g'], 'v_mix_post_g': out['v_mix_post_g'], 'v_mlp_pre_g': out['v_mlp_pre_g'], 'v_mlp_post_g': out['v_mlp_post_g'], 'v_w_in': out['v_w_in'], 'v_mu_prev': out['v_mu_prev'], 'v_mu_next': out['v_mu_next'], 'v_decay_w0': out['v_decay_w0'], 'v_decay_w2': out['v_decay_w2'], 'v_iclr_a0': out['v_iclr_a0'], 'v_iclr_a2': out['v_iclr_a2'], 'v_k_k': out['v_k_k'], 'v_k_a': out['v_k_a'], 'v_r_k': out['v_r_k'], 'v_gate_w2': out['v_gate_w2'], 'v_lnx_w': out['v_lnx_w'], 'v_lnx_b': out['v_lnx_b'], 'v_conv_w': out['v_conv_w'], 'v_conv_b': out['v_conv_b'], 'v_conv_ln_w': out['v_conv_ln_w'], 'v_conv_ln_b': out['v_conv_ln_b'], 'v_w_out': out['v_w_out'], 'v_mlp_w1': out['v_mlp_w1'], 'v_mlp_w2': out['v_mlp_w2']}


def _loss(weights, diff, rest, loss_target):
    with _jax.named_scope("forward"):
        args = {**rest, TWIN_DIFF_INPUT: diff, **{k: w.astype(_WEIGHT_DTYPES[k]) for k, w in weights.items()}}
        y = _forward(args)
    with _jax.named_scope("loss_head"):
        err = _jnp.square(y.astype(_jnp.float32) - loss_target)
        return 0.5 * _jnp.sum(_jnp.mean(err, axis=-1)) if err.ndim else 0.5 * err


def _adamw(w, g, m, v):
    m = ADAM_B1 * m + (1.0 - ADAM_B1) * g
    v = ADAM_B2 * v + (1.0 - ADAM_B2) * _jnp.square(g)
    m_hat = m / (1.0 - ADAM_B1 ** ADAM_STEP)
    v_hat = v / (1.0 - ADAM_B2 ** ADAM_STEP)
    delta = -ADAM_LR * (m_hat / (_jnp.sqrt(v_hat) + ADAM_EPS) + ADAM_WD * w)
    return delta, m, v


def reference(x, c, ctx, c_ctx, ada_w, ada_b, mix_pre_g, mix_post_g, mlp_pre_g, mlp_post_g, w_in, mu_prev, mu_next, decay_w0, decay_w2, iclr_a0, iclr_a2, k_k, k_a, r_k, gate_w2, lnx_w, lnx_b, conv_w, conv_b, conv_ln_w, conv_ln_b, w_out, mlp_w1, mlp_w2, loss_target, m_c_ctx, m_ada_w, m_ada_b, m_mix_pre_g, m_mix_post_g, m_mlp_pre_g, m_mlp_post_g, m_w_in, m_mu_prev, m_mu_next, m_decay_w0, m_decay_w2, m_iclr_a0, m_iclr_a2, m_k_k, m_k_a, m_r_k, m_gate_w2, m_lnx_w, m_lnx_b, m_conv_w, m_conv_b, m_conv_ln_w, m_conv_ln_b, m_w_out, m_mlp_w1, m_mlp_w2, v_c_ctx, v_ada_w, v_ada_b, v_mix_pre_g, v_mix_post_g, v_mlp_pre_g, v_mlp_post_g, v_w_in, v_mu_prev, v_mu_next, v_decay_w0, v_decay_w2, v_iclr_a0, v_iclr_a2, v_k_k, v_k_a, v_r_k, v_gate_w2, v_lnx_w, v_lnx_b, v_conv_w, v_conv_b, v_conv_ln_w, v_conv_ln_b, v_w_out, v_mlp_w1, v_mlp_w2):
    given = dict(x=x, c=c, ctx=ctx, c_ctx=c_ctx, ada_w=ada_w, ada_b=ada_b, mix_pre_g=mix_pre_g, mix_post_g=mix_post_g, mlp_pre_g=mlp_pre_g, mlp_post_g=mlp_post_g, w_in=w_in, mu_prev=mu_prev, mu_next=mu_next, decay_w0=decay_w0, decay_w2=decay_w2, iclr_a0=iclr_a0, iclr_a2=iclr_a2, k_k=k_k, k_a=k_a, r_k=r_k, gate_w2=gate_w2, lnx_w=lnx_w, lnx_b=lnx_b, conv_w=conv_w, conv_b=conv_b, conv_ln_w=conv_ln_w, conv_ln_b=conv_ln_b, w_out=w_out, mlp_w1=mlp_w1, mlp_w2=mlp_w2, loss_target=loss_target, m_c_ctx=m_c_ctx, m_ada_w=m_ada_w, m_ada_b=m_ada_b, m_mix_pre_g=m_mix_pre_g, m_mix_post_g=m_mix_post_g, m_mlp_pre_g=m_mlp_pre_g, m_mlp_post_g=m_mlp_post_g, m_w_in=m_w_in, m_mu_prev=m_mu_prev, m_mu_next=m_mu_next, m_decay_w0=m_decay_w0, m_decay_w2=m_decay_w2, m_iclr_a0=m_iclr_a0, m_iclr_a2=m_iclr_a2, m_k_k=m_k_k, m_k_a=m_k_a, m_r_k=m_r_k, m_gate_w2=m_gate_w2, m_lnx_w=m_lnx_w, m_lnx_b=m_lnx_b, m_conv_w=m_conv_w, m_conv_b=m_conv_b, m_conv_ln_w=m_conv_ln_w, m_conv_ln_b=m_conv_ln_b, m_w_out=m_w_out, m_mlp_w1=m_mlp_w1, m_mlp_w2=m_mlp_w2, v_c_ctx=v_c_ctx, v_ada_w=v_ada_w, v_ada_b=v_ada_b, v_mix_pre_g=v_mix_pre_g, v_mix_post_g=v_mix_post_g, v_mlp_pre_g=v_mlp_pre_g, v_mlp_post_g=v_mlp_post_g, v_w_in=v_w_in, v_mu_prev=v_mu_prev, v_mu_next=v_mu_next, v_decay_w0=v_decay_w0, v_decay_w2=v_decay_w2, v_iclr_a0=v_iclr_a0, v_iclr_a2=v_iclr_a2, v_k_k=v_k_k, v_k_a=v_k_a, v_r_k=v_r_k, v_gate_w2=v_gate_w2, v_lnx_w=v_lnx_w, v_lnx_b=v_lnx_b, v_conv_w=v_conv_w, v_conv_b=v_conv_b, v_conv_ln_w=v_conv_ln_w, v_conv_ln_b=v_conv_ln_b, v_w_out=v_w_out, v_mlp_w1=v_mlp_w1, v_mlp_w2=v_mlp_w2)
    weights = {n: given[n] for n in TWIN_WEIGHTS}
    shared = {n: given[n] for n in SHARED_INPUTS}
    per_example = {n: given[n] for n in ['x', 'c', 'ctx']}
    grad_fn = _jax.value_and_grad(_loss, argnums=(0, 1))

    def one_microbatch(ex, loss_target):
        ex = dict(ex)
        diff = ex.pop(TWIN_DIFF_INPUT)
        return grad_fn(weights, diff, {**shared, **ex}, loss_target)

    if N_MICROBATCH == 1:
        loss, (grad_w, grad_x) = one_microbatch(per_example, given["loss_target"])
    else:
        def body(carry, xs):
            loss_sum, grad_sum = carry
            l_k, (gw_k, gx_k) = one_microbatch(xs[0], xs[1])
            with _jax.named_scope("update"):
                return (loss_sum + l_k, _jax.tree.map(_jnp.add, grad_sum, gw_k)), gx_k

        init = (_jnp.zeros((), _jnp.float32), _jax.tree.map(_jnp.zeros_like, weights))
        (loss, grad_w), grad_x = _jax.lax.scan(body, init, (per_example, given["loss_target"]))
    with _jax.named_scope("update"):
        delta_w, new_m, new_v = {}, {}, {}
        for n in TWIN_WEIGHTS:
            delta_w[n], new_m[n], new_v[n] = _adamw(weights[n], grad_w[n], given["m_" + n], given["v_" + n])
    return (loss, grad_x, *[grad_w[n] for n in TWIN_WEIGHTS], *[delta_w[n] for n in TWIN_WEIGHTS],
            *[new_m[n] for n in TWIN_WEIGHTS], *[new_v[n] for n in TWIN_WEIGHTS])
```

```python
import functools

import jax
import jax.numpy as jnp
from jax import lax
from jax.experimental import pallas as pl
from jax.experimental.pallas import tpu as pltpu

F32 = jnp.float32
BF16 = jnp.bfloat16
HI = lax.Precision.HIGHEST

D = 1024
W = 512
HS = 64
RWC = 2304
CVC = 1024
GDW = 256
LRW = 128
DFF = 4096
TT = 256
LINE = 64
KCONV = 31
EPS_RMS = 1e-6
EPS_LN = 1e-5
EPS_GN = 64e-5
SCAN_CH = 128
SCAN_SUB = 32

ADAM_LR = 0.001
ADAM_B1 = 0.9
ADAM_B2 = 0.999
ADAM_EPS = 1e-08
ADAM_WD = 0.01
ADAM_STEP = 10

_SEGS = ((0, 1536, 1536), (1536, 64, 128), (1600, 64, 128), (1664, 64, 128), (1728, 64, 128),
         (1792, 160, 256), (1952, 1024, 1024))

MESH = pl.DeviceIdType.MESH


def _bs(shape, imap):
    return pl.BlockSpec(shape, imap)


def _cp(sem=None, mb=48):
    return pltpu.CompilerParams(dimension_semantics=sem, vmem_limit_bytes=mb << 20)


def _pad_cols(a, ncols):
    out = []
    for s, w, pw in _SEGS:
        if s >= ncols:
            break
        piece = a[..., s:s + w]
        if pw > w:
            piece = jnp.pad(piece, [(0, 0)] * (a.ndim - 1) + [(0, pw - w)])
        out.append(piece)
    return jnp.concatenate(out, axis=-1)


def _unpad_cols(a, ncols):
    out = []
    off = 0
    for s, w, pw in _SEGS:
        if s >= ncols:
            break
        out.append(a[..., off:off + w])
        off += pw
    return jnp.concatenate(out, axis=-1)


def _sigmoid(x):
    return 1.0 / (1.0 + jnp.exp(-x))


def _softplus(x):
    return jnp.maximum(x, 0.0) + jnp.log(1.0 + jnp.exp(-jnp.abs(x)))


def _e128(dtype):
    r = lax.broadcasted_iota(jnp.int32, (128, 128), 0) >= HS
    c = lax.broadcasted_iota(jnp.int32, (128, 128), 1) >= HS
    return (r == c).astype(dtype)


def _segsum(x, e):
    return jnp.concatenate(
        [jnp.dot(x[:, 128 * g:128 * (g + 1)], e, precision=HI, preferred_element_type=F32) for g in range(4)],
        axis=1)


def _colsum(x):
    return jnp.sum(x, axis=0, keepdims=True)


def _rowmean(x):
    return jnp.mean(x, axis=-1, keepdims=True)


def _imask64():
    row = lax.broadcasted_iota(jnp.int32, (HS, 128), 0)
    lane = lax.broadcasted_iota(jnp.int32, (HS, 128), 1)
    return jnp.where(lane >= HS, lane - HS, lane) == row


def _segb(x, e):
    hi = x.astype(BF16)
    lo = (x - hi.astype(F32)).astype(BF16)
    return jnp.dot(hi, e, preferred_element_type=F32) + jnp.dot(lo, e, preferred_element_type=F32)


def _colb(row, im, e):
    return _segb(jnp.where(im, jnp.broadcast_to(row, (HS, 128)), 0.0), e)


def _conv_pos():
    return lax.broadcasted_iota(jnp.int32, (TT, W), 0) & (LINE - 1)


def _shifted(u, s, pos):
    if s == 0:
        return u
    sh = pltpu.roll(u, (-s) % TT, 0)
    valid = jnp.logical_and(pos + s >= 0, pos + s < LINE)
    return jnp.where(valid, sh, 0.0)


def _acc(ref, val, first):
    @pl.when(first)
    def _():
        ref[...] = jnp.zeros(ref.shape, ref.dtype)
    ref[...] += val


class _Tiles:
    def __init__(self, B, t_lat):
        self.B = B
        self.NLT = t_lat // TT
        self.TPS = self.NLT + 1
        self.NT = B * self.TPS
        self.NL = B * self.NLT
        self.NTOK = self.NT * TT
        self.NLAT = self.NL * TT
        self.TTOT = self.TPS * TT
        self.BW = B * W

    def b(self, i):
        return i // self.TPS

    def q(self, i):
        return i % self.TPS

    def lat(self, i):
        return (i // self.TPS) * self.NLT + jnp.maximum(i % self.TPS - 1, 0)

    def tok(self, l):
        return (l // self.NLT) * self.TPS + 1 + l % self.NLT

    def mod_spec(self):
        return _bs((1, 1, 6, D), lambda i: (i // self.TPS, jnp.minimum(i % self.TPS, 1), 0, 0))

    def tm_spec(self):
        return _bs((TT, W), lambda i: (i % self.TPS, i // self.TPS))

    def tm2_spec(self):
        return _bs((2, TT, W), lambda i: (0, i % self.TPS, i // self.TPS))


def _row(shape_last):
    return _bs((1, shape_last), lambda i: (0, 0))


def _mix_in(T, xc, modrows, g, w_rw, w_cv):
    def kern(x_ref, mod_ref, g_ref, wr_ref, wc_ref, prw_ref, pcv_ref, h_ref):
        x = x_ref[...]
        s = lax.rsqrt(_rowmean(x * x) + EPS_RMS)
        h = (x * s * g_ref[...]) * (1.0 + mod_ref[0, 0, 1:2, :]) + mod_ref[0, 0, 0:1, :]
        hb = h.astype(BF16)
        h_ref[...] = hb
        prw_ref[...] = jnp.dot(hb, wr_ref[...], preferred_element_type=F32)
        pcv_ref[...] = jnp.dot(hb, wc_ref[...], preferred_element_type=F32)

    return pl.pallas_call(
        kern, grid=(T.NT,), name="mix_in",
        in_specs=[_bs((TT, D), lambda i: (i, 0)), T.mod_spec(), _row(D),
                  _bs((D, RWC), lambda i: (0, 0)), _bs((D, CVC), lambda i: (0, 0))],
        out_specs=[_bs((TT, RWC), lambda i: (i, 0)), _bs((TT, CVC), lambda i: (i, 0)), _bs((TT, D), lambda i: (i, 0))],
        out_shape=[jax.ShapeDtypeStruct((T.NTOK, RWC), F32), jax.ShapeDtypeStruct((T.NTOK, CVC), F32),
                   jax.ShapeDtypeStruct((T.NTOK, D), BF16)],
        compiler_params=_cp(("parallel",)),
    )(xc, modrows, g, w_rw, w_cv)


def _halo_specs(T):
    nb8 = T.NTOK // 8
    prev = _bs((8, RWC), lambda i: (jnp.maximum(i * (TT // 8) - 1, 0), 0))
    nxt = _bs((8, RWC), lambda i: (jnp.minimum((i + 1) * (TT // 8), nb8 - 1), 0))
    return prev, nxt


def _halo_masks(T, i):
    q = i % T.TPS
    has_prev = jnp.logical_and(q != 0, q != 1).astype(F32)
    has_next = jnp.logical_and(q != 0, q != T.TPS - 1).astype(F32)
    return has_prev, has_next


def _neighbours(z, prev_row, next_row):
    rowi = lax.broadcasted_iota(jnp.int32, z.shape, 0)
    zprev = jnp.where(rowi == 0, prev_row, pltpu.roll(z, 1, 0))
    znext = jnp.where(rowi == TT - 1, next_row, pltpu.roll(z, TT - 1, 0))
    return zprev, znext


def _prep_math(rw, w0, w2, a0, a2, k_k, k_a, e):
    r = rw[:, 0:512]
    k = rw[:, 512:1024]
    v = rw[:, 1024:1536]
    kr = k * k_k
    ss = _segsum(kr * kr, e)
    rt = jnp.sqrt(ss)
    inv = 1.0 / jnp.maximum(rt, 1e-12)
    kk = kr * inv
    o = dict(r=r, k=k, v=v, kr=kr, rt=rt, inv=inv, kk=kk, th=[], pre=[], ex=[], dec=[], iclr=[], kd=[], bb=[], ad=[])
    for d in (0, 1):
        wd = rw[:, 1536 + LRW * d:1536 + LRW * (d + 1)]
        ad = rw[:, 1792 + LRW * d:1792 + LRW * (d + 1)]
        th = jnp.tanh(wd)
        pre = w0[d] + jnp.dot(th, w2[d], precision=HI, preferred_element_type=F32)
        ex = jnp.exp(-_softplus(-pre) - 0.5)
        dec = jnp.exp(-ex)
        iclr = _sigmoid(a0[d] + jnp.dot(ad, a2[d], precision=HI, preferred_element_type=F32))
        o["th"].append(th)
        o["pre"].append(pre)
        o["ex"].append(ex)
        o["dec"].append(dec)
        o["iclr"].append(iclr)
        o["ad"].append(ad)
        o["kd"].append(k * (1.0 + (iclr - 1.0) * k_a))
        o["bb"].append(kk * iclr)
    return o


def _load_prep_params(w0_ref, w2_ref, a0_ref, a2_ref):
    w0 = [w0_ref[0:1, :], w0_ref[1:2, :]]
    a0 = [a0_ref[0:1, :], a0_ref[1:2, :]]
    w2 = [w2_ref[0], w2_ref[1]]
    a2 = [a2_ref[0], a2_ref[1]]
    return w0, w2, a0, a2


def _prep_param_specs():
    return [_bs((2, W), lambda i: (0, 0)), _bs((2, LRW, W), lambda i: (0, 0, 0)),
            _bs((2, W), lambda i: (0, 0)), _bs((2, LRW, W), lambda i: (0, 0, 0)), _row(W), _row(W)]


def _rwkv_prep(T, p_rw, mu_p, mu_n, w0, w2, a0, a2, k_k, k_a):
    def kern(p_ref, pp_ref, pn_ref, mp_ref, mn_ref, w0_ref, w2_ref, a0_ref, a2_ref, kk_ref, ka_ref,
             r_o, v_o, kk_o, dec_o, kd_o, bb_o, rw_o):
        i = pl.program_id(0)
        has_prev, has_next = _halo_masks(T, i)
        z = p_ref[...]
        zprev, znext = _neighbours(z, pp_ref[7:8, :] * has_prev, pn_ref[0:1, :] * has_next)
        rw = z + mp_ref[...] * (zprev - z) + mn_ref[...] * (znext - z)
        rw_o[...] = rw
        w0v, w2v, a0v, a2v = _load_prep_params(w0_ref, w2_ref, a0_ref, a2_ref)
        o = _prep_math(rw, w0v, w2v, a0v, a2v, kk_ref[...], ka_ref[...], _e128(F32))
        r_o[...] = o["r"]
        v_o[...] = o["v"]
        kk_o[...] = o["kk"]
        for d in (0, 1):
            dec_o[d] = o["dec"][d]
            kd_o[d] = o["kd"][d]
            bb_o[d] = o["bb"][d]

    prev, nxt = _halo_specs(T)
    tm = jax.ShapeDtypeStruct((T.TTOT, T.BW), F32)
    tm2 = jax.ShapeDtypeStruct((2, T.TTOT, T.BW), F32)
    return pl.pallas_call(
        kern, grid=(T.NT,), name="rwkv_prep",
        in_specs=[_bs((TT, RWC), lambda i: (i, 0)), prev, nxt, _row(RWC), _row(RWC)] + _prep_param_specs(),
        out_specs=[T.tm_spec(), T.tm_spec(), T.tm_spec(), T.tm2_spec(), T.tm2_spec(), T.tm2_spec(),
                   _bs((TT, RWC), lambda i: (i, 0))],
        out_shape=[tm, tm, tm, tm2, tm2, tm2, jax.ShapeDtypeStruct((T.NTOK, RWC), F32)],
        compiler_params=_cp(("parallel",)),
    )(p_rw, p_rw, p_rw, mu_p, mu_n, w0, w2, a0, a2, k_k, k_a)


def _scan_fwd(T, r, v, kk, dec, kd, bb):
    NP = T.BW // 128
    NCH = T.TTOT // SCAN_CH
    NCC = TT // SCAN_CH
    NQ = SCAN_CH // SCAN_SUB

    def tmap(d, i):
        rev = jnp.where(i < NCC, NCC - 1 - i, NCH - 1 - (i - NCC))
        return jnp.where(d == 0, i, rev)

    def kern(r_ref, v_ref, kk_ref, dec_ref, kd_ref, bb_ref, y_ref, ck_ref, s_ref):
        d = pl.program_id(0)
        i = pl.program_id(1)

        @pl.when(i == 0)
        def _():
            s_ref[...] = jnp.zeros_like(s_ref)

        e = _e128(BF16)
        im = _imask64()

        def step(t, carry):
            row = jnp.where(d == 0, t, SCAN_CH - 1 - t)
            a_all = -kk_ref[pl.ds(row, 1), :]
            w_all = dec_ref[0, pl.ds(row, 1), :]
            b_all = bb_ref[0, pl.ds(row, 1), :]
            k_all = kd_ref[0, pl.ds(row, 1), :]
            v_all = v_ref[pl.ds(row, 1), :]
            r_all = r_ref[pl.ds(row, 1), :]
            ys = []
            for j in range(NP):
                sl = slice(128 * j, 128 * (j + 1))
                s = s_ref[j]
                sa = _segb(s * a_all[:, sl], e)
                vc = _colb(v_all[:, sl], im, e)
                s = s * w_all[:, sl] + sa * b_all[:, sl] + vc * k_all[:, sl]
                s_ref[j] = s
                yb = _segb(s * r_all[:, sl], e)
                ys.append(_colsum(jnp.where(im, yb, 0.0)))
            y_ref[0, pl.ds(row, 1), :] = jnp.concatenate(ys, axis=1)
            return carry

        for qd in range(NQ):
            ck_ref[0, qd] = s_ref[...]
            lax.fori_loop(qd * SCAN_SUB, (qd + 1) * SCAN_SUB, step, 0)

    sh = _bs((SCAN_CH, T.BW), lambda d, i: (tmap(d, i), 0))
    dr = _bs((1, SCAN_CH, T.BW), lambda d, i: (d, tmap(d, i), 0))
    return pl.pallas_call(
        kern, grid=(2, NCH), name="scan_fwd",
        in_specs=[sh, sh, sh, dr, dr, dr],
        out_specs=[dr, _bs((1, NQ, NP, HS, 128), lambda d, i: (d, i, 0, 0, 0))],
        out_shape=[jax.ShapeDtypeStruct((2, T.TTOT, T.BW), F32),
                   jax.ShapeDtypeStruct((2, NCH * NQ, NP, HS, 128), F32)],
        scratch_shapes=[pltpu.VMEM((NP, HS, 128), F32)],
        compiler_params=_cp(("arbitrary", "arbitrary")),
    )(r, v, kk, dec, kd, bb)


def _readout_fwd(y, r, v, gd, kbar, rk, gw, lw, lb, e):
    mu = _segsum(y, e) * (1.0 / HS)
    yc = y - mu
    var = _segsum(yc * yc, e) * (1.0 / HS)
    rstd = lax.rsqrt(var + EPS_GN)
    yhat = yc * rstd
    yn = yhat * lw + lb
    q = _segsum(r * kbar * rk, e)
    sg = _sigmoid(gd)
    gg = jnp.dot(sg, gw, precision=HI, preferred_element_type=F32)
    return dict(yhat=yhat, rstd=rstd, yn=yn, q=q, sg=sg, gg=gg, out=(yn + q * v) * gg)


def _conv_fwd(cva, cvb, cw_ref, cb, lw, lb):
    pos = _conv_pos()
    sgb = _sigmoid(cvb)
    u = cva * sgb
    c = jnp.zeros_like(u)
    for j in range(KCONV):
        c = c + cw_ref[j:j + 1, :] * _shifted(u, j - KCONV // 2, pos)
    c = c + cb
    mu = _rowmean(c)
    cc = c - mu
    rstd = lax.rsqrt(_rowmean(cc * cc) + EPS_LN)
    chat = cc * rstd
    cn = chat * lw + lb
    scn = _sigmoid(cn)
    return dict(sgb=sgb, u=u, chat=chat, rstd=rstd, cn=cn, scn=scn, out=cn * scn, pos=pos)


def _mix_out(T, y, kd, rw, p_cv, xc, modrows, rk, gw, lnw, lnb, cw, cb, clw, clb, pg, w_out):
    tk = T.tok

    def kern(y_ref, kd_ref, rw_ref, pcv_ref, x_ref, mod_ref, rk_ref, gw_ref, lw_ref, lb_ref, cw_ref, cb_ref,
             clw_ref, clb_ref, pg_ref, wo_ref, cat_o, mix_o, x1_o):
        e = _e128(F32)
        ro = _readout_fwd(y_ref[0] + y_ref[1], rw_ref[:, 0:512], rw_ref[:, 1024:1536], rw_ref[:, 2048:2304],
                          0.5 * (kd_ref[0] + kd_ref[1]), rk_ref[...], gw_ref[...], lw_ref[...], lb_ref[...], e)
        cv = _conv_fwd(pcv_ref[:, 0:512], pcv_ref[:, 512:1024], cw_ref, cb_ref[...], clw_ref[...], clb_ref[...])
        catb = jnp.concatenate([ro["out"], cv["out"]], axis=1).astype(BF16)
        cat_o[...] = catb
        mix = jnp.dot(catb, wo_ref[...], preferred_element_type=F32)
        mix_o[...] = mix
        sm = lax.rsqrt(_rowmean(mix * mix) + EPS_RMS)
        x1_o[...] = x_ref[...] + mod_ref[0, 0, 2:3, :] * (mix * sm * pg_ref[...])

    lat = lambda l: (l, 0)
    return pl.pallas_call(
        kern, grid=(T.NL,), name="mix_out",
        in_specs=[_bs((2, TT, W), lambda l: (0, 1 + l % T.NLT, l // T.NLT)),
                  _bs((2, TT, W), lambda l: (0, 1 + l % T.NLT, l // T.NLT)),
                  _bs((TT, RWC), lambda l: (tk(l), 0)), _bs((TT, CVC), lambda l: (tk(l), 0)),
                  _bs((TT, D), lambda l: (tk(l), 0)),
                  _bs((1, 1, 6, D), lambda l: (l // T.NLT, 1, 0, 0)),
                  _row(W), _bs((GDW, W), lambda l: (0, 0)), _row(W), _row(W),
                  _bs((32, W), lambda l: (0, 0)), _row(W), _row(W), _row(W), _row(D),
                  _bs((D, D), lambda l: (0, 0))],
        out_specs=[_bs((TT, D), lat), _bs((TT, D), lat), _bs((TT, D), lat)],
        out_shape=[jax.ShapeDtypeStruct((T.NLAT, D), BF16), jax.ShapeDtypeStruct((T.NLAT, D), F32),
                   jax.ShapeDtypeStruct((T.NLAT, D), F32)],
        compiler_params=_cp(("parallel",)),
    )(y, kd, rw, p_cv, xc, modrows, rk, gw, lnw, lnb, cw, cb, clw, clb, pg, w_out)


MT = 512
FC = 1024


def _mlp_fwd(T, x1, modrows, g, w1, w2):
    per_b = T.NLT * TT // MT

    def kern(x_ref, mod_ref, g_ref, w1_ref, w2_ref, m_o, h2_o, h2_s):
        f = pl.program_id(1)

        @pl.when(f == 0)
        def _():
            x = x_ref[...]
            s = lax.rsqrt(_rowmean(x * x) + EPS_RMS)
            h2 = (x * s * g_ref[...]) * (1.0 + mod_ref[0, 0, 4:5, :]) + mod_ref[0, 0, 3:4, :]
            h2_s[...] = h2.astype(BF16)
            h2_o[...] = h2.astype(BF16)
            m_o[...] = jnp.zeros_like(m_o)

        a = jnp.dot(h2_s[...], w1_ref[...], preferred_element_type=F32)
        rl = jnp.maximum(a, 0.0)
        m_o[...] += jnp.dot((rl * rl).astype(BF16), w2_ref[...], preferred_element_type=F32)

    tok = lambda t, f: (t, 0)
    return pl.pallas_call(
        kern, grid=(T.NLAT // MT, DFF // FC), name="mlp_fwd",
        in_specs=[_bs((MT, D), tok), _bs((1, 1, 6, D), lambda t, f: (t // per_b, 1, 0, 0)),
                  _bs((1, D), lambda t, f: (0, 0)), _bs((D, FC), lambda t, f: (0, f)), _bs((FC, D), lambda t, f: (f, 0))],
        out_specs=[_bs((MT, D), tok), _bs((MT, D), tok)],
        out_shape=[jax.ShapeDtypeStruct((T.NLAT, D), F32), jax.ShapeDtypeStruct((T.NLAT, D), BF16)],
        scratch_shapes=[pltpu.VMEM((MT, D), BF16)],
        compiler_params=_cp(("parallel", "arbitrary")),
    )(x1, modrows, g, w1, w2)


def _loss_head(T, m, x1, tgt, modrows, pg):
    def kern(m_ref, x1_ref, t_ref, mod_ref, pg_ref, loss_o, dm_o, dx2_o, dg2_o, dpg_o):
        l = pl.program_id(0)
        m_ = m_ref[...]
        sm = lax.rsqrt(_rowmean(m_ * m_) + EPS_RMS)
        mn = m_ * sm
        g2 = mod_ref[0, 0, 5:6, :]
        pgv = pg_ref[...]
        diff = x1_ref[...] + g2 * (mn * pgv) - t_ref[...]
        sq = jnp.sum(_colsum(diff * diff), axis=1, keepdims=True)
        _acc(loss_o, jnp.zeros((8, 128), F32) + (0.5 / D) * sq, l == 0)
        dx2 = diff * (1.0 / D)
        dx2_o[...] = dx2
        _acc(dg2_o.at[0], _colsum(dx2 * mn * pgv), l % T.NLT == 0)
        _acc(dpg_o, _colsum(dx2 * g2 * mn), l == 0)
        dmn = dx2 * g2 * pgv
        dm_o[...] = (sm * (dmn - mn * _rowmean(dmn * mn))).astype(BF16)

    lat = lambda l: (l, 0)
    return pl.pallas_call(
        kern, grid=(T.NL,), name="loss_head",
        in_specs=[_bs((TT, D), lat), _bs((TT, D), lat), _bs((TT, D), lat),
                  _bs((1, 1, 6, D), lambda l: (l // T.NLT, 1, 0, 0)), _row(D)],
        out_specs=[_bs((8, 128), lambda l: (0, 0)), _bs((TT, D), lat), _bs((TT, D), lat),
                   _bs((1, 1, D), lambda l: (l // T.NLT, 0, 0)), _row(D)],
        out_shape=[jax.ShapeDtypeStruct((8, 128), F32), jax.ShapeDtypeStruct((T.NLAT, D), BF16),
                   jax.ShapeDtypeStruct((T.NLAT, D), F32), jax.ShapeDtypeStruct((T.B, 1, D), F32),
                   jax.ShapeDtypeStruct((1, D), F32)],
        compiler_params=_cp(("arbitrary",)),
    )(m, x1, tgt, modrows, pg)


_NT_DIMS = (((1,), (1,)), ((), ()))
_TN_DIMS = (((0,), (0,)), ((), ()))


def _mlp_bwd(T, h2, dm, w1, w2):
    def kern(h2_ref, dm_ref, w1_ref, w2_ref, f_o, da_o, dh2_o):
        f = pl.program_id(1)
        a = jnp.dot(h2_ref[...], w1_ref[...], preferred_element_type=F32)
        rl = jnp.maximum(a, 0.0)
        f_o[...] = (rl * rl).astype(BF16)
        df = lax.dot_general(dm_ref[...], w2_ref[...], _NT_DIMS, preferred_element_type=F32)
        dab = (df * (2.0 * rl)).astype(BF16)
        da_o[...] = dab
        _acc(dh2_o, lax.dot_general(dab, w1_ref[...], _NT_DIMS, preferred_element_type=F32), f == 0)

    tok = lambda t, f: (t, 0)
    return pl.pallas_call(
        kern, grid=(T.NLAT // MT, DFF // FC), name="mlp_bwd",
        in_specs=[_bs((MT, D), tok), _bs((MT, D), tok), _bs((D, FC), lambda t, f: (0, f)),
                  _bs((FC, D), lambda t, f: (f, 0))],
        out_specs=[_bs((MT, FC), lambda t, f: (t, f)), _bs((MT, FC), lambda t, f: (t, f)), _bs((MT, D), tok)],
        out_shape=[jax.ShapeDtypeStruct((T.NLAT, DFF), BF16), jax.ShapeDtypeStruct((T.NLAT, DFF), BF16),
                   jax.ShapeDtypeStruct((T.NLAT, D), F32)],
        compiler_params=_cp(("parallel", "arbitrary")),
    )(h2, dm, w1, w2)


def _mlp_in_bwd(T, dh2, x1, dx2, modrows, g):
    def kern(dh_ref, x1_ref, dx2_ref, mod_ref, g_ref, dx1_o, dmod_o, dg_o):
        i = pl.program_id(0)
        lat = (i % T.TPS != 0).astype(F32)
        x = x1_ref[...]
        s = lax.rsqrt(_rowmean(x * x) + EPS_RMS)
        xh = x * s
        gv = g_ref[...]
        dh = dh_ref[...] * lat
        n2 = xh * gv
        first_b = i % T.TPS == 0
        _acc(dmod_o.at[0, 0:1, :], _colsum(dh), first_b)
        _acc(dmod_o.at[0, 1:2, :], _colsum(dh * n2), first_b)
        dn2 = dh * (1.0 + mod_ref[0, 0, 4:5, :])
        _acc(dg_o, _colsum(dn2 * xh), i == 0)
        dxh = dn2 * gv
        dx1_o[...] = (dx2_ref[...] + s * (dxh - xh * _rowmean(dxh * xh))) * lat

    lat_i = lambda i: (T.lat(i), 0)
    return pl.pallas_call(
        kern, grid=(T.NT,), name="mlp_in_bwd",
        in_specs=[_bs((TT, D), lat_i), _bs((TT, D), lat_i), _bs((TT, D), lat_i),
                  _bs((1, 1, 6, D), lambda i: (i // T.TPS, 1, 0, 0)), _row(D)],
        out_specs=[_bs((TT, D), lambda i: (i, 0)), _bs((1, 2, D), lambda i: (i // T.TPS, 0, 0)), _row(D)],
        out_shape=[jax.ShapeDtypeStruct((T.NTOK, D), F32), jax.ShapeDtypeStruct((T.B, 2, D), F32),
                   jax.ShapeDtypeStruct((1, D), F32)],
        compiler_params=_cp(("arbitrary",)),
    )(dh2, x1, dx2, modrows, g)


def _mix_post_bwd(T, dx1, mix, modrows, pg, w_out):
    def kern(dx_ref, mix_ref, mod_ref, pg_ref, wo_ref, dmix_o, dcat_o, dg1_o, dpg_o):
        i = pl.program_id(0)
        lat = (i % T.TPS != 0).astype(F32)
        dx = dx_ref[...]
        mix = mix_ref[...]
        sm = lax.rsqrt(_rowmean(mix * mix) + EPS_RMS)
        mh = mix * sm
        g1 = mod_ref[0, 0, 2:3, :]
        pgv = pg_ref[...]
        _acc(dg1_o.at[0], _colsum(dx * mh * pgv), i % T.TPS == 0)
        _acc(dpg_o, _colsum(dx * g1 * mh), i == 0)
        dmh = dx * g1 * pgv
        dmix = ((sm * (dmh - mh * _rowmean(dmh * mh))) * lat).astype(BF16)
        dmix_o[...] = dmix
        dcat_o[...] = lax.dot_general(dmix, wo_ref[...], _NT_DIMS, preferred_element_type=F32)

    tok = lambda i: (i, 0)
    return pl.pallas_call(
        kern, grid=(T.NT,), name="mix_post_bwd",
        in_specs=[_bs((TT, D), tok), _bs((TT, D), lambda i: (T.lat(i), 0)),
                  _bs((1, 1, 6, D), lambda i: (i // T.TPS, 1, 0, 0)), _row(D), _bs((D, D), lambda i: (0, 0))],
        out_specs=[_bs((TT, D), tok), _bs((TT, D), tok), _bs((1, 1, D), lambda i: (i // T.TPS, 0, 0)), _row(D)],
        out_shape=[jax.ShapeDtypeStruct((T.NTOK, D), BF16), jax.ShapeDtypeStruct((T.NTOK, D), F32),
                   jax.ShapeDtypeStruct((T.B, 1, D), F32), jax.ShapeDtypeStruct((1, D), F32)],
        compiler_params=_cp(("arbitrary",)),
    )(dx1, mix, modrows, pg, w_out)


def _conv_bwd(T, dcat, p_cv, cw, cb, clw, clb):
    def kern(dc_ref, pcv_ref, cw_ref, cb_ref, clw_ref, clb_ref, dp_o, dcw_o, dcb_o, dlw_o, dlb_o):
        i = pl.program_id(0)
        first = i == 0
        cva = pcv_ref[:, 0:512]
        cv = _conv_fwd(cva, pcv_ref[:, 512:1024], cw_ref, cb_ref[...], clw_ref[...], clb_ref[...])
        dout = dc_ref[...]
        scn = cv["scn"]
        dcn = dout * (scn * (1.0 + cv["cn"] * (1.0 - scn)))
        chat = cv["chat"]
        _acc(dlw_o, _colsum(dcn * chat), first)
        _acc(dlb_o, _colsum(dcn), first)
        dchat = dcn * clw_ref[...]
        dc = cv["rstd"] * (dchat - _rowmean(dchat) - chat * _rowmean(dchat * chat))
        _acc(dcb_o, _colsum(dc), first)
        pos = cv["pos"]
        u = cv["u"]
        du = jnp.zeros_like(u)
        for j in range(KCONV):
            s = j - KCONV // 2
            _acc(dcw_o.at[j:j + 1, :], _colsum(dc * _shifted(u, s, pos)), first)
            du = du + cw_ref[j:j + 1, :] * _shifted(dc, -s, pos)
        _acc(dcw_o.at[KCONV:KCONV + 1, :], jnp.zeros((1, W), F32), first)
        sgb = cv["sgb"]
        dp_o[...] = jnp.concatenate([du * sgb, du * cva * sgb * (1.0 - sgb)], axis=1).astype(BF16)

    return pl.pallas_call(
        kern, grid=(T.NT,), name="conv_bwd",
        in_specs=[_bs((TT, W), lambda i: (i, 1)), _bs((TT, CVC), lambda i: (i, 0)),
                  _bs((32, W), lambda i: (0, 0)), _row(W), _row(W), _row(W)],
        out_specs=[_bs((TT, CVC), lambda i: (i, 0)), _bs((32, W), lambda i: (0, 0)), _row(W), _row(W), _row(W)],
        out_shape=[jax.ShapeDtypeStruct((T.NTOK, CVC), BF16), jax.ShapeDtypeStruct((32, W), F32),
                   jax.ShapeDtypeStruct((1, W), F32), jax.ShapeDtypeStruct((1, W), F32),
                   jax.ShapeDtypeStruct((1, W), F32)],
        compiler_params=_cp(("arbitrary",)),
    )(dcat, p_cv, cw, cb, clw, clb)


def _readout_bwd(T, dcat, y, kd, rw, rk, gw, lnw, lnb):
    def kern(dc_ref, y_ref, kd_ref, rw_ref, rk_ref, gw_ref, lw_ref, lb_ref,
             dy_o, dr_o, dv_o, dkb_o, dgd_o, drk_o, dgw_o, dlw_o, dlb_o):
        i = pl.program_id(0)
        first = i == 0
        e = _e128(F32)
        r = rw_ref[:, 0:512]
        v = rw_ref[:, 1024:1536]
        kbar = 0.5 * (kd_ref[0] + kd_ref[1])
        rk = rk_ref[...]
        ro = _readout_fwd(y_ref[0] + y_ref[1], r, v, rw_ref[:, 2048:2304], kbar, rk, gw_ref[...],
                          lw_ref[...], lb_ref[...], e)
        dout = dc_ref[...]
        dgg = dout * (ro["yn"] + ro["q"] * v)
        t1 = dout * ro["gg"]
        yhat = ro["yhat"]
        _acc(dlw_o, _colsum(t1 * yhat), first)
        _acc(dlb_o, _colsum(t1), first)
        dyh = t1 * lw_ref[...]
        dy_o[...] = ro["rstd"] * (dyh - _segsum(dyh, e) * (1.0 / HS) - yhat * (_segsum(dyh * yhat, e) * (1.0 / HS)))
        dq = _segsum(t1 * v, e)
        dv_o[...] = t1 * ro["q"]
        dr_o[...] = dq * kbar * rk
        dkb_o[...] = dq * r * rk
        _acc(drk_o, _colsum(dq * r * kbar), first)
        sg = ro["sg"]
        dsg = lax.dot_general(dgg, gw_ref[...], _NT_DIMS, precision=HI, preferred_element_type=F32)
        dgd_o[...] = dsg * sg * (1.0 - sg)
        _acc(dgw_o, lax.dot_general(sg, dgg, _TN_DIMS, precision=HI, preferred_element_type=F32), first)

    tok = lambda i: (i, 0)
    f32s = lambda *s: jax.ShapeDtypeStruct(s, F32)
    return pl.pallas_call(
        kern, grid=(T.NT,), name="readout_bwd",
        in_specs=[_bs((TT, W), tok), T.tm2_spec(), T.tm2_spec(), _bs((TT, RWC), tok),
                  _row(W), _bs((GDW, W), lambda i: (0, 0)), _row(W), _row(W)],
        out_specs=[T.tm_spec(), _bs((TT, W), tok), _bs((TT, W), tok), _bs((TT, W), tok), _bs((TT, GDW), tok),
                   _row(W), _bs((GDW, W), lambda i: (0, 0)), _row(W), _row(W)],
        out_shape=[f32s(T.TTOT, T.BW), f32s(T.NTOK, W), f32s(T.NTOK, W), f32s(T.NTOK, W), f32s(T.NTOK, GDW),
                   f32s(1, W), f32s(GDW, W), f32s(1, W), f32s(1, W)],
        compiler_params=_cp(("arbitrary",)),
    )(dcat, y, kd, rw, rk, gw, lnw, lnb)


def _scan_bwd(T, dy, r, v, kk, dec, kd, bb, ck):
    NP = T.BW // 128
    NS = T.TTOT // SCAN_SUB
    NSC = TT // SCAN_SUB

    def tmap(d, g):
        s = NS - 1 - g
        rev = jnp.where(s < NSC, NSC - 1 - s, NS - 1 - (s - NSC))
        return jnp.where(d == 0, s, rev)

    def kern(dy_ref, r_ref, v_ref, kk_ref, dec_ref, kd_ref, bb_ref, ck_ref,
             dr_o, dw_o, dk_o, dv_o, da_o, db_o, hist, ds_ref):
        d = pl.program_id(0)
        g = pl.program_id(1)

        @pl.when(g == 0)
        def _():
            ds_ref[...] = jnp.zeros_like(ds_ref)

        e = _e128(BF16)
        im = _imask64()

        def rows(t):
            row = jnp.where(d == 0, t, SCAN_SUB - 1 - t)
            return (row, -kk_ref[pl.ds(row, 1), :], dec_ref[0, pl.ds(row, 1), :], bb_ref[0, pl.ds(row, 1), :],
                    kd_ref[0, pl.ds(row, 1), :], v_ref[pl.ds(row, 1), :])

        def fwd(t, carry):
            _, a_all, w_all, b_all, k_all, v_all = rows(t)
            for j in range(NP):
                sl = slice(128 * j, 128 * (j + 1))
                s = hist[t, j]
                sa = _segb(s * a_all[:, sl], e)
                vc = _colb(v_all[:, sl], im, e)
                hist[t + 1, j] = s * w_all[:, sl] + sa * b_all[:, sl] + vc * k_all[:, sl]
            return carry

        hist[0] = ck_ref[0, 0]
        lax.fori_loop(0, SCAN_SUB, fwd, 0)

        def bwd(tt, carry):
            t = SCAN_SUB - 1 - tt
            row, a_all, w_all, b_all, k_all, v_all = rows(t)
            r_all = r_ref[pl.ds(row, 1), :]
            dy_all = dy_ref[pl.ds(row, 1), :]
            outs = [[] for _ in range(6)]
            for j in range(NP):
                sl = slice(128 * j, 128 * (j + 1))
                sp = hist[t, j]
                st = hist[t + 1, j]
                a_, w_, b_, k_ = a_all[:, sl], w_all[:, sl], b_all[:, sl], k_all[:, sl]
                sa = _segb(sp * a_, e)
                vc = _colb(v_all[:, sl], im, e)
                dyc = _colb(dy_all[:, sl], im, e)
                ds = ds_ref[j] + dyc * r_all[:, sl]
                outs[0].append(_colsum(st * dyc))
                outs[1].append(_colsum(ds * sp))
                dsa = _segb(ds * b_, e)
                outs[5].append(_colsum(ds * sa))
                dvb = _segb(ds * k_, e)
                outs[3].append(_colsum(jnp.where(im, dvb, 0.0)))
                outs[2].append(_colsum(ds * vc))
                outs[4].append(_colsum(sp * dsa))
                ds_ref[j] = ds * w_ + dsa * a_
            for o_ref, lst in zip((dr_o, dw_o, dk_o, dv_o, da_o, db_o), outs):
                o_ref[0, pl.ds(row, 1), :] = jnp.concatenate(lst, axis=1)
            return carry

        lax.fori_loop(0, SCAN_SUB, bwd, 0)

    sh = _bs((SCAN_SUB, T.BW), lambda d, g: (tmap(d, g), 0))
    dr = _bs((1, SCAN_SUB, T.BW), lambda d, g: (d, tmap(d, g), 0))
    o2 = jax.ShapeDtypeStruct((2, T.TTOT, T.BW), F32)
    return pl.pallas_call(
        kern, grid=(2, NS), name="scan_bwd",
        in_specs=[sh, sh, sh, sh, dr, dr, dr, _bs((1, 1, NP, HS, 128), lambda d, g: (d, NS - 1 - g, 0, 0, 0))],
        out_specs=[dr] * 6,
        out_shape=[o2] * 6,
        scratch_shapes=[pltpu.VMEM((SCAN_SUB + 1, NP, HS, 128), F32), pltpu.VMEM((NP, HS, 128), F32)],
        compiler_params=_cp(("arbitrary", "arbitrary"), mb=56),
    )(dy, r, v, kk, dec, kd, bb, ck)


def _prep_bwd(T, rw, dr_s, ddec, dkd, dv_s, da_s, dbb, dr_ro, dv_ro, dkbar, dgd, w0, w2, a0, a2, k_k, k_a):
    def kern(rw_ref, drs_ref, ddec_ref, dkd_ref, dvs_ref, das_ref, dbb_ref, drr_ref, dvr_ref, dkb_ref, dgd_ref,
             w0_ref, w2_ref, a0_ref, a2_ref, kk_ref, ka_ref,
             drw_o, dw0_o, dw2_o, da0_o, da2_o, dkk_o, dka_o):
        i = pl.program_id(0)
        first = i == 0
        e = _e128(F32)
        w0v, w2v, a0v, a2v = _load_prep_params(w0_ref, w2_ref, a0_ref, a2_ref)
        k_k = kk_ref[...]
        k_a = ka_ref[...]
        o = _prep_math(rw_ref[...], w0v, w2v, a0v, a2v, k_k, k_a, e)
        k, kk = o["k"], o["kk"]
        dkbh = 0.5 * dkb_ref[...]
        dk = jnp.zeros_like(k)
        dkk = -(das_ref[0] + das_ref[1])
        dka = jnp.zeros((1, W), F32)
        dwd, dad = [], []
        for d in (0, 1):
            iclr = o["iclr"][d]
            dkd_d = dkd_ref[d] + dkbh
            dbb_d = dbb_ref[d]
            dk = dk + dkd_d * (1.0 + (iclr - 1.0) * k_a)
            dka = dka + _colsum(dkd_d * k * (iclr - 1.0))
            dkk = dkk + dbb_d * iclr
            dicl = dkd_d * k * k_a + dbb_d * kk
            dpa = dicl * iclr * (1.0 - iclr)
            _acc(da0_o.at[d:d + 1, :], _colsum(dpa), first)
            dad.append(lax.dot_general(dpa, a2v[d], _NT_DIMS, precision=HI, preferred_element_type=F32))
            _acc(da2_o.at[d], lax.dot_general(o["ad"][d], dpa, _TN_DIMS, precision=HI, preferred_element_type=F32),
                 first)
            dpre = -ddec_ref[d] * o["dec"][d] * o["ex"][d] * _sigmoid(-o["pre"][d])
            _acc(dw0_o.at[d:d + 1, :], _colsum(dpre), first)
            th = o["th"][d]
            dth = lax.dot_general(dpre, w2v[d], _NT_DIMS, precision=HI, preferred_element_type=F32)
            _acc(dw2_o.at[d], lax.dot_general(th, dpre, _TN_DIMS, precision=HI, preferred_element_type=F32), first)
            dwd.append(dth * (1.0 - th * th))
        inv = o["inv"]
        kr = o["kr"]
        proj = _segsum(dkk * kr, e)
        dkr = dkk * inv - jnp.where(o["rt"] > 1e-12, kr * inv * inv * inv * proj, 0.0)
        dk = dk + dkr * k_k
        _acc(dkk_o, _colsum(dkr * k), first)
        _acc(dka_o, dka, first)
        dr = drs_ref[0] + drs_ref[1] + drr_ref[...]
        dv = dvs_ref[0] + dvs_ref[1] + dvr_ref[...]
        drw_o[...] = jnp.concatenate([dr, dk, dv, dwd[0], dwd[1], dad[0], dad[1], dgd_ref[...]], axis=1)

    tok = lambda i: (i, 0)
    f32s = lambda *s: jax.ShapeDtypeStruct(s, F32)
    p2 = lambda i: (0, 0)
    p3 = lambda i: (0, 0, 0)
    return pl.pallas_call(
        kern, grid=(T.NT,), name="prep_bwd",
        in_specs=[_bs((TT, RWC), tok)] + [T.tm2_spec()] * 6 + [_bs((TT, W), tok)] * 3 + [_bs((TT, GDW), tok)]
        + _prep_param_specs(),
        out_specs=[_bs((TT, RWC), tok), _bs((2, W), p2), _bs((2, LRW, W), p3), _bs((2, W), p2),
                   _bs((2, LRW, W), p3), _row(W), _row(W)],
        out_shape=[f32s(T.NTOK, RWC), f32s(2, W), f32s(2, LRW, W), f32s(2, W), f32s(2, LRW, W), f32s(1, W), f32s(1, W)],
        compiler_params=_cp(("arbitrary",), mb=56),
    )(rw, dr_s, ddec, dkd, dv_s, da_s, dbb, dr_ro, dv_ro, dkbar, dgd, w0, w2, a0, a2, k_k, k_a)


def _shift_bwd(T, drw, p_rw, mu_p, mu_n):
    def kern(d_ref, dp_ref, dn_ref, p_ref, pp_ref, pn_ref, mp_ref, mn_ref, dprw_o, dmp_o, dmn_o):
        i = pl.program_id(0)
        first = i == 0
        has_prev, has_next = _halo_masks(T, i)
        mp = mp_ref[...]
        mn = mn_ref[...]
        drw = d_ref[...]
        z = p_ref[...]
        zprev, znext = _neighbours(z, pp_ref[7:8, :] * has_prev, pn_ref[0:1, :] * has_next)
        _acc(dmp_o, _colsum(drw * (zprev - z)), first)
        _acc(dmn_o, _colsum(drw * (znext - z)), first)
        dprev, dnext = _neighbours(drw, dp_ref[7:8, :] * has_prev, dn_ref[0:1, :] * has_next)
        dprw_o[...] = (drw * (1.0 - mp - mn) + mp * dnext + mn * dprev).astype(BF16)

    tok = lambda i: (i, 0)
    prev, nxt = _halo_specs(T)
    f32s = lambda *s: jax.ShapeDtypeStruct(s, F32)
    return pl.pallas_call(
        kern, grid=(T.NT,), name="shift_bwd",
        in_specs=[_bs((TT, RWC), tok), prev, nxt, _bs((TT, RWC), tok), prev, nxt, _row(RWC), _row(RWC)],
        out_specs=[_bs((TT, RWC), tok), _row(RWC), _row(RWC)],
        out_shape=[jax.ShapeDtypeStruct((T.NTOK, RWC), BF16), f32s(1, RWC), f32s(1, RWC)],
        compiler_params=_cp(("arbitrary",), mb=56),
    )(drw, drw, drw, p_rw, p_rw, p_rw, mu_p, mu_n)


def _mix_in_bwd(T, dp_rw, dp_cv, xc, dx1, modrows, g, w_rw, w_cv):
    def kern(drw_ref, dcv_ref, x_ref, dx1_ref, mod_ref, g_ref, wr_ref, wc_ref, dxc_o, dmod_o, dg_o):
        i = pl.program_id(0)
        dh = (lax.dot_general(drw_ref[...], wr_ref[...], _NT_DIMS, preferred_element_type=F32)
              + lax.dot_general(dcv_ref[...], wc_ref[...], _NT_DIMS, preferred_element_type=F32))
        x = x_ref[...]
        s = lax.rsqrt(_rowmean(x * x) + EPS_RMS)
        xh = x * s
        gv = g_ref[...]
        q = i % T.TPS
        first_kind = jnp.logical_or(q == 0, q == 1)
        _acc(dmod_o.at[0, 0, 0:1, :], _colsum(dh), first_kind)
        _acc(dmod_o.at[0, 0, 1:2, :], _colsum(dh * (xh * gv)), first_kind)
        dn1 = dh * (1.0 + mod_ref[0, 0, 1:2, :])
        _acc(dg_o, _colsum(dn1 * xh), i == 0)
        dxh = dn1 * gv
        dxc_o[...] = dx1_ref[...] + s * (dxh - xh * _rowmean(dxh * xh))

    tok = lambda i: (i, 0)
    f32s = lambda *s: jax.ShapeDtypeStruct(s, F32)
    return pl.pallas_call(
        kern, grid=(T.NT,), name="mix_in_bwd",
        in_specs=[_bs((TT, RWC), tok), _bs((TT, CVC), tok), _bs((TT, D), tok), _bs((TT, D), tok), T.mod_spec(),
                  _row(D), _bs((D, RWC), lambda i: (0, 0)), _bs((D, CVC), lambda i: (0, 0))],
        out_specs=[_bs((TT, D), tok),
                   _bs((1, 1, 2, D), lambda i: (i // T.TPS, jnp.minimum(i % T.TPS, 1), 0, 0)), _row(D)],
        out_shape=[f32s(T.NTOK, D), f32s(T.B, 2, 2, D), f32s(1, D)],
        compiler_params=_cp(("arbitrary",)),
    )(dp_rw, dp_cv, xc, dx1, modrows, g, w_rw, w_cv)


def _matmul_tn(a, b, name, tk, nk, tn, amap=None, bmap=None, tm=1024):
    M = a.shape[1]
    N = b.shape[1]
    amap = amap or (lambda k: k)
    bmap = bmap or (lambda k: k)

    def kern(a_ref, b_ref, o_ref):
        _acc(o_ref, lax.dot_general(a_ref[...], b_ref[...], _TN_DIMS, preferred_element_type=F32),
             pl.program_id(2) == 0)

    return pl.pallas_call(
        kern, grid=(M // tm, N // tn, nk), name=name,
        in_specs=[_bs((tk, tm), lambda i, j, k: (amap(k), i)), _bs((tk, tn), lambda i, j, k: (bmap(k), j))],
        out_specs=_bs((tm, tn), lambda i, j, k: (i, j)),
        out_shape=jax.ShapeDtypeStruct((M, N), F32),
        compiler_params=_cp(("parallel", "parallel", "arbitrary")),
    )(a, b)


def _silu(x):
    return x * _sigmoid(x)


def _ada_fwd(c_all, c_ctx, ada_w, ada_b_blk):
    nb = c_all.shape[0]
    R = nb + 8
    ncol = ada_w.shape[1]

    def kern(c_ref, cc_ref, w_ref, b_ref, o_ref):
        lhs = jnp.concatenate([_silu(c_ref[...]), _silu(cc_ref[...]), jnp.zeros((7, D), F32)], axis=0)
        o_ref[...] = jnp.dot(lhs, w_ref[...], precision=HI, preferred_element_type=F32) + b_ref[...]

    return pl.pallas_call(
        kern, name="ada_fwd", out_shape=jax.ShapeDtypeStruct((R, ncol), F32),
        compiler_params=_cp(None, 40),
    )(c_all, c_ctx, ada_w, ada_b_blk)


def _ada_bwd(c_all, c_ctx, ada_w, ex, cx, ex_blk, cx_blk):
    nb = c_all.shape[0]
    ncol = ada_w.shape[1]

    def kern(c_ref, cc_ref, w_ref, ex_ref, cx_ref, exb_ref, cxb_ref, gw_o, gb_o, ds_o):
        lhs = jnp.concatenate([_silu(c_ref[...]), _silu(cc_ref[...]), jnp.zeros((7, D), F32)], axis=0)
        dmc_blk = _colsum(cxb_ref[...])
        rhs = jnp.concatenate([exb_ref[...], dmc_blk, jnp.zeros((7, ncol), F32)], axis=0)
        gw_o[...] = lax.dot_general(lhs, rhs, _TN_DIMS, precision=HI, preferred_element_type=F32)
        gb_o[...] = _colsum(ex_ref[...]) + _colsum(cx_ref[...])
        ds_o[...] = lax.dot_general(jnp.concatenate([dmc_blk, jnp.zeros((7, ncol), F32)], axis=0), w_ref[...],
                                    _NT_DIMS, precision=HI, preferred_element_type=F32)

    return pl.pallas_call(
        kern, name="ada_bwd",
        out_shape=[jax.ShapeDtypeStruct((D, ncol), F32), jax.ShapeDtypeStruct((1, ex.shape[1]), F32),
                   jax.ShapeDtypeStruct((8, D), F32)],
        compiler_params=_cp(None, 48),
    )(c_all, c_ctx, ada_w, ex, cx, ex_blk, cx_blk)


def _cctx_final(parts, c_ctx):
    def kern(p_ref, c_ref, o_ref):
        tot = p_ref[0, 0:1, :]
        for j in range(1, parts.shape[0]):
            tot = tot + p_ref[j, 0:1, :]
        c = c_ref[...]
        sg = _sigmoid(c)
        o_ref[...] = tot * (sg * (1.0 + c * (1.0 - sg)))

    return pl.pallas_call(kern, name="cctx_final", out_shape=jax.ShapeDtypeStruct((1, D), F32))(parts, c_ctx)


def _peer(kind, p, ix, iy, ic):
    if kind == "chips":
        return (p // 2, p % 2, ic)
    if kind == "all":
        return (p // 4, (p // 2) % 2, p % 2)
    return (ix, iy, p)


def _exchange(x, kind, bcast, name):
    npeer = {"chips": 4, "all": 8, "sib": 2}[kind]
    slab = x.shape if bcast else x.shape[1:]

    def kern(x_ref, o_ref, send_sems, recv_sems, lsem):
        ix, iy, ic = lax.axis_index("x"), lax.axis_index("y"), lax.axis_index("c")
        me = {"chips": 2 * ix + iy, "all": 4 * ix + 2 * iy + ic, "sib": ic}[kind]
        own = pltpu.make_async_copy(x_ref if bcast else x_ref.at[me], o_ref.at[me], lsem)
        own.start()

        def copy(p):
            return pltpu.make_async_remote_copy(
                src_ref=x_ref if bcast else x_ref.at[p], dst_ref=o_ref.at[me],
                send_sem=send_sems.at[p], recv_sem=recv_sems.at[me],
                device_id=_peer(kind, p, ix, iy, ic), device_id_type=MESH)

        def arrival(p):
            return pltpu.make_async_remote_copy(
                src_ref=x_ref if bcast else x_ref.at[p], dst_ref=o_ref.at[p],
                send_sem=send_sems.at[p], recv_sem=recv_sems.at[p],
                device_id=_peer(kind, p, ix, iy, ic), device_id_type=MESH)

        for p in range(npeer):
            @pl.when(me != p)
            def _():
                copy(p).start()
        for p in range(npeer):
            @pl.when(me != p)
            def _():
                arrival(p).wait_recv()
        for p in range(npeer):
            @pl.when(me != p)
            def _():
                copy(p).wait_send()
        own.wait()

    any_spec = pl.BlockSpec(memory_space=pl.ANY)
    return pl.pallas_call(
        kern, name=name, in_specs=[any_spec], out_specs=any_spec,
        out_shape=jax.ShapeDtypeStruct((npeer,) + tuple(slab), x.dtype),
        scratch_shapes=[pltpu.SemaphoreType.DMA((npeer,)), pltpu.SemaphoreType.DMA((npeer,)),
                        pltpu.SemaphoreType.DMA],
    )(x)


def _sum_slots(x, name):
    n, R, C = x.shape
    budget = (8 << 20) // (n * C * x.dtype.itemsize)
    tr = max([t for t in range(8, R + 1, 8) if R % t == 0 and t <= max(budget, 8)], default=R)

    def kern(x_ref, o_ref):
        tot = x_ref[0]
        for s in range(1, n):
            tot = tot + x_ref[s]
        o_ref[...] = tot

    return pl.pallas_call(
        kern, grid=(R // tr,), name=name,
        in_specs=[_bs((n, tr, C), lambda i: (0, i, 0))], out_specs=_bs((tr, C), lambda i: (i, 0)),
        out_shape=jax.ShapeDtypeStruct((R, C), x.dtype), compiler_params=_cp(("parallel",)),
    )(x)


def _adamw(w, g, m, v, name):
    shape = w.shape
    last = shape[-1]
    rows = 1
    for s in shape[:-1]:
        rows *= s
    w2, g2, m2, v2 = (t.reshape(rows, last) for t in (w, g, m, v))
    tr = rows
    for cand in (256, 128, 64, 32, 16, 8):
        if rows > cand and rows % cand == 0 and rows * last > (1 << 18):
            tr = cand
            break
    c1 = 1.0 - ADAM_B1 ** ADAM_STEP
    c2 = 1.0 - ADAM_B2 ** ADAM_STEP

    def kern(w_ref, g_ref, m_ref, v_ref, d_o, m_o, v_o):
        gv = g_ref[...]
        mn = ADAM_B1 * m_ref[...] + (1.0 - ADAM_B1) * gv
        vn = ADAM_B2 * v_ref[...] + (1.0 - ADAM_B2) * (gv * gv)
        m_o[...] = mn
        v_o[...] = vn
        d_o[...] = -ADAM_LR * ((mn / c1) / (jnp.sqrt(vn / c2) + ADAM_EPS) + ADAM_WD * w_ref[...])

    spec = _bs((tr, last), lambda i: (i, 0))
    o = jax.ShapeDtypeStruct((rows, last), F32)
    d, mn, vn = pl.pallas_call(
        kern, grid=(rows // tr,), name=name, in_specs=[spec] * 4, out_specs=[spec] * 3, out_shape=[o, o, o],
        compiler_params=_cp(("parallel",)),
    )(w2, g2, m2, v2)
    return d.reshape(shape), mn.reshape(shape), vn.reshape(shape)


_WEIGHT_NAMES = ("c_ctx", "ada_w", "ada_b", "mix_pre_g", "mix_post_g", "mlp_pre_g", "mlp_post_g", "w_in", "mu_prev",
                 "mu_next", "decay_w0", "decay_w2", "iclr_a0", "iclr_a2", "k_k", "k_a", "r_k", "gate_w2", "lnx_w",
                 "lnx_b", "conv_w", "conv_b", "conv_ln_w", "conv_ln_b", "w_out", "mlp_w1", "mlp_w2")


def _pack_rows(parts, cols=512):
    flat = jnp.concatenate([p.reshape(-1) for p in parts])
    rows = -(-flat.shape[0] // cols)
    rows = -(-rows // 16) * 16
    flat = jnp.pad(flat, (0, rows * cols - flat.shape[0]))
    return flat.reshape(rows, cols)


def _unpack(flat, shapes):
    out = []
    off = 0
    for s in shapes:
        n = 1
        for d in s:
            n *= d
        out.append(flat[off:off + n].reshape(s))
        off += n
    return out


def _local_step(T, xc, tgt, modrows, P):
    p_rw, p_cv, h = _mix_in(T, xc, modrows, P["mix_pre_g"], P["w_rw"], P["w_cv"])
    prep_params = (P["w0"], P["w2"], P["a0"], P["a2"], P["k_k"], P["k_a"])
    r, v, kk, dec, kd, bb, rw = _rwkv_prep(T, p_rw, P["mu_p"], P["mu_n"], *prep_params)
    y, ck = _scan_fwd(T, r, v, kk, dec, kd, bb)
    ro_params = (P["r_k"], P["gate_w2"], P["lnx_w"], P["lnx_b"])
    cv_params = (P["conv_w"], P["conv_b"], P["conv_ln_w"], P["conv_ln_b"])
    cat, mix, x1 = _mix_out(T, y, kd, rw, p_cv, xc, modrows, *ro_params, *cv_params, P["mix_post_g"], P["w_out"])
    m, h2 = _mlp_fwd(T, x1, modrows, P["mlp_pre_g"], P["w1"], P["w2m"])
    loss_acc, dm, dx2, dg2, d_mlp_post = _loss_head(T, m, x1, tgt, modrows, P["mlp_post_g"])
    fact, da, dh2 = _mlp_bwd(T, h2, dm, P["w1"], P["w2m"])
    dx1, dmod2, d_mlp_pre = _mlp_in_bwd(T, dh2, x1, dx2, modrows, P["mlp_pre_g"])
    dmix, dcat, dg1, d_mix_post = _mix_post_bwd(T, dx1, mix, modrows, P["mix_post_g"], P["w_out"])
    dp_cv, d_conv_w, d_conv_b, d_cln_w, d_cln_b = _conv_bwd(T, dcat, p_cv, *cv_params)
    dy, dr_ro, dv_ro, dkbar, dgd, d_r_k, d_gate, d_lnx_w, d_lnx_b = _readout_bwd(T, dcat, y, kd, rw, *ro_params)
    dr_s, ddec, dkd, dv_s, da_s, dbb = _scan_bwd(T, dy, r, v, kk, dec, kd, bb, ck)
    drw, d_w0, d_w2, d_a0, d_a2, d_k_k, d_k_a = _prep_bwd(T, rw, dr_s, ddec, dkd, dv_s, da_s, dbb, dr_ro, dv_ro,
                                                          dkbar, dgd, *prep_params)
    dp_rw, d_mu_p, d_mu_n = _shift_bwd(T, drw, p_rw, P["mu_p"], P["mu_n"])
    dxc, dmod1, d_mix_pre = _mix_in_bwd(T, dp_rw, dp_cv, xc, dx1, modrows, P["mix_pre_g"], P["w_rw"], P["w_cv"])
    nk = T.NTOK // TT
    dw_rw = _matmul_tn(h, dp_rw, "dw_in_rw", TT, nk, 768)
    dw_cv = _matmul_tn(h, dp_cv, "dw_in_cv", TT, nk, 1024)
    dw_out = _matmul_tn(cat, dmix, "dw_out", TT, T.NL, 1024, bmap=T.tok)
    dw1 = _matmul_tn(h2, da, "dw_mlp1", 512, T.NLAT // 512, 1024)
    dw2m = _matmul_tn(fact, dm, "dw_mlp2", 512, T.NLAT // 512, 1024)
    small = dict(mix_pre_g=d_mix_pre, mix_post_g=d_mix_post, mlp_pre_g=d_mlp_pre, mlp_post_g=d_mlp_post,
                 mu_p=d_mu_p, mu_n=d_mu_n, w0=d_w0, w2=d_w2, a0=d_a0, a2=d_a2, k_k=d_k_k, k_a=d_k_a, r_k=d_r_k,
                 gate_w2=d_gate, lnx_w=d_lnx_w, lnx_b=d_lnx_b, conv_w=d_conv_w, conv_b=d_conv_b,
                 conv_ln_w=d_cln_w, conv_ln_b=d_cln_b)
    big = dict(w_rw=dw_rw, w_cv=dw_cv, w_out=dw_out, w1=dw1, w2m=dw2m)
    dmods = dict(dmod1=dmod1, dg1=dg1, dmod2=dmod2, dg2=dg2)
    return loss_acc[0, 0], dxc, small, big, dmods


_SMALL_ORDER = ("mix_pre_g", "mix_post_g", "mlp_pre_g", "mlp_post_g", "mu_p", "mu_n", "w0", "w2", "a0", "a2", "k_k",
                "k_a", "r_k", "gate_w2", "lnx_w", "lnx_b", "conv_w", "conv_b", "conv_ln_w", "conv_ln_b")


def kernel(x, c, ctx, c_ctx, ada_w, ada_b, mix_pre_g, mix_post_g, mlp_pre_g, mlp_post_g, w_in, mu_prev, mu_next, decay_w0, decay_w2, iclr_a0, iclr_a2, k_k, k_a, r_k, gate_w2, lnx_w, lnx_b, conv_w, conv_b, conv_ln_w, conv_ln_b, w_out, mlp_w1, mlp_w2, loss_target, m_c_ctx, m_ada_w, m_ada_b, m_mix_pre_g, m_mix_post_g, m_mlp_pre_g, m_mlp_post_g, m_w_in, m_mu_prev, m_mu_next, m_decay_w0, m_decay_w2, m_iclr_a0, m_iclr_a2, m_k_k, m_k_a, m_r_k, m_gate_w2, m_lnx_w, m_lnx_b, m_conv_w, m_conv_b, m_conv_ln_w, m_conv_ln_b, m_w_out, m_mlp_w1, m_mlp_w2, v_c_ctx, v_ada_w, v_ada_b, v_mix_pre_g, v_mix_post_g, v_mlp_pre_g, v_mlp_post_g, v_w_in, v_mu_prev, v_mu_next, v_decay_w0, v_decay_w2, v_iclr_a0, v_iclr_a2, v_k_k, v_k_a, v_r_k, v_gate_w2, v_lnx_w, v_lnx_b, v_conv_w, v_conv_b, v_conv_ln_w, v_conv_ln_b, v_w_out, v_mlp_w1, v_mlp_w2):
    weights = dict(zip(_WEIGHT_NAMES, (c_ctx, ada_w, ada_b, mix_pre_g, mix_post_g, mlp_pre_g, mlp_post_g, w_in, mu_prev, mu_next, decay_w0, decay_w2, iclr_a0, iclr_a2, k_k, k_a, r_k, gate_w2, lnx_w, lnx_b, conv_w, conv_b, conv_ln_w, conv_ln_b, w_out, mlp_w1, mlp_w2)))
    moms = dict(zip(_WEIGHT_NAMES, (m_c_ctx, m_ada_w, m_ada_b, m_mix_pre_g, m_mix_post_g, m_mlp_pre_g, m_mlp_post_g, m_w_in, m_mu_prev, m_mu_next, m_decay_w0, m_decay_w2, m_iclr_a0, m_iclr_a2, m_k_k, m_k_a, m_r_k, m_gate_w2, m_lnx_w, m_lnx_b, m_conv_w, m_conv_b, m_conv_ln_w, m_conv_ln_b, m_w_out, m_mlp_w1, m_mlp_w2)))
    vars_ = dict(zip(_WEIGHT_NAMES, (v_c_ctx, v_ada_w, v_ada_b, v_mix_pre_g, v_mix_post_g, v_mlp_pre_g, v_mlp_post_g, v_w_in, v_mu_prev, v_mu_next, v_decay_w0, v_decay_w2, v_iclr_a0, v_iclr_a2, v_k_k, v_k_a, v_r_k, v_gate_w2, v_lnx_w, v_lnx_b, v_conv_w, v_conv_b, v_conv_ln_w, v_conv_ln_b, v_w_out, v_mlp_w1, v_mlp_w2)))

    B, t_lat, _ = x.shape
    assert ctx.shape[1] == TT and t_lat % TT == 0 and (t_lat * B) % MT == 0
    T = _Tiles(B, t_lat)
    ix, iy, ic = lax.axis_index("x"), lax.axis_index("y"), lax.axis_index("c")
    chip = 2 * ix + iy
    dev = 4 * ix + 2 * iy + ic
    nsh = 4
    in_sh = w_in.shape[2]
    ada_sh = ada_w.shape[2]
    lane_sh = decay_w0.shape[2]

    big_parts = (w_in[0], w_out[0], mlp_w1[0], mlp_w2[0])
    big_shapes = [p.shape for p in big_parts]
    wg = _exchange(_pack_rows([p.astype(BF16) for p in big_parts]), "chips", True, "gather_big_weights")
    per = [_unpack(wg[j].reshape(-1), big_shapes) for j in range(nsh)]
    w_in_f = jnp.concatenate([per[j][0] for j in range(nsh)], axis=1)
    w_out_f = jnp.concatenate([per[j][1] for j in range(nsh)], axis=0)
    w1_f = jnp.concatenate([per[j][2] for j in range(nsh)], axis=1)
    w2_f = jnp.concatenate([per[j][3] for j in range(nsh)], axis=0)
    w_in_p = _pad_cols(w_in_f, w_in_f.shape[1])

    sm_parts = (decay_w0[0], decay_w2[0], iclr_a0[0], iclr_a2[0], gate_w2[0], conv_w[0])
    sm_shapes = [p.shape for p in sm_parts]
    sg = _exchange(_pack_rows(sm_parts), "chips", True, "gather_small_weights")
    pers = [_unpack(sg[j].reshape(-1), sm_shapes) for j in range(nsh)]
    w0_f, w2_f_, a0_f, a2_f, gate_f, convw_f = (jnp.concatenate([pers[j][t] for j in range(nsh)], axis=-1)
                                                for t in range(6))

    def pad_rows(a, n):
        return jnp.pad(a, [(0, 0)] * (a.ndim - 2) + [(0, n - a.shape[-2]), (0, 0)])

    P = dict(
        w_rw=w_in_p[:, :RWC], w_cv=w_in_p[:, RWC:], w_out=w_out_f, w1=w1_f, w2m=w2_f,
        mix_pre_g=mix_pre_g, mix_post_g=mix_post_g, mlp_pre_g=mlp_pre_g, mlp_post_g=mlp_post_g,
        mu_p=_pad_cols(mu_prev, mu_prev.shape[1]), mu_n=_pad_cols(mu_next, mu_next.shape[1]),
        w0=w0_f, w2=pad_rows(w2_f_, LRW), a0=a0_f, a2=pad_rows(a2_f, LRW), k_k=k_k, k_a=k_a,
        r_k=r_k.reshape(1, W), gate_w2=pad_rows(gate_f, GDW), lnx_w=lnx_w, lnx_b=lnx_b,
        conv_w=pad_rows(convw_f, 32), conv_b=conv_b, conv_ln_w=conv_ln_w, conv_ln_b=conv_ln_b)

    c_ctx2 = c_ctx.reshape(1, D)
    c_all = _exchange(jnp.pad(c, ((0, 8 - B), (0, 0))), "all", True, "gather_c")[:, :B].reshape(8 * B, D)
    ada_b_blk = lax.dynamic_slice(ada_b, (0, chip * ada_sh), (1, ada_sh))
    mod_blk = _ada_fwd(c_all, c_ctx2, ada_w[0], ada_b_blk)
    mod_g = _exchange(mod_blk, "chips", True, "gather_mod")
    mod_all = jnp.concatenate([mod_g[j] for j in range(nsh)], axis=1)
    mod_x = lax.dynamic_slice(mod_all, (dev * B, 0), (B, 6 * D)).reshape(B, 6, D)
    mod_c = jnp.broadcast_to(mod_all[8 * B].reshape(1, 6, D), (B, 6, D))
    modrows = jnp.stack([mod_c, mod_x], axis=1)

    xc = jnp.concatenate([ctx, x], axis=1).reshape(T.NTOK, D)
    tgt = loss_target.reshape(T.NLAT, D)
    loss_loc, dxc, small, big, dm_ = _local_step(T, xc, tgt, modrows, P)
    loss = lax.psum(loss_loc, ("x", "y", "c"))
    grad_x = dxc.reshape(B, T.TTOT, D)[:, TT:, :]

    dmod_x = jnp.concatenate([dm_["dmod1"][:, 1], dm_["dg1"], dm_["dmod2"], dm_["dg2"]], axis=1)
    dmod_c = jnp.concatenate([dm_["dmod1"][:, 0], jnp.zeros((B, 4, D), F32)], axis=1)
    dpack = jnp.concatenate([dmod_x.reshape(B, 6 * D), dmod_c.reshape(B, 6 * D)], axis=0)
    dg = _exchange(dpack, "all", True, "gather_dmod")
    ex = dg[:, :B].reshape(8 * B, 6 * D)
    cx = dg[:, B:].reshape(8 * B, 6 * D)
    ex_blk = lax.dynamic_slice(ex, (0, chip * ada_sh), (8 * B, ada_sh))
    cx_blk = lax.dynamic_slice(cx, (0, chip * ada_sh), (8 * B, ada_sh))
    g_ada_w, g_ada_b, dscc = _ada_bwd(c_all, c_ctx2, ada_w[0], ex, cx, ex_blk, cx_blk)
    dscc_g = _exchange(dscc, "chips", True, "gather_dcctx")
    g_c_ctx = _cctx_final(dscc_g, c_ctx2).reshape(D)

    sm_list = [small[n] for n in _SMALL_ORDER]
    sm_shapes2 = [a.shape for a in sm_list]
    sm_tot = _sum_slots(_exchange(_pack_rows(sm_list), "all", True, "gather_small_grads"), "sum_small_grads")
    S = dict(zip(_SMALL_ORDER, _unpack(sm_tot.reshape(-1), sm_shapes2)))

    def shard_last(a):
        return lax.dynamic_slice_in_dim(a, chip * lane_sh, lane_sh, axis=a.ndim - 1)

    grads = dict(
        c_ctx=g_c_ctx, ada_w=g_ada_w[None], ada_b=g_ada_b,
        mix_pre_g=S["mix_pre_g"], mix_post_g=S["mix_post_g"], mlp_pre_g=S["mlp_pre_g"], mlp_post_g=S["mlp_post_g"],
        mu_prev=_unpad_cols(S["mu_p"], mu_prev.shape[1]), mu_next=_unpad_cols(S["mu_n"], mu_next.shape[1]),
        decay_w0=shard_last(S["w0"])[None], decay_w2=shard_last(S["w2"][:, :decay_w2.shape[2]])[None],
        iclr_a0=shard_last(S["a0"])[None], iclr_a2=shard_last(S["a2"][:, :iclr_a2.shape[2]])[None],
        k_k=S["k_k"], k_a=S["k_a"], r_k=S["r_k"].reshape(r_k.shape),
        gate_w2=shard_last(S["gate_w2"][:gate_w2.shape[1]])[None], lnx_w=S["lnx_w"], lnx_b=S["lnx_b"],
        conv_w=shard_last(S["conv_w"][:KCONV])[None], conv_b=S["conv_b"], conv_ln_w=S["conv_ln_w"],
        conv_ln_b=S["conv_ln_b"])

    dw_in_f = _unpad_cols(jnp.concatenate([big["w_rw"], big["w_cv"]], axis=1), w_in_f.shape[1])
    oshape = w_out.shape[1]
    mshape = mlp_w1.shape[2]
    slabs = jnp.stack([
        _pack_rows([dw_in_f[:, in_sh * j:in_sh * (j + 1)], big["w_out"][oshape * j:oshape * (j + 1)],
                    big["w1"][:, mshape * j:mshape * (j + 1)], big["w2m"][mshape * j:mshape * (j + 1)]])
        for j in range(nsh)])
    part = _sum_slots(_exchange(slabs, "chips", False, "reduce_big_grads"), "sum_big_grads")
    tot = _sum_slots(_exchange(part, "sib", True, "swap_big_grads"), "sum_sib_grads")
    g_w_in, g_w_out, g_w1, g_w2 = _unpack(tot.reshape(-1), big_shapes)
    grads.update(w_in=g_w_in[None], w_out=g_w_out[None], mlp_w1=g_w1[None], mlp_w2=g_w2[None])

    deltas, new_m, new_v = {}, {}, {}
    for n in _WEIGHT_NAMES:
        g = grads[n].reshape(weights[n].shape)
        grads[n] = g
        deltas[n], new_m[n], new_v[n] = _adamw(weights[n], g, moms[n], vars_[n], "adamw_" + n)

    return (loss, grad_x, *[grads[n] for n in _WEIGHT_NAMES], *[deltas[n] for n in _WEIGHT_NAMES],
            *[new_m[n] for n in _WEIGHT_NAMES], *[new_v[n] for n in _WEIGHT_NAMES])
```

```python
import functools

import jax
import jax.numpy as jnp
from jax import lax
from jax.experimental import pallas as pl
from jax.experimental.pallas import tpu as pltpu

F32 = jnp.float32
BF16 = jnp.bfloat16
HI = lax.Precision.HIGHEST

D = 1024
W = 512
HS = 64
RWC = 2304
CVC = 1024
GDW = 256
LRW = 128
DFF = 4096
TT = 256
LINE = 64
KCONV = 31
EPS_RMS = 1e-6
EPS_LN = 1e-5
EPS_GN = 64e-5
SCAN_CH = 128
SCAN_SUB = 32

ADAM_LR = 0.001
ADAM_B1 = 0.9
ADAM_B2 = 0.999
ADAM_EPS = 1e-08
ADAM_WD = 0.01
ADAM_STEP = 10

_SEGS = ((0, 1536, 1536), (1536, 64, 128), (1600, 64, 128), (1664, 64, 128), (1728, 64, 128),
         (1792, 160, 256), (1952, 1024, 1024))

MESH = pl.DeviceIdType.MESH


def _bs(shape, imap):
    return pl.BlockSpec(shape, imap)


def _cp(sem=None, mb=48):
    return pltpu.CompilerParams(dimension_semantics=sem, vmem_limit_bytes=mb << 20)


def _pad_cols(a, ncols):
    out = []
    for s, w, pw in _SEGS:
        if s >= ncols:
            break
        piece = a[..., s:s + w]
        if pw > w:
            piece = jnp.pad(piece, [(0, 0)] * (a.ndim - 1) + [(0, pw - w)])
        out.append(piece)
    return jnp.concatenate(out, axis=-1)


def _unpad_cols(a, ncols):
    out = []
    off = 0
    for s, w, pw in _SEGS:
        if s >= ncols:
            break
        out.append(a[..., off:off + w])
        off += pw
    return jnp.concatenate(out, axis=-1)


def _sigmoid(x):
    return 1.0 / (1.0 + jnp.exp(-x))


def _softplus(x):
    return jnp.maximum(x, 0.0) + jnp.log(1.0 + jnp.exp(-jnp.abs(x)))


def _e128(dtype):
    r = lax.broadcasted_iota(jnp.int32, (128, 128), 0) >= HS
    c = lax.broadcasted_iota(jnp.int32, (128, 128), 1) >= HS
    return (r == c).astype(dtype)


def _segsum(x, e):
    return jnp.concatenate(
        [jnp.dot(x[:, 128 * g:128 * (g + 1)], e, precision=HI, preferred_element_type=F32) for g in range(4)],
        axis=1)


def _colsum(x):
    return jnp.sum(x, axis=0, keepdims=True)


def _rowmean(x):
    return jnp.mean(x, axis=-1, keepdims=True)


def _diag(x, npairs):
    row = lax.broadcasted_iota(jnp.int32, (HS, 128), 0)
    lane = lax.broadcasted_iota(jnp.int32, (HS, 128), 1) & (HS - 1)
    keep = jnp.broadcast_to((lane == row)[None], (npairs, HS, 128))
    return jnp.where(keep, x.reshape(npairs, HS, 128), 0.0).reshape(npairs * HS, 128)


def _segb(x, e):
    return jnp.dot(x.astype(BF16), e, preferred_element_type=F32)


_segb1 = _segb


def _expand(row, npairs):
    return jnp.concatenate([jnp.broadcast_to(row[:, 128 * j:128 * (j + 1)], (HS, 128)) for j in range(npairs)], axis=0)


def _colb(row, npairs, e):
    return _segb1(_diag(_expand(row, npairs), npairs), e)


def _pair_colsum(x, npairs):
    return jnp.concatenate([_colsum(x[HS * j:HS * (j + 1)]) for j in range(npairs)], axis=1)


def _conv_pos():
    return lax.broadcasted_iota(jnp.int32, (TT, W), 0) & (LINE - 1)


def _shifted(u, s, pos):
    if s == 0:
        return u
    sh = pltpu.roll(u, (-s) % TT, 0)
    valid = jnp.logical_and(pos + s >= 0, pos + s < LINE)
    return jnp.where(valid, sh, 0.0)


def _acc(ref, val, first):
    @pl.when(first)
    def _():
        ref[...] = jnp.zeros(ref.shape, ref.dtype)
    ref[...] += val


class _Tiles:
    def __init__(self, B, t_lat):
        self.B = B
        self.NLT = t_lat // TT
        self.TPS = self.NLT + 1
        self.NT = B * self.TPS
        self.NL = B * self.NLT
        self.NTOK = self.NT * TT
        self.NLAT = self.NL * TT
        self.TTOT = self.TPS * TT
        self.BW = B * W

    def b(self, i):
        return i // self.TPS

    def q(self, i):
        return i % self.TPS

    def lat(self, i):
        return (i // self.TPS) * self.NLT + jnp.maximum(i % self.TPS - 1, 0)

    def tok(self, l):
        return (l // self.NLT) * self.TPS + 1 + l % self.NLT

    def mod_spec(self):
        return _bs((1, 1, 6, D), lambda i: (i // self.TPS, jnp.minimum(i % self.TPS, 1), 0, 0))

    def tm_spec(self):
        return _bs((TT, W), lambda i: (i % self.TPS, i // self.TPS))

    def tm2_spec(self):
        return _bs((2, TT, W), lambda i: (0, i % self.TPS, i // self.TPS))


def _row(shape_last):
    return _bs((1, shape_last), lambda i: (0, 0))


def _mix_in(T, xc, modrows, g, w_rw, w_cv):
    def kern(x_ref, mod_ref, g_ref, wr_ref, wc_ref, prw_ref, pcv_ref, h_ref):
        x = x_ref[...]
        s = lax.rsqrt(_rowmean(x * x) + EPS_RMS)
        h = (x * s * g_ref[...]) * (1.0 + mod_ref[0, 0, 1:2, :]) + mod_ref[0, 0, 0:1, :]
        hb = h.astype(BF16)
        h_ref[...] = hb
        prw_ref[...] = jnp.dot(hb, wr_ref[...], preferred_element_type=F32)
        pcv_ref[...] = jnp.dot(hb, wc_ref[...], preferred_element_type=F32)

    return pl.pallas_call(
        kern, grid=(T.NT,), name="mix_in",
        in_specs=[_bs((TT, D), lambda i: (i, 0)), T.mod_spec(), _row(D),
                  _bs((D, RWC), lambda i: (0, 0)), _bs((D, CVC), lambda i: (0, 0))],
        out_specs=[_bs((TT, RWC), lambda i: (i, 0)), _bs((TT, CVC), lambda i: (i, 0)), _bs((TT, D), lambda i: (i, 0))],
        out_shape=[jax.ShapeDtypeStruct((T.NTOK, RWC), F32), jax.ShapeDtypeStruct((T.NTOK, CVC), F32),
                   jax.ShapeDtypeStruct((T.NTOK, D), BF16)],
        compiler_params=_cp(("parallel",)),
    )(xc, modrows, g, w_rw, w_cv)


def _halo_specs(T):
    nb8 = T.NTOK // 8
    prev = _bs((8, RWC), lambda i: (jnp.maximum(i * (TT // 8) - 1, 0), 0))
    nxt = _bs((8, RWC), lambda i: (jnp.minimum((i + 1) * (TT // 8), nb8 - 1), 0))
    return prev, nxt


def _halo_masks(T, i):
    q = i % T.TPS
    has_prev = jnp.logical_and(q != 0, q != 1).astype(F32)
    has_next = jnp.logical_and(q != 0, q != T.TPS - 1).astype(F32)
    return has_prev, has_next


def _neighbours(z, prev_row, next_row):
    rowi = lax.broadcasted_iota(jnp.int32, z.shape, 0)
    zprev = jnp.where(rowi == 0, prev_row, pltpu.roll(z, 1, 0))
    znext = jnp.where(rowi == TT - 1, next_row, pltpu.roll(z, TT - 1, 0))
    return zprev, znext


def _prep_math(rw, w0, w2, a0, a2, k_k, k_a, e):
    r = rw[:, 0:512]
    k = rw[:, 512:1024]
    v = rw[:, 1024:1536]
    kr = k * k_k
    ss = _segsum(kr * kr, e)
    rt = jnp.sqrt(ss)
    inv = 1.0 / jnp.maximum(rt, 1e-12)
    kk = kr * inv
    o = dict(r=r, k=k, v=v, kr=kr, rt=rt, inv=inv, kk=kk, th=[], pre=[], ex=[], dec=[], iclr=[], kd=[], bb=[], ad=[])
    for d in (0, 1):
        wd = rw[:, 1536 + LRW * d:1536 + LRW * (d + 1)]
        ad = rw[:, 1792 + LRW * d:1792 + LRW * (d + 1)]
        th = jnp.tanh(wd)
        pre = w0[d] + jnp.dot(th, w2[d], precision=HI, preferred_element_type=F32)
        ex = jnp.exp(-_softplus(-pre) - 0.5)
        dec = jnp.exp(-ex)
        iclr = _sigmoid(a0[d] + jnp.dot(ad, a2[d], precision=HI, preferred_element_type=F32))
        o["th"].append(th)
        o["pre"].append(pre)
        o["ex"].append(ex)
        o["dec"].append(dec)
        o["iclr"].append(iclr)
        o["ad"].append(ad)
        o["kd"].append(k * (1.0 + (iclr - 1.0) * k_a))
        o["bb"].append(kk * iclr)
    return o


def _load_prep_params(w0_ref, w2_ref, a0_ref, a2_ref):
    w0 = [w0_ref[0:1, :], w0_ref[1:2, :]]
    a0 = [a0_ref[0:1, :], a0_ref[1:2, :]]
    w2 = [w2_ref[0], w2_ref[1]]
    a2 = [a2_ref[0], a2_ref[1]]
    return w0, w2, a0, a2


def _prep_param_specs():
    return [_bs((2, W), lambda i: (0, 0)), _bs((2, LRW, W), lambda i: (0, 0, 0)),
            _bs((2, W), lambda i: (0, 0)), _bs((2, LRW, W), lambda i: (0, 0, 0)), _row(W), _row(W)]


def _rwkv_prep(T, p_rw, mu_p, mu_n, w0, w2, a0, a2, k_k, k_a):
    def kern(p_ref, pp_ref, pn_ref, mp_ref, mn_ref, w0_ref, w2_ref, a0_ref, a2_ref, kk_ref, ka_ref,
             r_o, v_o, kk_o, dec_o, kd_o, bb_o, rw_o):
        i = pl.program_id(0)
        has_prev, has_next = _halo_masks(T, i)
        z = p_ref[...]
        zprev, znext = _neighbours(z, pp_ref[7:8, :] * has_prev, pn_ref[0:1, :] * has_next)
        rw = z + mp_ref[...] * (zprev - z) + mn_ref[...] * (znext - z)
        rw_o[...] = rw
        w0v, w2v, a0v, a2v = _load_prep_params(w0_ref, w2_ref, a0_ref, a2_ref)
        o = _prep_math(rw, w0v, w2v, a0v, a2v, kk_ref[...], ka_ref[...], _e128(F32))
        r_o[...] = o["r"]
        v_o[...] = o["v"]
        kk_o[...] = o["kk"]
        for d in (0, 1):
            dec_o[d] = o["dec"][d]
            kd_o[d] = o["kd"][d]
            bb_o[d] = o["bb"][d]

    prev, nxt = _halo_specs(T)
    tm = jax.ShapeDtypeStruct((T.TTOT, T.BW), F32)
    tm2 = jax.ShapeDtypeStruct((2, T.TTOT, T.BW), F32)
    return pl.pallas_call(
        kern, grid=(T.NT,), name="rwkv_prep",
        in_specs=[_bs((TT, RWC), lambda i: (i, 0)), prev, nxt, _row(RWC), _row(RWC)] + _prep_param_specs(),
        out_specs=[T.tm_spec(), T.tm_spec(), T.tm_spec(), T.tm2_spec(), T.tm2_spec(), T.tm2_spec(),
                   _bs((TT, RWC), lambda i: (i, 0))],
        out_shape=[tm, tm, tm, tm2, tm2, tm2, jax.ShapeDtypeStruct((T.NTOK, RWC), F32)],
        compiler_params=_cp(("parallel",)),
    )(p_rw, p_rw, p_rw, mu_p, mu_n, w0, w2, a0, a2, k_k, k_a)


def _scan_fwd(T, r, v, kk, dec, kd, bb):
    NP = T.BW // 128
    NCH = T.TTOT // SCAN_CH
    NCC = TT // SCAN_CH
    NQ = SCAN_CH // SCAN_SUB

    def tmap(d, i):
        rev = jnp.where(i < NCC, NCC - 1 - i, NCH - 1 - (i - NCC))
        return jnp.where(d == 0, i, rev)

    def kern(r_ref, v_ref, kk_ref, dec_ref, kd_ref, bb_ref, y_ref, ck_ref, s_ref):
        d = pl.program_id(0)
        i = pl.program_id(1)

        @pl.when(i == 0)
        def _():
            s_ref[...] = jnp.zeros_like(s_ref)

        e = _e128(BF16)

        def step(t, carry):
            row = jnp.where(d == 0, t, SCAN_CH - 1 - t)
            a_ = _expand(-kk_ref[pl.ds(row, 1), :], NP)
            s = s_ref[...]
            sa = _segb(s * a_, e)
            vc = _colb(v_ref[pl.ds(row, 1), :], NP, e)
            s = (s * _expand(dec_ref[0, pl.ds(row, 1), :], NP) + sa * _expand(bb_ref[0, pl.ds(row, 1), :], NP)
                 + vc * _expand(kd_ref[0, pl.ds(row, 1), :], NP))
            s_ref[...] = s
            yb = _segb1(s * _expand(r_ref[pl.ds(row, 1), :], NP), e)
            y_ref[0, pl.ds(row, 1), :] = _pair_colsum(_diag(yb, NP), NP)
            return carry

        for qd in range(NQ):
            ck_ref[0, qd] = s_ref[...]
            lax.fori_loop(qd * SCAN_SUB, (qd + 1) * SCAN_SUB, step, 0)

    sh = _bs((SCAN_CH, T.BW), lambda d, i: (tmap(d, i), 0))
    dr = _bs((1, SCAN_CH, T.BW), lambda d, i: (d, tmap(d, i), 0))
    return pl.pallas_call(
        kern, grid=(2, NCH), name="scan_fwd",
        in_specs=[sh, sh, sh, dr, dr, dr],
        out_specs=[dr, _bs((1, NQ, NP * HS, 128), lambda d, i: (d, i, 0, 0))],
        out_shape=[jax.ShapeDtypeStruct((2, T.TTOT, T.BW), F32),
                   jax.ShapeDtypeStruct((2, NCH * NQ, NP * HS, 128), F32)],
        scratch_shapes=[pltpu.VMEM((NP * HS, 128), F32)],
        compiler_params=_cp(("arbitrary", "arbitrary")),
    )(r, v, kk, dec, kd, bb)


def _readout_fwd(y, r, v, gd, kbar, rk, gw, lw, lb, e):
    mu = _segsum(y, e) * (1.0 / HS)
    yc = y - mu
    var = _segsum(yc * yc, e) * (1.0 / HS)
    rstd = lax.rsqrt(var + EPS_GN)
    yhat = yc * rstd
    yn = yhat * lw + lb
    q = _segsum(r * kbar * rk, e)
    sg = _sigmoid(gd)
    gg = jnp.dot(sg, gw, precision=HI, preferred_element_type=F32)
    return dict(yhat=yhat, rstd=rstd, yn=yn, q=q, sg=sg, gg=gg, out=(yn + q * v) * gg)


def _conv_fwd(cva, cvb, cw_ref, cb, lw, lb):
    pos = _conv_pos()
    sgb = _sigmoid(cvb)
    u = cva * sgb
    c = jnp.zeros_like(u)
    for j in range(KCONV):
        c = c + cw_ref[j:j + 1, :] * _shifted(u, j - KCONV // 2, pos)
    c = c + cb
    mu = _rowmean(c)
    cc = c - mu
    rstd = lax.rsqrt(_rowmean(cc * cc) + EPS_LN)
    chat = cc * rstd
    cn = chat * lw + lb
    scn = _sigmoid(cn)
    return dict(sgb=sgb, u=u, chat=chat, rstd=rstd, cn=cn, scn=scn, out=cn * scn, pos=pos)


def _mix_out(T, y, kd, rw, p_cv, xc, modrows, rk, gw, lnw, lnb, cw, cb, clw, clb, pg, w_out):
    tk = T.tok

    def kern(y_ref, kd_ref, rw_ref, pcv_ref, x_ref, mod_ref, rk_ref, gw_ref, lw_ref, lb_ref, cw_ref, cb_ref,
             clw_ref, clb_ref, pg_ref, wo_ref, cat_o, mix_o, x1_o):
        e = _e128(F32)
        ro = _readout_fwd(y_ref[0] + y_ref[1], rw_ref[:, 0:512], rw_ref[:, 1024:1536], rw_ref[:, 2048:2304],
                          0.5 * (kd_ref[0] + kd_ref[1]), rk_ref[...], gw_ref[...], lw_ref[...], lb_ref[...], e)
        cv = _conv_fwd(pcv_ref[:, 0:512], pcv_ref[:, 512:1024], cw_ref, cb_ref[...], clw_ref[...], clb_ref[...])
        catb = jnp.concatenate([ro["out"], cv["out"]], axis=1).astype(BF16)
        cat_o[...] = catb
        mix = jnp.dot(catb, wo_ref[...], preferred_element_type=F32)
        mix_o[...] = mix
        sm = lax.rsqrt(_rowmean(mix * mix) + EPS_RMS)
        x1_o[...] = x_ref[...] + mod_ref[0, 0, 2:3, :] * (mix * sm * pg_ref[...])

    lat = lambda l: (l, 0)
    return pl.pallas_call(
        kern, grid=(T.NL,), name="mix_out",
        in_specs=[_bs((2, TT, W), lambda l: (0, 1 + l % T.NLT, l // T.NLT)),
                  _bs((2, TT, W), lambda l: (0, 1 + l % T.NLT, l // T.NLT)),
                  _bs((TT, RWC), lambda l: (tk(l), 0)), _bs((TT, CVC), lambda l: (tk(l), 0)),
                  _bs((TT, D), lambda l: (tk(l), 0)),
                  _bs((1, 1, 6, D), lambda l: (l // T.NLT, 1, 0, 0)),
                  _row(W), _bs((GDW, W), lambda l: (0, 0)), _row(W), _row(W),
                  _bs((32, W), lambda l: (0, 0)), _row(W), _row(W), _row(W), _row(D),
                  _bs((D, D), lambda l: (0, 0))],
        out_specs=[_bs((TT, D), lat), _bs((TT, D), lat), _bs((TT, D), lat)],
        out_shape=[jax.ShapeDtypeStruct((T.NLAT, D), BF16), jax.ShapeDtypeStruct((T.NLAT, D), F32),
                   jax.ShapeDtypeStruct((T.NLAT, D), F32)],
        compiler_params=_cp(("parallel",)),
    )(y, kd, rw, p_cv, xc, modrows, rk, gw, lnw, lnb, cw, cb, clw, clb, pg, w_out)


MT = 512
FC = 1024


def _mlp_fwd(T, x1, modrows, g, w1, w2):
    per_b = T.NLT * TT // MT

    def kern(x_ref, mod_ref, g_ref, w1_ref, w2_ref, m_o, h2_o, h2_s):
        f = pl.program_id(1)

        @pl.when(f == 0)
        def _():
            x = x_ref[...]
            s = lax.rsqrt(_rowmean(x * x) + EPS_RMS)
            h2 = (x * s * g_ref[...]) * (1.0 + mod_ref[0, 0, 4:5, :]) + mod_ref[0, 0, 3:4, :]
            h2_s[...] = h2.astype(BF16)
            h2_o[...] = h2.astype(BF16)
            m_o[...] = jnp.zeros_like(m_o)

        a = jnp.dot(h2_s[...], w1_ref[...], preferred_element_type=F32)
        rl = jnp.maximum(a, 0.0)
        m_o[...] += jnp.dot((rl * rl).astype(BF16), w2_ref[...], preferred_element_type=F32)

    tok = lambda t, f: (t, 0)
    return pl.pallas_call(
        kern, grid=(T.NLAT // MT, DFF // FC), name="mlp_fwd",
        in_specs=[_bs((MT, D), tok), _bs((1, 1, 6, D), lambda t, f: (t // per_b, 1, 0, 0)),
                  _bs((1, D), lambda t, f: (0, 0)), _bs((D, FC), lambda t, f: (0, f)), _bs((FC, D), lambda t, f: (f, 0))],
        out_specs=[_bs((MT, D), tok), _bs((MT, D), tok)],
        out_shape=[jax.ShapeDtypeStruct((T.NLAT, D), F32), jax.ShapeDtypeStruct((T.NLAT, D), BF16)],
        scratch_shapes=[pltpu.VMEM((MT, D), BF16)],
        compiler_params=_cp(("parallel", "arbitrary")),
    )(x1, modrows, g, w1, w2)


def _loss_head(T, m, x1, tgt, modrows, pg):
    def kern(m_ref, x1_ref, t_ref, mod_ref, pg_ref, loss_o, dm_o, dx2_o, dg2_o, dpg_o):
        l = pl.program_id(0)
        m_ = m_ref[...]
        sm = lax.rsqrt(_rowmean(m_ * m_) + EPS_RMS)
        mn = m_ * sm
        g2 = mod_ref[0, 0, 5:6, :]
        pgv = pg_ref[...]
        diff = x1_ref[...] + g2 * (mn * pgv) - t_ref[...]
        sq = jnp.sum(_colsum(diff * diff), axis=1, keepdims=True)
        _acc(loss_o, jnp.zeros((8, 128), F32) + (0.5 / D) * sq, l == 0)
        dx2 = diff * (1.0 / D)
        dx2_o[...] = dx2
        _acc(dg2_o.at[0], _colsum(dx2 * mn * pgv), l % T.NLT == 0)
        _acc(dpg_o, _colsum(dx2 * g2 * mn), l == 0)
        dmn = dx2 * g2 * pgv
        dm_o[...] = (sm * (dmn - mn * _rowmean(dmn * mn))).astype(BF16)

    lat = lambda l: (l, 0)
    return pl.pallas_call(
        kern, grid=(T.NL,), name="loss_head",
        in_specs=[_bs((TT, D), lat), _bs((TT, D), lat), _bs((TT, D), lat),
                  _bs((1, 1, 6, D), lambda l: (l // T.NLT, 1, 0, 0)), _row(D)],
        out_specs=[_bs((8, 128), lambda l: (0, 0)), _bs((TT, D), lat), _bs((TT, D), lat),
                   _bs((1, 1, D), lambda l: (l // T.NLT, 0, 0)), _row(D)],
        out_shape=[jax.ShapeDtypeStruct((8, 128), F32), jax.ShapeDtypeStruct((T.NLAT, D), BF16),
                   jax.ShapeDtypeStruct((T.NLAT, D), F32), jax.ShapeDtypeStruct((T.B, 1, D), F32),
                   jax.ShapeDtypeStruct((1, D), F32)],
        compiler_params=_cp(("arbitrary",)),
    )(m, x1, tgt, modrows, pg)


_NT_DIMS = (((1,), (1,)), ((), ()))
_TN_DIMS = (((0,), (0,)), ((), ()))


def _mlp_bwd(T, h2, dm, w1, w2):
    def kern(h2_ref, dm_ref, w1_ref, w2_ref, f_o, da_o, dh2_o):
        f = pl.program_id(1)
        a = jnp.dot(h2_ref[...], w1_ref[...], preferred_element_type=F32)
        rl = jnp.maximum(a, 0.0)
        f_o[...] = (rl * rl).astype(BF16)
        df = lax.dot_general(dm_ref[...], w2_ref[...], _NT_DIMS, preferred_element_type=F32)
        dab = (df * (2.0 * rl)).astype(BF16)
        da_o[...] = dab
        _acc(dh2_o, lax.dot_general(dab, w1_ref[...], _NT_DIMS, preferred_element_type=F32), f == 0)

    tok = lambda t, f: (t, 0)
    return pl.pallas_call(
        kern, grid=(T.NLAT // MT, DFF // FC), name="mlp_bwd",
        in_specs=[_bs((MT, D), tok), _bs((MT, D), tok), _bs((D, FC), lambda t, f: (0, f)),
                  _bs((FC, D), lambda t, f: (f, 0))],
        out_specs=[_bs((MT, FC), lambda t, f: (t, f)), _bs((MT, FC), lambda t, f: (t, f)), _bs((MT, D), tok)],
        out_shape=[jax.ShapeDtypeStruct((T.NLAT, DFF), BF16), jax.ShapeDtypeStruct((T.NLAT, DFF), BF16),
                   jax.ShapeDtypeStruct((T.NLAT, D), F32)],
        compiler_params=_cp(("parallel", "arbitrary")),
    )(h2, dm, w1, w2)


def _mlp_in_bwd(T, dh2, x1, dx2, modrows, g):
    def kern(dh_ref, x1_ref, dx2_ref, mod_ref, g_ref, dx1_o, dmod_o, dg_o):
        i = pl.program_id(0)
        lat = (i % T.TPS != 0).astype(F32)
        x = x1_ref[...]
        s = lax.rsqrt(_rowmean(x * x) + EPS_RMS)
        xh = x * s
        gv = g_ref[...]
        dh = dh_ref[...] * lat
        n2 = xh * gv
        first_b = i % T.TPS == 0
        _acc(dmod_o.at[0, 0:1, :], _colsum(dh), first_b)
        _acc(dmod_o.at[0, 1:2, :], _colsum(dh * n2), first_b)
        dn2 = dh * (1.0 + mod_ref[0, 0, 4:5, :])
        _acc(dg_o, _colsum(dn2 * xh), i == 0)
        dxh = dn2 * gv
        dx1_o[...] = (dx2_ref[...] + s * (dxh - xh * _rowmean(dxh * xh))) * lat

    lat_i = lambda i: (T.lat(i), 0)
    return pl.pallas_call(
        kern, grid=(T.NT,), name="mlp_in_bwd",
        in_specs=[_bs((TT, D), lat_i), _bs((TT, D), lat_i), _bs((TT, D), lat_i),
                  _bs((1, 1, 6, D), lambda i: (i // T.TPS, 1, 0, 0)), _row(D)],
        out_specs=[_bs((TT, D), lambda i: (i, 0)), _bs((1, 2, D), lambda i: (i // T.TPS, 0, 0)), _row(D)],
        out_shape=[jax.ShapeDtypeStruct((T.NTOK, D), F32), jax.ShapeDtypeStruct((T.B, 2, D), F32),
                   jax.ShapeDtypeStruct((1, D), F32)],
        compiler_params=_cp(("arbitrary",)),
    )(dh2, x1, dx2, modrows, g)


def _mix_post_bwd(T, dx1, mix, modrows, pg, w_out):
    def kern(dx_ref, mix_ref, mod_ref, pg_ref, wo_ref, dmix_o, dcat_o, dg1_o, dpg_o):
        i = pl.program_id(0)
        lat = (i % T.TPS != 0).astype(F32)
        dx = dx_ref[...]
        mix = mix_ref[...]
        sm = lax.rsqrt(_rowmean(mix * mix) + EPS_RMS)
        mh = mix * sm
        g1 = mod_ref[0, 0, 2:3, :]
        pgv = pg_ref[...]
        _acc(dg1_o.at[0], _colsum(dx * mh * pgv), i % T.TPS == 0)
        _acc(dpg_o, _colsum(dx * g1 * mh), i == 0)
        dmh = dx * g1 * pgv
        dmix = ((sm * (dmh - mh * _rowmean(dmh * mh))) * lat).astype(BF16)
        dmix_o[...] = dmix
        dcat_o[...] = lax.dot_general(dmix, wo_ref[...], _NT_DIMS, preferred_element_type=F32)

    tok = lambda i: (i, 0)
    return pl.pallas_call(
        kern, grid=(T.NT,), name="mix_post_bwd",
        in_specs=[_bs((TT, D), tok), _bs((TT, D), lambda i: (T.lat(i), 0)),
                  _bs((1, 1, 6, D), lambda i: (i // T.TPS, 1, 0, 0)), _row(D), _bs((D, D), lambda i: (0, 0))],
        out_specs=[_bs((TT, D), tok), _bs((TT, D), tok), _bs((1, 1, D), lambda i: (i // T.TPS, 0, 0)), _row(D)],
        out_shape=[jax.ShapeDtypeStruct((T.NTOK, D), BF16), jax.ShapeDtypeStruct((T.NTOK, D), F32),
                   jax.ShapeDtypeStruct((T.B, 1, D), F32), jax.ShapeDtypeStruct((1, D), F32)],
        compiler_params=_cp(("arbitrary",)),
    )(dx1, mix, modrows, pg, w_out)


def _conv_bwd(T, dcat, p_cv, cw, cb, clw, clb):
    def kern(dc_ref, pcv_ref, cw_ref, cb_ref, clw_ref, clb_ref, dp_o, dcw_o, dcb_o, dlw_o, dlb_o):
        i = pl.program_id(0)
        first = i == 0
        cva = pcv_ref[:, 0:512]
        cv = _conv_fwd(cva, pcv_ref[:, 512:1024], cw_ref, cb_ref[...], clw_ref[...], clb_ref[...])
        dout = dc_ref[...]
        scn = cv["scn"]
        dcn = dout * (scn * (1.0 + cv["cn"] * (1.0 - scn)))
        chat = cv["chat"]
        _acc(dlw_o, _colsum(dcn * chat), first)
        _acc(dlb_o, _colsum(dcn), first)
        dchat = dcn * clw_ref[...]
        dc = cv["rstd"] * (dchat - _rowmean(dchat) - chat * _rowmean(dchat * chat))
        _acc(dcb_o, _colsum(dc), first)
        pos = cv["pos"]
        u = cv["u"]
        du = jnp.zeros_like(u)
        for j in range(KCONV):
            s = j - KCONV // 2
            _acc(dcw_o.at[j:j + 1, :], _colsum(dc * _shifted(u, s, pos)), first)
            du = du + cw_ref[j:j + 1, :] * _shifted(dc, -s, pos)
        _acc(dcw_o.at[KCONV:KCONV + 1, :], jnp.zeros((1, W), F32), first)
        sgb = cv["sgb"]
        dp_o[...] = jnp.concatenate([du * sgb, du * cva * sgb * (1.0 - sgb)], axis=1).astype(BF16)

    return pl.pallas_call(
        kern, grid=(T.NT,), name="conv_bwd",
        in_specs=[_bs((TT, W), lambda i: (i, 1)), _bs((TT, CVC), lambda i: (i, 0)),
                  _bs((32, W), lambda i: (0, 0)), _row(W), _row(W), _row(W)],
        out_specs=[_bs((TT, CVC), lambda i: (i, 0)), _bs((32, W), lambda i: (0, 0)), _row(W), _row(W), _row(W)],
        out_shape=[jax.ShapeDtypeStruct((T.NTOK, CVC), BF16), jax.ShapeDtypeStruct((32, W), F32),
                   jax.ShapeDtypeStruct((1, W), F32), jax.ShapeDtypeStruct((1, W), F32),
                   jax.ShapeDtypeStruct((1, W), F32)],
        compiler_params=_cp(("arbitrary",)),
    )(dcat, p_cv, cw, cb, clw, clb)


def _readout_bwd(T, dcat, y, kd, rw, rk, gw, lnw, lnb):
    def kern(dc_ref, y_ref, kd_ref, rw_ref, rk_ref, gw_ref, lw_ref, lb_ref,
             dy_o, dr_o, dv_o, dkb_o, dgd_o, drk_o, dgw_o, dlw_o, dlb_o):
        i = pl.program_id(0)
        first = i == 0
        e = _e128(F32)
        r = rw_ref[:, 0:512]
        v = rw_ref[:, 1024:1536]
        kbar = 0.5 * (kd_ref[0] + kd_ref[1])
        rk = rk_ref[...]
        ro = _readout_fwd(y_ref[0] + y_ref[1], r, v, rw_ref[:, 2048:2304], kbar, rk, gw_ref[...],
                          lw_ref[...], lb_ref[...], e)
        dout = dc_ref[...]
        dgg = dout * (ro["yn"] + ro["q"] * v)
        t1 = dout * ro["gg"]
        yhat = ro["yhat"]
        _acc(dlw_o, _colsum(t1 * yhat), first)
        _acc(dlb_o, _colsum(t1), first)
        dyh = t1 * lw_ref[...]
        dy_o[...] = ro["rstd"] * (dyh - _segsum(dyh, e) * (1.0 / HS) - yhat * (_segsum(dyh * yhat, e) * (1.0 / HS)))
        dq = _segsum(t1 * v, e)
        dv_o[...] = t1 * ro["q"]
        dr_o[...] = dq * kbar * rk
        dkb_o[...] = dq * r * rk
        _acc(drk_o, _colsum(dq * r * kbar), first)
        sg = ro["sg"]
        dsg = lax.dot_general(dgg, gw_ref[...], _NT_DIMS, precision=HI, preferred_element_type=F32)
        dgd_o[...] = dsg * sg * (1.0 - sg)
        _acc(dgw_o, lax.dot_general(sg, dgg, _TN_DIMS, precision=HI, preferred_element_type=F32), first)

    tok = lambda i: (i, 0)
    f32s = lambda *s: jax.ShapeDtypeStruct(s, F32)
    return pl.pallas_call(
        kern, grid=(T.NT,), name="readout_bwd",
        in_specs=[_bs((TT, W), tok), T.tm2_spec(), T.tm2_spec(), _bs((TT, RWC), tok),
                  _row(W), _bs((GDW, W), lambda i: (0, 0)), _row(W), _row(W)],
        out_specs=[T.tm_spec(), _bs((TT, W), tok), _bs((TT, W), tok), _bs((TT, W), tok), _bs((TT, GDW), tok),
                   _row(W), _bs((GDW, W), lambda i: (0, 0)), _row(W), _row(W)],
        out_shape=[f32s(T.TTOT, T.BW), f32s(T.NTOK, W), f32s(T.NTOK, W), f32s(T.NTOK, W), f32s(T.NTOK, GDW),
                   f32s(1, W), f32s(GDW, W), f32s(1, W), f32s(1, W)],
        compiler_params=_cp(("arbitrary",)),
    )(dcat, y, kd, rw, rk, gw, lnw, lnb)


def _scan_bwd(T, dy, r, v, kk, dec, kd, bb, ck):
    NP = T.BW // 128
    NS = T.TTOT // SCAN_SUB
    NSC = TT // SCAN_SUB

    def tmap(d, g):
        s = NS - 1 - g
        rev = jnp.where(s < NSC, NSC - 1 - s, NS - 1 - (s - NSC))
        return jnp.where(d == 0, s, rev)

    def kern(dy_ref, r_ref, v_ref, kk_ref, dec_ref, kd_ref, bb_ref, ck_ref,
             dr_o, dw_o, dk_o, dv_o, da_o, db_o, hist, ds_ref):
        d = pl.program_id(0)
        g = pl.program_id(1)

        @pl.when(g == 0)
        def _():
            ds_ref[...] = jnp.zeros_like(ds_ref)

        e = _e128(BF16)

        def local_row(t):
            return jnp.where(d == 0, t, SCAN_SUB - 1 - t)

        def fwd(t, carry):
            row = local_row(t)
            s = hist[t]
            sa = _segb(s * _expand(-kk_ref[pl.ds(row, 1), :], NP), e)
            vc = _colb(v_ref[pl.ds(row, 1), :], NP, e)
            hist[t + 1] = (s * _expand(dec_ref[0, pl.ds(row, 1), :], NP) + sa * _expand(bb_ref[0, pl.ds(row, 1), :], NP)
                           + vc * _expand(kd_ref[0, pl.ds(row, 1), :], NP))
            return carry

        hist[0] = ck_ref[0, 0]
        lax.fori_loop(0, SCAN_SUB, fwd, 0)

        def bwd(tt, carry):
            t = SCAN_SUB - 1 - tt
            row = local_row(t)
            sp = hist[t]
            st = hist[t + 1]
            a_ = _expand(-kk_ref[pl.ds(row, 1), :], NP)
            b_ = _expand(bb_ref[0, pl.ds(row, 1), :], NP)
            k_ = _expand(kd_ref[0, pl.ds(row, 1), :], NP)
            sa = _segb(sp * a_, e)
            vc = _colb(v_ref[pl.ds(row, 1), :], NP, e)
            dyc = _colb(dy_ref[pl.ds(row, 1), :], NP, e)
            ds = ds_ref[...] + dyc * _expand(r_ref[pl.ds(row, 1), :], NP)
            dr_o[0, pl.ds(row, 1), :] = _pair_colsum(st * dyc, NP)
            dw_o[0, pl.ds(row, 1), :] = _pair_colsum(ds * sp, NP)
            dsa = _segb(ds * b_, e)
            db_o[0, pl.ds(row, 1), :] = _pair_colsum(ds * sa, NP)
            dvb = _segb1(ds * k_, e)
            dv_o[0, pl.ds(row, 1), :] = _pair_colsum(_diag(dvb, NP), NP)
            dk_o[0, pl.ds(row, 1), :] = _pair_colsum(ds * vc, NP)
            da_o[0, pl.ds(row, 1), :] = _pair_colsum(sp * dsa, NP)
            ds_ref[...] = ds * _expand(dec_ref[0, pl.ds(row, 1), :], NP) + dsa * a_
            return carry

        lax.fori_loop(0, SCAN_SUB, bwd, 0)

    sh = _bs((SCAN_SUB, T.BW), lambda d, g: (tmap(d, g), 0))
    dr = _bs((1, SCAN_SUB, T.BW), lambda d, g: (d, tmap(d, g), 0))
    o2 = jax.ShapeDtypeStruct((2, T.TTOT, T.BW), F32)
    return pl.pallas_call(
        kern, grid=(2, NS), name="scan_bwd",
        in_specs=[sh, sh, sh, sh, dr, dr, dr, _bs((1, 1, NP * HS, 128), lambda d, g: (d, NS - 1 - g, 0, 0))],
        out_specs=[dr] * 6,
        out_shape=[o2] * 6,
        scratch_shapes=[pltpu.VMEM((SCAN_SUB + 1, NP * HS, 128), F32), pltpu.VMEM((NP * HS, 128), F32)],
        compiler_params=_cp(("arbitrary", "arbitrary"), mb=56),
    )(dy, r, v, kk, dec, kd, bb, ck)


def _prep_bwd(T, rw, dr_s, ddec, dkd, dv_s, da_s, dbb, dr_ro, dv_ro, dkbar, dgd, w0, w2, a0, a2, k_k, k_a):
    def kern(rw_ref, drs_ref, ddec_ref, dkd_ref, dvs_ref, das_ref, dbb_ref, drr_ref, dvr_ref, dkb_ref, dgd_ref,
             w0_ref, w2_ref, a0_ref, a2_ref, kk_ref, ka_ref,
             drw_o, dw0_o, dw2_o, da0_o, da2_o, dkk_o, dka_o):
        i = pl.program_id(0)
        first = i == 0
        e = _e128(F32)
        w0v, w2v, a0v, a2v = _load_prep_params(w0_ref, w2_ref, a0_ref, a2_ref)
        k_k = kk_ref[...]
        k_a = ka_ref[...]
        o = _prep_math(rw_ref[...], w0v, w2v, a0v, a2v, k_k, k_a, e)
        k, kk = o["k"], o["kk"]
        dkbh = 0.5 * dkb_ref[...]
        dk = jnp.zeros_like(k)
        dkk = -(das_ref[0] + das_ref[1])
        dka = jnp.zeros((1, W), F32)
        dwd, dad = [], []
        for d in (0, 1):
            iclr = o["iclr"][d]
            dkd_d = dkd_ref[d] + dkbh
            dbb_d = dbb_ref[d]
            dk = dk + dkd_d * (1.0 + (iclr - 1.0) * k_a)
            dka = dka + _colsum(dkd_d * k * (iclr - 1.0))
            dkk = dkk + dbb_d * iclr
            dicl = dkd_d * k * k_a + dbb_d * kk
            dpa = dicl * iclr * (1.0 - iclr)
            _acc(da0_o.at[d:d + 1, :], _colsum(dpa), first)
            dad.append(lax.dot_general(dpa, a2v[d], _NT_DIMS, precision=HI, preferred_element_type=F32))
            _acc(da2_o.at[d], lax.dot_general(o["ad"][d], dpa, _TN_DIMS, precision=HI, preferred_element_type=F32),
                 first)
            dpre = -ddec_ref[d] * o["dec"][d] * o["ex"][d] * _sigmoid(-o["pre"][d])
            _acc(dw0_o.at[d:d + 1, :], _colsum(dpre), first)
            th = o["th"][d]
            dth = lax.dot_general(dpre, w2v[d], _NT_DIMS, precision=HI, preferred_element_type=F32)
            _acc(dw2_o.at[d], lax.dot_general(th, dpre, _TN_DIMS, precision=HI, preferred_element_type=F32), first)
            dwd.append(dth * (1.0 - th * th))
        inv = o["inv"]
        kr = o["kr"]
        proj = _segsum(dkk * kr, e)
        dkr = dkk * inv - jnp.where(o["rt"] > 1e-12, kr * inv * inv * inv * proj, 0.0)
        dk = dk + dkr * k_k
        _acc(dkk_o, _colsum(dkr * k), first)
        _acc(dka_o, dka, first)
        dr = drs_ref[0] + drs_ref[1] + drr_ref[...]
        dv = dvs_ref[0] + dvs_ref[1] + dvr_ref[...]
        drw_o[...] = jnp.concatenate([dr, dk, dv, dwd[0], dwd[1], dad[0], dad[1], dgd_ref[...]], axis=1)

    tok = lambda i: (i, 0)
    f32s = lambda *s: jax.ShapeDtypeStruct(s, F32)
    p2 = lambda i: (0, 0)
    p3 = lambda i: (0, 0, 0)
    return pl.pallas_call(
        kern, grid=(T.NT,), name="prep_bwd",
        in_specs=[_bs((TT, RWC), tok)] + [T.tm2_spec()] * 6 + [_bs((TT, W), tok)] * 3 + [_bs((TT, GDW), tok)]
        + _prep_param_specs(),
        out_specs=[_bs((TT, RWC), tok), _bs((2, W), p2), _bs((2, LRW, W), p3), _bs((2, W), p2),
                   _bs((2, LRW, W), p3), _row(W), _row(W)],
        out_shape=[f32s(T.NTOK, RWC), f32s(2, W), f32s(2, LRW, W), f32s(2, W), f32s(2, LRW, W), f32s(1, W), f32s(1, W)],
        compiler_params=_cp(("arbitrary",), mb=56),
    )(rw, dr_s, ddec, dkd, dv_s, da_s, dbb, dr_ro, dv_ro, dkbar, dgd, w0, w2, a0, a2, k_k, k_a)


def _shift_bwd(T, drw, p_rw, mu_p, mu_n):
    def kern(d_ref, dp_ref, dn_ref, p_ref, pp_ref, pn_ref, mp_ref, mn_ref, dprw_o, dmp_o, dmn_o):
        i = pl.program_id(0)
        first = i == 0
        has_prev, has_next = _halo_masks(T, i)
        mp = mp_ref[...]
        mn = mn_ref[...]
        drw = d_ref[...]
        z = p_ref[...]
        zprev, znext = _neighbours(z, pp_ref[7:8, :] * has_prev, pn_ref[0:1, :] * has_next)
        _acc(dmp_o, _colsum(drw * (zprev - z)), first)
        _acc(dmn_o, _colsum(drw * (znext - z)), first)
        dprev, dnext = _neighbours(drw, dp_ref[7:8, :] * has_prev, dn_ref[0:1, :] * has_next)
        dprw_o[...] = (drw * (1.0 - mp - mn) + mp * dnext + mn * dprev).astype(BF16)

    tok = lambda i: (i, 0)
    prev, nxt = _halo_specs(T)
    f32s = lambda *s: jax.ShapeDtypeStruct(s, F32)
    return pl.pallas_call(
        kern, grid=(T.NT,), name="shift_bwd",
        in_specs=[_bs((TT, RWC), tok), prev, nxt, _bs((TT, RWC), tok), prev, nxt, _row(RWC), _row(RWC)],
        out_specs=[_bs((TT, RWC), tok), _row(RWC), _row(RWC)],
        out_shape=[jax.ShapeDtypeStruct((T.NTOK, RWC), BF16), f32s(1, RWC), f32s(1, RWC)],
        compiler_params=_cp(("arbitrary",), mb=56),
    )(drw, drw, drw, p_rw, p_rw, p_rw, mu_p, mu_n)


def _mix_in_bwd(T, dp_rw, dp_cv, xc, dx1, modrows, g, w_rw, w_cv):
    def kern(drw_ref, dcv_ref, x_ref, dx1_ref, mod_ref, g_ref, wr_ref, wc_ref, dxc_o, dmod_o, dg_o):
        i = pl.program_id(0)
        dh = (lax.dot_general(drw_ref[...], wr_ref[...], _NT_DIMS, preferred_element_type=F32)
              + lax.dot_general(dcv_ref[...], wc_ref[...], _NT_DIMS, preferred_element_type=F32))
        x = x_ref[...]
        s = lax.rsqrt(_rowmean(x * x) + EPS_RMS)
        xh = x * s
        gv = g_ref[...]
        q = i % T.TPS
        first_kind = jnp.logical_or(q == 0, q == 1)
        _acc(dmod_o.at[0, 0, 0:1, :], _colsum(dh), first_kind)
        _acc(dmod_o.at[0, 0, 1:2, :], _colsum(dh * (xh * gv)), first_kind)
        dn1 = dh * (1.0 + mod_ref[0, 0, 1:2, :])
        _acc(dg_o, _colsum(dn1 * xh), i == 0)
        dxh = dn1 * gv
        dxc_o[...] = dx1_ref[...] + s * (dxh - xh * _rowmean(dxh * xh))

    tok = lambda i: (i, 0)
    f32s = lambda *s: jax.ShapeDtypeStruct(s, F32)
    return pl.pallas_call(
        kern, grid=(T.NT,), name="mix_in_bwd",
        in_specs=[_bs((TT, RWC), tok), _bs((TT, CVC), tok), _bs((TT, D), tok), _bs((TT, D), tok), T.mod_spec(),
                  _row(D), _bs((D, RWC), lambda i: (0, 0)), _bs((D, CVC), lambda i: (0, 0))],
        out_specs=[_bs((TT, D), tok),
                   _bs((1, 1, 2, D), lambda i: (i // T.TPS, jnp.minimum(i % T.TPS, 1), 0, 0)), _row(D)],
        out_shape=[f32s(T.NTOK, D), f32s(T.B, 2, 2, D), f32s(1, D)],
        compiler_params=_cp(("arbitrary",)),
    )(dp_rw, dp_cv, xc, dx1, modrows, g, w_rw, w_cv)


def _matmul_tn(a, b, name, tk, nk, tn, amap=None, bmap=None, tm=1024):
    M = a.shape[1]
    N = b.shape[1]
    amap = amap or (lambda k: k)
    bmap = bmap or (lambda k: k)

    def kern(a_ref, b_ref, o_ref):
        _acc(o_ref, lax.dot_general(a_ref[...], b_ref[...], _TN_DIMS, preferred_element_type=F32),
             pl.program_id(2) == 0)

    return pl.pallas_call(
        kern, grid=(M // tm, N // tn, nk), name=name,
        in_specs=[_bs((tk, tm), lambda i, j, k: (amap(k), i)), _bs((tk, tn), lambda i, j, k: (bmap(k), j))],
        out_specs=_bs((tm, tn), lambda i, j, k: (i, j)),
        out_shape=jax.ShapeDtypeStruct((M, N), F32),
        compiler_params=_cp(("parallel", "parallel", "arbitrary")),
    )(a, b)


def _silu(x):
    return x * _sigmoid(x)


def _ada_fwd(c_all, c_ctx, ada_w, ada_b_blk):
    nb = c_all.shape[0]
    R = nb + 8
    ncol = ada_w.shape[1]

    def kern(c_ref, cc_ref, w_ref, b_ref, o_ref):
        lhs = jnp.concatenate([_silu(c_ref[...]), _silu(cc_ref[...]), jnp.zeros((7, D), F32)], axis=0)
        o_ref[...] = jnp.dot(lhs, w_ref[...], precision=HI, preferred_element_type=F32) + b_ref[...]

    return pl.pallas_call(
        kern, name="ada_fwd", out_shape=jax.ShapeDtypeStruct((R, ncol), F32),
        compiler_params=_cp(None, 40),
    )(c_all, c_ctx, ada_w, ada_b_blk)


def _ada_bwd(c_all, c_ctx, ada_w, ex, cx, ex_blk, cx_blk):
    nb = c_all.shape[0]
    ncol = ada_w.shape[1]

    def kern(c_ref, cc_ref, w_ref, ex_ref, cx_ref, exb_ref, cxb_ref, gw_o, gb_o, ds_o):
        lhs = jnp.concatenate([_silu(c_ref[...]), _silu(cc_ref[...]), jnp.zeros((7, D), F32)], axis=0)
        dmc_blk = _colsum(cxb_ref[...])
        rhs = jnp.concatenate([exb_ref[...], dmc_blk, jnp.zeros((7, ncol), F32)], axis=0)
        gw_o[...] = lax.dot_general(lhs, rhs, _TN_DIMS, precision=HI, preferred_element_type=F32)
        gb_o[...] = _colsum(ex_ref[...]) + _colsum(cx_ref[...])
        ds_o[...] = lax.dot_general(jnp.concatenate([dmc_blk, jnp.zeros((7, ncol), F32)], axis=0), w_ref[...],
                                    _NT_DIMS, precision=HI, preferred_element_type=F32)

    return pl.pallas_call(
        kern, name="ada_bwd",
        out_shape=[jax.ShapeDtypeStruct((D, ncol), F32), jax.ShapeDtypeStruct((1, ex.shape[1]), F32),
                   jax.ShapeDtypeStruct((8, D), F32)],
        compiler_params=_cp(None, 48),
    )(c_all, c_ctx, ada_w, ex, cx, ex_blk, cx_blk)


def _cctx_final(parts, c_ctx):
    def kern(p_ref, c_ref, o_ref):
        tot = p_ref[0, 0:1, :]
        for j in range(1, parts.shape[0]):
            tot = tot + p_ref[j, 0:1, :]
        c = c_ref[...]
        sg = _sigmoid(c)
        o_ref[...] = tot * (sg * (1.0 + c * (1.0 - sg)))

    return pl.pallas_call(kern, name="cctx_final", out_shape=jax.ShapeDtypeStruct((1, D), F32))(parts, c_ctx)


def _peer(kind, p, ix, iy, ic):
    if kind == "chips":
        return (p // 2, p % 2, ic)
    if kind == "all":
        return (p // 4, (p // 2) % 2, p % 2)
    return (ix, iy, p)


def _exchange(x, kind, bcast, name, chunks=1):
    npeer = {"chips": 4, "all": 8, "sib": 2}[kind]
    slab = x.shape if bcast else x.shape[1:]
    assert chunks == 1 or slab[0] == chunks

    def kern(x_ref, o_ref, send_sems, recv_sems, lsem):
        ix, iy, ic = lax.axis_index("x"), lax.axis_index("y"), lax.axis_index("c")
        me = {"chips": 2 * ix + iy, "all": 4 * ix + 2 * iy + ic, "sib": ic}[kind]
        own = pltpu.make_async_copy(x_ref if bcast else x_ref.at[me], o_ref.at[me], lsem)
        own.start()

        def part(ref, k):
            return ref if chunks == 1 else ref.at[k]

        def copy(p, k):
            return pltpu.make_async_remote_copy(
                src_ref=part(x_ref if bcast else x_ref.at[p], k), dst_ref=part(o_ref.at[me], k),
                send_sem=send_sems.at[p, k], recv_sem=recv_sems.at[me, k],
                device_id=_peer(kind, p, ix, iy, ic), device_id_type=MESH)

        def arrival(p, k):
            return pltpu.make_async_remote_copy(
                src_ref=part(x_ref if bcast else x_ref.at[p], k), dst_ref=part(o_ref.at[p], k),
                send_sem=send_sems.at[p, k], recv_sem=recv_sems.at[p, k],
                device_id=_peer(kind, p, ix, iy, ic), device_id_type=MESH)

        for p in range(npeer):
            @pl.when(me != p)
            def _():
                for k in range(chunks):
                    copy(p, k).start()
        for p in range(npeer):
            @pl.when(me != p)
            def _():
                for k in range(chunks):
                    arrival(p, k).wait_recv()
        for p in range(npeer):
            @pl.when(me != p)
            def _():
                for k in range(chunks):
                    copy(p, k).wait_send()
        own.wait()

    any_spec = pl.BlockSpec(memory_space=pl.ANY)
    return pl.pallas_call(
        kern, name=name, in_specs=[any_spec], out_specs=any_spec,
        out_shape=jax.ShapeDtypeStruct((npeer,) + tuple(slab), x.dtype),
        scratch_shapes=[pltpu.SemaphoreType.DMA((npeer, chunks)), pltpu.SemaphoreType.DMA((npeer, chunks)),
                        pltpu.SemaphoreType.DMA],
    )(x)


def _sum_slots(x, name):
    n, R, C = x.shape
    budget = (8 << 20) // (n * C * x.dtype.itemsize)
    tr = max([t for t in range(8, R + 1, 8) if R % t == 0 and t <= max(budget, 8)], default=R)

    def kern(x_ref, o_ref):
        tot = x_ref[0]
        for s in range(1, n):
            tot = tot + x_ref[s]
        o_ref[...] = tot

    return pl.pallas_call(
        kern, grid=(R // tr,), name=name,
        in_specs=[_bs((n, tr, C), lambda i: (0, i, 0))], out_specs=_bs((tr, C), lambda i: (i, 0)),
        out_shape=jax.ShapeDtypeStruct((R, C), x.dtype), compiler_params=_cp(("parallel",)),
    )(x)


def _adamw(w, g, m, v, name):
    shape = w.shape
    last = shape[-1]
    rows = 1
    for s in shape[:-1]:
        rows *= s
    w2, g2, m2, v2 = (t.reshape(rows, last) for t in (w, g, m, v))
    tr = rows
    for cand in (256, 128, 64, 32, 16, 8):
        if rows > cand and rows % cand == 0 and rows * last > (1 << 18):
            tr = cand
            break
    c1 = 1.0 - ADAM_B1 ** ADAM_STEP
    c2 = 1.0 - ADAM_B2 ** ADAM_STEP

    def kern(w_ref, g_ref, m_ref, v_ref, d_o, m_o, v_o):
        gv = g_ref[...]
        mn = ADAM_B1 * m_ref[...] + (1.0 - ADAM_B1) * gv
        vn = ADAM_B2 * v_ref[...] + (1.0 - ADAM_B2) * (gv * gv)
        m_o[...] = mn
        v_o[...] = vn
        d_o[...] = -ADAM_LR * ((mn / c1) / (jnp.sqrt(vn / c2) + ADAM_EPS) + ADAM_WD * w_ref[...])

    spec = _bs((tr, last), lambda i: (i, 0))
    o = jax.ShapeDtypeStruct((rows, last), F32)
    d, mn, vn = pl.pallas_call(
        kern, grid=(rows // tr,), name=name, in_specs=[spec] * 4, out_specs=[spec] * 3, out_shape=[o, o, o],
        compiler_params=_cp(("parallel",)),
    )(w2, g2, m2, v2)
    return d.reshape(shape), mn.reshape(shape), vn.reshape(shape)


_WEIGHT_NAMES = ("c_ctx", "ada_w", "ada_b", "mix_pre_g", "mix_post_g", "mlp_pre_g", "mlp_post_g", "w_in", "mu_prev",
                 "mu_next", "decay_w0", "decay_w2", "iclr_a0", "iclr_a2", "k_k", "k_a", "r_k", "gate_w2", "lnx_w",
                 "lnx_b", "conv_w", "conv_b", "conv_ln_w", "conv_ln_b", "w_out", "mlp_w1", "mlp_w2")


def _pack_rows(parts, cols=512):
    flat = jnp.concatenate([p.reshape(-1) for p in parts])
    rows = -(-flat.shape[0] // cols)
    rows = -(-rows // 16) * 16
    flat = jnp.pad(flat, (0, rows * cols - flat.shape[0]))
    return flat.reshape(rows, cols)


def _unpack(flat, shapes):
    out = []
    off = 0
    for s in shapes:
        n = 1
        for d in s:
            n *= d
        out.append(flat[off:off + n].reshape(s))
        off += n
    return out


def _local_step(T, xc, tgt, modrows, P):
    p_rw, p_cv, h = _mix_in(T, xc, modrows, P["mix_pre_g"], P["w_rw"], P["w_cv"])
    prep_params = (P["w0"], P["w2"], P["a0"], P["a2"], P["k_k"], P["k_a"])
    r, v, kk, dec, kd, bb, rw = _rwkv_prep(T, p_rw, P["mu_p"], P["mu_n"], *prep_params)
    y, ck = _scan_fwd(T, r, v, kk, dec, kd, bb)
    ro_params = (P["r_k"], P["gate_w2"], P["lnx_w"], P["lnx_b"])
    cv_params = (P["conv_w"], P["conv_b"], P["conv_ln_w"], P["conv_ln_b"])
    cat, mix, x1 = _mix_out(T, y, kd, rw, p_cv, xc, modrows, *ro_params, *cv_params, P["mix_post_g"], P["w_out"])
    m, h2 = _mlp_fwd(T, x1, modrows, P["mlp_pre_g"], P["w1"], P["w2m"])
    loss_acc, dm, dx2, dg2, d_mlp_post = _loss_head(T, m, x1, tgt, modrows, P["mlp_post_g"])
    fact, da, dh2 = _mlp_bwd(T, h2, dm, P["w1"], P["w2m"])
    dx1, dmod2, d_mlp_pre = _mlp_in_bwd(T, dh2, x1, dx2, modrows, P["mlp_pre_g"])
    dmix, dcat, dg1, d_mix_post = _mix_post_bwd(T, dx1, mix, modrows, P["mix_post_g"], P["w_out"])
    dp_cv, d_conv_w, d_conv_b, d_cln_w, d_cln_b = _conv_bwd(T, dcat, p_cv, *cv_params)
    dy, dr_ro, dv_ro, dkbar, dgd, d_r_k, d_gate, d_lnx_w, d_lnx_b = _readout_bwd(T, dcat, y, kd, rw, *ro_params)
    dr_s, ddec, dkd, dv_s, da_s, dbb = _scan_bwd(T, dy, r, v, kk, dec, kd, bb, ck)
    drw, d_w0, d_w2, d_a0, d_a2, d_k_k, d_k_a = _prep_bwd(T, rw, dr_s, ddec, dkd, dv_s, da_s, dbb, dr_ro, dv_ro,
                                                          dkbar, dgd, *prep_params)
    dp_rw, d_mu_p, d_mu_n = _shift_bwd(T, drw, p_rw, P["mu_p"], P["mu_n"])
    dxc, dmod1, d_mix_pre = _mix_in_bwd(T, dp_rw, dp_cv, xc, dx1, modrows, P["mix_pre_g"], P["w_rw"], P["w_cv"])
    nk = T.NTOK // TT
    dw_rw = _matmul_tn(h, dp_rw, "dw_in_rw", TT, nk, 768)
    dw_cv = _matmul_tn(h, dp_cv, "dw_in_cv", TT, nk, 1024)
    dw_out = _matmul_tn(cat, dmix, "dw_out", TT, T.NL, 1024, bmap=T.tok)
    dw1 = _matmul_tn(h2, da, "dw_mlp1", 512, T.NLAT // 512, 1024)
    dw2m = _matmul_tn(fact, dm, "dw_mlp2", 512, T.NLAT // 512, 1024)
    small = dict(mix_pre_g=d_mix_pre, mix_post_g=d_mix_post, mlp_pre_g=d_mlp_pre, mlp_post_g=d_mlp_post,
                 mu_p=d_mu_p, mu_n=d_mu_n, w0=d_w0, w2=d_w2, a0=d_a0, a2=d_a2, k_k=d_k_k, k_a=d_k_a, r_k=d_r_k,
                 gate_w2=d_gate, lnx_w=d_lnx_w, lnx_b=d_lnx_b, conv_w=d_conv_w, conv_b=d_conv_b,
                 conv_ln_w=d_cln_w, conv_ln_b=d_cln_b)
    big = dict(w_rw=dw_rw, w_cv=dw_cv, w_out=dw_out, w1=dw1, w2m=dw2m)
    dmods = dict(dmod1=dmod1, dg1=dg1, dmod2=dmod2, dg2=dg2)
    return loss_acc[0, 0], dxc, small, big, dmods


_SMALL_ORDER = ("mix_pre_g", "mix_post_g", "mlp_pre_g", "mlp_post_g", "mu_p", "mu_n", "w0", "w2", "a0", "a2", "k_k",
                "k_a", "r_k", "gate_w2", "lnx_w", "lnx_b", "conv_w", "conv_b", "conv_ln_w", "conv_ln_b")


def kernel(x, c, ctx, c_ctx, ada_w, ada_b, mix_pre_g, mix_post_g, mlp_pre_g, mlp_post_g, w_in, mu_prev, mu_next, decay_w0, decay_w2, iclr_a0, iclr_a2, k_k, k_a, r_k, gate_w2, lnx_w, lnx_b, conv_w, conv_b, conv_ln_w, conv_ln_b, w_out, mlp_w1, mlp_w2, loss_target, m_c_ctx, m_ada_w, m_ada_b, m_mix_pre_g, m_mix_post_g, m_mlp_pre_g, m_mlp_post_g, m_w_in, m_mu_prev, m_mu_next, m_decay_w0, m_decay_w2, m_iclr_a0, m_iclr_a2, m_k_k, m_k_a, m_r_k, m_gate_w2, m_lnx_w, m_lnx_b, m_conv_w, m_conv_b, m_conv_ln_w, m_conv_ln_b, m_w_out, m_mlp_w1, m_mlp_w2, v_c_ctx, v_ada_w, v_ada_b, v_mix_pre_g, v_mix_post_g, v_mlp_pre_g, v_mlp_post_g, v_w_in, v_mu_prev, v_mu_next, v_decay_w0, v_decay_w2, v_iclr_a0, v_iclr_a2, v_k_k, v_k_a, v_r_k, v_gate_w2, v_lnx_w, v_lnx_b, v_conv_w, v_conv_b, v_conv_ln_w, v_conv_ln_b, v_w_out, v_mlp_w1, v_mlp_w2):
    weights = dict(zip(_WEIGHT_NAMES, (c_ctx, ada_w, ada_b, mix_pre_g, mix_post_g, mlp_pre_g, mlp_post_g, w_in, mu_prev, mu_next, decay_w0, decay_w2, iclr_a0, iclr_a2, k_k, k_a, r_k, gate_w2, lnx_w, lnx_b, conv_w, conv_b, conv_ln_w, conv_ln_b, w_out, mlp_w1, mlp_w2)))
    moms = dict(zip(_WEIGHT_NAMES, (m_c_ctx, m_ada_w, m_ada_b, m_mix_pre_g, m_mix_post_g, m_mlp_pre_g, m_mlp_post_g, m_w_in, m_mu_prev, m_mu_next, m_decay_w0, m_decay_w2, m_iclr_a0, m_iclr_a2, m_k_k, m_k_a, m_r_k, m_gate_w2, m_lnx_w, m_lnx_b, m_conv_w, m_conv_b, m_conv_ln_w, m_conv_ln_b, m_w_out, m_mlp_w1, m_mlp_w2)))
    vars_ = dict(zip(_WEIGHT_NAMES, (v_c_ctx, v_ada_w, v_ada_b, v_mix_pre_g, v_mix_post_g, v_mlp_pre_g, v_mlp_post_g, v_w_in, v_mu_prev, v_mu_next, v_decay_w0, v_decay_w2, v_iclr_a0, v_iclr_a2, v_k_k, v_k_a, v_r_k, v_gate_w2, v_lnx_w, v_lnx_b, v_conv_w, v_conv_b, v_conv_ln_w, v_conv_ln_b, v_w_out, v_mlp_w1, v_mlp_w2)))

    B, t_lat, _ = x.shape
    assert ctx.shape[1] == TT and t_lat % TT == 0 and (t_lat * B) % MT == 0
    T = _Tiles(B, t_lat)
    ix, iy, ic = lax.axis_index("x"), lax.axis_index("y"), lax.axis_index("c")
    chip = 2 * ix + iy
    dev = 4 * ix + 2 * iy + ic
    nsh = 4
    in_sh = w_in.shape[2]
    ada_sh = ada_w.shape[2]
    lane_sh = decay_w0.shape[2]

    big_parts = (w_in[0], w_out[0], mlp_w1[0], mlp_w2[0])
    big_shapes = [p.shape for p in big_parts]
    wg = _exchange(_pack_rows([p.astype(BF16) for p in big_parts]), "chips", True, "gather_big_weights")
    per = [_unpack(wg[j].reshape(-1), big_shapes) for j in range(nsh)]
    w_in_f = jnp.concatenate([per[j][0] for j in range(nsh)], axis=1)
    w_out_f = jnp.concatenate([per[j][1] for j in range(nsh)], axis=0)
    w1_f = jnp.concatenate([per[j][2] for j in range(nsh)], axis=1)
    w2_f = jnp.concatenate([per[j][3] for j in range(nsh)], axis=0)
    w_in_p = _pad_cols(w_in_f, w_in_f.shape[1])

    sm_parts = (decay_w0[0], decay_w2[0], iclr_a0[0], iclr_a2[0], gate_w2[0], conv_w[0])
    sm_shapes = [p.shape for p in sm_parts]
    sg = _exchange(_pack_rows(sm_parts), "chips", True, "gather_small_weights")
    pers = [_unpack(sg[j].reshape(-1), sm_shapes) for j in range(nsh)]
    w0_f, w2_f_, a0_f, a2_f, gate_f, convw_f = (jnp.concatenate([pers[j][t] for j in range(nsh)], axis=-1)
                                                for t in range(6))

    def pad_rows(a, n):
        return jnp.pad(a, [(0, 0)] * (a.ndim - 2) + [(0, n - a.shape[-2]), (0, 0)])

    P = dict(
        w_rw=w_in_p[:, :RWC], w_cv=w_in_p[:, RWC:], w_out=w_out_f, w1=w1_f, w2m=w2_f,
        mix_pre_g=mix_pre_g, mix_post_g=mix_post_g, mlp_pre_g=mlp_pre_g, mlp_post_g=mlp_post_g,
        mu_p=_pad_cols(mu_prev, mu_prev.shape[1]), mu_n=_pad_cols(mu_next, mu_next.shape[1]),
        w0=w0_f, w2=pad_rows(w2_f_, LRW), a0=a0_f, a2=pad_rows(a2_f, LRW), k_k=k_k, k_a=k_a,
        r_k=r_k.reshape(1, W), gate_w2=pad_rows(gate_f, GDW), lnx_w=lnx_w, lnx_b=lnx_b,
        conv_w=pad_rows(convw_f, 32), conv_b=conv_b, conv_ln_w=conv_ln_w, conv_ln_b=conv_ln_b)

    c_ctx2 = c_ctx.reshape(1, D)
    c_all = _exchange(jnp.pad(c, ((0, 8 - B), (0, 0))), "all", True, "gather_c")[:, :B].reshape(8 * B, D)
    ada_b_blk = lax.dynamic_slice(ada_b, (0, chip * ada_sh), (1, ada_sh))
    mod_blk = _ada_fwd(c_all, c_ctx2, ada_w[0], ada_b_blk)
    mod_g = _exchange(mod_blk, "chips", True, "gather_mod")
    mod_all = jnp.concatenate([mod_g[j] for j in range(nsh)], axis=1)
    mod_x = lax.dynamic_slice(mod_all, (dev * B, 0), (B, 6 * D)).reshape(B, 6, D)
    mod_c = jnp.broadcast_to(mod_all[8 * B].reshape(1, 6, D), (B, 6, D))
    modrows = jnp.stack([mod_c, mod_x], axis=1)

    xc = jnp.concatenate([ctx, x], axis=1).reshape(T.NTOK, D)
    tgt = loss_target.reshape(T.NLAT, D)
    loss_loc, dxc, small, big, dm_ = _local_step(T, xc, tgt, modrows, P)
    loss = lax.psum(loss_loc, ("x", "y", "c"))
    grad_x = dxc.reshape(B, T.TTOT, D)[:, TT:, :]

    dmod_x = jnp.concatenate([dm_["dmod1"][:, 1], dm_["dg1"], dm_["dmod2"], dm_["dg2"]], axis=1)
    dmod_c = jnp.concatenate([dm_["dmod1"][:, 0], jnp.zeros((B, 4, D), F32)], axis=1)
    dpack = jnp.concatenate([dmod_x.reshape(B, 6 * D), dmod_c.reshape(B, 6 * D)], axis=0)
    dg = _exchange(dpack, "all", True, "gather_dmod")
    ex = dg[:, :B].reshape(8 * B, 6 * D)
    cx = dg[:, B:].reshape(8 * B, 6 * D)
    ex_blk = lax.dynamic_slice(ex, (0, chip * ada_sh), (8 * B, ada_sh))
    cx_blk = lax.dynamic_slice(cx, (0, chip * ada_sh), (8 * B, ada_sh))
    g_ada_w, g_ada_b, dscc = _ada_bwd(c_all, c_ctx2, ada_w[0], ex, cx, ex_blk, cx_blk)
    dscc_g = _exchange(dscc, "chips", True, "gather_dcctx")
    g_c_ctx = _cctx_final(dscc_g, c_ctx2).reshape(D)

    small = dict(small, mu_p=_unpad_cols(small["mu_p"], mu_prev.shape[1]),
                 mu_n=_unpad_cols(small["mu_n"], mu_next.shape[1]),
                 w2=small["w2"][:, :decay_w2.shape[2]], a2=small["a2"][:, :iclr_a2.shape[2]],
                 gate_w2=small["gate_w2"][:gate_w2.shape[1]], conv_w=small["conv_w"][:KCONV])
    sm_list = [small[n] for n in _SMALL_ORDER]
    sm_shapes2 = [a.shape for a in sm_list]
    sm_pair = _sum_slots(_exchange(_pack_rows(sm_list), "sib", True, "swap_small_grads"), "sum_small_sib")
    sm_tot = _sum_slots(_exchange(sm_pair, "chips", True, "gather_small_grads"), "sum_small_grads")
    S = dict(zip(_SMALL_ORDER, _unpack(sm_tot.reshape(-1), sm_shapes2)))

    def shard_last(a):
        return lax.dynamic_slice_in_dim(a, chip * lane_sh, lane_sh, axis=a.ndim - 1)

    grads = dict(
        c_ctx=g_c_ctx, ada_w=g_ada_w[None], ada_b=g_ada_b,
        mix_pre_g=S["mix_pre_g"], mix_post_g=S["mix_post_g"], mlp_pre_g=S["mlp_pre_g"], mlp_post_g=S["mlp_post_g"],
        mu_prev=S["mu_p"], mu_next=S["mu_n"],
        decay_w0=shard_last(S["w0"])[None], decay_w2=shard_last(S["w2"])[None],
        iclr_a0=shard_last(S["a0"])[None], iclr_a2=shard_last(S["a2"])[None],
        k_k=S["k_k"], k_a=S["k_a"], r_k=S["r_k"].reshape(r_k.shape),
        gate_w2=shard_last(S["gate_w2"])[None], lnx_w=S["lnx_w"], lnx_b=S["lnx_b"],
        conv_w=shard_last(S["conv_w"])[None], conv_b=S["conv_b"], conv_ln_w=S["conv_ln_w"],
        conv_ln_b=S["conv_ln_b"])

    dw_in_f = _unpad_cols(jnp.concatenate([big["w_rw"], big["w_cv"]], axis=1), w_in_f.shape[1])
    oshape = w_out.shape[1]
    mshape = mlp_w1.shape[2]
    slabs = [
        _pack_rows([dw_in_f[:, in_sh * j:in_sh * (j + 1)], big["w_out"][oshape * j:oshape * (j + 1)],
                    big["w1"][:, mshape * j:mshape * (j + 1)], big["w2m"][mshape * j:mshape * (j + 1)]])
        for j in range(nsh)]
    nck = 3
    lr, lc = slabs[0].shape
    hr = lr // 2
    cr = hr // nck
    assert cr * 2 * nck == lr and cr % 8 == 0
    halves = jnp.stack([jnp.concatenate([s[h * hr:(h + 1) * hr].reshape(nck, cr, lc) for s in slabs], axis=0)
                        for h in (0, 1)])
    pair = _exchange(halves, "sib", False, "reduce_sib_grads", chunks=nsh * nck)
    pair = _sum_slots(pair.reshape(2, nsh * nck * cr, lc), "sum_sib_grads")
    mine = _sum_slots(_exchange(pair.reshape(nsh, nck * cr, lc), "chips", False, "reduce_big_grads"), "sum_big_grads")
    tot = _exchange(mine.reshape(nck, cr, lc), "sib", True, "swap_big_grads", chunks=nck)
    g_w_in, g_w_out, g_w1, g_w2 = _unpack(tot.reshape(-1), big_shapes)
    grads.update(w_in=g_w_in[None], w_out=g_w_out[None], mlp_w1=g_w1[None], mlp_w2=g_w2[None])

    deltas, new_m, new_v = {}, {}, {}
    for n in _WEIGHT_NAMES:
        g = grads[n].reshape(weights[n].shape)
        grads[n] = g
        deltas[n], new_m[n], new_v[n] = _adamw(weights[n], g, moms[n], vars_[n], "adamw_" + n)

    return (loss, grad_x, *[grads[n] for n in _WEIGHT_NAMES], *[deltas[n] for n in _WEIGHT_NAMES],
            *[new_m[n] for n in _WEIGHT_NAMES], *[new_v[n] for n in _WEIGHT_NAMES])
```

```python
import functools

import jax
import jax.numpy as jnp
from jax import lax
from jax.experimental import pallas as pl
from jax.experimental.pallas import tpu as pltpu

F32 = jnp.float32
BF16 = jnp.bfloat16
HI = lax.Precision.HIGHEST

D = 1024
W = 512
HS = 64
RWC = 2304
CVC = 1024
GDW = 256
LRW = 128
DFF = 4096
TT = 256
LINE = 64
KCONV = 31
EPS_RMS = 1e-6
EPS_LN = 1e-5
EPS_GN = 64e-5
SCAN_CH = 128
SCAN_SUB = 32
SCAN_UNROLL = 8

ADAM_LR = 0.001
ADAM_B1 = 0.9
ADAM_B2 = 0.999
ADAM_EPS = 1e-08
ADAM_WD = 0.01
ADAM_STEP = 10

_SEGS = ((0, 1536, 1536), (1536, 64, 128), (1600, 64, 128), (1664, 64, 128), (1728, 64, 128),
         (1792, 160, 256), (1952, 1024, 1024))

MESH = pl.DeviceIdType.MESH


def _bs(shape, imap):
    return pl.BlockSpec(shape, imap)


def _cp(sem=None, mb=48):
    return pltpu.CompilerParams(dimension_semantics=sem, vmem_limit_bytes=mb << 20)


def _pad_cols(a, ncols):
    out = []
    for s, w, pw in _SEGS:
        if s >= ncols:
            break
        piece = a[..., s:s + w]
        if pw > w:
            piece = jnp.pad(piece, [(0, 0)] * (a.ndim - 1) + [(0, pw - w)])
        out.append(piece)
    return jnp.concatenate(out, axis=-1)


def _unpad_cols(a, ncols):
    out = []
    off = 0
    for s, w, pw in _SEGS:
        if s >= ncols:
            break
        out.append(a[..., off:off + w])
        off += pw
    return jnp.concatenate(out, axis=-1)


def _sigmoid(x):
    return 1.0 / (1.0 + jnp.exp(-x))


def _softplus(x):
    return jnp.maximum(x, 0.0) + jnp.log(1.0 + jnp.exp(-jnp.abs(x)))


def _e128(dtype):
    r = lax.broadcasted_iota(jnp.int32, (128, 128), 0) >= HS
    c = lax.broadcasted_iota(jnp.int32, (128, 128), 1) >= HS
    return (r == c).astype(dtype)


def _segsum(x, e):
    return jnp.concatenate(
        [jnp.dot(x[:, 128 * g:128 * (g + 1)], e, precision=HI, preferred_element_type=F32) for g in range(4)],
        axis=1)


def _colsum(x):
    return jnp.sum(x, axis=0, keepdims=True)


def _rowmean(x):
    return jnp.mean(x, axis=-1, keepdims=True)


def _diag(x, npairs):
    row = lax.broadcasted_iota(jnp.int32, (HS, 128), 0)
    lane = lax.broadcasted_iota(jnp.int32, (HS, 128), 1) & (HS - 1)
    keep = jnp.broadcast_to((lane == row)[None], (npairs, HS, 128))
    return jnp.where(keep, x.reshape(npairs, HS, 128), 0.0).reshape(npairs * HS, 128)


def _segb(x, e):
    return jnp.dot(x.astype(BF16), e, preferred_element_type=F32)


_segb1 = _segb


def _expand(row, npairs):
    return jnp.concatenate([jnp.broadcast_to(row[:, 128 * j:128 * (j + 1)], (HS, 128)) for j in range(npairs)], axis=0)


def _colb(row, npairs, e):
    return _segb1(_diag(_expand(row, npairs), npairs), e)


def _pair_colsum(x, npairs):
    return jnp.concatenate([_colsum(x[HS * j:HS * (j + 1)]) for j in range(npairs)], axis=1)


def _conv_pos():
    return lax.broadcasted_iota(jnp.int32, (TT, W), 0) & (LINE - 1)


def _shifted(u, s, pos):
    if s == 0:
        return u
    sh = pltpu.roll(u, (-s) % TT, 0)
    valid = jnp.logical_and(pos + s >= 0, pos + s < LINE)
    return jnp.where(valid, sh, 0.0)


def _acc(ref, val, first):
    @pl.when(first)
    def _():
        ref[...] = jnp.zeros(ref.shape, ref.dtype)
    ref[...] += val


class _Tiles:
    def __init__(self, B, t_lat):
        self.B = B
        self.NLT = t_lat // TT
        self.TPS = self.NLT + 1
        self.NT = B * self.TPS
        self.NL = B * self.NLT
        self.NTOK = self.NT * TT
        self.NLAT = self.NL * TT
        self.TTOT = self.TPS * TT
        self.BW = B * W

    def b(self, i):
        return i // self.TPS

    def q(self, i):
        return i % self.TPS

    def lat(self, i):
        return (i // self.TPS) * self.NLT + jnp.maximum(i % self.TPS - 1, 0)

    def tok(self, l):
        return (l // self.NLT) * self.TPS + 1 + l % self.NLT

    def mod_spec(self):
        return _bs((1, 1, 6, D), lambda i: (i // self.TPS, jnp.minimum(i % self.TPS, 1), 0, 0))

    def tm_spec(self):
        return _bs((TT, W), lambda i: (i % self.TPS, i // self.TPS))

    def tm2_spec(self):
        return _bs((2, TT, W), lambda i: (0, i % self.TPS, i // self.TPS))


def _row(shape_last):
    return _bs((1, shape_last), lambda i: (0, 0))


def _mix_in(T, xc, modrows, g, w_rw, w_cv):
    def kern(x_ref, mod_ref, g_ref, wr_ref, wc_ref, prw_ref, pcv_ref, h_ref):
        x = x_ref[...]
        s = lax.rsqrt(_rowmean(x * x) + EPS_RMS)
        h = (x * s * g_ref[...]) * (1.0 + mod_ref[0, 0, 1:2, :]) + mod_ref[0, 0, 0:1, :]
        hb = h.astype(BF16)
        h_ref[...] = hb
        prw_ref[...] = jnp.dot(hb, wr_ref[...], preferred_element_type=F32)
        pcv_ref[...] = jnp.dot(hb, wc_ref[...], preferred_element_type=F32)

    return pl.pallas_call(
        kern, grid=(T.NT,), name="mix_in",
        in_specs=[_bs((TT, D), lambda i: (i, 0)), T.mod_spec(), _row(D),
                  _bs((D, RWC), lambda i: (0, 0)), _bs((D, CVC), lambda i: (0, 0))],
        out_specs=[_bs((TT, RWC), lambda i: (i, 0)), _bs((TT, CVC), lambda i: (i, 0)), _bs((TT, D), lambda i: (i, 0))],
        out_shape=[jax.ShapeDtypeStruct((T.NTOK, RWC), F32), jax.ShapeDtypeStruct((T.NTOK, CVC), F32),
                   jax.ShapeDtypeStruct((T.NTOK, D), BF16)],
        compiler_params=_cp(("parallel",)),
    )(xc, modrows, g, w_rw, w_cv)


def _halo_specs(T):
    nb8 = T.NTOK // 8
    prev = _bs((8, RWC), lambda i: (jnp.maximum(i * (TT // 8) - 1, 0), 0))
    nxt = _bs((8, RWC), lambda i: (jnp.minimum((i + 1) * (TT // 8), nb8 - 1), 0))
    return prev, nxt


def _halo_masks(T, i):
    q = i % T.TPS
    has_prev = jnp.logical_and(q != 0, q != 1).astype(F32)
    has_next = jnp.logical_and(q != 0, q != T.TPS - 1).astype(F32)
    return has_prev, has_next


def _neighbours(z, prev_row, next_row):
    rowi = lax.broadcasted_iota(jnp.int32, z.shape, 0)
    zprev = jnp.where(rowi == 0, prev_row, pltpu.roll(z, 1, 0))
    znext = jnp.where(rowi == TT - 1, next_row, pltpu.roll(z, TT - 1, 0))
    return zprev, znext


def _prep_math(rw, w0, w2, a0, a2, k_k, k_a, e):
    r = rw[:, 0:512]
    k = rw[:, 512:1024]
    v = rw[:, 1024:1536]
    kr = k * k_k
    ss = _segsum(kr * kr, e)
    rt = jnp.sqrt(ss)
    inv = 1.0 / jnp.maximum(rt, 1e-12)
    kk = kr * inv
    o = dict(r=r, k=k, v=v, kr=kr, rt=rt, inv=inv, kk=kk, th=[], pre=[], ex=[], dec=[], iclr=[], kd=[], bb=[], ad=[])
    for d in (0, 1):
        wd = rw[:, 1536 + LRW * d:1536 + LRW * (d + 1)]
        ad = rw[:, 1792 + LRW * d:1792 + LRW * (d + 1)]
        th = jnp.tanh(wd)
        pre = w0[d] + jnp.dot(th, w2[d], precision=HI, preferred_element_type=F32)
        ex = jnp.exp(-_softplus(-pre) - 0.5)
        dec = jnp.exp(-ex)
        iclr = _sigmoid(a0[d] + jnp.dot(ad, a2[d], precision=HI, preferred_element_type=F32))
        o["th"].append(th)
        o["pre"].append(pre)
        o["ex"].append(ex)
        o["dec"].append(dec)
        o["iclr"].append(iclr)
        o["ad"].append(ad)
        o["kd"].append(k * (1.0 + (iclr - 1.0) * k_a))
        o["bb"].append(kk * iclr)
    return o


def _load_prep_params(w0_ref, w2_ref, a0_ref, a2_ref):
    w0 = [w0_ref[0:1, :], w0_ref[1:2, :]]
    a0 = [a0_ref[0:1, :], a0_ref[1:2, :]]
    w2 = [w2_ref[0], w2_ref[1]]
    a2 = [a2_ref[0], a2_ref[1]]
    return w0, w2, a0, a2


def _prep_param_specs():
    return [_bs((2, W), lambda i: (0, 0)), _bs((2, LRW, W), lambda i: (0, 0, 0)),
            _bs((2, W), lambda i: (0, 0)), _bs((2, LRW, W), lambda i: (0, 0, 0)), _row(W), _row(W)]


def _rwkv_prep(T, p_rw, mu_p, mu_n, w0, w2, a0, a2, k_k, k_a):
    def kern(p_ref, pp_ref, pn_ref, mp_ref, mn_ref, w0_ref, w2_ref, a0_ref, a2_ref, kk_ref, ka_ref,
             r_o, v_o, kk_o, dec_o, kd_o, bb_o, rw_o):
        i = pl.program_id(0)
        has_prev, has_next = _halo_masks(T, i)
        z = p_ref[...]
        zprev, znext = _neighbours(z, pp_ref[7:8, :] * has_prev, pn_ref[0:1, :] * has_next)
        rw = z + mp_ref[...] * (zprev - z) + mn_ref[...] * (znext - z)
        rw_o[...] = rw
        w0v, w2v, a0v, a2v = _load_prep_params(w0_ref, w2_ref, a0_ref, a2_ref)
        o = _prep_math(rw, w0v, w2v, a0v, a2v, kk_ref[...], ka_ref[...], _e128(F32))
        r_o[...] = o["r"]
        v_o[...] = o["v"]
        kk_o[...] = o["kk"]
        for d in (0, 1):
            dec_o[d] = o["dec"][d]
            kd_o[d] = o["kd"][d]
            bb_o[d] = o["bb"][d]

    prev, nxt = _halo_specs(T)
    tm = jax.ShapeDtypeStruct((T.TTOT, T.BW), F32)
    tm2 = jax.ShapeDtypeStruct((2, T.TTOT, T.BW), F32)
    return pl.pallas_call(
        kern, grid=(T.NT,), name="rwkv_prep",
        in_specs=[_bs((TT, RWC), lambda i: (i, 0)), prev, nxt, _row(RWC), _row(RWC)] + _prep_param_specs(),
        out_specs=[T.tm_spec(), T.tm_spec(), T.tm_spec(), T.tm2_spec(), T.tm2_spec(), T.tm2_spec(),
                   _bs((TT, RWC), lambda i: (i, 0))],
        out_shape=[tm, tm, tm, tm2, tm2, tm2, jax.ShapeDtypeStruct((T.NTOK, RWC), F32)],
        compiler_params=_cp(("parallel",)),
    )(p_rw, p_rw, p_rw, mu_p, mu_n, w0, w2, a0, a2, k_k, k_a)


def _scan_fwd(T, r, v, kk, dec, kd, bb):
    NP = T.BW // 128
    NCH = T.TTOT // SCAN_CH
    NCC = TT // SCAN_CH
    NQ = SCAN_CH // SCAN_SUB

    def tmap(d, i):
        rev = jnp.where(i < NCC, NCC - 1 - i, NCH - 1 - (i - NCC))
        return jnp.where(d == 0, i, rev)

    def kern(r_ref, v_ref, kk_ref, dec_ref, kd_ref, bb_ref, y_ref, ck_ref, s_ref):
        d = pl.program_id(0)
        i = pl.program_id(1)

        @pl.when(i == 0)
        def _():
            s_ref[...] = jnp.zeros_like(s_ref)

        e = _e128(BF16)

        def step(t, carry):
            row = jnp.where(d == 0, t, SCAN_CH - 1 - t)
            a_ = _expand(-kk_ref[pl.ds(row, 1), :], NP)
            s = s_ref[...]
            sa = _segb(s * a_, e)
            vc = _colb(v_ref[pl.ds(row, 1), :], NP, e)
            s = (s * _expand(dec_ref[0, pl.ds(row, 1), :], NP) + sa * _expand(bb_ref[0, pl.ds(row, 1), :], NP)
                 + vc * _expand(kd_ref[0, pl.ds(row, 1), :], NP))
            s_ref[...] = s
            yb = _segb1(s * _expand(r_ref[pl.ds(row, 1), :], NP), e)
            y_ref[0, pl.ds(row, 1), :] = _pair_colsum(_diag(yb, NP), NP)
            return carry

        for qd in range(NQ):
            ck_ref[0, qd] = s_ref[...]
            lax.fori_loop(qd * SCAN_SUB, (qd + 1) * SCAN_SUB, step, 0, unroll=SCAN_UNROLL)

    sh = _bs((SCAN_CH, T.BW), lambda d, i: (tmap(d, i), 0))
    dr = _bs((1, SCAN_CH, T.BW), lambda d, i: (d, tmap(d, i), 0))
    return pl.pallas_call(
        kern, grid=(2, NCH), name="scan_fwd",
        in_specs=[sh, sh, sh, dr, dr, dr],
        out_specs=[dr, _bs((1, NQ, NP * HS, 128), lambda d, i: (d, i, 0, 0))],
        out_shape=[jax.ShapeDtypeStruct((2, T.TTOT, T.BW), F32),
                   jax.ShapeDtypeStruct((2, NCH * NQ, NP * HS, 128), F32)],
        scratch_shapes=[pltpu.VMEM((NP * HS, 128), F32)],
        compiler_params=_cp(("arbitrary", "arbitrary")),
    )(r, v, kk, dec, kd, bb)


def _readout_fwd(y, r, v, gd, kbar, rk, gw, lw, lb, e):
    mu = _segsum(y, e) * (1.0 / HS)
    yc = y - mu
    var = _segsum(yc * yc, e) * (1.0 / HS)
    rstd = lax.rsqrt(var + EPS_GN)
    yhat = yc * rstd
    yn = yhat * lw + lb
    q = _segsum(r * kbar * rk, e)
    sg = _sigmoid(gd)
    gg = jnp.dot(sg, gw, precision=HI, preferred_element_type=F32)
    return dict(yhat=yhat, rstd=rstd, yn=yn, q=q, sg=sg, gg=gg, out=(yn + q * v) * gg)


def _conv_fwd(cva, cvb, cw_ref, cb, lw, lb):
    pos = _conv_pos()
    sgb = _sigmoid(cvb)
    u = cva * sgb
    c = jnp.zeros_like(u)
    for j in range(KCONV):
        c = c + cw_ref[j:j + 1, :] * _shifted(u, j - KCONV // 2, pos)
    c = c + cb
    mu = _rowmean(c)
    cc = c - mu
    rstd = lax.rsqrt(_rowmean(cc * cc) + EPS_LN)
    chat = cc * rstd
    cn = chat * lw + lb
    scn = _sigmoid(cn)
    return dict(sgb=sgb, u=u, chat=chat, rstd=rstd, cn=cn, scn=scn, out=cn * scn, pos=pos)


def _mix_out(T, y, kd, rw, p_cv, xc, modrows, rk, gw, lnw, lnb, cw, cb, clw, clb, pg, w_out):
    tk = T.tok

    def kern(y_ref, kd_ref, rw_ref, pcv_ref, x_ref, mod_ref, rk_ref, gw_ref, lw_ref, lb_ref, cw_ref, cb_ref,
             clw_ref, clb_ref, pg_ref, wo_ref, cat_o, mix_o, x1_o):
        e = _e128(F32)
        ro = _readout_fwd(y_ref[0] + y_ref[1], rw_ref[:, 0:512], rw_ref[:, 1024:1536], rw_ref[:, 2048:2304],
                          0.5 * (kd_ref[0] + kd_ref[1]), rk_ref[...], gw_ref[...], lw_ref[...], lb_ref[...], e)
        cv = _conv_fwd(pcv_ref[:, 0:512], pcv_ref[:, 512:1024], cw_ref, cb_ref[...], clw_ref[...], clb_ref[...])
        catb = jnp.concatenate([ro["out"], cv["out"]], axis=1).astype(BF16)
        cat_o[...] = catb
        mix = jnp.dot(catb, wo_ref[...], preferred_element_type=F32)
        mix_o[...] = mix
        sm = lax.rsqrt(_rowmean(mix * mix) + EPS_RMS)
        x1_o[...] = x_ref[...] + mod_ref[0, 0, 2:3, :] * (mix * sm * pg_ref[...])

    lat = lambda l: (l, 0)
    return pl.pallas_call(
        kern, grid=(T.NL,), name="mix_out",
        in_specs=[_bs((2, TT, W), lambda l: (0, 1 + l % T.NLT, l // T.NLT)),
                  _bs((2, TT, W), lambda l: (0, 1 + l % T.NLT, l // T.NLT)),
                  _bs((TT, RWC), lambda l: (tk(l), 0)), _bs((TT, CVC), lambda l: (tk(l), 0)),
                  _bs((TT, D), lambda l: (tk(l), 0)),
                  _bs((1, 1, 6, D), lambda l: (l // T.NLT, 1, 0, 0)),
                  _row(W), _bs((GDW, W), lambda l: (0, 0)), _row(W), _row(W),
                  _bs((32, W), lambda l: (0, 0)), _row(W), _row(W), _row(W), _row(D),
                  _bs((D, D), lambda l: (0, 0))],
        out_specs=[_bs((TT, D), lat), _bs((TT, D), lat), _bs((TT, D), lat)],
        out_shape=[jax.ShapeDtypeStruct((T.NLAT, D), BF16), jax.ShapeDtypeStruct((T.NLAT, D), F32),
                   jax.ShapeDtypeStruct((T.NLAT, D), F32)],
        compiler_params=_cp(("parallel",)),
    )(y, kd, rw, p_cv, xc, modrows, rk, gw, lnw, lnb, cw, cb, clw, clb, pg, w_out)


MT = 512
FC = 1024


def _mlp_fwd(T, x1, modrows, g, w1, w2):
    per_b = T.NLT * TT // MT

    def kern(x_ref, mod_ref, g_ref, w1_ref, w2_ref, m_o, h2_o, h2_s):
        f = pl.program_id(1)

        @pl.when(f == 0)
        def _():
            x = x_ref[...]
            s = lax.rsqrt(_rowmean(x * x) + EPS_RMS)
            h2 = (x * s * g_ref[...]) * (1.0 + mod_ref[0, 0, 4:5, :]) + mod_ref[0, 0, 3:4, :]
            h2_s[...] = h2.astype(BF16)
            h2_o[...] = h2.astype(BF16)
            m_o[...] = jnp.zeros_like(m_o)

        a = jnp.dot(h2_s[...], w1_ref[...], preferred_element_type=F32)
        rl = jnp.maximum(a, 0.0)
        m_o[...] += jnp.dot((rl * rl).astype(BF16), w2_ref[...], preferred_element_type=F32)

    tok = lambda t, f: (t, 0)
    return pl.pallas_call(
        kern, grid=(T.NLAT // MT, DFF // FC), name="mlp_fwd",
        in_specs=[_bs((MT, D), tok), _bs((1, 1, 6, D), lambda t, f: (t // per_b, 1, 0, 0)),
                  _bs((1, D), lambda t, f: (0, 0)), _bs((D, FC), lambda t, f: (0, f)), _bs((FC, D), lambda t, f: (f, 0))],
        out_specs=[_bs((MT, D), tok), _bs((MT, D), tok)],
        out_shape=[jax.ShapeDtypeStruct((T.NLAT, D), F32), jax.ShapeDtypeStruct((T.NLAT, D), BF16)],
        scratch_shapes=[pltpu.VMEM((MT, D), BF16)],
        compiler_params=_cp(("parallel", "arbitrary")),
    )(x1, modrows, g, w1, w2)


def _loss_head(T, m, x1, tgt, modrows, pg):
    def kern(m_ref, x1_ref, t_ref, mod_ref, pg_ref, loss_o, dm_o, dx2_o, dg2_o, dpg_o):
        l = pl.program_id(0)
        m_ = m_ref[...]
        sm = lax.rsqrt(_rowmean(m_ * m_) + EPS_RMS)
        mn = m_ * sm
        g2 = mod_ref[0, 0, 5:6, :]
        pgv = pg_ref[...]
        diff = x1_ref[...] + g2 * (mn * pgv) - t_ref[...]
        sq = jnp.sum(_colsum(diff * diff), axis=1, keepdims=True)
        _acc(loss_o, jnp.zeros((8, 128), F32) + (0.5 / D) * sq, l == 0)
        dx2 = diff * (1.0 / D)
        dx2_o[...] = dx2
        _acc(dg2_o.at[0], _colsum(dx2 * mn * pgv), l % T.NLT == 0)
        _acc(dpg_o, _colsum(dx2 * g2 * mn), l == 0)
        dmn = dx2 * g2 * pgv
        dm_o[...] = (sm * (dmn - mn * _rowmean(dmn * mn))).astype(BF16)

    lat = lambda l: (l, 0)
    return pl.pallas_call(
        kern, grid=(T.NL,), name="loss_head",
        in_specs=[_bs((TT, D), lat), _bs((TT, D), lat), _bs((TT, D), lat),
                  _bs((1, 1, 6, D), lambda l: (l // T.NLT, 1, 0, 0)), _row(D)],
        out_specs=[_bs((8, 128), lambda l: (0, 0)), _bs((TT, D), lat), _bs((TT, D), lat),
                   _bs((1, 1, D), lambda l: (l // T.NLT, 0, 0)), _row(D)],
        out_shape=[jax.ShapeDtypeStruct((8, 128), F32), jax.ShapeDtypeStruct((T.NLAT, D), BF16),
                   jax.ShapeDtypeStruct((T.NLAT, D), F32), jax.ShapeDtypeStruct((T.B, 1, D), F32),
                   jax.ShapeDtypeStruct((1, D), F32)],
        compiler_params=_cp(("arbitrary",)),
    )(m, x1, tgt, modrows, pg)


_NT_DIMS = (((1,), (1,)), ((), ()))
_TN_DIMS = (((0,), (0,)), ((), ()))


def _mlp_bwd(T, h2, dm, w1, w2):
    def kern(h2_ref, dm_ref, w1_ref, w2_ref, f_o, da_o, dh2_o):
        f = pl.program_id(1)
        a = jnp.dot(h2_ref[...], w1_ref[...], preferred_element_type=F32)
        rl = jnp.maximum(a, 0.0)
        f_o[...] = (rl * rl).astype(BF16)
        df = lax.dot_general(dm_ref[...], w2_ref[...], _NT_DIMS, preferred_element_type=F32)
        dab = (df * (2.0 * rl)).astype(BF16)
        da_o[...] = dab
        _acc(dh2_o, lax.dot_general(dab, w1_ref[...], _NT_DIMS, preferred_element_type=F32), f == 0)

    tok = lambda t, f: (t, 0)
    return pl.pallas_call(
        kern, grid=(T.NLAT // MT, DFF // FC), name="mlp_bwd",
        in_specs=[_bs((MT, D), tok), _bs((MT, D), tok), _bs((D, FC), lambda t, f: (0, f)),
                  _bs((FC, D), lambda t, f: (f, 0))],
        out_specs=[_bs((MT, FC), lambda t, f: (t, f)), _bs((MT, FC), lambda t, f: (t, f)), _bs((MT, D), tok)],
        out_shape=[jax.ShapeDtypeStruct((T.NLAT, DFF), BF16), jax.ShapeDtypeStruct((T.NLAT, DFF), BF16),
                   jax.ShapeDtypeStruct((T.NLAT, D), F32)],
        compiler_params=_cp(("parallel", "arbitrary")),
    )(h2, dm, w1, w2)


def _mlp_in_bwd(T, dh2, x1, dx2, modrows, g):
    def kern(dh_ref, x1_ref, dx2_ref, mod_ref, g_ref, dx1_o, dmod_o, dg_o):
        i = pl.program_id(0)
        lat = (i % T.TPS != 0).astype(F32)
        x = x1_ref[...]
        s = lax.rsqrt(_rowmean(x * x) + EPS_RMS)
        xh = x * s
        gv = g_ref[...]
        dh = dh_ref[...] * lat
        n2 = xh * gv
        first_b = i % T.TPS == 0
        _acc(dmod_o.at[0, 0:1, :], _colsum(dh), first_b)
        _acc(dmod_o.at[0, 1:2, :], _colsum(dh * n2), first_b)
        dn2 = dh * (1.0 + mod_ref[0, 0, 4:5, :])
        _acc(dg_o, _colsum(dn2 * xh), i == 0)
        dxh = dn2 * gv
        dx1_o[...] = (dx2_ref[...] + s * (dxh - xh * _rowmean(dxh * xh))) * lat

    lat_i = lambda i: (T.lat(i), 0)
    return pl.pallas_call(
        kern, grid=(T.NT,), name="mlp_in_bwd",
        in_specs=[_bs((TT, D), lat_i), _bs((TT, D), lat_i), _bs((TT, D), lat_i),
                  _bs((1, 1, 6, D), lambda i: (i // T.TPS, 1, 0, 0)), _row(D)],
        out_specs=[_bs((TT, D), lambda i: (i, 0)), _bs((1, 2, D), lambda i: (i // T.TPS, 0, 0)), _row(D)],
        out_shape=[jax.ShapeDtypeStruct((T.NTOK, D), F32), jax.ShapeDtypeStruct((T.B, 2, D), F32),
                   jax.ShapeDtypeStruct((1, D), F32)],
        compiler_params=_cp(("arbitrary",)),
    )(dh2, x1, dx2, modrows, g)


def _mix_post_bwd(T, dx1, mix, modrows, pg, w_out):
    def kern(dx_ref, mix_ref, mod_ref, pg_ref, wo_ref, dmix_o, dcat_o, dg1_o, dpg_o):
        i = pl.program_id(0)
        lat = (i % T.TPS != 0).astype(F32)
        dx = dx_ref[...]
        mix = mix_ref[...]
        sm = lax.rsqrt(_rowmean(mix * mix) + EPS_RMS)
        mh = mix * sm
        g1 = mod_ref[0, 0, 2:3, :]
        pgv = pg_ref[...]
        _acc(dg1_o.at[0], _colsum(dx * mh * pgv), i % T.TPS == 0)
        _acc(dpg_o, _colsum(dx * g1 * mh), i == 0)
        dmh = dx * g1 * pgv
        dmix = ((sm * (dmh - mh * _rowmean(dmh * mh))) * lat).astype(BF16)
        dmix_o[...] = dmix
        dcat_o[...] = lax.dot_general(dmix, wo_ref[...], _NT_DIMS, preferred_element_type=F32)

    tok = lambda i: (i, 0)
    return pl.pallas_call(
        kern, grid=(T.NT,), name="mix_post_bwd",
        in_specs=[_bs((TT, D), tok), _bs((TT, D), lambda i: (T.lat(i), 0)),
                  _bs((1, 1, 6, D), lambda i: (i // T.TPS, 1, 0, 0)), _row(D), _bs((D, D), lambda i: (0, 0))],
        out_specs=[_bs((TT, D), tok), _bs((TT, D), tok), _bs((1, 1, D), lambda i: (i // T.TPS, 0, 0)), _row(D)],
        out_shape=[jax.ShapeDtypeStruct((T.NTOK, D), BF16), jax.ShapeDtypeStruct((T.NTOK, D), F32),
                   jax.ShapeDtypeStruct((T.B, 1, D), F32), jax.ShapeDtypeStruct((1, D), F32)],
        compiler_params=_cp(("arbitrary",)),
    )(dx1, mix, modrows, pg, w_out)


def _conv_bwd(T, dcat, p_cv, cw, cb, clw, clb):
    def kern(dc_ref, pcv_ref, cw_ref, cb_ref, clw_ref, clb_ref, dp_o, dcw_o, dcb_o, dlw_o, dlb_o):
        i = pl.program_id(0)
        first = i == 0
        cva = pcv_ref[:, 0:512]
        cv = _conv_fwd(cva, pcv_ref[:, 512:1024], cw_ref, cb_ref[...], clw_ref[...], clb_ref[...])
        dout = dc_ref[...]
        scn = cv["scn"]
        dcn = dout * (scn * (1.0 + cv["cn"] * (1.0 - scn)))
        chat = cv["chat"]
        _acc(dlw_o, _colsum(dcn * chat), first)
        _acc(dlb_o, _colsum(dcn), first)
        dchat = dcn * clw_ref[...]
        dc = cv["rstd"] * (dchat - _rowmean(dchat) - chat * _rowmean(dchat * chat))
        _acc(dcb_o, _colsum(dc), first)
        pos = cv["pos"]
        u = cv["u"]
        du = jnp.zeros_like(u)
        for j in range(KCONV):
            s = j - KCONV // 2
            _acc(dcw_o.at[j:j + 1, :], _colsum(dc * _shifted(u, s, pos)), first)
            du = du + cw_ref[j:j + 1, :] * _shifted(dc, -s, pos)
        _acc(dcw_o.at[KCONV:KCONV + 1, :], jnp.zeros((1, W), F32), first)
        sgb = cv["sgb"]
        dp_o[...] = jnp.concatenate([du * sgb, du * cva * sgb * (1.0 - sgb)], axis=1).astype(BF16)

    return pl.pallas_call(
        kern, grid=(T.NT,), name="conv_bwd",
        in_specs=[_bs((TT, W), lambda i: (i, 1)), _bs((TT, CVC), lambda i: (i, 0)),
                  _bs((32, W), lambda i: (0, 0)), _row(W), _row(W), _row(W)],
        out_specs=[_bs((TT, CVC), lambda i: (i, 0)), _bs((32, W), lambda i: (0, 0)), _row(W), _row(W), _row(W)],
        out_shape=[jax.ShapeDtypeStruct((T.NTOK, CVC), BF16), jax.ShapeDtypeStruct((32, W), F32),
                   jax.ShapeDtypeStruct((1, W), F32), jax.ShapeDtypeStruct((1, W), F32),
                   jax.ShapeDtypeStruct((1, W), F32)],
        compiler_params=_cp(("arbitrary",)),
    )(dcat, p_cv, cw, cb, clw, clb)


def _readout_bwd(T, dcat, y, kd, rw, rk, gw, lnw, lnb):
    def kern(dc_ref, y_ref, kd_ref, rw_ref, rk_ref, gw_ref, lw_ref, lb_ref,
             dy_o, dr_o, dv_o, dkb_o, dgd_o, drk_o, dgw_o, dlw_o, dlb_o):
        i = pl.program_id(0)
        first = i == 0
        e = _e128(F32)
        r = rw_ref[:, 0:512]
        v = rw_ref[:, 1024:1536]
        kbar = 0.5 * (kd_ref[0] + kd_ref[1])
        rk = rk_ref[...]
        ro = _readout_fwd(y_ref[0] + y_ref[1], r, v, rw_ref[:, 2048:2304], kbar, rk, gw_ref[...],
                          lw_ref[...], lb_ref[...], e)
        dout = dc_ref[...]
        dgg = dout * (ro["yn"] + ro["q"] * v)
        t1 = dout * ro["gg"]
        yhat = ro["yhat"]
        _acc(dlw_o, _colsum(t1 * yhat), first)
        _acc(dlb_o, _colsum(t1), first)
        dyh = t1 * lw_ref[...]
        dy_o[...] = ro["rstd"] * (dyh - _segsum(dyh, e) * (1.0 / HS) - yhat * (_segsum(dyh * yhat, e) * (1.0 / HS)))
        dq = _segsum(t1 * v, e)
        dv_o[...] = t1 * ro["q"]
        dr_o[...] = dq * kbar * rk
        dkb_o[...] = dq * r * rk
        _acc(drk_o, _colsum(dq * r * kbar), first)
        sg = ro["sg"]
        dsg = lax.dot_general(dgg, gw_ref[...], _NT_DIMS, precision=HI, preferred_element_type=F32)
        dgd_o[...] = dsg * sg * (1.0 - sg)
        _acc(dgw_o, lax.dot_general(sg, dgg, _TN_DIMS, precision=HI, preferred_element_type=F32), first)

    tok = lambda i: (i, 0)
    f32s = lambda *s: jax.ShapeDtypeStruct(s, F32)
    return pl.pallas_call(
        kern, grid=(T.NT,), name="readout_bwd",
        in_specs=[_bs((TT, W), tok), T.tm2_spec(), T.tm2_spec(), _bs((TT, RWC), tok),
                  _row(W), _bs((GDW, W), lambda i: (0, 0)), _row(W), _row(W)],
        out_specs=[T.tm_spec(), _bs((TT, W), tok), _bs((TT, W), tok), _bs((TT, W), tok), _bs((TT, GDW), tok),
                   _row(W), _bs((GDW, W), lambda i: (0, 0)), _row(W), _row(W)],
        out_shape=[f32s(T.TTOT, T.BW), f32s(T.NTOK, W), f32s(T.NTOK, W), f32s(T.NTOK, W), f32s(T.NTOK, GDW),
                   f32s(1, W), f32s(GDW, W), f32s(1, W), f32s(1, W)],
        compiler_params=_cp(("arbitrary",)),
    )(dcat, y, kd, rw, rk, gw, lnw, lnb)


def _scan_bwd(T, dy, r, v, kk, dec, kd, bb, ck):
    NP = T.BW // 128
    NS = T.TTOT // SCAN_SUB
    NSC = TT // SCAN_SUB

    def tmap(d, g):
        s = NS - 1 - g
        rev = jnp.where(s < NSC, NSC - 1 - s, NS - 1 - (s - NSC))
        return jnp.where(d == 0, s, rev)

    def kern(dy_ref, r_ref, v_ref, kk_ref, dec_ref, kd_ref, bb_ref, ck_ref,
             dr_o, dw_o, dk_o, dv_o, da_o, db_o, hist, ds_ref):
        d = pl.program_id(0)
        g = pl.program_id(1)

        @pl.when(g == 0)
        def _():
            ds_ref[...] = jnp.zeros_like(ds_ref)

        e = _e128(BF16)

        def local_row(t):
            return jnp.where(d == 0, t, SCAN_SUB - 1 - t)

        def fwd(t, carry):
            row = local_row(t)
            s = hist[t]
            sa = _segb(s * _expand(-kk_ref[pl.ds(row, 1), :], NP), e)
            vc = _colb(v_ref[pl.ds(row, 1), :], NP, e)
            hist[t + 1] = (s * _expand(dec_ref[0, pl.ds(row, 1), :], NP) + sa * _expand(bb_ref[0, pl.ds(row, 1), :], NP)
                           + vc * _expand(kd_ref[0, pl.ds(row, 1), :], NP))
            return carry

        hist[0] = ck_ref[0, 0]
        lax.fori_loop(0, SCAN_SUB, fwd, 0, unroll=SCAN_UNROLL)

        def bwd(tt, carry):
            t = SCAN_SUB - 1 - tt
            row = local_row(t)
            sp = hist[t]
            st = hist[t + 1]
            a_ = _expand(-kk_ref[pl.ds(row, 1), :], NP)
            b_ = _expand(bb_ref[0, pl.ds(row, 1), :], NP)
            k_ = _expand(kd_ref[0, pl.ds(row, 1), :], NP)
            sa = _segb(sp * a_, e)
            vc = _colb(v_ref[pl.ds(row, 1), :], NP, e)
            dyc = _colb(dy_ref[pl.ds(row, 1), :], NP, e)
            ds = ds_ref[...] + dyc * _expand(r_ref[pl.ds(row, 1), :], NP)
            dr_o[0, pl.ds(row, 1), :] = _pair_colsum(st * dyc, NP)
            dw_o[0, pl.ds(row, 1), :] = _pair_colsum(ds * sp, NP)
            dsa = _segb(ds * b_, e)
            db_o[0, pl.ds(row, 1), :] = _pair_colsum(ds * sa, NP)
            dvb = _segb1(ds * k_, e)
            dv_o[0, pl.ds(row, 1), :] = _pair_colsum(_diag(dvb, NP), NP)
            dk_o[0, pl.ds(row, 1), :] = _pair_colsum(ds * vc, NP)
            da_o[0, pl.ds(row, 1), :] = _pair_colsum(sp * dsa, NP)
            ds_ref[...] = ds * _expand(dec_ref[0, pl.ds(row, 1), :], NP) + dsa * a_
            return carry

        lax.fori_loop(0, SCAN_SUB, bwd, 0, unroll=SCAN_UNROLL)

    sh = _bs((SCAN_SUB, T.BW), lambda d, g: (tmap(d, g), 0))
    dr = _bs((1, SCAN_SUB, T.BW), lambda d, g: (d, tmap(d, g), 0))
    o2 = jax.ShapeDtypeStruct((2, T.TTOT, T.BW), F32)
    return pl.pallas_call(
        kern, grid=(2, NS), name="scan_bwd",
        in_specs=[sh, sh, sh, sh, dr, dr, dr, _bs((1, 1, NP * HS, 128), lambda d, g: (d, NS - 1 - g, 0, 0))],
        out_specs=[dr] * 6,
        out_shape=[o2] * 6,
        scratch_shapes=[pltpu.VMEM((SCAN_SUB + 1, NP * HS, 128), F32), pltpu.VMEM((NP * HS, 128), F32)],
        compiler_params=_cp(("arbitrary", "arbitrary"), mb=56),
    )(dy, r, v, kk, dec, kd, bb, ck)


def _prep_bwd(T, rw, dr_s, ddec, dkd, dv_s, da_s, dbb, dr_ro, dv_ro, dkbar, dgd, w0, w2, a0, a2, k_k, k_a):
    def kern(rw_ref, drs_ref, ddec_ref, dkd_ref, dvs_ref, das_ref, dbb_ref, drr_ref, dvr_ref, dkb_ref, dgd_ref,
             w0_ref, w2_ref, a0_ref, a2_ref, kk_ref, ka_ref,
             drw_o, dw0_o, dw2_o, da0_o, da2_o, dkk_o, dka_o):
        i = pl.program_id(0)
        first = i == 0
        e = _e128(F32)
        w0v, w2v, a0v, a2v = _load_prep_params(w0_ref, w2_ref, a0_ref, a2_ref)
        k_k = kk_ref[...]
        k_a = ka_ref[...]
        o = _prep_math(rw_ref[...], w0v, w2v, a0v, a2v, k_k, k_a, e)
        k, kk = o["k"], o["kk"]
        dkbh = 0.5 * dkb_ref[...]
        dk = jnp.zeros_like(k)
        dkk = -(das_ref[0] + das_ref[1])
        dka = jnp.zeros((1, W), F32)
        dwd, dad = [], []
        for d in (0, 1):
            iclr = o["iclr"][d]
            dkd_d = dkd_ref[d] + dkbh
            dbb_d = dbb_ref[d]
            dk = dk + dkd_d * (1.0 + (iclr - 1.0) * k_a)
            dka = dka + _colsum(dkd_d * k * (iclr - 1.0))
            dkk = dkk + dbb_d * iclr
            dicl = dkd_d * k * k_a + dbb_d * kk
            dpa = dicl * iclr * (1.0 - iclr)
            _acc(da0_o.at[d:d + 1, :], _colsum(dpa), first)
            dad.append(lax.dot_general(dpa, a2v[d], _NT_DIMS, precision=HI, preferred_element_type=F32))
            _acc(da2_o.at[d], lax.dot_general(o["ad"][d], dpa, _TN_DIMS, precision=HI, preferred_element_type=F32),
                 first)
            dpre = -ddec_ref[d] * o["dec"][d] * o["ex"][d] * _sigmoid(-o["pre"][d])
            _acc(dw0_o.at[d:d + 1, :], _colsum(dpre), first)
            th = o["th"][d]
            dth = lax.dot_general(dpre, w2v[d], _NT_DIMS, precision=HI, preferred_element_type=F32)
            _acc(dw2_o.at[d], lax.dot_general(th, dpre, _TN_DIMS, precision=HI, preferred_element_type=F32), first)
            dwd.append(dth * (1.0 - th * th))
        inv = o["inv"]
        kr = o["kr"]
        proj = _segsum(dkk * kr, e)
        dkr = dkk * inv - jnp.where(o["rt"] > 1e-12, kr * inv * inv * inv * proj, 0.0)
        dk = dk + dkr * k_k
        _acc(dkk_o, _colsum(dkr * k), first)
        _acc(dka_o, dka, first)
        dr = drs_ref[0] + drs_ref[1] + drr_ref[...]
        dv = dvs_ref[0] + dvs_ref[1] + dvr_ref[...]
        drw_o[...] = jnp.concatenate([dr, dk, dv, dwd[0], dwd[1], dad[0], dad[1], dgd_ref[...]], axis=1)

    tok = lambda i: (i, 0)
    f32s = lambda *s: jax.ShapeDtypeStruct(s, F32)
    p2 = lambda i: (0, 0)
    p3 = lambda i: (0, 0, 0)
    return pl.pallas_call(
        kern, grid=(T.NT,), name="prep_bwd",
        in_specs=[_bs((TT, RWC), tok)] + [T.tm2_spec()] * 6 + [_bs((TT, W), tok)] * 3 + [_bs((TT, GDW), tok)]
        + _prep_param_specs(),
        out_specs=[_bs((TT, RWC), tok), _bs((2, W), p2), _bs((2, LRW, W), p3), _bs((2, W), p2),
                   _bs((2, LRW, W), p3), _row(W), _row(W)],
        out_shape=[f32s(T.NTOK, RWC), f32s(2, W), f32s(2, LRW, W), f32s(2, W), f32s(2, LRW, W), f32s(1, W), f32s(1, W)],
        compiler_params=_cp(("arbitrary",), mb=56),
    )(rw, dr_s, ddec, dkd, dv_s, da_s, dbb, dr_ro, dv_ro, dkbar, dgd, w0, w2, a0, a2, k_k, k_a)


def _shift_bwd(T, drw, p_rw, mu_p, mu_n):
    def kern(d_ref, dp_ref, dn_ref, p_ref, pp_ref, pn_ref, mp_ref, mn_ref, dprw_o, dmp_o, dmn_o):
        i = pl.program_id(0)
        first = i == 0
        has_prev, has_next = _halo_masks(T, i)
        mp = mp_ref[...]
        mn = mn_ref[...]
        drw = d_ref[...]
        z = p_ref[...]
        zprev, znext = _neighbours(z, pp_ref[7:8, :] * has_prev, pn_ref[0:1, :] * has_next)
        _acc(dmp_o, _colsum(drw * (zprev - z)), first)
        _acc(dmn_o, _colsum(drw * (znext - z)), first)
        dprev, dnext = _neighbours(drw, dp_ref[7:8, :] * has_prev, dn_ref[0:1, :] * has_next)
        dprw_o[...] = (drw * (1.0 - mp - mn) + mp * dnext + mn * dprev).astype(BF16)

    tok = lambda i: (i, 0)
    prev, nxt = _halo_specs(T)
    f32s = lambda *s: jax.ShapeDtypeStruct(s, F32)
    return pl.pallas_call(
        kern, grid=(T.NT,), name="shift_bwd",
        in_specs=[_bs((TT, RWC), tok), prev, nxt, _bs((TT, RWC), tok), prev, nxt, _row(RWC), _row(RWC)],
        out_specs=[_bs((TT, RWC), tok), _row(RWC), _row(RWC)],
        out_shape=[jax.ShapeDtypeStruct((T.NTOK, RWC), BF16), f32s(1, RWC), f32s(1, RWC)],
        compiler_params=_cp(("arbitrary",), mb=56),
    )(drw, drw, drw, p_rw, p_rw, p_rw, mu_p, mu_n)


def _mix_in_bwd(T, dp_rw, dp_cv, xc, dx1, modrows, g, w_rw, w_cv):
    def kern(drw_ref, dcv_ref, x_ref, dx1_ref, mod_ref, g_ref, wr_ref, wc_ref, dxc_o, dmod_o, dg_o):
        i = pl.program_id(0)
        dh = (lax.dot_general(drw_ref[...], wr_ref[...], _NT_DIMS, preferred_element_type=F32)
              + lax.dot_general(dcv_ref[...], wc_ref[...], _NT_DIMS, preferred_element_type=F32))
        x = x_ref[...]
        s = lax.rsqrt(_rowmean(x * x) + EPS_RMS)
        xh = x * s
        gv = g_ref[...]
        q = i % T.TPS
        first_kind = jnp.logical_or(q == 0, q == 1)
        _acc(dmod_o.at[0, 0, 0:1, :], _colsum(dh), first_kind)
        _acc(dmod_o.at[0, 0, 1:2, :], _colsum(dh * (xh * gv)), first_kind)
        dn1 = dh * (1.0 + mod_ref[0, 0, 1:2, :])
        _acc(dg_o, _colsum(dn1 * xh), i == 0)
        dxh = dn1 * gv
        dxc_o[...] = dx1_ref[...] + s * (dxh - xh * _rowmean(dxh * xh))

    tok = lambda i: (i, 0)
    f32s = lambda *s: jax.ShapeDtypeStruct(s, F32)
    return pl.pallas_call(
        kern, grid=(T.NT,), name="mix_in_bwd",
        in_specs=[_bs((TT, RWC), tok), _bs((TT, CVC), tok), _bs((TT, D), tok), _bs((TT, D), tok), T.mod_spec(),
                  _row(D), _bs((D, RWC), lambda i: (0, 0)), _bs((D, CVC), lambda i: (0, 0))],
        out_specs=[_bs((TT, D), tok),
                   _bs((1, 1, 2, D), lambda i: (i // T.TPS, jnp.minimum(i % T.TPS, 1), 0, 0)), _row(D)],
        out_shape=[f32s(T.NTOK, D), f32s(T.B, 2, 2, D), f32s(1, D)],
        compiler_params=_cp(("arbitrary",)),
    )(dp_rw, dp_cv, xc, dx1, modrows, g, w_rw, w_cv)


def _matmul_tn(a, b, name, tk, nk, tn, amap=None, bmap=None, tm=1024):
    M = a.shape[1]
    N = b.shape[1]
    amap = amap or (lambda k: k)
    bmap = bmap or (lambda k: k)

    def kern(a_ref, b_ref, o_ref):
        _acc(o_ref, lax.dot_general(a_ref[...], b_ref[...], _TN_DIMS, preferred_element_type=F32),
             pl.program_id(2) == 0)

    return pl.pallas_call(
        kern, grid=(M // tm, N // tn, nk), name=name,
        in_specs=[_bs((tk, tm), lambda i, j, k: (amap(k), i)), _bs((tk, tn), lambda i, j, k: (bmap(k), j))],
        out_specs=_bs((tm, tn), lambda i, j, k: (i, j)),
        out_shape=jax.ShapeDtypeStruct((M, N), F32),
        compiler_params=_cp(("parallel", "parallel", "arbitrary")),
    )(a, b)


def _silu(x):
    return x * _sigmoid(x)


def _ada_fwd(c_all, c_ctx, ada_w, ada_b_blk):
    nb = c_all.shape[0]
    R = nb + 8
    ncol = ada_w.shape[1]

    def kern(c_ref, cc_ref, w_ref, b_ref, o_ref):
        lhs = jnp.concatenate([_silu(c_ref[...]), _silu(cc_ref[...]), jnp.zeros((7, D), F32)], axis=0)
        o_ref[...] = jnp.dot(lhs, w_ref[...], precision=HI, preferred_element_type=F32) + b_ref[...]

    return pl.pallas_call(
        kern, name="ada_fwd", out_shape=jax.ShapeDtypeStruct((R, ncol), F32),
        compiler_params=_cp(None, 40),
    )(c_all, c_ctx, ada_w, ada_b_blk)


def _ada_bwd(c_all, c_ctx, ada_w, ex, cx, ex_blk, cx_blk):
    nb = c_all.shape[0]
    ncol = ada_w.shape[1]

    def kern(c_ref, cc_ref, w_ref, ex_ref, cx_ref, exb_ref, cxb_ref, gw_o, gb_o, ds_o):
        lhs = jnp.concatenate([_silu(c_ref[...]), _silu(cc_ref[...]), jnp.zeros((7, D), F32)], axis=0)
        dmc_blk = _colsum(cxb_ref[...])
        rhs = jnp.concatenate([exb_ref[...], dmc_blk, jnp.zeros((7, ncol), F32)], axis=0)
        gw_o[...] = lax.dot_general(lhs, rhs, _TN_DIMS, precision=HI, preferred_element_type=F32)
        gb_o[...] = _colsum(ex_ref[...]) + _colsum(cx_ref[...])
        ds_o[...] = lax.dot_general(jnp.concatenate([dmc_blk, jnp.zeros((7, ncol), F32)], axis=0), w_ref[...],
                                    _NT_DIMS, precision=HI, preferred_element_type=F32)

    return pl.pallas_call(
        kern, name="ada_bwd",
        out_shape=[jax.ShapeDtypeStruct((D, ncol), F32), jax.ShapeDtypeStruct((1, ex.shape[1]), F32),
                   jax.ShapeDtypeStruct((8, D), F32)],
        compiler_params=_cp(None, 48),
    )(c_all, c_ctx, ada_w, ex, cx, ex_blk, cx_blk)


def _cctx_final(parts, c_ctx):
    def kern(p_ref, c_ref, o_ref):
        tot = p_ref[0, 0:1, :]
        for j in range(1, parts.shape[0]):
            tot = tot + p_ref[j, 0:1, :]
        c = c_ref[...]
        sg = _sigmoid(c)
        o_ref[...] = tot * (sg * (1.0 + c * (1.0 - sg)))

    return pl.pallas_call(kern, name="cctx_final", out_shape=jax.ShapeDtypeStruct((1, D), F32))(parts, c_ctx)


def _peer(kind, p, ix, iy, ic):
    if kind == "chips":
        return (p // 2, p % 2, ic)
    if kind == "all":
        return (p // 4, (p // 2) % 2, p % 2)
    return (ix, iy, p)


def _exchange(x, kind, bcast, name, chunks=1):
    npeer = {"chips": 4, "all": 8, "sib": 2}[kind]
    slab = x.shape if bcast else x.shape[1:]
    assert chunks == 1 or slab[0] == chunks

    def kern(x_ref, o_ref, send_sems, recv_sems, lsem):
        ix, iy, ic = lax.axis_index("x"), lax.axis_index("y"), lax.axis_index("c")
        me = {"chips": 2 * ix + iy, "all": 4 * ix + 2 * iy + ic, "sib": ic}[kind]
        own = pltpu.make_async_copy(x_ref if bcast else x_ref.at[me], o_ref.at[me], lsem)
        own.start()

        def part(ref, k):
            return ref if chunks == 1 else ref.at[k]

        def copy(p, k):
            return pltpu.make_async_remote_copy(
                src_ref=part(x_ref if bcast else x_ref.at[p], k), dst_ref=part(o_ref.at[me], k),
                send_sem=send_sems.at[p, k], recv_sem=recv_sems.at[me, k],
                device_id=_peer(kind, p, ix, iy, ic), device_id_type=MESH)

        def arrival(p, k):
            return pltpu.make_async_remote_copy(
                src_ref=part(x_ref if bcast else x_ref.at[p], k), dst_ref=part(o_ref.at[p], k),
                send_sem=send_sems.at[p, k], recv_sem=recv_sems.at[p, k],
                device_id=_peer(kind, p, ix, iy, ic), device_id_type=MESH)

        for p in range(npeer):
            @pl.when(me != p)
            def _():
                for k in range(chunks):
                    copy(p, k).start()
        for p in range(npeer):
            @pl.when(me != p)
            def _():
                for k in range(chunks):
                    arrival(p, k).wait_recv()
        for p in range(npeer):
            @pl.when(me != p)
            def _():
                for k in range(chunks):
                    copy(p, k).wait_send()
        own.wait()

    any_spec = pl.BlockSpec(memory_space=pl.ANY)
    return pl.pallas_call(
        kern, name=name, in_specs=[any_spec], out_specs=any_spec,
        out_shape=jax.ShapeDtypeStruct((npeer,) + tuple(slab), x.dtype),
        scratch_shapes=[pltpu.SemaphoreType.DMA((npeer, chunks)), pltpu.SemaphoreType.DMA((npeer, chunks)),
                        pltpu.SemaphoreType.DMA],
    )(x)


def _sum_slots(x, name):
    n, R, C = x.shape
    budget = (8 << 20) // (n * C * x.dtype.itemsize)
    tr = max([t for t in range(8, R + 1, 8) if R % t == 0 and t <= max(budget, 8)], default=R)

    def kern(x_ref, o_ref):
        tot = x_ref[0]
        for s in range(1, n):
            tot = tot + x_ref[s]
        o_ref[...] = tot

    return pl.pallas_call(
        kern, grid=(R // tr,), name=name,
        in_specs=[_bs((n, tr, C), lambda i: (0, i, 0))], out_specs=_bs((tr, C), lambda i: (i, 0)),
        out_shape=jax.ShapeDtypeStruct((R, C), x.dtype), compiler_params=_cp(("parallel",)),
    )(x)


def _sib_stream(x, me, name, add):
    K, R, C = x.shape[-3:]

    def kern(me_ref, *refs):
        if add:
            own_ref, send_ref, o_ref, rbuf, ssem, rsem, credit = refs
        else:
            send_ref, o_ref, rbuf, ssem, rsem, credit = refs
        k = pl.program_id(0)
        slot = k % 2
        sib = (lax.axis_index("x"), lax.axis_index("y"), 1 - lax.axis_index("c"))

        @pl.when(k >= 2)
        def _():
            pl.semaphore_wait(credit.at[slot], 1)

        src = send_ref.at[0, 0] if add else send_ref.at[0]
        cp = pltpu.make_async_remote_copy(src_ref=src, dst_ref=rbuf.at[slot], send_sem=ssem.at[slot],
                                          recv_sem=rsem.at[slot], device_id=sib, device_id_type=MESH)
        cp.start()
        cp.wait_recv()
        o_ref[0] = own_ref[0, 0] + rbuf[slot] if add else rbuf[slot]
        cp.wait_send()

        @pl.when(k + 2 < K)
        def _():
            pl.semaphore_signal(credit.at[slot], 1, device_id=sib, device_id_type=MESH)

    if add:
        in_specs = [_bs((1, 1, R, C), lambda k, me_ref: (me_ref[0], k, 0, 0)),
                    _bs((1, 1, R, C), lambda k, me_ref: (1 - me_ref[0], k, 0, 0))]
        args = (x, x)
    else:
        in_specs = [_bs((1, R, C), lambda k, me_ref: (k, 0, 0))]
        args = (x,)
    return pl.pallas_call(
        kern, name=name,
        grid_spec=pltpu.PrefetchScalarGridSpec(
            num_scalar_prefetch=1, grid=(K,), in_specs=in_specs,
            out_specs=_bs((1, R, C), lambda k, me_ref: (k, 0, 0)),
            scratch_shapes=[pltpu.VMEM((2, R, C), x.dtype), pltpu.SemaphoreType.DMA((2,)),
                            pltpu.SemaphoreType.DMA((2,)), pltpu.SemaphoreType.REGULAR((2,))]),
        out_shape=jax.ShapeDtypeStruct((K, R, C), x.dtype),
        compiler_params=_cp(("arbitrary",)),
    )(me, *args)


def _adamw(w, g, m, v, name):
    shape = w.shape
    last = shape[-1]
    rows = 1
    for s in shape[:-1]:
        rows *= s
    w2, g2, m2, v2 = (t.reshape(rows, last) for t in (w, g, m, v))
    tr = rows
    for cand in (256, 128, 64, 32, 16, 8):
        if rows > cand and rows % cand == 0 and rows * last > (1 << 18):
            tr = cand
            break
    c1 = 1.0 - ADAM_B1 ** ADAM_STEP
    c2 = 1.0 - ADAM_B2 ** ADAM_STEP

    def kern(w_ref, g_ref, m_ref, v_ref, d_o, m_o, v_o):
        gv = g_ref[...]
        mn = ADAM_B1 * m_ref[...] + (1.0 - ADAM_B1) * gv
        vn = ADAM_B2 * v_ref[...] + (1.0 - ADAM_B2) * (gv * gv)
        m_o[...] = mn
        v_o[...] = vn
        d_o[...] = -ADAM_LR * ((mn / c1) / (jnp.sqrt(vn / c2) + ADAM_EPS) + ADAM_WD * w_ref[...])

    spec = _bs((tr, last), lambda i: (i, 0))
    o = jax.ShapeDtypeStruct((rows, last), F32)
    d, mn, vn = pl.pallas_call(
        kern, grid=(rows // tr,), name=name, in_specs=[spec] * 4, out_specs=[spec] * 3, out_shape=[o, o, o],
        compiler_params=_cp(("parallel",)),
    )(w2, g2, m2, v2)
    return d.reshape(shape), mn.reshape(shape), vn.reshape(shape)


_WEIGHT_NAMES = ("c_ctx", "ada_w", "ada_b", "mix_pre_g", "mix_post_g", "mlp_pre_g", "mlp_post_g", "w_in", "mu_prev",
                 "mu_next", "decay_w0", "decay_w2", "iclr_a0", "iclr_a2", "k_k", "k_a", "r_k", "gate_w2", "lnx_w",
                 "lnx_b", "conv_w", "conv_b", "conv_ln_w", "conv_ln_b", "w_out", "mlp_w1", "mlp_w2")


def _pack_rows(parts, cols=512):
    flat = jnp.concatenate([p.reshape(-1) for p in parts])
    rows = -(-flat.shape[0] // cols)
    rows = -(-rows // 16) * 16
    flat = jnp.pad(flat, (0, rows * cols - flat.shape[0]))
    return flat.reshape(rows, cols)


def _pack2d(parts, cols=512):
    return jnp.concatenate([p.reshape(-1, cols) for p in parts], axis=0)


def _unpack2d(buf, shapes):
    out = []
    off = 0
    for s in shapes:
        n = 1
        for d in s:
            n *= d
        n //= buf.shape[1]
        out.append(buf[off:off + n].reshape(s))
        off += n
    return out


def _unpack(flat, shapes):
    out = []
    off = 0
    for s in shapes:
        n = 1
        for d in s:
            n *= d
        out.append(flat[off:off + n].reshape(s))
        off += n
    return out


def _local_step(T, xc, tgt, modrows, P):
    p_rw, p_cv, h = _mix_in(T, xc, modrows, P["mix_pre_g"], P["w_rw"], P["w_cv"])
    prep_params = (P["w0"], P["w2"], P["a0"], P["a2"], P["k_k"], P["k_a"])
    r, v, kk, dec, kd, bb, rw = _rwkv_prep(T, p_rw, P["mu_p"], P["mu_n"], *prep_params)
    y, ck = _scan_fwd(T, r, v, kk, dec, kd, bb)
    ro_params = (P["r_k"], P["gate_w2"], P["lnx_w"], P["lnx_b"])
    cv_params = (P["conv_w"], P["conv_b"], P["conv_ln_w"], P["conv_ln_b"])
    cat, mix, x1 = _mix_out(T, y, kd, rw, p_cv, xc, modrows, *ro_params, *cv_params, P["mix_post_g"], P["w_out"])
    m, h2 = _mlp_fwd(T, x1, modrows, P["mlp_pre_g"], P["w1"], P["w2m"])
    loss_acc, dm, dx2, dg2, d_mlp_post = _loss_head(T, m, x1, tgt, modrows, P["mlp_post_g"])
    fact, da, dh2 = _mlp_bwd(T, h2, dm, P["w1"], P["w2m"])
    dx1, dmod2, d_mlp_pre = _mlp_in_bwd(T, dh2, x1, dx2, modrows, P["mlp_pre_g"])
    dmix, dcat, dg1, d_mix_post = _mix_post_bwd(T, dx1, mix, modrows, P["mix_post_g"], P["w_out"])
    dp_cv, d_conv_w, d_conv_b, d_cln_w, d_cln_b = _conv_bwd(T, dcat, p_cv, *cv_params)
    dy, dr_ro, dv_ro, dkbar, dgd, d_r_k, d_gate, d_lnx_w, d_lnx_b = _readout_bwd(T, dcat, y, kd, rw, *ro_params)
    dr_s, ddec, dkd, dv_s, da_s, dbb = _scan_bwd(T, dy, r, v, kk, dec, kd, bb, ck)
    drw, d_w0, d_w2, d_a0, d_a2, d_k_k, d_k_a = _prep_bwd(T, rw, dr_s, ddec, dkd, dv_s, da_s, dbb, dr_ro, dv_ro,
                                                          dkbar, dgd, *prep_params)
    dp_rw, d_mu_p, d_mu_n = _shift_bwd(T, drw, p_rw, P["mu_p"], P["mu_n"])
    dxc, dmod1, d_mix_pre = _mix_in_bwd(T, dp_rw, dp_cv, xc, dx1, modrows, P["mix_pre_g"], P["w_rw"], P["w_cv"])
    nk = T.NTOK // TT
    dw_rw = _matmul_tn(h, dp_rw, "dw_in_rw", TT, nk, 768)
    dw_cv = _matmul_tn(h, dp_cv, "dw_in_cv", TT, nk, 1024)
    dw_out = _matmul_tn(cat, dmix, "dw_out", TT, T.NL, 1024, bmap=T.tok)
    dw1 = _matmul_tn(h2, da, "dw_mlp1", 512, T.NLAT // 512, 1024)
    dw2m = _matmul_tn(fact, dm, "dw_mlp2", 512, T.NLAT // 512, 1024)
    small = dict(mix_pre_g=d_mix_pre, mix_post_g=d_mix_post, mlp_pre_g=d_mlp_pre, mlp_post_g=d_mlp_post,
                 mu_p=d_mu_p, mu_n=d_mu_n, w0=d_w0, w2=d_w2, a0=d_a0, a2=d_a2, k_k=d_k_k, k_a=d_k_a, r_k=d_r_k,
                 gate_w2=d_gate, lnx_w=d_lnx_w, lnx_b=d_lnx_b, conv_w=d_conv_w, conv_b=d_conv_b,
                 conv_ln_w=d_cln_w, conv_ln_b=d_cln_b)
    big = dict(w_rw=dw_rw, w_cv=dw_cv, w_out=dw_out, w1=dw1, w2m=dw2m)
    dmods = dict(dmod1=dmod1, dg1=dg1, dmod2=dmod2, dg2=dg2)
    return loss_acc[0, 0], dxc, small, big, dmods


_SMALL_ORDER = ("mix_pre_g", "mix_post_g", "mlp_pre_g", "mlp_post_g", "mu_p", "mu_n", "w0", "w2", "a0", "a2", "k_k",
                "k_a", "r_k", "gate_w2", "lnx_w", "lnx_b", "conv_w", "conv_b", "conv_ln_w", "conv_ln_b")


def kernel(x, c, ctx, c_ctx, ada_w, ada_b, mix_pre_g, mix_post_g, mlp_pre_g, mlp_post_g, w_in, mu_prev, mu_next, decay_w0, decay_w2, iclr_a0, iclr_a2, k_k, k_a, r_k, gate_w2, lnx_w, lnx_b, conv_w, conv_b, conv_ln_w, conv_ln_b, w_out, mlp_w1, mlp_w2, loss_target, m_c_ctx, m_ada_w, m_ada_b, m_mix_pre_g, m_mix_post_g, m_mlp_pre_g, m_mlp_post_g, m_w_in, m_mu_prev, m_mu_next, m_decay_w0, m_decay_w2, m_iclr_a0, m_iclr_a2, m_k_k, m_k_a, m_r_k, m_gate_w2, m_lnx_w, m_lnx_b, m_conv_w, m_conv_b, m_conv_ln_w, m_conv_ln_b, m_w_out, m_mlp_w1, m_mlp_w2, v_c_ctx, v_ada_w, v_ada_b, v_mix_pre_g, v_mix_post_g, v_mlp_pre_g, v_mlp_post_g, v_w_in, v_mu_prev, v_mu_next, v_decay_w0, v_decay_w2, v_iclr_a0, v_iclr_a2, v_k_k, v_k_a, v_r_k, v_gate_w2, v_lnx_w, v_lnx_b, v_conv_w, v_conv_b, v_conv_ln_w, v_conv_ln_b, v_w_out, v_mlp_w1, v_mlp_w2):
    weights = dict(zip(_WEIGHT_NAMES, (c_ctx, ada_w, ada_b, mix_pre_g, mix_post_g, mlp_pre_g, mlp_post_g, w_in, mu_prev, mu_next, decay_w0, decay_w2, iclr_a0, iclr_a2, k_k, k_a, r_k, gate_w2, lnx_w, lnx_b, conv_w, conv_b, conv_ln_w, conv_ln_b, w_out, mlp_w1, mlp_w2)))
    moms = dict(zip(_WEIGHT_NAMES, (m_c_ctx, m_ada_w, m_ada_b, m_mix_pre_g, m_mix_post_g, m_mlp_pre_g, m_mlp_post_g, m_w_in, m_mu_prev, m_mu_next, m_decay_w0, m_decay_w2, m_iclr_a0, m_iclr_a2, m_k_k, m_k_a, m_r_k, m_gate_w2, m_lnx_w, m_lnx_b, m_conv_w, m_conv_b, m_conv_ln_w, m_conv_ln_b, m_w_out, m_mlp_w1, m_mlp_w2)))
    vars_ = dict(zip(_WEIGHT_NAMES, (v_c_ctx, v_ada_w, v_ada_b, v_mix_pre_g, v_mix_post_g, v_mlp_pre_g, v_mlp_post_g, v_w_in, v_mu_prev, v_mu_next, v_decay_w0, v_decay_w2, v_iclr_a0, v_iclr_a2, v_k_k, v_k_a, v_r_k, v_gate_w2, v_lnx_w, v_lnx_b, v_conv_w, v_conv_b, v_conv_ln_w, v_conv_ln_b, v_w_out, v_mlp_w1, v_mlp_w2)))

    B, t_lat, _ = x.shape
    assert ctx.shape[1] == TT and t_lat % TT == 0 and (t_lat * B) % MT == 0
    T = _Tiles(B, t_lat)
    ix, iy, ic = lax.axis_index("x"), lax.axis_index("y"), lax.axis_index("c")
    chip = 2 * ix + iy
    dev = 4 * ix + 2 * iy + ic
    nsh = 4
    in_sh = w_in.shape[2]
    ada_sh = ada_w.shape[2]
    lane_sh = decay_w0.shape[2]

    big_parts = (w_in[0], w_out[0], mlp_w1[0], mlp_w2[0])
    big_shapes = [p.shape for p in big_parts]
    wg = _exchange(_pack2d([p.astype(BF16) for p in big_parts]), "chips", True, "gather_big_weights")
    per = [_unpack2d(wg[j], big_shapes) for j in range(nsh)]
    w_in_f = jnp.concatenate([per[j][0] for j in range(nsh)], axis=1)
    w_out_f = jnp.concatenate([per[j][1] for j in range(nsh)], axis=0)
    w1_f = jnp.concatenate([per[j][2] for j in range(nsh)], axis=1)
    w2_f = jnp.concatenate([per[j][3] for j in range(nsh)], axis=0)
    w_in_p = _pad_cols(w_in_f, w_in_f.shape[1])

    sm_parts = (decay_w0[0], decay_w2[0], iclr_a0[0], iclr_a2[0], gate_w2[0], conv_w[0])
    sm_shapes = [p.shape for p in sm_parts]
    sg = _exchange(_pack_rows(sm_parts), "chips", True, "gather_small_weights")
    pers = [_unpack(sg[j].reshape(-1), sm_shapes) for j in range(nsh)]
    w0_f, w2_f_, a0_f, a2_f, gate_f, convw_f = (jnp.concatenate([pers[j][t] for j in range(nsh)], axis=-1)
                                                for t in range(6))

    def pad_rows(a, n):
        return jnp.pad(a, [(0, 0)] * (a.ndim - 2) + [(0, n - a.shape[-2]), (0, 0)])

    P = dict(
        w_rw=w_in_p[:, :RWC], w_cv=w_in_p[:, RWC:], w_out=w_out_f, w1=w1_f, w2m=w2_f,
        mix_pre_g=mix_pre_g, mix_post_g=mix_post_g, mlp_pre_g=mlp_pre_g, mlp_post_g=mlp_post_g,
        mu_p=_pad_cols(mu_prev, mu_prev.shape[1]), mu_n=_pad_cols(mu_next, mu_next.shape[1]),
        w0=w0_f, w2=pad_rows(w2_f_, LRW), a0=a0_f, a2=pad_rows(a2_f, LRW), k_k=k_k, k_a=k_a,
        r_k=r_k.reshape(1, W), gate_w2=pad_rows(gate_f, GDW), lnx_w=lnx_w, lnx_b=lnx_b,
        conv_w=pad_rows(convw_f, 32), conv_b=conv_b, conv_ln_w=conv_ln_w, conv_ln_b=conv_ln_b)

    c_ctx2 = c_ctx.reshape(1, D)
    c_all = _exchange(jnp.pad(c, ((0, 8 - B), (0, 0))), "all", True, "gather_c")[:, :B].reshape(8 * B, D)
    ada_b_blk = lax.dynamic_slice(ada_b, (0, chip * ada_sh), (1, ada_sh))
    mod_blk = _ada_fwd(c_all, c_ctx2, ada_w[0], ada_b_blk)
    mod_g = _exchange(mod_blk, "chips", True, "gather_mod")
    mod_all = jnp.concatenate([mod_g[j] for j in range(nsh)], axis=1)
    mod_x = lax.dynamic_slice(mod_all, (dev * B, 0), (B, 6 * D)).reshape(B, 6, D)
    mod_c = jnp.broadcast_to(mod_all[8 * B].reshape(1, 6, D), (B, 6, D))
    modrows = jnp.stack([mod_c, mod_x], axis=1)

    xc = jnp.concatenate([ctx, x], axis=1).reshape(T.NTOK, D)
    tgt = loss_target.reshape(T.NLAT, D)
    loss_loc, dxc, small, big, dm_ = _local_step(T, xc, tgt, modrows, P)
    loss = lax.psum(loss_loc, ("x", "y", "c"))
    grad_x = dxc.reshape(B, T.TTOT, D)[:, TT:, :]

    dmod_x = jnp.concatenate([dm_["dmod1"][:, 1], dm_["dg1"], dm_["dmod2"], dm_["dg2"]], axis=1)
    dmod_c = jnp.concatenate([dm_["dmod1"][:, 0], jnp.zeros((B, 4, D), F32)], axis=1)
    dpack = jnp.concatenate([dmod_x.reshape(B, 6 * D), dmod_c.reshape(B, 6 * D)], axis=0)
    dg = _exchange(dpack, "all", True, "gather_dmod")
    ex = dg[:, :B].reshape(8 * B, 6 * D)
    cx = dg[:, B:].reshape(8 * B, 6 * D)
    ex_blk = lax.dynamic_slice(ex, (0, chip * ada_sh), (8 * B, ada_sh))
    cx_blk = lax.dynamic_slice(cx, (0, chip * ada_sh), (8 * B, ada_sh))
    g_ada_w, g_ada_b, dscc = _ada_bwd(c_all, c_ctx2, ada_w[0], ex, cx, ex_blk, cx_blk)
    dscc_g = _exchange(dscc, "chips", True, "gather_dcctx")
    g_c_ctx = _cctx_final(dscc_g, c_ctx2).reshape(D)

    small = dict(small, mu_p=_unpad_cols(small["mu_p"], mu_prev.shape[1]),
                 mu_n=_unpad_cols(small["mu_n"], mu_next.shape[1]),
                 w2=small["w2"][:, :decay_w2.shape[2]], a2=small["a2"][:, :iclr_a2.shape[2]],
                 gate_w2=small["gate_w2"][:gate_w2.shape[1]], conv_w=small["conv_w"][:KCONV])
    sm_list = [small[n] for n in _SMALL_ORDER]
    sm_shapes2 = [a.shape for a in sm_list]
    me1 = ic.reshape(1).astype(jnp.int32)
    sm_pack = _pack_rows(sm_list)
    sm_pair = _sib_stream(jnp.stack([sm_pack, sm_pack])[:, None], me1, "sib_small_grads", True)[0]
    sm_tot = _sum_slots(_exchange(sm_pair, "chips", True, "gather_small_grads"), "sum_small_grads")
    S = dict(zip(_SMALL_ORDER, _unpack(sm_tot.reshape(-1), sm_shapes2)))

    def shard_last(a):
        return lax.dynamic_slice_in_dim(a, chip * lane_sh, lane_sh, axis=a.ndim - 1)

    grads = dict(
        c_ctx=g_c_ctx, ada_w=g_ada_w[None], ada_b=g_ada_b,
        mix_pre_g=S["mix_pre_g"], mix_post_g=S["mix_post_g"], mlp_pre_g=S["mlp_pre_g"], mlp_post_g=S["mlp_post_g"],
        mu_prev=S["mu_p"], mu_next=S["mu_n"],
        decay_w0=shard_last(S["w0"])[None], decay_w2=shard_last(S["w2"])[None],
        iclr_a0=shard_last(S["a0"])[None], iclr_a2=shard_last(S["a2"])[None],
        k_k=S["k_k"], k_a=S["k_a"], r_k=S["r_k"].reshape(r_k.shape),
        gate_w2=shard_last(S["gate_w2"])[None], lnx_w=S["lnx_w"], lnx_b=S["lnx_b"],
        conv_w=shard_last(S["conv_w"])[None], conv_b=S["conv_b"], conv_ln_w=S["conv_ln_w"],
        conv_ln_b=S["conv_ln_b"])

    dw_in_f = _unpad_cols(jnp.concatenate([big["w_rw"], big["w_cv"]], axis=1), w_in_f.shape[1])
    oshape = w_out.shape[1]
    mshape = mlp_w1.shape[2]
    slabs = [
        _pack2d([dw_in_f[:, in_sh * j:in_sh * (j + 1)], big["w_out"][oshape * j:oshape * (j + 1)],
                 big["w1"][:, mshape * j:mshape * (j + 1)], big["w2m"][mshape * j:mshape * (j + 1)]])
        for j in range(nsh)]
    nck = 3
    lr, lc = slabs[0].shape
    hr = lr // 2
    cr = hr // nck
    assert cr * 2 * nck == lr and cr % 8 == 0
    halves = jnp.stack([jnp.concatenate([s[h * hr:(h + 1) * hr].reshape(nck, cr, lc) for s in slabs], axis=0)
                        for h in (0, 1)])
    pair = _sib_stream(halves, me1, "sib_reduce_grads", True)
    mine = _sum_slots(_exchange(pair.reshape(nsh, nck * cr, lc), "chips", False, "reduce_big_grads"), "sum_big_grads")
    mine = mine.reshape(nck, cr, lc)
    other = _sib_stream(mine, me1, "sib_swap_grads", False)
    tot = jnp.where(ic == 0, jnp.concatenate([mine, other], axis=0), jnp.concatenate([other, mine], axis=0))
    g_w_in, g_w_out, g_w1, g_w2 = _unpack2d(tot.reshape(lr, lc), big_shapes)
    grads.update(w_in=g_w_in[None], w_out=g_w_out[None], mlp_w1=g_w1[None], mlp_w2=g_w2[None])

    deltas, new_m, new_v = {}, {}, {}
    for n in _WEIGHT_NAMES:
        g = grads[n].reshape(weights[n].shape)
        grads[n] = g
        deltas[n], new_m[n], new_v[n] = _adamw(weights[n], g, moms[n], vars_[n], "adamw_" + n)

    return (loss, grad_x, *[grads[n] for n in _WEIGHT_NAMES], *[deltas[n] for n in _WEIGHT_NAMES],
            *[new_m[n] for n in _WEIGHT_NAMES], *[new_v[n] for n in _WEIGHT_NAMES])
```

```python
import functools

import jax
import jax.numpy as jnp
from jax import lax
from jax.experimental import pallas as pl
from jax.experimental.pallas import tpu as pltpu

F32 = jnp.float32
BF16 = jnp.bfloat16
HI = lax.Precision.HIGHEST

D = 1024
W = 512
HS = 64
RWC = 2304
CVC = 1024
GDW = 256
LRW = 128
DFF = 4096
TT = 256
LINE = 64
KCONV = 31
EPS_RMS = 1e-6
EPS_LN = 1e-5
EPS_GN = 64e-5
SCAN_CH = 128
SCAN_G = 8
SCAN_BSUB = 16

ADAM_LR = 0.001
ADAM_B1 = 0.9
ADAM_B2 = 0.999
ADAM_EPS = 1e-08
ADAM_WD = 0.01
ADAM_STEP = 10

_SEGS = ((0, 1536, 1536), (1536, 64, 128), (1600, 64, 128), (1664, 64, 128), (1728, 64, 128),
         (1792, 160, 256), (1952, 1024, 1024))

MESH = pl.DeviceIdType.MESH


def _bs(shape, imap):
    return pl.BlockSpec(shape, imap)


def _cp(sem=None, mb=48):
    return pltpu.CompilerParams(dimension_semantics=sem, vmem_limit_bytes=mb << 20)


def _pad_cols(a, ncols):
    out = []
    for s, w, pw in _SEGS:
        if s >= ncols:
            break
        piece = a[..., s:s + w]
        if pw > w:
            piece = jnp.pad(piece, [(0, 0)] * (a.ndim - 1) + [(0, pw - w)])
        out.append(piece)
    return jnp.concatenate(out, axis=-1)


def _unpad_cols(a, ncols):
    out = []
    off = 0
    for s, w, pw in _SEGS:
        if s >= ncols:
            break
        out.append(a[..., off:off + w])
        off += pw
    return jnp.concatenate(out, axis=-1)


def _sigmoid(x):
    return 1.0 / (1.0 + jnp.exp(-x))


def _softplus(x):
    return jnp.maximum(x, 0.0) + jnp.log(1.0 + jnp.exp(-jnp.abs(x)))


def _e128(dtype):
    r = lax.broadcasted_iota(jnp.int32, (128, 128), 0) >= HS
    c = lax.broadcasted_iota(jnp.int32, (128, 128), 1) >= HS
    return (r == c).astype(dtype)


def _segsum(x, e):
    return jnp.concatenate(
        [jnp.dot(x[:, 128 * g:128 * (g + 1)], e, precision=HI, preferred_element_type=F32) for g in range(4)],
        axis=1)


def _colsum(x):
    return jnp.sum(x, axis=0, keepdims=True)


def _rowmean(x):
    return jnp.mean(x, axis=-1, keepdims=True)


def _diag(x, npairs):
    row = lax.broadcasted_iota(jnp.int32, (HS, 128), 0)
    lane = lax.broadcasted_iota(jnp.int32, (HS, 128), 1) & (HS - 1)
    keep = jnp.broadcast_to((lane == row)[None], (npairs, HS, 128))
    return jnp.where(keep, x.reshape(npairs, HS, 128), 0.0).reshape(npairs * HS, 128)


def _segb(x, e):
    return jnp.dot(x.astype(BF16), e, preferred_element_type=F32)


_segb1 = _segb


def _expand(row, npairs):
    return jnp.concatenate([jnp.broadcast_to(row[:, 128 * j:128 * (j + 1)], (HS, 128)) for j in range(npairs)], axis=0)


def _colb(row, npairs, e):
    return _segb1(_diag(_expand(row, npairs), npairs), e)


def _pair_colsum(x, npairs):
    return jnp.concatenate([_colsum(x[HS * j:HS * (j + 1)]) for j in range(npairs)], axis=1)


def _conv_pos():
    return lax.broadcasted_iota(jnp.int32, (TT, W), 0) & (LINE - 1)


def _shifted(u, s, pos):
    if s == 0:
        return u
    sh = pltpu.roll(u, (-s) % TT, 0)
    valid = jnp.logical_and(pos + s >= 0, pos + s < LINE)
    return jnp.where(valid, sh, 0.0)


def _acc(ref, val, first):
    @pl.when(first)
    def _():
        ref[...] = jnp.zeros(ref.shape, ref.dtype)
    ref[...] += val


class _Tiles:
    def __init__(self, B, t_lat):
        self.B = B
        self.NLT = t_lat // TT
        self.TPS = self.NLT + 1
        self.NT = B * self.TPS
        self.NL = B * self.NLT
        self.NTOK = self.NT * TT
        self.NLAT = self.NL * TT
        self.TTOT = self.TPS * TT
        self.BW = B * W

    def b(self, i):
        return i // self.TPS

    def q(self, i):
        return i % self.TPS

    def lat(self, i):
        return (i // self.TPS) * self.NLT + jnp.maximum(i % self.TPS - 1, 0)

    def tok(self, l):
        return (l // self.NLT) * self.TPS + 1 + l % self.NLT

    def mod_spec(self):
        return _bs((1, 1, 6, D), lambda i: (i // self.TPS, jnp.minimum(i % self.TPS, 1), 0, 0))

    def tm_spec(self):
        return _bs((TT, W), lambda i: (i % self.TPS, i // self.TPS))

    def tm2_spec(self):
        return _bs((2, TT, W), lambda i: (0, i % self.TPS, i // self.TPS))


def _row(shape_last):
    return _bs((1, shape_last), lambda i: (0, 0))


def _mix_in(T, xc, modrows, g, w_rw, w_cv):
    def kern(x_ref, mod_ref, g_ref, wr_ref, wc_ref, prw_ref, pcv_ref, h_ref):
        x = x_ref[...]
        s = lax.rsqrt(_rowmean(x * x) + EPS_RMS)
        h = (x * s * g_ref[...]) * (1.0 + mod_ref[0, 0, 1:2, :]) + mod_ref[0, 0, 0:1, :]
        hb = h.astype(BF16)
        h_ref[...] = hb
        prw_ref[...] = jnp.dot(hb, wr_ref[...], preferred_element_type=F32)
        pcv_ref[...] = jnp.dot(hb, wc_ref[...], preferred_element_type=F32)

    return pl.pallas_call(
        kern, grid=(T.NT,), name="mix_in",
        in_specs=[_bs((TT, D), lambda i: (i, 0)), T.mod_spec(), _row(D),
                  _bs((D, RWC), lambda i: (0, 0)), _bs((D, CVC), lambda i: (0, 0))],
        out_specs=[_bs((TT, RWC), lambda i: (i, 0)), _bs((TT, CVC), lambda i: (i, 0)), _bs((TT, D), lambda i: (i, 0))],
        out_shape=[jax.ShapeDtypeStruct((T.NTOK, RWC), F32), jax.ShapeDtypeStruct((T.NTOK, CVC), F32),
                   jax.ShapeDtypeStruct((T.NTOK, D), BF16)],
        compiler_params=_cp(("parallel",)),
    )(xc, modrows, g, w_rw, w_cv)


def _halo_specs(T):
    nb8 = T.NTOK // 8
    prev = _bs((8, RWC), lambda i: (jnp.maximum(i * (TT // 8) - 1, 0), 0))
    nxt = _bs((8, RWC), lambda i: (jnp.minimum((i + 1) * (TT // 8), nb8 - 1), 0))
    return prev, nxt


def _halo_masks(T, i):
    q = i % T.TPS
    has_prev = jnp.logical_and(q != 0, q != 1).astype(F32)
    has_next = jnp.logical_and(q != 0, q != T.TPS - 1).astype(F32)
    return has_prev, has_next


def _neighbours(z, prev_row, next_row):
    rowi = lax.broadcasted_iota(jnp.int32, z.shape, 0)
    zprev = jnp.where(rowi == 0, prev_row, pltpu.roll(z, 1, 0))
    znext = jnp.where(rowi == TT - 1, next_row, pltpu.roll(z, TT - 1, 0))
    return zprev, znext


def _prep_math(rw, w0, w2, a0, a2, k_k, k_a, e):
    r = rw[:, 0:512]
    k = rw[:, 512:1024]
    v = rw[:, 1024:1536]
    kr = k * k_k
    ss = _segsum(kr * kr, e)
    rt = jnp.sqrt(ss)
    inv = 1.0 / jnp.maximum(rt, 1e-12)
    kk = kr * inv
    o = dict(r=r, k=k, v=v, kr=kr, rt=rt, inv=inv, kk=kk, th=[], pre=[], ex=[], dec=[], iclr=[], kd=[], bb=[], ad=[])
    for d in (0, 1):
        wd = rw[:, 1536 + LRW * d:1536 + LRW * (d + 1)]
        ad = rw[:, 1792 + LRW * d:1792 + LRW * (d + 1)]
        th = jnp.tanh(wd)
        pre = w0[d] + jnp.dot(th, w2[d], precision=HI, preferred_element_type=F32)
        ex = jnp.exp(-_softplus(-pre) - 0.5)
        dec = jnp.exp(-ex)
        iclr = _sigmoid(a0[d] + jnp.dot(ad, a2[d], precision=HI, preferred_element_type=F32))
        o["th"].append(th)
        o["pre"].append(pre)
        o["ex"].append(ex)
        o["dec"].append(dec)
        o["iclr"].append(iclr)
        o["ad"].append(ad)
        o["kd"].append(k * (1.0 + (iclr - 1.0) * k_a))
        o["bb"].append(kk * iclr)
    return o


def _load_prep_params(w0_ref, w2_ref, a0_ref, a2_ref):
    w0 = [w0_ref[0:1, :], w0_ref[1:2, :]]
    a0 = [a0_ref[0:1, :], a0_ref[1:2, :]]
    w2 = [w2_ref[0], w2_ref[1]]
    a2 = [a2_ref[0], a2_ref[1]]
    return w0, w2, a0, a2


def _prep_param_specs():
    return [_bs((2, W), lambda i: (0, 0)), _bs((2, LRW, W), lambda i: (0, 0, 0)),
            _bs((2, W), lambda i: (0, 0)), _bs((2, LRW, W), lambda i: (0, 0, 0)), _row(W), _row(W)]


def _rwkv_prep(T, p_rw, mu_p, mu_n, w0, w2, a0, a2, k_k, k_a):
    def kern(p_ref, pp_ref, pn_ref, mp_ref, mn_ref, w0_ref, w2_ref, a0_ref, a2_ref, kk_ref, ka_ref,
             r_o, v_o, kk_o, dec_o, kd_o, bb_o, rw_o):
        i = pl.program_id(0)
        has_prev, has_next = _halo_masks(T, i)
        z = p_ref[...]
        zprev, znext = _neighbours(z, pp_ref[7:8, :] * has_prev, pn_ref[0:1, :] * has_next)
        rw = z + mp_ref[...] * (zprev - z) + mn_ref[...] * (znext - z)
        rw_o[...] = rw
        w0v, w2v, a0v, a2v = _load_prep_params(w0_ref, w2_ref, a0_ref, a2_ref)
        o = _prep_math(rw, w0v, w2v, a0v, a2v, kk_ref[...], ka_ref[...], _e128(F32))
        r_o[...] = o["r"]
        v_o[...] = o["v"]
        kk_o[...] = o["kk"]
        for d in (0, 1):
            dec_o[d] = o["dec"][d]
            kd_o[d] = o["kd"][d]
            bb_o[d] = o["bb"][d]

    prev, nxt = _halo_specs(T)
    tm = jax.ShapeDtypeStruct((T.TTOT, T.BW), F32)
    tm2 = jax.ShapeDtypeStruct((2, T.TTOT, T.BW), F32)
    return pl.pallas_call(
        kern, grid=(T.NT,), name="rwkv_prep",
        in_specs=[_bs((TT, RWC), lambda i: (i, 0)), prev, nxt, _row(RWC), _row(RWC)] + _prep_param_specs(),
        out_specs=[T.tm_spec(), T.tm_spec(), T.tm_spec(), T.tm2_spec(), T.tm2_spec(), T.tm2_spec(),
                   _bs((TT, RWC), lambda i: (i, 0))],
        out_shape=[tm, tm, tm, tm2, tm2, tm2, jax.ShapeDtypeStruct((T.NTOK, RWC), F32)],
        compiler_params=_cp(("parallel",)),
    )(p_rw, p_rw, p_rw, mu_p, mu_n, w0, w2, a0, a2, k_k, k_a)


def _scan_fwd(T, r, v, kk, dec, kd, bb):
    NP = T.BW // 128
    R = NP * HS
    NCH = T.TTOT // SCAN_CH
    NCC = TT // SCAN_CH
    G = SCAN_G
    NG = SCAN_CH // G
    NGRP = 4
    assert NG % NGRP == 0

    def tmap(d, i):
        rev = jnp.where(i < NCC, NCC - 1 - i, NCH - 1 - (i - NCC))
        return jnp.where(d == 0, i, rev)

    def kern(r_ref, v_ref, kk_ref, dec_ref, kd_ref, bb_ref, y_ref, fin_ref, hist_ref, ring, sems):
        d = pl.program_id(0)
        i = pl.program_id(1)

        @pl.when(i == 0)
        def _():
            ring[0] = jnp.zeros((R, 128), F32)

        e = _e128(BF16)

        def hist_copy(k):
            grp = k % NGRP
            return pltpu.make_async_copy(ring.at[pl.ds(grp * G, G)],
                                         hist_ref.at[d, pl.ds(i * SCAN_CH + k * G, G)], sems.at[grp])

        def body(k, carry):
            @pl.when(k >= NGRP - 1)
            def _():
                hist_copy(k - (NGRP - 1)).wait()

            base = (k % NGRP) * G
            for u in range(G):
                t = k * G + u
                row = jnp.where(d == 0, t, SCAN_CH - 1 - t)
                s = ring[base + u]
                sa = _segb(s * _expand(-kk_ref[pl.ds(row, 1), :], NP), e)
                vc = _colb(v_ref[pl.ds(row, 1), :], NP, e)
                s = (s * _expand(dec_ref[0, pl.ds(row, 1), :], NP) + sa * _expand(bb_ref[0, pl.ds(row, 1), :], NP)
                     + vc * _expand(kd_ref[0, pl.ds(row, 1), :], NP))
                ring[(base + u + 1) if u < G - 1 else ((k + 1) % NGRP) * G] = s
                yb = _segb(s * _expand(r_ref[pl.ds(row, 1), :], NP), e)
                y_ref[0, pl.ds(row, 1), :] = _pair_colsum(_diag(yb, NP), NP)
            hist_copy(k).start()
            return carry

        lax.fori_loop(0, NG, body, 0)
        for k in range(NG - (NGRP - 1), NG):
            hist_copy(k).wait()

        @pl.when(i == NCH - 1)
        def _():
            fin_ref[0] = ring[0]

    sh = _bs((SCAN_CH, T.BW), lambda d, i: (tmap(d, i), 0))
    dr = _bs((1, SCAN_CH, T.BW), lambda d, i: (d, tmap(d, i), 0))
    return pl.pallas_call(
        kern, grid=(2, NCH), name="scan_fwd",
        in_specs=[sh, sh, sh, dr, dr, dr],
        out_specs=[dr, _bs((1, R, 128), lambda d, i: (d, 0, 0)), pl.BlockSpec(memory_space=pl.ANY)],
        out_shape=[jax.ShapeDtypeStruct((2, T.TTOT, T.BW), F32), jax.ShapeDtypeStruct((2, R, 128), F32),
                   jax.ShapeDtypeStruct((2, T.TTOT, R, 128), F32)],
        scratch_shapes=[pltpu.VMEM((NGRP * G, R, 128), F32), pltpu.SemaphoreType.DMA((NGRP,))],
        compiler_params=_cp(("arbitrary", "arbitrary")),
    )(r, v, kk, dec, kd, bb)


def _readout_fwd(y, r, v, gd, kbar, rk, gw, lw, lb, e):
    mu = _segsum(y, e) * (1.0 / HS)
    yc = y - mu
    var = _segsum(yc * yc, e) * (1.0 / HS)
    rstd = lax.rsqrt(var + EPS_GN)
    yhat = yc * rstd
    yn = yhat * lw + lb
    q = _segsum(r * kbar * rk, e)
    sg = _sigmoid(gd)
    gg = jnp.dot(sg, gw, precision=HI, preferred_element_type=F32)
    return dict(yhat=yhat, rstd=rstd, yn=yn, q=q, sg=sg, gg=gg, out=(yn + q * v) * gg)


def _conv_fwd(cva, cvb, cw_ref, cb, lw, lb):
    pos = _conv_pos()
    sgb = _sigmoid(cvb)
    u = cva * sgb
    c = jnp.zeros_like(u)
    for j in range(KCONV):
        c = c + cw_ref[j:j + 1, :] * _shifted(u, j - KCONV // 2, pos)
    c = c + cb
    mu = _rowmean(c)
    cc = c - mu
    rstd = lax.rsqrt(_rowmean(cc * cc) + EPS_LN)
    chat = cc * rstd
    cn = chat * lw + lb
    scn = _sigmoid(cn)
    return dict(sgb=sgb, u=u, chat=chat, rstd=rstd, cn=cn, scn=scn, out=cn * scn, pos=pos)


def _mix_out(T, y, kd, rw, p_cv, xc, modrows, rk, gw, lnw, lnb, cw, cb, clw, clb, pg, w_out):
    tk = T.tok

    def kern(y_ref, kd_ref, rw_ref, pcv_ref, x_ref, mod_ref, rk_ref, gw_ref, lw_ref, lb_ref, cw_ref, cb_ref,
             clw_ref, clb_ref, pg_ref, wo_ref, cat_o, mix_o, x1_o):
        e = _e128(F32)
        ro = _readout_fwd(y_ref[0] + y_ref[1], rw_ref[:, 0:512], rw_ref[:, 1024:1536], rw_ref[:, 2048:2304],
                          0.5 * (kd_ref[0] + kd_ref[1]), rk_ref[...], gw_ref[...], lw_ref[...], lb_ref[...], e)
        cv = _conv_fwd(pcv_ref[:, 0:512], pcv_ref[:, 512:1024], cw_ref, cb_ref[...], clw_ref[...], clb_ref[...])
        catb = jnp.concatenate([ro["out"], cv["out"]], axis=1).astype(BF16)
        cat_o[...] = catb
        mix = jnp.dot(catb, wo_ref[...], preferred_element_type=F32)
        mix_o[...] = mix
        sm = lax.rsqrt(_rowmean(mix * mix) + EPS_RMS)
        x1_o[...] = x_ref[...] + mod_ref[0, 0, 2:3, :] * (mix * sm * pg_ref[...])

    lat = lambda l: (l, 0)
    return pl.pallas_call(
        kern, grid=(T.NL,), name="mix_out",
        in_specs=[_bs((2, TT, W), lambda l: (0, 1 + l % T.NLT, l // T.NLT)),
                  _bs((2, TT, W), lambda l: (0, 1 + l % T.NLT, l // T.NLT)),
                  _bs((TT, RWC), lambda l: (tk(l), 0)), _bs((TT, CVC), lambda l: (tk(l), 0)),
                  _bs((TT, D), lambda l: (tk(l), 0)),
                  _bs((1, 1, 6, D), lambda l: (l // T.NLT, 1, 0, 0)),
                  _row(W), _bs((GDW, W), lambda l: (0, 0)), _row(W), _row(W),
                  _bs((32, W), lambda l: (0, 0)), _row(W), _row(W), _row(W), _row(D),
                  _bs((D, D), lambda l: (0, 0))],
        out_specs=[_bs((TT, D), lat), _bs((TT, D), lat), _bs((TT, D), lat)],
        out_shape=[jax.ShapeDtypeStruct((T.NLAT, D), BF16), jax.ShapeDtypeStruct((T.NLAT, D), F32),
                   jax.ShapeDtypeStruct((T.NLAT, D), F32)],
        compiler_params=_cp(("parallel",)),
    )(y, kd, rw, p_cv, xc, modrows, rk, gw, lnw, lnb, cw, cb, clw, clb, pg, w_out)


MT = 512
FC = 1024


def _mlp_fwd(T, x1, modrows, g, w1, w2):
    per_b = T.NLT * TT // MT

    def kern(x_ref, mod_ref, g_ref, w1_ref, w2_ref, m_o, h2_o, h2_s):
        f = pl.program_id(1)

        @pl.when(f == 0)
        def _():
            x = x_ref[...]
            s = lax.rsqrt(_rowmean(x * x) + EPS_RMS)
            h2 = (x * s * g_ref[...]) * (1.0 + mod_ref[0, 0, 4:5, :]) + mod_ref[0, 0, 3:4, :]
            h2_s[...] = h2.astype(BF16)
            h2_o[...] = h2.astype(BF16)
            m_o[...] = jnp.zeros_like(m_o)

        a = jnp.dot(h2_s[...], w1_ref[...], preferred_element_type=F32)
        rl = jnp.maximum(a, 0.0)
        m_o[...] += jnp.dot((rl * rl).astype(BF16), w2_ref[...], preferred_element_type=F32)

    tok = lambda t, f: (t, 0)
    return pl.pallas_call(
        kern, grid=(T.NLAT // MT, DFF // FC), name="mlp_fwd",
        in_specs=[_bs((MT, D), tok), _bs((1, 1, 6, D), lambda t, f: (t // per_b, 1, 0, 0)),
                  _bs((1, D), lambda t, f: (0, 0)), _bs((D, FC), lambda t, f: (0, f)), _bs((FC, D), lambda t, f: (f, 0))],
        out_specs=[_bs((MT, D), tok), _bs((MT, D), tok)],
        out_shape=[jax.ShapeDtypeStruct((T.NLAT, D), F32), jax.ShapeDtypeStruct((T.NLAT, D), BF16)],
        scratch_shapes=[pltpu.VMEM((MT, D), BF16)],
        compiler_params=_cp(("parallel", "arbitrary")),
    )(x1, modrows, g, w1, w2)


def _loss_head(T, m, x1, tgt, modrows, pg):
    def kern(m_ref, x1_ref, t_ref, mod_ref, pg_ref, loss_o, dm_o, dx2_o, dg2_o, dpg_o):
        l = pl.program_id(0)
        m_ = m_ref[...]
        sm = lax.rsqrt(_rowmean(m_ * m_) + EPS_RMS)
        mn = m_ * sm
        g2 = mod_ref[0, 0, 5:6, :]
        pgv = pg_ref[...]
        diff = x1_ref[...] + g2 * (mn * pgv) - t_ref[...]
        sq = jnp.sum(_colsum(diff * diff), axis=1, keepdims=True)
        _acc(loss_o, jnp.zeros((8, 128), F32) + (0.5 / D) * sq, l == 0)
        dx2 = diff * (1.0 / D)
        dx2_o[...] = dx2
        _acc(dg2_o.at[0], _colsum(dx2 * mn * pgv), l % T.NLT == 0)
        _acc(dpg_o, _colsum(dx2 * g2 * mn), l == 0)
        dmn = dx2 * g2 * pgv
        dm_o[...] = (sm * (dmn - mn * _rowmean(dmn * mn))).astype(BF16)

    lat = lambda l: (l, 0)
    return pl.pallas_call(
        kern, grid=(T.NL,), name="loss_head",
        in_specs=[_bs((TT, D), lat), _bs((TT, D), lat), _bs((TT, D), lat),
                  _bs((1, 1, 6, D), lambda l: (l // T.NLT, 1, 0, 0)), _row(D)],
        out_specs=[_bs((8, 128), lambda l: (0, 0)), _bs((TT, D), lat), _bs((TT, D), lat),
                   _bs((1, 1, D), lambda l: (l // T.NLT, 0, 0)), _row(D)],
        out_shape=[jax.ShapeDtypeStruct((8, 128), F32), jax.ShapeDtypeStruct((T.NLAT, D), BF16),
                   jax.ShapeDtypeStruct((T.NLAT, D), F32), jax.ShapeDtypeStruct((T.B, 1, D), F32),
                   jax.ShapeDtypeStruct((1, D), F32)],
        compiler_params=_cp(("arbitrary",)),
    )(m, x1, tgt, modrows, pg)


_NT_DIMS = (((1,), (1,)), ((), ()))
_TN_DIMS = (((0,), (0,)), ((), ()))


def _mlp_bwd(T, h2, dm, w1, w2):
    def kern(h2_ref, dm_ref, w1_ref, w2_ref, f_o, da_o, dh2_o):
        f = pl.program_id(1)
        a = jnp.dot(h2_ref[...], w1_ref[...], preferred_element_type=F32)
        rl = jnp.maximum(a, 0.0)
        f_o[...] = (rl * rl).astype(BF16)
        df = lax.dot_general(dm_ref[...], w2_ref[...], _NT_DIMS, preferred_element_type=F32)
        dab = (df * (2.0 * rl)).astype(BF16)
        da_o[...] = dab
        _acc(dh2_o, lax.dot_general(dab, w1_ref[...], _NT_DIMS, preferred_element_type=F32), f == 0)

    tok = lambda t, f: (t, 0)
    return pl.pallas_call(
        kern, grid=(T.NLAT // MT, DFF // FC), name="mlp_bwd",
        in_specs=[_bs((MT, D), tok), _bs((MT, D), tok), _bs((D, FC), lambda t, f: (0, f)),
                  _bs((FC, D), lambda t, f: (f, 0))],
        out_specs=[_bs((MT, FC), lambda t, f: (t, f)), _bs((MT, FC), lambda t, f: (t, f)), _bs((MT, D), tok)],
        out_shape=[jax.ShapeDtypeStruct((T.NLAT, DFF), BF16), jax.ShapeDtypeStruct((T.NLAT, DFF), BF16),
                   jax.ShapeDtypeStruct((T.NLAT, D), F32)],
        compiler_params=_cp(("parallel", "arbitrary")),
    )(h2, dm, w1, w2)


def _mlp_in_bwd(T, dh2, x1, dx2, modrows, g):
    def kern(dh_ref, x1_ref, dx2_ref, mod_ref, g_ref, dx1_o, dmod_o, dg_o):
        i = pl.program_id(0)
        lat = (i % T.TPS != 0).astype(F32)
        x = x1_ref[...]
        s = lax.rsqrt(_rowmean(x * x) + EPS_RMS)
        xh = x * s
        gv = g_ref[...]
        dh = dh_ref[...] * lat
        n2 = xh * gv
        first_b = i % T.TPS == 0
        _acc(dmod_o.at[0, 0:1, :], _colsum(dh), first_b)
        _acc(dmod_o.at[0, 1:2, :], _colsum(dh * n2), first_b)
        dn2 = dh * (1.0 + mod_ref[0, 0, 4:5, :])
        _acc(dg_o, _colsum(dn2 * xh), i == 0)
        dxh = dn2 * gv
        dx1_o[...] = (dx2_ref[...] + s * (dxh - xh * _rowmean(dxh * xh))) * lat

    lat_i = lambda i: (T.lat(i), 0)
    return pl.pallas_call(
        kern, grid=(T.NT,), name="mlp_in_bwd",
        in_specs=[_bs((TT, D), lat_i), _bs((TT, D), lat_i), _bs((TT, D), lat_i),
                  _bs((1, 1, 6, D), lambda i: (i // T.TPS, 1, 0, 0)), _row(D)],
        out_specs=[_bs((TT, D), lambda i: (i, 0)), _bs((1, 2, D), lambda i: (i // T.TPS, 0, 0)), _row(D)],
        out_shape=[jax.ShapeDtypeStruct((T.NTOK, D), F32), jax.ShapeDtypeStruct((T.B, 2, D), F32),
                   jax.ShapeDtypeStruct((1, D), F32)],
        compiler_params=_cp(("arbitrary",)),
    )(dh2, x1, dx2, modrows, g)


def _mix_post_bwd(T, dx1, mix, modrows, pg, w_out):
    def kern(dx_ref, mix_ref, mod_ref, pg_ref, wo_ref, dmix_o, dcat_o, dg1_o, dpg_o):
        i = pl.program_id(0)
        lat = (i % T.TPS != 0).astype(F32)
        dx = dx_ref[...]
        mix = mix_ref[...]
        sm = lax.rsqrt(_rowmean(mix * mix) + EPS_RMS)
        mh = mix * sm
        g1 = mod_ref[0, 0, 2:3, :]
        pgv = pg_ref[...]
        _acc(dg1_o.at[0], _colsum(dx * mh * pgv), i % T.TPS == 0)
        _acc(dpg_o, _colsum(dx * g1 * mh), i == 0)
        dmh = dx * g1 * pgv
        dmix = ((sm * (dmh - mh * _rowmean(dmh * mh))) * lat).astype(BF16)
        dmix_o[...] = dmix
        dcat_o[...] = lax.dot_general(dmix, wo_ref[...], _NT_DIMS, preferred_element_type=F32)

    tok = lambda i: (i, 0)
    return pl.pallas_call(
        kern, grid=(T.NT,), name="mix_post_bwd",
        in_specs=[_bs((TT, D), tok), _bs((TT, D), lambda i: (T.lat(i), 0)),
                  _bs((1, 1, 6, D), lambda i: (i // T.TPS, 1, 0, 0)), _row(D), _bs((D, D), lambda i: (0, 0))],
        out_specs=[_bs((TT, D), tok), _bs((TT, D), tok), _bs((1, 1, D), lambda i: (i // T.TPS, 0, 0)), _row(D)],
        out_shape=[jax.ShapeDtypeStruct((T.NTOK, D), BF16), jax.ShapeDtypeStruct((T.NTOK, D), F32),
                   jax.ShapeDtypeStruct((T.B, 1, D), F32), jax.ShapeDtypeStruct((1, D), F32)],
        compiler_params=_cp(("arbitrary",)),
    )(dx1, mix, modrows, pg, w_out)


def _conv_bwd(T, dcat, p_cv, cw, cb, clw, clb):
    def kern(dc_ref, pcv_ref, cw_ref, cb_ref, clw_ref, clb_ref, dp_o, dcw_o, dcb_o, dlw_o, dlb_o):
        i = pl.program_id(0)
        first = i == 0
        cva = pcv_ref[:, 0:512]
        cv = _conv_fwd(cva, pcv_ref[:, 512:1024], cw_ref, cb_ref[...], clw_ref[...], clb_ref[...])
        dout = dc_ref[...]
        scn = cv["scn"]
        dcn = dout * (scn * (1.0 + cv["cn"] * (1.0 - scn)))
        chat = cv["chat"]
        _acc(dlw_o, _colsum(dcn * chat), first)
        _acc(dlb_o, _colsum(dcn), first)
        dchat = dcn * clw_ref[...]
        dc = cv["rstd"] * (dchat - _rowmean(dchat) - chat * _rowmean(dchat * chat))
        _acc(dcb_o, _colsum(dc), first)
        pos = cv["pos"]
        u = cv["u"]
        du = jnp.zeros_like(u)
        for j in range(KCONV):
            s = j - KCONV // 2
            _acc(dcw_o.at[j:j + 1, :], _colsum(dc * _shifted(u, s, pos)), first)
            du = du + cw_ref[j:j + 1, :] * _shifted(dc, -s, pos)
        _acc(dcw_o.at[KCONV:KCONV + 1, :], jnp.zeros((1, W), F32), first)
        sgb = cv["sgb"]
        dp_o[...] = jnp.concatenate([du * sgb, du * cva * sgb * (1.0 - sgb)], axis=1).astype(BF16)

    return pl.pallas_call(
        kern, grid=(T.NT,), name="conv_bwd",
        in_specs=[_bs((TT, W), lambda i: (i, 1)), _bs((TT, CVC), lambda i: (i, 0)),
                  _bs((32, W), lambda i: (0, 0)), _row(W), _row(W), _row(W)],
        out_specs=[_bs((TT, CVC), lambda i: (i, 0)), _bs((32, W), lambda i: (0, 0)), _row(W), _row(W), _row(W)],
        out_shape=[jax.ShapeDtypeStruct((T.NTOK, CVC), BF16), jax.ShapeDtypeStruct((32, W), F32),
                   jax.ShapeDtypeStruct((1, W), F32), jax.ShapeDtypeStruct((1, W), F32),
                   jax.ShapeDtypeStruct((1, W), F32)],
        compiler_params=_cp(("arbitrary",)),
    )(dcat, p_cv, cw, cb, clw, clb)


def _readout_bwd(T, dcat, y, kd, rw, rk, gw, lnw, lnb):
    def kern(dc_ref, y_ref, kd_ref, rw_ref, rk_ref, gw_ref, lw_ref, lb_ref,
             dy_o, dr_o, dv_o, dkb_o, dgd_o, drk_o, dgw_o, dlw_o, dlb_o):
        i = pl.program_id(0)
        first = i == 0
        e = _e128(F32)
        r = rw_ref[:, 0:512]
        v = rw_ref[:, 1024:1536]
        kbar = 0.5 * (kd_ref[0] + kd_ref[1])
        rk = rk_ref[...]
        ro = _readout_fwd(y_ref[0] + y_ref[1], r, v, rw_ref[:, 2048:2304], kbar, rk, gw_ref[...],
                          lw_ref[...], lb_ref[...], e)
        dout = dc_ref[...]
        dgg = dout * (ro["yn"] + ro["q"] * v)
        t1 = dout * ro["gg"]
        yhat = ro["yhat"]
        _acc(dlw_o, _colsum(t1 * yhat), first)
        _acc(dlb_o, _colsum(t1), first)
        dyh = t1 * lw_ref[...]
        dy_o[...] = ro["rstd"] * (dyh - _segsum(dyh, e) * (1.0 / HS) - yhat * (_segsum(dyh * yhat, e) * (1.0 / HS)))
        dq = _segsum(t1 * v, e)
        dv_o[...] = t1 * ro["q"]
        dr_o[...] = dq * kbar * rk
        dkb_o[...] = dq * r * rk
        _acc(drk_o, _colsum(dq * r * kbar), first)
        sg = ro["sg"]
        dsg = lax.dot_general(dgg, gw_ref[...], _NT_DIMS, precision=HI, preferred_element_type=F32)
        dgd_o[...] = dsg * sg * (1.0 - sg)
        _acc(dgw_o, lax.dot_general(sg, dgg, _TN_DIMS, precision=HI, preferred_element_type=F32), first)

    tok = lambda i: (i, 0)
    f32s = lambda *s: jax.ShapeDtypeStruct(s, F32)
    return pl.pallas_call(
        kern, grid=(T.NT,), name="readout_bwd",
        in_specs=[_bs((TT, W), tok), T.tm2_spec(), T.tm2_spec(), _bs((TT, RWC), tok),
                  _row(W), _bs((GDW, W), lambda i: (0, 0)), _row(W), _row(W)],
        out_specs=[T.tm_spec(), _bs((TT, W), tok), _bs((TT, W), tok), _bs((TT, W), tok), _bs((TT, GDW), tok),
                   _row(W), _bs((GDW, W), lambda i: (0, 0)), _row(W), _row(W)],
        out_shape=[f32s(T.TTOT, T.BW), f32s(T.NTOK, W), f32s(T.NTOK, W), f32s(T.NTOK, W), f32s(T.NTOK, GDW),
                   f32s(1, W), f32s(GDW, W), f32s(1, W), f32s(1, W)],
        compiler_params=_cp(("arbitrary",)),
    )(dcat, y, kd, rw, rk, gw, lnw, lnb)


def _scan_bwd(T, dy, r, v, kk, dec, kd, bb, hist, fin):
    NP = T.BW // 128
    R = NP * HS
    SB = SCAN_BSUB
    NS = T.TTOT // SB
    NSC = TT // SB

    def tmap(d, g):
        s = NS - 1 - g
        rev = jnp.where(s < NSC, NSC - 1 - s, NS - 1 - (s - NSC))
        return jnp.where(d == 0, s, rev)

    def kern(dy_ref, r_ref, v_ref, kk_ref, dec_ref, kd_ref, bb_ref, h_ref, fin_ref,
             dr_o, dw_o, dk_o, dv_o, da_o, db_o, ds_ref, snext):
        d = pl.program_id(0)
        g = pl.program_id(1)

        @pl.when(g == 0)
        def _():
            ds_ref[...] = jnp.zeros_like(ds_ref)
            snext[...] = fin_ref[0]

        e = _e128(BF16)

        for t in range(SB - 1, -1, -1):
            row = jnp.where(d == 0, t, SB - 1 - t)
            sp = h_ref[0, t]
            st = snext[...] if t == SB - 1 else h_ref[0, t + 1]
            a_ = _expand(-kk_ref[pl.ds(row, 1), :], NP)
            b_ = _expand(bb_ref[0, pl.ds(row, 1), :], NP)
            k_ = _expand(kd_ref[0, pl.ds(row, 1), :], NP)
            sa = _segb(sp * a_, e)
            vc = _colb(v_ref[pl.ds(row, 1), :], NP, e)
            dyc = _colb(dy_ref[pl.ds(row, 1), :], NP, e)
            ds = ds_ref[...] + dyc * _expand(r_ref[pl.ds(row, 1), :], NP)
            dr_o[0, pl.ds(row, 1), :] = _pair_colsum(st * dyc, NP)
            dw_o[0, pl.ds(row, 1), :] = _pair_colsum(ds * sp, NP)
            dsa = _segb(ds * b_, e)
            db_o[0, pl.ds(row, 1), :] = _pair_colsum(ds * sa, NP)
            dvb = _segb(ds * k_, e)
            dv_o[0, pl.ds(row, 1), :] = _pair_colsum(_diag(dvb, NP), NP)
            dk_o[0, pl.ds(row, 1), :] = _pair_colsum(ds * vc, NP)
            da_o[0, pl.ds(row, 1), :] = _pair_colsum(sp * dsa, NP)
            ds_ref[...] = ds * _expand(dec_ref[0, pl.ds(row, 1), :], NP) + dsa * a_
        snext[...] = h_ref[0, 0]

    sh = _bs((SB, T.BW), lambda d, g: (tmap(d, g), 0))
    dr = _bs((1, SB, T.BW), lambda d, g: (d, tmap(d, g), 0))
    o2 = jax.ShapeDtypeStruct((2, T.TTOT, T.BW), F32)
    return pl.pallas_call(
        kern, grid=(2, NS), name="scan_bwd",
        in_specs=[sh, sh, sh, sh, dr, dr, dr, _bs((1, SB, R, 128), lambda d, g: (d, NS - 1 - g, 0, 0)),
                  _bs((1, R, 128), lambda d, g: (d, 0, 0))],
        out_specs=[dr] * 6,
        out_shape=[o2] * 6,
        scratch_shapes=[pltpu.VMEM((R, 128), F32), pltpu.VMEM((R, 128), F32)],
        compiler_params=_cp(("arbitrary", "arbitrary"), mb=48),
    )(dy, r, v, kk, dec, kd, bb, hist, fin)


def _prep_bwd(T, rw, dr_s, ddec, dkd, dv_s, da_s, dbb, dr_ro, dv_ro, dkbar, dgd, w0, w2, a0, a2, k_k, k_a):
    def kern(rw_ref, drs_ref, ddec_ref, dkd_ref, dvs_ref, das_ref, dbb_ref, drr_ref, dvr_ref, dkb_ref, dgd_ref,
             w0_ref, w2_ref, a0_ref, a2_ref, kk_ref, ka_ref,
             drw_o, dw0_o, dw2_o, da0_o, da2_o, dkk_o, dka_o):
        i = pl.program_id(0)
        first = i == 0
        e = _e128(F32)
        w0v, w2v, a0v, a2v = _load_prep_params(w0_ref, w2_ref, a0_ref, a2_ref)
        k_k = kk_ref[...]
        k_a = ka_ref[...]
        o = _prep_math(rw_ref[...], w0v, w2v, a0v, a2v, k_k, k_a, e)
        k, kk = o["k"], o["kk"]
        dkbh = 0.5 * dkb_ref[...]
        dk = jnp.zeros_like(k)
        dkk = -(das_ref[0] + das_ref[1])
        dka = jnp.zeros((1, W), F32)
        dwd, dad = [], []
        for d in (0, 1):
            iclr = o["iclr"][d]
            dkd_d = dkd_ref[d] + dkbh
            dbb_d = dbb_ref[d]
            dk = dk + dkd_d * (1.0 + (iclr - 1.0) * k_a)
            dka = dka + _colsum(dkd_d * k * (iclr - 1.0))
            dkk = dkk + dbb_d * iclr
            dicl = dkd_d * k * k_a + dbb_d * kk
            dpa = dicl * iclr * (1.0 - iclr)
            _acc(da0_o.at[d:d + 1, :], _colsum(dpa), first)
            dad.append(lax.dot_general(dpa, a2v[d], _NT_DIMS, precision=HI, preferred_element_type=F32))
            _acc(da2_o.at[d], lax.dot_general(o["ad"][d], dpa, _TN_DIMS, precision=HI, preferred_element_type=F32),
                 first)
            dpre = -ddec_ref[d] * o["dec"][d] * o["ex"][d] * _sigmoid(-o["pre"][d])
            _acc(dw0_o.at[d:d + 1, :], _colsum(dpre), first)
            th = o["th"][d]
            dth = lax.dot_general(dpre, w2v[d], _NT_DIMS, precision=HI, preferred_element_type=F32)
            _acc(dw2_o.at[d], lax.dot_general(th, dpre, _TN_DIMS, precision=HI, preferred_element_type=F32), first)
            dwd.append(dth * (1.0 - th * th))
        inv = o["inv"]
        kr = o["kr"]
        proj = _segsum(dkk * kr, e)
        dkr = dkk * inv - jnp.where(o["rt"] > 1e-12, kr * inv * inv * inv * proj, 0.0)
        dk = dk + dkr * k_k
        _acc(dkk_o, _colsum(dkr * k), first)
        _acc(dka_o, dka, first)
        dr = drs_ref[0] + drs_ref[1] + drr_ref[...]
        dv = dvs_ref[0] + dvs_ref[1] + dvr_ref[...]
        drw_o[...] = jnp.concatenate([dr, dk, dv, dwd[0], dwd[1], dad[0], dad[1], dgd_ref[...]], axis=1)

    tok = lambda i: (i, 0)
    f32s = lambda *s: jax.ShapeDtypeStruct(s, F32)
    p2 = lambda i: (0, 0)
    p3 = lambda i: (0, 0, 0)
    return pl.pallas_call(
        kern, grid=(T.NT,), name="prep_bwd",
        in_specs=[_bs((TT, RWC), tok)] + [T.tm2_spec()] * 6 + [_bs((TT, W), tok)] * 3 + [_bs((TT, GDW), tok)]
        + _prep_param_specs(),
        out_specs=[_bs((TT, RWC), tok), _bs((2, W), p2), _bs((2, LRW, W), p3), _bs((2, W), p2),
                   _bs((2, LRW, W), p3), _row(W), _row(W)],
        out_shape=[f32s(T.NTOK, RWC), f32s(2, W), f32s(2, LRW, W), f32s(2, W), f32s(2, LRW, W), f32s(1, W), f32s(1, W)],
        compiler_params=_cp(("arbitrary",), mb=56),
    )(rw, dr_s, ddec, dkd, dv_s, da_s, dbb, dr_ro, dv_ro, dkbar, dgd, w0, w2, a0, a2, k_k, k_a)


def _shift_bwd(T, drw, p_rw, mu_p, mu_n):
    def kern(d_ref, dp_ref, dn_ref, p_ref, pp_ref, pn_ref, mp_ref, mn_ref, dprw_o, dmp_o, dmn_o):
        i = pl.program_id(0)
        first = i == 0
        has_prev, has_next = _halo_masks(T, i)
        mp = mp_ref[...]
        mn = mn_ref[...]
        drw = d_ref[...]
        z = p_ref[...]
        zprev, znext = _neighbours(z, pp_ref[7:8, :] * has_prev, pn_ref[0:1, :] * has_next)
        _acc(dmp_o, _colsum(drw * (zprev - z)), first)
        _acc(dmn_o, _colsum(drw * (znext - z)), first)
        dprev, dnext = _neighbours(drw, dp_ref[7:8, :] * has_prev, dn_ref[0:1, :] * has_next)
        dprw_o[...] = (drw * (1.0 - mp - mn) + mp * dnext + mn * dprev).astype(BF16)

    tok = lambda i: (i, 0)
    prev, nxt = _halo_specs(T)
    f32s = lambda *s: jax.ShapeDtypeStruct(s, F32)
    return pl.pallas_call(
        kern, grid=(T.NT,), name="shift_bwd",
        in_specs=[_bs((TT, RWC), tok), prev, nxt, _bs((TT, RWC), tok), prev, nxt, _row(RWC), _row(RWC)],
        out_specs=[_bs((TT, RWC), tok), _row(RWC), _row(RWC)],
        out_shape=[jax.ShapeDtypeStruct((T.NTOK, RWC), BF16), f32s(1, RWC), f32s(1, RWC)],
        compiler_params=_cp(("arbitrary",), mb=56),
    )(drw, drw, drw, p_rw, p_rw, p_rw, mu_p, mu_n)


def _mix_in_bwd(T, dp_rw, dp_cv, xc, dx1, modrows, g, w_rw, w_cv):
    def kern(drw_ref, dcv_ref, x_ref, dx1_ref, mod_ref, g_ref, wr_ref, wc_ref, dxc_o, dmod_o, dg_o):
        i = pl.program_id(0)
        dh = (lax.dot_general(drw_ref[...], wr_ref[...], _NT_DIMS, preferred_element_type=F32)
              + lax.dot_general(dcv_ref[...], wc_ref[...], _NT_DIMS, preferred_element_type=F32))
        x = x_ref[...]
        s = lax.rsqrt(_rowmean(x * x) + EPS_RMS)
        xh = x * s
        gv = g_ref[...]
        q = i % T.TPS
        first_kind = jnp.logical_or(q == 0, q == 1)
        _acc(dmod_o.at[0, 0, 0:1, :], _colsum(dh), first_kind)
        _acc(dmod_o.at[0, 0, 1:2, :], _colsum(dh * (xh * gv)), first_kind)
        dn1 = dh * (1.0 + mod_ref[0, 0, 1:2, :])
        _acc(dg_o, _colsum(dn1 * xh), i == 0)
        dxh = dn1 * gv
        dxc_o[...] = dx1_ref[...] + s * (dxh - xh * _rowmean(dxh * xh))

    tok = lambda i: (i, 0)
    f32s = lambda *s: jax.ShapeDtypeStruct(s, F32)
    return pl.pallas_call(
        kern, grid=(T.NT,), name="mix_in_bwd",
        in_specs=[_bs((TT, RWC), tok), _bs((TT, CVC), tok), _bs((TT, D), tok), _bs((TT, D), tok), T.mod_spec(),
                  _row(D), _bs((D, RWC), lambda i: (0, 0)), _bs((D, CVC), lambda i: (0, 0))],
        out_specs=[_bs((TT, D), tok),
                   _bs((1, 1, 2, D), lambda i: (i // T.TPS, jnp.minimum(i % T.TPS, 1), 0, 0)), _row(D)],
        out_shape=[f32s(T.NTOK, D), f32s(T.B, 2, 2, D), f32s(1, D)],
        compiler_params=_cp(("arbitrary",)),
    )(dp_rw, dp_cv, xc, dx1, modrows, g, w_rw, w_cv)


def _matmul_tn(a, b, name, tk, nk, tn, amap=None, bmap=None, tm=1024):
    M = a.shape[1]
    N = b.shape[1]
    amap = amap or (lambda k: k)
    bmap = bmap or (lambda k: k)

    def kern(a_ref, b_ref, o_ref):
        _acc(o_ref, lax.dot_general(a_ref[...], b_ref[...], _TN_DIMS, preferred_element_type=F32),
             pl.program_id(2) == 0)

    return pl.pallas_call(
        kern, grid=(M // tm, N // tn, nk), name=name,
        in_specs=[_bs((tk, tm), lambda i, j, k: (amap(k), i)), _bs((tk, tn), lambda i, j, k: (bmap(k), j))],
        out_specs=_bs((tm, tn), lambda i, j, k: (i, j)),
        out_shape=jax.ShapeDtypeStruct((M, N), F32),
        compiler_params=_cp(("parallel", "parallel", "arbitrary")),
    )(a, b)


def _silu(x):
    return x * _sigmoid(x)


def _ada_fwd(c_all, c_ctx, ada_w, ada_b_blk):
    nb = c_all.shape[0]
    R = nb + 8
    ncol = ada_w.shape[1]

    def kern(c_ref, cc_ref, w_ref, b_ref, o_ref):
        lhs = jnp.concatenate([_silu(c_ref[...]), _silu(cc_ref[...]), jnp.zeros((7, D), F32)], axis=0)
        o_ref[...] = jnp.dot(lhs, w_ref[...], precision=HI, preferred_element_type=F32) + b_ref[...]

    return pl.pallas_call(
        kern, name="ada_fwd", out_shape=jax.ShapeDtypeStruct((R, ncol), F32),
        compiler_params=_cp(None, 40),
    )(c_all, c_ctx, ada_w, ada_b_blk)


def _ada_bwd(c_all, c_ctx, ada_w, ex, cx, ex_blk, cx_blk):
    nb = c_all.shape[0]
    ncol = ada_w.shape[1]

    def kern(c_ref, cc_ref, w_ref, ex_ref, cx_ref, exb_ref, cxb_ref, gw_o, gb_o, ds_o):
        lhs = jnp.concatenate([_silu(c_ref[...]), _silu(cc_ref[...]), jnp.zeros((7, D), F32)], axis=0)
        dmc_blk = _colsum(cxb_ref[...])
        rhs = jnp.concatenate([exb_ref[...], dmc_blk, jnp.zeros((7, ncol), F32)], axis=0)
        gw_o[...] = lax.dot_general(lhs, rhs, _TN_DIMS, precision=HI, preferred_element_type=F32)
        gb_o[...] = _colsum(ex_ref[...]) + _colsum(cx_ref[...])
        ds_o[...] = lax.dot_general(jnp.concatenate([dmc_blk, jnp.zeros((7, ncol), F32)], axis=0), w_ref[...],
                                    _NT_DIMS, precision=HI, preferred_element_type=F32)

    return pl.pallas_call(
        kern, name="ada_bwd",
        out_shape=[jax.ShapeDtypeStruct((D, ncol), F32), jax.ShapeDtypeStruct((1, ex.shape[1]), F32),
                   jax.ShapeDtypeStruct((8, D), F32)],
        compiler_params=_cp(None, 48),
    )(c_all, c_ctx, ada_w, ex, cx, ex_blk, cx_blk)


def _cctx_final(parts, c_ctx):
    def kern(p_ref, c_ref, o_ref):
        tot = p_ref[0, 0:1, :]
        for j in range(1, parts.shape[0]):
            tot = tot + p_ref[j, 0:1, :]
        c = c_ref[...]
        sg = _sigmoid(c)
        o_ref[...] = tot * (sg * (1.0 + c * (1.0 - sg)))

    return pl.pallas_call(kern, name="cctx_final", out_shape=jax.ShapeDtypeStruct((1, D), F32))(parts, c_ctx)


def _peer(kind, p, ix, iy, ic):
    if kind == "chips":
        return (p // 2, p % 2, ic)
    if kind == "all":
        return (p // 4, (p // 2) % 2, p % 2)
    return (ix, iy, p)


def _exchange(x, kind, bcast, name, chunks=1):
    npeer = {"chips": 4, "all": 8, "sib": 2}[kind]
    slab = x.shape if bcast else x.shape[1:]
    assert chunks == 1 or slab[0] == chunks

    def kern(x_ref, o_ref, send_sems, recv_sems, lsem):
        ix, iy, ic = lax.axis_index("x"), lax.axis_index("y"), lax.axis_index("c")
        me = {"chips": 2 * ix + iy, "all": 4 * ix + 2 * iy + ic, "sib": ic}[kind]
        own = pltpu.make_async_copy(x_ref if bcast else x_ref.at[me], o_ref.at[me], lsem)
        own.start()

        def part(ref, k):
            return ref if chunks == 1 else ref.at[k]

        def copy(p, k):
            return pltpu.make_async_remote_copy(
                src_ref=part(x_ref if bcast else x_ref.at[p], k), dst_ref=part(o_ref.at[me], k),
                send_sem=send_sems.at[p, k], recv_sem=recv_sems.at[me, k],
                device_id=_peer(kind, p, ix, iy, ic), device_id_type=MESH)

        def arrival(p, k):
            return pltpu.make_async_remote_copy(
                src_ref=part(x_ref if bcast else x_ref.at[p], k), dst_ref=part(o_ref.at[p], k),
                send_sem=send_sems.at[p, k], recv_sem=recv_sems.at[p, k],
                device_id=_peer(kind, p, ix, iy, ic), device_id_type=MESH)

        for p in range(npeer):
            @pl.when(me != p)
            def _():
                for k in range(chunks):
                    copy(p, k).start()
        for p in range(npeer):
            @pl.when(me != p)
            def _():
                for k in range(chunks):
                    arrival(p, k).wait_recv()
        for p in range(npeer):
            @pl.when(me != p)
            def _():
                for k in range(chunks):
                    copy(p, k).wait_send()
        own.wait()

    any_spec = pl.BlockSpec(memory_space=pl.ANY)
    return pl.pallas_call(
        kern, name=name, in_specs=[any_spec], out_specs=any_spec,
        out_shape=jax.ShapeDtypeStruct((npeer,) + tuple(slab), x.dtype),
        scratch_shapes=[pltpu.SemaphoreType.DMA((npeer, chunks)), pltpu.SemaphoreType.DMA((npeer, chunks)),
                        pltpu.SemaphoreType.DMA],
    )(x)


def _sum_slots(x, name):
    n, R, C = x.shape
    budget = (8 << 20) // (n * C * x.dtype.itemsize)
    tr = max([t for t in range(8, R + 1, 8) if R % t == 0 and t <= max(budget, 8)], default=R)

    def kern(x_ref, o_ref):
        tot = x_ref[0]
        for s in range(1, n):
            tot = tot + x_ref[s]
        o_ref[...] = tot

    return pl.pallas_call(
        kern, grid=(R // tr,), name=name,
        in_specs=[_bs((n, tr, C), lambda i: (0, i, 0))], out_specs=_bs((tr, C), lambda i: (i, 0)),
        out_shape=jax.ShapeDtypeStruct((R, C), x.dtype), compiler_params=_cp(("parallel",)),
    )(x)


def _sib_stream(x, me, name, add):
    K, R, C = x.shape[-3:]

    def kern(me_ref, *refs):
        if add:
            own_ref, send_ref, o_ref, rbuf, ssem, rsem, credit = refs
        else:
            send_ref, o_ref, rbuf, ssem, rsem, credit = refs
        k = pl.program_id(0)
        slot = k % 2
        sib = (lax.axis_index("x"), lax.axis_index("y"), 1 - lax.axis_index("c"))

        @pl.when(k >= 2)
        def _():
            pl.semaphore_wait(credit.at[slot], 1)

        src = send_ref.at[0, 0] if add else send_ref.at[0]
        cp = pltpu.make_async_remote_copy(src_ref=src, dst_ref=rbuf.at[slot], send_sem=ssem.at[slot],
                                          recv_sem=rsem.at[slot], device_id=sib, device_id_type=MESH)
        cp.start()
        cp.wait_recv()
        o_ref[0] = own_ref[0, 0] + rbuf[slot] if add else rbuf[slot]
        cp.wait_send()

        @pl.when(k + 2 < K)
        def _():
            pl.semaphore_signal(credit.at[slot], 1, device_id=sib, device_id_type=MESH)

    if add:
        in_specs = [_bs((1, 1, R, C), lambda k, me_ref: (me_ref[0], k, 0, 0)),
                    _bs((1, 1, R, C), lambda k, me_ref: (1 - me_ref[0], k, 0, 0))]
        args = (x, x)
    else:
        in_specs = [_bs((1, R, C), lambda k, me_ref: (k, 0, 0))]
        args = (x,)
    return pl.pallas_call(
        kern, name=name,
        grid_spec=pltpu.PrefetchScalarGridSpec(
            num_scalar_prefetch=1, grid=(K,), in_specs=in_specs,
            out_specs=_bs((1, R, C), lambda k, me_ref: (k, 0, 0)),
            scratch_shapes=[pltpu.VMEM((2, R, C), x.dtype), pltpu.SemaphoreType.DMA((2,)),
                            pltpu.SemaphoreType.DMA((2,)), pltpu.SemaphoreType.REGULAR((2,))]),
        out_shape=jax.ShapeDtypeStruct((K, R, C), x.dtype),
        compiler_params=_cp(("arbitrary",)),
    )(me, *args)


def _adamw(w, g, m, v, name):
    shape = w.shape
    last = shape[-1]
    rows = 1
    for s in shape[:-1]:
        rows *= s
    w2, g2, m2, v2 = (t.reshape(rows, last) for t in (w, g, m, v))
    tr = rows
    for cand in (256, 128, 64, 32, 16, 8):
        if rows > cand and rows % cand == 0 and rows * last > (1 << 18):
            tr = cand
            break
    c1 = 1.0 - ADAM_B1 ** ADAM_STEP
    c2 = 1.0 - ADAM_B2 ** ADAM_STEP

    def kern(w_ref, g_ref, m_ref, v_ref, d_o, m_o, v_o):
        gv = g_ref[...]
        mn = ADAM_B1 * m_ref[...] + (1.0 - ADAM_B1) * gv
        vn = ADAM_B2 * v_ref[...] + (1.0 - ADAM_B2) * (gv * gv)
        m_o[...] = mn
        v_o[...] = vn
        d_o[...] = -ADAM_LR * ((mn / c1) / (jnp.sqrt(vn / c2) + ADAM_EPS) + ADAM_WD * w_ref[...])

    spec = _bs((tr, last), lambda i: (i, 0))
    o = jax.ShapeDtypeStruct((rows, last), F32)
    d, mn, vn = pl.pallas_call(
        kern, grid=(rows // tr,), name=name, in_specs=[spec] * 4, out_specs=[spec] * 3, out_shape=[o, o, o],
        compiler_params=_cp(("parallel",)),
    )(w2, g2, m2, v2)
    return d.reshape(shape), mn.reshape(shape), vn.reshape(shape)


_WEIGHT_NAMES = ("c_ctx", "ada_w", "ada_b", "mix_pre_g", "mix_post_g", "mlp_pre_g", "mlp_post_g", "w_in", "mu_prev",
                 "mu_next", "decay_w0", "decay_w2", "iclr_a0", "iclr_a2", "k_k", "k_a", "r_k", "gate_w2", "lnx_w",
                 "lnx_b", "conv_w", "conv_b", "conv_ln_w", "conv_ln_b", "w_out", "mlp_w1", "mlp_w2")


def _pack_rows(parts, cols=512):
    flat = jnp.concatenate([p.reshape(-1) for p in parts])
    rows = -(-flat.shape[0] // cols)
    rows = -(-rows // 16) * 16
    flat = jnp.pad(flat, (0, rows * cols - flat.shape[0]))
    return flat.reshape(rows, cols)


def _pack2d(parts, cols=512):
    return jnp.concatenate([p.reshape(-1, cols) for p in parts], axis=0)


def _unpack2d(buf, shapes):
    out = []
    off = 0
    for s in shapes:
        n = 1
        for d in s:
            n *= d
        n //= buf.shape[1]
        out.append(buf[off:off + n].reshape(s))
        off += n
    return out


def _unpack(flat, shapes):
    out = []
    off = 0
    for s in shapes:
        n = 1
        for d in s:
            n *= d
        out.append(flat[off:off + n].reshape(s))
        off += n
    return out


def _local_step(T, xc, tgt, modrows, P):
    p_rw, p_cv, h = _mix_in(T, xc, modrows, P["mix_pre_g"], P["w_rw"], P["w_cv"])
    prep_params = (P["w0"], P["w2"], P["a0"], P["a2"], P["k_k"], P["k_a"])
    r, v, kk, dec, kd, bb, rw = _rwkv_prep(T, p_rw, P["mu_p"], P["mu_n"], *prep_params)
    y, fin, hist = _scan_fwd(T, r, v, kk, dec, kd, bb)
    ro_params = (P["r_k"], P["gate_w2"], P["lnx_w"], P["lnx_b"])
    cv_params = (P["conv_w"], P["conv_b"], P["conv_ln_w"], P["conv_ln_b"])
    cat, mix, x1 = _mix_out(T, y, kd, rw, p_cv, xc, modrows, *ro_params, *cv_params, P["mix_post_g"], P["w_out"])
    m, h2 = _mlp_fwd(T, x1, modrows, P["mlp_pre_g"], P["w1"], P["w2m"])
    loss_acc, dm, dx2, dg2, d_mlp_post = _loss_head(T, m, x1, tgt, modrows, P["mlp_post_g"])
    fact, da, dh2 = _mlp_bwd(T, h2, dm, P["w1"], P["w2m"])
    dx1, dmod2, d_mlp_pre = _mlp_in_bwd(T, dh2, x1, dx2, modrows, P["mlp_pre_g"])
    dmix, dcat, dg1, d_mix_post = _mix_post_bwd(T, dx1, mix, modrows, P["mix_post_g"], P["w_out"])
    dp_cv, d_conv_w, d_conv_b, d_cln_w, d_cln_b = _conv_bwd(T, dcat, p_cv, *cv_params)
    dy, dr_ro, dv_ro, dkbar, dgd, d_r_k, d_gate, d_lnx_w, d_lnx_b = _readout_bwd(T, dcat, y, kd, rw, *ro_params)
    dr_s, ddec, dkd, dv_s, da_s, dbb = _scan_bwd(T, dy, r, v, kk, dec, kd, bb, hist, fin)
    drw, d_w0, d_w2, d_a0, d_a2, d_k_k, d_k_a = _prep_bwd(T, rw, dr_s, ddec, dkd, dv_s, da_s, dbb, dr_ro, dv_ro,
                                                          dkbar, dgd, *prep_params)
    dp_rw, d_mu_p, d_mu_n = _shift_bwd(T, drw, p_rw, P["mu_p"], P["mu_n"])
    dxc, dmod1, d_mix_pre = _mix_in_bwd(T, dp_rw, dp_cv, xc, dx1, modrows, P["mix_pre_g"], P["w_rw"], P["w_cv"])
    nk = T.NTOK // TT
    dw_rw = _matmul_tn(h, dp_rw, "dw_in_rw", TT, nk, 768)
    dw_cv = _matmul_tn(h, dp_cv, "dw_in_cv", TT, nk, 1024)
    dw_out = _matmul_tn(cat, dmix, "dw_out", TT, T.NL, 1024, bmap=T.tok)
    dw1 = _matmul_tn(h2, da, "dw_mlp1", 512, T.NLAT // 512, 1024)
    dw2m = _matmul_tn(fact, dm, "dw_mlp2", 512, T.NLAT // 512, 1024)
    small = dict(mix_pre_g=d_mix_pre, mix_post_g=d_mix_post, mlp_pre_g=d_mlp_pre, mlp_post_g=d_mlp_post,
                 mu_p=d_mu_p, mu_n=d_mu_n, w0=d_w0, w2=d_w2, a0=d_a0, a2=d_a2, k_k=d_k_k, k_a=d_k_a, r_k=d_r_k,
                 gate_w2=d_gate, lnx_w=d_lnx_w, lnx_b=d_lnx_b, conv_w=d_conv_w, conv_b=d_conv_b,
                 conv_ln_w=d_cln_w, conv_ln_b=d_cln_b)
    big = dict(w_rw=dw_rw, w_cv=dw_cv, w_out=dw_out, w1=dw1, w2m=dw2m)
    dmods = dict(dmod1=dmod1, dg1=dg1, dmod2=dmod2, dg2=dg2)
    return loss_acc[0, 0], dxc, small, big, dmods


_SMALL_ORDER = ("mix_pre_g", "mix_post_g", "mlp_pre_g", "mlp_post_g", "mu_p", "mu_n", "w0", "w2", "a0", "a2", "k_k",
                "k_a", "r_k", "gate_w2", "lnx_w", "lnx_b", "conv_w", "conv_b", "conv_ln_w", "conv_ln_b")


def kernel(x, c, ctx, c_ctx, ada_w, ada_b, mix_pre_g, mix_post_g, mlp_pre_g, mlp_post_g, w_in, mu_prev, mu_next, decay_w0, decay_w2, iclr_a0, iclr_a2, k_k, k_a, r_k, gate_w2, lnx_w, lnx_b, conv_w, conv_b, conv_ln_w, conv_ln_b, w_out, mlp_w1, mlp_w2, loss_target, m_c_ctx, m_ada_w, m_ada_b, m_mix_pre_g, m_mix_post_g, m_mlp_pre_g, m_mlp_post_g, m_w_in, m_mu_prev, m_mu_next, m_decay_w0, m_decay_w2, m_iclr_a0, m_iclr_a2, m_k_k, m_k_a, m_r_k, m_gate_w2, m_lnx_w, m_lnx_b, m_conv_w, m_conv_b, m_conv_ln_w, m_conv_ln_b, m_w_out, m_mlp_w1, m_mlp_w2, v_c_ctx, v_ada_w, v_ada_b, v_mix_pre_g, v_mix_post_g, v_mlp_pre_g, v_mlp_post_g, v_w_in, v_mu_prev, v_mu_next, v_decay_w0, v_decay_w2, v_iclr_a0, v_iclr_a2, v_k_k, v_k_a, v_r_k, v_gate_w2, v_lnx_w, v_lnx_b, v_conv_w, v_conv_b, v_conv_ln_w, v_conv_ln_b, v_w_out, v_mlp_w1, v_mlp_w2):
    weights = dict(zip(_WEIGHT_NAMES, (c_ctx, ada_w, ada_b, mix_pre_g, mix_post_g, mlp_pre_g, mlp_post_g, w_in, mu_prev, mu_next, decay_w0, decay_w2, iclr_a0, iclr_a2, k_k, k_a, r_k, gate_w2, lnx_w, lnx_b, conv_w, conv_b, conv_ln_w, conv_ln_b, w_out, mlp_w1, mlp_w2)))
    moms = dict(zip(_WEIGHT_NAMES, (m_c_ctx, m_ada_w, m_ada_b, m_mix_pre_g, m_mix_post_g, m_mlp_pre_g, m_mlp_post_g, m_w_in, m_mu_prev, m_mu_next, m_decay_w0, m_decay_w2, m_iclr_a0, m_iclr_a2, m_k_k, m_k_a, m_r_k, m_gate_w2, m_lnx_w, m_lnx_b, m_conv_w, m_conv_b, m_conv_ln_w, m_conv_ln_b, m_w_out, m_mlp_w1, m_mlp_w2)))
    vars_ = dict(zip(_WEIGHT_NAMES, (v_c_ctx, v_ada_w, v_ada_b, v_mix_pre_g, v_mix_post_g, v_mlp_pre_g, v_mlp_post_g, v_w_in, v_mu_prev, v_mu_next, v_decay_w0, v_decay_w2, v_iclr_a0, v_iclr_a2, v_k_k, v_k_a, v_r_k, v_gate_w2, v_lnx_w, v_lnx_b, v_conv_w, v_conv_b, v_conv_ln_w, v_conv_ln_b, v_w_out, v_mlp_w1, v_mlp_w2)))

    B, t_lat, _ = x.shape
    assert ctx.shape[1] == TT and t_lat % TT == 0 and (t_lat * B) % MT == 0
    T = _Tiles(B, t_lat)
    ix, iy, ic = lax.axis_index("x"), lax.axis_index("y"), lax.axis_index("c")
    chip = 2 * ix + iy
    dev = 4 * ix + 2 * iy + ic
    nsh = 4
    in_sh = w_in.shape[2]
    ada_sh = ada_w.shape[2]
    lane_sh = decay_w0.shape[2]

    big_parts = (w_in[0], w_out[0], mlp_w1[0], mlp_w2[0])
    big_shapes = [p.shape for p in big_parts]
    wg = _exchange(_pack2d([p.astype(BF16) for p in big_parts]), "chips", True, "gather_big_weights")
    per = [_unpack2d(wg[j], big_shapes) for j in range(nsh)]
    w_in_f = jnp.concatenate([per[j][0] for j in range(nsh)], axis=1)
    w_out_f = jnp.concatenate([per[j][1] for j in range(nsh)], axis=0)
    w1_f = jnp.concatenate([per[j][2] for j in range(nsh)], axis=1)
    w2_f = jnp.concatenate([per[j][3] for j in range(nsh)], axis=0)
    w_in_p = _pad_cols(w_in_f, w_in_f.shape[1])

    sm_parts = (decay_w0[0], decay_w2[0], iclr_a0[0], iclr_a2[0], gate_w2[0], conv_w[0])
    sm_shapes = [p.shape for p in sm_parts]
    sg = _exchange(_pack_rows(sm_parts), "chips", True, "gather_small_weights")
    pers = [_unpack(sg[j].reshape(-1), sm_shapes) for j in range(nsh)]
    w0_f, w2_f_, a0_f, a2_f, gate_f, convw_f = (jnp.concatenate([pers[j][t] for j in range(nsh)], axis=-1)
                                                for t in range(6))

    def pad_rows(a, n):
        return jnp.pad(a, [(0, 0)] * (a.ndim - 2) + [(0, n - a.shape[-2]), (0, 0)])

    P = dict(
        w_rw=w_in_p[:, :RWC], w_cv=w_in_p[:, RWC:], w_out=w_out_f, w1=w1_f, w2m=w2_f,
        mix_pre_g=mix_pre_g, mix_post_g=mix_post_g, mlp_pre_g=mlp_pre_g, mlp_post_g=mlp_post_g,
        mu_p=_pad_cols(mu_prev, mu_prev.shape[1]), mu_n=_pad_cols(mu_next, mu_next.shape[1]),
        w0=w0_f, w2=pad_rows(w2_f_, LRW), a0=a0_f, a2=pad_rows(a2_f, LRW), k_k=k_k, k_a=k_a,
        r_k=r_k.reshape(1, W), gate_w2=pad_rows(gate_f, GDW), lnx_w=lnx_w, lnx_b=lnx_b,
        conv_w=pad_rows(convw_f, 32), conv_b=conv_b, conv_ln_w=conv_ln_w, conv_ln_b=conv_ln_b)

    c_ctx2 = c_ctx.reshape(1, D)
    c_all = _exchange(jnp.pad(c, ((0, 8 - B), (0, 0))), "all", True, "gather_c")[:, :B].reshape(8 * B, D)
    ada_b_blk = lax.dynamic_slice(ada_b, (0, chip * ada_sh), (1, ada_sh))
    mod_blk = _ada_fwd(c_all, c_ctx2, ada_w[0], ada_b_blk)
    mod_g = _exchange(mod_blk, "chips", True, "gather_mod")
    mod_all = jnp.concatenate([mod_g[j] for j in range(nsh)], axis=1)
    mod_x = lax.dynamic_slice(mod_all, (dev * B, 0), (B, 6 * D)).reshape(B, 6, D)
    mod_c = jnp.broadcast_to(mod_all[8 * B].reshape(1, 6, D), (B, 6, D))
    modrows = jnp.stack([mod_c, mod_x], axis=1)

    xc = jnp.concatenate([ctx, x], axis=1).reshape(T.NTOK, D)
    tgt = loss_target.reshape(T.NLAT, D)
    loss_loc, dxc, small, big, dm_ = _local_step(T, xc, tgt, modrows, P)
    loss = lax.psum(loss_loc, ("x", "y", "c"))
    grad_x = dxc.reshape(B, T.TTOT, D)[:, TT:, :]

    dmod_x = jnp.concatenate([dm_["dmod1"][:, 1], dm_["dg1"], dm_["dmod2"], dm_["dg2"]], axis=1)
    dmod_c = jnp.concatenate([dm_["dmod1"][:, 0], jnp.zeros((B, 4, D), F32)], axis=1)
    dpack = jnp.concatenate([dmod_x.reshape(B, 6 * D), dmod_c.reshape(B, 6 * D)], axis=0)
    dg = _exchange(dpack, "all", True, "gather_dmod")
    ex = dg[:, :B].reshape(8 * B, 6 * D)
    cx = dg[:, B:].reshape(8 * B, 6 * D)
    ex_blk = lax.dynamic_slice(ex, (0, chip * ada_sh), (8 * B, ada_sh))
    cx_blk = lax.dynamic_slice(cx, (0, chip * ada_sh), (8 * B, ada_sh))
    g_ada_w, g_ada_b, dscc = _ada_bwd(c_all, c_ctx2, ada_w[0], ex, cx, ex_blk, cx_blk)
    dscc_g = _exchange(dscc, "chips", True, "gather_dcctx")
    g_c_ctx = _cctx_final(dscc_g, c_ctx2).reshape(D)

    small = dict(small, mu_p=_unpad_cols(small["mu_p"], mu_prev.shape[1]),
                 mu_n=_unpad_cols(small["mu_n"], mu_next.shape[1]),
                 w2=small["w2"][:, :decay_w2.shape[2]], a2=small["a2"][:, :iclr_a2.shape[2]],
                 gate_w2=small["gate_w2"][:gate_w2.shape[1]], conv_w=small["conv_w"][:KCONV])
    sm_list = [small[n] for n in _SMALL_ORDER]
    sm_shapes2 = [a.shape for a in sm_list]
    me1 = ic.reshape(1).astype(jnp.int32)
    sm_pack = _pack_rows(sm_list)
    sm_pair = _sib_stream(jnp.stack([sm_pack, sm_pack])[:, None], me1, "sib_small_grads", True)[0]
    sm_tot = _sum_slots(_exchange(sm_pair, "chips", True, "gather_small_grads"), "sum_small_grads")
    S = dict(zip(_SMALL_ORDER, _unpack(sm_tot.reshape(-1), sm_shapes2)))

    def shard_last(a):
        return lax.dynamic_slice_in_dim(a, chip * lane_sh, lane_sh, axis=a.ndim - 1)

    grads = dict(
        c_ctx=g_c_ctx, ada_w=g_ada_w[None], ada_b=g_ada_b,
        mix_pre_g=S["mix_pre_g"], mix_post_g=S["mix_post_g"], mlp_pre_g=S["mlp_pre_g"], mlp_post_g=S["mlp_post_g"],
        mu_prev=S["mu_p"], mu_next=S["mu_n"],
        decay_w0=shard_last(S["w0"])[None], decay_w2=shard_last(S["w2"])[None],
        iclr_a0=shard_last(S["a0"])[None], iclr_a2=shard_last(S["a2"])[None],
        k_k=S["k_k"], k_a=S["k_a"], r_k=S["r_k"].reshape(r_k.shape),
        gate_w2=shard_last(S["gate_w2"])[None], lnx_w=S["lnx_w"], lnx_b=S["lnx_b"],
        conv_w=shard_last(S["conv_w"])[None], conv_b=S["conv_b"], conv_ln_w=S["conv_ln_w"],
        conv_ln_b=S["conv_ln_b"])

    dw_in_f = _unpad_cols(jnp.concatenate([big["w_rw"], big["w_cv"]], axis=1), w_in_f.shape[1])
    oshape = w_out.shape[1]
    mshape = mlp_w1.shape[2]
    slabs = [
        _pack2d([dw_in_f[:, in_sh * j:in_sh * (j + 1)], big["w_out"][oshape * j:oshape * (j + 1)],
                 big["w1"][:, mshape * j:mshape * (j + 1)], big["w2m"][mshape * j:mshape * (j + 1)]])
        for j in range(nsh)]
    nck = 3
    lr, lc = slabs[0].shape
    hr = lr // 2
    cr = hr // nck
    assert cr * 2 * nck == lr and cr % 8 == 0
    halves = jnp.stack([jnp.concatenate([s[h * hr:(h + 1) * hr].reshape(nck, cr, lc) for s in slabs], axis=0)
                        for h in (0, 1)])
    pair = _sib_stream(halves, me1, "sib_reduce_grads", True)
    mine = _sum_slots(_exchange(pair.reshape(nsh, nck * cr, lc), "chips", False, "reduce_big_grads"), "sum_big_grads")
    mine = mine.reshape(nck, cr, lc)
    other = _sib_stream(mine, me1, "sib_swap_grads", False)
    tot = jnp.where(ic == 0, jnp.concatenate([mine, other], axis=0), jnp.concatenate([other, mine], axis=0))
    g_w_in, g_w_out, g_w1, g_w2 = _unpack2d(tot.reshape(lr, lc), big_shapes)
    grads.update(w_in=g_w_in[None], w_out=g_w_out[None], mlp_w1=g_w1[None], mlp_w2=g_w2[None])

    deltas, new_m, new_v = {}, {}, {}
    for n in _WEIGHT_NAMES:
        g = grads[n].reshape(weights[n].shape)
        grads[n] = g
        deltas[n], new_m[n], new_v[n] = _adamw(weights[n], g, moms[n], vars_[n], "adamw_" + n)

    return (loss, grad_x, *[grads[n] for n in _WEIGHT_NAMES], *[deltas[n] for n in _WEIGHT_NAMES],
            *[new_m[n] for n in _WEIGHT_NAMES], *[new_v[n] for n in _WEIGHT_NAMES])
```

```python
import functools

import jax
import jax.numpy as jnp
from jax import lax
from jax.experimental import pallas as pl
from jax.experimental.pallas import tpu as pltpu

F32 = jnp.float32
BF16 = jnp.bfloat16
HI = lax.Precision.HIGHEST

D = 1024
W = 512
HS = 64
RWC = 2304
CVC = 1024
GDW = 256
LRW = 128
DFF = 4096
TT = 256
LINE = 64
KCONV = 31
EPS_RMS = 1e-6
EPS_LN = 1e-5
EPS_GN = 64e-5
SCAN_CH = 128
SCAN_G = 8
SCAN_BSUB = 16

ADAM_LR = 0.001
ADAM_B1 = 0.9
ADAM_B2 = 0.999
ADAM_EPS = 1e-08
ADAM_WD = 0.01
ADAM_STEP = 10

_SEGS = ((0, 1536, 1536), (1536, 64, 128), (1600, 64, 128), (1664, 64, 128), (1728, 64, 128),
         (1792, 160, 256), (1952, 1024, 1024))

MESH = pl.DeviceIdType.MESH


def _bs(shape, imap):
    return pl.BlockSpec(shape, imap)


def _cp(sem=None, mb=48):
    return pltpu.CompilerParams(dimension_semantics=sem, vmem_limit_bytes=mb << 20)


def _pad_cols(a, ncols):
    out = []
    for s, w, pw in _SEGS:
        if s >= ncols:
            break
        piece = a[..., s:s + w]
        if pw > w:
            piece = jnp.pad(piece, [(0, 0)] * (a.ndim - 1) + [(0, pw - w)])
        out.append(piece)
    return jnp.concatenate(out, axis=-1)


def _unpad_cols(a, ncols):
    out = []
    off = 0
    for s, w, pw in _SEGS:
        if s >= ncols:
            break
        out.append(a[..., off:off + w])
        off += pw
    return jnp.concatenate(out, axis=-1)


def _sigmoid(x):
    return 1.0 / (1.0 + jnp.exp(-x))


def _softplus(x):
    return jnp.maximum(x, 0.0) + jnp.log(1.0 + jnp.exp(-jnp.abs(x)))


def _e128(dtype):
    r = lax.broadcasted_iota(jnp.int32, (128, 128), 0) >= HS
    c = lax.broadcasted_iota(jnp.int32, (128, 128), 1) >= HS
    return (r == c).astype(dtype)


def _segsum(x, e):
    hi = x.astype(BF16)
    lo = (x - hi.astype(F32)).astype(BF16)
    return jnp.concatenate(
        [jnp.dot(hi[:, 128 * g:128 * (g + 1)], e, preferred_element_type=F32)
         + jnp.dot(lo[:, 128 * g:128 * (g + 1)], e, preferred_element_type=F32) for g in range(4)], axis=1)


_NT_DIMS = (((1,), (1,)), ((), ()))
_TN_DIMS = (((0,), (0,)), ((), ()))


def _bdot(a, b, dims=None):
    a = a.astype(BF16)
    b = b.astype(BF16)
    if dims is None:
        return jnp.dot(a, b, preferred_element_type=F32)
    return lax.dot_general(a, b, dims, preferred_element_type=F32)


def _colsum(x):
    return jnp.sum(x, axis=0, keepdims=True)


def _rowmean(x):
    return jnp.mean(x, axis=-1, keepdims=True)


def _diag(x, npairs):
    row = lax.broadcasted_iota(jnp.int32, (HS, 128), 0)
    lane = lax.broadcasted_iota(jnp.int32, (HS, 128), 1) & (HS - 1)
    keep = jnp.broadcast_to((lane == row)[None], (npairs, HS, 128))
    return jnp.where(keep, x.reshape(npairs, HS, 128), 0.0).reshape(npairs * HS, 128)


def _segb(x, e):
    return jnp.dot(x.astype(BF16), e, preferred_element_type=F32)


_segb1 = _segb


def _expand(row, npairs):
    return jnp.concatenate([jnp.broadcast_to(row[:, 128 * j:128 * (j + 1)], (HS, 128)) for j in range(npairs)], axis=0)


def _colb(row, npairs, e):
    return _segb1(_diag(_expand(row, npairs), npairs), e)


def _pair_colsum(x, npairs):
    return jnp.concatenate([_colsum(x[HS * j:HS * (j + 1)]) for j in range(npairs)], axis=1)


def _conv_pos():
    return lax.broadcasted_iota(jnp.int32, (TT, W), 0) & (LINE - 1)


def _shifted(u, s, pos):
    if s == 0:
        return u
    sh = pltpu.roll(u, (-s) % TT, 0)
    valid = jnp.logical_and(pos + s >= 0, pos + s < LINE)
    return jnp.where(valid, sh, 0.0)


def _acc(ref, val, first):
    @pl.when(first)
    def _():
        ref[...] = jnp.zeros(ref.shape, ref.dtype)
    ref[...] += val


class _Tiles:
    def __init__(self, B, t_lat):
        self.B = B
        self.NLT = t_lat // TT
        self.TPS = self.NLT + 1
        self.NT = B * self.TPS
        self.NL = B * self.NLT
        self.NTOK = self.NT * TT
        self.NLAT = self.NL * TT
        self.TTOT = self.TPS * TT
        self.BW = B * W

    def b(self, i):
        return i // self.TPS

    def q(self, i):
        return i % self.TPS

    def lat(self, i):
        return (i // self.TPS) * self.NLT + jnp.maximum(i % self.TPS - 1, 0)

    def tok(self, l):
        return (l // self.NLT) * self.TPS + 1 + l % self.NLT

    def mod_spec(self):
        return _bs((1, 1, 6, D), lambda i: (i // self.TPS, jnp.minimum(i % self.TPS, 1), 0, 0))

    def tm_spec(self):
        return _bs((TT, W), lambda i: (i % self.TPS, i // self.TPS))

    def tm2_spec(self):
        return _bs((2, TT, W), lambda i: (0, i % self.TPS, i // self.TPS))


def _row(shape_last):
    return _bs((1, shape_last), lambda i: (0, 0))


def _mix_in(T, xc, modrows, g, w_rw, w_cv):
    def kern(x_ref, mod_ref, g_ref, wr_ref, wc_ref, prw_ref, pcv_ref, h_ref):
        x = x_ref[...]
        s = lax.rsqrt(_rowmean(x * x) + EPS_RMS)
        h = (x * s * g_ref[...]) * (1.0 + mod_ref[0, 0, 1:2, :]) + mod_ref[0, 0, 0:1, :]
        hb = h.astype(BF16)
        h_ref[...] = hb
        prw_ref[...] = jnp.dot(hb, wr_ref[...], preferred_element_type=F32)
        pcv_ref[...] = jnp.dot(hb, wc_ref[...], preferred_element_type=F32)

    return pl.pallas_call(
        kern, grid=(T.NT,), name="mix_in",
        in_specs=[_bs((TT, D), lambda i: (i, 0)), T.mod_spec(), _row(D),
                  _bs((D, RWC), lambda i: (0, 0)), _bs((D, CVC), lambda i: (0, 0))],
        out_specs=[_bs((TT, RWC), lambda i: (i, 0)), _bs((TT, CVC), lambda i: (i, 0)), _bs((TT, D), lambda i: (i, 0))],
        out_shape=[jax.ShapeDtypeStruct((T.NTOK, RWC), F32), jax.ShapeDtypeStruct((T.NTOK, CVC), F32),
                   jax.ShapeDtypeStruct((T.NTOK, D), BF16)],
        compiler_params=_cp(("parallel",)),
    )(xc, modrows, g, w_rw, w_cv)


def _halo_specs(T):
    nb8 = T.NTOK // 8
    prev = _bs((8, RWC), lambda i: (jnp.maximum(i * (TT // 8) - 1, 0), 0))
    nxt = _bs((8, RWC), lambda i: (jnp.minimum((i + 1) * (TT // 8), nb8 - 1), 0))
    return prev, nxt


def _halo_masks(T, i):
    q = i % T.TPS
    has_prev = jnp.logical_and(q != 0, q != 1).astype(F32)
    has_next = jnp.logical_and(q != 0, q != T.TPS - 1).astype(F32)
    return has_prev, has_next


def _neighbours(z, prev_row, next_row):
    rowi = lax.broadcasted_iota(jnp.int32, z.shape, 0)
    zprev = jnp.where(rowi == 0, prev_row, pltpu.roll(z, 1, 0))
    znext = jnp.where(rowi == TT - 1, next_row, pltpu.roll(z, TT - 1, 0))
    return zprev, znext


def _prep_math(rw, w0, w2, a0, a2, k_k, k_a, e):
    r = rw[:, 0:512]
    k = rw[:, 512:1024]
    v = rw[:, 1024:1536]
    kr = k * k_k
    ss = _segsum(kr * kr, e)
    rt = jnp.sqrt(ss)
    inv = 1.0 / jnp.maximum(rt, 1e-12)
    kk = kr * inv
    o = dict(r=r, k=k, v=v, kr=kr, rt=rt, inv=inv, kk=kk, th=[], pre=[], ex=[], dec=[], iclr=[], kd=[], bb=[], ad=[])
    for d in (0, 1):
        wd = rw[:, 1536 + LRW * d:1536 + LRW * (d + 1)]
        ad = rw[:, 1792 + LRW * d:1792 + LRW * (d + 1)]
        th = jnp.tanh(wd)
        pre = w0[d] + _bdot(th, w2[d])
        ex = jnp.exp(-_softplus(-pre) - 0.5)
        dec = jnp.exp(-ex)
        iclr = _sigmoid(a0[d] + _bdot(ad, a2[d]))
        o["th"].append(th)
        o["pre"].append(pre)
        o["ex"].append(ex)
        o["dec"].append(dec)
        o["iclr"].append(iclr)
        o["ad"].append(ad)
        o["kd"].append(k * (1.0 + (iclr - 1.0) * k_a))
        o["bb"].append(kk * iclr)
    return o


def _load_prep_params(w0_ref, w2_ref, a0_ref, a2_ref):
    w0 = [w0_ref[0:1, :], w0_ref[1:2, :]]
    a0 = [a0_ref[0:1, :], a0_ref[1:2, :]]
    w2 = [w2_ref[0], w2_ref[1]]
    a2 = [a2_ref[0], a2_ref[1]]
    return w0, w2, a0, a2


def _prep_param_specs():
    return [_bs((2, W), lambda i: (0, 0)), _bs((2, LRW, W), lambda i: (0, 0, 0)),
            _bs((2, W), lambda i: (0, 0)), _bs((2, LRW, W), lambda i: (0, 0, 0)), _row(W), _row(W)]


def _rwkv_prep(T, p_rw, mu_p, mu_n, w0, w2, a0, a2, k_k, k_a):
    def kern(p_ref, pp_ref, pn_ref, mp_ref, mn_ref, w0_ref, w2_ref, a0_ref, a2_ref, kk_ref, ka_ref,
             r_o, v_o, kk_o, dec_o, kd_o, bb_o, rw_o):
        i = pl.program_id(0)
        has_prev, has_next = _halo_masks(T, i)
        z = p_ref[...]
        zprev, znext = _neighbours(z, pp_ref[7:8, :] * has_prev, pn_ref[0:1, :] * has_next)
        rw = z + mp_ref[...] * (zprev - z) + mn_ref[...] * (znext - z)
        rw_o[...] = rw
        w0v, w2v, a0v, a2v = _load_prep_params(w0_ref, w2_ref, a0_ref, a2_ref)
        o = _prep_math(rw, w0v, w2v, a0v, a2v, kk_ref[...], ka_ref[...], _e128(BF16))
        r_o[...] = o["r"]
        v_o[...] = o["v"]
        kk_o[...] = o["kk"]
        for d in (0, 1):
            dec_o[d] = o["dec"][d]
            kd_o[d] = o["kd"][d]
            bb_o[d] = o["bb"][d]

    prev, nxt = _halo_specs(T)
    tm = jax.ShapeDtypeStruct((T.TTOT, T.BW), F32)
    tm2 = jax.ShapeDtypeStruct((2, T.TTOT, T.BW), F32)
    return pl.pallas_call(
        kern, grid=(T.NT,), name="rwkv_prep",
        in_specs=[_bs((TT, RWC), lambda i: (i, 0)), prev, nxt, _row(RWC), _row(RWC)] + _prep_param_specs(),
        out_specs=[T.tm_spec(), T.tm_spec(), T.tm_spec(), T.tm2_spec(), T.tm2_spec(), T.tm2_spec(),
                   _bs((TT, RWC), lambda i: (i, 0))],
        out_shape=[tm, tm, tm, tm2, tm2, tm2, jax.ShapeDtypeStruct((T.NTOK, RWC), F32)],
        compiler_params=_cp(("parallel",)),
    )(p_rw, p_rw, p_rw, mu_p, mu_n, w0, w2, a0, a2, k_k, k_a)


def _scan_fwd(T, r, v, kk, dec, kd, bb):
    NP = T.BW // 128
    R = NP * HS
    NCH = T.TTOT // SCAN_CH
    NCC = TT // SCAN_CH
    G = SCAN_G
    NG = SCAN_CH // G
    NGRP = 4
    assert NG % NGRP == 0

    def tmap(d, i):
        rev = jnp.where(i < NCC, NCC - 1 - i, NCH - 1 - (i - NCC))
        return jnp.where(d == 0, i, rev)

    def kern(r_ref, v_ref, kk_ref, dec_ref, kd_ref, bb_ref, y_ref, fin_ref, hist_ref, ring, sems):
        d = pl.program_id(0)
        i = pl.program_id(1)

        @pl.when(i == 0)
        def _():
            ring[0] = jnp.zeros((R, 128), F32)

        e = _e128(BF16)

        def hist_copy(k):
            grp = k % NGRP
            return pltpu.make_async_copy(ring.at[pl.ds(grp * G, G)],
                                         hist_ref.at[d, pl.ds(i * SCAN_CH + k * G, G)], sems.at[grp])

        def body(k, carry):
            @pl.when(k >= NGRP - 1)
            def _():
                hist_copy(k - (NGRP - 1)).wait()

            base = (k % NGRP) * G
            for u in range(G):
                t = k * G + u
                row = jnp.where(d == 0, t, SCAN_CH - 1 - t)
                s = ring[base + u]
                sa = _segb(s * _expand(-kk_ref[pl.ds(row, 1), :], NP), e)
                vc = _colb(v_ref[pl.ds(row, 1), :], NP, e)
                s = (s * _expand(dec_ref[0, pl.ds(row, 1), :], NP) + sa * _expand(bb_ref[0, pl.ds(row, 1), :], NP)
                     + vc * _expand(kd_ref[0, pl.ds(row, 1), :], NP))
                ring[(base + u + 1) if u < G - 1 else ((k + 1) % NGRP) * G] = s
                yb = _segb(s * _expand(r_ref[pl.ds(row, 1), :], NP), e)
                y_ref[0, pl.ds(row, 1), :] = _pair_colsum(_diag(yb, NP), NP)
            hist_copy(k).start()
            return carry

        lax.fori_loop(0, NG, body, 0)
        for k in range(NG - (NGRP - 1), NG):
            hist_copy(k).wait()

        @pl.when(i == NCH - 1)
        def _():
            fin_ref[0] = ring[0]

    sh = _bs((SCAN_CH, T.BW), lambda d, i: (tmap(d, i), 0))
    dr = _bs((1, SCAN_CH, T.BW), lambda d, i: (d, tmap(d, i), 0))
    return pl.pallas_call(
        kern, grid=(2, NCH), name="scan_fwd",
        in_specs=[sh, sh, sh, dr, dr, dr],
        out_specs=[dr, _bs((1, R, 128), lambda d, i: (d, 0, 0)), pl.BlockSpec(memory_space=pl.ANY)],
        out_shape=[jax.ShapeDtypeStruct((2, T.TTOT, T.BW), F32), jax.ShapeDtypeStruct((2, R, 128), F32),
                   jax.ShapeDtypeStruct((2, T.TTOT, R, 128), F32)],
        scratch_shapes=[pltpu.VMEM((NGRP * G, R, 128), F32), pltpu.SemaphoreType.DMA((NGRP,))],
        compiler_params=_cp(("arbitrary", "arbitrary")),
    )(r, v, kk, dec, kd, bb)


def _readout_fwd(y, r, v, gd, kbar, rk, gw, lw, lb, e):
    mu = _segsum(y, e) * (1.0 / HS)
    yc = y - mu
    var = _segsum(yc * yc, e) * (1.0 / HS)
    rstd = lax.rsqrt(var + EPS_GN)
    yhat = yc * rstd
    yn = yhat * lw + lb
    q = _segsum(r * kbar * rk, e)
    sg = _sigmoid(gd)
    gg = _bdot(sg, gw)
    return dict(yhat=yhat, rstd=rstd, yn=yn, q=q, sg=sg, gg=gg, out=(yn + q * v) * gg)


def _conv_fwd(cva, cvb, cw_ref, cb, lw, lb, c=None):
    pos = _conv_pos()
    sgb = _sigmoid(cvb)
    u = cva * sgb
    if c is None:
        c = jnp.zeros_like(u)
        for j in range(KCONV):
            c = c + cw_ref[j:j + 1, :] * _shifted(u, j - KCONV // 2, pos)
        c = c + cb
    mu = _rowmean(c)
    cc = c - mu
    rstd = lax.rsqrt(_rowmean(cc * cc) + EPS_LN)
    chat = cc * rstd
    cn = chat * lw + lb
    scn = _sigmoid(cn)
    return dict(sgb=sgb, u=u, c=c, chat=chat, rstd=rstd, cn=cn, scn=scn, out=cn * scn, pos=pos)


def _mix_out(T, y, kd, rw, p_cv, xc, modrows, rk, gw, lnw, lnb, cw, cb, clw, clb, pg, w_out):
    tk = T.tok

    def kern(y_ref, kd_ref, rw_ref, pcv_ref, x_ref, mod_ref, rk_ref, gw_ref, lw_ref, lb_ref, cw_ref, cb_ref,
             clw_ref, clb_ref, pg_ref, wo_ref, cat_o, mix_o, x1_o, conv_o):
        e = _e128(BF16)
        ro = _readout_fwd(y_ref[0] + y_ref[1], rw_ref[:, 0:512], rw_ref[:, 1024:1536], rw_ref[:, 2048:2304],
                          0.5 * (kd_ref[0] + kd_ref[1]), rk_ref[...], gw_ref[...], lw_ref[...], lb_ref[...], e)
        cv = _conv_fwd(pcv_ref[:, 0:512], pcv_ref[:, 512:1024], cw_ref, cb_ref[...], clw_ref[...], clb_ref[...])
        conv_o[...] = cv["c"]
        catb = jnp.concatenate([ro["out"], cv["out"]], axis=1).astype(BF16)
        cat_o[...] = catb
        mix = jnp.dot(catb, wo_ref[...], preferred_element_type=F32)
        mix_o[...] = mix
        sm = lax.rsqrt(_rowmean(mix * mix) + EPS_RMS)
        x1_o[...] = x_ref[...] + mod_ref[0, 0, 2:3, :] * (mix * sm * pg_ref[...])

    lat = lambda l: (l, 0)
    return pl.pallas_call(
        kern, grid=(T.NL,), name="mix_out",
        in_specs=[_bs((2, TT, W), lambda l: (0, 1 + l % T.NLT, l // T.NLT)),
                  _bs((2, TT, W), lambda l: (0, 1 + l % T.NLT, l // T.NLT)),
                  _bs((TT, RWC), lambda l: (tk(l), 0)), _bs((TT, CVC), lambda l: (tk(l), 0)),
                  _bs((TT, D), lambda l: (tk(l), 0)),
                  _bs((1, 1, 6, D), lambda l: (l // T.NLT, 1, 0, 0)),
                  _row(W), _bs((GDW, W), lambda l: (0, 0)), _row(W), _row(W),
                  _bs((32, W), lambda l: (0, 0)), _row(W), _row(W), _row(W), _row(D),
                  _bs((D, D), lambda l: (0, 0))],
        out_specs=[_bs((TT, D), lat), _bs((TT, D), lat), _bs((TT, D), lat), _bs((TT, W), lat)],
        out_shape=[jax.ShapeDtypeStruct((T.NLAT, D), BF16), jax.ShapeDtypeStruct((T.NLAT, D), F32),
                   jax.ShapeDtypeStruct((T.NLAT, D), F32), jax.ShapeDtypeStruct((T.NLAT, W), F32)],
        compiler_params=_cp(("parallel",)),
    )(y, kd, rw, p_cv, xc, modrows, rk, gw, lnw, lnb, cw, cb, clw, clb, pg, w_out)


MT = 512
FC = 1024


def _mlp_fwd(T, x1, modrows, g, w1, w2):
    per_b = T.NLT * TT // MT

    def kern(x_ref, mod_ref, g_ref, w1_ref, w2_ref, m_o, h2_o, h2_s):
        f = pl.program_id(1)

        @pl.when(f == 0)
        def _():
            x = x_ref[...]
            s = lax.rsqrt(_rowmean(x * x) + EPS_RMS)
            h2 = (x * s * g_ref[...]) * (1.0 + mod_ref[0, 0, 4:5, :]) + mod_ref[0, 0, 3:4, :]
            h2_s[...] = h2.astype(BF16)
            h2_o[...] = h2.astype(BF16)
            m_o[...] = jnp.zeros_like(m_o)

        a = jnp.dot(h2_s[...], w1_ref[...], preferred_element_type=F32)
        rl = jnp.maximum(a, 0.0)
        m_o[...] += jnp.dot((rl * rl).astype(BF16), w2_ref[...], preferred_element_type=F32)

    tok = lambda t, f: (t, 0)
    return pl.pallas_call(
        kern, grid=(T.NLAT // MT, DFF // FC), name="mlp_fwd",
        in_specs=[_bs((MT, D), tok), _bs((1, 1, 6, D), lambda t, f: (t // per_b, 1, 0, 0)),
                  _bs((1, D), lambda t, f: (0, 0)), _bs((D, FC), lambda t, f: (0, f)), _bs((FC, D), lambda t, f: (f, 0))],
        out_specs=[_bs((MT, D), tok), _bs((MT, D), tok)],
        out_shape=[jax.ShapeDtypeStruct((T.NLAT, D), F32), jax.ShapeDtypeStruct((T.NLAT, D), BF16)],
        scratch_shapes=[pltpu.VMEM((MT, D), BF16)],
        compiler_params=_cp(("parallel", "arbitrary")),
    )(x1, modrows, g, w1, w2)


def _loss_head(T, m, x1, tgt, modrows, pg):
    def kern(m_ref, x1_ref, t_ref, mod_ref, pg_ref, loss_o, dm_o, dx2_o, dg2_o, dpg_o):
        l = pl.program_id(0)
        m_ = m_ref[...]
        sm = lax.rsqrt(_rowmean(m_ * m_) + EPS_RMS)
        mn = m_ * sm
        g2 = mod_ref[0, 0, 5:6, :]
        pgv = pg_ref[...]
        diff = x1_ref[...] + g2 * (mn * pgv) - t_ref[...]
        sq = jnp.sum(_colsum(diff * diff), axis=1, keepdims=True)
        _acc(loss_o, jnp.zeros((8, 128), F32) + (0.5 / D) * sq, l == 0)
        dx2 = diff * (1.0 / D)
        dx2_o[...] = dx2
        _acc(dg2_o.at[0], _colsum(dx2 * mn * pgv), l % T.NLT == 0)
        _acc(dpg_o, _colsum(dx2 * g2 * mn), l == 0)
        dmn = dx2 * g2 * pgv
        dm_o[...] = (sm * (dmn - mn * _rowmean(dmn * mn))).astype(BF16)

    lat = lambda l: (l, 0)
    return pl.pallas_call(
        kern, grid=(T.NL,), name="loss_head",
        in_specs=[_bs((TT, D), lat), _bs((TT, D), lat), _bs((TT, D), lat),
                  _bs((1, 1, 6, D), lambda l: (l // T.NLT, 1, 0, 0)), _row(D)],
        out_specs=[_bs((8, 128), lambda l: (0, 0)), _bs((TT, D), lat), _bs((TT, D), lat),
                   _bs((1, 1, D), lambda l: (l // T.NLT, 0, 0)), _row(D)],
        out_shape=[jax.ShapeDtypeStruct((8, 128), F32), jax.ShapeDtypeStruct((T.NLAT, D), BF16),
                   jax.ShapeDtypeStruct((T.NLAT, D), F32), jax.ShapeDtypeStruct((T.B, 1, D), F32),
                   jax.ShapeDtypeStruct((1, D), F32)],
        compiler_params=_cp(("arbitrary",)),
    )(m, x1, tgt, modrows, pg)


def _mlp_bwd(T, h2, dm, w1, w2):
    def kern(h2_ref, dm_ref, w1_ref, w2_ref, f_o, da_o, dh2_o):
        f = pl.program_id(1)
        a = jnp.dot(h2_ref[...], w1_ref[...], preferred_element_type=F32)
        rl = jnp.maximum(a, 0.0)
        f_o[...] = (rl * rl).astype(BF16)
        df = lax.dot_general(dm_ref[...], w2_ref[...], _NT_DIMS, preferred_element_type=F32)
        dab = (df * (2.0 * rl)).astype(BF16)
        da_o[...] = dab
        _acc(dh2_o, lax.dot_general(dab, w1_ref[...], _NT_DIMS, preferred_element_type=F32), f == 0)

    tok = lambda t, f: (t, 0)
    return pl.pallas_call(
        kern, grid=(T.NLAT // MT, DFF // FC), name="mlp_bwd",
        in_specs=[_bs((MT, D), tok), _bs((MT, D), tok), _bs((D, FC), lambda t, f: (0, f)),
                  _bs((FC, D), lambda t, f: (f, 0))],
        out_specs=[_bs((MT, FC), lambda t, f: (t, f)), _bs((MT, FC), lambda t, f: (t, f)), _bs((MT, D), tok)],
        out_shape=[jax.ShapeDtypeStruct((T.NLAT, DFF), BF16), jax.ShapeDtypeStruct((T.NLAT, DFF), BF16),
                   jax.ShapeDtypeStruct((T.NLAT, D), F32)],
        compiler_params=_cp(("parallel", "arbitrary")),
    )(h2, dm, w1, w2)


def _mlp_in_bwd(T, dh2, x1, dx2, modrows, g):
    def kern(dh_ref, x1_ref, dx2_ref, mod_ref, g_ref, dx1_o, dmod_o, dg_o):
        i = pl.program_id(0)
        lat = (i % T.TPS != 0).astype(F32)
        x = x1_ref[...]
        s = lax.rsqrt(_rowmean(x * x) + EPS_RMS)
        xh = x * s
        gv = g_ref[...]
        dh = dh_ref[...] * lat
        n2 = xh * gv
        first_b = i % T.TPS == 0
        _acc(dmod_o.at[0, 0:1, :], _colsum(dh), first_b)
        _acc(dmod_o.at[0, 1:2, :], _colsum(dh * n2), first_b)
        dn2 = dh * (1.0 + mod_ref[0, 0, 4:5, :])
        _acc(dg_o, _colsum(dn2 * xh), i == 0)
        dxh = dn2 * gv
        dx1_o[...] = (dx2_ref[...] + s * (dxh - xh * _rowmean(dxh * xh))) * lat

    lat_i = lambda i: (T.lat(i), 0)
    return pl.pallas_call(
        kern, grid=(T.NT,), name="mlp_in_bwd",
        in_specs=[_bs((TT, D), lat_i), _bs((TT, D), lat_i), _bs((TT, D), lat_i),
                  _bs((1, 1, 6, D), lambda i: (i // T.TPS, 1, 0, 0)), _row(D)],
        out_specs=[_bs((TT, D), lambda i: (i, 0)), _bs((1, 2, D), lambda i: (i // T.TPS, 0, 0)), _row(D)],
        out_shape=[jax.ShapeDtypeStruct((T.NTOK, D), F32), jax.ShapeDtypeStruct((T.B, 2, D), F32),
                   jax.ShapeDtypeStruct((1, D), F32)],
        compiler_params=_cp(("arbitrary",)),
    )(dh2, x1, dx2, modrows, g)


def _mix_post_bwd(T, dx1, mix, modrows, pg, w_out):
    def kern(dx_ref, mix_ref, mod_ref, pg_ref, wo_ref, dmix_o, dcat_o, dg1_o, dpg_o):
        i = pl.program_id(0)
        lat = (i % T.TPS != 0).astype(F32)
        dx = dx_ref[...]
        mix = mix_ref[...]
        sm = lax.rsqrt(_rowmean(mix * mix) + EPS_RMS)
        mh = mix * sm
        g1 = mod_ref[0, 0, 2:3, :]
        pgv = pg_ref[...]
        _acc(dg1_o.at[0], _colsum(dx * mh * pgv), i % T.TPS == 0)
        _acc(dpg_o, _colsum(dx * g1 * mh), i == 0)
        dmh = dx * g1 * pgv
        dmix = ((sm * (dmh - mh * _rowmean(dmh * mh))) * lat).astype(BF16)
        dmix_o[...] = dmix
        dcat_o[...] = lax.dot_general(dmix, wo_ref[...], _NT_DIMS, preferred_element_type=F32)

    tok = lambda i: (i, 0)
    return pl.pallas_call(
        kern, grid=(T.NT,), name="mix_post_bwd",
        in_specs=[_bs((TT, D), tok), _bs((TT, D), lambda i: (T.lat(i), 0)),
                  _bs((1, 1, 6, D), lambda i: (i // T.TPS, 1, 0, 0)), _row(D), _bs((D, D), lambda i: (0, 0))],
        out_specs=[_bs((TT, D), tok), _bs((TT, D), tok), _bs((1, 1, D), lambda i: (i // T.TPS, 0, 0)), _row(D)],
        out_shape=[jax.ShapeDtypeStruct((T.NTOK, D), BF16), jax.ShapeDtypeStruct((T.NTOK, D), F32),
                   jax.ShapeDtypeStruct((T.B, 1, D), F32), jax.ShapeDtypeStruct((1, D), F32)],
        compiler_params=_cp(("arbitrary",)),
    )(dx1, mix, modrows, pg, w_out)


def _conv_bwd(T, dcat, p_cv, conv, cw, cb, clw, clb):
    def kern(dc_ref, pcv_ref, conv_ref, cw_ref, cb_ref, clw_ref, clb_ref, dp_o, dcw_o, dcb_o, dlw_o, dlb_o):
        i = pl.program_id(0)
        is_lat = i % T.TPS != 0

        @pl.when(i == 0)
        def _():
            for ref in (dcw_o, dcb_o, dlw_o, dlb_o):
                ref[...] = jnp.zeros(ref.shape, ref.dtype)

        @pl.when(jnp.logical_not(is_lat))
        def _():
            dp_o[...] = jnp.zeros(dp_o.shape, dp_o.dtype)

        @pl.when(is_lat)
        def _():
            cva = pcv_ref[:, 0:512]
            cv = _conv_fwd(cva, pcv_ref[:, 512:1024], cw_ref, cb_ref[...], clw_ref[...], clb_ref[...],
                           c=conv_ref[...])
            scn = cv["scn"]
            dcn = dc_ref[...] * (scn * (1.0 + cv["cn"] * (1.0 - scn)))
            chat = cv["chat"]
            dlw_o[...] += _colsum(dcn * chat)
            dlb_o[...] += _colsum(dcn)
            dchat = dcn * clw_ref[...]
            dc = cv["rstd"] * (dchat - _rowmean(dchat) - chat * _rowmean(dchat * chat))
            dcb_o[...] += _colsum(dc)
            pos = cv["pos"]
            u = cv["u"]
            du = jnp.zeros_like(u)
            for j in range(KCONV):
                s = j - KCONV // 2
                dcw_o[j:j + 1, :] += _colsum(dc * _shifted(u, s, pos))
                du = du + cw_ref[j:j + 1, :] * _shifted(dc, -s, pos)
            sgb = cv["sgb"]
            dp_o[...] = jnp.concatenate([du * sgb, du * cva * sgb * (1.0 - sgb)], axis=1).astype(BF16)

    return pl.pallas_call(
        kern, grid=(T.NT,), name="conv_bwd",
        in_specs=[_bs((TT, W), lambda i: (i, 1)), _bs((TT, CVC), lambda i: (i, 0)),
                  _bs((TT, W), lambda i: (T.lat(i), 0)),
                  _bs((32, W), lambda i: (0, 0)), _row(W), _row(W), _row(W)],
        out_specs=[_bs((TT, CVC), lambda i: (i, 0)), _bs((32, W), lambda i: (0, 0)), _row(W), _row(W), _row(W)],
        out_shape=[jax.ShapeDtypeStruct((T.NTOK, CVC), BF16), jax.ShapeDtypeStruct((32, W), F32),
                   jax.ShapeDtypeStruct((1, W), F32), jax.ShapeDtypeStruct((1, W), F32),
                   jax.ShapeDtypeStruct((1, W), F32)],
        compiler_params=_cp(("arbitrary",)),
    )(dcat, p_cv, conv, cw, cb, clw, clb)


def _readout_bwd(T, dcat, y, kd, rw, rk, gw, lnw, lnb):
    def kern(dc_ref, y_ref, kd_ref, rw_ref, rk_ref, gw_ref, lw_ref, lb_ref,
             dy_o, dr_o, dv_o, dkb_o, dgd_o, drk_o, dgw_o, dlw_o, dlb_o):
        i = pl.program_id(0)
        first = i == 0
        e = _e128(BF16)
        r = rw_ref[:, 0:512]
        v = rw_ref[:, 1024:1536]
        kbar = 0.5 * (kd_ref[0] + kd_ref[1])
        rk = rk_ref[...]
        ro = _readout_fwd(y_ref[0] + y_ref[1], r, v, rw_ref[:, 2048:2304], kbar, rk, gw_ref[...],
                          lw_ref[...], lb_ref[...], e)
        dout = dc_ref[...]
        dgg = dout * (ro["yn"] + ro["q"] * v)
        t1 = dout * ro["gg"]
        yhat = ro["yhat"]
        _acc(dlw_o, _colsum(t1 * yhat), first)
        _acc(dlb_o, _colsum(t1), first)
        dyh = t1 * lw_ref[...]
        dy_o[...] = ro["rstd"] * (dyh - _segsum(dyh, e) * (1.0 / HS) - yhat * (_segsum(dyh * yhat, e) * (1.0 / HS)))
        dq = _segsum(t1 * v, e)
        dv_o[...] = t1 * ro["q"]
        dr_o[...] = dq * kbar * rk
        dkb_o[...] = dq * r * rk
        _acc(drk_o, _colsum(dq * r * kbar), first)
        sg = ro["sg"]
        dsg = _bdot(dgg, gw_ref[...], _NT_DIMS)
        dgd_o[...] = dsg * sg * (1.0 - sg)
        _acc(dgw_o, _bdot(sg, dgg, _TN_DIMS), first)

    tok = lambda i: (i, 0)
    f32s = lambda *s: jax.ShapeDtypeStruct(s, F32)
    return pl.pallas_call(
        kern, grid=(T.NT,), name="readout_bwd",
        in_specs=[_bs((TT, W), tok), T.tm2_spec(), T.tm2_spec(), _bs((TT, RWC), tok),
                  _row(W), _bs((GDW, W), lambda i: (0, 0)), _row(W), _row(W)],
        out_specs=[T.tm_spec(), _bs((TT, W), tok), _bs((TT, W), tok), _bs((TT, W), tok), _bs((TT, GDW), tok),
                   _row(W), _bs((GDW, W), lambda i: (0, 0)), _row(W), _row(W)],
        out_shape=[f32s(T.TTOT, T.BW), f32s(T.NTOK, W), f32s(T.NTOK, W), f32s(T.NTOK, W), f32s(T.NTOK, GDW),
                   f32s(1, W), f32s(GDW, W), f32s(1, W), f32s(1, W)],
        compiler_params=_cp(("arbitrary",)),
    )(dcat, y, kd, rw, rk, gw, lnw, lnb)


def _scan_bwd(T, dy, r, v, kk, dec, kd, bb, hist, fin):
    NP = T.BW // 128
    R = NP * HS
    SB = SCAN_BSUB
    NS = T.TTOT // SB
    NSC = TT // SB

    def tmap(d, g):
        s = NS - 1 - g
        rev = jnp.where(s < NSC, NSC - 1 - s, NS - 1 - (s - NSC))
        return jnp.where(d == 0, s, rev)

    def kern(dy_ref, r_ref, v_ref, kk_ref, dec_ref, kd_ref, bb_ref, h_ref, fin_ref,
             dr_o, dw_o, dk_o, dv_o, da_o, db_o, ds_ref, snext):
        d = pl.program_id(0)
        g = pl.program_id(1)

        @pl.when(g == 0)
        def _():
            ds_ref[...] = jnp.zeros_like(ds_ref)
            snext[...] = fin_ref[0]

        e = _e128(BF16)

        for t in range(SB - 1, -1, -1):
            row = jnp.where(d == 0, t, SB - 1 - t)
            sp = h_ref[0, t]
            st = snext[...] if t == SB - 1 else h_ref[0, t + 1]
            a_ = _expand(-kk_ref[pl.ds(row, 1), :], NP)
            b_ = _expand(bb_ref[0, pl.ds(row, 1), :], NP)
            k_ = _expand(kd_ref[0, pl.ds(row, 1), :], NP)
            sa = _segb(sp * a_, e)
            vc = _colb(v_ref[pl.ds(row, 1), :], NP, e)
            dyc = _colb(dy_ref[pl.ds(row, 1), :], NP, e)
            ds = ds_ref[...] + dyc * _expand(r_ref[pl.ds(row, 1), :], NP)
            dsa = _segb(ds * b_, e)
            ds_ref[...] = ds * _expand(dec_ref[0, pl.ds(row, 1), :], NP) + dsa * a_
            dvb = _segb(ds * k_, e)
            dr_o[0, pl.ds(row, 1), :] = _pair_colsum(st * dyc, NP)
            dw_o[0, pl.ds(row, 1), :] = _pair_colsum(ds * sp, NP)
            db_o[0, pl.ds(row, 1), :] = _pair_colsum(ds * sa, NP)
            dv_o[0, pl.ds(row, 1), :] = _pair_colsum(_diag(dvb, NP), NP)
            dk_o[0, pl.ds(row, 1), :] = _pair_colsum(ds * vc, NP)
            da_o[0, pl.ds(row, 1), :] = _pair_colsum(sp * dsa, NP)
        snext[...] = h_ref[0, 0]

    sh = _bs((SB, T.BW), lambda d, g: (tmap(d, g), 0))
    dr = _bs((1, SB, T.BW), lambda d, g: (d, tmap(d, g), 0))
    o2 = jax.ShapeDtypeStruct((2, T.TTOT, T.BW), F32)
    return pl.pallas_call(
        kern, grid=(2, NS), name="scan_bwd",
        in_specs=[sh, sh, sh, sh, dr, dr, dr, _bs((1, SB, R, 128), lambda d, g: (d, NS - 1 - g, 0, 0)),
                  _bs((1, R, 128), lambda d, g: (d, 0, 0))],
        out_specs=[dr] * 6,
        out_shape=[o2] * 6,
        scratch_shapes=[pltpu.VMEM((R, 128), F32), pltpu.VMEM((R, 128), F32)],
        compiler_params=_cp(("arbitrary", "arbitrary"), mb=48),
    )(dy, r, v, kk, dec, kd, bb, hist, fin)


def _prep_bwd(T, rw, dr_s, ddec, dkd, dv_s, da_s, dbb, dr_ro, dv_ro, dkbar, dgd, w0, w2, a0, a2, k_k, k_a):
    def kern(rw_ref, drs_ref, ddec_ref, dkd_ref, dvs_ref, das_ref, dbb_ref, drr_ref, dvr_ref, dkb_ref, dgd_ref,
             w0_ref, w2_ref, a0_ref, a2_ref, kk_ref, ka_ref,
             drw_o, dw0_o, dw2_o, da0_o, da2_o, dkk_o, dka_o):
        i = pl.program_id(0)
        first = i == 0
        e = _e128(BF16)
        w0v, w2v, a0v, a2v = _load_prep_params(w0_ref, w2_ref, a0_ref, a2_ref)
        k_k = kk_ref[...]
        k_a = ka_ref[...]
        o = _prep_math(rw_ref[...], w0v, w2v, a0v, a2v, k_k, k_a, e)
        k, kk = o["k"], o["kk"]
        dkbh = 0.5 * dkb_ref[...]
        dk = jnp.zeros_like(k)
        dkk = -(das_ref[0] + das_ref[1])
        dka = jnp.zeros((1, W), F32)
        dwd, dad = [], []
        for d in (0, 1):
            iclr = o["iclr"][d]
            dkd_d = dkd_ref[d] + dkbh
            dbb_d = dbb_ref[d]
            dk = dk + dkd_d * (1.0 + (iclr - 1.0) * k_a)
            dka = dka + _colsum(dkd_d * k * (iclr - 1.0))
            dkk = dkk + dbb_d * iclr
            dicl = dkd_d * k * k_a + dbb_d * kk
            dpa = dicl * iclr * (1.0 - iclr)
            _acc(da0_o.at[d:d + 1, :], _colsum(dpa), first)
            dad.append(_bdot(dpa, a2v[d], _NT_DIMS))
            _acc(da2_o.at[d], _bdot(o["ad"][d], dpa, _TN_DIMS), first)
            dpre = -ddec_ref[d] * o["dec"][d] * o["ex"][d] * _sigmoid(-o["pre"][d])
            _acc(dw0_o.at[d:d + 1, :], _colsum(dpre), first)
            th = o["th"][d]
            dth = _bdot(dpre, w2v[d], _NT_DIMS)
            _acc(dw2_o.at[d], _bdot(th, dpre, _TN_DIMS), first)
            dwd.append(dth * (1.0 - th * th))
        inv = o["inv"]
        kr = o["kr"]
        proj = _segsum(dkk * kr, e)
        dkr = dkk * inv - jnp.where(o["rt"] > 1e-12, kr * inv * inv * inv * proj, 0.0)
        dk = dk + dkr * k_k
        _acc(dkk_o, _colsum(dkr * k), first)
        _acc(dka_o, dka, first)
        dr = drs_ref[0] + drs_ref[1] + drr_ref[...]
        dv = dvs_ref[0] + dvs_ref[1] + dvr_ref[...]
        drw_o[...] = jnp.concatenate([dr, dk, dv, dwd[0], dwd[1], dad[0], dad[1], dgd_ref[...]], axis=1)

    tok = lambda i: (i, 0)
    f32s = lambda *s: jax.ShapeDtypeStruct(s, F32)
    p2 = lambda i: (0, 0)
    p3 = lambda i: (0, 0, 0)
    return pl.pallas_call(
        kern, grid=(T.NT,), name="prep_bwd",
        in_specs=[_bs((TT, RWC), tok)] + [T.tm2_spec()] * 6 + [_bs((TT, W), tok)] * 3 + [_bs((TT, GDW), tok)]
        + _prep_param_specs(),
        out_specs=[_bs((TT, RWC), tok), _bs((2, W), p2), _bs((2, LRW, W), p3), _bs((2, W), p2),
                   _bs((2, LRW, W), p3), _row(W), _row(W)],
        out_shape=[f32s(T.NTOK, RWC), f32s(2, W), f32s(2, LRW, W), f32s(2, W), f32s(2, LRW, W), f32s(1, W), f32s(1, W)],
        compiler_params=_cp(("arbitrary",), mb=56),
    )(rw, dr_s, ddec, dkd, dv_s, da_s, dbb, dr_ro, dv_ro, dkbar, dgd, w0, w2, a0, a2, k_k, k_a)


def _shift_bwd(T, drw, p_rw, mu_p, mu_n):
    def kern(d_ref, dp_ref, dn_ref, p_ref, pp_ref, pn_ref, mp_ref, mn_ref, dprw_o, dmp_o, dmn_o):
        i = pl.program_id(0)
        first = i == 0
        has_prev, has_next = _halo_masks(T, i)
        mp = mp_ref[...]
        mn = mn_ref[...]
        drw = d_ref[...]
        z = p_ref[...]
        zprev, znext = _neighbours(z, pp_ref[7:8, :] * has_prev, pn_ref[0:1, :] * has_next)
        _acc(dmp_o, _colsum(drw * (zprev - z)), first)
        _acc(dmn_o, _colsum(drw * (znext - z)), first)
        dprev, dnext = _neighbours(drw, dp_ref[7:8, :] * has_prev, dn_ref[0:1, :] * has_next)
        dprw_o[...] = (drw * (1.0 - mp - mn) + mp * dnext + mn * dprev).astype(BF16)

    tok = lambda i: (i, 0)
    prev, nxt = _halo_specs(T)
    f32s = lambda *s: jax.ShapeDtypeStruct(s, F32)
    return pl.pallas_call(
        kern, grid=(T.NT,), name="shift_bwd",
        in_specs=[_bs((TT, RWC), tok), prev, nxt, _bs((TT, RWC), tok), prev, nxt, _row(RWC), _row(RWC)],
        out_specs=[_bs((TT, RWC), tok), _row(RWC), _row(RWC)],
        out_shape=[jax.ShapeDtypeStruct((T.NTOK, RWC), BF16), f32s(1, RWC), f32s(1, RWC)],
        compiler_params=_cp(("arbitrary",), mb=56),
    )(drw, drw, drw, p_rw, p_rw, p_rw, mu_p, mu_n)


def _mix_in_bwd(T, dp_rw, dp_cv, xc, dx1, modrows, g, w_rw, w_cv):
    def kern(drw_ref, dcv_ref, x_ref, dx1_ref, mod_ref, g_ref, wr_ref, wc_ref, dxc_o, dmod_o, dg_o):
        i = pl.program_id(0)
        dh = (lax.dot_general(drw_ref[...], wr_ref[...], _NT_DIMS, preferred_element_type=F32)
              + lax.dot_general(dcv_ref[...], wc_ref[...], _NT_DIMS, preferred_element_type=F32))
        x = x_ref[...]
        s = lax.rsqrt(_rowmean(x * x) + EPS_RMS)
        xh = x * s
        gv = g_ref[...]
        q = i % T.TPS
        first_kind = jnp.logical_or(q == 0, q == 1)
        _acc(dmod_o.at[0, 0, 0:1, :], _colsum(dh), first_kind)
        _acc(dmod_o.at[0, 0, 1:2, :], _colsum(dh * (xh * gv)), first_kind)
        dn1 = dh * (1.0 + mod_ref[0, 0, 1:2, :])
        _acc(dg_o, _colsum(dn1 * xh), i == 0)
        dxh = dn1 * gv
        dxc_o[...] = dx1_ref[...] + s * (dxh - xh * _rowmean(dxh * xh))

    tok = lambda i: (i, 0)
    f32s = lambda *s: jax.ShapeDtypeStruct(s, F32)
    return pl.pallas_call(
        kern, grid=(T.NT,), name="mix_in_bwd",
        in_specs=[_bs((TT, RWC), tok), _bs((TT, CVC), tok), _bs((TT, D), tok), _bs((TT, D), tok), T.mod_spec(),
                  _row(D), _bs((D, RWC), lambda i: (0, 0)), _bs((D, CVC), lambda i: (0, 0))],
        out_specs=[_bs((TT, D), tok),
                   _bs((1, 1, 2, D), lambda i: (i // T.TPS, jnp.minimum(i % T.TPS, 1), 0, 0)), _row(D)],
        out_shape=[f32s(T.NTOK, D), f32s(T.B, 2, 2, D), f32s(1, D)],
        compiler_params=_cp(("arbitrary",)),
    )(dp_rw, dp_cv, xc, dx1, modrows, g, w_rw, w_cv)


def _matmul_tn(a, b, name, tk, nk, tn, amap=None, bmap=None, tm=1024):
    M = a.shape[1]
    N = b.shape[1]
    amap = amap or (lambda k: k)
    bmap = bmap or (lambda k: k)

    def kern(a_ref, b_ref, o_ref):
        _acc(o_ref, lax.dot_general(a_ref[...], b_ref[...], _TN_DIMS, preferred_element_type=F32),
             pl.program_id(2) == 0)

    return pl.pallas_call(
        kern, grid=(M // tm, N // tn, nk), name=name,
        in_specs=[_bs((tk, tm), lambda i, j, k: (amap(k), i)), _bs((tk, tn), lambda i, j, k: (bmap(k), j))],
        out_specs=_bs((tm, tn), lambda i, j, k: (i, j)),
        out_shape=jax.ShapeDtypeStruct((M, N), F32),
        compiler_params=_cp(("parallel", "parallel", "arbitrary")),
    )(a, b)


def _silu(x):
    return x * _sigmoid(x)


def _ada_fwd(c_all, c_ctx, ada_w, ada_b_blk):
    nb = c_all.shape[0]
    R = nb + 8
    ncol = ada_w.shape[1]

    def kern(c_ref, cc_ref, w_ref, b_ref, o_ref):
        lhs = jnp.concatenate([_silu(c_ref[...]), _silu(cc_ref[...]), jnp.zeros((7, D), F32)], axis=0)
        o_ref[...] = jnp.dot(lhs, w_ref[...], precision=HI, preferred_element_type=F32) + b_ref[...]

    return pl.pallas_call(
        kern, name="ada_fwd", out_shape=jax.ShapeDtypeStruct((R, ncol), F32),
        compiler_params=_cp(None, 40),
    )(c_all, c_ctx, ada_w, ada_b_blk)


def _ada_bwd(c_all, c_ctx, ada_w, ex, cx, ex_blk, cx_blk):
    nb = c_all.shape[0]
    ncol = ada_w.shape[1]

    def kern(c_ref, cc_ref, w_ref, ex_ref, cx_ref, exb_ref, cxb_ref, gw_o, gb_o, ds_o):
        lhs = jnp.concatenate([_silu(c_ref[...]), _silu(cc_ref[...]), jnp.zeros((7, D), F32)], axis=0)
        dmc_blk = _colsum(cxb_ref[...])
        rhs = jnp.concatenate([exb_ref[...], dmc_blk, jnp.zeros((7, ncol), F32)], axis=0)
        gw_o[...] = lax.dot_general(lhs, rhs, _TN_DIMS, precision=HI, preferred_element_type=F32)
        gb_o[...] = _colsum(ex_ref[...]) + _colsum(cx_ref[...])
        ds_o[...] = lax.dot_general(jnp.concatenate([dmc_blk, jnp.zeros((7, ncol), F32)], axis=0), w_ref[...],
                                    _NT_DIMS, precision=HI, preferred_element_type=F32)

    return pl.pallas_call(
        kern, name="ada_bwd",
        out_shape=[jax.ShapeDtypeStruct((D, ncol), F32), jax.ShapeDtypeStruct((1, ex.shape[1]), F32),
                   jax.ShapeDtypeStruct((8, D), F32)],
        compiler_params=_cp(None, 48),
    )(c_all, c_ctx, ada_w, ex, cx, ex_blk, cx_blk)


def _cctx_final(parts, c_ctx):
    def kern(p_ref, c_ref, o_ref):
        tot = p_ref[0, 0:1, :]
        for j in range(1, parts.shape[0]):
            tot = tot + p_ref[j, 0:1, :]
        c = c_ref[...]
        sg = _sigmoid(c)
        o_ref[...] = tot * (sg * (1.0 + c * (1.0 - sg)))

    return pl.pallas_call(kern, name="cctx_final", out_shape=jax.ShapeDtypeStruct((1, D), F32))(parts, c_ctx)


def _peer(kind, p, ix, iy, ic):
    if kind == "chips":
        return (p // 2, p % 2, ic)
    if kind == "all":
        return (p // 4, (p // 2) % 2, p % 2)
    return (ix, iy, p)


def _exchange(x, kind, bcast, name, chunks=1):
    npeer = {"chips": 4, "all": 8, "sib": 2}[kind]
    slab = x.shape if bcast else x.shape[1:]
    assert chunks == 1 or slab[0] == chunks

    def kern(x_ref, o_ref, send_sems, recv_sems, lsem):
        ix, iy, ic = lax.axis_index("x"), lax.axis_index("y"), lax.axis_index("c")
        me = {"chips": 2 * ix + iy, "all": 4 * ix + 2 * iy + ic, "sib": ic}[kind]
        own = pltpu.make_async_copy(x_ref if bcast else x_ref.at[me], o_ref.at[me], lsem)
        own.start()

        def part(ref, k):
            return ref if chunks == 1 else ref.at[k]

        def copy(p, k):
            return pltpu.make_async_remote_copy(
                src_ref=part(x_ref if bcast else x_ref.at[p], k), dst_ref=part(o_ref.at[me], k),
                send_sem=send_sems.at[p, k], recv_sem=recv_sems.at[me, k],
                device_id=_peer(kind, p, ix, iy, ic), device_id_type=MESH)

        def arrival(p, k):
            return pltpu.make_async_remote_copy(
                src_ref=part(x_ref if bcast else x_ref.at[p], k), dst_ref=part(o_ref.at[p], k),
                send_sem=send_sems.at[p, k], recv_sem=recv_sems.at[p, k],
                device_id=_peer(kind, p, ix, iy, ic), device_id_type=MESH)

        for p in range(npeer):
            @pl.when(me != p)
            def _():
                for k in range(chunks):
                    copy(p, k).start()
        for p in range(npeer):
            @pl.when(me != p)
            def _():
                for k in range(chunks):
                    arrival(p, k).wait_recv()
        for p in range(npeer):
            @pl.when(me != p)
            def _():
                for k in range(chunks):
                    copy(p, k).wait_send()
        own.wait()

    any_spec = pl.BlockSpec(memory_space=pl.ANY)
    return pl.pallas_call(
        kern, name=name, in_specs=[any_spec], out_specs=any_spec,
        out_shape=jax.ShapeDtypeStruct((npeer,) + tuple(slab), x.dtype),
        scratch_shapes=[pltpu.SemaphoreType.DMA((npeer, chunks)), pltpu.SemaphoreType.DMA((npeer, chunks)),
                        pltpu.SemaphoreType.DMA],
    )(x)


def _sum_slots(x, name):
    n, R, C = x.shape
    budget = (8 << 20) // (n * C * x.dtype.itemsize)
    tr = max([t for t in range(8, R + 1, 8) if R % t == 0 and t <= max(budget, 8)], default=R)

    def kern(x_ref, o_ref):
        tot = x_ref[0]
        for s in range(1, n):
            tot = tot + x_ref[s]
        o_ref[...] = tot

    return pl.pallas_call(
        kern, grid=(R // tr,), name=name,
        in_specs=[_bs((n, tr, C), lambda i: (0, i, 0))], out_specs=_bs((tr, C), lambda i: (i, 0)),
        out_shape=jax.ShapeDtypeStruct((R, C), x.dtype), compiler_params=_cp(("parallel",)),
    )(x)


def _sib_stream(x, me, name, add):
    K, R, C = x.shape[-3:]

    def kern(me_ref, *refs):
        if add:
            own_ref, send_ref, o_ref, rbuf, ssem, rsem, credit = refs
        else:
            send_ref, o_ref, rbuf, ssem, rsem, credit = refs
        k = pl.program_id(0)
        slot = k % 2
        sib = (lax.axis_index("x"), lax.axis_index("y"), 1 - lax.axis_index("c"))

        @pl.when(k >= 2)
        def _():
            pl.semaphore_wait(credit.at[slot], 1)

        src = send_ref.at[0, 0] if add else send_ref.at[0]
        cp = pltpu.make_async_remote_copy(src_ref=src, dst_ref=rbuf.at[slot], send_sem=ssem.at[slot],
                                          recv_sem=rsem.at[slot], device_id=sib, device_id_type=MESH)
        cp.start()
        cp.wait_recv()
        o_ref[0] = own_ref[0, 0] + rbuf[slot] if add else rbuf[slot]
        cp.wait_send()

        @pl.when(k + 2 < K)
        def _():
            pl.semaphore_signal(credit.at[slot], 1, device_id=sib, device_id_type=MESH)

    if add:
        in_specs = [_bs((1, 1, R, C), lambda k, me_ref: (me_ref[0], k, 0, 0)),
                    _bs((1, 1, R, C), lambda k, me_ref: (1 - me_ref[0], k, 0, 0))]
        args = (x, x)
    else:
        in_specs = [_bs((1, R, C), lambda k, me_ref: (k, 0, 0))]
        args = (x,)
    return pl.pallas_call(
        kern, name=name,
        grid_spec=pltpu.PrefetchScalarGridSpec(
            num_scalar_prefetch=1, grid=(K,), in_specs=in_specs,
            out_specs=_bs((1, R, C), lambda k, me_ref: (k, 0, 0)),
            scratch_shapes=[pltpu.VMEM((2, R, C), x.dtype), pltpu.SemaphoreType.DMA((2,)),
                            pltpu.SemaphoreType.DMA((2,)), pltpu.SemaphoreType.REGULAR((2,))]),
        out_shape=jax.ShapeDtypeStruct((K, R, C), x.dtype),
        compiler_params=_cp(("arbitrary",)),
    )(me, *args)


def _adamw(w, g, m, v, name):
    shape = w.shape
    if len(shape) == 1:
        outs = _adamw(*(t.reshape(1, -1) for t in (w, g, m, v)), name)
        return tuple(t.reshape(shape) for t in outs)
    nd = len(shape)
    size = 1
    for s in shape:
        size *= s
    rows = shape[-2]
    tr = rows
    if size > (1 << 18) and all(s == 1 for s in shape[:-2]):
        tr = max(t for t in (256, 128, 64, 32, 16, 8) if rows % t == 0)
    c1 = 1.0 - ADAM_B1 ** ADAM_STEP
    c2 = 1.0 - ADAM_B2 ** ADAM_STEP

    def kern(w_ref, g_ref, m_ref, v_ref, d_o, m_o, v_o):
        gv = g_ref[...]
        mn = ADAM_B1 * m_ref[...] + (1.0 - ADAM_B1) * gv
        vn = ADAM_B2 * v_ref[...] + (1.0 - ADAM_B2) * (gv * gv)
        m_o[...] = mn
        v_o[...] = vn
        d_o[...] = -ADAM_LR * ((mn / c1) / (jnp.sqrt(vn / c2) + ADAM_EPS) + ADAM_WD * w_ref[...])

    spec = _bs(shape[:-2] + (tr, shape[-1]), lambda i: (0,) * (nd - 2) + (i, 0))
    o = jax.ShapeDtypeStruct(shape, F32)
    return tuple(pl.pallas_call(
        kern, grid=(rows // tr,), name=name, in_specs=[spec] * 4, out_specs=[spec] * 3, out_shape=[o, o, o],
        compiler_params=_cp(("parallel",)),
    )(w, g, m, v))


_WEIGHT_NAMES = ("c_ctx", "ada_w", "ada_b", "mix_pre_g", "mix_post_g", "mlp_pre_g", "mlp_post_g", "w_in", "mu_prev",
                 "mu_next", "decay_w0", "decay_w2", "iclr_a0", "iclr_a2", "k_k", "k_a", "r_k", "gate_w2", "lnx_w",
                 "lnx_b", "conv_w", "conv_b", "conv_ln_w", "conv_ln_b", "w_out", "mlp_w1", "mlp_w2")


def _pack_rows(parts, cols=512):
    flat = jnp.concatenate([p.reshape(-1) for p in parts])
    rows = -(-flat.shape[0] // cols)
    rows = -(-rows // 16) * 16
    flat = jnp.pad(flat, (0, rows * cols - flat.shape[0]))
    return flat.reshape(rows, cols)


def _pack2d(parts, cols=512):
    return jnp.concatenate([p.reshape(-1, cols) for p in parts], axis=0)


def _unpack2d(buf, shapes):
    out = []
    off = 0
    for s in shapes:
        n = 1
        for d in s:
            n *= d
        n //= buf.shape[1]
        out.append(buf[off:off + n].reshape(s))
        off += n
    return out


def _unpack(flat, shapes):
    out = []
    off = 0
    for s in shapes:
        n = 1
        for d in s:
            n *= d
        out.append(flat[off:off + n].reshape(s))
        off += n
    return out


def _local_step(T, xc, tgt, modrows, P):
    p_rw, p_cv, h = _mix_in(T, xc, modrows, P["mix_pre_g"], P["w_rw"], P["w_cv"])
    prep_params = (P["w0"], P["w2"], P["a0"], P["a2"], P["k_k"], P["k_a"])
    r, v, kk, dec, kd, bb, rw = _rwkv_prep(T, p_rw, P["mu_p"], P["mu_n"], *prep_params)
    y, fin, hist = _scan_fwd(T, r, v, kk, dec, kd, bb)
    ro_params = (P["r_k"], P["gate_w2"], P["lnx_w"], P["lnx_b"])
    cv_params = (P["conv_w"], P["conv_b"], P["conv_ln_w"], P["conv_ln_b"])
    cat, mix, x1, conv = _mix_out(T, y, kd, rw, p_cv, xc, modrows, *ro_params, *cv_params, P["mix_post_g"], P["w_out"])
    m, h2 = _mlp_fwd(T, x1, modrows, P["mlp_pre_g"], P["w1"], P["w2m"])
    loss_acc, dm, dx2, dg2, d_mlp_post = _loss_head(T, m, x1, tgt, modrows, P["mlp_post_g"])
    fact, da, dh2 = _mlp_bwd(T, h2, dm, P["w1"], P["w2m"])
    dx1, dmod2, d_mlp_pre = _mlp_in_bwd(T, dh2, x1, dx2, modrows, P["mlp_pre_g"])
    dmix, dcat, dg1, d_mix_post = _mix_post_bwd(T, dx1, mix, modrows, P["mix_post_g"], P["w_out"])
    dp_cv, d_conv_w, d_conv_b, d_cln_w, d_cln_b = _conv_bwd(T, dcat, p_cv, conv, *cv_params)
    dy, dr_ro, dv_ro, dkbar, dgd, d_r_k, d_gate, d_lnx_w, d_lnx_b = _readout_bwd(T, dcat, y, kd, rw, *ro_params)
    dr_s, ddec, dkd, dv_s, da_s, dbb = _scan_bwd(T, dy, r, v, kk, dec, kd, bb, hist, fin)
    drw, d_w0, d_w2, d_a0, d_a2, d_k_k, d_k_a = _prep_bwd(T, rw, dr_s, ddec, dkd, dv_s, da_s, dbb, dr_ro, dv_ro,
                                                          dkbar, dgd, *prep_params)
    dp_rw, d_mu_p, d_mu_n = _shift_bwd(T, drw, p_rw, P["mu_p"], P["mu_n"])
    dxc, dmod1, d_mix_pre = _mix_in_bwd(T, dp_rw, dp_cv, xc, dx1, modrows, P["mix_pre_g"], P["w_rw"], P["w_cv"])
    nk = T.NTOK // TT
    dw_rw = _matmul_tn(h, dp_rw, "dw_in_rw", TT, nk, 768)
    dw_cv = _matmul_tn(h, dp_cv, "dw_in_cv", TT, nk, 1024)
    dw_out = _matmul_tn(cat, dmix, "dw_out", TT, T.NL, 1024, bmap=T.tok)
    dw1 = _matmul_tn(h2, da, "dw_mlp1", 512, T.NLAT // 512, 1024)
    dw2m = _matmul_tn(fact, dm, "dw_mlp2", 512, T.NLAT // 512, 1024)
    small = dict(mix_pre_g=d_mix_pre, mix_post_g=d_mix_post, mlp_pre_g=d_mlp_pre, mlp_post_g=d_mlp_post,
                 mu_p=d_mu_p, mu_n=d_mu_n, w0=d_w0, w2=d_w2, a0=d_a0, a2=d_a2, k_k=d_k_k, k_a=d_k_a, r_k=d_r_k,
                 gate_w2=d_gate, lnx_w=d_lnx_w, lnx_b=d_lnx_b, conv_w=d_conv_w, conv_b=d_conv_b,
                 conv_ln_w=d_cln_w, conv_ln_b=d_cln_b)
    big = dict(w_rw=dw_rw, w_cv=dw_cv, w_out=dw_out, w1=dw1, w2m=dw2m)
    dmods = dict(dmod1=dmod1, dg1=dg1, dmod2=dmod2, dg2=dg2)
    return loss_acc[0, 0], dxc, small, big, dmods


_SMALL_ORDER = ("mix_pre_g", "mix_post_g", "mlp_pre_g", "mlp_post_g", "mu_p", "mu_n", "w0", "w2", "a0", "a2", "k_k",
                "k_a", "r_k", "gate_w2", "lnx_w", "lnx_b", "conv_w", "conv_b", "conv_ln_w", "conv_ln_b")


def kernel(x, c, ctx, c_ctx, ada_w, ada_b, mix_pre_g, mix_post_g, mlp_pre_g, mlp_post_g, w_in, mu_prev, mu_next, decay_w0, decay_w2, iclr_a0, iclr_a2, k_k, k_a, r_k, gate_w2, lnx_w, lnx_b, conv_w, conv_b, conv_ln_w, conv_ln_b, w_out, mlp_w1, mlp_w2, loss_target, m_c_ctx, m_ada_w, m_ada_b, m_mix_pre_g, m_mix_post_g, m_mlp_pre_g, m_mlp_post_g, m_w_in, m_mu_prev, m_mu_next, m_decay_w0, m_decay_w2, m_iclr_a0, m_iclr_a2, m_k_k, m_k_a, m_r_k, m_gate_w2, m_lnx_w, m_lnx_b, m_conv_w, m_conv_b, m_conv_ln_w, m_conv_ln_b, m_w_out, m_mlp_w1, m_mlp_w2, v_c_ctx, v_ada_w, v_ada_b, v_mix_pre_g, v_mix_post_g, v_mlp_pre_g, v_mlp_post_g, v_w_in, v_mu_prev, v_mu_next, v_decay_w0, v_decay_w2, v_iclr_a0, v_iclr_a2, v_k_k, v_k_a, v_r_k, v_gate_w2, v_lnx_w, v_lnx_b, v_conv_w, v_conv_b, v_conv_ln_w, v_conv_ln_b, v_w_out, v_mlp_w1, v_mlp_w2):
    weights = dict(zip(_WEIGHT_NAMES, (c_ctx, ada_w, ada_b, mix_pre_g, mix_post_g, mlp_pre_g, mlp_post_g, w_in, mu_prev, mu_next, decay_w0, decay_w2, iclr_a0, iclr_a2, k_k, k_a, r_k, gate_w2, lnx_w, lnx_b, conv_w, conv_b, conv_ln_w, conv_ln_b, w_out, mlp_w1, mlp_w2)))
    moms = dict(zip(_WEIGHT_NAMES, (m_c_ctx, m_ada_w, m_ada_b, m_mix_pre_g, m_mix_post_g, m_mlp_pre_g, m_mlp_post_g, m_w_in, m_mu_prev, m_mu_next, m_decay_w0, m_decay_w2, m_iclr_a0, m_iclr_a2, m_k_k, m_k_a, m_r_k, m_gate_w2, m_lnx_w, m_lnx_b, m_conv_w, m_conv_b, m_conv_ln_w, m_conv_ln_b, m_w_out, m_mlp_w1, m_mlp_w2)))
    vars_ = dict(zip(_WEIGHT_NAMES, (v_c_ctx, v_ada_w, v_ada_b, v_mix_pre_g, v_mix_post_g, v_mlp_pre_g, v_mlp_post_g, v_w_in, v_mu_prev, v_mu_next, v_decay_w0, v_decay_w2, v_iclr_a0, v_iclr_a2, v_k_k, v_k_a, v_r_k, v_gate_w2, v_lnx_w, v_lnx_b, v_conv_w, v_conv_b, v_conv_ln_w, v_conv_ln_b, v_w_out, v_mlp_w1, v_mlp_w2)))

    B, t_lat, _ = x.shape
    assert ctx.shape[1] == TT and t_lat % TT == 0 and (t_lat * B) % MT == 0
    T = _Tiles(B, t_lat)
    ix, iy, ic = lax.axis_index("x"), lax.axis_index("y"), lax.axis_index("c")
    chip = 2 * ix + iy
    dev = 4 * ix + 2 * iy + ic
    nsh = 4
    in_sh = w_in.shape[2]
    ada_sh = ada_w.shape[2]
    lane_sh = decay_w0.shape[2]

    big_parts = (w_in[0], w_out[0], mlp_w1[0], mlp_w2[0])
    big_shapes = [p.shape for p in big_parts]
    wg = _exchange(_pack2d([p.astype(BF16) for p in big_parts]), "chips", True, "gather_big_weights")
    per = [_unpack2d(wg[j], big_shapes) for j in range(nsh)]
    w_in_f = jnp.concatenate([per[j][0] for j in range(nsh)], axis=1)
    w_out_f = jnp.concatenate([per[j][1] for j in range(nsh)], axis=0)
    w1_f = jnp.concatenate([per[j][2] for j in range(nsh)], axis=1)
    w2_f = jnp.concatenate([per[j][3] for j in range(nsh)], axis=0)
    w_in_p = _pad_cols(w_in_f, w_in_f.shape[1])

    sm_parts = (decay_w0[0], decay_w2[0], iclr_a0[0], iclr_a2[0], gate_w2[0], conv_w[0])
    sm_shapes = [p.shape for p in sm_parts]
    sg = _exchange(_pack_rows(sm_parts), "chips", True, "gather_small_weights")
    pers = [_unpack(sg[j].reshape(-1), sm_shapes) for j in range(nsh)]
    w0_f, w2_f_, a0_f, a2_f, gate_f, convw_f = (jnp.concatenate([pers[j][t] for j in range(nsh)], axis=-1)
                                                for t in range(6))

    def pad_rows(a, n):
        return jnp.pad(a, [(0, 0)] * (a.ndim - 2) + [(0, n - a.shape[-2]), (0, 0)])

    P = dict(
        w_rw=w_in_p[:, :RWC], w_cv=w_in_p[:, RWC:], w_out=w_out_f, w1=w1_f, w2m=w2_f,
        mix_pre_g=mix_pre_g, mix_post_g=mix_post_g, mlp_pre_g=mlp_pre_g, mlp_post_g=mlp_post_g,
        mu_p=_pad_cols(mu_prev, mu_prev.shape[1]), mu_n=_pad_cols(mu_next, mu_next.shape[1]),
        w0=w0_f, w2=pad_rows(w2_f_, LRW), a0=a0_f, a2=pad_rows(a2_f, LRW), k_k=k_k, k_a=k_a,
        r_k=r_k.reshape(1, W), gate_w2=pad_rows(gate_f, GDW), lnx_w=lnx_w, lnx_b=lnx_b,
        conv_w=pad_rows(convw_f, 32), conv_b=conv_b, conv_ln_w=conv_ln_w, conv_ln_b=conv_ln_b)

    c_ctx2 = c_ctx.reshape(1, D)
    c_all = _exchange(jnp.pad(c, ((0, 8 - B), (0, 0))), "all", True, "gather_c")[:, :B].reshape(8 * B, D)
    ada_b_blk = lax.dynamic_slice(ada_b, (0, chip * ada_sh), (1, ada_sh))
    mod_blk = _ada_fwd(c_all, c_ctx2, ada_w[0], ada_b_blk)
    mod_g = _exchange(mod_blk, "chips", True, "gather_mod")
    mod_all = jnp.concatenate([mod_g[j] for j in range(nsh)], axis=1)
    mod_x = lax.dynamic_slice(mod_all, (dev * B, 0), (B, 6 * D)).reshape(B, 6, D)
    mod_c = jnp.broadcast_to(mod_all[8 * B].reshape(1, 6, D), (B, 6, D))
    modrows = jnp.stack([mod_c, mod_x], axis=1)

    xc = jnp.concatenate([ctx, x], axis=1).reshape(T.NTOK, D)
    tgt = loss_target.reshape(T.NLAT, D)
    loss_loc, dxc, small, big, dm_ = _local_step(T, xc, tgt, modrows, P)
    loss = lax.psum(loss_loc, ("x", "y", "c"))
    grad_x = dxc.reshape(B, T.TTOT, D)[:, TT:, :]

    dmod_x = jnp.concatenate([dm_["dmod1"][:, 1], dm_["dg1"], dm_["dmod2"], dm_["dg2"]], axis=1)
    dmod_c = jnp.concatenate([dm_["dmod1"][:, 0], jnp.zeros((B, 4, D), F32)], axis=1)
    dpack = jnp.concatenate([dmod_x.reshape(B, 6 * D), dmod_c.reshape(B, 6 * D)], axis=0)
    dg = _exchange(dpack, "all", True, "gather_dmod")
    ex = dg[:, :B].reshape(8 * B, 6 * D)
    cx = dg[:, B:].reshape(8 * B, 6 * D)
    ex_blk = lax.dynamic_slice(ex, (0, chip * ada_sh), (8 * B, ada_sh))
    cx_blk = lax.dynamic_slice(cx, (0, chip * ada_sh), (8 * B, ada_sh))
    g_ada_w, g_ada_b, dscc = _ada_bwd(c_all, c_ctx2, ada_w[0], ex, cx, ex_blk, cx_blk)
    dscc_g = _exchange(dscc, "chips", True, "gather_dcctx")
    g_c_ctx = _cctx_final(dscc_g, c_ctx2).reshape(D)

    small = dict(small, mu_p=_unpad_cols(small["mu_p"], mu_prev.shape[1]),
                 mu_n=_unpad_cols(small["mu_n"], mu_next.shape[1]),
                 w2=small["w2"][:, :decay_w2.shape[2]], a2=small["a2"][:, :iclr_a2.shape[2]],
                 gate_w2=small["gate_w2"][:gate_w2.shape[1]], conv_w=small["conv_w"][:KCONV])
    sm_list = [small[n] for n in _SMALL_ORDER]
    sm_shapes2 = [a.shape for a in sm_list]
    me1 = ic.reshape(1).astype(jnp.int32)
    sm_pack = _pack_rows(sm_list)
    sm_pair = _sib_stream(jnp.stack([sm_pack, sm_pack])[:, None], me1, "sib_small_grads", True)[0]
    sm_tot = _sum_slots(_exchange(sm_pair, "chips", True, "gather_small_grads"), "sum_small_grads")
    S = dict(zip(_SMALL_ORDER, _unpack(sm_tot.reshape(-1), sm_shapes2)))

    def shard_last(a):
        return lax.dynamic_slice_in_dim(a, chip * lane_sh, lane_sh, axis=a.ndim - 1)

    grads = dict(
        c_ctx=g_c_ctx, ada_w=g_ada_w[None], ada_b=g_ada_b,
        mix_pre_g=S["mix_pre_g"], mix_post_g=S["mix_post_g"], mlp_pre_g=S["mlp_pre_g"], mlp_post_g=S["mlp_post_g"],
        mu_prev=S["mu_p"], mu_next=S["mu_n"],
        decay_w0=shard_last(S["w0"])[None], decay_w2=shard_last(S["w2"])[None],
        iclr_a0=shard_last(S["a0"])[None], iclr_a2=shard_last(S["a2"])[None],
        k_k=S["k_k"], k_a=S["k_a"], r_k=S["r_k"].reshape(r_k.shape),
        gate_w2=shard_last(S["gate_w2"])[None], lnx_w=S["lnx_w"], lnx_b=S["lnx_b"],
        conv_w=shard_last(S["conv_w"])[None], conv_b=S["conv_b"], conv_ln_w=S["conv_ln_w"],
        conv_ln_b=S["conv_ln_b"])

    dw_in_f = _unpad_cols(jnp.concatenate([big["w_rw"], big["w_cv"]], axis=1), w_in_f.shape[1])
    oshape = w_out.shape[1]
    mshape = mlp_w1.shape[2]
    slabs = [
        _pack2d([dw_in_f[:, in_sh * j:in_sh * (j + 1)], big["w_out"][oshape * j:oshape * (j + 1)],
                 big["w1"][:, mshape * j:mshape * (j + 1)], big["w2m"][mshape * j:mshape * (j + 1)]])
        for j in range(nsh)]
    nck = 3
    lr, lc = slabs[0].shape
    hr = lr // 2
    cr = hr // nck
    assert cr * 2 * nck == lr and cr % 8 == 0
    halves = jnp.stack([jnp.concatenate([s[h * hr:(h + 1) * hr].reshape(nck, cr, lc) for s in slabs], axis=0)
                        for h in (0, 1)])
    pair = _sib_stream(halves, me1, "sib_reduce_grads", True)
    mine = _sum_slots(_exchange(pair.reshape(nsh, nck * cr, lc), "chips", False, "reduce_big_grads"), "sum_big_grads")
    mine = mine.reshape(nck, cr, lc)
    other = _sib_stream(mine, me1, "sib_swap_grads", False)
    tot = jnp.where(ic == 0, jnp.concatenate([mine, other], axis=0), jnp.concatenate([other, mine], axis=0))
    g_w_in, g_w_out, g_w1, g_w2 = _unpack2d(tot.reshape(lr, lc), big_shapes)
    grads.update(w_in=g_w_in[None], w_out=g_w_out[None], mlp_w1=g_w1[None], mlp_w2=g_w2[None])

    deltas, new_m, new_v = {}, {}, {}
    for n in _WEIGHT_NAMES:
        g = grads[n].reshape(weights[n].shape)
        grads[n] = g
        deltas[n], new_m[n], new_v[n] = _adamw(weights[n], g, moms[n], vars_[n], "adamw_" + n)

    return (loss, grad_x, *[grads[n] for n in _WEIGHT_NAMES], *[deltas[n] for n in _WEIGHT_NAMES],
            *[new_m[n] for n in _WEIGHT_NAMES], *[new_v[n] for n in _WEIGHT_NAMES])
```

```python
import functools

import jax
import jax.numpy as jnp
from jax import lax
from jax.experimental import pallas as pl
from jax.experimental.pallas import tpu as pltpu

F32 = jnp.float32
BF16 = jnp.bfloat16
HI = lax.Precision.HIGHEST

D = 1024
W = 512
HS = 64
RWC = 2304
CVC = 1024
GDW = 256
LRW = 128
DFF = 4096
TT = 256
LINE = 64
KCONV = 31
EPS_RMS = 1e-6
EPS_LN = 1e-5
EPS_GN = 64e-5
SCAN_CH = 128
SCAN_G = 8
SCAN_BSUB = 16

ADAM_LR = 0.001
ADAM_B1 = 0.9
ADAM_B2 = 0.999
ADAM_EPS = 1e-08
ADAM_WD = 0.01
ADAM_STEP = 10

_SEGS = ((0, 1536, 1536), (1536, 64, 128), (1600, 64, 128), (1664, 64, 128), (1728, 64, 128),
         (1792, 160, 256), (1952, 1024, 1024))

MESH = pl.DeviceIdType.MESH


def _bs(shape, imap):
    return pl.BlockSpec(shape, imap)


def _cp(sem=None, mb=48):
    return pltpu.CompilerParams(dimension_semantics=sem, vmem_limit_bytes=mb << 20)


def _pad_cols(a, ncols):
    out = []
    for s, w, pw in _SEGS:
        if s >= ncols:
            break
        piece = a[..., s:s + w]
        if pw > w:
            piece = jnp.pad(piece, [(0, 0)] * (a.ndim - 1) + [(0, pw - w)])
        out.append(piece)
    return jnp.concatenate(out, axis=-1)


def _unpad_cols(a, ncols):
    out = []
    off = 0
    for s, w, pw in _SEGS:
        if s >= ncols:
            break
        out.append(a[..., off:off + w])
        off += pw
    return jnp.concatenate(out, axis=-1)


def _sigmoid(x):
    return 1.0 / (1.0 + jnp.exp(-x))


def _softplus(x):
    return jnp.maximum(x, 0.0) + jnp.log(1.0 + jnp.exp(-jnp.abs(x)))


def _e128(dtype):
    r = lax.broadcasted_iota(jnp.int32, (128, 128), 0) >= HS
    c = lax.broadcasted_iota(jnp.int32, (128, 128), 1) >= HS
    return (r == c).astype(dtype)


def _segsum(x, e):
    hi = x.astype(BF16)
    lo = (x - hi.astype(F32)).astype(BF16)
    return jnp.concatenate(
        [jnp.dot(hi[:, 128 * g:128 * (g + 1)], e, preferred_element_type=F32)
         + jnp.dot(lo[:, 128 * g:128 * (g + 1)], e, preferred_element_type=F32) for g in range(4)], axis=1)


_NT_DIMS = (((1,), (1,)), ((), ()))
_TN_DIMS = (((0,), (0,)), ((), ()))


def _bdot(a, b, dims=None):
    a = a.astype(BF16)
    b = b.astype(BF16)
    if dims is None:
        return jnp.dot(a, b, preferred_element_type=F32)
    return lax.dot_general(a, b, dims, preferred_element_type=F32)


def _colsum(x):
    return jnp.sum(x, axis=0, keepdims=True)


def _rowmean(x):
    return jnp.mean(x, axis=-1, keepdims=True)


def _diag(x, npairs):
    row = lax.broadcasted_iota(jnp.int32, (HS, 128), 0)
    lane = lax.broadcasted_iota(jnp.int32, (HS, 128), 1) & (HS - 1)
    keep = jnp.broadcast_to((lane == row)[None], (npairs, HS, 128))
    return jnp.where(keep, x.reshape(npairs, HS, 128), 0.0).reshape(npairs * HS, 128)


def _segb(x, e):
    return jnp.dot(x.astype(BF16), e, preferred_element_type=F32)


_segb1 = _segb


def _expand(row, npairs):
    return jnp.concatenate([jnp.broadcast_to(row[:, 128 * j:128 * (j + 1)], (HS, 128)) for j in range(npairs)], axis=0)


def _colb(row, npairs, e):
    return _segb1(_diag(_expand(row, npairs), npairs), e)


def _pair_colsum(x, npairs):
    return jnp.concatenate([_colsum(x[HS * j:HS * (j + 1)]) for j in range(npairs)], axis=1)


def _conv_pos():
    return lax.broadcasted_iota(jnp.int32, (TT, W), 0) & (LINE - 1)


def _shifted(u, s, pos):
    if s == 0:
        return u
    sh = pltpu.roll(u, (-s) % TT, 0)
    valid = jnp.logical_and(pos + s >= 0, pos + s < LINE)
    return jnp.where(valid, sh, 0.0)


def _acc(ref, val, first):
    @pl.when(first)
    def _():
        ref[...] = jnp.zeros(ref.shape, ref.dtype)
    ref[...] += val


class _Tiles:
    def __init__(self, B, t_lat):
        self.B = B
        self.NLT = t_lat // TT
        self.TPS = self.NLT + 1
        self.NT = B * self.TPS
        self.NL = B * self.NLT
        self.NTOK = self.NT * TT
        self.NLAT = self.NL * TT
        self.TTOT = self.TPS * TT
        self.BW = B * W

    def b(self, i):
        return i // self.TPS

    def q(self, i):
        return i % self.TPS

    def lat(self, i):
        return (i // self.TPS) * self.NLT + jnp.maximum(i % self.TPS - 1, 0)

    def tok(self, l):
        return (l // self.NLT) * self.TPS + 1 + l % self.NLT

    def mod_spec(self):
        return _bs((1, 1, 6, D), lambda i: (i // self.TPS, jnp.minimum(i % self.TPS, 1), 0, 0))

    def tm_spec(self):
        return _bs((TT, W), lambda i: (i % self.TPS, i // self.TPS))

    def tm2_spec(self):
        return _bs((2, TT, W), lambda i: (0, i % self.TPS, i // self.TPS))


def _row(shape_last):
    return _bs((1, shape_last), lambda i: (0, 0))


def _tok_specs(T):
    return [_bs((TT, D), lambda i: (T.lat(i), 0)), _bs((TT, D), lambda i: (i // T.TPS, 0))]


def _tok_tile(T, x_ref, c_ref):
    is_ctx = (pl.program_id(0) % T.TPS == 0).astype(F32)
    return c_ref[...] * is_ctx + x_ref[...] * (1.0 - is_ctx)


def _mix_in(T, x2, c2, modrows, g, w_rw, w_cv):
    def kern(x_ref, c_ref, mod_ref, g_ref, wr_ref, wc_ref, prw_ref, pcv_ref, h_ref):
        x = _tok_tile(T, x_ref, c_ref)
        s = lax.rsqrt(_rowmean(x * x) + EPS_RMS)
        h = (x * s * g_ref[...]) * (1.0 + mod_ref[0, 0, 1:2, :]) + mod_ref[0, 0, 0:1, :]
        hb = h.astype(BF16)
        h_ref[...] = hb
        prw_ref[...] = jnp.dot(hb, wr_ref[...], preferred_element_type=F32)
        pcv_ref[...] = jnp.dot(hb, wc_ref[...], preferred_element_type=F32)

    return pl.pallas_call(
        kern, grid=(T.NT,), name="mix_in",
        in_specs=_tok_specs(T) + [T.mod_spec(), _row(D),
                                  _bs((D, RWC), lambda i: (0, 0)), _bs((D, CVC), lambda i: (0, 0))],
        out_specs=[_bs((TT, RWC), lambda i: (i, 0)), _bs((TT, CVC), lambda i: (i, 0)), _bs((TT, D), lambda i: (i, 0))],
        out_shape=[jax.ShapeDtypeStruct((T.NTOK, RWC), F32), jax.ShapeDtypeStruct((T.NTOK, CVC), F32),
                   jax.ShapeDtypeStruct((T.NTOK, D), BF16)],
        compiler_params=_cp(("parallel",)),
    )(x2, c2, modrows, g, w_rw, w_cv)


def _halo_specs(T):
    nb8 = T.NTOK // 8
    prev = _bs((8, RWC), lambda i: (jnp.maximum(i * (TT // 8) - 1, 0), 0))
    nxt = _bs((8, RWC), lambda i: (jnp.minimum((i + 1) * (TT // 8), nb8 - 1), 0))
    return prev, nxt


def _halo_masks(T, i):
    q = i % T.TPS
    has_prev = jnp.logical_and(q != 0, q != 1).astype(F32)
    has_next = jnp.logical_and(q != 0, q != T.TPS - 1).astype(F32)
    return has_prev, has_next


def _neighbours(z, prev_row, next_row):
    rowi = lax.broadcasted_iota(jnp.int32, z.shape, 0)
    zprev = jnp.where(rowi == 0, prev_row, pltpu.roll(z, 1, 0))
    znext = jnp.where(rowi == TT - 1, next_row, pltpu.roll(z, TT - 1, 0))
    return zprev, znext


def _prep_math(rw, w0, w2, a0, a2, k_k, k_a, e):
    r = rw[:, 0:512]
    k = rw[:, 512:1024]
    v = rw[:, 1024:1536]
    kr = k * k_k
    ss = _segsum(kr * kr, e)
    rt = jnp.sqrt(ss)
    inv = 1.0 / jnp.maximum(rt, 1e-12)
    kk = kr * inv
    o = dict(r=r, k=k, v=v, kr=kr, rt=rt, inv=inv, kk=kk, th=[], pre=[], ex=[], dec=[], iclr=[], kd=[], bb=[], ad=[])
    for d in (0, 1):
        wd = rw[:, 1536 + LRW * d:1536 + LRW * (d + 1)]
        ad = rw[:, 1792 + LRW * d:1792 + LRW * (d + 1)]
        th = jnp.tanh(wd)
        pre = w0[d] + _bdot(th, w2[d])
        ex = jnp.exp(-_softplus(-pre) - 0.5)
        dec = jnp.exp(-ex)
        iclr = _sigmoid(a0[d] + _bdot(ad, a2[d]))
        o["th"].append(th)
        o["pre"].append(pre)
        o["ex"].append(ex)
        o["dec"].append(dec)
        o["iclr"].append(iclr)
        o["ad"].append(ad)
        o["kd"].append(k * (1.0 + (iclr - 1.0) * k_a))
        o["bb"].append(kk * iclr)
    return o


def _load_prep_params(w0_ref, w2_ref, a0_ref, a2_ref):
    w0 = [w0_ref[0:1, :], w0_ref[1:2, :]]
    a0 = [a0_ref[0:1, :], a0_ref[1:2, :]]
    w2 = [w2_ref[0], w2_ref[1]]
    a2 = [a2_ref[0], a2_ref[1]]
    return w0, w2, a0, a2


def _prep_param_specs():
    return [_bs((2, W), lambda i: (0, 0)), _bs((2, LRW, W), lambda i: (0, 0, 0)),
            _bs((2, W), lambda i: (0, 0)), _bs((2, LRW, W), lambda i: (0, 0, 0)), _row(W), _row(W)]


def _rwkv_prep(T, p_rw, mu_p, mu_n, w0, w2, a0, a2, k_k, k_a):
    def kern(p_ref, pp_ref, pn_ref, mp_ref, mn_ref, w0_ref, w2_ref, a0_ref, a2_ref, kk_ref, ka_ref,
             r_o, v_o, kk_o, dec_o, kd_o, bb_o, rw_o):
        i = pl.program_id(0)
        has_prev, has_next = _halo_masks(T, i)
        z = p_ref[...]
        zprev, znext = _neighbours(z, pp_ref[7:8, :] * has_prev, pn_ref[0:1, :] * has_next)
        rw = z + mp_ref[...] * (zprev - z) + mn_ref[...] * (znext - z)
        rw_o[...] = rw
        w0v, w2v, a0v, a2v = _load_prep_params(w0_ref, w2_ref, a0_ref, a2_ref)
        o = _prep_math(rw, w0v, w2v, a0v, a2v, kk_ref[...], ka_ref[...], _e128(BF16))
        r_o[...] = o["r"]
        v_o[...] = o["v"]
        kk_o[...] = o["kk"]
        for d in (0, 1):
            dec_o[d] = o["dec"][d]
            kd_o[d] = o["kd"][d]
            bb_o[d] = o["bb"][d]

    prev, nxt = _halo_specs(T)
    tm = jax.ShapeDtypeStruct((T.TTOT, T.BW), F32)
    tm2 = jax.ShapeDtypeStruct((2, T.TTOT, T.BW), F32)
    return pl.pallas_call(
        kern, grid=(T.NT,), name="rwkv_prep",
        in_specs=[_bs((TT, RWC), lambda i: (i, 0)), prev, nxt, _row(RWC), _row(RWC)] + _prep_param_specs(),
        out_specs=[T.tm_spec(), T.tm_spec(), T.tm_spec(), T.tm2_spec(), T.tm2_spec(), T.tm2_spec(),
                   _bs((TT, RWC), lambda i: (i, 0))],
        out_shape=[tm, tm, tm, tm2, tm2, tm2, jax.ShapeDtypeStruct((T.NTOK, RWC), F32)],
        compiler_params=_cp(("parallel",)),
    )(p_rw, p_rw, p_rw, mu_p, mu_n, w0, w2, a0, a2, k_k, k_a)


def _scan_fwd(T, r, v, kk, dec, kd, bb):
    NP = T.BW // 128
    R = NP * HS
    NCH = T.TTOT // SCAN_CH
    NCC = TT // SCAN_CH
    G = SCAN_G
    NG = SCAN_CH // G
    NGRP = 4
    assert NG % NGRP == 0

    def tmap(d, i):
        rev = jnp.where(i < NCC, NCC - 1 - i, NCH - 1 - (i - NCC))
        return jnp.where(d == 0, i, rev)

    def kern(r_ref, v_ref, kk_ref, dec_ref, kd_ref, bb_ref, y_ref, fin_ref, hist_ref, ring, sems):
        d = pl.program_id(0)
        i = pl.program_id(1)

        @pl.when(i == 0)
        def _():
            ring[0] = jnp.zeros((R, 128), F32)

        e = _e128(BF16)

        def hist_copy(k):
            grp = k % NGRP
            return pltpu.make_async_copy(ring.at[pl.ds(grp * G, G)],
                                         hist_ref.at[d, pl.ds(i * SCAN_CH + k * G, G)], sems.at[grp])

        def body(k, carry):
            @pl.when(k >= NGRP - 1)
            def _():
                hist_copy(k - (NGRP - 1)).wait()

            base = (k % NGRP) * G
            for u in range(G):
                t = k * G + u
                row = jnp.where(d == 0, t, SCAN_CH - 1 - t)
                s = ring[base + u]
                sa = _segb(s * _expand(-kk_ref[pl.ds(row, 1), :], NP), e)
                vc = _colb(v_ref[pl.ds(row, 1), :], NP, e)
                s = (s * _expand(dec_ref[0, pl.ds(row, 1), :], NP) + sa * _expand(bb_ref[0, pl.ds(row, 1), :], NP)
                     + vc * _expand(kd_ref[0, pl.ds(row, 1), :], NP))
                ring[(base + u + 1) if u < G - 1 else ((k + 1) % NGRP) * G] = s
                yb = _segb(s * _expand(r_ref[pl.ds(row, 1), :], NP), e)
                y_ref[0, pl.ds(row, 1), :] = _pair_colsum(_diag(yb, NP), NP)
            hist_copy(k).start()
            return carry

        lax.fori_loop(0, NG, body, 0)
        for k in range(NG - (NGRP - 1), NG):
            hist_copy(k).wait()

        @pl.when(i == NCH - 1)
        def _():
            fin_ref[0] = ring[0]

    sh = _bs((SCAN_CH, T.BW), lambda d, i: (tmap(d, i), 0))
    dr = _bs((1, SCAN_CH, T.BW), lambda d, i: (d, tmap(d, i), 0))
    return pl.pallas_call(
        kern, grid=(2, NCH), name="scan_fwd",
        in_specs=[sh, sh, sh, dr, dr, dr],
        out_specs=[dr, _bs((1, R, 128), lambda d, i: (d, 0, 0)), pl.BlockSpec(memory_space=pl.ANY)],
        out_shape=[jax.ShapeDtypeStruct((2, T.TTOT, T.BW), F32), jax.ShapeDtypeStruct((2, R, 128), F32),
                   jax.ShapeDtypeStruct((2, T.TTOT, R, 128), F32)],
        scratch_shapes=[pltpu.VMEM((NGRP * G, R, 128), F32), pltpu.SemaphoreType.DMA((NGRP,))],
        compiler_params=_cp(("arbitrary", "arbitrary")),
    )(r, v, kk, dec, kd, bb)


def _readout_fwd(y, r, v, gd, kbar, rk, gw, lw, lb, e):
    mu = _segsum(y, e) * (1.0 / HS)
    yc = y - mu
    var = _segsum(yc * yc, e) * (1.0 / HS)
    rstd = lax.rsqrt(var + EPS_GN)
    yhat = yc * rstd
    yn = yhat * lw + lb
    q = _segsum(r * kbar * rk, e)
    sg = _sigmoid(gd)
    gg = _bdot(sg, gw)
    return dict(yhat=yhat, rstd=rstd, yn=yn, q=q, sg=sg, gg=gg, out=(yn + q * v) * gg)


def _conv_fwd(cva, cvb, cw_ref, cb, lw, lb, c=None):
    pos = _conv_pos()
    sgb = _sigmoid(cvb)
    u = cva * sgb
    if c is None:
        c = jnp.zeros_like(u)
        for j in range(KCONV):
            c = c + cw_ref[j:j + 1, :] * _shifted(u, j - KCONV // 2, pos)
        c = c + cb
    mu = _rowmean(c)
    cc = c - mu
    rstd = lax.rsqrt(_rowmean(cc * cc) + EPS_LN)
    chat = cc * rstd
    cn = chat * lw + lb
    scn = _sigmoid(cn)
    return dict(sgb=sgb, u=u, c=c, chat=chat, rstd=rstd, cn=cn, scn=scn, out=cn * scn, pos=pos)


def _mix_out(T, y, kd, rw, p_cv, x2, modrows, rk, gw, lnw, lnb, cw, cb, clw, clb, pg, w_out):
    tk = T.tok

    def kern(y_ref, kd_ref, rw_ref, pcv_ref, x_ref, mod_ref, rk_ref, gw_ref, lw_ref, lb_ref, cw_ref, cb_ref,
             clw_ref, clb_ref, pg_ref, wo_ref, cat_o, mix_o, x1_o, conv_o):
        e = _e128(BF16)
        ro = _readout_fwd(y_ref[0] + y_ref[1], rw_ref[:, 0:512], rw_ref[:, 1024:1536], rw_ref[:, 2048:2304],
                          0.5 * (kd_ref[0] + kd_ref[1]), rk_ref[...], gw_ref[...], lw_ref[...], lb_ref[...], e)
        cv = _conv_fwd(pcv_ref[:, 0:512], pcv_ref[:, 512:1024], cw_ref, cb_ref[...], clw_ref[...], clb_ref[...])
        conv_o[...] = cv["c"]
        catb = jnp.concatenate([ro["out"], cv["out"]], axis=1).astype(BF16)
        cat_o[...] = catb
        mix = jnp.dot(catb, wo_ref[...], preferred_element_type=F32)
        mix_o[...] = mix
        sm = lax.rsqrt(_rowmean(mix * mix) + EPS_RMS)
        x1_o[...] = x_ref[...] + mod_ref[0, 0, 2:3, :] * (mix * sm * pg_ref[...])

    lat = lambda l: (l, 0)
    return pl.pallas_call(
        kern, grid=(T.NL,), name="mix_out",
        in_specs=[_bs((2, TT, W), lambda l: (0, 1 + l % T.NLT, l // T.NLT)),
                  _bs((2, TT, W), lambda l: (0, 1 + l % T.NLT, l // T.NLT)),
                  _bs((TT, RWC), lambda l: (tk(l), 0)), _bs((TT, CVC), lambda l: (tk(l), 0)),
                  _bs((TT, D), lambda l: (l, 0)),
                  _bs((1, 1, 6, D), lambda l: (l // T.NLT, 1, 0, 0)),
                  _row(W), _bs((GDW, W), lambda l: (0, 0)), _row(W), _row(W),
                  _bs((32, W), lambda l: (0, 0)), _row(W), _row(W), _row(W), _row(D),
                  _bs((D, D), lambda l: (0, 0))],
        out_specs=[_bs((TT, D), lat), _bs((TT, D), lat), _bs((TT, D), lat), _bs((TT, W), lat)],
        out_shape=[jax.ShapeDtypeStruct((T.NLAT, D), BF16), jax.ShapeDtypeStruct((T.NLAT, D), F32),
                   jax.ShapeDtypeStruct((T.NLAT, D), F32), jax.ShapeDtypeStruct((T.NLAT, W), F32)],
        compiler_params=_cp(("parallel",)),
    )(y, kd, rw, p_cv, x2, modrows, rk, gw, lnw, lnb, cw, cb, clw, clb, pg, w_out)


MT = 512
FC = 1024


def _mlp_fwd(T, x1, modrows, g, w1, w2):
    per_b = T.NLT * TT // MT

    def kern(x_ref, mod_ref, g_ref, w1_ref, w2_ref, m_o, h2_o, h2_s):
        f = pl.program_id(1)

        @pl.when(f == 0)
        def _():
            x = x_ref[...]
            s = lax.rsqrt(_rowmean(x * x) + EPS_RMS)
            h2 = (x * s * g_ref[...]) * (1.0 + mod_ref[0, 0, 4:5, :]) + mod_ref[0, 0, 3:4, :]
            h2_s[...] = h2.astype(BF16)
            h2_o[...] = h2.astype(BF16)
            m_o[...] = jnp.zeros_like(m_o)

        a = jnp.dot(h2_s[...], w1_ref[...], preferred_element_type=F32)
        rl = jnp.maximum(a, 0.0)
        m_o[...] += jnp.dot((rl * rl).astype(BF16), w2_ref[...], preferred_element_type=F32)

    tok = lambda t, f: (t, 0)
    return pl.pallas_call(
        kern, grid=(T.NLAT // MT, DFF // FC), name="mlp_fwd",
        in_specs=[_bs((MT, D), tok), _bs((1, 1, 6, D), lambda t, f: (t // per_b, 1, 0, 0)),
                  _bs((1, D), lambda t, f: (0, 0)), _bs((D, FC), lambda t, f: (0, f)), _bs((FC, D), lambda t, f: (f, 0))],
        out_specs=[_bs((MT, D), tok), _bs((MT, D), tok)],
        out_shape=[jax.ShapeDtypeStruct((T.NLAT, D), F32), jax.ShapeDtypeStruct((T.NLAT, D), BF16)],
        scratch_shapes=[pltpu.VMEM((MT, D), BF16)],
        compiler_params=_cp(("parallel", "arbitrary")),
    )(x1, modrows, g, w1, w2)


def _loss_head(T, m, x1, tgt, modrows, pg):
    def kern(m_ref, x1_ref, t_ref, mod_ref, pg_ref, loss_o, dm_o, dx2_o, dg2_o, dpg_o):
        l = pl.program_id(0)
        m_ = m_ref[...]
        sm = lax.rsqrt(_rowmean(m_ * m_) + EPS_RMS)
        mn = m_ * sm
        g2 = mod_ref[0, 0, 5:6, :]
        pgv = pg_ref[...]
        diff = x1_ref[...] + g2 * (mn * pgv) - t_ref[...]
        sq = jnp.sum(_colsum(diff * diff), axis=1, keepdims=True)
        _acc(loss_o, jnp.zeros((8, 128), F32) + (0.5 / D) * sq, l == 0)
        dx2 = diff * (1.0 / D)
        dx2_o[...] = dx2
        _acc(dg2_o.at[0], _colsum(dx2 * mn * pgv), l % T.NLT == 0)
        _acc(dpg_o, _colsum(dx2 * g2 * mn), l == 0)
        dmn = dx2 * g2 * pgv
        dm_o[...] = (sm * (dmn - mn * _rowmean(dmn * mn))).astype(BF16)

    lat = lambda l: (l, 0)
    return pl.pallas_call(
        kern, grid=(T.NL,), name="loss_head",
        in_specs=[_bs((TT, D), lat), _bs((TT, D), lat), _bs((TT, D), lat),
                  _bs((1, 1, 6, D), lambda l: (l // T.NLT, 1, 0, 0)), _row(D)],
        out_specs=[_bs((8, 128), lambda l: (0, 0)), _bs((TT, D), lat), _bs((TT, D), lat),
                   _bs((1, 1, D), lambda l: (l // T.NLT, 0, 0)), _row(D)],
        out_shape=[jax.ShapeDtypeStruct((8, 128), F32), jax.ShapeDtypeStruct((T.NLAT, D), BF16),
                   jax.ShapeDtypeStruct((T.NLAT, D), F32), jax.ShapeDtypeStruct((T.B, 1, D), F32),
                   jax.ShapeDtypeStruct((1, D), F32)],
        compiler_params=_cp(("arbitrary",)),
    )(m, x1, tgt, modrows, pg)


def _mlp_bwd(T, h2, dm, w1, w2):
    def kern(h2_ref, dm_ref, w1_ref, w2_ref, f_o, da_o, dh2_o):
        f = pl.program_id(1)
        a = jnp.dot(h2_ref[...], w1_ref[...], preferred_element_type=F32)
        rl = jnp.maximum(a, 0.0)
        f_o[...] = (rl * rl).astype(BF16)
        df = lax.dot_general(dm_ref[...], w2_ref[...], _NT_DIMS, preferred_element_type=F32)
        dab = (df * (2.0 * rl)).astype(BF16)
        da_o[...] = dab
        _acc(dh2_o, lax.dot_general(dab, w1_ref[...], _NT_DIMS, preferred_element_type=F32), f == 0)

    tok = lambda t, f: (t, 0)
    return pl.pallas_call(
        kern, grid=(T.NLAT // MT, DFF // FC), name="mlp_bwd",
        in_specs=[_bs((MT, D), tok), _bs((MT, D), tok), _bs((D, FC), lambda t, f: (0, f)),
                  _bs((FC, D), lambda t, f: (f, 0))],
        out_specs=[_bs((MT, FC), lambda t, f: (t, f)), _bs((MT, FC), lambda t, f: (t, f)), _bs((MT, D), tok)],
        out_shape=[jax.ShapeDtypeStruct((T.NLAT, DFF), BF16), jax.ShapeDtypeStruct((T.NLAT, DFF), BF16),
                   jax.ShapeDtypeStruct((T.NLAT, D), F32)],
        compiler_params=_cp(("parallel", "arbitrary")),
    )(h2, dm, w1, w2)


def _mlp_in_bwd(T, dh2, x1, dx2, modrows, g):
    def kern(dh_ref, x1_ref, dx2_ref, mod_ref, g_ref, dx1_o, dmod_o, dg_o):
        i = pl.program_id(0)
        lat = (i % T.TPS != 0).astype(F32)
        x = x1_ref[...]
        s = lax.rsqrt(_rowmean(x * x) + EPS_RMS)
        xh = x * s
        gv = g_ref[...]
        dh = dh_ref[...] * lat
        n2 = xh * gv
        first_b = i % T.TPS == 0
        _acc(dmod_o.at[0, 0:1, :], _colsum(dh), first_b)
        _acc(dmod_o.at[0, 1:2, :], _colsum(dh * n2), first_b)
        dn2 = dh * (1.0 + mod_ref[0, 0, 4:5, :])
        _acc(dg_o, _colsum(dn2 * xh), i == 0)
        dxh = dn2 * gv
        dx1_o[...] = (dx2_ref[...] + s * (dxh - xh * _rowmean(dxh * xh))) * lat

    lat_i = lambda i: (T.lat(i), 0)
    return pl.pallas_call(
        kern, grid=(T.NT,), name="mlp_in_bwd",
        in_specs=[_bs((TT, D), lat_i), _bs((TT, D), lat_i), _bs((TT, D), lat_i),
                  _bs((1, 1, 6, D), lambda i: (i // T.TPS, 1, 0, 0)), _row(D)],
        out_specs=[_bs((TT, D), lambda i: (i, 0)), _bs((1, 2, D), lambda i: (i // T.TPS, 0, 0)), _row(D)],
        out_shape=[jax.ShapeDtypeStruct((T.NTOK, D), F32), jax.ShapeDtypeStruct((T.B, 2, D), F32),
                   jax.ShapeDtypeStruct((1, D), F32)],
        compiler_params=_cp(("arbitrary",)),
    )(dh2, x1, dx2, modrows, g)


def _mix_post_bwd(T, dx1, mix, modrows, pg, w_out):
    def kern(dx_ref, mix_ref, mod_ref, pg_ref, wo_ref, dmix_o, dcat_o, dg1_o, dpg_o):
        i = pl.program_id(0)
        lat = (i % T.TPS != 0).astype(F32)
        dx = dx_ref[...]
        mix = mix_ref[...]
        sm = lax.rsqrt(_rowmean(mix * mix) + EPS_RMS)
        mh = mix * sm
        g1 = mod_ref[0, 0, 2:3, :]
        pgv = pg_ref[...]
        _acc(dg1_o.at[0], _colsum(dx * mh * pgv), i % T.TPS == 0)
        _acc(dpg_o, _colsum(dx * g1 * mh), i == 0)
        dmh = dx * g1 * pgv
        dmix = ((sm * (dmh - mh * _rowmean(dmh * mh))) * lat).astype(BF16)
        dmix_o[...] = dmix
        dcat_o[...] = lax.dot_general(dmix, wo_ref[...], _NT_DIMS, preferred_element_type=F32)

    tok = lambda i: (i, 0)
    return pl.pallas_call(
        kern, grid=(T.NT,), name="mix_post_bwd",
        in_specs=[_bs((TT, D), tok), _bs((TT, D), lambda i: (T.lat(i), 0)),
                  _bs((1, 1, 6, D), lambda i: (i // T.TPS, 1, 0, 0)), _row(D), _bs((D, D), lambda i: (0, 0))],
        out_specs=[_bs((TT, D), tok), _bs((TT, D), tok), _bs((1, 1, D), lambda i: (i // T.TPS, 0, 0)), _row(D)],
        out_shape=[jax.ShapeDtypeStruct((T.NTOK, D), BF16), jax.ShapeDtypeStruct((T.NTOK, D), F32),
                   jax.ShapeDtypeStruct((T.B, 1, D), F32), jax.ShapeDtypeStruct((1, D), F32)],
        compiler_params=_cp(("arbitrary",)),
    )(dx1, mix, modrows, pg, w_out)


def _conv_bwd(T, dcat, p_cv, conv, cw, cb, clw, clb):
    def kern(dc_ref, pcv_ref, conv_ref, cw_ref, cb_ref, clw_ref, clb_ref, dp_o, dcw_o, dcb_o, dlw_o, dlb_o):
        i = pl.program_id(0)
        is_lat = i % T.TPS != 0

        @pl.when(i == 0)
        def _():
            for ref in (dcw_o, dcb_o, dlw_o, dlb_o):
                ref[...] = jnp.zeros(ref.shape, ref.dtype)

        @pl.when(jnp.logical_not(is_lat))
        def _():
            dp_o[...] = jnp.zeros(dp_o.shape, dp_o.dtype)

        @pl.when(is_lat)
        def _():
            cva = pcv_ref[:, 0:512]
            cv = _conv_fwd(cva, pcv_ref[:, 512:1024], cw_ref, cb_ref[...], clw_ref[...], clb_ref[...],
                           c=conv_ref[...])
            scn = cv["scn"]
            dcn = dc_ref[...] * (scn * (1.0 + cv["cn"] * (1.0 - scn)))
            chat = cv["chat"]
            dlw_o[...] += _colsum(dcn * chat)
            dlb_o[...] += _colsum(dcn)
            dchat = dcn * clw_ref[...]
            dc = cv["rstd"] * (dchat - _rowmean(dchat) - chat * _rowmean(dchat * chat))
            dcb_o[...] += _colsum(dc)
            pos = cv["pos"]
            u = cv["u"]
            du = jnp.zeros_like(u)
            for j in range(KCONV):
                s = j - KCONV // 2
                dcw_o[j:j + 1, :] += _colsum(dc * _shifted(u, s, pos))
                du = du + cw_ref[j:j + 1, :] * _shifted(dc, -s, pos)
            sgb = cv["sgb"]
            dp_o[...] = jnp.concatenate([du * sgb, du * cva * sgb * (1.0 - sgb)], axis=1).astype(BF16)

    return pl.pallas_call(
        kern, grid=(T.NT,), name="conv_bwd",
        in_specs=[_bs((TT, W), lambda i: (i, 1)), _bs((TT, CVC), lambda i: (i, 0)),
                  _bs((TT, W), lambda i: (T.lat(i), 0)),
                  _bs((32, W), lambda i: (0, 0)), _row(W), _row(W), _row(W)],
        out_specs=[_bs((TT, CVC), lambda i: (i, 0)), _bs((32, W), lambda i: (0, 0)), _row(W), _row(W), _row(W)],
        out_shape=[jax.ShapeDtypeStruct((T.NTOK, CVC), BF16), jax.ShapeDtypeStruct((32, W), F32),
                   jax.ShapeDtypeStruct((1, W), F32), jax.ShapeDtypeStruct((1, W), F32),
                   jax.ShapeDtypeStruct((1, W), F32)],
        compiler_params=_cp(("arbitrary",)),
    )(dcat, p_cv, conv, cw, cb, clw, clb)


def _readout_bwd(T, dcat, y, kd, rw, rk, gw, lnw, lnb):
    def kern(dc_ref, y_ref, kd_ref, rw_ref, rk_ref, gw_ref, lw_ref, lb_ref,
             dy_o, dr_o, dv_o, dkb_o, dgd_o, drk_o, dgw_o, dlw_o, dlb_o):
        i = pl.program_id(0)
        first = i == 0
        e = _e128(BF16)
        r = rw_ref[:, 0:512]
        v = rw_ref[:, 1024:1536]
        kbar = 0.5 * (kd_ref[0] + kd_ref[1])
        rk = rk_ref[...]
        ro = _readout_fwd(y_ref[0] + y_ref[1], r, v, rw_ref[:, 2048:2304], kbar, rk, gw_ref[...],
                          lw_ref[...], lb_ref[...], e)
        dout = dc_ref[...]
        dgg = dout * (ro["yn"] + ro["q"] * v)
        t1 = dout * ro["gg"]
        yhat = ro["yhat"]
        _acc(dlw_o, _colsum(t1 * yhat), first)
        _acc(dlb_o, _colsum(t1), first)
        dyh = t1 * lw_ref[...]
        dy_o[...] = ro["rstd"] * (dyh - _segsum(dyh, e) * (1.0 / HS) - yhat * (_segsum(dyh * yhat, e) * (1.0 / HS)))
        dq = _segsum(t1 * v, e)
        dv_o[...] = t1 * ro["q"]
        dr_o[...] = dq * kbar * rk
        dkb_o[...] = dq * r * rk
        _acc(drk_o, _colsum(dq * r * kbar), first)
        sg = ro["sg"]
        dsg = _bdot(dgg, gw_ref[...], _NT_DIMS)
        dgd_o[...] = dsg * sg * (1.0 - sg)
        _acc(dgw_o, _bdot(sg, dgg, _TN_DIMS), first)

    tok = lambda i: (i, 0)
    f32s = lambda *s: jax.ShapeDtypeStruct(s, F32)
    return pl.pallas_call(
        kern, grid=(T.NT,), name="readout_bwd",
        in_specs=[_bs((TT, W), tok), T.tm2_spec(), T.tm2_spec(), _bs((TT, RWC), tok),
                  _row(W), _bs((GDW, W), lambda i: (0, 0)), _row(W), _row(W)],
        out_specs=[T.tm_spec(), _bs((TT, W), tok), _bs((TT, W), tok), _bs((TT, W), tok), _bs((TT, GDW), tok),
                   _row(W), _bs((GDW, W), lambda i: (0, 0)), _row(W), _row(W)],
        out_shape=[f32s(T.TTOT, T.BW), f32s(T.NTOK, W), f32s(T.NTOK, W), f32s(T.NTOK, W), f32s(T.NTOK, GDW),
                   f32s(1, W), f32s(GDW, W), f32s(1, W), f32s(1, W)],
        compiler_params=_cp(("arbitrary",)),
    )(dcat, y, kd, rw, rk, gw, lnw, lnb)


def _scan_bwd(T, dy, r, v, kk, dec, kd, bb, hist, fin):
    NP = T.BW // 128
    R = NP * HS
    SB = SCAN_BSUB
    NS = T.TTOT // SB
    NSC = TT // SB

    def tmap(d, g):
        s = NS - 1 - g
        rev = jnp.where(s < NSC, NSC - 1 - s, NS - 1 - (s - NSC))
        return jnp.where(d == 0, s, rev)

    def kern(dy_ref, r_ref, v_ref, kk_ref, dec_ref, kd_ref, bb_ref, h_ref, fin_ref,
             dr_o, dw_o, dk_o, dv_o, da_o, db_o, ds_ref, snext):
        d = pl.program_id(0)
        g = pl.program_id(1)

        @pl.when(g == 0)
        def _():
            ds_ref[...] = jnp.zeros_like(ds_ref)
            snext[...] = fin_ref[0]

        e = _e128(BF16)

        for t in range(SB - 1, -1, -1):
            row = jnp.where(d == 0, t, SB - 1 - t)
            sp = h_ref[0, t]
            st = snext[...] if t == SB - 1 else h_ref[0, t + 1]
            a_ = _expand(-kk_ref[pl.ds(row, 1), :], NP)
            b_ = _expand(bb_ref[0, pl.ds(row, 1), :], NP)
            k_ = _expand(kd_ref[0, pl.ds(row, 1), :], NP)
            sa = _segb(sp * a_, e)
            vc = _colb(v_ref[pl.ds(row, 1), :], NP, e)
            dyc = _colb(dy_ref[pl.ds(row, 1), :], NP, e)
            ds = ds_ref[...] + dyc * _expand(r_ref[pl.ds(row, 1), :], NP)
            dsa = _segb(ds * b_, e)
            ds_ref[...] = ds * _expand(dec_ref[0, pl.ds(row, 1), :], NP) + dsa * a_
            dvb = _segb(ds * k_, e)
            dr_o[0, pl.ds(row, 1), :] = _pair_colsum(st * dyc, NP)
            dw_o[0, pl.ds(row, 1), :] = _pair_colsum(ds * sp, NP)
            db_o[0, pl.ds(row, 1), :] = _pair_colsum(ds * sa, NP)
            dv_o[0, pl.ds(row, 1), :] = _pair_colsum(_diag(dvb, NP), NP)
            dk_o[0, pl.ds(row, 1), :] = _pair_colsum(ds * vc, NP)
            da_o[0, pl.ds(row, 1), :] = _pair_colsum(sp * dsa, NP)
        snext[...] = h_ref[0, 0]

    sh = _bs((SB, T.BW), lambda d, g: (tmap(d, g), 0))
    dr = _bs((1, SB, T.BW), lambda d, g: (d, tmap(d, g), 0))
    o2 = jax.ShapeDtypeStruct((2, T.TTOT, T.BW), F32)
    return pl.pallas_call(
        kern, grid=(2, NS), name="scan_bwd",
        in_specs=[sh, sh, sh, sh, dr, dr, dr, _bs((1, SB, R, 128), lambda d, g: (d, NS - 1 - g, 0, 0)),
                  _bs((1, R, 128), lambda d, g: (d, 0, 0))],
        out_specs=[dr] * 6,
        out_shape=[o2] * 6,
        scratch_shapes=[pltpu.VMEM((R, 128), F32), pltpu.VMEM((R, 128), F32)],
        compiler_params=_cp(("arbitrary", "arbitrary"), mb=48),
    )(dy, r, v, kk, dec, kd, bb, hist, fin)


def _prep_bwd(T, rw, dr_s, ddec, dkd, dv_s, da_s, dbb, dr_ro, dv_ro, dkbar, dgd, w0, w2, a0, a2, k_k, k_a):
    def kern(rw_ref, drs_ref, ddec_ref, dkd_ref, dvs_ref, das_ref, dbb_ref, drr_ref, dvr_ref, dkb_ref, dgd_ref,
             w0_ref, w2_ref, a0_ref, a2_ref, kk_ref, ka_ref,
             drw_o, dw0_o, dw2_o, da0_o, da2_o, dkk_o, dka_o):
        i = pl.program_id(0)
        first = i == 0
        e = _e128(BF16)
        w0v, w2v, a0v, a2v = _load_prep_params(w0_ref, w2_ref, a0_ref, a2_ref)
        k_k = kk_ref[...]
        k_a = ka_ref[...]
        o = _prep_math(rw_ref[...], w0v, w2v, a0v, a2v, k_k, k_a, e)
        k, kk = o["k"], o["kk"]
        dkbh = 0.5 * dkb_ref[...]
        dk = jnp.zeros_like(k)
        dkk = -(das_ref[0] + das_ref[1])
        dka = jnp.zeros((1, W), F32)
        dwd, dad = [], []
        for d in (0, 1):
            iclr = o["iclr"][d]
            dkd_d = dkd_ref[d] + dkbh
            dbb_d = dbb_ref[d]
            dk = dk + dkd_d * (1.0 + (iclr - 1.0) * k_a)
            dka = dka + _colsum(dkd_d * k * (iclr - 1.0))
            dkk = dkk + dbb_d * iclr
            dicl = dkd_d * k * k_a + dbb_d * kk
            dpa = dicl * iclr * (1.0 - iclr)
            _acc(da0_o.at[d:d + 1, :], _colsum(dpa), first)
            dad.append(_bdot(dpa, a2v[d], _NT_DIMS))
            _acc(da2_o.at[d], _bdot(o["ad"][d], dpa, _TN_DIMS), first)
            dpre = -ddec_ref[d] * o["dec"][d] * o["ex"][d] * _sigmoid(-o["pre"][d])
            _acc(dw0_o.at[d:d + 1, :], _colsum(dpre), first)
            th = o["th"][d]
            dth = _bdot(dpre, w2v[d], _NT_DIMS)
            _acc(dw2_o.at[d], _bdot(th, dpre, _TN_DIMS), first)
            dwd.append(dth * (1.0 - th * th))
        inv = o["inv"]
        kr = o["kr"]
        proj = _segsum(dkk * kr, e)
        dkr = dkk * inv - jnp.where(o["rt"] > 1e-12, kr * inv * inv * inv * proj, 0.0)
        dk = dk + dkr * k_k
        _acc(dkk_o, _colsum(dkr * k), first)
        _acc(dka_o, dka, first)
        dr = drs_ref[0] + drs_ref[1] + drr_ref[...]
        dv = dvs_ref[0] + dvs_ref[1] + dvr_ref[...]
        drw_o[...] = jnp.concatenate([dr, dk, dv, dwd[0], dwd[1], dad[0], dad[1], dgd_ref[...]], axis=1)

    tok = lambda i: (i, 0)
    f32s = lambda *s: jax.ShapeDtypeStruct(s, F32)
    p2 = lambda i: (0, 0)
    p3 = lambda i: (0, 0, 0)
    return pl.pallas_call(
        kern, grid=(T.NT,), name="prep_bwd",
        in_specs=[_bs((TT, RWC), tok)] + [T.tm2_spec()] * 6 + [_bs((TT, W), tok)] * 3 + [_bs((TT, GDW), tok)]
        + _prep_param_specs(),
        out_specs=[_bs((TT, RWC), tok), _bs((2, W), p2), _bs((2, LRW, W), p3), _bs((2, W), p2),
                   _bs((2, LRW, W), p3), _row(W), _row(W)],
        out_shape=[f32s(T.NTOK, RWC), f32s(2, W), f32s(2, LRW, W), f32s(2, W), f32s(2, LRW, W), f32s(1, W), f32s(1, W)],
        compiler_params=_cp(("arbitrary",), mb=56),
    )(rw, dr_s, ddec, dkd, dv_s, da_s, dbb, dr_ro, dv_ro, dkbar, dgd, w0, w2, a0, a2, k_k, k_a)


def _shift_bwd(T, drw, p_rw, mu_p, mu_n):
    def kern(d_ref, dp_ref, dn_ref, p_ref, pp_ref, pn_ref, mp_ref, mn_ref, dprw_o, dmp_o, dmn_o):
        i = pl.program_id(0)
        first = i == 0
        has_prev, has_next = _halo_masks(T, i)
        mp = mp_ref[...]
        mn = mn_ref[...]
        drw = d_ref[...]
        z = p_ref[...]
        zprev, znext = _neighbours(z, pp_ref[7:8, :] * has_prev, pn_ref[0:1, :] * has_next)
        _acc(dmp_o, _colsum(drw * (zprev - z)), first)
        _acc(dmn_o, _colsum(drw * (znext - z)), first)
        dprev, dnext = _neighbours(drw, dp_ref[7:8, :] * has_prev, dn_ref[0:1, :] * has_next)
        dprw_o[...] = (drw * (1.0 - mp - mn) + mp * dnext + mn * dprev).astype(BF16)

    tok = lambda i: (i, 0)
    prev, nxt = _halo_specs(T)
    f32s = lambda *s: jax.ShapeDtypeStruct(s, F32)
    return pl.pallas_call(
        kern, grid=(T.NT,), name="shift_bwd",
        in_specs=[_bs((TT, RWC), tok), prev, nxt, _bs((TT, RWC), tok), prev, nxt, _row(RWC), _row(RWC)],
        out_specs=[_bs((TT, RWC), tok), _row(RWC), _row(RWC)],
        out_shape=[jax.ShapeDtypeStruct((T.NTOK, RWC), BF16), f32s(1, RWC), f32s(1, RWC)],
        compiler_params=_cp(("arbitrary",), mb=56),
    )(drw, drw, drw, p_rw, p_rw, p_rw, mu_p, mu_n)


def _mix_in_bwd(T, dp_rw, dp_cv, x2, c2, dx1, modrows, g, w_rw, w_cv):
    def kern(drw_ref, dcv_ref, x_ref, c_ref, dx1_ref, mod_ref, g_ref, wr_ref, wc_ref, dxc_o, dmod_o, dg_o):
        i = pl.program_id(0)
        dh = (lax.dot_general(drw_ref[...], wr_ref[...], _NT_DIMS, preferred_element_type=F32)
              + lax.dot_general(dcv_ref[...], wc_ref[...], _NT_DIMS, preferred_element_type=F32))
        x = _tok_tile(T, x_ref, c_ref)
        s = lax.rsqrt(_rowmean(x * x) + EPS_RMS)
        xh = x * s
        gv = g_ref[...]
        q = i % T.TPS
        first_kind = jnp.logical_or(q == 0, q == 1)
        _acc(dmod_o.at[0, 0, 0:1, :], _colsum(dh), first_kind)
        _acc(dmod_o.at[0, 0, 1:2, :], _colsum(dh * (xh * gv)), first_kind)
        dn1 = dh * (1.0 + mod_ref[0, 0, 1:2, :])
        _acc(dg_o, _colsum(dn1 * xh), i == 0)
        dxh = dn1 * gv
        dxc_o[...] = dx1_ref[...] + s * (dxh - xh * _rowmean(dxh * xh))

    tok = lambda i: (i, 0)
    f32s = lambda *s: jax.ShapeDtypeStruct(s, F32)
    return pl.pallas_call(
        kern, grid=(T.NT,), name="mix_in_bwd",
        in_specs=[_bs((TT, RWC), tok), _bs((TT, CVC), tok)] + _tok_specs(T) + [
            _bs((TT, D), tok), T.mod_spec(), _row(D), _bs((D, RWC), lambda i: (0, 0)), _bs((D, CVC), lambda i: (0, 0))],
        out_specs=[_bs((TT, D), lambda i: (T.lat(i), 0)),
                   _bs((1, 1, 2, D), lambda i: (i // T.TPS, jnp.minimum(i % T.TPS, 1), 0, 0)), _row(D)],
        out_shape=[f32s(T.NLAT, D), f32s(T.B, 2, 2, D), f32s(1, D)],
        compiler_params=_cp(("arbitrary",)),
    )(dp_rw, dp_cv, x2, c2, dx1, modrows, g, w_rw, w_cv)


def _matmul_tn(a, b, name, tk, nk, tn, amap=None, bmap=None, tm=1024):
    M = a.shape[1]
    N = b.shape[1]
    amap = amap or (lambda k: k)
    bmap = bmap or (lambda k: k)

    def kern(a_ref, b_ref, o_ref):
        _acc(o_ref, lax.dot_general(a_ref[...], b_ref[...], _TN_DIMS, preferred_element_type=F32),
             pl.program_id(2) == 0)

    return pl.pallas_call(
        kern, grid=(M // tm, N // tn, nk), name=name,
        in_specs=[_bs((tk, tm), lambda i, j, k: (amap(k), i)), _bs((tk, tn), lambda i, j, k: (bmap(k), j))],
        out_specs=_bs((tm, tn), lambda i, j, k: (i, j)),
        out_shape=jax.ShapeDtypeStruct((M, N), F32),
        compiler_params=_cp(("parallel", "parallel", "arbitrary")),
    )(a, b)


def _silu(x):
    return x * _sigmoid(x)


def _ada_fwd(c_all, c_ctx, ada_w, ada_b_blk):
    nb = c_all.shape[0]
    R = nb + 8
    ncol = ada_w.shape[1]

    def kern(c_ref, cc_ref, w_ref, b_ref, o_ref):
        lhs = jnp.concatenate([_silu(c_ref[...]), _silu(cc_ref[...]), jnp.zeros((7, D), F32)], axis=0)
        o_ref[...] = jnp.dot(lhs, w_ref[...], precision=HI, preferred_element_type=F32) + b_ref[...]

    return pl.pallas_call(
        kern, name="ada_fwd", out_shape=jax.ShapeDtypeStruct((R, ncol), F32),
        compiler_params=_cp(None, 40),
    )(c_all, c_ctx, ada_w, ada_b_blk)


def _ada_bwd(c_all, c_ctx, ada_w, ex, cx, ex_blk, cx_blk):
    nb = c_all.shape[0]
    ncol = ada_w.shape[1]

    def kern(c_ref, cc_ref, w_ref, ex_ref, cx_ref, exb_ref, cxb_ref, gw_o, gb_o, ds_o):
        lhs = jnp.concatenate([_silu(c_ref[...]), _silu(cc_ref[...]), jnp.zeros((7, D), F32)], axis=0)
        dmc_blk = _colsum(cxb_ref[...])
        rhs = jnp.concatenate([exb_ref[...], dmc_blk, jnp.zeros((7, ncol), F32)], axis=0)
        gw_o[...] = lax.dot_general(lhs, rhs, _TN_DIMS, precision=HI, preferred_element_type=F32)
        gb_o[...] = _colsum(ex_ref[...]) + _colsum(cx_ref[...])
        ds_o[...] = lax.dot_general(jnp.concatenate([dmc_blk, jnp.zeros((7, ncol), F32)], axis=0), w_ref[...],
                                    _NT_DIMS, precision=HI, preferred_element_type=F32)

    return pl.pallas_call(
        kern, name="ada_bwd",
        out_shape=[jax.ShapeDtypeStruct((D, ncol), F32), jax.ShapeDtypeStruct((1, ex.shape[1]), F32),
                   jax.ShapeDtypeStruct((8, D), F32)],
        compiler_params=_cp(None, 48),
    )(c_all, c_ctx, ada_w, ex, cx, ex_blk, cx_blk)


def _cctx_final(parts, c_ctx):
    def kern(p_ref, c_ref, o_ref):
        tot = p_ref[0, 0:1, :]
        for j in range(1, parts.shape[0]):
            tot = tot + p_ref[j, 0:1, :]
        c = c_ref[...]
        sg = _sigmoid(c)
        o_ref[...] = tot * (sg * (1.0 + c * (1.0 - sg)))

    return pl.pallas_call(kern, name="cctx_final", out_shape=jax.ShapeDtypeStruct((1, D), F32))(parts, c_ctx)


def _peer(kind, p, ix, iy, ic):
    if kind == "chips":
        return (p // 2, p % 2, ic)
    if kind == "all":
        return (p // 4, (p // 2) % 2, p % 2)
    return (ix, iy, p)


def _exchange(x, kind, bcast, name, chunks=1):
    npeer = {"chips": 4, "all": 8, "sib": 2}[kind]
    slab = x.shape if bcast else x.shape[1:]
    assert chunks == 1 or slab[0] == chunks

    def kern(x_ref, o_ref, send_sems, recv_sems, lsem):
        ix, iy, ic = lax.axis_index("x"), lax.axis_index("y"), lax.axis_index("c")
        me = {"chips": 2 * ix + iy, "all": 4 * ix + 2 * iy + ic, "sib": ic}[kind]
        own = pltpu.make_async_copy(x_ref if bcast else x_ref.at[me], o_ref.at[me], lsem)
        own.start()

        def part(ref, k):
            return ref if chunks == 1 else ref.at[k]

        def copy(p, k):
            return pltpu.make_async_remote_copy(
                src_ref=part(x_ref if bcast else x_ref.at[p], k), dst_ref=part(o_ref.at[me], k),
                send_sem=send_sems.at[p, k], recv_sem=recv_sems.at[me, k],
                device_id=_peer(kind, p, ix, iy, ic), device_id_type=MESH)

        def arrival(p, k):
            return pltpu.make_async_remote_copy(
                src_ref=part(x_ref if bcast else x_ref.at[p], k), dst_ref=part(o_ref.at[p], k),
                send_sem=send_sems.at[p, k], recv_sem=recv_sems.at[p, k],
                device_id=_peer(kind, p, ix, iy, ic), device_id_type=MESH)

        for p in range(npeer):
            @pl.when(me != p)
            def _():
                for k in range(chunks):
                    copy(p, k).start()
        for p in range(npeer):
            @pl.when(me != p)
            def _():
                for k in range(chunks):
                    arrival(p, k).wait_recv()
        for p in range(npeer):
            @pl.when(me != p)
            def _():
                for k in range(chunks):
                    copy(p, k).wait_send()
        own.wait()

    any_spec = pl.BlockSpec(memory_space=pl.ANY)
    return pl.pallas_call(
        kern, name=name, in_specs=[any_spec], out_specs=any_spec,
        out_shape=jax.ShapeDtypeStruct((npeer,) + tuple(slab), x.dtype),
        scratch_shapes=[pltpu.SemaphoreType.DMA((npeer, chunks)), pltpu.SemaphoreType.DMA((npeer, chunks)),
                        pltpu.SemaphoreType.DMA],
    )(x)


def _sum_slots(x, name):
    n, R, C = x.shape
    budget = (8 << 20) // (n * C * x.dtype.itemsize)
    tr = max([t for t in range(8, R + 1, 8) if R % t == 0 and t <= max(budget, 8)], default=R)

    def kern(x_ref, o_ref):
        tot = x_ref[0]
        for s in range(1, n):
            tot = tot + x_ref[s]
        o_ref[...] = tot

    return pl.pallas_call(
        kern, grid=(R // tr,), name=name,
        in_specs=[_bs((n, tr, C), lambda i: (0, i, 0))], out_specs=_bs((tr, C), lambda i: (i, 0)),
        out_shape=jax.ShapeDtypeStruct((R, C), x.dtype), compiler_params=_cp(("parallel",)),
    )(x)


def _sib_stream(x, me, name, add):
    K, R, C = x.shape[-3:]

    def kern(me_ref, *refs):
        if add:
            own_ref, send_ref, o_ref, rbuf, ssem, rsem, credit = refs
        else:
            send_ref, o_ref, rbuf, ssem, rsem, credit = refs
        k = pl.program_id(0)
        slot = k % 2
        sib = (lax.axis_index("x"), lax.axis_index("y"), 1 - lax.axis_index("c"))

        @pl.when(k >= 2)
        def _():
            pl.semaphore_wait(credit.at[slot], 1)

        src = send_ref.at[0, 0] if add else send_ref.at[0]
        cp = pltpu.make_async_remote_copy(src_ref=src, dst_ref=rbuf.at[slot], send_sem=ssem.at[slot],
                                          recv_sem=rsem.at[slot], device_id=sib, device_id_type=MESH)
        cp.start()
        cp.wait_recv()
        o_ref[0] = own_ref[0, 0] + rbuf[slot] if add else rbuf[slot]
        cp.wait_send()

        @pl.when(k + 2 < K)
        def _():
            pl.semaphore_signal(credit.at[slot], 1, device_id=sib, device_id_type=MESH)

    if add:
        in_specs = [_bs((1, 1, R, C), lambda k, me_ref: (me_ref[0], k, 0, 0)),
                    _bs((1, 1, R, C), lambda k, me_ref: (1 - me_ref[0], k, 0, 0))]
        args = (x, x)
    else:
        in_specs = [_bs((1, R, C), lambda k, me_ref: (k, 0, 0))]
        args = (x,)
    return pl.pallas_call(
        kern, name=name,
        grid_spec=pltpu.PrefetchScalarGridSpec(
            num_scalar_prefetch=1, grid=(K,), in_specs=in_specs,
            out_specs=_bs((1, R, C), lambda k, me_ref: (k, 0, 0)),
            scratch_shapes=[pltpu.VMEM((2, R, C), x.dtype), pltpu.SemaphoreType.DMA((2,)),
                            pltpu.SemaphoreType.DMA((2,)), pltpu.SemaphoreType.REGULAR((2,))]),
        out_shape=jax.ShapeDtypeStruct((K, R, C), x.dtype),
        compiler_params=_cp(("arbitrary",)),
    )(me, *args)


def _adamw(w, g, m, v, name):
    shape = w.shape
    if len(shape) == 1:
        outs = _adamw(*(t.reshape(1, -1) for t in (w, g, m, v)), name)
        return tuple(t.reshape(shape) for t in outs)
    nd = len(shape)
    size = 1
    for s in shape:
        size *= s
    rows = shape[-2]
    tr = rows
    if size > (1 << 18) and all(s == 1 for s in shape[:-2]):
        tr = max(t for t in (256, 128, 64, 32, 16, 8) if rows % t == 0)
    c1 = 1.0 - ADAM_B1 ** ADAM_STEP
    c2 = 1.0 - ADAM_B2 ** ADAM_STEP

    def kern(w_ref, g_ref, m_ref, v_ref, d_o, m_o, v_o):
        gv = g_ref[...]
        mn = ADAM_B1 * m_ref[...] + (1.0 - ADAM_B1) * gv
        vn = ADAM_B2 * v_ref[...] + (1.0 - ADAM_B2) * (gv * gv)
        m_o[...] = mn
        v_o[...] = vn
        d_o[...] = -ADAM_LR * ((mn / c1) / (jnp.sqrt(vn / c2) + ADAM_EPS) + ADAM_WD * w_ref[...])

    spec = _bs(shape[:-2] + (tr, shape[-1]), lambda i: (0,) * (nd - 2) + (i, 0))
    o = jax.ShapeDtypeStruct(shape, F32)
    return tuple(pl.pallas_call(
        kern, grid=(rows // tr,), name=name, in_specs=[spec] * 4, out_specs=[spec] * 3, out_shape=[o, o, o],
        compiler_params=_cp(("parallel",)),
    )(w, g, m, v))


_WEIGHT_NAMES = ("c_ctx", "ada_w", "ada_b", "mix_pre_g", "mix_post_g", "mlp_pre_g", "mlp_post_g", "w_in", "mu_prev",
                 "mu_next", "decay_w0", "decay_w2", "iclr_a0", "iclr_a2", "k_k", "k_a", "r_k", "gate_w2", "lnx_w",
                 "lnx_b", "conv_w", "conv_b", "conv_ln_w", "conv_ln_b", "w_out", "mlp_w1", "mlp_w2")


def _pack_rows(parts, cols=512):
    flat = jnp.concatenate([p.reshape(-1) for p in parts])
    rows = -(-flat.shape[0] // cols)
    rows = -(-rows // 16) * 16
    flat = jnp.pad(flat, (0, rows * cols - flat.shape[0]))
    return flat.reshape(rows, cols)


def _pack2d(parts, cols=512):
    return jnp.concatenate([p.reshape(-1, cols) for p in parts], axis=0)


def _unpack2d(buf, shapes):
    out = []
    off = 0
    for s in shapes:
        n = 1
        for d in s:
            n *= d
        n //= buf.shape[1]
        out.append(buf[off:off + n].reshape(s))
        off += n
    return out


def _unpack(flat, shapes):
    out = []
    off = 0
    for s in shapes:
        n = 1
        for d in s:
            n *= d
        out.append(flat[off:off + n].reshape(s))
        off += n
    return out


def _local_step(T, x2, c2, tgt, modrows, P):
    p_rw, p_cv, h = _mix_in(T, x2, c2, modrows, P["mix_pre_g"], P["w_rw"], P["w_cv"])
    prep_params = (P["w0"], P["w2"], P["a0"], P["a2"], P["k_k"], P["k_a"])
    r, v, kk, dec, kd, bb, rw = _rwkv_prep(T, p_rw, P["mu_p"], P["mu_n"], *prep_params)
    y, fin, hist = _scan_fwd(T, r, v, kk, dec, kd, bb)
    ro_params = (P["r_k"], P["gate_w2"], P["lnx_w"], P["lnx_b"])
    cv_params = (P["conv_w"], P["conv_b"], P["conv_ln_w"], P["conv_ln_b"])
    cat, mix, x1, conv = _mix_out(T, y, kd, rw, p_cv, x2, modrows, *ro_params, *cv_params, P["mix_post_g"], P["w_out"])
    m, h2 = _mlp_fwd(T, x1, modrows, P["mlp_pre_g"], P["w1"], P["w2m"])
    loss_acc, dm, dx2, dg2, d_mlp_post = _loss_head(T, m, x1, tgt, modrows, P["mlp_post_g"])
    fact, da, dh2 = _mlp_bwd(T, h2, dm, P["w1"], P["w2m"])
    dx1, dmod2, d_mlp_pre = _mlp_in_bwd(T, dh2, x1, dx2, modrows, P["mlp_pre_g"])
    dmix, dcat, dg1, d_mix_post = _mix_post_bwd(T, dx1, mix, modrows, P["mix_post_g"], P["w_out"])
    dp_cv, d_conv_w, d_conv_b, d_cln_w, d_cln_b = _conv_bwd(T, dcat, p_cv, conv, *cv_params)
    dy, dr_ro, dv_ro, dkbar, dgd, d_r_k, d_gate, d_lnx_w, d_lnx_b = _readout_bwd(T, dcat, y, kd, rw, *ro_params)
    dr_s, ddec, dkd, dv_s, da_s, dbb = _scan_bwd(T, dy, r, v, kk, dec, kd, bb, hist, fin)
    drw, d_w0, d_w2, d_a0, d_a2, d_k_k, d_k_a = _prep_bwd(T, rw, dr_s, ddec, dkd, dv_s, da_s, dbb, dr_ro, dv_ro,
                                                          dkbar, dgd, *prep_params)
    dp_rw, d_mu_p, d_mu_n = _shift_bwd(T, drw, p_rw, P["mu_p"], P["mu_n"])
    dxc, dmod1, d_mix_pre = _mix_in_bwd(T, dp_rw, dp_cv, x2, c2, dx1, modrows, P["mix_pre_g"], P["w_rw"], P["w_cv"])
    kt = max(t for t in (1024, 768, 512, 256) if T.NTOK % t == 0)
    kl = max(t for t in (1024, 512, 256) if T.NLAT % t == 0)
    dw_rw = _matmul_tn(h, dp_rw, "dw_in_rw", kt, T.NTOK // kt, 768)
    dw_cv = _matmul_tn(h, dp_cv, "dw_in_cv", kt, T.NTOK // kt, 1024)
    dw_out = _matmul_tn(cat, dmix, "dw_out", TT, T.NL, 1024, bmap=T.tok)
    dw1 = _matmul_tn(h2, da, "dw_mlp1", kl, T.NLAT // kl, 1024)
    dw2m = _matmul_tn(fact, dm, "dw_mlp2", kl, T.NLAT // kl, 1024)
    small = dict(mix_pre_g=d_mix_pre, mix_post_g=d_mix_post, mlp_pre_g=d_mlp_pre, mlp_post_g=d_mlp_post,
                 mu_p=d_mu_p, mu_n=d_mu_n, w0=d_w0, w2=d_w2, a0=d_a0, a2=d_a2, k_k=d_k_k, k_a=d_k_a, r_k=d_r_k,
                 gate_w2=d_gate, lnx_w=d_lnx_w, lnx_b=d_lnx_b, conv_w=d_conv_w, conv_b=d_conv_b,
                 conv_ln_w=d_cln_w, conv_ln_b=d_cln_b)
    big = dict(w_rw=dw_rw, w_cv=dw_cv, w_out=dw_out, w1=dw1, w2m=dw2m)
    dmods = dict(dmod1=dmod1, dg1=dg1, dmod2=dmod2, dg2=dg2)
    return loss_acc[0, 0], dxc, small, big, dmods


_SMALL_ORDER = ("mix_pre_g", "mix_post_g", "mlp_pre_g", "mlp_post_g", "mu_p", "mu_n", "w0", "w2", "a0", "a2", "k_k",
                "k_a", "r_k", "gate_w2", "lnx_w", "lnx_b", "conv_w", "conv_b", "conv_ln_w", "conv_ln_b")


def kernel(x, c, ctx, c_ctx, ada_w, ada_b, mix_pre_g, mix_post_g, mlp_pre_g, mlp_post_g, w_in, mu_prev, mu_next, decay_w0, decay_w2, iclr_a0, iclr_a2, k_k, k_a, r_k, gate_w2, lnx_w, lnx_b, conv_w, conv_b, conv_ln_w, conv_ln_b, w_out, mlp_w1, mlp_w2, loss_target, m_c_ctx, m_ada_w, m_ada_b, m_mix_pre_g, m_mix_post_g, m_mlp_pre_g, m_mlp_post_g, m_w_in, m_mu_prev, m_mu_next, m_decay_w0, m_decay_w2, m_iclr_a0, m_iclr_a2, m_k_k, m_k_a, m_r_k, m_gate_w2, m_lnx_w, m_lnx_b, m_conv_w, m_conv_b, m_conv_ln_w, m_conv_ln_b, m_w_out, m_mlp_w1, m_mlp_w2, v_c_ctx, v_ada_w, v_ada_b, v_mix_pre_g, v_mix_post_g, v_mlp_pre_g, v_mlp_post_g, v_w_in, v_mu_prev, v_mu_next, v_decay_w0, v_decay_w2, v_iclr_a0, v_iclr_a2, v_k_k, v_k_a, v_r_k, v_gate_w2, v_lnx_w, v_lnx_b, v_conv_w, v_conv_b, v_conv_ln_w, v_conv_ln_b, v_w_out, v_mlp_w1, v_mlp_w2):
    weights = dict(zip(_WEIGHT_NAMES, (c_ctx, ada_w, ada_b, mix_pre_g, mix_post_g, mlp_pre_g, mlp_post_g, w_in, mu_prev, mu_next, decay_w0, decay_w2, iclr_a0, iclr_a2, k_k, k_a, r_k, gate_w2, lnx_w, lnx_b, conv_w, conv_b, conv_ln_w, conv_ln_b, w_out, mlp_w1, mlp_w2)))
    moms = dict(zip(_WEIGHT_NAMES, (m_c_ctx, m_ada_w, m_ada_b, m_mix_pre_g, m_mix_post_g, m_mlp_pre_g, m_mlp_post_g, m_w_in, m_mu_prev, m_mu_next, m_decay_w0, m_decay_w2, m_iclr_a0, m_iclr_a2, m_k_k, m_k_a, m_r_k, m_gate_w2, m_lnx_w, m_lnx_b, m_conv_w, m_conv_b, m_conv_ln_w, m_conv_ln_b, m_w_out, m_mlp_w1, m_mlp_w2)))
    vars_ = dict(zip(_WEIGHT_NAMES, (v_c_ctx, v_ada_w, v_ada_b, v_mix_pre_g, v_mix_post_g, v_mlp_pre_g, v_mlp_post_g, v_w_in, v_mu_prev, v_mu_next, v_decay_w0, v_decay_w2, v_iclr_a0, v_iclr_a2, v_k_k, v_k_a, v_r_k, v_gate_w2, v_lnx_w, v_lnx_b, v_conv_w, v_conv_b, v_conv_ln_w, v_conv_ln_b, v_w_out, v_mlp_w1, v_mlp_w2)))

    B, t_lat, _ = x.shape
    assert ctx.shape[1] == TT and t_lat % TT == 0 and (t_lat * B) % MT == 0
    T = _Tiles(B, t_lat)
    ix, iy, ic = lax.axis_index("x"), lax.axis_index("y"), lax.axis_index("c")
    chip = 2 * ix + iy
    dev = 4 * ix + 2 * iy + ic
    nsh = 4
    in_sh = w_in.shape[2]
    ada_sh = ada_w.shape[2]
    lane_sh = decay_w0.shape[2]

    big_parts = (w_in[0], w_out[0], mlp_w1[0], mlp_w2[0])
    big_shapes = [p.shape for p in big_parts]
    wg = _exchange(_pack2d([p.astype(BF16) for p in big_parts]), "chips", True, "gather_big_weights")
    per = [_unpack2d(wg[j], big_shapes) for j in range(nsh)]
    w_in_f = jnp.concatenate([per[j][0] for j in range(nsh)], axis=1)
    w_out_f = jnp.concatenate([per[j][1] for j in range(nsh)], axis=0)
    w1_f = jnp.concatenate([per[j][2] for j in range(nsh)], axis=1)
    w2_f = jnp.concatenate([per[j][3] for j in range(nsh)], axis=0)
    w_in_p = _pad_cols(w_in_f, w_in_f.shape[1])

    sm_parts = (decay_w0[0], decay_w2[0], iclr_a0[0], iclr_a2[0], gate_w2[0], conv_w[0])
    sm_shapes = [p.shape for p in sm_parts]
    sg = _exchange(_pack_rows(sm_parts), "chips", True, "gather_small_weights")
    pers = [_unpack(sg[j].reshape(-1), sm_shapes) for j in range(nsh)]
    w0_f, w2_f_, a0_f, a2_f, gate_f, convw_f = (jnp.concatenate([pers[j][t] for j in range(nsh)], axis=-1)
                                                for t in range(6))

    def pad_rows(a, n):
        return jnp.pad(a, [(0, 0)] * (a.ndim - 2) + [(0, n - a.shape[-2]), (0, 0)])

    P = dict(
        w_rw=w_in_p[:, :RWC], w_cv=w_in_p[:, RWC:], w_out=w_out_f, w1=w1_f, w2m=w2_f,
        mix_pre_g=mix_pre_g, mix_post_g=mix_post_g, mlp_pre_g=mlp_pre_g, mlp_post_g=mlp_post_g,
        mu_p=_pad_cols(mu_prev, mu_prev.shape[1]), mu_n=_pad_cols(mu_next, mu_next.shape[1]),
        w0=w0_f, w2=pad_rows(w2_f_, LRW), a0=a0_f, a2=pad_rows(a2_f, LRW), k_k=k_k, k_a=k_a,
        r_k=r_k.reshape(1, W), gate_w2=pad_rows(gate_f, GDW), lnx_w=lnx_w, lnx_b=lnx_b,
        conv_w=pad_rows(convw_f, 32), conv_b=conv_b, conv_ln_w=conv_ln_w, conv_ln_b=conv_ln_b)

    c_ctx2 = c_ctx.reshape(1, D)
    c_all = _exchange(jnp.pad(c, ((0, 8 - B), (0, 0))), "all", True, "gather_c")[:, :B].reshape(8 * B, D)
    ada_b_blk = lax.dynamic_slice(ada_b, (0, chip * ada_sh), (1, ada_sh))
    mod_blk = _ada_fwd(c_all, c_ctx2, ada_w[0], ada_b_blk)
    mod_g = _exchange(mod_blk, "chips", True, "gather_mod")
    mod_all = jnp.concatenate([mod_g[j] for j in range(nsh)], axis=1)
    mod_x = lax.dynamic_slice(mod_all, (dev * B, 0), (B, 6 * D)).reshape(B, 6, D)
    mod_c = jnp.broadcast_to(mod_all[8 * B].reshape(1, 6, D), (B, 6, D))
    modrows = jnp.stack([mod_c, mod_x], axis=1)

    x2 = x.reshape(T.NLAT, D)
    c2 = ctx.reshape(B * TT, D)
    tgt = loss_target.reshape(T.NLAT, D)
    loss_loc, dxl, small, big, dm_ = _local_step(T, x2, c2, tgt, modrows, P)
    loss = lax.psum(loss_loc, ("x", "y", "c"))
    grad_x = dxl.reshape(x.shape)

    dmod_x = jnp.concatenate([dm_["dmod1"][:, 1], dm_["dg1"], dm_["dmod2"], dm_["dg2"]], axis=1)
    dmod_c = jnp.concatenate([dm_["dmod1"][:, 0], jnp.zeros((B, 4, D), F32)], axis=1)
    dpack = jnp.concatenate([dmod_x.reshape(B, 6 * D), dmod_c.reshape(B, 6 * D)], axis=0)
    dg = _exchange(dpack, "all", True, "gather_dmod")
    ex = dg[:, :B].reshape(8 * B, 6 * D)
    cx = dg[:, B:].reshape(8 * B, 6 * D)
    ex_blk = lax.dynamic_slice(ex, (0, chip * ada_sh), (8 * B, ada_sh))
    cx_blk = lax.dynamic_slice(cx, (0, chip * ada_sh), (8 * B, ada_sh))
    g_ada_w, g_ada_b, dscc = _ada_bwd(c_all, c_ctx2, ada_w[0], ex, cx, ex_blk, cx_blk)
    dscc_g = _exchange(dscc, "chips", True, "gather_dcctx")
    g_c_ctx = _cctx_final(dscc_g, c_ctx2).reshape(D)

    small = dict(small, mu_p=_unpad_cols(small["mu_p"], mu_prev.shape[1]),
                 mu_n=_unpad_cols(small["mu_n"], mu_next.shape[1]),
                 w2=small["w2"][:, :decay_w2.shape[2]], a2=small["a2"][:, :iclr_a2.shape[2]],
                 gate_w2=small["gate_w2"][:gate_w2.shape[1]], conv_w=small["conv_w"][:KCONV])
    sm_list = [small[n] for n in _SMALL_ORDER]
    sm_shapes2 = [a.shape for a in sm_list]
    me1 = ic.reshape(1).astype(jnp.int32)
    sm_pack = _pack_rows(sm_list)
    sm_pair = _sib_stream(jnp.stack([sm_pack, sm_pack])[:, None], me1, "sib_small_grads", True)[0]
    sm_tot = _sum_slots(_exchange(sm_pair, "chips", True, "gather_small_grads"), "sum_small_grads")
    S = dict(zip(_SMALL_ORDER, _unpack(sm_tot.reshape(-1), sm_shapes2)))

    def shard_last(a):
        return lax.dynamic_slice_in_dim(a, chip * lane_sh, lane_sh, axis=a.ndim - 1)

    grads = dict(
        c_ctx=g_c_ctx, ada_w=g_ada_w[None], ada_b=g_ada_b,
        mix_pre_g=S["mix_pre_g"], mix_post_g=S["mix_post_g"], mlp_pre_g=S["mlp_pre_g"], mlp_post_g=S["mlp_post_g"],
        mu_prev=S["mu_p"], mu_next=S["mu_n"],
        decay_w0=shard_last(S["w0"])[None], decay_w2=shard_last(S["w2"])[None],
        iclr_a0=shard_last(S["a0"])[None], iclr_a2=shard_last(S["a2"])[None],
        k_k=S["k_k"], k_a=S["k_a"], r_k=S["r_k"].reshape(r_k.shape),
        gate_w2=shard_last(S["gate_w2"])[None], lnx_w=S["lnx_w"], lnx_b=S["lnx_b"],
        conv_w=shard_last(S["conv_w"])[None], conv_b=S["conv_b"], conv_ln_w=S["conv_ln_w"],
        conv_ln_b=S["conv_ln_b"])

    dw_in_f = _unpad_cols(jnp.concatenate([big["w_rw"], big["w_cv"]], axis=1), w_in_f.shape[1])
    oshape = w_out.shape[1]
    mshape = mlp_w1.shape[2]
    slabs = [
        _pack2d([dw_in_f[:, in_sh * j:in_sh * (j + 1)], big["w_out"][oshape * j:oshape * (j + 1)],
                 big["w1"][:, mshape * j:mshape * (j + 1)], big["w2m"][mshape * j:mshape * (j + 1)]])
        for j in range(nsh)]
    nck = 3
    lr, lc = slabs[0].shape
    hr = lr // 2
    cr = hr // nck
    assert cr * 2 * nck == lr and cr % 8 == 0
    halves = jnp.stack([jnp.concatenate([s[h * hr:(h + 1) * hr].reshape(nck, cr, lc) for s in slabs], axis=0)
                        for h in (0, 1)])
    pair = _sib_stream(halves, me1, "sib_reduce_grads", True)
    mine = _sum_slots(_exchange(pair.reshape(nsh, nck * cr, lc), "chips", False, "reduce_big_grads"), "sum_big_grads")
    mine = mine.reshape(nck, cr, lc)
    other = _sib_stream(mine, me1, "sib_swap_grads", False)
    tot = jnp.where(ic == 0, jnp.concatenate([mine, other], axis=0), jnp.concatenate([other, mine], axis=0))
    g_w_in, g_w_out, g_w1, g_w2 = _unpack2d(tot.reshape(lr, lc), big_shapes)
    grads.update(w_in=g_w_in[None], w_out=g_w_out[None], mlp_w1=g_w1[None], mlp_w2=g_w2[None])

    deltas, new_m, new_v = {}, {}, {}
    for n in _WEIGHT_NAMES:
        g = grads[n].reshape(weights[n].shape)
        grads[n] = g
        deltas[n], new_m[n], new_v[n] = _adamw(weights[n], g, moms[n], vars_[n], "adamw_" + n)

    return (loss, grad_x, *[grads[n] for n in _WEIGHT_NAMES], *[deltas[n] for n in _WEIGHT_NAMES],
            *[new_m[n] for n in _WEIGHT_NAMES], *[new_v[n] for n in _WEIGHT_NAMES])
```

```python
import functools

import jax
import jax.numpy as jnp
from jax import lax
from jax.experimental import pallas as pl
from jax.experimental.pallas import tpu as pltpu

F32 = jnp.float32
BF16 = jnp.bfloat16
HI = lax.Precision.HIGHEST

D = 1024
W = 512
HS = 64
RWC = 2304
CVC = 1024
GDW = 256
LRW = 128
DFF = 4096
TT = 256
LINE = 64
KCONV = 31
EPS_RMS = 1e-6
EPS_LN = 1e-5
EPS_GN = 64e-5
SCAN_CH = 128
SCAN_G = 8
SCAN_BSUB = 16

ADAM_LR = 0.001
ADAM_B1 = 0.9
ADAM_B2 = 0.999
ADAM_EPS = 1e-08
ADAM_WD = 0.01
ADAM_STEP = 10

_SEGS = ((0, 1536, 1536), (1536, 64, 128), (1600, 64, 128), (1664, 64, 128), (1728, 64, 128),
         (1792, 160, 256), (1952, 1024, 1024))

MESH = pl.DeviceIdType.MESH


def _bs(shape, imap):
    return pl.BlockSpec(shape, imap)


def _cp(sem=None, mb=48):
    return pltpu.CompilerParams(dimension_semantics=sem, vmem_limit_bytes=mb << 20)


def _pad_cols(a, ncols):
    out = []
    for s, w, pw in _SEGS:
        if s >= ncols:
            break
        piece = a[..., s:s + w]
        if pw > w:
            piece = jnp.pad(piece, [(0, 0)] * (a.ndim - 1) + [(0, pw - w)])
        out.append(piece)
    return jnp.concatenate(out, axis=-1)


def _unpad_cols(a, ncols):
    out = []
    off = 0
    for s, w, pw in _SEGS:
        if s >= ncols:
            break
        out.append(a[..., off:off + w])
        off += pw
    return jnp.concatenate(out, axis=-1)


def _sigmoid(x):
    return 1.0 / (1.0 + jnp.exp(-x))


def _softplus(x):
    return jnp.maximum(x, 0.0) + jnp.log(1.0 + jnp.exp(-jnp.abs(x)))


def _e128(dtype):
    r = lax.broadcasted_iota(jnp.int32, (128, 128), 0) >= HS
    c = lax.broadcasted_iota(jnp.int32, (128, 128), 1) >= HS
    return (r == c).astype(dtype)


def _segsum(x, e):
    hi = x.astype(BF16)
    lo = (x - hi.astype(F32)).astype(BF16)
    return jnp.concatenate(
        [jnp.dot(hi[:, 128 * g:128 * (g + 1)], e, preferred_element_type=F32)
         + jnp.dot(lo[:, 128 * g:128 * (g + 1)], e, preferred_element_type=F32) for g in range(4)], axis=1)


_NT_DIMS = (((1,), (1,)), ((), ()))
_TN_DIMS = (((0,), (0,)), ((), ()))


def _bdot(a, b, dims=None):
    a = a.astype(BF16)
    b = b.astype(BF16)
    if dims is None:
        return jnp.dot(a, b, preferred_element_type=F32)
    return lax.dot_general(a, b, dims, preferred_element_type=F32)


def _colsum(x):
    return jnp.sum(x, axis=0, keepdims=True)


def _rowmean(x):
    return jnp.mean(x, axis=-1, keepdims=True)


def _diag(x, npairs):
    row = lax.broadcasted_iota(jnp.int32, (HS, 128), 0)
    lane = lax.broadcasted_iota(jnp.int32, (HS, 128), 1) & (HS - 1)
    keep = jnp.broadcast_to((lane == row)[None], (npairs, HS, 128))
    return jnp.where(keep, x.reshape(npairs, HS, 128), 0.0).reshape(npairs * HS, 128)


def _segb(x, e):
    return jnp.dot(x.astype(BF16), e, preferred_element_type=F32)


_segb1 = _segb


def _expand(row, npairs):
    return jnp.concatenate([jnp.broadcast_to(row[:, 128 * j:128 * (j + 1)], (HS, 128)) for j in range(npairs)], axis=0)


def _colb(row, npairs, e):
    return _segb1(_diag(_expand(row, npairs), npairs), e)


def _pair_colsum(x, npairs):
    return jnp.concatenate([_colsum(x[HS * j:HS * (j + 1)]) for j in range(npairs)], axis=1)


def _conv_pos():
    return lax.broadcasted_iota(jnp.int32, (TT, W), 0) & (LINE - 1)


def _shifted(u, s, pos):
    if s == 0:
        return u
    sh = pltpu.roll(u, (-s) % TT, 0)
    valid = jnp.logical_and(pos + s >= 0, pos + s < LINE)
    return jnp.where(valid, sh, 0.0)


def _acc(ref, val, first):
    @pl.when(first)
    def _():
        ref[...] = jnp.zeros(ref.shape, ref.dtype)
    ref[...] += val


class _Tiles:
    def __init__(self, B, t_lat):
        self.B = B
        self.NLT = t_lat // TT
        self.TPS = self.NLT + 1
        self.NT = B * self.TPS
        self.NL = B * self.NLT
        self.NTOK = self.NT * TT
        self.NLAT = self.NL * TT
        self.TTOT = self.TPS * TT
        self.BW = B * W

    def b(self, i):
        return i // self.TPS

    def q(self, i):
        return i % self.TPS

    def lat(self, i):
        return (i // self.TPS) * self.NLT + jnp.maximum(i % self.TPS - 1, 0)

    def tok(self, l):
        return (l // self.NLT) * self.TPS + 1 + l % self.NLT

    def mod_spec(self):
        return _bs((1, 1, 6, D), lambda i: (i // self.TPS, jnp.minimum(i % self.TPS, 1), 0, 0))

    def tm_spec(self):
        return _bs((TT, W), lambda i: (i % self.TPS, i // self.TPS))

    def tm2_spec(self):
        return _bs((2, TT, W), lambda i: (0, i % self.TPS, i // self.TPS))


def _row(shape_last):
    return _bs((1, shape_last), lambda i: (0, 0))


def _tok_specs(T):
    return [_bs((TT, D), lambda i: (T.lat(i), 0)), _bs((TT, D), lambda i: (i // T.TPS, 0))]


def _tok_tile(T, x_ref, c_ref):
    is_ctx = (pl.program_id(0) % T.TPS == 0).astype(F32)
    return c_ref[...] * is_ctx + x_ref[...] * (1.0 - is_ctx)


def _mix_in(T, x2, c2, modrows, g, w_rw, w_cv):
    def kern(x_ref, c_ref, mod_ref, g_ref, wr_ref, wc_ref, prw_ref, pcv_ref, h_ref):
        x = _tok_tile(T, x_ref, c_ref)
        s = lax.rsqrt(_rowmean(x * x) + EPS_RMS)
        h = (x * s * g_ref[...]) * (1.0 + mod_ref[0, 0, 1:2, :]) + mod_ref[0, 0, 0:1, :]
        hb = h.astype(BF16)
        h_ref[...] = hb
        prw_ref[...] = jnp.dot(hb, wr_ref[...], preferred_element_type=F32)
        pcv_ref[...] = jnp.dot(hb, wc_ref[...], preferred_element_type=F32)

    return pl.pallas_call(
        kern, grid=(T.NT,), name="mix_in",
        in_specs=_tok_specs(T) + [T.mod_spec(), _row(D),
                                  _bs((D, RWC), lambda i: (0, 0)), _bs((D, CVC), lambda i: (0, 0))],
        out_specs=[_bs((TT, RWC), lambda i: (i, 0)), _bs((TT, CVC), lambda i: (i, 0)), _bs((TT, D), lambda i: (i, 0))],
        out_shape=[jax.ShapeDtypeStruct((T.NTOK, RWC), F32), jax.ShapeDtypeStruct((T.NTOK, CVC), F32),
                   jax.ShapeDtypeStruct((T.NTOK, D), BF16)],
        compiler_params=_cp(("parallel",)),
    )(x2, c2, modrows, g, w_rw, w_cv)


def _halo_specs(T):
    nb8 = T.NTOK // 8
    prev = _bs((8, RWC), lambda i: (jnp.maximum(i * (TT // 8) - 1, 0), 0))
    nxt = _bs((8, RWC), lambda i: (jnp.minimum((i + 1) * (TT // 8), nb8 - 1), 0))
    return prev, nxt


def _halo_masks(T, i):
    q = i % T.TPS
    has_prev = jnp.logical_and(q != 0, q != 1).astype(F32)
    has_next = jnp.logical_and(q != 0, q != T.TPS - 1).astype(F32)
    return has_prev, has_next


def _neighbours(z, prev_row, next_row):
    rowi = lax.broadcasted_iota(jnp.int32, z.shape, 0)
    zprev = jnp.where(rowi == 0, prev_row, pltpu.roll(z, 1, 0))
    znext = jnp.where(rowi == TT - 1, next_row, pltpu.roll(z, TT - 1, 0))
    return zprev, znext


def _prep_math(rw, w0, w2, a0, a2, k_k, k_a, e):
    r = rw[:, 0:512]
    k = rw[:, 512:1024]
    v = rw[:, 1024:1536]
    kr = k * k_k
    ss = _segsum(kr * kr, e)
    rt = jnp.sqrt(ss)
    inv = 1.0 / jnp.maximum(rt, 1e-12)
    kk = kr * inv
    o = dict(r=r, k=k, v=v, kr=kr, rt=rt, inv=inv, kk=kk, th=[], pre=[], ex=[], dec=[], iclr=[], kd=[], bb=[], ad=[])
    for d in (0, 1):
        wd = rw[:, 1536 + LRW * d:1536 + LRW * (d + 1)]
        ad = rw[:, 1792 + LRW * d:1792 + LRW * (d + 1)]
        th = jnp.tanh(wd)
        pre = w0[d] + _bdot(th, w2[d])
        ex = jnp.exp(-_softplus(-pre) - 0.5)
        dec = jnp.exp(-ex)
        iclr = _sigmoid(a0[d] + _bdot(ad, a2[d]))
        o["th"].append(th)
        o["pre"].append(pre)
        o["ex"].append(ex)
        o["dec"].append(dec)
        o["iclr"].append(iclr)
        o["ad"].append(ad)
        o["kd"].append(k * (1.0 + (iclr - 1.0) * k_a))
        o["bb"].append(kk * iclr)
    return o


def _load_prep_params(w0_ref, w2_ref, a0_ref, a2_ref):
    w0 = [w0_ref[0:1, :], w0_ref[1:2, :]]
    a0 = [a0_ref[0:1, :], a0_ref[1:2, :]]
    w2 = [w2_ref[0], w2_ref[1]]
    a2 = [a2_ref[0], a2_ref[1]]
    return w0, w2, a0, a2


def _prep_param_specs():
    return [_bs((2, W), lambda i: (0, 0)), _bs((2, LRW, W), lambda i: (0, 0, 0)),
            _bs((2, W), lambda i: (0, 0)), _bs((2, LRW, W), lambda i: (0, 0, 0)), _row(W), _row(W)]


def _rwkv_prep(T, p_rw, mu_p, mu_n, w0, w2, a0, a2, k_k, k_a):
    def kern(p_ref, pp_ref, pn_ref, mp_ref, mn_ref, w0_ref, w2_ref, a0_ref, a2_ref, kk_ref, ka_ref,
             r_o, v_o, kk_o, dec_o, kd_o, bb_o, rw_o):
        i = pl.program_id(0)
        has_prev, has_next = _halo_masks(T, i)
        z = p_ref[...]
        zprev, znext = _neighbours(z, pp_ref[7:8, :] * has_prev, pn_ref[0:1, :] * has_next)
        rw = z + mp_ref[...] * (zprev - z) + mn_ref[...] * (znext - z)
        rw_o[...] = rw
        w0v, w2v, a0v, a2v = _load_prep_params(w0_ref, w2_ref, a0_ref, a2_ref)
        o = _prep_math(rw, w0v, w2v, a0v, a2v, kk_ref[...], ka_ref[...], _e128(BF16))
        r_o[...] = o["r"]
        v_o[...] = o["v"]
        kk_o[...] = o["kk"]
        for d in (0, 1):
            dec_o[d] = o["dec"][d]
            kd_o[d] = o["kd"][d]
            bb_o[d] = o["bb"][d]

    prev, nxt = _halo_specs(T)
    tm = jax.ShapeDtypeStruct((T.TTOT, T.BW), F32)
    tm2 = jax.ShapeDtypeStruct((2, T.TTOT, T.BW), F32)
    return pl.pallas_call(
        kern, grid=(T.NT,), name="rwkv_prep",
        in_specs=[_bs((TT, RWC), lambda i: (i, 0)), prev, nxt, _row(RWC), _row(RWC)] + _prep_param_specs(),
        out_specs=[T.tm_spec(), T.tm_spec(), T.tm_spec(), T.tm2_spec(), T.tm2_spec(), T.tm2_spec(),
                   _bs((TT, RWC), lambda i: (i, 0))],
        out_shape=[tm, tm, tm, tm2, tm2, tm2, jax.ShapeDtypeStruct((T.NTOK, RWC), F32)],
        compiler_params=_cp(("parallel",)),
    )(p_rw, p_rw, p_rw, mu_p, mu_n, w0, w2, a0, a2, k_k, k_a)


def _scan_fwd(T, r, v, kk, dec, kd, bb):
    NP = T.BW // 128
    R = NP * HS
    NCH = T.TTOT // SCAN_CH
    NCC = TT // SCAN_CH
    G = SCAN_G
    NG = SCAN_CH // G
    NGRP = 4
    assert NG % NGRP == 0

    def tmap(d, i):
        rev = jnp.where(i < NCC, NCC - 1 - i, NCH - 1 - (i - NCC))
        return jnp.where(d == 0, i, rev)

    def kern(r_ref, v_ref, kk_ref, dec_ref, kd_ref, bb_ref, y_ref, fin_ref, hist_ref, ring, sems):
        d = pl.program_id(0)
        i = pl.program_id(1)

        @pl.when(i == 0)
        def _():
            ring[0] = jnp.zeros((R, 128), F32)

        e = _e128(BF16)

        def hist_copy(k):
            grp = k % NGRP
            return pltpu.make_async_copy(ring.at[pl.ds(grp * G, G)],
                                         hist_ref.at[d, pl.ds(i * SCAN_CH + k * G, G)], sems.at[grp])

        def body(k, carry):
            @pl.when(k >= NGRP - 1)
            def _():
                hist_copy(k - (NGRP - 1)).wait()

            base = (k % NGRP) * G
            for u in range(G):
                t = k * G + u
                row = jnp.where(d == 0, t, SCAN_CH - 1 - t)
                s = ring[base + u]
                sa = _segb(s * _expand(-kk_ref[pl.ds(row, 1), :], NP), e)
                vc = _colb(v_ref[pl.ds(row, 1), :], NP, e)
                s = (s * _expand(dec_ref[0, pl.ds(row, 1), :], NP) + sa * _expand(bb_ref[0, pl.ds(row, 1), :], NP)
                     + vc * _expand(kd_ref[0, pl.ds(row, 1), :], NP))
                ring[(base + u + 1) if u < G - 1 else ((k + 1) % NGRP) * G] = s
                yb = _segb(s * _expand(r_ref[pl.ds(row, 1), :], NP), e)
                y_ref[0, pl.ds(row, 1), :] = _pair_colsum(_diag(yb, NP), NP)
            hist_copy(k).start()
            return carry

        lax.fori_loop(0, NG, body, 0)
        for k in range(NG - (NGRP - 1), NG):
            hist_copy(k).wait()

        @pl.when(i == NCH - 1)
        def _():
            fin_ref[0] = ring[0]

    sh = _bs((SCAN_CH, T.BW), lambda d, i: (tmap(d, i), 0))
    dr = _bs((1, SCAN_CH, T.BW), lambda d, i: (d, tmap(d, i), 0))
    return pl.pallas_call(
        kern, grid=(2, NCH), name="scan_fwd",
        in_specs=[sh, sh, sh, dr, dr, dr],
        out_specs=[dr, _bs((1, R, 128), lambda d, i: (d, 0, 0)), pl.BlockSpec(memory_space=pl.ANY)],
        out_shape=[jax.ShapeDtypeStruct((2, T.TTOT, T.BW), F32), jax.ShapeDtypeStruct((2, R, 128), F32),
                   jax.ShapeDtypeStruct((2, T.TTOT, R, 128), F32)],
        scratch_shapes=[pltpu.VMEM((NGRP * G, R, 128), F32), pltpu.SemaphoreType.DMA((NGRP,))],
        compiler_params=_cp(("arbitrary", "arbitrary")),
    )(r, v, kk, dec, kd, bb)


def _readout_fwd(y, r, v, gd, kbar, rk, gw, lw, lb, e):
    mu = _segsum(y, e) * (1.0 / HS)
    yc = y - mu
    var = _segsum(yc * yc, e) * (1.0 / HS)
    rstd = lax.rsqrt(var + EPS_GN)
    yhat = yc * rstd
    yn = yhat * lw + lb
    q = _segsum(r * kbar * rk, e)
    sg = _sigmoid(gd)
    gg = _bdot(sg, gw)
    return dict(yhat=yhat, rstd=rstd, yn=yn, q=q, sg=sg, gg=gg, out=(yn + q * v) * gg)


def _conv_fwd(cva, cvb, cw_ref, cb, lw, lb, c=None):
    pos = _conv_pos()
    sgb = _sigmoid(cvb)
    u = cva * sgb
    if c is None:
        c = jnp.zeros_like(u)
        for j in range(KCONV):
            c = c + cw_ref[j:j + 1, :] * _shifted(u, j - KCONV // 2, pos)
        c = c + cb
    mu = _rowmean(c)
    cc = c - mu
    rstd = lax.rsqrt(_rowmean(cc * cc) + EPS_LN)
    chat = cc * rstd
    cn = chat * lw + lb
    scn = _sigmoid(cn)
    return dict(sgb=sgb, u=u, c=c, chat=chat, rstd=rstd, cn=cn, scn=scn, out=cn * scn, pos=pos)


def _mix_out(T, y, kd, rw, p_cv, x2, modrows, rk, gw, lnw, lnb, cw, cb, clw, clb, pg, w_out):
    tk = T.tok

    def kern(y_ref, kd_ref, rw_ref, pcv_ref, x_ref, mod_ref, rk_ref, gw_ref, lw_ref, lb_ref, cw_ref, cb_ref,
             clw_ref, clb_ref, pg_ref, wo_ref, cat_o, mix_o, x1_o, conv_o):
        e = _e128(BF16)
        ro = _readout_fwd(y_ref[0] + y_ref[1], rw_ref[:, 0:512], rw_ref[:, 1024:1536], rw_ref[:, 2048:2304],
                          0.5 * (kd_ref[0] + kd_ref[1]), rk_ref[...], gw_ref[...], lw_ref[...], lb_ref[...], e)
        cv = _conv_fwd(pcv_ref[:, 0:512], pcv_ref[:, 512:1024], cw_ref, cb_ref[...], clw_ref[...], clb_ref[...])
        conv_o[...] = cv["c"]
        catb = jnp.concatenate([ro["out"], cv["out"]], axis=1).astype(BF16)
        cat_o[...] = catb
        mix = jnp.dot(catb, wo_ref[...], preferred_element_type=F32)
        mix_o[...] = mix
        sm = lax.rsqrt(_rowmean(mix * mix) + EPS_RMS)
        x1_o[...] = x_ref[...] + mod_ref[0, 0, 2:3, :] * (mix * sm * pg_ref[...])

    lat = lambda l: (l, 0)
    return pl.pallas_call(
        kern, grid=(T.NL,), name="mix_out",
        in_specs=[_bs((2, TT, W), lambda l: (0, 1 + l % T.NLT, l // T.NLT)),
                  _bs((2, TT, W), lambda l: (0, 1 + l % T.NLT, l // T.NLT)),
                  _bs((TT, RWC), lambda l: (tk(l), 0)), _bs((TT, CVC), lambda l: (tk(l), 0)),
                  _bs((TT, D), lambda l: (l, 0)),
                  _bs((1, 1, 6, D), lambda l: (l // T.NLT, 1, 0, 0)),
                  _row(W), _bs((GDW, W), lambda l: (0, 0)), _row(W), _row(W),
                  _bs((32, W), lambda l: (0, 0)), _row(W), _row(W), _row(W), _row(D),
                  _bs((D, D), lambda l: (0, 0))],
        out_specs=[_bs((TT, D), lat), _bs((TT, D), lat), _bs((TT, D), lat), _bs((TT, W), lat)],
        out_shape=[jax.ShapeDtypeStruct((T.NLAT, D), BF16), jax.ShapeDtypeStruct((T.NLAT, D), F32),
                   jax.ShapeDtypeStruct((T.NLAT, D), F32), jax.ShapeDtypeStruct((T.NLAT, W), F32)],
        compiler_params=_cp(("parallel",)),
    )(y, kd, rw, p_cv, x2, modrows, rk, gw, lnw, lnb, cw, cb, clw, clb, pg, w_out)


MT = 512
FC = 1024


def _mlp_fwd(T, x1, modrows, g, w1, w2):
    per_b = T.NLT * TT // MT

    def kern(x_ref, mod_ref, g_ref, w1_ref, w2_ref, m_o, h2_o, h2_s):
        f = pl.program_id(1)

        @pl.when(f == 0)
        def _():
            x = x_ref[...]
            s = lax.rsqrt(_rowmean(x * x) + EPS_RMS)
            h2 = (x * s * g_ref[...]) * (1.0 + mod_ref[0, 0, 4:5, :]) + mod_ref[0, 0, 3:4, :]
            h2_s[...] = h2.astype(BF16)
            h2_o[...] = h2.astype(BF16)
            m_o[...] = jnp.zeros_like(m_o)

        a = jnp.dot(h2_s[...], w1_ref[...], preferred_element_type=F32)
        rl = jnp.maximum(a, 0.0)
        m_o[...] += jnp.dot((rl * rl).astype(BF16), w2_ref[...], preferred_element_type=F32)

    tok = lambda t, f: (t, 0)
    return pl.pallas_call(
        kern, grid=(T.NLAT // MT, DFF // FC), name="mlp_fwd",
        in_specs=[_bs((MT, D), tok), _bs((1, 1, 6, D), lambda t, f: (t // per_b, 1, 0, 0)),
                  _bs((1, D), lambda t, f: (0, 0)), _bs((D, FC), lambda t, f: (0, f)), _bs((FC, D), lambda t, f: (f, 0))],
        out_specs=[_bs((MT, D), tok), _bs((MT, D), tok)],
        out_shape=[jax.ShapeDtypeStruct((T.NLAT, D), F32), jax.ShapeDtypeStruct((T.NLAT, D), BF16)],
        scratch_shapes=[pltpu.VMEM((MT, D), BF16)],
        compiler_params=_cp(("parallel", "arbitrary")),
    )(x1, modrows, g, w1, w2)


def _loss_head(T, m, x1, tgt, modrows, pg):
    def kern(m_ref, x1_ref, t_ref, mod_ref, pg_ref, loss_o, dm_o, dx2_o, dg2_o, dpg_o):
        l = pl.program_id(0)
        m_ = m_ref[...]
        sm = lax.rsqrt(_rowmean(m_ * m_) + EPS_RMS)
        mn = m_ * sm
        g2 = mod_ref[0, 0, 5:6, :]
        pgv = pg_ref[...]
        diff = x1_ref[...] + g2 * (mn * pgv) - t_ref[...]
        sq = jnp.sum(_colsum(diff * diff), axis=1, keepdims=True)
        _acc(loss_o, jnp.zeros((8, 128), F32) + (0.5 / D) * sq, l == 0)
        dx2 = diff * (1.0 / D)
        dx2_o[...] = dx2
        _acc(dg2_o.at[0], _colsum(dx2 * mn * pgv), l % T.NLT == 0)
        _acc(dpg_o, _colsum(dx2 * g2 * mn), l == 0)
        dmn = dx2 * g2 * pgv
        dm_o[...] = (sm * (dmn - mn * _rowmean(dmn * mn))).astype(BF16)

    lat = lambda l: (l, 0)
    return pl.pallas_call(
        kern, grid=(T.NL,), name="loss_head",
        in_specs=[_bs((TT, D), lat), _bs((TT, D), lat), _bs((TT, D), lat),
                  _bs((1, 1, 6, D), lambda l: (l // T.NLT, 1, 0, 0)), _row(D)],
        out_specs=[_bs((8, 128), lambda l: (0, 0)), _bs((TT, D), lat), _bs((TT, D), lat),
                   _bs((1, 1, D), lambda l: (l // T.NLT, 0, 0)), _row(D)],
        out_shape=[jax.ShapeDtypeStruct((8, 128), F32), jax.ShapeDtypeStruct((T.NLAT, D), BF16),
                   jax.ShapeDtypeStruct((T.NLAT, D), F32), jax.ShapeDtypeStruct((T.B, 1, D), F32),
                   jax.ShapeDtypeStruct((1, D), F32)],
        compiler_params=_cp(("arbitrary",)),
    )(m, x1, tgt, modrows, pg)


def _mlp_bwd(T, h2, dm, w1, w2):
    def kern(h2_ref, dm_ref, w1_ref, w2_ref, f_o, da_o, dh2_o):
        f = pl.program_id(1)
        a = jnp.dot(h2_ref[...], w1_ref[...], preferred_element_type=F32)
        rl = jnp.maximum(a, 0.0)
        f_o[...] = (rl * rl).astype(BF16)
        df = lax.dot_general(dm_ref[...], w2_ref[...], _NT_DIMS, preferred_element_type=F32)
        dab = (df * (2.0 * rl)).astype(BF16)
        da_o[...] = dab
        _acc(dh2_o, lax.dot_general(dab, w1_ref[...], _NT_DIMS, preferred_element_type=F32), f == 0)

    tok = lambda t, f: (t, 0)
    return pl.pallas_call(
        kern, grid=(T.NLAT // MT, DFF // FC), name="mlp_bwd",
        in_specs=[_bs((MT, D), tok), _bs((MT, D), tok), _bs((D, FC), lambda t, f: (0, f)),
                  _bs((FC, D), lambda t, f: (f, 0))],
        out_specs=[_bs((MT, FC), lambda t, f: (t, f)), _bs((MT, FC), lambda t, f: (t, f)), _bs((MT, D), tok)],
        out_shape=[jax.ShapeDtypeStruct((T.NLAT, DFF), BF16), jax.ShapeDtypeStruct((T.NLAT, DFF), BF16),
                   jax.ShapeDtypeStruct((T.NLAT, D), F32)],
        compiler_params=_cp(("parallel", "arbitrary")),
    )(h2, dm, w1, w2)


def _mlp_in_bwd(T, dh2, x1, dx2, modrows, g):
    def kern(dh_ref, x1_ref, dx2_ref, mod_ref, g_ref, dx1_o, dmod_o, dg_o):
        i = pl.program_id(0)
        lat = (i % T.TPS != 0).astype(F32)
        x = x1_ref[...]
        s = lax.rsqrt(_rowmean(x * x) + EPS_RMS)
        xh = x * s
        gv = g_ref[...]
        dh = dh_ref[...] * lat
        n2 = xh * gv
        first_b = i % T.TPS == 0
        _acc(dmod_o.at[0, 0:1, :], _colsum(dh), first_b)
        _acc(dmod_o.at[0, 1:2, :], _colsum(dh * n2), first_b)
        dn2 = dh * (1.0 + mod_ref[0, 0, 4:5, :])
        _acc(dg_o, _colsum(dn2 * xh), i == 0)
        dxh = dn2 * gv
        dx1_o[...] = (dx2_ref[...] + s * (dxh - xh * _rowmean(dxh * xh))) * lat

    lat_i = lambda i: (T.lat(i), 0)
    return pl.pallas_call(
        kern, grid=(T.NT,), name="mlp_in_bwd",
        in_specs=[_bs((TT, D), lat_i), _bs((TT, D), lat_i), _bs((TT, D), lat_i),
                  _bs((1, 1, 6, D), lambda i: (i // T.TPS, 1, 0, 0)), _row(D)],
        out_specs=[_bs((TT, D), lambda i: (i, 0)), _bs((1, 2, D), lambda i: (i // T.TPS, 0, 0)), _row(D)],
        out_shape=[jax.ShapeDtypeStruct((T.NTOK, D), F32), jax.ShapeDtypeStruct((T.B, 2, D), F32),
                   jax.ShapeDtypeStruct((1, D), F32)],
        compiler_params=_cp(("arbitrary",)),
    )(dh2, x1, dx2, modrows, g)


def _mix_post_bwd(T, dx1, mix, modrows, pg, w_out):
    def kern(dx_ref, mix_ref, mod_ref, pg_ref, wo_ref, dmix_o, dcat_o, dg1_o, dpg_o):
        i = pl.program_id(0)
        lat = (i % T.TPS != 0).astype(F32)
        dx = dx_ref[...]
        mix = mix_ref[...]
        sm = lax.rsqrt(_rowmean(mix * mix) + EPS_RMS)
        mh = mix * sm
        g1 = mod_ref[0, 0, 2:3, :]
        pgv = pg_ref[...]
        _acc(dg1_o.at[0], _colsum(dx * mh * pgv), i % T.TPS == 0)
        _acc(dpg_o, _colsum(dx * g1 * mh), i == 0)
        dmh = dx * g1 * pgv
        dmix = ((sm * (dmh - mh * _rowmean(dmh * mh))) * lat).astype(BF16)
        dmix_o[...] = dmix
        dcat_o[...] = lax.dot_general(dmix, wo_ref[...], _NT_DIMS, preferred_element_type=F32)

    tok = lambda i: (i, 0)
    return pl.pallas_call(
        kern, grid=(T.NT,), name="mix_post_bwd",
        in_specs=[_bs((TT, D), tok), _bs((TT, D), lambda i: (T.lat(i), 0)),
                  _bs((1, 1, 6, D), lambda i: (i // T.TPS, 1, 0, 0)), _row(D), _bs((D, D), lambda i: (0, 0))],
        out_specs=[_bs((TT, D), tok), _bs((TT, D), tok), _bs((1, 1, D), lambda i: (i // T.TPS, 0, 0)), _row(D)],
        out_shape=[jax.ShapeDtypeStruct((T.NTOK, D), BF16), jax.ShapeDtypeStruct((T.NTOK, D), F32),
                   jax.ShapeDtypeStruct((T.B, 1, D), F32), jax.ShapeDtypeStruct((1, D), F32)],
        compiler_params=_cp(("arbitrary",)),
    )(dx1, mix, modrows, pg, w_out)


def _conv_bwd(T, dcat, p_cv, conv, cw, cb, clw, clb):
    def kern(dc_ref, pcv_ref, conv_ref, cw_ref, cb_ref, clw_ref, clb_ref, dp_o, dcw_o, dcb_o, dlw_o, dlb_o):
        i = pl.program_id(0)
        is_lat = i % T.TPS != 0

        @pl.when(i == 0)
        def _():
            for ref in (dcw_o, dcb_o, dlw_o, dlb_o):
                ref[...] = jnp.zeros(ref.shape, ref.dtype)

        @pl.when(jnp.logical_not(is_lat))
        def _():
            dp_o[...] = jnp.zeros(dp_o.shape, dp_o.dtype)

        @pl.when(is_lat)
        def _():
            cva = pcv_ref[:, 0:512]
            cv = _conv_fwd(cva, pcv_ref[:, 512:1024], cw_ref, cb_ref[...], clw_ref[...], clb_ref[...],
                           c=conv_ref[...])
            scn = cv["scn"]
            dcn = dc_ref[...] * (scn * (1.0 + cv["cn"] * (1.0 - scn)))
            chat = cv["chat"]
            dlw_o[...] += _colsum(dcn * chat)
            dlb_o[...] += _colsum(dcn)
            dchat = dcn * clw_ref[...]
            dc = cv["rstd"] * (dchat - _rowmean(dchat) - chat * _rowmean(dchat * chat))
            dcb_o[...] += _colsum(dc)
            pos = cv["pos"]
            u = cv["u"]
            du = jnp.zeros_like(u)
            for j in range(KCONV):
                s = j - KCONV // 2
                dcw_o[j:j + 1, :] += _colsum(dc * _shifted(u, s, pos))
                du = du + cw_ref[j:j + 1, :] * _shifted(dc, -s, pos)
            sgb = cv["sgb"]
            dp_o[...] = jnp.concatenate([du * sgb, du * cva * sgb * (1.0 - sgb)], axis=1).astype(BF16)

    return pl.pallas_call(
        kern, grid=(T.NT,), name="conv_bwd",
        in_specs=[_bs((TT, W), lambda i: (i, 1)), _bs((TT, CVC), lambda i: (i, 0)),
                  _bs((TT, W), lambda i: (T.lat(i), 0)),
                  _bs((32, W), lambda i: (0, 0)), _row(W), _row(W), _row(W)],
        out_specs=[_bs((TT, CVC), lambda i: (i, 0)), _bs((32, W), lambda i: (0, 0)), _row(W), _row(W), _row(W)],
        out_shape=[jax.ShapeDtypeStruct((T.NTOK, CVC), BF16), jax.ShapeDtypeStruct((32, W), F32),
                   jax.ShapeDtypeStruct((1, W), F32), jax.ShapeDtypeStruct((1, W), F32),
                   jax.ShapeDtypeStruct((1, W), F32)],
        compiler_params=_cp(("arbitrary",)),
    )(dcat, p_cv, conv, cw, cb, clw, clb)


def _readout_bwd(T, dcat, y, kd, rw, rk, gw, lnw, lnb):
    def kern(dc_ref, y_ref, kd_ref, rw_ref, rk_ref, gw_ref, lw_ref, lb_ref,
             dy_o, dr_o, dv_o, dkb_o, dgd_o, drk_o, dgw_o, dlw_o, dlb_o):
        i = pl.program_id(0)
        first = i == 0
        e = _e128(BF16)
        r = rw_ref[:, 0:512]
        v = rw_ref[:, 1024:1536]
        kbar = 0.5 * (kd_ref[0] + kd_ref[1])
        rk = rk_ref[...]
        ro = _readout_fwd(y_ref[0] + y_ref[1], r, v, rw_ref[:, 2048:2304], kbar, rk, gw_ref[...],
                          lw_ref[...], lb_ref[...], e)
        dout = dc_ref[...]
        dgg = dout * (ro["yn"] + ro["q"] * v)
        t1 = dout * ro["gg"]
        yhat = ro["yhat"]
        _acc(dlw_o, _colsum(t1 * yhat), first)
        _acc(dlb_o, _colsum(t1), first)
        dyh = t1 * lw_ref[...]
        dy_o[...] = ro["rstd"] * (dyh - _segsum(dyh, e) * (1.0 / HS) - yhat * (_segsum(dyh * yhat, e) * (1.0 / HS)))
        dq = _segsum(t1 * v, e)
        dv_o[...] = t1 * ro["q"]
        dr_o[...] = dq * kbar * rk
        dkb_o[...] = dq * r * rk
        _acc(drk_o, _colsum(dq * r * kbar), first)
        sg = ro["sg"]
        dsg = _bdot(dgg, gw_ref[...], _NT_DIMS)
        dgd_o[...] = dsg * sg * (1.0 - sg)
        _acc(dgw_o, _bdot(sg, dgg, _TN_DIMS), first)

    tok = lambda i: (i, 0)
    f32s = lambda *s: jax.ShapeDtypeStruct(s, F32)
    return pl.pallas_call(
        kern, grid=(T.NT,), name="readout_bwd",
        in_specs=[_bs((TT, W), tok), T.tm2_spec(), T.tm2_spec(), _bs((TT, RWC), tok),
                  _row(W), _bs((GDW, W), lambda i: (0, 0)), _row(W), _row(W)],
        out_specs=[T.tm_spec(), _bs((TT, W), tok), _bs((TT, W), tok), _bs((TT, W), tok), _bs((TT, GDW), tok),
                   _row(W), _bs((GDW, W), lambda i: (0, 0)), _row(W), _row(W)],
        out_shape=[f32s(T.TTOT, T.BW), f32s(T.NTOK, W), f32s(T.NTOK, W), f32s(T.NTOK, W), f32s(T.NTOK, GDW),
                   f32s(1, W), f32s(GDW, W), f32s(1, W), f32s(1, W)],
        compiler_params=_cp(("arbitrary",)),
    )(dcat, y, kd, rw, rk, gw, lnw, lnb)


def _scan_bwd(T, dy, r, v, kk, dec, kd, bb, hist, fin):
    NP = T.BW // 128
    R = NP * HS
    SB = SCAN_BSUB
    NS = T.TTOT // SB
    NSC = TT // SB

    def tmap(d, g):
        s = NS - 1 - g
        rev = jnp.where(s < NSC, NSC - 1 - s, NS - 1 - (s - NSC))
        return jnp.where(d == 0, s, rev)

    def kern(dy_ref, r_ref, v_ref, kk_ref, dec_ref, kd_ref, bb_ref, h_ref, fin_ref,
             dr_o, dw_o, dk_o, dv_o, da_o, db_o, ds_ref, snext):
        d = pl.program_id(0)
        g = pl.program_id(1)

        @pl.when(g == 0)
        def _():
            ds_ref[...] = jnp.zeros_like(ds_ref)
            snext[...] = fin_ref[0]

        e = _e128(BF16)

        for t in range(SB - 1, -1, -1):
            row = jnp.where(d == 0, t, SB - 1 - t)
            sp = h_ref[0, t]
            st = snext[...] if t == SB - 1 else h_ref[0, t + 1]
            a_ = _expand(-kk_ref[pl.ds(row, 1), :], NP)
            b_ = _expand(bb_ref[0, pl.ds(row, 1), :], NP)
            k_ = _expand(kd_ref[0, pl.ds(row, 1), :], NP)
            sa = _segb(sp * a_, e)
            vc = _colb(v_ref[pl.ds(row, 1), :], NP, e)
            dyc = _colb(dy_ref[pl.ds(row, 1), :], NP, e)
            ds = ds_ref[...] + dyc * _expand(r_ref[pl.ds(row, 1), :], NP)
            dsa = _segb(ds * b_, e)
            ds_ref[...] = ds * _expand(dec_ref[0, pl.ds(row, 1), :], NP) + dsa * a_
            dvb = _segb(ds * k_, e)
            dr_o[0, pl.ds(row, 1), :] = _pair_colsum(st * dyc, NP)
            dw_o[0, pl.ds(row, 1), :] = _pair_colsum(ds * sp, NP)
            db_o[0, pl.ds(row, 1), :] = _pair_colsum(ds * sa, NP)
            dv_o[0, pl.ds(row, 1), :] = _pair_colsum(_diag(dvb, NP), NP)
            dk_o[0, pl.ds(row, 1), :] = _pair_colsum(ds * vc, NP)
            da_o[0, pl.ds(row, 1), :] = _pair_colsum(sp * dsa, NP)
        snext[...] = h_ref[0, 0]

    sh = _bs((SB, T.BW), lambda d, g: (tmap(d, g), 0))
    dr = _bs((1, SB, T.BW), lambda d, g: (d, tmap(d, g), 0))
    o2 = jax.ShapeDtypeStruct((2, T.TTOT, T.BW), F32)
    return pl.pallas_call(
        kern, grid=(2, NS), name="scan_bwd",
        in_specs=[sh, sh, sh, sh, dr, dr, dr, _bs((1, SB, R, 128), lambda d, g: (d, NS - 1 - g, 0, 0)),
                  _bs((1, R, 128), lambda d, g: (d, 0, 0))],
        out_specs=[dr] * 6,
        out_shape=[o2] * 6,
        scratch_shapes=[pltpu.VMEM((R, 128), F32), pltpu.VMEM((R, 128), F32)],
        compiler_params=_cp(("arbitrary", "arbitrary"), mb=48),
    )(dy, r, v, kk, dec, kd, bb, hist, fin)


def _prep_bwd(T, rw, dr_s, ddec, dkd, dv_s, da_s, dbb, dr_ro, dv_ro, dkbar, dgd, w0, w2, a0, a2, k_k, k_a):
    def kern(rw_ref, drs_ref, ddec_ref, dkd_ref, dvs_ref, das_ref, dbb_ref, drr_ref, dvr_ref, dkb_ref, dgd_ref,
             w0_ref, w2_ref, a0_ref, a2_ref, kk_ref, ka_ref,
             drw_o, dw0_o, dw2_o, da0_o, da2_o, dkk_o, dka_o):
        i = pl.program_id(0)
        first = i == 0
        e = _e128(BF16)
        w0v, w2v, a0v, a2v = _load_prep_params(w0_ref, w2_ref, a0_ref, a2_ref)
        k_k = kk_ref[...]
        k_a = ka_ref[...]
        o = _prep_math(rw_ref[...], w0v, w2v, a0v, a2v, k_k, k_a, e)
        k, kk = o["k"], o["kk"]
        dkbh = 0.5 * dkb_ref[...]
        dk = jnp.zeros_like(k)
        dkk = -(das_ref[0] + das_ref[1])
        dka = jnp.zeros((1, W), F32)
        dwd, dad = [], []
        for d in (0, 1):
            iclr = o["iclr"][d]
            dkd_d = dkd_ref[d] + dkbh
            dbb_d = dbb_ref[d]
            dk = dk + dkd_d * (1.0 + (iclr - 1.0) * k_a)
            dka = dka + _colsum(dkd_d * k * (iclr - 1.0))
            dkk = dkk + dbb_d * iclr
            dicl = dkd_d * k * k_a + dbb_d * kk
            dpa = dicl * iclr * (1.0 - iclr)
            _acc(da0_o.at[d:d + 1, :], _colsum(dpa), first)
            dad.append(_bdot(dpa, a2v[d], _NT_DIMS))
            _acc(da2_o.at[d], _bdot(o["ad"][d], dpa, _TN_DIMS), first)
            dpre = -ddec_ref[d] * o["dec"][d] * o["ex"][d] * _sigmoid(-o["pre"][d])
            _acc(dw0_o.at[d:d + 1, :], _colsum(dpre), first)
            th = o["th"][d]
            dth = _bdot(dpre, w2v[d], _NT_DIMS)
            _acc(dw2_o.at[d], _bdot(th, dpre, _TN_DIMS), first)
            dwd.append(dth * (1.0 - th * th))
        inv = o["inv"]
        kr = o["kr"]
        proj = _segsum(dkk * kr, e)
        dkr = dkk * inv - jnp.where(o["rt"] > 1e-12, kr * inv * inv * inv * proj, 0.0)
        dk = dk + dkr * k_k
        _acc(dkk_o, _colsum(dkr * k), first)
        _acc(dka_o, dka, first)
        dr = drs_ref[0] + drs_ref[1] + drr_ref[...]
        dv = dvs_ref[0] + dvs_ref[1] + dvr_ref[...]
        drw_o[...] = jnp.concatenate([dr, dk, dv, dwd[0], dwd[1], dad[0], dad[1], dgd_ref[...]], axis=1)

    tok = lambda i: (i, 0)
    f32s = lambda *s: jax.ShapeDtypeStruct(s, F32)
    p2 = lambda i: (0, 0)
    p3 = lambda i: (0, 0, 0)
    return pl.pallas_call(
        kern, grid=(T.NT,), name="prep_bwd",
        in_specs=[_bs((TT, RWC), tok)] + [T.tm2_spec()] * 6 + [_bs((TT, W), tok)] * 3 + [_bs((TT, GDW), tok)]
        + _prep_param_specs(),
        out_specs=[_bs((TT, RWC), tok), _bs((2, W), p2), _bs((2, LRW, W), p3), _bs((2, W), p2),
                   _bs((2, LRW, W), p3), _row(W), _row(W)],
        out_shape=[f32s(T.NTOK, RWC), f32s(2, W), f32s(2, LRW, W), f32s(2, W), f32s(2, LRW, W), f32s(1, W), f32s(1, W)],
        compiler_params=_cp(("arbitrary",), mb=56),
    )(rw, dr_s, ddec, dkd, dv_s, da_s, dbb, dr_ro, dv_ro, dkbar, dgd, w0, w2, a0, a2, k_k, k_a)


def _shift_bwd(T, drw, p_rw, mu_p, mu_n):
    def kern(d_ref, dp_ref, dn_ref, p_ref, pp_ref, pn_ref, mp_ref, mn_ref, dprw_o, dmp_o, dmn_o):
        i = pl.program_id(0)
        first = i == 0
        has_prev, has_next = _halo_masks(T, i)
        mp = mp_ref[...]
        mn = mn_ref[...]
        drw = d_ref[...]
        z = p_ref[...]
        zprev, znext = _neighbours(z, pp_ref[7:8, :] * has_prev, pn_ref[0:1, :] * has_next)
        _acc(dmp_o, _colsum(drw * (zprev - z)), first)
        _acc(dmn_o, _colsum(drw * (znext - z)), first)
        dprev, dnext = _neighbours(drw, dp_ref[7:8, :] * has_prev, dn_ref[0:1, :] * has_next)
        dprw_o[...] = (drw * (1.0 - mp - mn) + mp * dnext + mn * dprev).astype(BF16)

    tok = lambda i: (i, 0)
    prev, nxt = _halo_specs(T)
    f32s = lambda *s: jax.ShapeDtypeStruct(s, F32)
    return pl.pallas_call(
        kern, grid=(T.NT,), name="shift_bwd",
        in_specs=[_bs((TT, RWC), tok), prev, nxt, _bs((TT, RWC), tok), prev, nxt, _row(RWC), _row(RWC)],
        out_specs=[_bs((TT, RWC), tok), _row(RWC), _row(RWC)],
        out_shape=[jax.ShapeDtypeStruct((T.NTOK, RWC), BF16), f32s(1, RWC), f32s(1, RWC)],
        compiler_params=_cp(("arbitrary",), mb=56),
    )(drw, drw, drw, p_rw, p_rw, p_rw, mu_p, mu_n)


def _mix_in_bwd(T, dp_rw, dp_cv, x2, c2, dx1, modrows, g, w_rw, w_cv):
    def kern(drw_ref, dcv_ref, x_ref, c_ref, dx1_ref, mod_ref, g_ref, wr_ref, wc_ref, dxc_o, dmod_o, dg_o):
        i = pl.program_id(0)
        dh = (lax.dot_general(drw_ref[...], wr_ref[...], _NT_DIMS, preferred_element_type=F32)
              + lax.dot_general(dcv_ref[...], wc_ref[...], _NT_DIMS, preferred_element_type=F32))
        x = _tok_tile(T, x_ref, c_ref)
        s = lax.rsqrt(_rowmean(x * x) + EPS_RMS)
        xh = x * s
        gv = g_ref[...]
        q = i % T.TPS
        first_kind = jnp.logical_or(q == 0, q == 1)
        _acc(dmod_o.at[0, 0, 0:1, :], _colsum(dh), first_kind)
        _acc(dmod_o.at[0, 0, 1:2, :], _colsum(dh * (xh * gv)), first_kind)
        dn1 = dh * (1.0 + mod_ref[0, 0, 1:2, :])
        _acc(dg_o, _colsum(dn1 * xh), i == 0)
        dxh = dn1 * gv
        dxc_o[...] = dx1_ref[...] + s * (dxh - xh * _rowmean(dxh * xh))

    tok = lambda i: (i, 0)
    f32s = lambda *s: jax.ShapeDtypeStruct(s, F32)
    return pl.pallas_call(
        kern, grid=(T.NT,), name="mix_in_bwd",
        in_specs=[_bs((TT, RWC), tok), _bs((TT, CVC), tok)] + _tok_specs(T) + [
            _bs((TT, D), tok), T.mod_spec(), _row(D), _bs((D, RWC), lambda i: (0, 0)), _bs((D, CVC), lambda i: (0, 0))],
        out_specs=[_bs((TT, D), lambda i: (T.lat(i), 0)),
                   _bs((1, 1, 2, D), lambda i: (i // T.TPS, jnp.minimum(i % T.TPS, 1), 0, 0)), _row(D)],
        out_shape=[f32s(T.NLAT, D), f32s(T.B, 2, 2, D), f32s(1, D)],
        compiler_params=_cp(("arbitrary",)),
    )(dp_rw, dp_cv, x2, c2, dx1, modrows, g, w_rw, w_cv)


def _matmul_tn(a, b, name, tk, nk, tn, amap=None, bmap=None, tm=1024):
    M = a.shape[1]
    N = b.shape[1]
    amap = amap or (lambda k: k)
    bmap = bmap or (lambda k: k)

    def kern(a_ref, b_ref, o_ref):
        _acc(o_ref, lax.dot_general(a_ref[...], b_ref[...], _TN_DIMS, preferred_element_type=F32),
             pl.program_id(2) == 0)

    return pl.pallas_call(
        kern, grid=(M // tm, N // tn, nk), name=name,
        in_specs=[_bs((tk, tm), lambda i, j, k: (amap(k), i)), _bs((tk, tn), lambda i, j, k: (bmap(k), j))],
        out_specs=_bs((tm, tn), lambda i, j, k: (i, j)),
        out_shape=jax.ShapeDtypeStruct((M, N), F32),
        compiler_params=_cp(("parallel", "parallel", "arbitrary")),
    )(a, b)


def _silu(x):
    return x * _sigmoid(x)


def _ada_fwd(c_all, c_ctx, ada_w, ada_b_blk):
    nb = c_all.shape[0]
    R = nb + 8
    ncol = ada_w.shape[1]

    def kern(c_ref, cc_ref, w_ref, b_ref, o_ref):
        lhs = jnp.concatenate([_silu(c_ref[...]), _silu(cc_ref[...]), jnp.zeros((7, D), F32)], axis=0)
        o_ref[...] = jnp.dot(lhs, w_ref[...], precision=HI, preferred_element_type=F32) + b_ref[...]

    return pl.pallas_call(
        kern, name="ada_fwd", out_shape=jax.ShapeDtypeStruct((R, ncol), F32),
        compiler_params=_cp(None, 40),
    )(c_all, c_ctx, ada_w, ada_b_blk)


def _ada_bwd(c_all, c_ctx, ada_w, ex, cx, ex_blk, cx_blk):
    nb = c_all.shape[0]
    ncol = ada_w.shape[1]

    def kern(c_ref, cc_ref, w_ref, ex_ref, cx_ref, exb_ref, cxb_ref, gw_o, gb_o, ds_o):
        lhs = jnp.concatenate([_silu(c_ref[...]), _silu(cc_ref[...]), jnp.zeros((7, D), F32)], axis=0)
        dmc_blk = _colsum(cxb_ref[...])
        rhs = jnp.concatenate([exb_ref[...], dmc_blk, jnp.zeros((7, ncol), F32)], axis=0)
        gw_o[...] = lax.dot_general(lhs, rhs, _TN_DIMS, precision=HI, preferred_element_type=F32)
        gb_o[...] = _colsum(ex_ref[...]) + _colsum(cx_ref[...])
        ds_o[...] = lax.dot_general(jnp.concatenate([dmc_blk, jnp.zeros((7, ncol), F32)], axis=0), w_ref[...],
                                    _NT_DIMS, precision=HI, preferred_element_type=F32)

    return pl.pallas_call(
        kern, name="ada_bwd",
        out_shape=[jax.ShapeDtypeStruct((D, ncol), F32), jax.ShapeDtypeStruct((1, ex.shape[1]), F32),
                   jax.ShapeDtypeStruct((8, D), F32)],
        compiler_params=_cp(None, 48),
    )(c_all, c_ctx, ada_w, ex, cx, ex_blk, cx_blk)


def _cctx_final(parts, c_ctx):
    def kern(p_ref, c_ref, o_ref):
        tot = p_ref[0, 0:1, :]
        for j in range(1, parts.shape[0]):
            tot = tot + p_ref[j, 0:1, :]
        c = c_ref[...]
        sg = _sigmoid(c)
        o_ref[...] = tot * (sg * (1.0 + c * (1.0 - sg)))

    return pl.pallas_call(kern, name="cctx_final", out_shape=jax.ShapeDtypeStruct((1, D), F32))(parts, c_ctx)


def _peer(kind, p, ix, iy, ic):
    if kind == "chips":
        return (p // 2, p % 2, ic)
    if kind == "all":
        return (p // 4, (p // 2) % 2, p % 2)
    return (ix, iy, p)


def _exchange(x, kind, bcast, name, chunks=1):
    npeer = {"chips": 4, "all": 8, "sib": 2}[kind]
    slab = x.shape if bcast else x.shape[1:]
    assert chunks == 1 or slab[0] == chunks

    def kern(x_ref, o_ref, send_sems, recv_sems, lsem):
        ix, iy, ic = lax.axis_index("x"), lax.axis_index("y"), lax.axis_index("c")
        me = {"chips": 2 * ix + iy, "all": 4 * ix + 2 * iy + ic, "sib": ic}[kind]
        own = pltpu.make_async_copy(x_ref if bcast else x_ref.at[me], o_ref.at[me], lsem)
        own.start()

        def part(ref, k):
            return ref if chunks == 1 else ref.at[k]

        def copy(p, k):
            return pltpu.make_async_remote_copy(
                src_ref=part(x_ref if bcast else x_ref.at[p], k), dst_ref=part(o_ref.at[me], k),
                send_sem=send_sems.at[p, k], recv_sem=recv_sems.at[me, k],
                device_id=_peer(kind, p, ix, iy, ic), device_id_type=MESH)

        def arrival(p, k):
            return pltpu.make_async_remote_copy(
                src_ref=part(x_ref if bcast else x_ref.at[p], k), dst_ref=part(o_ref.at[p], k),
                send_sem=send_sems.at[p, k], recv_sem=recv_sems.at[p, k],
                device_id=_peer(kind, p, ix, iy, ic), device_id_type=MESH)

        for p in range(npeer):
            @pl.when(me != p)
            def _():
                for k in range(chunks):
                    copy(p, k).start()
        for p in range(npeer):
            @pl.when(me != p)
            def _():
                for k in range(chunks):
                    arrival(p, k).wait_recv()
        for p in range(npeer):
            @pl.when(me != p)
            def _():
                for k in range(chunks):
                    copy(p, k).wait_send()
        own.wait()

    any_spec = pl.BlockSpec(memory_space=pl.ANY)
    return pl.pallas_call(
        kern, name=name, in_specs=[any_spec], out_specs=any_spec,
        out_shape=jax.ShapeDtypeStruct((npeer,) + tuple(slab), x.dtype),
        scratch_shapes=[pltpu.SemaphoreType.DMA((npeer, chunks)), pltpu.SemaphoreType.DMA((npeer, chunks)),
                        pltpu.SemaphoreType.DMA],
    )(x)


_HBM = pl.BlockSpec(memory_space=pltpu.HBM)
_SEM = pl.BlockSpec(memory_space=pltpu.SEMAPHORE)
_FLOWS = pltpu.SideEffectType.DATAFLOW_SIDE_EFFECTING


def _other_chips(ix, iy, ic):
    return ((1 - ix, iy, ic), (ix, 1 - iy, ic), (1 - ix, 1 - iy, ic))


def _chip_index(dev):
    return 2 * dev[0] + dev[1]


def _gather_start(x, name):
    def kern(x_ref, land_ref, send_sems, recv_sems, x_thru, land_thru, token):
        ix, iy, ic = lax.axis_index("x"), lax.axis_index("y"), lax.axis_index("c")
        me = 2 * ix + iy
        for k, peer in enumerate(_other_chips(ix, iy, ic)):
            pltpu.make_async_remote_copy(src_ref=x_ref, dst_ref=land_ref.at[me], send_sem=send_sems.at[k],
                                         recv_sem=recv_sems.at[k], device_id=peer, device_id_type=MESH).start()
        token[...] = jnp.zeros(token.shape, token.dtype)

    land = lax.empty((4,) + x.shape, x.dtype)
    return pl.pallas_call(
        kern, name=name,
        out_shape=(pltpu.SemaphoreType.DMA((3,)), pltpu.SemaphoreType.DMA((3,)), pltpu.HBM(x.shape, x.dtype),
                   pltpu.HBM(land.shape, land.dtype), jax.ShapeDtypeStruct((8, 128), F32)),
        in_specs=(_HBM, _HBM), out_specs=(_SEM, _SEM, _HBM, _HBM, pl.BlockSpec(memory_space=pltpu.VMEM)),
        input_output_aliases={0: 2, 1: 3},
        compiler_params=pltpu.CompilerParams(has_side_effects=_FLOWS),
    )(pltpu.with_memory_space_constraint(x, pltpu.HBM), pltpu.with_memory_space_constraint(land, pltpu.HBM))


def _gather_wait(send_sems, recv_sems, x_thru, land_thru, after, name):
    def kern(x_ref, land_ref, send_sems_ref, recv_sems_ref, after_ref, x_dead, land_out):
        ix, iy, ic = lax.axis_index("x"), lax.axis_index("y"), lax.axis_index("c")
        for k, peer in enumerate(_other_chips(ix, iy, ic)):
            copy = pltpu.make_async_remote_copy(src_ref=x_ref, dst_ref=land_ref.at[_chip_index(peer)],
                                                send_sem=send_sems_ref.at[k], recv_sem=recv_sems_ref.at[k],
                                                device_id=peer, device_id_type=MESH)
            copy.wait_send()
            copy.wait_recv()

    return pl.pallas_call(
        kern, name=name,
        out_shape=(pltpu.HBM(x_thru.shape, x_thru.dtype), pltpu.HBM(land_thru.shape, land_thru.dtype)),
        in_specs=(_HBM, _HBM, _SEM, _SEM, pl.BlockSpec(memory_space=pl.ANY)), out_specs=(_HBM, _HBM),
        input_output_aliases={0: 0, 1: 1},
        compiler_params=pltpu.CompilerParams(has_side_effects=_FLOWS),
    )(x_thru, land_thru, send_sems, recv_sems, after)


def _sum_slots(x, name):
    n, R, C = x.shape
    budget = (8 << 20) // (n * C * x.dtype.itemsize)
    tr = max([t for t in range(8, R + 1, 8) if R % t == 0 and t <= max(budget, 8)], default=R)

    def kern(x_ref, o_ref):
        tot = x_ref[0]
        for s in range(1, n):
            tot = tot + x_ref[s]
        o_ref[...] = tot

    return pl.pallas_call(
        kern, grid=(R // tr,), name=name,
        in_specs=[_bs((n, tr, C), lambda i: (0, i, 0))], out_specs=_bs((tr, C), lambda i: (i, 0)),
        out_shape=jax.ShapeDtypeStruct((R, C), x.dtype), compiler_params=_cp(("parallel",)),
    )(x)


def _sib_stream(x, me, name, add):
    K, R, C = x.shape[-3:]

    def kern(me_ref, *refs):
        if add:
            own_ref, send_ref, o_ref, rbuf, ssem, rsem, credit = refs
        else:
            send_ref, o_ref, rbuf, ssem, rsem, credit = refs
        k = pl.program_id(0)
        slot = k % 2
        sib = (lax.axis_index("x"), lax.axis_index("y"), 1 - lax.axis_index("c"))

        @pl.when(k >= 2)
        def _():
            pl.semaphore_wait(credit.at[slot], 1)

        src = send_ref.at[0, 0] if add else send_ref.at[0]
        cp = pltpu.make_async_remote_copy(src_ref=src, dst_ref=rbuf.at[slot], send_sem=ssem.at[slot],
                                          recv_sem=rsem.at[slot], device_id=sib, device_id_type=MESH)
        cp.start()
        cp.wait_recv()
        o_ref[0] = own_ref[0, 0] + rbuf[slot] if add else rbuf[slot]
        cp.wait_send()

        @pl.when(k + 2 < K)
        def _():
            pl.semaphore_signal(credit.at[slot], 1, device_id=sib, device_id_type=MESH)

    if add:
        in_specs = [_bs((1, 1, R, C), lambda k, me_ref: (me_ref[0], k, 0, 0)),
                    _bs((1, 1, R, C), lambda k, me_ref: (1 - me_ref[0], k, 0, 0))]
        args = (x, x)
    else:
        in_specs = [_bs((1, R, C), lambda k, me_ref: (k, 0, 0))]
        args = (x,)
    return pl.pallas_call(
        kern, name=name,
        grid_spec=pltpu.PrefetchScalarGridSpec(
            num_scalar_prefetch=1, grid=(K,), in_specs=in_specs,
            out_specs=_bs((1, R, C), lambda k, me_ref: (k, 0, 0)),
            scratch_shapes=[pltpu.VMEM((2, R, C), x.dtype), pltpu.SemaphoreType.DMA((2,)),
                            pltpu.SemaphoreType.DMA((2,)), pltpu.SemaphoreType.REGULAR((2,))]),
        out_shape=jax.ShapeDtypeStruct((K, R, C), x.dtype),
        compiler_params=_cp(("arbitrary",)),
    )(me, *args)


def _adamw(w, g, m, v, name):
    shape = w.shape
    if len(shape) == 1:
        outs = _adamw(*(t.reshape(1, -1) for t in (w, g, m, v)), name)
        return tuple(t.reshape(shape) for t in outs)
    nd = len(shape)
    size = 1
    for s in shape:
        size *= s
    rows = shape[-2]
    tr = rows
    if size > (1 << 18) and all(s == 1 for s in shape[:-2]):
        tr = max(t for t in (256, 128, 64, 32, 16, 8) if rows % t == 0)
    c1 = 1.0 - ADAM_B1 ** ADAM_STEP
    c2 = 1.0 - ADAM_B2 ** ADAM_STEP

    def kern(w_ref, g_ref, m_ref, v_ref, d_o, m_o, v_o):
        gv = g_ref[...]
        mn = ADAM_B1 * m_ref[...] + (1.0 - ADAM_B1) * gv
        vn = ADAM_B2 * v_ref[...] + (1.0 - ADAM_B2) * (gv * gv)
        m_o[...] = mn
        v_o[...] = vn
        d_o[...] = -ADAM_LR * ((mn / c1) / (jnp.sqrt(vn / c2) + ADAM_EPS) + ADAM_WD * w_ref[...])

    spec = _bs(shape[:-2] + (tr, shape[-1]), lambda i: (0,) * (nd - 2) + (i, 0))
    o = jax.ShapeDtypeStruct(shape, F32)
    return tuple(pl.pallas_call(
        kern, grid=(rows // tr,), name=name, in_specs=[spec] * 4, out_specs=[spec] * 3, out_shape=[o, o, o],
        compiler_params=_cp(("parallel",)),
    )(w, g, m, v))


_WEIGHT_NAMES = ("c_ctx", "ada_w", "ada_b", "mix_pre_g", "mix_post_g", "mlp_pre_g", "mlp_post_g", "w_in", "mu_prev",
                 "mu_next", "decay_w0", "decay_w2", "iclr_a0", "iclr_a2", "k_k", "k_a", "r_k", "gate_w2", "lnx_w",
                 "lnx_b", "conv_w", "conv_b", "conv_ln_w", "conv_ln_b", "w_out", "mlp_w1", "mlp_w2")


def _pack_rows(parts, cols=512):
    flat = jnp.concatenate([p.reshape(-1) for p in parts])
    rows = -(-flat.shape[0] // cols)
    rows = -(-rows // 16) * 16
    flat = jnp.pad(flat, (0, rows * cols - flat.shape[0]))
    return flat.reshape(rows, cols)


def _pack2d(parts, cols=512):
    return jnp.concatenate([p.reshape(-1, cols) for p in parts], axis=0)


def _unpack2d(buf, shapes):
    out = []
    off = 0
    for s in shapes:
        n = 1
        for d in s:
            n *= d
        n //= buf.shape[1]
        out.append(buf[off:off + n].reshape(s))
        off += n
    return out


def _unpack(flat, shapes):
    out = []
    off = 0
    for s in shapes:
        n = 1
        for d in s:
            n *= d
        out.append(flat[off:off + n].reshape(s))
        off += n
    return out


def _local_step(T, x2, c2, tgt, modrows, P, late_weights):
    p_rw, p_cv, h = _mix_in(T, x2, c2, modrows, P["mix_pre_g"], P["w_rw"], P["w_cv"])
    prep_params = (P["w0"], P["w2"], P["a0"], P["a2"], P["k_k"], P["k_a"])
    r, v, kk, dec, kd, bb, rw = _rwkv_prep(T, p_rw, P["mu_p"], P["mu_n"], *prep_params)
    y, fin, hist = _scan_fwd(T, r, v, kk, dec, kd, bb)
    P = dict(P, **late_weights(fin))
    ro_params = (P["r_k"], P["gate_w2"], P["lnx_w"], P["lnx_b"])
    cv_params = (P["conv_w"], P["conv_b"], P["conv_ln_w"], P["conv_ln_b"])
    cat, mix, x1, conv = _mix_out(T, y, kd, rw, p_cv, x2, modrows, *ro_params, *cv_params, P["mix_post_g"], P["w_out"])
    m, h2 = _mlp_fwd(T, x1, modrows, P["mlp_pre_g"], P["w1"], P["w2m"])
    loss_acc, dm, dx2, dg2, d_mlp_post = _loss_head(T, m, x1, tgt, modrows, P["mlp_post_g"])
    fact, da, dh2 = _mlp_bwd(T, h2, dm, P["w1"], P["w2m"])
    dx1, dmod2, d_mlp_pre = _mlp_in_bwd(T, dh2, x1, dx2, modrows, P["mlp_pre_g"])
    dmix, dcat, dg1, d_mix_post = _mix_post_bwd(T, dx1, mix, modrows, P["mix_post_g"], P["w_out"])
    dp_cv, d_conv_w, d_conv_b, d_cln_w, d_cln_b = _conv_bwd(T, dcat, p_cv, conv, *cv_params)
    dy, dr_ro, dv_ro, dkbar, dgd, d_r_k, d_gate, d_lnx_w, d_lnx_b = _readout_bwd(T, dcat, y, kd, rw, *ro_params)
    dr_s, ddec, dkd, dv_s, da_s, dbb = _scan_bwd(T, dy, r, v, kk, dec, kd, bb, hist, fin)
    drw, d_w0, d_w2, d_a0, d_a2, d_k_k, d_k_a = _prep_bwd(T, rw, dr_s, ddec, dkd, dv_s, da_s, dbb, dr_ro, dv_ro,
                                                          dkbar, dgd, *prep_params)
    dp_rw, d_mu_p, d_mu_n = _shift_bwd(T, drw, p_rw, P["mu_p"], P["mu_n"])
    dxc, dmod1, d_mix_pre = _mix_in_bwd(T, dp_rw, dp_cv, x2, c2, dx1, modrows, P["mix_pre_g"], P["w_rw"], P["w_cv"])
    kt = max(t for t in (1024, 768, 512, 256) if T.NTOK % t == 0)
    kl = max(t for t in (1024, 512, 256) if T.NLAT % t == 0)
    dw_rw = _matmul_tn(h, dp_rw, "dw_in_rw", kt, T.NTOK // kt, 768)
    dw_cv = _matmul_tn(h, dp_cv, "dw_in_cv", kt, T.NTOK // kt, 1024)
    dw_out = _matmul_tn(cat, dmix, "dw_out", TT, T.NL, 1024, bmap=T.tok)
    dw1 = _matmul_tn(h2, da, "dw_mlp1", kl, T.NLAT // kl, 1024)
    dw2m = _matmul_tn(fact, dm, "dw_mlp2", kl, T.NLAT // kl, 1024)
    small = dict(mix_pre_g=d_mix_pre, mix_post_g=d_mix_post, mlp_pre_g=d_mlp_pre, mlp_post_g=d_mlp_post,
                 mu_p=d_mu_p, mu_n=d_mu_n, w0=d_w0, w2=d_w2, a0=d_a0, a2=d_a2, k_k=d_k_k, k_a=d_k_a, r_k=d_r_k,
                 gate_w2=d_gate, lnx_w=d_lnx_w, lnx_b=d_lnx_b, conv_w=d_conv_w, conv_b=d_conv_b,
                 conv_ln_w=d_cln_w, conv_ln_b=d_cln_b)
    big = dict(w_rw=dw_rw, w_cv=dw_cv, w_out=dw_out, w1=dw1, w2m=dw2m)
    dmods = dict(dmod1=dmod1, dg1=dg1, dmod2=dmod2, dg2=dg2)
    return loss_acc[0, 0], dxc, small, big, dmods


_SMALL_ORDER = ("mix_pre_g", "mix_post_g", "mlp_pre_g", "mlp_post_g", "mu_p", "mu_n", "w0", "w2", "a0", "a2", "k_k",
                "k_a", "r_k", "gate_w2", "lnx_w", "lnx_b", "conv_w", "conv_b", "conv_ln_w", "conv_ln_b")


def kernel(x, c, ctx, c_ctx, ada_w, ada_b, mix_pre_g, mix_post_g, mlp_pre_g, mlp_post_g, w_in, mu_prev, mu_next, decay_w0, decay_w2, iclr_a0, iclr_a2, k_k, k_a, r_k, gate_w2, lnx_w, lnx_b, conv_w, conv_b, conv_ln_w, conv_ln_b, w_out, mlp_w1, mlp_w2, loss_target, m_c_ctx, m_ada_w, m_ada_b, m_mix_pre_g, m_mix_post_g, m_mlp_pre_g, m_mlp_post_g, m_w_in, m_mu_prev, m_mu_next, m_decay_w0, m_decay_w2, m_iclr_a0, m_iclr_a2, m_k_k, m_k_a, m_r_k, m_gate_w2, m_lnx_w, m_lnx_b, m_conv_w, m_conv_b, m_conv_ln_w, m_conv_ln_b, m_w_out, m_mlp_w1, m_mlp_w2, v_c_ctx, v_ada_w, v_ada_b, v_mix_pre_g, v_mix_post_g, v_mlp_pre_g, v_mlp_post_g, v_w_in, v_mu_prev, v_mu_next, v_decay_w0, v_decay_w2, v_iclr_a0, v_iclr_a2, v_k_k, v_k_a, v_r_k, v_gate_w2, v_lnx_w, v_lnx_b, v_conv_w, v_conv_b, v_conv_ln_w, v_conv_ln_b, v_w_out, v_mlp_w1, v_mlp_w2):
    weights = dict(zip(_WEIGHT_NAMES, (c_ctx, ada_w, ada_b, mix_pre_g, mix_post_g, mlp_pre_g, mlp_post_g, w_in, mu_prev, mu_next, decay_w0, decay_w2, iclr_a0, iclr_a2, k_k, k_a, r_k, gate_w2, lnx_w, lnx_b, conv_w, conv_b, conv_ln_w, conv_ln_b, w_out, mlp_w1, mlp_w2)))
    moms = dict(zip(_WEIGHT_NAMES, (m_c_ctx, m_ada_w, m_ada_b, m_mix_pre_g, m_mix_post_g, m_mlp_pre_g, m_mlp_post_g, m_w_in, m_mu_prev, m_mu_next, m_decay_w0, m_decay_w2, m_iclr_a0, m_iclr_a2, m_k_k, m_k_a, m_r_k, m_gate_w2, m_lnx_w, m_lnx_b, m_conv_w, m_conv_b, m_conv_ln_w, m_conv_ln_b, m_w_out, m_mlp_w1, m_mlp_w2)))
    vars_ = dict(zip(_WEIGHT_NAMES, (v_c_ctx, v_ada_w, v_ada_b, v_mix_pre_g, v_mix_post_g, v_mlp_pre_g, v_mlp_post_g, v_w_in, v_mu_prev, v_mu_next, v_decay_w0, v_decay_w2, v_iclr_a0, v_iclr_a2, v_k_k, v_k_a, v_r_k, v_gate_w2, v_lnx_w, v_lnx_b, v_conv_w, v_conv_b, v_conv_ln_w, v_conv_ln_b, v_w_out, v_mlp_w1, v_mlp_w2)))

    B, t_lat, _ = x.shape
    assert ctx.shape[1] == TT and t_lat % TT == 0 and (t_lat * B) % MT == 0
    T = _Tiles(B, t_lat)
    ix, iy, ic = lax.axis_index("x"), lax.axis_index("y"), lax.axis_index("c")
    chip = 2 * ix + iy
    dev = 4 * ix + 2 * iy + ic
    nsh = 4
    in_sh = w_in.shape[2]
    ada_sh = ada_w.shape[2]
    lane_sh = decay_w0.shape[2]

    big_parts = (w_in[0], w_out[0], mlp_w1[0], mlp_w2[0])
    big_shapes = [p.shape for p in big_parts]
    wg_in = _exchange(w_in[0].astype(BF16), "chips", True, "gather_w_in")
    w_in_f = jnp.concatenate([wg_in[j] for j in range(nsh)], axis=1)
    w_in_p = _pad_cols(w_in_f, w_in_f.shape[1])
    late_pack = _pack2d([p.astype(BF16) for p in big_parts[1:]])
    late_sems_s, late_sems_r, late_x, late_land, late_token = _gather_start(late_pack, "gather_mlp_start")

    def late_weights(after):
        own, land = _gather_wait(late_sems_s, late_sems_r, late_x, late_land, after, "gather_mlp_wait")
        land = lax.dynamic_update_slice(land, own[None], (chip, 0, 0))
        per = [_unpack2d(land[j], big_shapes[1:]) for j in range(nsh)]
        return dict(w_out=jnp.concatenate([per[j][0] for j in range(nsh)], axis=0),
                    w1=jnp.concatenate([per[j][1] for j in range(nsh)], axis=1),
                    w2m=jnp.concatenate([per[j][2] for j in range(nsh)], axis=0))

    sm_parts = (decay_w0[0], decay_w2[0], iclr_a0[0], iclr_a2[0], gate_w2[0], conv_w[0])
    sm_shapes = [p.shape for p in sm_parts]
    sg = _exchange(_pack_rows(sm_parts), "chips", True, "gather_small_weights")
    pers = [_unpack(sg[j].reshape(-1), sm_shapes) for j in range(nsh)]
    w0_f, w2_f_, a0_f, a2_f, gate_f, convw_f = (jnp.concatenate([pers[j][t] for j in range(nsh)], axis=-1)
                                                for t in range(6))

    def pad_rows(a, n):
        return jnp.pad(a, [(0, 0)] * (a.ndim - 2) + [(0, n - a.shape[-2]), (0, 0)])

    P = dict(
        w_rw=w_in_p[:, :RWC], w_cv=w_in_p[:, RWC:],
        mix_pre_g=mix_pre_g, mix_post_g=mix_post_g, mlp_pre_g=mlp_pre_g, mlp_post_g=mlp_post_g,
        mu_p=_pad_cols(mu_prev, mu_prev.shape[1]), mu_n=_pad_cols(mu_next, mu_next.shape[1]),
        w0=w0_f, w2=pad_rows(w2_f_, LRW), a0=a0_f, a2=pad_rows(a2_f, LRW), k_k=k_k, k_a=k_a,
        r_k=r_k.reshape(1, W), gate_w2=pad_rows(gate_f, GDW), lnx_w=lnx_w, lnx_b=lnx_b,
        conv_w=pad_rows(convw_f, 32), conv_b=conv_b, conv_ln_w=conv_ln_w, conv_ln_b=conv_ln_b)

    c_ctx2 = c_ctx.reshape(1, D)
    c_all = _exchange(jnp.pad(c, ((0, 8 - B), (0, 0))), "all", True, "gather_c")[:, :B].reshape(8 * B, D)
    ada_b_blk = lax.dynamic_slice(ada_b, (0, chip * ada_sh), (1, ada_sh))
    mod_blk = _ada_fwd(c_all, c_ctx2, ada_w[0], ada_b_blk)
    mod_g = _exchange(mod_blk, "chips", True, "gather_mod")
    mod_all = jnp.concatenate([mod_g[j] for j in range(nsh)], axis=1)
    mod_x = lax.dynamic_slice(mod_all, (dev * B, 0), (B, 6 * D)).reshape(B, 6, D)
    mod_c = jnp.broadcast_to(mod_all[8 * B].reshape(1, 6, D), (B, 6, D))
    modrows = jnp.stack([mod_c, mod_x], axis=1) + late_token[0, 0]

    x2 = x.reshape(T.NLAT, D)
    c2 = ctx.reshape(B * TT, D)
    tgt = loss_target.reshape(T.NLAT, D)
    loss_loc, dxl, small, big, dm_ = _local_step(T, x2, c2, tgt, modrows, P, late_weights)
    loss = lax.psum(loss_loc, ("x", "y", "c"))
    grad_x = dxl.reshape(x.shape)

    dmod_x = jnp.concatenate([dm_["dmod1"][:, 1], dm_["dg1"], dm_["dmod2"], dm_["dg2"]], axis=1)
    dmod_c = jnp.concatenate([dm_["dmod1"][:, 0], jnp.zeros((B, 4, D), F32)], axis=1)
    dpack = jnp.concatenate([dmod_x.reshape(B, 6 * D), dmod_c.reshape(B, 6 * D)], axis=0)
    dg = _exchange(dpack, "all", True, "gather_dmod")
    ex = dg[:, :B].reshape(8 * B, 6 * D)
    cx = dg[:, B:].reshape(8 * B, 6 * D)
    ex_blk = lax.dynamic_slice(ex, (0, chip * ada_sh), (8 * B, ada_sh))
    cx_blk = lax.dynamic_slice(cx, (0, chip * ada_sh), (8 * B, ada_sh))
    g_ada_w, g_ada_b, dscc = _ada_bwd(c_all, c_ctx2, ada_w[0], ex, cx, ex_blk, cx_blk)
    dscc_g = _exchange(dscc, "chips", True, "gather_dcctx")
    g_c_ctx = _cctx_final(dscc_g, c_ctx2).reshape(D)

    small = dict(small, mu_p=_unpad_cols(small["mu_p"], mu_prev.shape[1]),
                 mu_n=_unpad_cols(small["mu_n"], mu_next.shape[1]),
                 w2=small["w2"][:, :decay_w2.shape[2]], a2=small["a2"][:, :iclr_a2.shape[2]],
                 gate_w2=small["gate_w2"][:gate_w2.shape[1]], conv_w=small["conv_w"][:KCONV])
    sm_list = [small[n] for n in _SMALL_ORDER]
    sm_shapes2 = [a.shape for a in sm_list]
    me1 = ic.reshape(1).astype(jnp.int32)
    sm_pack = _pack_rows(sm_list)
    sm_pair = _sib_stream(jnp.stack([sm_pack, sm_pack])[:, None], me1, "sib_small_grads", True)[0]
    sm_tot = _sum_slots(_exchange(sm_pair, "chips", True, "gather_small_grads"), "sum_small_grads")
    S = dict(zip(_SMALL_ORDER, _unpack(sm_tot.reshape(-1), sm_shapes2)))

    def shard_last(a):
        return lax.dynamic_slice_in_dim(a, chip * lane_sh, lane_sh, axis=a.ndim - 1)

    grads = dict(
        c_ctx=g_c_ctx, ada_w=g_ada_w[None], ada_b=g_ada_b,
        mix_pre_g=S["mix_pre_g"], mix_post_g=S["mix_post_g"], mlp_pre_g=S["mlp_pre_g"], mlp_post_g=S["mlp_post_g"],
        mu_prev=S["mu_p"], mu_next=S["mu_n"],
        decay_w0=shard_last(S["w0"])[None], decay_w2=shard_last(S["w2"])[None],
        iclr_a0=shard_last(S["a0"])[None], iclr_a2=shard_last(S["a2"])[None],
        k_k=S["k_k"], k_a=S["k_a"], r_k=S["r_k"].reshape(r_k.shape),
        gate_w2=shard_last(S["gate_w2"])[None], lnx_w=S["lnx_w"], lnx_b=S["lnx_b"],
        conv_w=shard_last(S["conv_w"])[None], conv_b=S["conv_b"], conv_ln_w=S["conv_ln_w"],
        conv_ln_b=S["conv_ln_b"])

    dw_in_f = _unpad_cols(jnp.concatenate([big["w_rw"], big["w_cv"]], axis=1), w_in_f.shape[1])
    oshape = w_out.shape[1]
    mshape = mlp_w1.shape[2]
    slabs = [
        _pack2d([dw_in_f[:, in_sh * j:in_sh * (j + 1)], big["w_out"][oshape * j:oshape * (j + 1)],
                 big["w1"][:, mshape * j:mshape * (j + 1)], big["w2m"][mshape * j:mshape * (j + 1)]])
        for j in range(nsh)]
    nck = 3
    lr, lc = slabs[0].shape
    hr = lr // 2
    cr = hr // nck
    assert cr * 2 * nck == lr and cr % 8 == 0
    halves = jnp.stack([jnp.concatenate([s[h * hr:(h + 1) * hr].reshape(nck, cr, lc) for s in slabs], axis=0)
                        for h in (0, 1)])
    pair = _sib_stream(halves, me1, "sib_reduce_grads", True)
    mine = _sum_slots(_exchange(pair.reshape(nsh, nck * cr, lc), "chips", False, "reduce_big_grads"), "sum_big_grads")
    mine = mine.reshape(nck, cr, lc)
    other = _sib_stream(mine, me1, "sib_swap_grads", False)
    tot = jnp.where(ic == 0, jnp.concatenate([mine, other], axis=0), jnp.concatenate([other, mine], axis=0))
    g_w_in, g_w_out, g_w1, g_w2 = _unpack2d(tot.reshape(lr, lc), big_shapes)
    grads.update(w_in=g_w_in[None], w_out=g_w_out[None], mlp_w1=g_w1[None], mlp_w2=g_w2[None])

    deltas, new_m, new_v = {}, {}, {}
    for n in _WEIGHT_NAMES:
        g = grads[n].reshape(weights[n].shape)
        grads[n] = g
        deltas[n], new_m[n], new_v[n] = _adamw(weights[n], g, moms[n], vars_[n], "adamw_" + n)

    return (loss, grad_x, *[grads[n] for n in _WEIGHT_NAMES], *[deltas[n] for n in _WEIGHT_NAMES],
            *[new_m[n] for n in _WEIGHT_NAMES], *[new_v[n] for n in _WEIGHT_NAMES])
```

```python
import functools

import jax
import jax.numpy as jnp
from jax import lax
from jax.experimental import pallas as pl
from jax.experimental.pallas import tpu as pltpu

F32 = jnp.float32
BF16 = jnp.bfloat16
HI = lax.Precision.HIGHEST

D = 1024
W = 512
HS = 64
RWC = 2304
CVC = 1024
GDW = 256
LRW = 128
DFF = 4096
TT = 256
LINE = 64
KCONV = 31
EPS_RMS = 1e-6
EPS_LN = 1e-5
EPS_GN = 64e-5
SCAN_CH = 128
SCAN_G = 8
SCAN_BSUB = 16

ADAM_LR = 0.001
ADAM_B1 = 0.9
ADAM_B2 = 0.999
ADAM_EPS = 1e-08
ADAM_WD = 0.01
ADAM_STEP = 10

_SEGS = ((0, 1536, 1536), (1536, 64, 128), (1600, 64, 128), (1664, 64, 128), (1728, 64, 128),
         (1792, 160, 256), (1952, 1024, 1024))

MESH = pl.DeviceIdType.MESH


def _bs(shape, imap):
    return pl.BlockSpec(shape, imap)


def _cp(sem=None, mb=48):
    return pltpu.CompilerParams(dimension_semantics=sem, vmem_limit_bytes=mb << 20)


def _pad_cols(a, ncols):
    out = []
    for s, w, pw in _SEGS:
        if s >= ncols:
            break
        piece = a[..., s:s + w]
        if pw > w:
            piece = jnp.pad(piece, [(0, 0)] * (a.ndim - 1) + [(0, pw - w)])
        out.append(piece)
    return jnp.concatenate(out, axis=-1)


def _unpad_cols(a, ncols):
    out = []
    off = 0
    for s, w, pw in _SEGS:
        if s >= ncols:
            break
        out.append(a[..., off:off + w])
        off += pw
    return jnp.concatenate(out, axis=-1)


def _sigmoid(x):
    return 1.0 / (1.0 + jnp.exp(-x))


def _softplus(x):
    return jnp.maximum(x, 0.0) + jnp.log(1.0 + jnp.exp(-jnp.abs(x)))


def _e128(dtype):
    r = lax.broadcasted_iota(jnp.int32, (128, 128), 0) >= HS
    c = lax.broadcasted_iota(jnp.int32, (128, 128), 1) >= HS
    return (r == c).astype(dtype)


def _segsum(x, e):
    hi = x.astype(BF16)
    lo = (x - hi.astype(F32)).astype(BF16)
    return jnp.concatenate(
        [jnp.dot(hi[:, 128 * g:128 * (g + 1)], e, preferred_element_type=F32)
         + jnp.dot(lo[:, 128 * g:128 * (g + 1)], e, preferred_element_type=F32) for g in range(4)], axis=1)


_NT_DIMS = (((1,), (1,)), ((), ()))
_TN_DIMS = (((0,), (0,)), ((), ()))


def _bdot(a, b, dims=None):
    a = a.astype(BF16)
    b = b.astype(BF16)
    if dims is None:
        return jnp.dot(a, b, preferred_element_type=F32)
    return lax.dot_general(a, b, dims, preferred_element_type=F32)


def _colsum(x):
    return jnp.sum(x, axis=0, keepdims=True)


def _rowmean(x):
    return jnp.mean(x, axis=-1, keepdims=True)


def _diag(x, npairs):
    row = lax.broadcasted_iota(jnp.int32, (HS, 128), 0)
    lane = lax.broadcasted_iota(jnp.int32, (HS, 128), 1) & (HS - 1)
    keep = jnp.broadcast_to((lane == row)[None], (npairs, HS, 128))
    return jnp.where(keep, x.reshape(npairs, HS, 128), 0.0).reshape(npairs * HS, 128)


def _segb(x, e):
    return jnp.dot(x.astype(BF16), e, preferred_element_type=F32)


_segb1 = _segb


def _expand(row, npairs):
    return jnp.concatenate([jnp.broadcast_to(row[:, 128 * j:128 * (j + 1)], (HS, 128)) for j in range(npairs)], axis=0)


def _colb(row, npairs, e):
    return _segb1(_diag(_expand(row, npairs), npairs), e)


def _pair_colsum(x, npairs):
    return jnp.concatenate([_colsum(x[HS * j:HS * (j + 1)]) for j in range(npairs)], axis=1)


def _conv_pos():
    return lax.broadcasted_iota(jnp.int32, (TT, W), 0) & (LINE - 1)


def _shifted(u, s, pos):
    if s == 0:
        return u
    sh = pltpu.roll(u, (-s) % TT, 0)
    valid = jnp.logical_and(pos + s >= 0, pos + s < LINE)
    return jnp.where(valid, sh, 0.0)


def _acc(ref, val, first):
    @pl.when(first)
    def _():
        ref[...] = jnp.zeros(ref.shape, ref.dtype)
    ref[...] += val


class _Tiles:
    def __init__(self, B, t_lat):
        self.B = B
        self.NLT = t_lat // TT
        self.TPS = self.NLT + 1
        self.NT = B * self.TPS
        self.NL = B * self.NLT
        self.NTOK = self.NT * TT
        self.NLAT = self.NL * TT
        self.TTOT = self.TPS * TT
        self.BW = B * W

    def b(self, i):
        return i // self.TPS

    def q(self, i):
        return i % self.TPS

    def lat(self, i):
        return (i // self.TPS) * self.NLT + jnp.maximum(i % self.TPS - 1, 0)

    def tok(self, l):
        return (l // self.NLT) * self.TPS + 1 + l % self.NLT

    def mod_spec(self):
        return _bs((1, 1, 6, D), lambda i: (i // self.TPS, jnp.minimum(i % self.TPS, 1), 0, 0))

    def tm_spec(self):
        return _bs((TT, W), lambda i: (i % self.TPS, i // self.TPS))

    def tm2_spec(self):
        return _bs((2, TT, W), lambda i: (0, i % self.TPS, i // self.TPS))


def _row(shape_last):
    return _bs((1, shape_last), lambda i: (0, 0))


def _tok_specs(T):
    return [_bs((TT, D), lambda i: (T.lat(i), 0)), _bs((TT, D), lambda i: (i // T.TPS, 0))]


def _tok_tile(T, x_ref, c_ref):
    is_ctx = (pl.program_id(0) % T.TPS == 0).astype(F32)
    return c_ref[...] * is_ctx + x_ref[...] * (1.0 - is_ctx)


def _mix_in(T, x2, c2, modrows, g, w_rw, w_cv):
    def kern(x_ref, c_ref, mod_ref, g_ref, wr_ref, wc_ref, prw_ref, pcv_ref, h_ref):
        x = _tok_tile(T, x_ref, c_ref)
        s = lax.rsqrt(_rowmean(x * x) + EPS_RMS)
        h = (x * s * g_ref[...]) * (1.0 + mod_ref[0, 0, 1:2, :]) + mod_ref[0, 0, 0:1, :]
        hb = h.astype(BF16)
        h_ref[...] = hb
        prw_ref[...] = jnp.dot(hb, wr_ref[...], preferred_element_type=F32)
        pcv_ref[...] = jnp.dot(hb, wc_ref[...], preferred_element_type=F32)

    return pl.pallas_call(
        kern, grid=(T.NT,), name="mix_in",
        in_specs=_tok_specs(T) + [T.mod_spec(), _row(D),
                                  _bs((D, RWC), lambda i: (0, 0)), _bs((D, CVC), lambda i: (0, 0))],
        out_specs=[_bs((TT, RWC), lambda i: (i, 0)), _bs((TT, CVC), lambda i: (i, 0)), _bs((TT, D), lambda i: (i, 0))],
        out_shape=[jax.ShapeDtypeStruct((T.NTOK, RWC), F32), jax.ShapeDtypeStruct((T.NTOK, CVC), F32),
                   jax.ShapeDtypeStruct((T.NTOK, D), BF16)],
        compiler_params=_cp(("parallel",)),
    )(x2, c2, modrows, g, w_rw, w_cv)


def _halo_specs(T):
    nb8 = T.NTOK // 8
    prev = _bs((8, RWC), lambda i: (jnp.maximum(i * (TT // 8) - 1, 0), 0))
    nxt = _bs((8, RWC), lambda i: (jnp.minimum((i + 1) * (TT // 8), nb8 - 1), 0))
    return prev, nxt


def _halo_masks(T, i):
    q = i % T.TPS
    has_prev = jnp.logical_and(q != 0, q != 1).astype(F32)
    has_next = jnp.logical_and(q != 0, q != T.TPS - 1).astype(F32)
    return has_prev, has_next


def _neighbours(z, prev_row, next_row):
    rowi = lax.broadcasted_iota(jnp.int32, z.shape, 0)
    zprev = jnp.where(rowi == 0, prev_row, pltpu.roll(z, 1, 0))
    znext = jnp.where(rowi == TT - 1, next_row, pltpu.roll(z, TT - 1, 0))
    return zprev, znext


def _prep_math(rw, w0, w2, a0, a2, k_k, k_a, e):
    r = rw[:, 0:512]
    k = rw[:, 512:1024]
    v = rw[:, 1024:1536]
    kr = k * k_k
    ss = _segsum(kr * kr, e)
    rt = jnp.sqrt(ss)
    inv = 1.0 / jnp.maximum(rt, 1e-12)
    kk = kr * inv
    o = dict(r=r, k=k, v=v, kr=kr, rt=rt, inv=inv, kk=kk, th=[], pre=[], ex=[], dec=[], iclr=[], kd=[], bb=[], ad=[])
    for d in (0, 1):
        wd = rw[:, 1536 + LRW * d:1536 + LRW * (d + 1)]
        ad = rw[:, 1792 + LRW * d:1792 + LRW * (d + 1)]
        th = jnp.tanh(wd)
        pre = w0[d] + _bdot(th, w2[d])
        ex = jnp.exp(-_softplus(-pre) - 0.5)
        dec = jnp.exp(-ex)
        iclr = _sigmoid(a0[d] + _bdot(ad, a2[d]))
        o["th"].append(th)
        o["pre"].append(pre)
        o["ex"].append(ex)
        o["dec"].append(dec)
        o["iclr"].append(iclr)
        o["ad"].append(ad)
        o["kd"].append(k * (1.0 + (iclr - 1.0) * k_a))
        o["bb"].append(kk * iclr)
    return o


def _load_prep_params(w0_ref, w2_ref, a0_ref, a2_ref):
    w0 = [w0_ref[0:1, :], w0_ref[1:2, :]]
    a0 = [a0_ref[0:1, :], a0_ref[1:2, :]]
    w2 = [w2_ref[0], w2_ref[1]]
    a2 = [a2_ref[0], a2_ref[1]]
    return w0, w2, a0, a2


def _prep_param_specs():
    return [_bs((2, W), lambda i: (0, 0)), _bs((2, LRW, W), lambda i: (0, 0, 0)),
            _bs((2, W), lambda i: (0, 0)), _bs((2, LRW, W), lambda i: (0, 0, 0)), _row(W), _row(W)]


def _rwkv_prep(T, p_rw, mu_p, mu_n, w0, w2, a0, a2, k_k, k_a):
    def kern(p_ref, pp_ref, pn_ref, mp_ref, mn_ref, w0_ref, w2_ref, a0_ref, a2_ref, kk_ref, ka_ref,
             r_o, v_o, kk_o, dec_o, kd_o, bb_o, rw_o):
        i = pl.program_id(0)
        has_prev, has_next = _halo_masks(T, i)
        z = p_ref[...]
        zprev, znext = _neighbours(z, pp_ref[7:8, :] * has_prev, pn_ref[0:1, :] * has_next)
        rw = z + mp_ref[...] * (zprev - z) + mn_ref[...] * (znext - z)
        rw_o[...] = rw
        w0v, w2v, a0v, a2v = _load_prep_params(w0_ref, w2_ref, a0_ref, a2_ref)
        o = _prep_math(rw, w0v, w2v, a0v, a2v, kk_ref[...], ka_ref[...], _e128(BF16))
        r_o[...] = o["r"]
        v_o[...] = o["v"]
        kk_o[...] = o["kk"]
        for d in (0, 1):
            dec_o[d] = o["dec"][d]
            kd_o[d] = o["kd"][d]
            bb_o[d] = o["bb"][d]

    prev, nxt = _halo_specs(T)
    tm = jax.ShapeDtypeStruct((T.TTOT, T.BW), F32)
    tm2 = jax.ShapeDtypeStruct((2, T.TTOT, T.BW), F32)
    return pl.pallas_call(
        kern, grid=(T.NT,), name="rwkv_prep",
        in_specs=[_bs((TT, RWC), lambda i: (i, 0)), prev, nxt, _row(RWC), _row(RWC)] + _prep_param_specs(),
        out_specs=[T.tm_spec(), T.tm_spec(), T.tm_spec(), T.tm2_spec(), T.tm2_spec(), T.tm2_spec(),
                   _bs((TT, RWC), lambda i: (i, 0))],
        out_shape=[tm, tm, tm, tm2, tm2, tm2, jax.ShapeDtypeStruct((T.NTOK, RWC), F32)],
        compiler_params=_cp(("parallel",)),
    )(p_rw, p_rw, p_rw, mu_p, mu_n, w0, w2, a0, a2, k_k, k_a)


def _scan_fwd(T, r, v, kk, dec, kd, bb):
    NP = T.BW // 128
    R = NP * HS
    NCH = T.TTOT // SCAN_CH
    NCC = TT // SCAN_CH
    G = SCAN_G
    NG = SCAN_CH // G
    NGRP = 4
    assert NG % NGRP == 0

    def tmap(d, i):
        rev = jnp.where(i < NCC, NCC - 1 - i, NCH - 1 - (i - NCC))
        return jnp.where(d == 0, i, rev)

    def kern(r_ref, v_ref, kk_ref, dec_ref, kd_ref, bb_ref, y_ref, fin_ref, hist_ref, ring, sems):
        d = pl.program_id(0)
        i = pl.program_id(1)

        @pl.when(i == 0)
        def _():
            ring[0] = jnp.zeros((R, 128), F32)

        e = _e128(BF16)

        def hist_copy(k):
            grp = k % NGRP
            return pltpu.make_async_copy(ring.at[pl.ds(grp * G, G)],
                                         hist_ref.at[d, pl.ds(i * SCAN_CH + k * G, G)], sems.at[grp])

        def body(k, carry):
            @pl.when(k >= NGRP - 1)
            def _():
                hist_copy(k - (NGRP - 1)).wait()

            base = (k % NGRP) * G
            for u in range(G):
                t = k * G + u
                row = jnp.where(d == 0, t, SCAN_CH - 1 - t)
                s = ring[base + u]
                sa = _segb(s * _expand(-kk_ref[pl.ds(row, 1), :], NP), e)
                vc = _colb(v_ref[pl.ds(row, 1), :], NP, e)
                s = (s * _expand(dec_ref[0, pl.ds(row, 1), :], NP) + sa * _expand(bb_ref[0, pl.ds(row, 1), :], NP)
                     + vc * _expand(kd_ref[0, pl.ds(row, 1), :], NP))
                ring[(base + u + 1) if u < G - 1 else ((k + 1) % NGRP) * G] = s
                yb = _segb(s * _expand(r_ref[pl.ds(row, 1), :], NP), e)
                y_ref[0, pl.ds(row, 1), :] = _pair_colsum(_diag(yb, NP), NP)
            hist_copy(k).start()
            return carry

        lax.fori_loop(0, NG, body, 0)
        for k in range(NG - (NGRP - 1), NG):
            hist_copy(k).wait()

        @pl.when(i == NCH - 1)
        def _():
            fin_ref[0] = ring[0]

    sh = _bs((SCAN_CH, T.BW), lambda d, i: (tmap(d, i), 0))
    dr = _bs((1, SCAN_CH, T.BW), lambda d, i: (d, tmap(d, i), 0))
    return pl.pallas_call(
        kern, grid=(2, NCH), name="scan_fwd",
        in_specs=[sh, sh, sh, dr, dr, dr],
        out_specs=[dr, _bs((1, R, 128), lambda d, i: (d, 0, 0)), pl.BlockSpec(memory_space=pl.ANY)],
        out_shape=[jax.ShapeDtypeStruct((2, T.TTOT, T.BW), F32), jax.ShapeDtypeStruct((2, R, 128), F32),
                   jax.ShapeDtypeStruct((2, T.TTOT, R, 128), F32)],
        scratch_shapes=[pltpu.VMEM((NGRP * G, R, 128), F32), pltpu.SemaphoreType.DMA((NGRP,))],
        compiler_params=_cp(("arbitrary", "arbitrary")),
    )(r, v, kk, dec, kd, bb)


def _readout_fwd(y, r, v, gd, kbar, rk, gw, lw, lb, e):
    mu = _segsum(y, e) * (1.0 / HS)
    yc = y - mu
    var = _segsum(yc * yc, e) * (1.0 / HS)
    rstd = lax.rsqrt(var + EPS_GN)
    yhat = yc * rstd
    yn = yhat * lw + lb
    q = _segsum(r * kbar * rk, e)
    sg = _sigmoid(gd)
    gg = _bdot(sg, gw)
    return dict(yhat=yhat, rstd=rstd, yn=yn, q=q, sg=sg, gg=gg, out=(yn + q * v) * gg)


def _conv_fwd(cva, cvb, cw_ref, cb, lw, lb, c=None):
    pos = _conv_pos()
    sgb = _sigmoid(cvb)
    u = cva * sgb
    if c is None:
        c = jnp.zeros_like(u)
        for j in range(KCONV):
            c = c + cw_ref[j:j + 1, :] * _shifted(u, j - KCONV // 2, pos)
        c = c + cb
    mu = _rowmean(c)
    cc = c - mu
    rstd = lax.rsqrt(_rowmean(cc * cc) + EPS_LN)
    chat = cc * rstd
    cn = chat * lw + lb
    scn = _sigmoid(cn)
    return dict(sgb=sgb, u=u, c=c, chat=chat, rstd=rstd, cn=cn, scn=scn, out=cn * scn, pos=pos)


def _mix_out(T, y, kd, rw, p_cv, x2, modrows, rk, gw, lnw, lnb, cw, cb, clw, clb, pg, w_out):
    tk = T.tok

    def kern(y_ref, kd_ref, rw_ref, pcv_ref, x_ref, mod_ref, rk_ref, gw_ref, lw_ref, lb_ref, cw_ref, cb_ref,
             clw_ref, clb_ref, pg_ref, wo_ref, cat_o, mix_o, x1_o, conv_o):
        e = _e128(BF16)
        ro = _readout_fwd(y_ref[0] + y_ref[1], rw_ref[:, 0:512], rw_ref[:, 1024:1536], rw_ref[:, 2048:2304],
                          0.5 * (kd_ref[0] + kd_ref[1]), rk_ref[...], gw_ref[...], lw_ref[...], lb_ref[...], e)
        cv = _conv_fwd(pcv_ref[:, 0:512], pcv_ref[:, 512:1024], cw_ref, cb_ref[...], clw_ref[...], clb_ref[...])
        conv_o[...] = cv["c"]
        catb = jnp.concatenate([ro["out"], cv["out"]], axis=1).astype(BF16)
        cat_o[...] = catb
        mix = jnp.dot(catb, wo_ref[...], preferred_element_type=F32)
        mix_o[...] = mix
        sm = lax.rsqrt(_rowmean(mix * mix) + EPS_RMS)
        x1_o[...] = x_ref[...] + mod_ref[0, 0, 2:3, :] * (mix * sm * pg_ref[...])

    lat = lambda l: (l, 0)
    return pl.pallas_call(
        kern, grid=(T.NL,), name="mix_out",
        in_specs=[_bs((2, TT, W), lambda l: (0, 1 + l % T.NLT, l // T.NLT)),
                  _bs((2, TT, W), lambda l: (0, 1 + l % T.NLT, l // T.NLT)),
                  _bs((TT, RWC), lambda l: (tk(l), 0)), _bs((TT, CVC), lambda l: (tk(l), 0)),
                  _bs((TT, D), lambda l: (l, 0)),
                  _bs((1, 1, 6, D), lambda l: (l // T.NLT, 1, 0, 0)),
                  _row(W), _bs((GDW, W), lambda l: (0, 0)), _row(W), _row(W),
                  _bs((32, W), lambda l: (0, 0)), _row(W), _row(W), _row(W), _row(D),
                  _bs((D, D), lambda l: (0, 0))],
        out_specs=[_bs((TT, D), lat), _bs((TT, D), lat), _bs((TT, D), lat), _bs((TT, W), lat)],
        out_shape=[jax.ShapeDtypeStruct((T.NLAT, D), BF16), jax.ShapeDtypeStruct((T.NLAT, D), F32),
                   jax.ShapeDtypeStruct((T.NLAT, D), F32), jax.ShapeDtypeStruct((T.NLAT, W), F32)],
        compiler_params=_cp(("parallel",)),
    )(y, kd, rw, p_cv, x2, modrows, rk, gw, lnw, lnb, cw, cb, clw, clb, pg, w_out)


MT = 512
FC = 1024


def _late_weight_specs():
    assert FC == D
    return [_bs((1, D, FC), lambda t, f: (f, 0, 0)), _bs((1, FC, D), lambda t, f: (f, 1, 0))]


def _mlp_fwd(T, x1, modrows, g, wl):
    per_b = T.NLT * TT // MT

    def kern(x_ref, mod_ref, g_ref, w1_ref, w2_ref, m_o, h2_o, h2_s):
        f = pl.program_id(1)

        @pl.when(f == 0)
        def _():
            x = x_ref[...]
            s = lax.rsqrt(_rowmean(x * x) + EPS_RMS)
            h2 = (x * s * g_ref[...]) * (1.0 + mod_ref[0, 0, 4:5, :]) + mod_ref[0, 0, 3:4, :]
            h2_s[...] = h2.astype(BF16)
            h2_o[...] = h2.astype(BF16)
            m_o[...] = jnp.zeros_like(m_o)

        a = jnp.dot(h2_s[...], w1_ref[0], preferred_element_type=F32)
        rl = jnp.maximum(a, 0.0)
        m_o[...] += jnp.dot((rl * rl).astype(BF16), w2_ref[0], preferred_element_type=F32)

    tok = lambda t, f: (t, 0)
    return pl.pallas_call(
        kern, grid=(T.NLAT // MT, DFF // FC), name="mlp_fwd",
        in_specs=[_bs((MT, D), tok), _bs((1, 1, 6, D), lambda t, f: (t // per_b, 1, 0, 0)),
                  _bs((1, D), lambda t, f: (0, 0))] + _late_weight_specs(),
        out_specs=[_bs((MT, D), tok), _bs((MT, D), tok)],
        out_shape=[jax.ShapeDtypeStruct((T.NLAT, D), F32), jax.ShapeDtypeStruct((T.NLAT, D), BF16)],
        scratch_shapes=[pltpu.VMEM((MT, D), BF16)],
        compiler_params=_cp(("parallel", "arbitrary")),
    )(x1, modrows, g, wl, wl)


def _loss_head(T, m, x1, tgt, modrows, pg):
    def kern(m_ref, x1_ref, t_ref, mod_ref, pg_ref, loss_o, dm_o, dx2_o, dg2_o, dpg_o):
        l = pl.program_id(0)
        m_ = m_ref[...]
        sm = lax.rsqrt(_rowmean(m_ * m_) + EPS_RMS)
        mn = m_ * sm
        g2 = mod_ref[0, 0, 5:6, :]
        pgv = pg_ref[...]
        diff = x1_ref[...] + g2 * (mn * pgv) - t_ref[...]
        sq = jnp.sum(_colsum(diff * diff), axis=1, keepdims=True)
        _acc(loss_o, jnp.zeros((8, 128), F32) + (0.5 / D) * sq, l == 0)
        dx2 = diff * (1.0 / D)
        dx2_o[...] = dx2
        _acc(dg2_o.at[0], _colsum(dx2 * mn * pgv), l % T.NLT == 0)
        _acc(dpg_o, _colsum(dx2 * g2 * mn), l == 0)
        dmn = dx2 * g2 * pgv
        dm_o[...] = (sm * (dmn - mn * _rowmean(dmn * mn))).astype(BF16)

    lat = lambda l: (l, 0)
    return pl.pallas_call(
        kern, grid=(T.NL,), name="loss_head",
        in_specs=[_bs((TT, D), lat), _bs((TT, D), lat), _bs((TT, D), lat),
                  _bs((1, 1, 6, D), lambda l: (l // T.NLT, 1, 0, 0)), _row(D)],
        out_specs=[_bs((8, 128), lambda l: (0, 0)), _bs((TT, D), lat), _bs((TT, D), lat),
                   _bs((1, 1, D), lambda l: (l // T.NLT, 0, 0)), _row(D)],
        out_shape=[jax.ShapeDtypeStruct((8, 128), F32), jax.ShapeDtypeStruct((T.NLAT, D), BF16),
                   jax.ShapeDtypeStruct((T.NLAT, D), F32), jax.ShapeDtypeStruct((T.B, 1, D), F32),
                   jax.ShapeDtypeStruct((1, D), F32)],
        compiler_params=_cp(("arbitrary",)),
    )(m, x1, tgt, modrows, pg)


def _mlp_bwd(T, h2, dm, wl):
    def kern(h2_ref, dm_ref, w1_ref, w2_ref, f_o, da_o, dh2_o):
        f = pl.program_id(1)
        a = jnp.dot(h2_ref[...], w1_ref[0], preferred_element_type=F32)
        rl = jnp.maximum(a, 0.0)
        f_o[...] = (rl * rl).astype(BF16)
        df = lax.dot_general(dm_ref[...], w2_ref[0], _NT_DIMS, preferred_element_type=F32)
        dab = (df * (2.0 * rl)).astype(BF16)
        da_o[...] = dab
        _acc(dh2_o, lax.dot_general(dab, w1_ref[0], _NT_DIMS, preferred_element_type=F32), f == 0)

    tok = lambda t, f: (t, 0)
    return pl.pallas_call(
        kern, grid=(T.NLAT // MT, DFF // FC), name="mlp_bwd",
        in_specs=[_bs((MT, D), tok), _bs((MT, D), tok)] + _late_weight_specs(),
        out_specs=[_bs((MT, FC), lambda t, f: (t, f)), _bs((MT, FC), lambda t, f: (t, f)), _bs((MT, D), tok)],
        out_shape=[jax.ShapeDtypeStruct((T.NLAT, DFF), BF16), jax.ShapeDtypeStruct((T.NLAT, DFF), BF16),
                   jax.ShapeDtypeStruct((T.NLAT, D), F32)],
        compiler_params=_cp(("parallel", "arbitrary")),
    )(h2, dm, wl, wl)


def _mlp_in_bwd(T, dh2, x1, dx2, modrows, g):
    def kern(dh_ref, x1_ref, dx2_ref, mod_ref, g_ref, dx1_o, dmod_o, dg_o):
        i = pl.program_id(0)
        lat = (i % T.TPS != 0).astype(F32)
        x = x1_ref[...]
        s = lax.rsqrt(_rowmean(x * x) + EPS_RMS)
        xh = x * s
        gv = g_ref[...]
        dh = dh_ref[...] * lat
        n2 = xh * gv
        first_b = i % T.TPS == 0
        _acc(dmod_o.at[0, 0:1, :], _colsum(dh), first_b)
        _acc(dmod_o.at[0, 1:2, :], _colsum(dh * n2), first_b)
        dn2 = dh * (1.0 + mod_ref[0, 0, 4:5, :])
        _acc(dg_o, _colsum(dn2 * xh), i == 0)
        dxh = dn2 * gv
        dx1_o[...] = (dx2_ref[...] + s * (dxh - xh * _rowmean(dxh * xh))) * lat

    lat_i = lambda i: (T.lat(i), 0)
    return pl.pallas_call(
        kern, grid=(T.NT,), name="mlp_in_bwd",
        in_specs=[_bs((TT, D), lat_i), _bs((TT, D), lat_i), _bs((TT, D), lat_i),
                  _bs((1, 1, 6, D), lambda i: (i // T.TPS, 1, 0, 0)), _row(D)],
        out_specs=[_bs((TT, D), lambda i: (i, 0)), _bs((1, 2, D), lambda i: (i // T.TPS, 0, 0)), _row(D)],
        out_shape=[jax.ShapeDtypeStruct((T.NTOK, D), F32), jax.ShapeDtypeStruct((T.B, 2, D), F32),
                   jax.ShapeDtypeStruct((1, D), F32)],
        compiler_params=_cp(("arbitrary",)),
    )(dh2, x1, dx2, modrows, g)


def _mix_post_bwd(T, dx1, mix, modrows, pg, w_out):
    def kern(dx_ref, mix_ref, mod_ref, pg_ref, wo_ref, dmix_o, dcat_o, dg1_o, dpg_o):
        i = pl.program_id(0)
        lat = (i % T.TPS != 0).astype(F32)
        dx = dx_ref[...]
        mix = mix_ref[...]
        sm = lax.rsqrt(_rowmean(mix * mix) + EPS_RMS)
        mh = mix * sm
        g1 = mod_ref[0, 0, 2:3, :]
        pgv = pg_ref[...]
        _acc(dg1_o.at[0], _colsum(dx * mh * pgv), i % T.TPS == 0)
        _acc(dpg_o, _colsum(dx * g1 * mh), i == 0)
        dmh = dx * g1 * pgv
        dmix = ((sm * (dmh - mh * _rowmean(dmh * mh))) * lat).astype(BF16)
        dmix_o[...] = dmix
        dcat_o[...] = lax.dot_general(dmix, wo_ref[...], _NT_DIMS, preferred_element_type=F32)

    tok = lambda i: (i, 0)
    return pl.pallas_call(
        kern, grid=(T.NT,), name="mix_post_bwd",
        in_specs=[_bs((TT, D), tok), _bs((TT, D), lambda i: (T.lat(i), 0)),
                  _bs((1, 1, 6, D), lambda i: (i // T.TPS, 1, 0, 0)), _row(D), _bs((D, D), lambda i: (0, 0))],
        out_specs=[_bs((TT, D), tok), _bs((TT, D), tok), _bs((1, 1, D), lambda i: (i // T.TPS, 0, 0)), _row(D)],
        out_shape=[jax.ShapeDtypeStruct((T.NTOK, D), BF16), jax.ShapeDtypeStruct((T.NTOK, D), F32),
                   jax.ShapeDtypeStruct((T.B, 1, D), F32), jax.ShapeDtypeStruct((1, D), F32)],
        compiler_params=_cp(("arbitrary",)),
    )(dx1, mix, modrows, pg, w_out)


def _conv_bwd(T, dcat, p_cv, conv, cw, cb, clw, clb):
    def kern(dc_ref, pcv_ref, conv_ref, cw_ref, cb_ref, clw_ref, clb_ref, dp_o, dcw_o, dcb_o, dlw_o, dlb_o):
        i = pl.program_id(0)
        is_lat = i % T.TPS != 0

        @pl.when(i == 0)
        def _():
            for ref in (dcw_o, dcb_o, dlw_o, dlb_o):
                ref[...] = jnp.zeros(ref.shape, ref.dtype)

        @pl.when(jnp.logical_not(is_lat))
        def _():
            dp_o[...] = jnp.zeros(dp_o.shape, dp_o.dtype)

        @pl.when(is_lat)
        def _():
            cva = pcv_ref[:, 0:512]
            cv = _conv_fwd(cva, pcv_ref[:, 512:1024], cw_ref, cb_ref[...], clw_ref[...], clb_ref[...],
                           c=conv_ref[...])
            scn = cv["scn"]
            dcn = dc_ref[...] * (scn * (1.0 + cv["cn"] * (1.0 - scn)))
            chat = cv["chat"]
            dlw_o[...] += _colsum(dcn * chat)
            dlb_o[...] += _colsum(dcn)
            dchat = dcn * clw_ref[...]
            dc = cv["rstd"] * (dchat - _rowmean(dchat) - chat * _rowmean(dchat * chat))
            dcb_o[...] += _colsum(dc)
            pos = cv["pos"]
            u = cv["u"]
            du = jnp.zeros_like(u)
            for j in range(KCONV):
                s = j - KCONV // 2
                dcw_o[j:j + 1, :] += _colsum(dc * _shifted(u, s, pos))
                du = du + cw_ref[j:j + 1, :] * _shifted(dc, -s, pos)
            sgb = cv["sgb"]
            dp_o[...] = jnp.concatenate([du * sgb, du * cva * sgb * (1.0 - sgb)], axis=1).astype(BF16)

    return pl.pallas_call(
        kern, grid=(T.NT,), name="conv_bwd",
        in_specs=[_bs((TT, W), lambda i: (i, 1)), _bs((TT, CVC), lambda i: (i, 0)),
                  _bs((TT, W), lambda i: (T.lat(i), 0)),
                  _bs((32, W), lambda i: (0, 0)), _row(W), _row(W), _row(W)],
        out_specs=[_bs((TT, CVC), lambda i: (i, 0)), _bs((32, W), lambda i: (0, 0)), _row(W), _row(W), _row(W)],
        out_shape=[jax.ShapeDtypeStruct((T.NTOK, CVC), BF16), jax.ShapeDtypeStruct((32, W), F32),
                   jax.ShapeDtypeStruct((1, W), F32), jax.ShapeDtypeStruct((1, W), F32),
                   jax.ShapeDtypeStruct((1, W), F32)],
        compiler_params=_cp(("arbitrary",)),
    )(dcat, p_cv, conv, cw, cb, clw, clb)


def _readout_bwd(T, dcat, y, kd, rw, rk, gw, lnw, lnb):
    def kern(dc_ref, y_ref, kd_ref, rw_ref, rk_ref, gw_ref, lw_ref, lb_ref,
             dy_o, dr_o, dv_o, dkb_o, dgd_o, drk_o, dgw_o, dlw_o, dlb_o):
        i = pl.program_id(0)
        first = i == 0
        e = _e128(BF16)
        r = rw_ref[:, 0:512]
        v = rw_ref[:, 1024:1536]
        kbar = 0.5 * (kd_ref[0] + kd_ref[1])
        rk = rk_ref[...]
        ro = _readout_fwd(y_ref[0] + y_ref[1], r, v, rw_ref[:, 2048:2304], kbar, rk, gw_ref[...],
                          lw_ref[...], lb_ref[...], e)
        dout = dc_ref[...]
        dgg = dout * (ro["yn"] + ro["q"] * v)
        t1 = dout * ro["gg"]
        yhat = ro["yhat"]
        _acc(dlw_o, _colsum(t1 * yhat), first)
        _acc(dlb_o, _colsum(t1), first)
        dyh = t1 * lw_ref[...]
        dy_o[...] = ro["rstd"] * (dyh - _segsum(dyh, e) * (1.0 / HS) - yhat * (_segsum(dyh * yhat, e) * (1.0 / HS)))
        dq = _segsum(t1 * v, e)
        dv_o[...] = t1 * ro["q"]
        dr_o[...] = dq * kbar * rk
        dkb_o[...] = dq * r * rk
        _acc(drk_o, _colsum(dq * r * kbar), first)
        sg = ro["sg"]
        dsg = _bdot(dgg, gw_ref[...], _NT_DIMS)
        dgd_o[...] = dsg * sg * (1.0 - sg)
        _acc(dgw_o, _bdot(sg, dgg, _TN_DIMS), first)

    tok = lambda i: (i, 0)
    f32s = lambda *s: jax.ShapeDtypeStruct(s, F32)
    return pl.pallas_call(
        kern, grid=(T.NT,), name="readout_bwd",
        in_specs=[_bs((TT, W), tok), T.tm2_spec(), T.tm2_spec(), _bs((TT, RWC), tok),
                  _row(W), _bs((GDW, W), lambda i: (0, 0)), _row(W), _row(W)],
        out_specs=[T.tm_spec(), _bs((TT, W), tok), _bs((TT, W), tok), _bs((TT, W), tok), _bs((TT, GDW), tok),
                   _row(W), _bs((GDW, W), lambda i: (0, 0)), _row(W), _row(W)],
        out_shape=[f32s(T.TTOT, T.BW), f32s(T.NTOK, W), f32s(T.NTOK, W), f32s(T.NTOK, W), f32s(T.NTOK, GDW),
                   f32s(1, W), f32s(GDW, W), f32s(1, W), f32s(1, W)],
        compiler_params=_cp(("arbitrary",)),
    )(dcat, y, kd, rw, rk, gw, lnw, lnb)


def _scan_bwd(T, dy, r, v, kk, dec, kd, bb, hist, fin):
    NP = T.BW // 128
    R = NP * HS
    SB = SCAN_BSUB
    NS = T.TTOT // SB
    NSC = TT // SB

    def tmap(d, g):
        s = NS - 1 - g
        rev = jnp.where(s < NSC, NSC - 1 - s, NS - 1 - (s - NSC))
        return jnp.where(d == 0, s, rev)

    def kern(dy_ref, r_ref, v_ref, kk_ref, dec_ref, kd_ref, bb_ref, h_ref, fin_ref,
             dr_o, dw_o, dk_o, dv_o, da_o, db_o, ds_ref, snext):
        d = pl.program_id(0)
        g = pl.program_id(1)

        @pl.when(g == 0)
        def _():
            ds_ref[...] = jnp.zeros_like(ds_ref)
            snext[...] = fin_ref[0]

        e = _e128(BF16)

        for t in range(SB - 1, -1, -1):
            row = jnp.where(d == 0, t, SB - 1 - t)
            sp = h_ref[0, t]
            st = snext[...] if t == SB - 1 else h_ref[0, t + 1]
            a_ = _expand(-kk_ref[pl.ds(row, 1), :], NP)
            b_ = _expand(bb_ref[0, pl.ds(row, 1), :], NP)
            k_ = _expand(kd_ref[0, pl.ds(row, 1), :], NP)
            sa = _segb(sp * a_, e)
            vc = _colb(v_ref[pl.ds(row, 1), :], NP, e)
            dyc = _colb(dy_ref[pl.ds(row, 1), :], NP, e)
            ds = ds_ref[...] + dyc * _expand(r_ref[pl.ds(row, 1), :], NP)
            dsa = _segb(ds * b_, e)
            ds_ref[...] = ds * _expand(dec_ref[0, pl.ds(row, 1), :], NP) + dsa * a_
            dvb = _segb(ds * k_, e)
            dr_o[0, pl.ds(row, 1), :] = _pair_colsum(st * dyc, NP)
            dw_o[0, pl.ds(row, 1), :] = _pair_colsum(ds * sp, NP)
            db_o[0, pl.ds(row, 1), :] = _pair_colsum(ds * sa, NP)
            dv_o[0, pl.ds(row, 1), :] = _pair_colsum(_diag(dvb, NP), NP)
            dk_o[0, pl.ds(row, 1), :] = _pair_colsum(ds * vc, NP)
            da_o[0, pl.ds(row, 1), :] = _pair_colsum(sp * dsa, NP)
        snext[...] = h_ref[0, 0]

    sh = _bs((SB, T.BW), lambda d, g: (tmap(d, g), 0))
    dr = _bs((1, SB, T.BW), lambda d, g: (d, tmap(d, g), 0))
    o2 = jax.ShapeDtypeStruct((2, T.TTOT, T.BW), F32)
    return pl.pallas_call(
        kern, grid=(2, NS), name="scan_bwd",
        in_specs=[sh, sh, sh, sh, dr, dr, dr, _bs((1, SB, R, 128), lambda d, g: (d, NS - 1 - g, 0, 0)),
                  _bs((1, R, 128), lambda d, g: (d, 0, 0))],
        out_specs=[dr] * 6,
        out_shape=[o2] * 6,
        scratch_shapes=[pltpu.VMEM((R, 128), F32), pltpu.VMEM((R, 128), F32)],
        compiler_params=_cp(("arbitrary", "arbitrary"), mb=48),
    )(dy, r, v, kk, dec, kd, bb, hist, fin)


def _prep_bwd(T, rw, dr_s, ddec, dkd, dv_s, da_s, dbb, dr_ro, dv_ro, dkbar, dgd, w0, w2, a0, a2, k_k, k_a):
    def kern(rw_ref, drs_ref, ddec_ref, dkd_ref, dvs_ref, das_ref, dbb_ref, drr_ref, dvr_ref, dkb_ref, dgd_ref,
             w0_ref, w2_ref, a0_ref, a2_ref, kk_ref, ka_ref,
             drw_o, dw0_o, dw2_o, da0_o, da2_o, dkk_o, dka_o):
        i = pl.program_id(0)
        first = i == 0
        e = _e128(BF16)
        w0v, w2v, a0v, a2v = _load_prep_params(w0_ref, w2_ref, a0_ref, a2_ref)
        k_k = kk_ref[...]
        k_a = ka_ref[...]
        o = _prep_math(rw_ref[...], w0v, w2v, a0v, a2v, k_k, k_a, e)
        k, kk = o["k"], o["kk"]
        dkbh = 0.5 * dkb_ref[...]
        dk = jnp.zeros_like(k)
        dkk = -(das_ref[0] + das_ref[1])
        dka = jnp.zeros((1, W), F32)
        dwd, dad = [], []
        for d in (0, 1):
            iclr = o["iclr"][d]
            dkd_d = dkd_ref[d] + dkbh
            dbb_d = dbb_ref[d]
            dk = dk + dkd_d * (1.0 + (iclr - 1.0) * k_a)
            dka = dka + _colsum(dkd_d * k * (iclr - 1.0))
            dkk = dkk + dbb_d * iclr
            dicl = dkd_d * k * k_a + dbb_d * kk
            dpa = dicl * iclr * (1.0 - iclr)
            _acc(da0_o.at[d:d + 1, :], _colsum(dpa), first)
            dad.append(_bdot(dpa, a2v[d], _NT_DIMS))
            _acc(da2_o.at[d], _bdot(o["ad"][d], dpa, _TN_DIMS), first)
            dpre = -ddec_ref[d] * o["dec"][d] * o["ex"][d] * _sigmoid(-o["pre"][d])
            _acc(dw0_o.at[d:d + 1, :], _colsum(dpre), first)
            th = o["th"][d]
            dth = _bdot(dpre, w2v[d], _NT_DIMS)
            _acc(dw2_o.at[d], _bdot(th, dpre, _TN_DIMS), first)
            dwd.append(dth * (1.0 - th * th))
        inv = o["inv"]
        kr = o["kr"]
        proj = _segsum(dkk * kr, e)
        dkr = dkk * inv - jnp.where(o["rt"] > 1e-12, kr * inv * inv * inv * proj, 0.0)
        dk = dk + dkr * k_k
        _acc(dkk_o, _colsum(dkr * k), first)
        _acc(dka_o, dka, first)
        dr = drs_ref[0] + drs_ref[1] + drr_ref[...]
        dv = dvs_ref[0] + dvs_ref[1] + dvr_ref[...]
        drw_o[...] = jnp.concatenate([dr, dk, dv, dwd[0], dwd[1], dad[0], dad[1], dgd_ref[...]], axis=1)

    tok = lambda i: (i, 0)
    f32s = lambda *s: jax.ShapeDtypeStruct(s, F32)
    p2 = lambda i: (0, 0)
    p3 = lambda i: (0, 0, 0)
    return pl.pallas_call(
        kern, grid=(T.NT,), name="prep_bwd",
        in_specs=[_bs((TT, RWC), tok)] + [T.tm2_spec()] * 6 + [_bs((TT, W), tok)] * 3 + [_bs((TT, GDW), tok)]
        + _prep_param_specs(),
        out_specs=[_bs((TT, RWC), tok), _bs((2, W), p2), _bs((2, LRW, W), p3), _bs((2, W), p2),
                   _bs((2, LRW, W), p3), _row(W), _row(W)],
        out_shape=[f32s(T.NTOK, RWC), f32s(2, W), f32s(2, LRW, W), f32s(2, W), f32s(2, LRW, W), f32s(1, W), f32s(1, W)],
        compiler_params=_cp(("arbitrary",), mb=56),
    )(rw, dr_s, ddec, dkd, dv_s, da_s, dbb, dr_ro, dv_ro, dkbar, dgd, w0, w2, a0, a2, k_k, k_a)


def _shift_bwd(T, drw, p_rw, mu_p, mu_n):
    def kern(d_ref, dp_ref, dn_ref, p_ref, pp_ref, pn_ref, mp_ref, mn_ref, dprw_o, dmp_o, dmn_o):
        i = pl.program_id(0)
        first = i == 0
        has_prev, has_next = _halo_masks(T, i)
        mp = mp_ref[...]
        mn = mn_ref[...]
        drw = d_ref[...]
        z = p_ref[...]
        zprev, znext = _neighbours(z, pp_ref[7:8, :] * has_prev, pn_ref[0:1, :] * has_next)
        _acc(dmp_o, _colsum(drw * (zprev - z)), first)
        _acc(dmn_o, _colsum(drw * (znext - z)), first)
        dprev, dnext = _neighbours(drw, dp_ref[7:8, :] * has_prev, dn_ref[0:1, :] * has_next)
        dprw_o[...] = (drw * (1.0 - mp - mn) + mp * dnext + mn * dprev).astype(BF16)

    tok = lambda i: (i, 0)
    prev, nxt = _halo_specs(T)
    f32s = lambda *s: jax.ShapeDtypeStruct(s, F32)
    return pl.pallas_call(
        kern, grid=(T.NT,), name="shift_bwd",
        in_specs=[_bs((TT, RWC), tok), prev, nxt, _bs((TT, RWC), tok), prev, nxt, _row(RWC), _row(RWC)],
        out_specs=[_bs((TT, RWC), tok), _row(RWC), _row(RWC)],
        out_shape=[jax.ShapeDtypeStruct((T.NTOK, RWC), BF16), f32s(1, RWC), f32s(1, RWC)],
        compiler_params=_cp(("arbitrary",), mb=56),
    )(drw, drw, drw, p_rw, p_rw, p_rw, mu_p, mu_n)


def _mix_in_bwd(T, dp_rw, dp_cv, x2, c2, dx1, modrows, g, w_rw, w_cv):
    def kern(drw_ref, dcv_ref, x_ref, c_ref, dx1_ref, mod_ref, g_ref, wr_ref, wc_ref, dxc_o, dmod_o, dg_o):
        i = pl.program_id(0)
        dh = (lax.dot_general(drw_ref[...], wr_ref[...], _NT_DIMS, preferred_element_type=F32)
              + lax.dot_general(dcv_ref[...], wc_ref[...], _NT_DIMS, preferred_element_type=F32))
        x = _tok_tile(T, x_ref, c_ref)
        s = lax.rsqrt(_rowmean(x * x) + EPS_RMS)
        xh = x * s
        gv = g_ref[...]
        q = i % T.TPS
        first_kind = jnp.logical_or(q == 0, q == 1)
        _acc(dmod_o.at[0, 0, 0:1, :], _colsum(dh), first_kind)
        _acc(dmod_o.at[0, 0, 1:2, :], _colsum(dh * (xh * gv)), first_kind)
        dn1 = dh * (1.0 + mod_ref[0, 0, 1:2, :])
        _acc(dg_o, _colsum(dn1 * xh), i == 0)
        dxh = dn1 * gv
        dxc_o[...] = dx1_ref[...] + s * (dxh - xh * _rowmean(dxh * xh))

    tok = lambda i: (i, 0)
    f32s = lambda *s: jax.ShapeDtypeStruct(s, F32)
    return pl.pallas_call(
        kern, grid=(T.NT,), name="mix_in_bwd",
        in_specs=[_bs((TT, RWC), tok), _bs((TT, CVC), tok)] + _tok_specs(T) + [
            _bs((TT, D), tok), T.mod_spec(), _row(D), _bs((D, RWC), lambda i: (0, 0)), _bs((D, CVC), lambda i: (0, 0))],
        out_specs=[_bs((TT, D), lambda i: (T.lat(i), 0)),
                   _bs((1, 1, 2, D), lambda i: (i // T.TPS, jnp.minimum(i % T.TPS, 1), 0, 0)), _row(D)],
        out_shape=[f32s(T.NLAT, D), f32s(T.B, 2, 2, D), f32s(1, D)],
        compiler_params=_cp(("arbitrary",)),
    )(dp_rw, dp_cv, x2, c2, dx1, modrows, g, w_rw, w_cv)


def _matmul_tn(a, b, name, tk, nk, tn, amap=None, bmap=None, tm=1024):
    M = a.shape[1]
    N = b.shape[1]
    amap = amap or (lambda k: k)
    bmap = bmap or (lambda k: k)

    def kern(a_ref, b_ref, o_ref):
        _acc(o_ref, lax.dot_general(a_ref[...], b_ref[...], _TN_DIMS, preferred_element_type=F32),
             pl.program_id(2) == 0)

    return pl.pallas_call(
        kern, grid=(M // tm, N // tn, nk), name=name,
        in_specs=[_bs((tk, tm), lambda i, j, k: (amap(k), i)), _bs((tk, tn), lambda i, j, k: (bmap(k), j))],
        out_specs=_bs((tm, tn), lambda i, j, k: (i, j)),
        out_shape=jax.ShapeDtypeStruct((M, N), F32),
        compiler_params=_cp(("parallel", "parallel", "arbitrary")),
    )(a, b)


def _silu(x):
    return x * _sigmoid(x)


def _ada_fwd(c_all, c_ctx, ada_w, ada_b_blk):
    nb = c_all.shape[0]
    R = nb + 8
    ncol = ada_w.shape[1]

    def kern(c_ref, cc_ref, w_ref, b_ref, o_ref):
        lhs = jnp.concatenate([_silu(c_ref[...]), _silu(cc_ref[...]), jnp.zeros((7, D), F32)], axis=0)
        o_ref[...] = jnp.dot(lhs, w_ref[...], precision=HI, preferred_element_type=F32) + b_ref[...]

    return pl.pallas_call(
        kern, name="ada_fwd", out_shape=jax.ShapeDtypeStruct((R, ncol), F32),
        compiler_params=_cp(None, 40),
    )(c_all, c_ctx, ada_w, ada_b_blk)


def _ada_bwd(c_all, c_ctx, ada_w, ex, cx, ex_blk, cx_blk):
    nb = c_all.shape[0]
    ncol = ada_w.shape[1]

    def kern(c_ref, cc_ref, w_ref, ex_ref, cx_ref, exb_ref, cxb_ref, gw_o, gb_o, ds_o):
        lhs = jnp.concatenate([_silu(c_ref[...]), _silu(cc_ref[...]), jnp.zeros((7, D), F32)], axis=0)
        dmc_blk = _colsum(cxb_ref[...])
        rhs = jnp.concatenate([exb_ref[...], dmc_blk, jnp.zeros((7, ncol), F32)], axis=0)
        gw_o[...] = lax.dot_general(lhs, rhs, _TN_DIMS, precision=HI, preferred_element_type=F32)
        gb_o[...] = _colsum(ex_ref[...]) + _colsum(cx_ref[...])
        ds_o[...] = lax.dot_general(jnp.concatenate([dmc_blk, jnp.zeros((7, ncol), F32)], axis=0), w_ref[...],
                                    _NT_DIMS, precision=HI, preferred_element_type=F32)

    return pl.pallas_call(
        kern, name="ada_bwd",
        out_shape=[jax.ShapeDtypeStruct((D, ncol), F32), jax.ShapeDtypeStruct((1, ex.shape[1]), F32),
                   jax.ShapeDtypeStruct((8, D), F32)],
        compiler_params=_cp(None, 48),
    )(c_all, c_ctx, ada_w, ex, cx, ex_blk, cx_blk)


def _cctx_final(parts, c_ctx):
    def kern(p_ref, c_ref, o_ref):
        tot = p_ref[0, 0:1, :]
        for j in range(1, parts.shape[0]):
            tot = tot + p_ref[j, 0:1, :]
        c = c_ref[...]
        sg = _sigmoid(c)
        o_ref[...] = tot * (sg * (1.0 + c * (1.0 - sg)))

    return pl.pallas_call(kern, name="cctx_final", out_shape=jax.ShapeDtypeStruct((1, D), F32))(parts, c_ctx)


def _peer(kind, p, ix, iy, ic):
    if kind == "chips":
        return (p // 2, p % 2, ic)
    if kind == "all":
        return (p // 4, (p // 2) % 2, p % 2)
    return (ix, iy, p)


def _exchange(x, kind, bcast, name, chunks=1):
    npeer = {"chips": 4, "all": 8, "sib": 2}[kind]
    slab = x.shape if bcast else x.shape[1:]
    assert chunks == 1 or slab[0] == chunks

    def kern(x_ref, o_ref, send_sems, recv_sems, lsem):
        ix, iy, ic = lax.axis_index("x"), lax.axis_index("y"), lax.axis_index("c")
        me = {"chips": 2 * ix + iy, "all": 4 * ix + 2 * iy + ic, "sib": ic}[kind]
        own = pltpu.make_async_copy(x_ref if bcast else x_ref.at[me], o_ref.at[me], lsem)
        own.start()

        def part(ref, k):
            return ref if chunks == 1 else ref.at[k]

        def copy(p, k):
            return pltpu.make_async_remote_copy(
                src_ref=part(x_ref if bcast else x_ref.at[p], k), dst_ref=part(o_ref.at[me], k),
                send_sem=send_sems.at[p, k], recv_sem=recv_sems.at[me, k],
                device_id=_peer(kind, p, ix, iy, ic), device_id_type=MESH)

        def arrival(p, k):
            return pltpu.make_async_remote_copy(
                src_ref=part(x_ref if bcast else x_ref.at[p], k), dst_ref=part(o_ref.at[p], k),
                send_sem=send_sems.at[p, k], recv_sem=recv_sems.at[p, k],
                device_id=_peer(kind, p, ix, iy, ic), device_id_type=MESH)

        for p in range(npeer):
            @pl.when(me != p)
            def _():
                for k in range(chunks):
                    copy(p, k).start()
        for p in range(npeer):
            @pl.when(me != p)
            def _():
                for k in range(chunks):
                    arrival(p, k).wait_recv()
        for p in range(npeer):
            @pl.when(me != p)
            def _():
                for k in range(chunks):
                    copy(p, k).wait_send()
        own.wait()

    any_spec = pl.BlockSpec(memory_space=pl.ANY)
    return pl.pallas_call(
        kern, name=name, in_specs=[any_spec], out_specs=any_spec,
        out_shape=jax.ShapeDtypeStruct((npeer,) + tuple(slab), x.dtype),
        scratch_shapes=[pltpu.SemaphoreType.DMA((npeer, chunks)), pltpu.SemaphoreType.DMA((npeer, chunks)),
                        pltpu.SemaphoreType.DMA],
    )(x)


_HBM = pl.BlockSpec(memory_space=pltpu.HBM)
_SEM = pl.BlockSpec(memory_space=pltpu.SEMAPHORE)
_FLOWS = pltpu.SideEffectType.DATAFLOW_SIDE_EFFECTING


def _other_chips(ix, iy, ic):
    return ((1 - ix, iy, ic), (ix, 1 - iy, ic), (1 - ix, 1 - iy, ic))


def _chip_index(dev):
    return 2 * dev[0] + dev[1]


def _gather_start(x, name, scatter=False):
    def kern(x_ref, land_ref, send_sems, recv_sems, x_thru, land_thru, token):
        ix, iy, ic = lax.axis_index("x"), lax.axis_index("y"), lax.axis_index("c")
        me = 2 * ix + iy
        for k, peer in enumerate(_other_chips(ix, iy, ic)):
            src = x_ref.at[_chip_index(peer)] if scatter else x_ref
            pltpu.make_async_remote_copy(src_ref=src, dst_ref=land_ref.at[me], send_sem=send_sems.at[k],
                                         recv_sem=recv_sems.at[k], device_id=peer, device_id_type=MESH).start()
        token[...] = jnp.zeros(token.shape, token.dtype)

    land = lax.empty(x.shape if scatter else (4,) + x.shape, x.dtype)
    return pl.pallas_call(
        kern, name=name,
        out_shape=(pltpu.SemaphoreType.DMA((3,)), pltpu.SemaphoreType.DMA((3,)), pltpu.HBM(x.shape, x.dtype),
                   pltpu.HBM(land.shape, land.dtype), jax.ShapeDtypeStruct((8, 128), F32)),
        in_specs=(_HBM, _HBM), out_specs=(_SEM, _SEM, _HBM, _HBM, pl.BlockSpec(memory_space=pltpu.VMEM)),
        input_output_aliases={0: 2, 1: 3},
        compiler_params=pltpu.CompilerParams(has_side_effects=_FLOWS),
    )(pltpu.with_memory_space_constraint(x, pltpu.HBM), pltpu.with_memory_space_constraint(land, pltpu.HBM))


def _gather_wait(send_sems, recv_sems, x_thru, land_thru, after, name, scatter=False):
    def kern(x_ref, land_ref, send_sems_ref, recv_sems_ref, after_ref, x_dead, land_out):
        ix, iy, ic = lax.axis_index("x"), lax.axis_index("y"), lax.axis_index("c")
        for k, peer in enumerate(_other_chips(ix, iy, ic)):
            src = x_ref.at[_chip_index(peer)] if scatter else x_ref
            copy = pltpu.make_async_remote_copy(src_ref=src, dst_ref=land_ref.at[_chip_index(peer)],
                                                send_sem=send_sems_ref.at[k], recv_sem=recv_sems_ref.at[k],
                                                device_id=peer, device_id_type=MESH)
            copy.wait_send()
            copy.wait_recv()

    return pl.pallas_call(
        kern, name=name,
        out_shape=(pltpu.HBM(x_thru.shape, x_thru.dtype), pltpu.HBM(land_thru.shape, land_thru.dtype)),
        in_specs=(_HBM, _HBM, _SEM, _SEM, pl.BlockSpec(memory_space=pl.ANY)), out_specs=(_HBM, _HBM),
        input_output_aliases={0: 0, 1: 1},
        compiler_params=pltpu.CompilerParams(has_side_effects=_FLOWS),
    )(x_thru, land_thru, send_sems, recv_sems, after)


def _sum_slots(x, name):
    n, R, C = x.shape
    budget = (8 << 20) // (n * C * x.dtype.itemsize)
    tr = max([t for t in range(8, R + 1, 8) if R % t == 0 and t <= max(budget, 8)], default=R)

    def kern(x_ref, o_ref):
        tot = x_ref[0]
        for s in range(1, n):
            tot = tot + x_ref[s]
        o_ref[...] = tot

    return pl.pallas_call(
        kern, grid=(R // tr,), name=name,
        in_specs=[_bs((n, tr, C), lambda i: (0, i, 0))], out_specs=_bs((tr, C), lambda i: (i, 0)),
        out_shape=jax.ShapeDtypeStruct((R, C), x.dtype), compiler_params=_cp(("parallel",)),
    )(x)


def _sib_stream(x, me, name, add, nck=1):
    if add:
        nslab, rows, C = x.shape
        R = rows // (2 * nck)
        assert R * 2 * nck == rows and R % 8 == 0
        K = nslab * nck
    else:
        K, R, C = x.shape

    def kern(me_ref, *refs):
        if add:
            own_ref, send_ref, o_ref, rbuf, ssem, rsem, credit = refs
        else:
            send_ref, o_ref, rbuf, ssem, rsem, credit = refs
        k = pl.program_id(0)
        slot = k % 2
        sib = (lax.axis_index("x"), lax.axis_index("y"), 1 - lax.axis_index("c"))

        @pl.when(k >= 2)
        def _():
            pl.semaphore_wait(credit.at[slot], 1)

        cp = pltpu.make_async_remote_copy(src_ref=send_ref.at[0], dst_ref=rbuf.at[slot], send_sem=ssem.at[slot],
                                          recv_sem=rsem.at[slot], device_id=sib, device_id_type=MESH)
        cp.start()
        cp.wait_recv()
        o_ref[0] = own_ref[0] + rbuf[slot] if add else rbuf[slot]
        cp.wait_send()

        @pl.when(k + 2 < K)
        def _():
            pl.semaphore_signal(credit.at[slot], 1, device_id=sib, device_id_type=MESH)

    if add:
        in_specs = [_bs((1, R, C), lambda k, me_ref: (k // nck, me_ref[0] * nck + k % nck, 0)),
                    _bs((1, R, C), lambda k, me_ref: (k // nck, (1 - me_ref[0]) * nck + k % nck, 0))]
        args = (x, x)
    else:
        in_specs = [_bs((1, R, C), lambda k, me_ref: (k, 0, 0))]
        args = (x,)
    return pl.pallas_call(
        kern, name=name,
        grid_spec=pltpu.PrefetchScalarGridSpec(
            num_scalar_prefetch=1, grid=(K,), in_specs=in_specs,
            out_specs=_bs((1, R, C), lambda k, me_ref: (k, 0, 0)),
            scratch_shapes=[pltpu.VMEM((2, R, C), x.dtype), pltpu.SemaphoreType.DMA((2,)),
                            pltpu.SemaphoreType.DMA((2,)), pltpu.SemaphoreType.REGULAR((2,))]),
        out_shape=jax.ShapeDtypeStruct((K, R, C), x.dtype),
        compiler_params=_cp(("arbitrary",)),
    )(me, *args)


def _adamw(w, g, m, v, name):
    shape = w.shape
    if len(shape) == 1:
        outs = _adamw(*(t.reshape(1, -1) for t in (w, g, m, v)), name)
        return tuple(t.reshape(shape) for t in outs)
    nd = len(shape)
    size = 1
    for s in shape:
        size *= s
    rows = shape[-2]
    tr = rows
    if size > (1 << 18) and all(s == 1 for s in shape[:-2]):
        tr = max(t for t in (256, 128, 64, 32, 16, 8) if rows % t == 0)
    c1 = 1.0 - ADAM_B1 ** ADAM_STEP
    c2 = 1.0 - ADAM_B2 ** ADAM_STEP

    def kern(w_ref, g_ref, m_ref, v_ref, d_o, m_o, v_o):
        gv = g_ref[...]
        mn = ADAM_B1 * m_ref[...] + (1.0 - ADAM_B1) * gv
        vn = ADAM_B2 * v_ref[...] + (1.0 - ADAM_B2) * (gv * gv)
        m_o[...] = mn
        v_o[...] = vn
        d_o[...] = -ADAM_LR * ((mn / c1) / (jnp.sqrt(vn / c2) + ADAM_EPS) + ADAM_WD * w_ref[...])

    spec = _bs(shape[:-2] + (tr, shape[-1]), lambda i: (0,) * (nd - 2) + (i, 0))
    o = jax.ShapeDtypeStruct(shape, F32)
    return tuple(pl.pallas_call(
        kern, grid=(rows // tr,), name=name, in_specs=[spec] * 4, out_specs=[spec] * 3, out_shape=[o, o, o],
        compiler_params=_cp(("parallel",)),
    )(w, g, m, v))


_WEIGHT_NAMES = ("c_ctx", "ada_w", "ada_b", "mix_pre_g", "mix_post_g", "mlp_pre_g", "mlp_post_g", "w_in", "mu_prev",
                 "mu_next", "decay_w0", "decay_w2", "iclr_a0", "iclr_a2", "k_k", "k_a", "r_k", "gate_w2", "lnx_w",
                 "lnx_b", "conv_w", "conv_b", "conv_ln_w", "conv_ln_b", "w_out", "mlp_w1", "mlp_w2")


def _pack_rows(parts, cols=512):
    flat = jnp.concatenate([p.reshape(-1) for p in parts])
    rows = -(-flat.shape[0] // cols)
    rows = -(-rows // 16) * 16
    flat = jnp.pad(flat, (0, rows * cols - flat.shape[0]))
    return flat.reshape(rows, cols)


def _unpack(flat, shapes):
    out = []
    off = 0
    for s in shapes:
        n = 1
        for d in s:
            n *= d
        out.append(flat[off:off + n].reshape(s))
        off += n
    return out


def _local_step(T, x2, c2, tgt, modrows, P, late_weights, early_grads):
    p_rw, p_cv, h = _mix_in(T, x2, c2, modrows, P["mix_pre_g"], P["w_rw"], P["w_cv"])
    prep_params = (P["w0"], P["w2"], P["a0"], P["a2"], P["k_k"], P["k_a"])
    r, v, kk, dec, kd, bb, rw = _rwkv_prep(T, p_rw, P["mu_p"], P["mu_n"], *prep_params)
    y, fin, hist = _scan_fwd(T, r, v, kk, dec, kd, bb)
    P = dict(P, **late_weights(fin))
    ro_params = (P["r_k"], P["gate_w2"], P["lnx_w"], P["lnx_b"])
    cv_params = (P["conv_w"], P["conv_b"], P["conv_ln_w"], P["conv_ln_b"])
    cat, mix, x1, conv = _mix_out(T, y, kd, rw, p_cv, x2, modrows, *ro_params, *cv_params, P["mix_post_g"], P["w_out"])
    m, h2 = _mlp_fwd(T, x1, modrows, P["mlp_pre_g"], P["wl"])
    loss_acc, dm, dx2, dg2, d_mlp_post = _loss_head(T, m, x1, tgt, modrows, P["mlp_post_g"])
    fact, da, dh2 = _mlp_bwd(T, h2, dm, P["wl"])
    dx1, dmod2, d_mlp_pre = _mlp_in_bwd(T, dh2, x1, dx2, modrows, P["mlp_pre_g"])
    dmix, dcat, dg1, d_mix_post = _mix_post_bwd(T, dx1, mix, modrows, P["mix_post_g"], P["w_out"])
    kl = max(t for t in (1024, 512, 256) if T.NLAT % t == 0)
    dw_out = _matmul_tn(cat, dmix, "dw_out", TT, T.NL, 1024, bmap=T.tok)
    dw1 = _matmul_tn(h2, da, "dw_mlp1", kl, T.NLAT // kl, 1024)
    dw2m = _matmul_tn(fact, dm, "dw_mlp2", kl, T.NLAT // kl, 1024)
    fin = fin + early_grads(dw1, dw2m, dw_out)
    dp_cv, d_conv_w, d_conv_b, d_cln_w, d_cln_b = _conv_bwd(T, dcat, p_cv, conv, *cv_params)
    dy, dr_ro, dv_ro, dkbar, dgd, d_r_k, d_gate, d_lnx_w, d_lnx_b = _readout_bwd(T, dcat, y, kd, rw, *ro_params)
    dr_s, ddec, dkd, dv_s, da_s, dbb = _scan_bwd(T, dy, r, v, kk, dec, kd, bb, hist, fin)
    drw, d_w0, d_w2, d_a0, d_a2, d_k_k, d_k_a = _prep_bwd(T, rw, dr_s, ddec, dkd, dv_s, da_s, dbb, dr_ro, dv_ro,
                                                          dkbar, dgd, *prep_params)
    dp_rw, d_mu_p, d_mu_n = _shift_bwd(T, drw, p_rw, P["mu_p"], P["mu_n"])
    dxc, dmod1, d_mix_pre = _mix_in_bwd(T, dp_rw, dp_cv, x2, c2, dx1, modrows, P["mix_pre_g"], P["w_rw"], P["w_cv"])
    kt = max(t for t in (1024, 768, 512, 256) if T.NTOK % t == 0)
    dw_rw = _matmul_tn(h, dp_rw, "dw_in_rw", kt, T.NTOK // kt, 768)
    dw_cv = _matmul_tn(h, dp_cv, "dw_in_cv", kt, T.NTOK // kt, 1024)
    small = dict(mix_pre_g=d_mix_pre, mix_post_g=d_mix_post, mlp_pre_g=d_mlp_pre, mlp_post_g=d_mlp_post,
                 mu_p=d_mu_p, mu_n=d_mu_n, w0=d_w0, w2=d_w2, a0=d_a0, a2=d_a2, k_k=d_k_k, k_a=d_k_a, r_k=d_r_k,
                 gate_w2=d_gate, lnx_w=d_lnx_w, lnx_b=d_lnx_b, conv_w=d_conv_w, conv_b=d_conv_b,
                 conv_ln_w=d_cln_w, conv_ln_b=d_cln_b)
    big = dict(w_rw=dw_rw, w_cv=dw_cv, w_out=dw_out, w1=dw1, w2m=dw2m, after_scan=dr_s)
    dmods = dict(dmod1=dmod1, dg1=dg1, dmod2=dmod2, dg2=dg2)
    return loss_acc[0, 0], dxc, small, big, dmods


_SMALL_ORDER = ("mix_pre_g", "mix_post_g", "mlp_pre_g", "mlp_post_g", "mu_p", "mu_n", "w0", "w2", "a0", "a2", "k_k",
                "k_a", "r_k", "gate_w2", "lnx_w", "lnx_b", "conv_w", "conv_b", "conv_ln_w", "conv_ln_b")


def kernel(x, c, ctx, c_ctx, ada_w, ada_b, mix_pre_g, mix_post_g, mlp_pre_g, mlp_post_g, w_in, mu_prev, mu_next, decay_w0, decay_w2, iclr_a0, iclr_a2, k_k, k_a, r_k, gate_w2, lnx_w, lnx_b, conv_w, conv_b, conv_ln_w, conv_ln_b, w_out, mlp_w1, mlp_w2, loss_target, m_c_ctx, m_ada_w, m_ada_b, m_mix_pre_g, m_mix_post_g, m_mlp_pre_g, m_mlp_post_g, m_w_in, m_mu_prev, m_mu_next, m_decay_w0, m_decay_w2, m_iclr_a0, m_iclr_a2, m_k_k, m_k_a, m_r_k, m_gate_w2, m_lnx_w, m_lnx_b, m_conv_w, m_conv_b, m_conv_ln_w, m_conv_ln_b, m_w_out, m_mlp_w1, m_mlp_w2, v_c_ctx, v_ada_w, v_ada_b, v_mix_pre_g, v_mix_post_g, v_mlp_pre_g, v_mlp_post_g, v_w_in, v_mu_prev, v_mu_next, v_decay_w0, v_decay_w2, v_iclr_a0, v_iclr_a2, v_k_k, v_k_a, v_r_k, v_gate_w2, v_lnx_w, v_lnx_b, v_conv_w, v_conv_b, v_conv_ln_w, v_conv_ln_b, v_w_out, v_mlp_w1, v_mlp_w2):
    weights = dict(zip(_WEIGHT_NAMES, (c_ctx, ada_w, ada_b, mix_pre_g, mix_post_g, mlp_pre_g, mlp_post_g, w_in, mu_prev, mu_next, decay_w0, decay_w2, iclr_a0, iclr_a2, k_k, k_a, r_k, gate_w2, lnx_w, lnx_b, conv_w, conv_b, conv_ln_w, conv_ln_b, w_out, mlp_w1, mlp_w2)))
    moms = dict(zip(_WEIGHT_NAMES, (m_c_ctx, m_ada_w, m_ada_b, m_mix_pre_g, m_mix_post_g, m_mlp_pre_g, m_mlp_post_g, m_w_in, m_mu_prev, m_mu_next, m_decay_w0, m_decay_w2, m_iclr_a0, m_iclr_a2, m_k_k, m_k_a, m_r_k, m_gate_w2, m_lnx_w, m_lnx_b, m_conv_w, m_conv_b, m_conv_ln_w, m_conv_ln_b, m_w_out, m_mlp_w1, m_mlp_w2)))
    vars_ = dict(zip(_WEIGHT_NAMES, (v_c_ctx, v_ada_w, v_ada_b, v_mix_pre_g, v_mix_post_g, v_mlp_pre_g, v_mlp_post_g, v_w_in, v_mu_prev, v_mu_next, v_decay_w0, v_decay_w2, v_iclr_a0, v_iclr_a2, v_k_k, v_k_a, v_r_k, v_gate_w2, v_lnx_w, v_lnx_b, v_conv_w, v_conv_b, v_conv_ln_w, v_conv_ln_b, v_w_out, v_mlp_w1, v_mlp_w2)))

    B, t_lat, _ = x.shape
    assert ctx.shape[1] == TT and t_lat % TT == 0 and (t_lat * B) % MT == 0
    T = _Tiles(B, t_lat)
    ix, iy, ic = lax.axis_index("x"), lax.axis_index("y"), lax.axis_index("c")
    chip = 2 * ix + iy
    dev = 4 * ix + 2 * iy + ic
    nsh = 4
    in_sh = w_in.shape[2]
    ada_sh = ada_w.shape[2]
    lane_sh = decay_w0.shape[2]

    wg_in = _exchange(w_in[0].astype(BF16), "chips", True, "gather_w_in")
    w_in_f = jnp.concatenate([wg_in[j] for j in range(nsh)], axis=1)
    w_in_p = _pad_cols(w_in_f, w_in_f.shape[1])
    late_pack = jnp.concatenate([mlp_w1[0], mlp_w2[0], w_out[0]], axis=0).astype(BF16)
    late_sems_s, late_sems_r, late_x, late_land, late_token = _gather_start(late_pack, "gather_mlp_start")
    n_w1, n_w2 = mlp_w1.shape[1], mlp_w2.shape[1]

    def late_weights(after):
        own, land = _gather_wait(late_sems_s, late_sems_r, late_x, late_land, after, "gather_mlp_wait")
        wl = lax.dynamic_update_slice(land, own[None], (chip, 0, 0))
        return dict(wl=wl, w_out=jnp.concatenate([wl[j, n_w1 + n_w2:] for j in range(nsh)], axis=0))

    sm_parts = (decay_w0[0], decay_w2[0], iclr_a0[0], iclr_a2[0], gate_w2[0], conv_w[0])
    sm_shapes = [p.shape for p in sm_parts]
    sg = _exchange(_pack_rows(sm_parts), "chips", True, "gather_small_weights")
    pers = [_unpack(sg[j].reshape(-1), sm_shapes) for j in range(nsh)]
    w0_f, w2_f_, a0_f, a2_f, gate_f, convw_f = (jnp.concatenate([pers[j][t] for j in range(nsh)], axis=-1)
                                                for t in range(6))

    def pad_rows(a, n):
        return jnp.pad(a, [(0, 0)] * (a.ndim - 2) + [(0, n - a.shape[-2]), (0, 0)])

    P = dict(
        w_rw=w_in_p[:, :RWC], w_cv=w_in_p[:, RWC:],
        mix_pre_g=mix_pre_g, mix_post_g=mix_post_g, mlp_pre_g=mlp_pre_g, mlp_post_g=mlp_post_g,
        mu_p=_pad_cols(mu_prev, mu_prev.shape[1]), mu_n=_pad_cols(mu_next, mu_next.shape[1]),
        w0=w0_f, w2=pad_rows(w2_f_, LRW), a0=a0_f, a2=pad_rows(a2_f, LRW), k_k=k_k, k_a=k_a,
        r_k=r_k.reshape(1, W), gate_w2=pad_rows(gate_f, GDW), lnx_w=lnx_w, lnx_b=lnx_b,
        conv_w=pad_rows(convw_f, 32), conv_b=conv_b, conv_ln_w=conv_ln_w, conv_ln_b=conv_ln_b)

    c_ctx2 = c_ctx.reshape(1, D)
    c_all = _exchange(jnp.pad(c, ((0, 8 - B), (0, 0))), "all", True, "gather_c")[:, :B].reshape(8 * B, D)
    ada_b_blk = lax.dynamic_slice(ada_b, (0, chip * ada_sh), (1, ada_sh))
    mod_blk = _ada_fwd(c_all, c_ctx2, ada_w[0], ada_b_blk)
    mod_g = _exchange(mod_blk, "chips", True, "gather_mod")
    mod_all = jnp.concatenate([mod_g[j] for j in range(nsh)], axis=1)
    mod_x = lax.dynamic_slice(mod_all, (dev * B, 0), (B, 6 * D)).reshape(B, 6, D)
    mod_c = jnp.broadcast_to(mod_all[8 * B].reshape(1, 6, D), (B, 6, D))
    modrows = jnp.stack([mod_c, mod_x], axis=1) + late_token[0, 0]

    x2 = x.reshape(T.NLAT, D)
    c2 = ctx.reshape(B * TT, D)
    tgt = loss_target.reshape(T.NLAT, D)
    me1 = ic.reshape(1).astype(jnp.int32)
    early = {}

    def early_grads(dw1, dw2, dw_out):
        c_1, n_o = mlp_w1.shape[2], w_out.shape[1]
        slabs = jnp.stack([jnp.concatenate([dw1[:, c_1 * j:c_1 * (j + 1)], dw2[n_w2 * j:n_w2 * (j + 1)],
                                            dw_out[n_o * j:n_o * (j + 1)]], axis=0) for j in range(nsh)])
        pair = _sib_stream(slabs, me1, "sib_reduce_mlp_grads", True, nck=3)
        send, recv, x_thru, land, token = _gather_start(pair.reshape(nsh, -1, slabs.shape[2]), "reduce_mlp_start",
                                                        scatter=True)
        early.update(send=send, recv=recv, x=x_thru, land=land)
        return token[0, 0]

    loss_loc, dxl, small, big, dm_ = _local_step(T, x2, c2, tgt, modrows, P, late_weights, early_grads)
    loss = lax.psum(loss_loc, ("x", "y", "c"))
    grad_x = dxl.reshape(x.shape)

    dmod_x = jnp.concatenate([dm_["dmod1"][:, 1], dm_["dg1"], dm_["dmod2"], dm_["dg2"]], axis=1)
    dmod_c = jnp.concatenate([dm_["dmod1"][:, 0], jnp.zeros((B, 4, D), F32)], axis=1)
    dpack = jnp.concatenate([dmod_x.reshape(B, 6 * D), dmod_c.reshape(B, 6 * D)], axis=0)
    dg = _exchange(dpack, "all", True, "gather_dmod")
    ex = dg[:, :B].reshape(8 * B, 6 * D)
    cx = dg[:, B:].reshape(8 * B, 6 * D)
    ex_blk = lax.dynamic_slice(ex, (0, chip * ada_sh), (8 * B, ada_sh))
    cx_blk = lax.dynamic_slice(cx, (0, chip * ada_sh), (8 * B, ada_sh))
    g_ada_w, g_ada_b, dscc = _ada_bwd(c_all, c_ctx2, ada_w[0], ex, cx, ex_blk, cx_blk)
    dscc_g = _exchange(dscc, "chips", True, "gather_dcctx")
    g_c_ctx = _cctx_final(dscc_g, c_ctx2).reshape(D)

    small = dict(small, mu_p=_unpad_cols(small["mu_p"], mu_prev.shape[1]),
                 mu_n=_unpad_cols(small["mu_n"], mu_next.shape[1]),
                 w2=small["w2"][:, :decay_w2.shape[2]], a2=small["a2"][:, :iclr_a2.shape[2]],
                 gate_w2=small["gate_w2"][:gate_w2.shape[1]], conv_w=small["conv_w"][:KCONV])
    sm_list = [small[n] for n in _SMALL_ORDER]
    sm_shapes2 = [a.shape for a in sm_list]
    sm_pack = _pack_rows(sm_list)
    sm_pair = _sib_stream(jnp.concatenate([sm_pack, sm_pack], axis=0)[None], me1, "sib_small_grads", True)[0]
    sm_tot = _sum_slots(_exchange(sm_pair, "chips", True, "gather_small_grads"), "sum_small_grads")
    S = dict(zip(_SMALL_ORDER, _unpack(sm_tot.reshape(-1), sm_shapes2)))

    def shard_last(a):
        return lax.dynamic_slice_in_dim(a, chip * lane_sh, lane_sh, axis=a.ndim - 1)

    grads = dict(
        c_ctx=g_c_ctx, ada_w=g_ada_w[None], ada_b=g_ada_b,
        mix_pre_g=S["mix_pre_g"], mix_post_g=S["mix_post_g"], mlp_pre_g=S["mlp_pre_g"], mlp_post_g=S["mlp_post_g"],
        mu_prev=S["mu_p"], mu_next=S["mu_n"],
        decay_w0=shard_last(S["w0"])[None], decay_w2=shard_last(S["w2"])[None],
        iclr_a0=shard_last(S["a0"])[None], iclr_a2=shard_last(S["a2"])[None],
        k_k=S["k_k"], k_a=S["k_a"], r_k=S["r_k"].reshape(r_k.shape),
        gate_w2=shard_last(S["gate_w2"])[None], lnx_w=S["lnx_w"], lnx_b=S["lnx_b"],
        conv_w=shard_last(S["conv_w"])[None], conv_b=S["conv_b"], conv_ln_w=S["conv_ln_w"],
        conv_ln_b=S["conv_ln_b"])

    def both_halves(mine_, name, nck):
        chunks = mine_.reshape(nck, mine_.shape[0] // nck, mine_.shape[1])
        other = _sib_stream(chunks, me1, name, False)
        full = jnp.where(ic == 0, jnp.concatenate([chunks, other], axis=0), jnp.concatenate([other, chunks], axis=0))
        return full.reshape(2 * mine_.shape[0], mine_.shape[1])

    own, land = _gather_wait(early["send"], early["recv"], early["x"], early["land"], big["after_scan"],
                             "reduce_mlp_wait", scatter=True)
    land = lax.dynamic_update_slice(land, lax.dynamic_index_in_dim(own, chip, 0, keepdims=True), (chip, 0, 0))
    tot = both_halves(_sum_slots(land, "sum_mlp_grads"), "sib_swap_mlp_grads", 3)
    g_w1, g_w2, g_w_out = tot[:n_w1], tot[n_w1:n_w1 + n_w2], tot[n_w1 + n_w2:]

    dw_in_f = _unpad_cols(jnp.concatenate([big["w_rw"], big["w_cv"]], axis=1), w_in_f.shape[1])
    slabs_in = jnp.stack([dw_in_f[:, in_sh * j:in_sh * (j + 1)] for j in range(nsh)])
    pair_in = _sib_stream(slabs_in, me1, "sib_reduce_w_in", True, nck=2)
    pair_in = pair_in.reshape(nsh, -1, in_sh)
    g_w_in = both_halves(_sum_slots(_exchange(pair_in, "chips", False, "reduce_w_in_grads"), "sum_w_in_grads"),
                         "sib_swap_w_in", 2)
    grads.update(w_in=g_w_in[None], w_out=g_w_out[None], mlp_w1=g_w1[None], mlp_w2=g_w2[None])

    deltas, new_m, new_v = {}, {}, {}
    for n in _WEIGHT_NAMES:
        g = grads[n].reshape(weights[n].shape)
        grads[n] = g
        deltas[n], new_m[n], new_v[n] = _adamw(weights[n], g, moms[n], vars_[n], "adamw_" + n)

    return (loss, grad_x, *[grads[n] for n in _WEIGHT_NAMES], *[deltas[n] for n in _WEIGHT_NAMES],
            *[new_m[n] for n in _WEIGHT_NAMES], *[new_v[n] for n in _WEIGHT_NAMES])
```

```python
import functools

import jax
import jax.numpy as jnp
from jax import lax
from jax.experimental import pallas as pl
from jax.experimental.pallas import tpu as pltpu

F32 = jnp.float32
BF16 = jnp.bfloat16
HI = lax.Precision.HIGHEST

D = 1024
W = 512
HS = 64
RWC = 2304
CVC = 1024
GDW = 256
LRW = 128
DFF = 4096
TT = 256
LINE = 64
KCONV = 31
EPS_RMS = 1e-6
EPS_LN = 1e-5
EPS_GN = 64e-5
SCAN_CH = 128
SCAN_G = 8
SCAN_BSUB = 16

ADAM_LR = 0.001
ADAM_B1 = 0.9
ADAM_B2 = 0.999
ADAM_EPS = 1e-08
ADAM_WD = 0.01
ADAM_STEP = 10

_SEGS = ((0, 1536, 1536), (1536, 64, 128), (1600, 64, 128), (1664, 64, 128), (1728, 64, 128),
         (1792, 160, 256), (1952, 1024, 1024))

MESH = pl.DeviceIdType.MESH


def _bs(shape, imap):
    return pl.BlockSpec(shape, imap)


def _cp(sem=None, mb=48):
    return pltpu.CompilerParams(dimension_semantics=sem, vmem_limit_bytes=mb << 20)


def _pad_cols(a, ncols):
    out = []
    for s, w, pw in _SEGS:
        if s >= ncols:
            break
        piece = a[..., s:s + w]
        if pw > w:
            piece = jnp.pad(piece, [(0, 0)] * (a.ndim - 1) + [(0, pw - w)])
        out.append(piece)
    return jnp.concatenate(out, axis=-1)


def _unpad_cols(a, ncols):
    out = []
    off = 0
    for s, w, pw in _SEGS:
        if s >= ncols:
            break
        out.append(a[..., off:off + w])
        off += pw
    return jnp.concatenate(out, axis=-1)


def _sigmoid(x):
    return 1.0 / (1.0 + jnp.exp(-x))


def _softplus(x):
    return jnp.maximum(x, 0.0) + jnp.log(1.0 + jnp.exp(-jnp.abs(x)))


def _e128(dtype):
    r = lax.broadcasted_iota(jnp.int32, (128, 128), 0) >= HS
    c = lax.broadcasted_iota(jnp.int32, (128, 128), 1) >= HS
    return (r == c).astype(dtype)


def _segsum(x, e):
    hi = x.astype(BF16)
    lo = (x - hi.astype(F32)).astype(BF16)
    return jnp.concatenate(
        [jnp.dot(hi[:, 128 * g:128 * (g + 1)], e, preferred_element_type=F32)
         + jnp.dot(lo[:, 128 * g:128 * (g + 1)], e, preferred_element_type=F32) for g in range(4)], axis=1)


_NT_DIMS = (((1,), (1,)), ((), ()))
_TN_DIMS = (((0,), (0,)), ((), ()))


def _bdot(a, b, dims=None):
    a = a.astype(BF16)
    b = b.astype(BF16)
    if dims is None:
        return jnp.dot(a, b, preferred_element_type=F32)
    return lax.dot_general(a, b, dims, preferred_element_type=F32)


def _colsum(x):
    return jnp.sum(x, axis=0, keepdims=True)


def _rowmean(x):
    return jnp.mean(x, axis=-1, keepdims=True)


def _diag(x, npairs):
    row = lax.broadcasted_iota(jnp.int32, (HS, 128), 0)
    lane = lax.broadcasted_iota(jnp.int32, (HS, 128), 1) & (HS - 1)
    keep = jnp.broadcast_to((lane == row)[None], (npairs, HS, 128))
    return jnp.where(keep, x.reshape(npairs, HS, 128), 0.0).reshape(npairs * HS, 128)


def _segb(x, e):
    return jnp.dot(x.astype(BF16), e, preferred_element_type=F32)


_segb1 = _segb


def _expand(row, npairs):
    return jnp.concatenate([jnp.broadcast_to(row[:, 128 * j:128 * (j + 1)], (HS, 128)) for j in range(npairs)], axis=0)


def _colb(row, npairs, e):
    return _segb1(_diag(_expand(row, npairs), npairs), e)


def _pair_colsum(x, npairs):
    return jnp.concatenate([_colsum(x[HS * j:HS * (j + 1)]) for j in range(npairs)], axis=1)


def _conv_pos():
    return lax.broadcasted_iota(jnp.int32, (TT, W), 0) & (LINE - 1)


def _shifted(u, s, pos):
    if s == 0:
        return u
    sh = pltpu.roll(u, (-s) % TT, 0)
    valid = jnp.logical_and(pos + s >= 0, pos + s < LINE)
    return jnp.where(valid, sh, 0.0)


def _acc(ref, val, first):
    @pl.when(first)
    def _():
        ref[...] = jnp.zeros(ref.shape, ref.dtype)
    ref[...] += val


class _Tiles:
    def __init__(self, B, t_lat):
        self.B = B
        self.NLT = t_lat // TT
        self.TPS = self.NLT + 1
        self.NT = B * self.TPS
        self.NL = B * self.NLT
        self.NTOK = self.NT * TT
        self.NLAT = self.NL * TT
        self.TTOT = self.TPS * TT
        self.BW = B * W

    def b(self, i):
        return i // self.TPS

    def q(self, i):
        return i % self.TPS

    def lat(self, i):
        return (i // self.TPS) * self.NLT + jnp.maximum(i % self.TPS - 1, 0)

    def tok(self, l):
        return (l // self.NLT) * self.TPS + 1 + l % self.NLT

    def mod_spec(self):
        return _bs((1, 1, 6, D), lambda i: (i // self.TPS, jnp.minimum(i % self.TPS, 1), 0, 0))

    def tm_spec(self):
        return _bs((TT, W), lambda i: (i % self.TPS, i // self.TPS))

    def tm2_spec(self):
        return _bs((2, TT, W), lambda i: (0, i % self.TPS, i // self.TPS))


def _row(shape_last):
    return _bs((1, shape_last), lambda i: (0, 0))


def _tok_specs(T):
    return [_bs((TT, D), lambda i: (T.lat(i), 0)), _bs((TT, D), lambda i: (i // T.TPS, 0))]


def _tok_tile(T, x_ref, c_ref):
    is_ctx = (pl.program_id(0) % T.TPS == 0).astype(F32)
    return c_ref[...] * is_ctx + x_ref[...] * (1.0 - is_ctx)


def _mix_in(T, x2, c2, modrows, g, w_rw, w_cv):
    def kern(x_ref, c_ref, mod_ref, g_ref, wr_ref, wc_ref, prw_ref, pcv_ref, h_ref):
        x = _tok_tile(T, x_ref, c_ref)
        s = lax.rsqrt(_rowmean(x * x) + EPS_RMS)
        h = (x * s * g_ref[...]) * (1.0 + mod_ref[0, 0, 1:2, :]) + mod_ref[0, 0, 0:1, :]
        hb = h.astype(BF16)
        h_ref[...] = hb
        prw_ref[...] = jnp.dot(hb, wr_ref[...], preferred_element_type=F32)
        pcv_ref[...] = jnp.dot(hb, wc_ref[...], preferred_element_type=F32)

    return pl.pallas_call(
        kern, grid=(T.NT,), name="mix_in",
        in_specs=_tok_specs(T) + [T.mod_spec(), _row(D),
                                  _bs((D, RWC), lambda i: (0, 0)), _bs((D, CVC), lambda i: (0, 0))],
        out_specs=[_bs((TT, RWC), lambda i: (i, 0)), _bs((TT, CVC), lambda i: (i, 0)), _bs((TT, D), lambda i: (i, 0))],
        out_shape=[jax.ShapeDtypeStruct((T.NTOK, RWC), F32), jax.ShapeDtypeStruct((T.NTOK, CVC), F32),
                   jax.ShapeDtypeStruct((T.NTOK, D), BF16)],
        compiler_params=_cp(("parallel",)),
    )(x2, c2, modrows, g, w_rw, w_cv)


def _halo_specs(T):
    nb8 = T.NTOK // 8
    prev = _bs((8, RWC), lambda i: (jnp.maximum(i * (TT // 8) - 1, 0), 0))
    nxt = _bs((8, RWC), lambda i: (jnp.minimum((i + 1) * (TT // 8), nb8 - 1), 0))
    return prev, nxt


def _halo_masks(T, i):
    q = i % T.TPS
    has_prev = jnp.logical_and(q != 0, q != 1).astype(F32)
    has_next = jnp.logical_and(q != 0, q != T.TPS - 1).astype(F32)
    return has_prev, has_next


def _neighbours(z, prev_row, next_row):
    rowi = lax.broadcasted_iota(jnp.int32, z.shape, 0)
    zprev = jnp.where(rowi == 0, prev_row, pltpu.roll(z, 1, 0))
    znext = jnp.where(rowi == TT - 1, next_row, pltpu.roll(z, TT - 1, 0))
    return zprev, znext


def _prep_math(rw, w0, w2, a0, a2, k_k, k_a, e):
    r = rw[:, 0:512]
    k = rw[:, 512:1024]
    v = rw[:, 1024:1536]
    kr = k * k_k
    ss = _segsum(kr * kr, e)
    rt = jnp.sqrt(ss)
    inv = 1.0 / jnp.maximum(rt, 1e-12)
    kk = kr * inv
    o = dict(r=r, k=k, v=v, kr=kr, rt=rt, inv=inv, kk=kk, th=[], pre=[], ex=[], dec=[], iclr=[], kd=[], bb=[], ad=[])
    for d in (0, 1):
        wd = rw[:, 1536 + LRW * d:1536 + LRW * (d + 1)]
        ad = rw[:, 1792 + LRW * d:1792 + LRW * (d + 1)]
        th = jnp.tanh(wd)
        pre = w0[d] + _bdot(th, w2[d])
        ex = jnp.exp(-_softplus(-pre) - 0.5)
        dec = jnp.exp(-ex)
        iclr = _sigmoid(a0[d] + _bdot(ad, a2[d]))
        o["th"].append(th)
        o["pre"].append(pre)
        o["ex"].append(ex)
        o["dec"].append(dec)
        o["iclr"].append(iclr)
        o["ad"].append(ad)
        o["kd"].append(k * (1.0 + (iclr - 1.0) * k_a))
        o["bb"].append(kk * iclr)
    return o


def _load_prep_params(w0_ref, w2_ref, a0_ref, a2_ref):
    w0 = [w0_ref[0:1, :], w0_ref[1:2, :]]
    a0 = [a0_ref[0:1, :], a0_ref[1:2, :]]
    w2 = [w2_ref[0], w2_ref[1]]
    a2 = [a2_ref[0], a2_ref[1]]
    return w0, w2, a0, a2


def _prep_param_specs():
    return [_bs((2, W), lambda i: (0, 0)), _bs((2, LRW, W), lambda i: (0, 0, 0)),
            _bs((2, W), lambda i: (0, 0)), _bs((2, LRW, W), lambda i: (0, 0, 0)), _row(W), _row(W)]


def _rwkv_prep(T, p_rw, mu_p, mu_n, w0, w2, a0, a2, k_k, k_a):
    def kern(p_ref, pp_ref, pn_ref, mp_ref, mn_ref, w0_ref, w2_ref, a0_ref, a2_ref, kk_ref, ka_ref,
             r_o, v_o, kk_o, dec_o, kd_o, bb_o, rw_o):
        i = pl.program_id(0)
        has_prev, has_next = _halo_masks(T, i)
        z = p_ref[...]
        zprev, znext = _neighbours(z, pp_ref[7:8, :] * has_prev, pn_ref[0:1, :] * has_next)
        rw = z + mp_ref[...] * (zprev - z) + mn_ref[...] * (znext - z)
        rw_o[...] = rw
        w0v, w2v, a0v, a2v = _load_prep_params(w0_ref, w2_ref, a0_ref, a2_ref)
        o = _prep_math(rw, w0v, w2v, a0v, a2v, kk_ref[...], ka_ref[...], _e128(BF16))
        r_o[...] = o["r"]
        v_o[...] = o["v"]
        kk_o[...] = o["kk"]
        for d in (0, 1):
            dec_o[d] = o["dec"][d]
            kd_o[d] = o["kd"][d]
            bb_o[d] = o["bb"][d]

    prev, nxt = _halo_specs(T)
    tm = jax.ShapeDtypeStruct((T.TTOT, T.BW), F32)
    tm2 = jax.ShapeDtypeStruct((2, T.TTOT, T.BW), F32)
    return pl.pallas_call(
        kern, grid=(T.NT,), name="rwkv_prep",
        in_specs=[_bs((TT, RWC), lambda i: (i, 0)), prev, nxt, _row(RWC), _row(RWC)] + _prep_param_specs(),
        out_specs=[T.tm_spec(), T.tm_spec(), T.tm_spec(), T.tm2_spec(), T.tm2_spec(), T.tm2_spec(),
                   _bs((TT, RWC), lambda i: (i, 0))],
        out_shape=[tm, tm, tm, tm2, tm2, tm2, jax.ShapeDtypeStruct((T.NTOK, RWC), F32)],
        compiler_params=_cp(("parallel",)),
    )(p_rw, p_rw, p_rw, mu_p, mu_n, w0, w2, a0, a2, k_k, k_a)


def _scan_fwd(T, r, v, kk, dec, kd, bb):
    NP = T.BW // 128
    R = NP * HS
    NCH = T.TTOT // SCAN_CH
    NCC = TT // SCAN_CH
    G = SCAN_G
    NG = SCAN_CH // G
    NGRP = 4
    assert NG % NGRP == 0

    def tmap(d, i):
        rev = jnp.where(i < NCC, NCC - 1 - i, NCH - 1 - (i - NCC))
        return jnp.where(d == 0, i, rev)

    def kern(r_ref, v_ref, kk_ref, dec_ref, kd_ref, bb_ref, y_ref, fin_ref, hist_ref, ring, sems):
        d = pl.program_id(0)
        i = pl.program_id(1)

        @pl.when(i == 0)
        def _():
            ring[0] = jnp.zeros((R, 128), F32)

        e = _e128(BF16)

        def hist_copy(k):
            grp = k % NGRP
            return pltpu.make_async_copy(ring.at[pl.ds(grp * G, G)],
                                         hist_ref.at[d, pl.ds(i * SCAN_CH + k * G, G)], sems.at[grp])

        def make_body(with_y):
            def body(k, carry):
                @pl.when(k >= NGRP - 1)
                def _():
                    hist_copy(k - (NGRP - 1)).wait()

                base = (k % NGRP) * G
                for u in range(G):
                    t = k * G + u
                    row = jnp.where(d == 0, t, SCAN_CH - 1 - t)
                    s = ring[base + u]
                    sa = _segb(s * _expand(-kk_ref[pl.ds(row, 1), :], NP), e)
                    vc = _colb(v_ref[pl.ds(row, 1), :], NP, e)
                    s = (s * _expand(dec_ref[0, pl.ds(row, 1), :], NP) + sa * _expand(bb_ref[0, pl.ds(row, 1), :], NP)
                         + vc * _expand(kd_ref[0, pl.ds(row, 1), :], NP))
                    ring[(base + u + 1) if u < G - 1 else ((k + 1) % NGRP) * G] = s
                    if with_y:
                        yb = _segb(s * _expand(r_ref[pl.ds(row, 1), :], NP), e)
                        y_ref[0, pl.ds(row, 1), :] = _pair_colsum(_diag(yb, NP), NP)
                hist_copy(k).start()
                return carry
            return body

        @pl.when(i < NCC)
        def _():
            lax.fori_loop(0, NG, make_body(False), 0)

        @pl.when(i >= NCC)
        def _():
            lax.fori_loop(0, NG, make_body(True), 0)

        for k in range(NG - (NGRP - 1), NG):
            hist_copy(k).wait()

        @pl.when(i == NCH - 1)
        def _():
            fin_ref[0] = ring[0]

    sh = _bs((SCAN_CH, T.BW), lambda d, i: (tmap(d, i), 0))
    dr = _bs((1, SCAN_CH, T.BW), lambda d, i: (d, tmap(d, i), 0))
    return pl.pallas_call(
        kern, grid=(2, NCH), name="scan_fwd",
        in_specs=[sh, sh, sh, dr, dr, dr],
        out_specs=[dr, _bs((1, R, 128), lambda d, i: (d, 0, 0)), pl.BlockSpec(memory_space=pl.ANY)],
        out_shape=[jax.ShapeDtypeStruct((2, T.TTOT, T.BW), F32), jax.ShapeDtypeStruct((2, R, 128), F32),
                   jax.ShapeDtypeStruct((2, T.TTOT, R, 128), F32)],
        scratch_shapes=[pltpu.VMEM((NGRP * G, R, 128), F32), pltpu.SemaphoreType.DMA((NGRP,))],
        compiler_params=_cp(("arbitrary", "arbitrary")),
    )(r, v, kk, dec, kd, bb)


def _readout_fwd(y, r, v, gd, kbar, rk, gw, lw, lb, e):
    mu = _segsum(y, e) * (1.0 / HS)
    yc = y - mu
    var = _segsum(yc * yc, e) * (1.0 / HS)
    rstd = lax.rsqrt(var + EPS_GN)
    yhat = yc * rstd
    yn = yhat * lw + lb
    q = _segsum(r * kbar * rk, e)
    sg = _sigmoid(gd)
    gg = _bdot(sg, gw)
    return dict(yhat=yhat, rstd=rstd, yn=yn, q=q, sg=sg, gg=gg, out=(yn + q * v) * gg)


def _conv_fwd(cva, cvb, cw_ref, cb, lw, lb, c=None):
    pos = _conv_pos()
    sgb = _sigmoid(cvb)
    u = cva * sgb
    if c is None:
        c = jnp.zeros_like(u)
        for j in range(KCONV):
            c = c + cw_ref[j:j + 1, :] * _shifted(u, j - KCONV // 2, pos)
        c = c + cb
    mu = _rowmean(c)
    cc = c - mu
    rstd = lax.rsqrt(_rowmean(cc * cc) + EPS_LN)
    chat = cc * rstd
    cn = chat * lw + lb
    scn = _sigmoid(cn)
    return dict(sgb=sgb, u=u, c=c, chat=chat, rstd=rstd, cn=cn, scn=scn, out=cn * scn, pos=pos)


def _mix_out(T, y, kd, rw, p_cv, x2, modrows, rk, gw, lnw, lnb, cw, cb, clw, clb, pg, w_out):
    tk = T.tok

    def kern(y_ref, kd_ref, rw_ref, pcv_ref, x_ref, mod_ref, rk_ref, gw_ref, lw_ref, lb_ref, cw_ref, cb_ref,
             clw_ref, clb_ref, pg_ref, wo_ref, cat_o, mix_o, x1_o, conv_o):
        e = _e128(BF16)
        ro = _readout_fwd(y_ref[0] + y_ref[1], rw_ref[:, 0:512], rw_ref[:, 1024:1536], rw_ref[:, 2048:2304],
                          0.5 * (kd_ref[0] + kd_ref[1]), rk_ref[...], gw_ref[...], lw_ref[...], lb_ref[...], e)
        cv = _conv_fwd(pcv_ref[:, 0:512], pcv_ref[:, 512:1024], cw_ref, cb_ref[...], clw_ref[...], clb_ref[...])
        conv_o[...] = cv["c"]
        catb = jnp.concatenate([ro["out"], cv["out"]], axis=1).astype(BF16)
        cat_o[...] = catb
        mix = jnp.dot(catb, wo_ref[...], preferred_element_type=F32)
        mix_o[...] = mix
        sm = lax.rsqrt(_rowmean(mix * mix) + EPS_RMS)
        x1_o[...] = x_ref[...] + mod_ref[0, 0, 2:3, :] * (mix * sm * pg_ref[...])

    lat = lambda l: (l, 0)
    return pl.pallas_call(
        kern, grid=(T.NL,), name="mix_out",
        in_specs=[_bs((2, TT, W), lambda l: (0, 1 + l % T.NLT, l // T.NLT)),
                  _bs((2, TT, W), lambda l: (0, 1 + l % T.NLT, l // T.NLT)),
                  _bs((TT, RWC), lambda l: (tk(l), 0)), _bs((TT, CVC), lambda l: (tk(l), 0)),
                  _bs((TT, D), lambda l: (l, 0)),
                  _bs((1, 1, 6, D), lambda l: (l // T.NLT, 1, 0, 0)),
                  _row(W), _bs((GDW, W), lambda l: (0, 0)), _row(W), _row(W),
                  _bs((32, W), lambda l: (0, 0)), _row(W), _row(W), _row(W), _row(D),
                  _bs((D, D), lambda l: (0, 0))],
        out_specs=[_bs((TT, D), lat), _bs((TT, D), lat), _bs((TT, D), lat), _bs((TT, W), lat)],
        out_shape=[jax.ShapeDtypeStruct((T.NLAT, D), BF16), jax.ShapeDtypeStruct((T.NLAT, D), F32),
                   jax.ShapeDtypeStruct((T.NLAT, D), F32), jax.ShapeDtypeStruct((T.NLAT, W), F32)],
        compiler_params=_cp(("parallel",)),
    )(y, kd, rw, p_cv, x2, modrows, rk, gw, lnw, lnb, cw, cb, clw, clb, pg, w_out)


MT = 512
FC = 1024


def _late_weight_specs():
    assert FC == D
    return [_bs((1, D, FC), lambda t, f: (f, 0, 0)), _bs((1, FC, D), lambda t, f: (f, 1, 0))]


def _mlp_fwd(T, x1, modrows, g, wl):
    per_b = T.NLT * TT // MT

    def kern(x_ref, mod_ref, g_ref, w1_ref, w2_ref, m_o, h2_o, h2_s):
        f = pl.program_id(1)

        @pl.when(f == 0)
        def _():
            x = x_ref[...]
            s = lax.rsqrt(_rowmean(x * x) + EPS_RMS)
            h2 = (x * s * g_ref[...]) * (1.0 + mod_ref[0, 0, 4:5, :]) + mod_ref[0, 0, 3:4, :]
            h2_s[...] = h2.astype(BF16)
            h2_o[...] = h2.astype(BF16)
            m_o[...] = jnp.zeros_like(m_o)

        a = jnp.dot(h2_s[...], w1_ref[0], preferred_element_type=F32)
        rl = jnp.maximum(a, 0.0)
        m_o[...] += jnp.dot((rl * rl).astype(BF16), w2_ref[0], preferred_element_type=F32)

    tok = lambda t, f: (t, 0)
    return pl.pallas_call(
        kern, grid=(T.NLAT // MT, DFF // FC), name="mlp_fwd",
        in_specs=[_bs((MT, D), tok), _bs((1, 1, 6, D), lambda t, f: (t // per_b, 1, 0, 0)),
                  _bs((1, D), lambda t, f: (0, 0))] + _late_weight_specs(),
        out_specs=[_bs((MT, D), tok), _bs((MT, D), tok)],
        out_shape=[jax.ShapeDtypeStruct((T.NLAT, D), F32), jax.ShapeDtypeStruct((T.NLAT, D), BF16)],
        scratch_shapes=[pltpu.VMEM((MT, D), BF16)],
        compiler_params=_cp(("parallel", "arbitrary")),
    )(x1, modrows, g, wl, wl)


def _loss_head(T, m, x1, tgt, modrows, pg):
    def kern(m_ref, x1_ref, t_ref, mod_ref, pg_ref, loss_o, dm_o, dx2_o, dg2_o, dpg_o):
        l = pl.program_id(0)
        m_ = m_ref[...]
        sm = lax.rsqrt(_rowmean(m_ * m_) + EPS_RMS)
        mn = m_ * sm
        g2 = mod_ref[0, 0, 5:6, :]
        pgv = pg_ref[...]
        diff = x1_ref[...] + g2 * (mn * pgv) - t_ref[...]
        sq = jnp.sum(_colsum(diff * diff), axis=1, keepdims=True)
        _acc(loss_o, jnp.zeros((8, 128), F32) + (0.5 / D) * sq, l == 0)
        dx2 = diff * (1.0 / D)
        dx2_o[...] = dx2
        _acc(dg2_o.at[0], _colsum(dx2 * mn * pgv), l % T.NLT == 0)
        _acc(dpg_o, _colsum(dx2 * g2 * mn), l == 0)
        dmn = dx2 * g2 * pgv
        dm_o[...] = (sm * (dmn - mn * _rowmean(dmn * mn))).astype(BF16)

    lat = lambda l: (l, 0)
    return pl.pallas_call(
        kern, grid=(T.NL,), name="loss_head",
        in_specs=[_bs((TT, D), lat), _bs((TT, D), lat), _bs((TT, D), lat),
                  _bs((1, 1, 6, D), lambda l: (l // T.NLT, 1, 0, 0)), _row(D)],
        out_specs=[_bs((8, 128), lambda l: (0, 0)), _bs((TT, D), lat), _bs((TT, D), lat),
                   _bs((1, 1, D), lambda l: (l // T.NLT, 0, 0)), _row(D)],
        out_shape=[jax.ShapeDtypeStruct((8, 128), F32), jax.ShapeDtypeStruct((T.NLAT, D), BF16),
                   jax.ShapeDtypeStruct((T.NLAT, D), F32), jax.ShapeDtypeStruct((T.B, 1, D), F32),
                   jax.ShapeDtypeStruct((1, D), F32)],
        compiler_params=_cp(("arbitrary",)),
    )(m, x1, tgt, modrows, pg)


def _mlp_bwd(T, h2, dm, wl):
    def kern(h2_ref, dm_ref, w1_ref, w2_ref, f_o, da_o, dh2_o):
        f = pl.program_id(1)
        a = jnp.dot(h2_ref[...], w1_ref[0], preferred_element_type=F32)
        rl = jnp.maximum(a, 0.0)
        f_o[...] = (rl * rl).astype(BF16)
        df = lax.dot_general(dm_ref[...], w2_ref[0], _NT_DIMS, preferred_element_type=F32)
        dab = (df * (2.0 * rl)).astype(BF16)
        da_o[...] = dab
        _acc(dh2_o, lax.dot_general(dab, w1_ref[0], _NT_DIMS, preferred_element_type=F32), f == 0)

    tok = lambda t, f: (t, 0)
    return pl.pallas_call(
        kern, grid=(T.NLAT // MT, DFF // FC), name="mlp_bwd",
        in_specs=[_bs((MT, D), tok), _bs((MT, D), tok)] + _late_weight_specs(),
        out_specs=[_bs((MT, FC), lambda t, f: (t, f)), _bs((MT, FC), lambda t, f: (t, f)), _bs((MT, D), tok)],
        out_shape=[jax.ShapeDtypeStruct((T.NLAT, DFF), BF16), jax.ShapeDtypeStruct((T.NLAT, DFF), BF16),
                   jax.ShapeDtypeStruct((T.NLAT, D), F32)],
        compiler_params=_cp(("parallel", "arbitrary")),
    )(h2, dm, wl, wl)


def _mlp_in_bwd(T, dh2, x1, dx2, modrows, g):
    def kern(dh_ref, x1_ref, dx2_ref, mod_ref, g_ref, dx1_o, dmod_o, dg_o):
        i = pl.program_id(0)
        lat = (i % T.TPS != 0).astype(F32)
        x = x1_ref[...]
        s = lax.rsqrt(_rowmean(x * x) + EPS_RMS)
        xh = x * s
        gv = g_ref[...]
        dh = dh_ref[...] * lat
        n2 = xh * gv
        first_b = i % T.TPS == 0
        _acc(dmod_o.at[0, 0:1, :], _colsum(dh), first_b)
        _acc(dmod_o.at[0, 1:2, :], _colsum(dh * n2), first_b)
        dn2 = dh * (1.0 + mod_ref[0, 0, 4:5, :])
        _acc(dg_o, _colsum(dn2 * xh), i == 0)
        dxh = dn2 * gv
        dx1_o[...] = (dx2_ref[...] + s * (dxh - xh * _rowmean(dxh * xh))) * lat

    lat_i = lambda i: (T.lat(i), 0)
    return pl.pallas_call(
        kern, grid=(T.NT,), name="mlp_in_bwd",
        in_specs=[_bs((TT, D), lat_i), _bs((TT, D), lat_i), _bs((TT, D), lat_i),
                  _bs((1, 1, 6, D), lambda i: (i // T.TPS, 1, 0, 0)), _row(D)],
        out_specs=[_bs((TT, D), lambda i: (i, 0)), _bs((1, 2, D), lambda i: (i // T.TPS, 0, 0)), _row(D)],
        out_shape=[jax.ShapeDtypeStruct((T.NTOK, D), F32), jax.ShapeDtypeStruct((T.B, 2, D), F32),
                   jax.ShapeDtypeStruct((1, D), F32)],
        compiler_params=_cp(("arbitrary",)),
    )(dh2, x1, dx2, modrows, g)


def _mix_post_bwd(T, dx1, mix, modrows, pg, w_out):
    def kern(dx_ref, mix_ref, mod_ref, pg_ref, wo_ref, dmix_o, dcat_o, dg1_o, dpg_o):
        i = pl.program_id(0)
        lat = (i % T.TPS != 0).astype(F32)
        dx = dx_ref[...]
        mix = mix_ref[...]
        sm = lax.rsqrt(_rowmean(mix * mix) + EPS_RMS)
        mh = mix * sm
        g1 = mod_ref[0, 0, 2:3, :]
        pgv = pg_ref[...]
        _acc(dg1_o.at[0], _colsum(dx * mh * pgv), i % T.TPS == 0)
        _acc(dpg_o, _colsum(dx * g1 * mh), i == 0)
        dmh = dx * g1 * pgv
        dmix = ((sm * (dmh - mh * _rowmean(dmh * mh))) * lat).astype(BF16)
        dmix_o[...] = dmix
        dcat_o[...] = lax.dot_general(dmix, wo_ref[...], _NT_DIMS, preferred_element_type=F32)

    tok = lambda i: (i, 0)
    return pl.pallas_call(
        kern, grid=(T.NT,), name="mix_post_bwd",
        in_specs=[_bs((TT, D), tok), _bs((TT, D), lambda i: (T.lat(i), 0)),
                  _bs((1, 1, 6, D), lambda i: (i // T.TPS, 1, 0, 0)), _row(D), _bs((D, D), lambda i: (0, 0))],
        out_specs=[_bs((TT, D), tok), _bs((TT, D), tok), _bs((1, 1, D), lambda i: (i // T.TPS, 0, 0)), _row(D)],
        out_shape=[jax.ShapeDtypeStruct((T.NTOK, D), BF16), jax.ShapeDtypeStruct((T.NTOK, D), F32),
                   jax.ShapeDtypeStruct((T.B, 1, D), F32), jax.ShapeDtypeStruct((1, D), F32)],
        compiler_params=_cp(("arbitrary",)),
    )(dx1, mix, modrows, pg, w_out)


def _conv_bwd(T, dcat, p_cv, conv, cw, cb, clw, clb):
    def kern(dc_ref, pcv_ref, conv_ref, cw_ref, cb_ref, clw_ref, clb_ref, dp_o, dcw_o, dcb_o, dlw_o, dlb_o):
        i = pl.program_id(0)
        is_lat = i % T.TPS != 0

        @pl.when(i == 0)
        def _():
            for ref in (dcw_o, dcb_o, dlw_o, dlb_o):
                ref[...] = jnp.zeros(ref.shape, ref.dtype)

        @pl.when(jnp.logical_not(is_lat))
        def _():
            dp_o[...] = jnp.zeros(dp_o.shape, dp_o.dtype)

        @pl.when(is_lat)
        def _():
            cva = pcv_ref[:, 0:512]
            cv = _conv_fwd(cva, pcv_ref[:, 512:1024], cw_ref, cb_ref[...], clw_ref[...], clb_ref[...],
                           c=conv_ref[...])
            scn = cv["scn"]
            dcn = dc_ref[...] * (scn * (1.0 + cv["cn"] * (1.0 - scn)))
            chat = cv["chat"]
            dlw_o[...] += _colsum(dcn * chat)
            dlb_o[...] += _colsum(dcn)
            dchat = dcn * clw_ref[...]
            dc = cv["rstd"] * (dchat - _rowmean(dchat) - chat * _rowmean(dchat * chat))
            dcb_o[...] += _colsum(dc)
            pos = cv["pos"]
            u = cv["u"]
            du = jnp.zeros_like(u)
            for j in range(KCONV):
                s = j - KCONV // 2
                dcw_o[j:j + 1, :] += _colsum(dc * _shifted(u, s, pos))
                du = du + cw_ref[j:j + 1, :] * _shifted(dc, -s, pos)
            sgb = cv["sgb"]
            dp_o[...] = jnp.concatenate([du * sgb, du * cva * sgb * (1.0 - sgb)], axis=1).astype(BF16)

    return pl.pallas_call(
        kern, grid=(T.NT,), name="conv_bwd",
        in_specs=[_bs((TT, W), lambda i: (i, 1)), _bs((TT, CVC), lambda i: (i, 0)),
                  _bs((TT, W), lambda i: (T.lat(i), 0)),
                  _bs((32, W), lambda i: (0, 0)), _row(W), _row(W), _row(W)],
        out_specs=[_bs((TT, CVC), lambda i: (i, 0)), _bs((32, W), lambda i: (0, 0)), _row(W), _row(W), _row(W)],
        out_shape=[jax.ShapeDtypeStruct((T.NTOK, CVC), BF16), jax.ShapeDtypeStruct((32, W), F32),
                   jax.ShapeDtypeStruct((1, W), F32), jax.ShapeDtypeStruct((1, W), F32),
                   jax.ShapeDtypeStruct((1, W), F32)],
        compiler_params=_cp(("arbitrary",)),
    )(dcat, p_cv, conv, cw, cb, clw, clb)


def _readout_bwd(T, dcat, y, kd, rw, rk, gw, lnw, lnb):
    def kern(dc_ref, y_ref, kd_ref, rw_ref, rk_ref, gw_ref, lw_ref, lb_ref,
             dy_o, dr_o, dv_o, dkb_o, dgd_o, drk_o, dgw_o, dlw_o, dlb_o):
        i = pl.program_id(0)
        is_lat = i % T.TPS != 0

        @pl.when(i == 0)
        def _():
            for ref in (drk_o, dgw_o, dlw_o, dlb_o):
                ref[...] = jnp.zeros(ref.shape, ref.dtype)

        @pl.when(jnp.logical_not(is_lat))
        def _():
            for ref in (dy_o, dr_o, dv_o, dkb_o, dgd_o):
                ref[...] = jnp.zeros(ref.shape, ref.dtype)

        @pl.when(is_lat)
        def _():
            e = _e128(BF16)
            r = rw_ref[:, 0:512]
            v = rw_ref[:, 1024:1536]
            kbar = 0.5 * (kd_ref[0] + kd_ref[1])
            rk = rk_ref[...]
            ro = _readout_fwd(y_ref[0] + y_ref[1], r, v, rw_ref[:, 2048:2304], kbar, rk, gw_ref[...],
                              lw_ref[...], lb_ref[...], e)
            dout = dc_ref[...]
            dgg = dout * (ro["yn"] + ro["q"] * v)
            t1 = dout * ro["gg"]
            yhat = ro["yhat"]
            dlw_o[...] += _colsum(t1 * yhat)
            dlb_o[...] += _colsum(t1)
            dyh = t1 * lw_ref[...]
            dy_o[...] = ro["rstd"] * (dyh - _segsum(dyh, e) * (1.0 / HS) - yhat * (_segsum(dyh * yhat, e) * (1.0 / HS)))
            dq = _segsum(t1 * v, e)
            dv_o[...] = t1 * ro["q"]
            dr_o[...] = dq * kbar * rk
            dkb_o[...] = dq * r * rk
            drk_o[...] += _colsum(dq * r * kbar)
            sg = ro["sg"]
            dsg = _bdot(dgg, gw_ref[...], _NT_DIMS)
            dgd_o[...] = dsg * sg * (1.0 - sg)
            dgw_o[...] += _bdot(sg, dgg, _TN_DIMS)

    tok = lambda i: (i, 0)
    f32s = lambda *s: jax.ShapeDtypeStruct(s, F32)
    y_spec = _bs((2, TT, W), lambda i: (0, jnp.maximum(i % T.TPS, 1), i // T.TPS))
    return pl.pallas_call(
        kern, grid=(T.NT,), name="readout_bwd",
        in_specs=[_bs((TT, W), tok), y_spec, T.tm2_spec(), _bs((TT, RWC), tok),
                  _row(W), _bs((GDW, W), lambda i: (0, 0)), _row(W), _row(W)],
        out_specs=[T.tm_spec(), _bs((TT, W), tok), _bs((TT, W), tok), _bs((TT, W), tok), _bs((TT, GDW), tok),
                   _row(W), _bs((GDW, W), lambda i: (0, 0)), _row(W), _row(W)],
        out_shape=[f32s(T.TTOT, T.BW), f32s(T.NTOK, W), f32s(T.NTOK, W), f32s(T.NTOK, W), f32s(T.NTOK, GDW),
                   f32s(1, W), f32s(GDW, W), f32s(1, W), f32s(1, W)],
        compiler_params=_cp(("arbitrary",)),
    )(dcat, y, kd, rw, rk, gw, lnw, lnb)


def _scan_bwd(T, dy, r, v, kk, dec, kd, bb, hist, fin):
    NP = T.BW // 128
    R = NP * HS
    SB = SCAN_BSUB
    NS = T.TTOT // SB
    NSC = TT // SB

    def tmap(d, g):
        s = NS - 1 - g
        rev = jnp.where(s < NSC, NSC - 1 - s, NS - 1 - (s - NSC))
        return jnp.where(d == 0, s, rev)

    def kern(dy_ref, r_ref, v_ref, kk_ref, dec_ref, kd_ref, bb_ref, h_ref, fin_ref,
             dr_o, dw_o, dk_o, dv_o, da_o, db_o, ds_ref, snext):
        d = pl.program_id(0)
        g = pl.program_id(1)

        @pl.when(g == 0)
        def _():
            ds_ref[...] = jnp.zeros_like(ds_ref)
            snext[...] = fin_ref[0]

        e = _e128(BF16)

        def steps(with_dy):
            for t in range(SB - 1, -1, -1):
                row = jnp.where(d == 0, t, SB - 1 - t)
                sp = h_ref[0, t]
                a_ = _expand(-kk_ref[pl.ds(row, 1), :], NP)
                b_ = _expand(bb_ref[0, pl.ds(row, 1), :], NP)
                k_ = _expand(kd_ref[0, pl.ds(row, 1), :], NP)
                sa = _segb(sp * a_, e)
                vc = _colb(v_ref[pl.ds(row, 1), :], NP, e)
                if with_dy:
                    st = snext[...] if t == SB - 1 else h_ref[0, t + 1]
                    dyc = _colb(dy_ref[pl.ds(row, 1), :], NP, e)
                    ds = ds_ref[...] + dyc * _expand(r_ref[pl.ds(row, 1), :], NP)
                    dr_o[0, pl.ds(row, 1), :] = _pair_colsum(st * dyc, NP)
                else:
                    ds = ds_ref[...]
                    dr_o[0, pl.ds(row, 1), :] = jnp.zeros((1, T.BW), F32)
                dsa = _segb(ds * b_, e)
                ds_ref[...] = ds * _expand(dec_ref[0, pl.ds(row, 1), :], NP) + dsa * a_
                dvb = _segb(ds * k_, e)
                dw_o[0, pl.ds(row, 1), :] = _pair_colsum(ds * sp, NP)
                db_o[0, pl.ds(row, 1), :] = _pair_colsum(ds * sa, NP)
                dv_o[0, pl.ds(row, 1), :] = _pair_colsum(_diag(dvb, NP), NP)
                dk_o[0, pl.ds(row, 1), :] = _pair_colsum(ds * vc, NP)
                da_o[0, pl.ds(row, 1), :] = _pair_colsum(sp * dsa, NP)

        @pl.when(g < NS - NSC)
        def _():
            steps(True)

        @pl.when(g >= NS - NSC)
        def _():
            steps(False)

        snext[...] = h_ref[0, 0]

    sh = _bs((SB, T.BW), lambda d, g: (tmap(d, g), 0))
    dr = _bs((1, SB, T.BW), lambda d, g: (d, tmap(d, g), 0))
    o2 = jax.ShapeDtypeStruct((2, T.TTOT, T.BW), F32)
    return pl.pallas_call(
        kern, grid=(2, NS), name="scan_bwd",
        in_specs=[sh, sh, sh, sh, dr, dr, dr, _bs((1, SB, R, 128), lambda d, g: (d, NS - 1 - g, 0, 0)),
                  _bs((1, R, 128), lambda d, g: (d, 0, 0))],
        out_specs=[dr] * 6,
        out_shape=[o2] * 6,
        scratch_shapes=[pltpu.VMEM((R, 128), F32), pltpu.VMEM((R, 128), F32)],
        compiler_params=_cp(("arbitrary", "arbitrary"), mb=48),
    )(dy, r, v, kk, dec, kd, bb, hist, fin)


def _prep_bwd(T, rw, dr_s, ddec, dkd, dv_s, da_s, dbb, dr_ro, dv_ro, dkbar, dgd, w0, w2, a0, a2, k_k, k_a):
    def kern(rw_ref, drs_ref, ddec_ref, dkd_ref, dvs_ref, das_ref, dbb_ref, drr_ref, dvr_ref, dkb_ref, dgd_ref,
             w0_ref, w2_ref, a0_ref, a2_ref, kk_ref, ka_ref,
             drw_o, dw0_o, dw2_o, da0_o, da2_o, dkk_o, dka_o):
        i = pl.program_id(0)
        first = i == 0
        e = _e128(BF16)
        w0v, w2v, a0v, a2v = _load_prep_params(w0_ref, w2_ref, a0_ref, a2_ref)
        k_k = kk_ref[...]
        k_a = ka_ref[...]
        o = _prep_math(rw_ref[...], w0v, w2v, a0v, a2v, k_k, k_a, e)
        k, kk = o["k"], o["kk"]
        dkbh = 0.5 * dkb_ref[...]
        dk = jnp.zeros_like(k)
        dkk = -(das_ref[0] + das_ref[1])
        dka = jnp.zeros((1, W), F32)
        dwd, dad = [], []
        for d in (0, 1):
            iclr = o["iclr"][d]
            dkd_d = dkd_ref[d] + dkbh
            dbb_d = dbb_ref[d]
            dk = dk + dkd_d * (1.0 + (iclr - 1.0) * k_a)
            dka = dka + _colsum(dkd_d * k * (iclr - 1.0))
            dkk = dkk + dbb_d * iclr
            dicl = dkd_d * k * k_a + dbb_d * kk
            dpa = dicl * iclr * (1.0 - iclr)
            _acc(da0_o.at[d:d + 1, :], _colsum(dpa), first)
            dad.append(_bdot(dpa, a2v[d], _NT_DIMS))
            _acc(da2_o.at[d], _bdot(o["ad"][d], dpa, _TN_DIMS), first)
            dpre = -ddec_ref[d] * o["dec"][d] * o["ex"][d] * _sigmoid(-o["pre"][d])
            _acc(dw0_o.at[d:d + 1, :], _colsum(dpre), first)
            th = o["th"][d]
            dth = _bdot(dpre, w2v[d], _NT_DIMS)
            _acc(dw2_o.at[d], _bdot(th, dpre, _TN_DIMS), first)
            dwd.append(dth * (1.0 - th * th))
        inv = o["inv"]
        kr = o["kr"]
        proj = _segsum(dkk * kr, e)
        dkr = dkk * inv - jnp.where(o["rt"] > 1e-12, kr * inv * inv * inv * proj, 0.0)
        dk = dk + dkr * k_k
        _acc(dkk_o, _colsum(dkr * k), first)
        _acc(dka_o, dka, first)
        dr = drs_ref[0] + drs_ref[1] + drr_ref[...]
        dv = dvs_ref[0] + dvs_ref[1] + dvr_ref[...]
        drw_o[...] = jnp.concatenate([dr, dk, dv, dwd[0], dwd[1], dad[0], dad[1], dgd_ref[...]], axis=1)

    tok = lambda i: (i, 0)
    f32s = lambda *s: jax.ShapeDtypeStruct(s, F32)
    p2 = lambda i: (0, 0)
    p3 = lambda i: (0, 0, 0)
    return pl.pallas_call(
        kern, grid=(T.NT,), name="prep_bwd",
        in_specs=[_bs((TT, RWC), tok)] + [T.tm2_spec()] * 6 + [_bs((TT, W), tok)] * 3 + [_bs((TT, GDW), tok)]
        + _prep_param_specs(),
        out_specs=[_bs((TT, RWC), tok), _bs((2, W), p2), _bs((2, LRW, W), p3), _bs((2, W), p2),
                   _bs((2, LRW, W), p3), _row(W), _row(W)],
        out_shape=[f32s(T.NTOK, RWC), f32s(2, W), f32s(2, LRW, W), f32s(2, W), f32s(2, LRW, W), f32s(1, W), f32s(1, W)],
        compiler_params=_cp(("arbitrary",), mb=56),
    )(rw, dr_s, ddec, dkd, dv_s, da_s, dbb, dr_ro, dv_ro, dkbar, dgd, w0, w2, a0, a2, k_k, k_a)


def _shift_bwd(T, drw, p_rw, mu_p, mu_n):
    def kern(d_ref, dp_ref, dn_ref, p_ref, pp_ref, pn_ref, mp_ref, mn_ref, dprw_o, dmp_o, dmn_o):
        i = pl.program_id(0)
        first = i == 0
        has_prev, has_next = _halo_masks(T, i)
        mp = mp_ref[...]
        mn = mn_ref[...]
        drw = d_ref[...]
        z = p_ref[...]
        zprev, znext = _neighbours(z, pp_ref[7:8, :] * has_prev, pn_ref[0:1, :] * has_next)
        _acc(dmp_o, _colsum(drw * (zprev - z)), first)
        _acc(dmn_o, _colsum(drw * (znext - z)), first)
        dprev, dnext = _neighbours(drw, dp_ref[7:8, :] * has_prev, dn_ref[0:1, :] * has_next)
        dprw_o[...] = (drw * (1.0 - mp - mn) + mp * dnext + mn * dprev).astype(BF16)

    tok = lambda i: (i, 0)
    prev, nxt = _halo_specs(T)
    f32s = lambda *s: jax.ShapeDtypeStruct(s, F32)
    return pl.pallas_call(
        kern, grid=(T.NT,), name="shift_bwd",
        in_specs=[_bs((TT, RWC), tok), prev, nxt, _bs((TT, RWC), tok), prev, nxt, _row(RWC), _row(RWC)],
        out_specs=[_bs((TT, RWC), tok), _row(RWC), _row(RWC)],
        out_shape=[jax.ShapeDtypeStruct((T.NTOK, RWC), BF16), f32s(1, RWC), f32s(1, RWC)],
        compiler_params=_cp(("arbitrary",), mb=56),
    )(drw, drw, drw, p_rw, p_rw, p_rw, mu_p, mu_n)


def _mix_in_bwd(T, dp_rw, dp_cv, x2, c2, dx1, modrows, g, w_rw, w_cv):
    def kern(drw_ref, dcv_ref, x_ref, c_ref, dx1_ref, mod_ref, g_ref, wr_ref, wc_ref, dxc_o, dmod_o, dg_o):
        i = pl.program_id(0)
        dh = (lax.dot_general(drw_ref[...], wr_ref[...], _NT_DIMS, preferred_element_type=F32)
              + lax.dot_general(dcv_ref[...], wc_ref[...], _NT_DIMS, preferred_element_type=F32))
        x = _tok_tile(T, x_ref, c_ref)
        s = lax.rsqrt(_rowmean(x * x) + EPS_RMS)
        xh = x * s
        gv = g_ref[...]
        q = i % T.TPS
        first_kind = jnp.logical_or(q == 0, q == 1)
        _acc(dmod_o.at[0, 0, 0:1, :], _colsum(dh), first_kind)
        _acc(dmod_o.at[0, 0, 1:2, :], _colsum(dh * (xh * gv)), first_kind)
        dn1 = dh * (1.0 + mod_ref[0, 0, 1:2, :])
        _acc(dg_o, _colsum(dn1 * xh), i == 0)
        dxh = dn1 * gv
        dxc_o[...] = dx1_ref[...] + s * (dxh - xh * _rowmean(dxh * xh))

    tok = lambda i: (i, 0)
    f32s = lambda *s: jax.ShapeDtypeStruct(s, F32)
    return pl.pallas_call(
        kern, grid=(T.NT,), name="mix_in_bwd",
        in_specs=[_bs((TT, RWC), tok), _bs((TT, CVC), tok)] + _tok_specs(T) + [
            _bs((TT, D), tok), T.mod_spec(), _row(D), _bs((D, RWC), lambda i: (0, 0)), _bs((D, CVC), lambda i: (0, 0))],
        out_specs=[_bs((TT, D), lambda i: (T.lat(i), 0)),
                   _bs((1, 1, 2, D), lambda i: (i // T.TPS, jnp.minimum(i % T.TPS, 1), 0, 0)), _row(D)],
        out_shape=[f32s(T.NLAT, D), f32s(T.B, 2, 2, D), f32s(1, D)],
        compiler_params=_cp(("arbitrary",)),
    )(dp_rw, dp_cv, x2, c2, dx1, modrows, g, w_rw, w_cv)


def _matmul_tn(a, b, name, tk, nk, tn, amap=None, bmap=None, tm=1024):
    M = a.shape[1]
    N = b.shape[1]
    amap = amap or (lambda k: k)
    bmap = bmap or (lambda k: k)

    def kern(a_ref, b_ref, o_ref):
        _acc(o_ref, lax.dot_general(a_ref[...], b_ref[...], _TN_DIMS, preferred_element_type=F32),
             pl.program_id(2) == 0)

    return pl.pallas_call(
        kern, grid=(M // tm, N // tn, nk), name=name,
        in_specs=[_bs((tk, tm), lambda i, j, k: (amap(k), i)), _bs((tk, tn), lambda i, j, k: (bmap(k), j))],
        out_specs=_bs((tm, tn), lambda i, j, k: (i, j)),
        out_shape=jax.ShapeDtypeStruct((M, N), F32),
        compiler_params=_cp(("parallel", "parallel", "arbitrary")),
    )(a, b)


def _silu(x):
    return x * _sigmoid(x)


def _ada_fwd(c_all, c_ctx, ada_w, ada_b_blk):
    nb = c_all.shape[0]
    R = nb + 8
    ncol = ada_w.shape[1]

    def kern(c_ref, cc_ref, w_ref, b_ref, o_ref):
        lhs = jnp.concatenate([_silu(c_ref[...]), _silu(cc_ref[...]), jnp.zeros((7, D), F32)], axis=0)
        o_ref[...] = jnp.dot(lhs, w_ref[...], precision=HI, preferred_element_type=F32) + b_ref[...]

    return pl.pallas_call(
        kern, name="ada_fwd", out_shape=jax.ShapeDtypeStruct((R, ncol), F32),
        compiler_params=_cp(None, 40),
    )(c_all, c_ctx, ada_w, ada_b_blk)


def _ada_bwd(c_all, c_ctx, ada_w, ex, cx, ex_blk, cx_blk):
    nb = c_all.shape[0]
    ncol = ada_w.shape[1]

    def kern(c_ref, cc_ref, w_ref, ex_ref, cx_ref, exb_ref, cxb_ref, gw_o, gb_o, ds_o):
        lhs = jnp.concatenate([_silu(c_ref[...]), _silu(cc_ref[...]), jnp.zeros((7, D), F32)], axis=0)
        dmc_blk = _colsum(cxb_ref[...])
        rhs = jnp.concatenate([exb_ref[...], dmc_blk, jnp.zeros((7, ncol), F32)], axis=0)
        gw_o[...] = lax.dot_general(lhs, rhs, _TN_DIMS, precision=HI, preferred_element_type=F32)
        gb_o[...] = _colsum(ex_ref[...]) + _colsum(cx_ref[...])
        ds_o[...] = lax.dot_general(jnp.concatenate([dmc_blk, jnp.zeros((7, ncol), F32)], axis=0), w_ref[...],
                                    _NT_DIMS, precision=HI, preferred_element_type=F32)

    return pl.pallas_call(
        kern, name="ada_bwd",
        out_shape=[jax.ShapeDtypeStruct((D, ncol), F32), jax.ShapeDtypeStruct((1, ex.shape[1]), F32),
                   jax.ShapeDtypeStruct((8, D), F32)],
        compiler_params=_cp(None, 48),
    )(c_all, c_ctx, ada_w, ex, cx, ex_blk, cx_blk)


def _cctx_final(parts, c_ctx):
    def kern(p_ref, c_ref, o_ref):
        tot = p_ref[0, 0:1, :]
        for j in range(1, parts.shape[0]):
            tot = tot + p_ref[j, 0:1, :]
        c = c_ref[...]
        sg = _sigmoid(c)
        o_ref[...] = tot * (sg * (1.0 + c * (1.0 - sg)))

    return pl.pallas_call(kern, name="cctx_final", out_shape=jax.ShapeDtypeStruct((1, D), F32))(parts, c_ctx)


def _peer(kind, p, ix, iy, ic):
    if kind == "chips":
        return (p // 2, p % 2, ic)
    if kind == "all":
        return (p // 4, (p // 2) % 2, p % 2)
    return (ix, iy, p)


def _exchange(x, kind, bcast, name, chunks=1):
    npeer = {"chips": 4, "all": 8, "sib": 2}[kind]
    slab = x.shape if bcast else x.shape[1:]
    assert chunks == 1 or slab[0] == chunks

    def kern(x_ref, o_ref, send_sems, recv_sems, lsem):
        ix, iy, ic = lax.axis_index("x"), lax.axis_index("y"), lax.axis_index("c")
        me = {"chips": 2 * ix + iy, "all": 4 * ix + 2 * iy + ic, "sib": ic}[kind]
        own = pltpu.make_async_copy(x_ref if bcast else x_ref.at[me], o_ref.at[me], lsem)
        own.start()

        def part(ref, k):
            return ref if chunks == 1 else ref.at[k]

        def copy(p, k):
            return pltpu.make_async_remote_copy(
                src_ref=part(x_ref if bcast else x_ref.at[p], k), dst_ref=part(o_ref.at[me], k),
                send_sem=send_sems.at[p, k], recv_sem=recv_sems.at[me, k],
                device_id=_peer(kind, p, ix, iy, ic), device_id_type=MESH)

        def arrival(p, k):
            return pltpu.make_async_remote_copy(
                src_ref=part(x_ref if bcast else x_ref.at[p], k), dst_ref=part(o_ref.at[p], k),
                send_sem=send_sems.at[p, k], recv_sem=recv_sems.at[p, k],
                device_id=_peer(kind, p, ix, iy, ic), device_id_type=MESH)

        for p in range(npeer):
            @pl.when(me != p)
            def _():
                for k in range(chunks):
                    copy(p, k).start()
        for p in range(npeer):
            @pl.when(me != p)
            def _():
                for k in range(chunks):
                    arrival(p, k).wait_recv()
        for p in range(npeer):
            @pl.when(me != p)
            def _():
                for k in range(chunks):
                    copy(p, k).wait_send()
        own.wait()

    any_spec = pl.BlockSpec(memory_space=pl.ANY)
    return pl.pallas_call(
        kern, name=name, in_specs=[any_spec], out_specs=any_spec,
        out_shape=jax.ShapeDtypeStruct((npeer,) + tuple(slab), x.dtype),
        scratch_shapes=[pltpu.SemaphoreType.DMA((npeer, chunks)), pltpu.SemaphoreType.DMA((npeer, chunks)),
                        pltpu.SemaphoreType.DMA],
    )(x)


_HBM = pl.BlockSpec(memory_space=pltpu.HBM)
_SEM = pl.BlockSpec(memory_space=pltpu.SEMAPHORE)
_FLOWS = pltpu.SideEffectType.DATAFLOW_SIDE_EFFECTING


def _other_chips(ix, iy, ic):
    return ((1 - ix, iy, ic), (ix, 1 - iy, ic), (1 - ix, 1 - iy, ic))


def _chip_index(dev):
    return 2 * dev[0] + dev[1]


def _gather_start(x, name, scatter=False):
    def kern(x_ref, land_ref, send_sems, recv_sems, x_thru, land_thru, token):
        ix, iy, ic = lax.axis_index("x"), lax.axis_index("y"), lax.axis_index("c")
        me = 2 * ix + iy
        for k, peer in enumerate(_other_chips(ix, iy, ic)):
            src = x_ref.at[_chip_index(peer)] if scatter else x_ref
            pltpu.make_async_remote_copy(src_ref=src, dst_ref=land_ref.at[me], send_sem=send_sems.at[k],
                                         recv_sem=recv_sems.at[k], device_id=peer, device_id_type=MESH).start()
        token[...] = jnp.zeros(token.shape, token.dtype)

    land = lax.empty(x.shape if scatter else (4,) + x.shape, x.dtype)
    return pl.pallas_call(
        kern, name=name,
        out_shape=(pltpu.SemaphoreType.DMA((3,)), pltpu.SemaphoreType.DMA((3,)), pltpu.HBM(x.shape, x.dtype),
                   pltpu.HBM(land.shape, land.dtype), jax.ShapeDtypeStruct((8, 128), F32)),
        in_specs=(_HBM, _HBM), out_specs=(_SEM, _SEM, _HBM, _HBM, pl.BlockSpec(memory_space=pltpu.VMEM)),
        input_output_aliases={0: 2, 1: 3},
        compiler_params=pltpu.CompilerParams(has_side_effects=_FLOWS),
    )(pltpu.with_memory_space_constraint(x, pltpu.HBM), pltpu.with_memory_space_constraint(land, pltpu.HBM))


def _gather_wait(send_sems, recv_sems, x_thru, land_thru, after, name, scatter=False):
    def kern(x_ref, land_ref, send_sems_ref, recv_sems_ref, after_ref, x_dead, land_out):
        ix, iy, ic = lax.axis_index("x"), lax.axis_index("y"), lax.axis_index("c")
        for k, peer in enumerate(_other_chips(ix, iy, ic)):
            src = x_ref.at[_chip_index(peer)] if scatter else x_ref
            copy = pltpu.make_async_remote_copy(src_ref=src, dst_ref=land_ref.at[_chip_index(peer)],
                                                send_sem=send_sems_ref.at[k], recv_sem=recv_sems_ref.at[k],
                                                device_id=peer, device_id_type=MESH)
            copy.wait_send()
            copy.wait_recv()

    return pl.pallas_call(
        kern, name=name,
        out_shape=(pltpu.HBM(x_thru.shape, x_thru.dtype), pltpu.HBM(land_thru.shape, land_thru.dtype)),
        in_specs=(_HBM, _HBM, _SEM, _SEM, pl.BlockSpec(memory_space=pl.ANY)), out_specs=(_HBM, _HBM),
        input_output_aliases={0: 0, 1: 1},
        compiler_params=pltpu.CompilerParams(has_side_effects=_FLOWS),
    )(x_thru, land_thru, send_sems, recv_sems, after)


def _sum_slots(x, name):
    n, R, C = x.shape
    budget = (8 << 20) // (n * C * x.dtype.itemsize)
    tr = max([t for t in range(8, R + 1, 8) if R % t == 0 and t <= max(budget, 8)], default=R)

    def kern(x_ref, o_ref):
        tot = x_ref[0]
        for s in range(1, n):
            tot = tot + x_ref[s]
        o_ref[...] = tot

    return pl.pallas_call(
        kern, grid=(R // tr,), name=name,
        in_specs=[_bs((n, tr, C), lambda i: (0, i, 0))], out_specs=_bs((tr, C), lambda i: (i, 0)),
        out_shape=jax.ShapeDtypeStruct((R, C), x.dtype), compiler_params=_cp(("parallel",)),
    )(x)


def _sib_stream(x, me, name, add, nck=1):
    if add:
        nslab, rows, C = x.shape
        R = rows // (2 * nck)
        assert R * 2 * nck == rows and R % 8 == 0
        K = nslab * nck
    else:
        K, R, C = x.shape

    def kern(me_ref, *refs):
        if add:
            own_ref, send_ref, o_ref, rbuf, ssem, rsem, credit = refs
        else:
            send_ref, o_ref, rbuf, ssem, rsem, credit = refs
        k = pl.program_id(0)
        slot = k % 2
        sib = (lax.axis_index("x"), lax.axis_index("y"), 1 - lax.axis_index("c"))

        @pl.when(k >= 2)
        def _():
            pl.semaphore_wait(credit.at[slot], 1)

        cp = pltpu.make_async_remote_copy(src_ref=send_ref.at[0], dst_ref=rbuf.at[slot], send_sem=ssem.at[slot],
                                          recv_sem=rsem.at[slot], device_id=sib, device_id_type=MESH)
        cp.start()
        cp.wait_recv()
        o_ref[0] = own_ref[0] + rbuf[slot] if add else rbuf[slot]
        cp.wait_send()

        @pl.when(k + 2 < K)
        def _():
            pl.semaphore_signal(credit.at[slot], 1, device_id=sib, device_id_type=MESH)

    if add:
        in_specs = [_bs((1, R, C), lambda k, me_ref: (k // nck, me_ref[0] * nck + k % nck, 0)),
                    _bs((1, R, C), lambda k, me_ref: (k // nck, (1 - me_ref[0]) * nck + k % nck, 0))]
        args = (x, x)
    else:
        in_specs = [_bs((1, R, C), lambda k, me_ref: (k, 0, 0))]
        args = (x,)
    return pl.pallas_call(
        kern, name=name,
        grid_spec=pltpu.PrefetchScalarGridSpec(
            num_scalar_prefetch=1, grid=(K,), in_specs=in_specs,
            out_specs=_bs((1, R, C), lambda k, me_ref: (k, 0, 0)),
            scratch_shapes=[pltpu.VMEM((2, R, C), x.dtype), pltpu.SemaphoreType.DMA((2,)),
                            pltpu.SemaphoreType.DMA((2,)), pltpu.SemaphoreType.REGULAR((2,))]),
        out_shape=jax.ShapeDtypeStruct((K, R, C), x.dtype),
        compiler_params=_cp(("arbitrary",)),
    )(me, *args)


def _adamw(w, g, m, v, name):
    shape = w.shape
    if len(shape) == 1:
        outs = _adamw(*(t.reshape(1, -1) for t in (w, g, m, v)), name)
        return tuple(t.reshape(shape) for t in outs)
    nd = len(shape)
    size = 1
    for s in shape:
        size *= s
    rows = shape[-2]
    tr = rows
    if size > (1 << 18) and all(s == 1 for s in shape[:-2]):
        tr = max(t for t in (256, 128, 64, 32, 16, 8) if rows % t == 0)
    c1 = 1.0 - ADAM_B1 ** ADAM_STEP
    c2 = 1.0 - ADAM_B2 ** ADAM_STEP

    def kern(w_ref, g_ref, m_ref, v_ref, d_o, m_o, v_o):
        gv = g_ref[...]
        mn = ADAM_B1 * m_ref[...] + (1.0 - ADAM_B1) * gv
        vn = ADAM_B2 * v_ref[...] + (1.0 - ADAM_B2) * (gv * gv)
        m_o[...] = mn
        v_o[...] = vn
        d_o[...] = -ADAM_LR * ((mn / c1) / (jnp.sqrt(vn / c2) + ADAM_EPS) + ADAM_WD * w_ref[...])

    spec = _bs(shape[:-2] + (tr, shape[-1]), lambda i: (0,) * (nd - 2) + (i, 0))
    o = jax.ShapeDtypeStruct(shape, F32)
    return tuple(pl.pallas_call(
        kern, grid=(rows // tr,), name=name, in_specs=[spec] * 4, out_specs=[spec] * 3, out_shape=[o, o, o],
        compiler_params=_cp(("parallel",)),
    )(w, g, m, v))


_WEIGHT_NAMES = ("c_ctx", "ada_w", "ada_b", "mix_pre_g", "mix_post_g", "mlp_pre_g", "mlp_post_g", "w_in", "mu_prev",
                 "mu_next", "decay_w0", "decay_w2", "iclr_a0", "iclr_a2", "k_k", "k_a", "r_k", "gate_w2", "lnx_w",
                 "lnx_b", "conv_w", "conv_b", "conv_ln_w", "conv_ln_b", "w_out", "mlp_w1", "mlp_w2")


def _pack_rows(parts, cols=512):
    flat = jnp.concatenate([p.reshape(-1) for p in parts])
    rows = -(-flat.shape[0] // cols)
    rows = -(-rows // 16) * 16
    flat = jnp.pad(flat, (0, rows * cols - flat.shape[0]))
    return flat.reshape(rows, cols)


def _unpack(flat, shapes):
    out = []
    off = 0
    for s in shapes:
        n = 1
        for d in s:
            n *= d
        out.append(flat[off:off + n].reshape(s))
        off += n
    return out


def _local_step(T, x2, c2, tgt, modrows, P, late_weights, early_grads):
    p_rw, p_cv, h = _mix_in(T, x2, c2, modrows, P["mix_pre_g"], P["w_rw"], P["w_cv"])
    prep_params = (P["w0"], P["w2"], P["a0"], P["a2"], P["k_k"], P["k_a"])
    r, v, kk, dec, kd, bb, rw = _rwkv_prep(T, p_rw, P["mu_p"], P["mu_n"], *prep_params)
    y, fin, hist = _scan_fwd(T, r, v, kk, dec, kd, bb)
    P = dict(P, **late_weights(fin))
    ro_params = (P["r_k"], P["gate_w2"], P["lnx_w"], P["lnx_b"])
    cv_params = (P["conv_w"], P["conv_b"], P["conv_ln_w"], P["conv_ln_b"])
    cat, mix, x1, conv = _mix_out(T, y, kd, rw, p_cv, x2, modrows, *ro_params, *cv_params, P["mix_post_g"], P["w_out"])
    m, h2 = _mlp_fwd(T, x1, modrows, P["mlp_pre_g"], P["wl"])
    loss_acc, dm, dx2, dg2, d_mlp_post = _loss_head(T, m, x1, tgt, modrows, P["mlp_post_g"])
    fact, da, dh2 = _mlp_bwd(T, h2, dm, P["wl"])
    dx1, dmod2, d_mlp_pre = _mlp_in_bwd(T, dh2, x1, dx2, modrows, P["mlp_pre_g"])
    dmix, dcat, dg1, d_mix_post = _mix_post_bwd(T, dx1, mix, modrows, P["mix_post_g"], P["w_out"])
    kl = max(t for t in (1024, 512, 256) if T.NLAT % t == 0)
    dw_out = _matmul_tn(cat, dmix, "dw_out", TT, T.NL, 1024, bmap=T.tok)
    dw1 = _matmul_tn(h2, da, "dw_mlp1", kl, T.NLAT // kl, 1024)
    dw2m = _matmul_tn(fact, dm, "dw_mlp2", kl, T.NLAT // kl, 1024)
    fin = fin + early_grads(dw1, dw2m, dw_out)
    dp_cv, d_conv_w, d_conv_b, d_cln_w, d_cln_b = _conv_bwd(T, dcat, p_cv, conv, *cv_params)
    dy, dr_ro, dv_ro, dkbar, dgd, d_r_k, d_gate, d_lnx_w, d_lnx_b = _readout_bwd(T, dcat, y, kd, rw, *ro_params)
    dr_s, ddec, dkd, dv_s, da_s, dbb = _scan_bwd(T, dy, r, v, kk, dec, kd, bb, hist, fin)
    drw, d_w0, d_w2, d_a0, d_a2, d_k_k, d_k_a = _prep_bwd(T, rw, dr_s, ddec, dkd, dv_s, da_s, dbb, dr_ro, dv_ro,
                                                          dkbar, dgd, *prep_params)
    dp_rw, d_mu_p, d_mu_n = _shift_bwd(T, drw, p_rw, P["mu_p"], P["mu_n"])
    dxc, dmod1, d_mix_pre = _mix_in_bwd(T, dp_rw, dp_cv, x2, c2, dx1, modrows, P["mix_pre_g"], P["w_rw"], P["w_cv"])
    kt = max(t for t in (1024, 768, 512, 256) if T.NTOK % t == 0)
    dw_rw = _matmul_tn(h, dp_rw, "dw_in_rw", kt, T.NTOK // kt, 768)
    dw_cv = _matmul_tn(h, dp_cv, "dw_in_cv", kt, T.NTOK // kt, 1024)
    small = dict(mix_pre_g=d_mix_pre, mix_post_g=d_mix_post, mlp_pre_g=d_mlp_pre, mlp_post_g=d_mlp_post,
                 mu_p=d_mu_p, mu_n=d_mu_n, w0=d_w0, w2=d_w2, a0=d_a0, a2=d_a2, k_k=d_k_k, k_a=d_k_a, r_k=d_r_k,
                 gate_w2=d_gate, lnx_w=d_lnx_w, lnx_b=d_lnx_b, conv_w=d_conv_w, conv_b=d_conv_b,
                 conv_ln_w=d_cln_w, conv_ln_b=d_cln_b)
    big = dict(w_rw=dw_rw, w_cv=dw_cv, w_out=dw_out, w1=dw1, w2m=dw2m, after_scan=dr_s)
    dmods = dict(dmod1=dmod1, dg1=dg1, dmod2=dmod2, dg2=dg2)
    return loss_acc[0, 0], dxc, small, big, dmods


_SMALL_ORDER = ("mix_pre_g", "mix_post_g", "mlp_pre_g", "mlp_post_g", "mu_p", "mu_n", "w0", "w2", "a0", "a2", "k_k",
                "k_a", "r_k", "gate_w2", "lnx_w", "lnx_b", "conv_w", "conv_b", "conv_ln_w", "conv_ln_b")


def kernel(x, c, ctx, c_ctx, ada_w, ada_b, mix_pre_g, mix_post_g, mlp_pre_g, mlp_post_g, w_in, mu_prev, mu_next, decay_w0, decay_w2, iclr_a0, iclr_a2, k_k, k_a, r_k, gate_w2, lnx_w, lnx_b, conv_w, conv_b, conv_ln_w, conv_ln_b, w_out, mlp_w1, mlp_w2, loss_target, m_c_ctx, m_ada_w, m_ada_b, m_mix_pre_g, m_mix_post_g, m_mlp_pre_g, m_mlp_post_g, m_w_in, m_mu_prev, m_mu_next, m_decay_w0, m_decay_w2, m_iclr_a0, m_iclr_a2, m_k_k, m_k_a, m_r_k, m_gate_w2, m_lnx_w, m_lnx_b, m_conv_w, m_conv_b, m_conv_ln_w, m_conv_ln_b, m_w_out, m_mlp_w1, m_mlp_w2, v_c_ctx, v_ada_w, v_ada_b, v_mix_pre_g, v_mix_post_g, v_mlp_pre_g, v_mlp_post_g, v_w_in, v_mu_prev, v_mu_next, v_decay_w0, v_decay_w2, v_iclr_a0, v_iclr_a2, v_k_k, v_k_a, v_r_k, v_gate_w2, v_lnx_w, v_lnx_b, v_conv_w, v_conv_b, v_conv_ln_w, v_conv_ln_b, v_w_out, v_mlp_w1, v_mlp_w2):
    weights = dict(zip(_WEIGHT_NAMES, (c_ctx, ada_w, ada_b, mix_pre_g, mix_post_g, mlp_pre_g, mlp_post_g, w_in, mu_prev, mu_next, decay_w0, decay_w2, iclr_a0, iclr_a2, k_k, k_a, r_k, gate_w2, lnx_w, lnx_b, conv_w, conv_b, conv_ln_w, conv_ln_b, w_out, mlp_w1, mlp_w2)))
    moms = dict(zip(_WEIGHT_NAMES, (m_c_ctx, m_ada_w, m_ada_b, m_mix_pre_g, m_mix_post_g, m_mlp_pre_g, m_mlp_post_g, m_w_in, m_mu_prev, m_mu_next, m_decay_w0, m_decay_w2, m_iclr_a0, m_iclr_a2, m_k_k, m_k_a, m_r_k, m_gate_w2, m_lnx_w, m_lnx_b, m_conv_w, m_conv_b, m_conv_ln_w, m_conv_ln_b, m_w_out, m_mlp_w1, m_mlp_w2)))
    vars_ = dict(zip(_WEIGHT_NAMES, (v_c_ctx, v_ada_w, v_ada_b, v_mix_pre_g, v_mix_post_g, v_mlp_pre_g, v_mlp_post_g, v_w_in, v_mu_prev, v_mu_next, v_decay_w0, v_decay_w2, v_iclr_a0, v_iclr_a2, v_k_k, v_k_a, v_r_k, v_gate_w2, v_lnx_w, v_lnx_b, v_conv_w, v_conv_b, v_conv_ln_w, v_conv_ln_b, v_w_out, v_mlp_w1, v_mlp_w2)))

    B, t_lat, _ = x.shape
    assert ctx.shape[1] == TT and t_lat % TT == 0 and (t_lat * B) % MT == 0
    T = _Tiles(B, t_lat)
    ix, iy, ic = lax.axis_index("x"), lax.axis_index("y"), lax.axis_index("c")
    chip = 2 * ix + iy
    dev = 4 * ix + 2 * iy + ic
    nsh = 4
    in_sh = w_in.shape[2]
    ada_sh = ada_w.shape[2]
    lane_sh = decay_w0.shape[2]

    in_sems_s, in_sems_r, in_x, in_land, in_token = _gather_start(w_in[0].astype(BF16), "gather_w_in_start")
    late_pack = jnp.concatenate([mlp_w1[0], mlp_w2[0], w_out[0]], axis=0).astype(BF16)
    late_sems_s, late_sems_r, late_x, late_land, late_token = _gather_start(late_pack, "gather_mlp_start")
    n_w1, n_w2 = mlp_w1.shape[1], mlp_w2.shape[1]

    def late_weights(after):
        own, land = _gather_wait(late_sems_s, late_sems_r, late_x, late_land, after, "gather_mlp_wait")
        wl = lax.dynamic_update_slice(land, own[None], (chip, 0, 0))
        return dict(wl=wl, w_out=jnp.concatenate([wl[j, n_w1 + n_w2:] for j in range(nsh)], axis=0))

    sm_parts = (decay_w0[0], decay_w2[0], iclr_a0[0], iclr_a2[0], gate_w2[0], conv_w[0])
    sm_shapes = [p.shape for p in sm_parts]
    sg = _exchange(_pack_rows(sm_parts), "chips", True, "gather_small_weights")
    pers = [_unpack(sg[j].reshape(-1), sm_shapes) for j in range(nsh)]
    w0_f, w2_f_, a0_f, a2_f, gate_f, convw_f = (jnp.concatenate([pers[j][t] for j in range(nsh)], axis=-1)
                                                for t in range(6))

    def pad_rows(a, n):
        return jnp.pad(a, [(0, 0)] * (a.ndim - 2) + [(0, n - a.shape[-2]), (0, 0)])

    c_ctx2 = c_ctx.reshape(1, D)
    c_all = _exchange(jnp.pad(c + in_token[0, 0], ((0, 8 - B), (0, 0))), "all", True, "gather_c")[:, :B]
    c_all = c_all.reshape(8 * B, D)
    ada_b_blk = lax.dynamic_slice(ada_b, (0, chip * ada_sh), (1, ada_sh))
    mod_blk = _ada_fwd(c_all, c_ctx2, ada_w[0], ada_b_blk)
    mod_g = _exchange(mod_blk, "chips", True, "gather_mod")
    mod_all = jnp.concatenate([mod_g[j] for j in range(nsh)], axis=1)
    mod_x = lax.dynamic_slice(mod_all, (dev * B, 0), (B, 6 * D)).reshape(B, 6, D)
    mod_c = jnp.broadcast_to(mod_all[8 * B].reshape(1, 6, D), (B, 6, D))
    modrows = jnp.stack([mod_c, mod_x], axis=1) + late_token[0, 0]

    own_in, land_in = _gather_wait(in_sems_s, in_sems_r, in_x, in_land, mod_all, "gather_w_in_wait")
    wg_in = lax.dynamic_update_slice(land_in, own_in[None], (chip, 0, 0))
    w_in_f = jnp.concatenate([wg_in[j] for j in range(nsh)], axis=1)
    w_in_p = _pad_cols(w_in_f, w_in_f.shape[1])
    P = dict(
        w_rw=w_in_p[:, :RWC], w_cv=w_in_p[:, RWC:],
        mix_pre_g=mix_pre_g, mix_post_g=mix_post_g, mlp_pre_g=mlp_pre_g, mlp_post_g=mlp_post_g,
        mu_p=_pad_cols(mu_prev, mu_prev.shape[1]), mu_n=_pad_cols(mu_next, mu_next.shape[1]),
        w0=w0_f, w2=pad_rows(w2_f_, LRW), a0=a0_f, a2=pad_rows(a2_f, LRW), k_k=k_k, k_a=k_a,
        r_k=r_k.reshape(1, W), gate_w2=pad_rows(gate_f, GDW), lnx_w=lnx_w, lnx_b=lnx_b,
        conv_w=pad_rows(convw_f, 32), conv_b=conv_b, conv_ln_w=conv_ln_w, conv_ln_b=conv_ln_b)

    x2 = x.reshape(T.NLAT, D)
    c2 = ctx.reshape(B * TT, D)
    tgt = loss_target.reshape(T.NLAT, D)
    me1 = ic.reshape(1).astype(jnp.int32)
    early = {}

    def early_grads(dw1, dw2, dw_out):
        c_1, n_o = mlp_w1.shape[2], w_out.shape[1]
        slabs = jnp.stack([jnp.concatenate([dw1[:, c_1 * j:c_1 * (j + 1)], dw2[n_w2 * j:n_w2 * (j + 1)],
                                            dw_out[n_o * j:n_o * (j + 1)]], axis=0) for j in range(nsh)])
        pair = _sib_stream(slabs, me1, "sib_reduce_mlp_grads", True, nck=3)
        send, recv, x_thru, land, token = _gather_start(pair.reshape(nsh, -1, slabs.shape[2]), "reduce_mlp_start",
                                                        scatter=True)
        early.update(send=send, recv=recv, x=x_thru, land=land)
        return token[0, 0]

    loss_loc, dxl, small, big, dm_ = _local_step(T, x2, c2, tgt, modrows, P, late_weights, early_grads)
    loss = lax.psum(loss_loc, ("x", "y", "c"))
    grad_x = dxl.reshape(x.shape)

    dw_in_f = _unpad_cols(jnp.concatenate([big["w_rw"], big["w_cv"]], axis=1), w_in_f.shape[1])
    slabs_in = jnp.stack([dw_in_f[:, in_sh * j:in_sh * (j + 1)] for j in range(nsh)])
    pair_in = _sib_stream(slabs_in, me1, "sib_reduce_w_in", True, nck=2).reshape(nsh, -1, in_sh)
    win_s, win_r, win_x, win_land, win_token = _gather_start(pair_in, "reduce_w_in_start", scatter=True)

    dmod_x = jnp.concatenate([dm_["dmod1"][:, 1], dm_["dg1"], dm_["dmod2"], dm_["dg2"]], axis=1)
    dmod_c = jnp.concatenate([dm_["dmod1"][:, 0], jnp.zeros((B, 4, D), F32)], axis=1)
    dpack = jnp.concatenate([dmod_x.reshape(B, 6 * D), dmod_c.reshape(B, 6 * D)], axis=0)
    dpack = dpack + win_token[0, 0]
    dg = _exchange(dpack, "all", True, "gather_dmod")
    ex = dg[:, :B].reshape(8 * B, 6 * D)
    cx = dg[:, B:].reshape(8 * B, 6 * D)
    ex_blk = lax.dynamic_slice(ex, (0, chip * ada_sh), (8 * B, ada_sh))
    cx_blk = lax.dynamic_slice(cx, (0, chip * ada_sh), (8 * B, ada_sh))
    g_ada_w, g_ada_b, dscc = _ada_bwd(c_all, c_ctx2, ada_w[0], ex, cx, ex_blk, cx_blk)
    dscc_g = _exchange(dscc, "chips", True, "gather_dcctx")
    g_c_ctx = _cctx_final(dscc_g, c_ctx2).reshape(D)

    small = dict(small, mu_p=_unpad_cols(small["mu_p"], mu_prev.shape[1]),
                 mu_n=_unpad_cols(small["mu_n"], mu_next.shape[1]),
                 w2=small["w2"][:, :decay_w2.shape[2]], a2=small["a2"][:, :iclr_a2.shape[2]],
                 gate_w2=small["gate_w2"][:gate_w2.shape[1]], conv_w=small["conv_w"][:KCONV])
    sm_list = [small[n] for n in _SMALL_ORDER]
    sm_shapes2 = [a.shape for a in sm_list]
    sm_pack = _pack_rows(sm_list) + win_token[0, 0]
    sm_pair = _sib_stream(jnp.concatenate([sm_pack, sm_pack], axis=0)[None], me1, "sib_small_grads", True)[0]
    sm_tot = _sum_slots(_exchange(sm_pair, "chips", True, "gather_small_grads"), "sum_small_grads")
    S = dict(zip(_SMALL_ORDER, _unpack(sm_tot.reshape(-1), sm_shapes2)))

    def shard_last(a):
        return lax.dynamic_slice_in_dim(a, chip * lane_sh, lane_sh, axis=a.ndim - 1)

    grads = dict(
        c_ctx=g_c_ctx, ada_w=g_ada_w[None], ada_b=g_ada_b,
        mix_pre_g=S["mix_pre_g"], mix_post_g=S["mix_post_g"], mlp_pre_g=S["mlp_pre_g"], mlp_post_g=S["mlp_post_g"],
        mu_prev=S["mu_p"], mu_next=S["mu_n"],
        decay_w0=shard_last(S["w0"])[None], decay_w2=shard_last(S["w2"])[None],
        iclr_a0=shard_last(S["a0"])[None], iclr_a2=shard_last(S["a2"])[None],
        k_k=S["k_k"], k_a=S["k_a"], r_k=S["r_k"].reshape(r_k.shape),
        gate_w2=shard_last(S["gate_w2"])[None], lnx_w=S["lnx_w"], lnx_b=S["lnx_b"],
        conv_w=shard_last(S["conv_w"])[None], conv_b=S["conv_b"], conv_ln_w=S["conv_ln_w"],
        conv_ln_b=S["conv_ln_b"])

    def both_halves(mine_, name, nck):
        chunks = mine_.reshape(nck, mine_.shape[0] // nck, mine_.shape[1])
        other = _sib_stream(chunks, me1, name, False)
        full = jnp.where(ic == 0, jnp.concatenate([chunks, other], axis=0), jnp.concatenate([other, chunks], axis=0))
        return full.reshape(2 * mine_.shape[0], mine_.shape[1])

    own, land = _gather_wait(early["send"], early["recv"], early["x"], early["land"], big["after_scan"],
                             "reduce_mlp_wait", scatter=True)
    land = lax.dynamic_update_slice(land, lax.dynamic_index_in_dim(own, chip, 0, keepdims=True), (chip, 0, 0))
    tot = both_halves(_sum_slots(land, "sum_mlp_grads"), "sib_swap_mlp_grads", 3)
    g_w1, g_w2, g_w_out = tot[:n_w1], tot[n_w1:n_w1 + n_w2], tot[n_w1 + n_w2:]

    own, land = _gather_wait(win_s, win_r, win_x, win_land, sm_tot, "reduce_w_in_wait", scatter=True)
    land = lax.dynamic_update_slice(land, lax.dynamic_index_in_dim(own, chip, 0, keepdims=True), (chip, 0, 0))
    g_w_in = both_halves(_sum_slots(land, "sum_w_in_grads"), "sib_swap_w_in", 2)
    grads.update(w_in=g_w_in[None], w_out=g_w_out[None], mlp_w1=g_w1[None], mlp_w2=g_w2[None])

    deltas, new_m, new_v = {}, {}, {}
    for n in _WEIGHT_NAMES:
        g = grads[n].reshape(weights[n].shape)
        grads[n] = g
        deltas[n], new_m[n], new_v[n] = _adamw(weights[n], g, moms[n], vars_[n], "adamw_" + n)

    return (loss, grad_x, *[grads[n] for n in _WEIGHT_NAMES], *[deltas[n] for n in _WEIGHT_NAMES],
            *[new_m[n] for n in _WEIGHT_NAMES], *[new_v[n] for n in _WEIGHT_NAMES])
```

```python
import functools

import jax
import jax.numpy as jnp
from jax import lax
from jax.experimental import pallas as pl
from jax.experimental.pallas import tpu as pltpu

F32 = jnp.float32
BF16 = jnp.bfloat16
HI = lax.Precision.HIGHEST

D = 1024
W = 512
HS = 64
RWC = 2304
CVC = 1024
GDW = 256
LRW = 128
DFF = 4096
TT = 256
LINE = 64
KCONV = 31
EPS_RMS = 1e-6
EPS_LN = 1e-5
EPS_GN = 64e-5
SCAN_CH = 128
SCAN_G = 8
SCAN_BSUB = 16

ADAM_LR = 0.001
ADAM_B1 = 0.9
ADAM_B2 = 0.999
ADAM_EPS = 1e-08
ADAM_WD = 0.01
ADAM_STEP = 10

_SEGS = ((0, 1536, 1536), (1536, 64, 128), (1600, 64, 128), (1664, 64, 128), (1728, 64, 128),
         (1792, 160, 256), (1952, 1024, 1024))

MESH = pl.DeviceIdType.MESH


def _bs(shape, imap):
    return pl.BlockSpec(shape, imap)


def _cp(sem=None, mb=48):
    return pltpu.CompilerParams(dimension_semantics=sem, vmem_limit_bytes=mb << 20)


def _pad_cols(a, ncols):
    out = []
    for s, w, pw in _SEGS:
        if s >= ncols:
            break
        piece = a[..., s:s + w]
        if pw > w:
            piece = jnp.pad(piece, [(0, 0)] * (a.ndim - 1) + [(0, pw - w)])
        out.append(piece)
    return jnp.concatenate(out, axis=-1)


def _unpad_cols(a, ncols):
    out = []
    off = 0
    for s, w, pw in _SEGS:
        if s >= ncols:
            break
        out.append(a[..., off:off + w])
        off += pw
    return jnp.concatenate(out, axis=-1)


def _sigmoid(x):
    return 1.0 / (1.0 + jnp.exp(-x))


def _softplus(x):
    return jnp.maximum(x, 0.0) + jnp.log(1.0 + jnp.exp(-jnp.abs(x)))


def _e128(dtype):
    r = lax.broadcasted_iota(jnp.int32, (128, 128), 0) >= HS
    c = lax.broadcasted_iota(jnp.int32, (128, 128), 1) >= HS
    return (r == c).astype(dtype)


def _segsum(x, e):
    hi = x.astype(BF16)
    lo = (x - hi.astype(F32)).astype(BF16)
    return jnp.concatenate(
        [jnp.dot(hi[:, 128 * g:128 * (g + 1)], e, preferred_element_type=F32)
         + jnp.dot(lo[:, 128 * g:128 * (g + 1)], e, preferred_element_type=F32) for g in range(4)], axis=1)


_NT_DIMS = (((1,), (1,)), ((), ()))
_TN_DIMS = (((0,), (0,)), ((), ()))


def _bdot(a, b, dims=None):
    a = a.astype(BF16)
    b = b.astype(BF16)
    if dims is None:
        return jnp.dot(a, b, preferred_element_type=F32)
    return lax.dot_general(a, b, dims, preferred_element_type=F32)


def _colsum(x):
    return jnp.sum(x, axis=0, keepdims=True)


def _rowmean(x):
    return jnp.mean(x, axis=-1, keepdims=True)


def _diag(x, npairs):
    row = lax.broadcasted_iota(jnp.int32, (HS, 128), 0)
    lane = lax.broadcasted_iota(jnp.int32, (HS, 128), 1) & (HS - 1)
    keep = jnp.broadcast_to((lane == row)[None], (npairs, HS, 128))
    return jnp.where(keep, x.reshape(npairs, HS, 128), 0.0).reshape(npairs * HS, 128)


def _segb(x, e):
    return jnp.dot(x.astype(BF16), e, preferred_element_type=F32)


_segb1 = _segb


def _segp(xb, row, npairs, e):
    return jnp.dot(xb * _expand(row.astype(BF16), npairs), e, preferred_element_type=F32)


def _expand(row, npairs):
    return jnp.concatenate([jnp.broadcast_to(row[:, 128 * j:128 * (j + 1)], (HS, 128)) for j in range(npairs)], axis=0)


def _colb(row, npairs, e):
    return _segb1(_diag(_expand(row, npairs), npairs), e)


def _pair_colsum(x, npairs):
    return jnp.concatenate([_colsum(x[HS * j:HS * (j + 1)]) for j in range(npairs)], axis=1)


def _conv_pos():
    return lax.broadcasted_iota(jnp.int32, (TT, W), 0) & (LINE - 1)


def _shifted(u, s, pos):
    if s == 0:
        return u
    sh = pltpu.roll(u, (-s) % TT, 0)
    valid = jnp.logical_and(pos + s >= 0, pos + s < LINE)
    return jnp.where(valid, sh, 0.0)


def _acc(ref, val, first):
    @pl.when(first)
    def _():
        ref[...] = jnp.zeros(ref.shape, ref.dtype)
    ref[...] += val


class _Tiles:
    def __init__(self, B, t_lat):
        self.B = B
        self.NLT = t_lat // TT
        self.TPS = self.NLT + 1
        self.NT = B * self.TPS
        self.NL = B * self.NLT
        self.NTOK = self.NT * TT
        self.NLAT = self.NL * TT
        self.TTOT = self.TPS * TT
        self.BW = B * W

    def b(self, i):
        return i // self.TPS

    def q(self, i):
        return i % self.TPS

    def lat(self, i):
        return (i // self.TPS) * self.NLT + jnp.maximum(i % self.TPS - 1, 0)

    def tok(self, l):
        return (l // self.NLT) * self.TPS + 1 + l % self.NLT

    def mod_spec(self):
        return _bs((1, 1, 6, D), lambda i: (i // self.TPS, jnp.minimum(i % self.TPS, 1), 0, 0))

    def tm_spec(self):
        return _bs((TT, W), lambda i: (i % self.TPS, i // self.TPS))

    def tm2_spec(self):
        return _bs((2, TT, W), lambda i: (0, i % self.TPS, i // self.TPS))


def _row(shape_last):
    return _bs((1, shape_last), lambda i: (0, 0))


def _tok_specs(T):
    return [_bs((TT, D), lambda i: (T.lat(i), 0)), _bs((TT, D), lambda i: (i // T.TPS, 0))]


def _tok_tile(T, x_ref, c_ref):
    is_ctx = (pl.program_id(0) % T.TPS == 0).astype(F32)
    return c_ref[...] * is_ctx + x_ref[...] * (1.0 - is_ctx)


def _mix_in(T, x2, c2, modrows, g, w_rw, w_cv):
    def kern(x_ref, c_ref, mod_ref, g_ref, wr_ref, wc_ref, prw_ref, pcv_ref, h_ref):
        x = _tok_tile(T, x_ref, c_ref)
        s = lax.rsqrt(_rowmean(x * x) + EPS_RMS)
        h = (x * s * g_ref[...]) * (1.0 + mod_ref[0, 0, 1:2, :]) + mod_ref[0, 0, 0:1, :]
        hb = h.astype(BF16)
        h_ref[...] = hb
        prw_ref[...] = jnp.dot(hb, wr_ref[...], preferred_element_type=F32)
        pcv_ref[...] = jnp.dot(hb, wc_ref[...], preferred_element_type=F32)

    return pl.pallas_call(
        kern, grid=(T.NT,), name="mix_in",
        in_specs=_tok_specs(T) + [T.mod_spec(), _row(D),
                                  _bs((D, RWC), lambda i: (0, 0)), _bs((D, CVC), lambda i: (0, 0))],
        out_specs=[_bs((TT, RWC), lambda i: (i, 0)), _bs((TT, CVC), lambda i: (i, 0)), _bs((TT, D), lambda i: (i, 0))],
        out_shape=[jax.ShapeDtypeStruct((T.NTOK, RWC), F32), jax.ShapeDtypeStruct((T.NTOK, CVC), F32),
                   jax.ShapeDtypeStruct((T.NTOK, D), BF16)],
        compiler_params=_cp(("parallel",)),
    )(x2, c2, modrows, g, w_rw, w_cv)


def _halo_specs(T):
    nb8 = T.NTOK // 8
    prev = _bs((8, RWC), lambda i: (jnp.maximum(i * (TT // 8) - 1, 0), 0))
    nxt = _bs((8, RWC), lambda i: (jnp.minimum((i + 1) * (TT // 8), nb8 - 1), 0))
    return prev, nxt


def _halo_masks(T, i):
    q = i % T.TPS
    has_prev = jnp.logical_and(q != 0, q != 1).astype(F32)
    has_next = jnp.logical_and(q != 0, q != T.TPS - 1).astype(F32)
    return has_prev, has_next


def _neighbours(z, prev_row, next_row):
    rowi = lax.broadcasted_iota(jnp.int32, z.shape, 0)
    zprev = jnp.where(rowi == 0, prev_row, pltpu.roll(z, 1, 0))
    znext = jnp.where(rowi == TT - 1, next_row, pltpu.roll(z, TT - 1, 0))
    return zprev, znext


def _prep_math(rw, w0, w2, a0, a2, k_k, k_a, e):
    r = rw[:, 0:512]
    k = rw[:, 512:1024]
    v = rw[:, 1024:1536]
    kr = k * k_k
    ss = _segsum(kr * kr, e)
    rt = jnp.sqrt(ss)
    inv = 1.0 / jnp.maximum(rt, 1e-12)
    kk = kr * inv
    o = dict(r=r, k=k, v=v, kr=kr, rt=rt, inv=inv, kk=kk, th=[], pre=[], ex=[], dec=[], iclr=[], kd=[], bb=[], ad=[])
    for d in (0, 1):
        wd = rw[:, 1536 + LRW * d:1536 + LRW * (d + 1)]
        ad = rw[:, 1792 + LRW * d:1792 + LRW * (d + 1)]
        th = jnp.tanh(wd)
        pre = w0[d] + _bdot(th, w2[d])
        ex = jnp.exp(-_softplus(-pre) - 0.5)
        dec = jnp.exp(-ex)
        iclr = _sigmoid(a0[d] + _bdot(ad, a2[d]))
        o["th"].append(th)
        o["pre"].append(pre)
        o["ex"].append(ex)
        o["dec"].append(dec)
        o["iclr"].append(iclr)
        o["ad"].append(ad)
        o["kd"].append(k * (1.0 + (iclr - 1.0) * k_a))
        o["bb"].append(kk * iclr)
    return o


def _load_prep_params(w0_ref, w2_ref, a0_ref, a2_ref):
    w0 = [w0_ref[0:1, :], w0_ref[1:2, :]]
    a0 = [a0_ref[0:1, :], a0_ref[1:2, :]]
    w2 = [w2_ref[0], w2_ref[1]]
    a2 = [a2_ref[0], a2_ref[1]]
    return w0, w2, a0, a2


def _prep_param_specs():
    return [_bs((2, W), lambda i: (0, 0)), _bs((2, LRW, W), lambda i: (0, 0, 0)),
            _bs((2, W), lambda i: (0, 0)), _bs((2, LRW, W), lambda i: (0, 0, 0)), _row(W), _row(W)]


def _rwkv_prep(T, p_rw, mu_p, mu_n, w0, w2, a0, a2, k_k, k_a):
    def kern(p_ref, pp_ref, pn_ref, mp_ref, mn_ref, w0_ref, w2_ref, a0_ref, a2_ref, kk_ref, ka_ref,
             r_o, v_o, kk_o, dec_o, kd_o, bb_o, rw_o):
        i = pl.program_id(0)
        has_prev, has_next = _halo_masks(T, i)
        z = p_ref[...]
        zprev, znext = _neighbours(z, pp_ref[7:8, :] * has_prev, pn_ref[0:1, :] * has_next)
        rw = z + mp_ref[...] * (zprev - z) + mn_ref[...] * (znext - z)
        rw_o[...] = rw
        w0v, w2v, a0v, a2v = _load_prep_params(w0_ref, w2_ref, a0_ref, a2_ref)
        o = _prep_math(rw, w0v, w2v, a0v, a2v, kk_ref[...], ka_ref[...], _e128(BF16))
        r_o[...] = o["r"]
        v_o[...] = o["v"]
        kk_o[...] = o["kk"]
        for d in (0, 1):
            dec_o[d] = o["dec"][d]
            kd_o[d] = o["kd"][d]
            bb_o[d] = o["bb"][d]

    prev, nxt = _halo_specs(T)
    tm = jax.ShapeDtypeStruct((T.TTOT, T.BW), F32)
    tm2 = jax.ShapeDtypeStruct((2, T.TTOT, T.BW), F32)
    return pl.pallas_call(
        kern, grid=(T.NT,), name="rwkv_prep",
        in_specs=[_bs((TT, RWC), lambda i: (i, 0)), prev, nxt, _row(RWC), _row(RWC)] + _prep_param_specs(),
        out_specs=[T.tm_spec(), T.tm_spec(), T.tm_spec(), T.tm2_spec(), T.tm2_spec(), T.tm2_spec(),
                   _bs((TT, RWC), lambda i: (i, 0))],
        out_shape=[tm, tm, tm, tm2, tm2, tm2, jax.ShapeDtypeStruct((T.NTOK, RWC), F32)],
        compiler_params=_cp(("parallel",)),
    )(p_rw, p_rw, p_rw, mu_p, mu_n, w0, w2, a0, a2, k_k, k_a)


def _scan_fwd(T, r, v, kk, dec, kd, bb):
    NP = T.BW // 128
    R = NP * HS
    NCH = T.TTOT // SCAN_CH
    NCC = TT // SCAN_CH
    G = SCAN_G
    NG = SCAN_CH // G
    NGRP = 4
    assert NG % NGRP == 0

    def tmap(d, i):
        rev = jnp.where(i < NCC, NCC - 1 - i, NCH - 1 - (i - NCC))
        return jnp.where(d == 0, i, rev)

    def kern(r_ref, v_ref, kk_ref, dec_ref, kd_ref, bb_ref, y_ref, fin_ref, hist_ref, ring, sems):
        d = pl.program_id(0)
        i = pl.program_id(1)

        @pl.when(i == 0)
        def _():
            ring[0] = jnp.zeros((R, 128), F32)

        e = _e128(BF16)

        def hist_copy(k):
            grp = k % NGRP
            return pltpu.make_async_copy(ring.at[pl.ds(grp * G, G)],
                                         hist_ref.at[d, pl.ds(i * SCAN_CH + k * G, G)], sems.at[grp])

        def make_body(with_y):
            def body(k, carry):
                @pl.when(k >= NGRP - 1)
                def _():
                    hist_copy(k - (NGRP - 1)).wait()

                base = (k % NGRP) * G
                s = ring[base]
                sb = s.astype(BF16)
                for u in range(G):
                    t = k * G + u
                    row = jnp.where(d == 0, t, SCAN_CH - 1 - t)
                    sa = _segp(sb, -kk_ref[pl.ds(row, 1), :], NP, e)
                    vc = _colb(v_ref[pl.ds(row, 1), :], NP, e)
                    s = (s * _expand(dec_ref[0, pl.ds(row, 1), :], NP) + sa * _expand(bb_ref[0, pl.ds(row, 1), :], NP)
                         + vc * _expand(kd_ref[0, pl.ds(row, 1), :], NP))
                    ring[(base + u + 1) if u < G - 1 else ((k + 1) % NGRP) * G] = s
                    sb = s.astype(BF16)
                    if with_y:
                        yb = _segp(sb, r_ref[pl.ds(row, 1), :], NP, e)
                        y_ref[0, pl.ds(row, 1), :] = _pair_colsum(_diag(yb, NP), NP)
                hist_copy(k).start()
                return carry
            return body

        @pl.when(i < NCC)
        def _():
            lax.fori_loop(0, NG, make_body(False), 0)

        @pl.when(i >= NCC)
        def _():
            lax.fori_loop(0, NG, make_body(True), 0)

        for k in range(NG - (NGRP - 1), NG):
            hist_copy(k).wait()

        @pl.when(i == NCH - 1)
        def _():
            fin_ref[0] = ring[0]

    sh = _bs((SCAN_CH, T.BW), lambda d, i: (tmap(d, i), 0))
    dr = _bs((1, SCAN_CH, T.BW), lambda d, i: (d, tmap(d, i), 0))
    return pl.pallas_call(
        kern, grid=(2, NCH), name="scan_fwd",
        in_specs=[sh, sh, sh, dr, dr, dr],
        out_specs=[dr, _bs((1, R, 128), lambda d, i: (d, 0, 0)), pl.BlockSpec(memory_space=pl.ANY)],
        out_shape=[jax.ShapeDtypeStruct((2, T.TTOT, T.BW), F32), jax.ShapeDtypeStruct((2, R, 128), F32),
                   jax.ShapeDtypeStruct((2, T.TTOT, R, 128), F32)],
        scratch_shapes=[pltpu.VMEM((NGRP * G, R, 128), F32), pltpu.SemaphoreType.DMA((NGRP,))],
        compiler_params=_cp(("arbitrary", "arbitrary")),
    )(r, v, kk, dec, kd, bb)


def _readout_fwd(y, r, v, gd, kbar, rk, gw, lw, lb, e):
    mu = _segsum(y, e) * (1.0 / HS)
    yc = y - mu
    var = _segsum(yc * yc, e) * (1.0 / HS)
    rstd = lax.rsqrt(var + EPS_GN)
    yhat = yc * rstd
    yn = yhat * lw + lb
    q = _segsum(r * kbar * rk, e)
    sg = _sigmoid(gd)
    gg = _bdot(sg, gw)
    return dict(yhat=yhat, rstd=rstd, yn=yn, q=q, sg=sg, gg=gg, out=(yn + q * v) * gg)


def _conv_fwd(cva, cvb, cw_ref, cb, lw, lb, c=None):
    pos = _conv_pos()
    sgb = _sigmoid(cvb)
    u = cva * sgb
    if c is None:
        c = jnp.zeros_like(u)
        for j in range(KCONV):
            c = c + cw_ref[j:j + 1, :] * _shifted(u, j - KCONV // 2, pos)
        c = c + cb
    mu = _rowmean(c)
    cc = c - mu
    rstd = lax.rsqrt(_rowmean(cc * cc) + EPS_LN)
    chat = cc * rstd
    cn = chat * lw + lb
    scn = _sigmoid(cn)
    return dict(sgb=sgb, u=u, c=c, chat=chat, rstd=rstd, cn=cn, scn=scn, out=cn * scn, pos=pos)


def _mix_out(T, y, kd, rw, p_cv, x2, modrows, rk, gw, lnw, lnb, cw, cb, clw, clb, pg, w_out):
    tk = T.tok

    def kern(y_ref, kd_ref, rw_ref, pcv_ref, x_ref, mod_ref, rk_ref, gw_ref, lw_ref, lb_ref, cw_ref, cb_ref,
             clw_ref, clb_ref, pg_ref, wo_ref, cat_o, mix_o, x1_o, conv_o):
        e = _e128(BF16)
        ro = _readout_fwd(y_ref[0] + y_ref[1], rw_ref[:, 0:512], rw_ref[:, 1024:1536], rw_ref[:, 2048:2304],
                          0.5 * (kd_ref[0] + kd_ref[1]), rk_ref[...], gw_ref[...], lw_ref[...], lb_ref[...], e)
        cv = _conv_fwd(pcv_ref[:, 0:512], pcv_ref[:, 512:1024], cw_ref, cb_ref[...], clw_ref[...], clb_ref[...])
        conv_o[...] = cv["c"]
        catb = jnp.concatenate([ro["out"], cv["out"]], axis=1).astype(BF16)
        cat_o[...] = catb
        mix = jnp.dot(catb, wo_ref[...], preferred_element_type=F32)
        mix_o[...] = mix
        sm = lax.rsqrt(_rowmean(mix * mix) + EPS_RMS)
        x1_o[...] = x_ref[...] + mod_ref[0, 0, 2:3, :] * (mix * sm * pg_ref[...])

    lat = lambda l: (l, 0)
    return pl.pallas_call(
        kern, grid=(T.NL,), name="mix_out",
        in_specs=[_bs((2, TT, W), lambda l: (0, 1 + l % T.NLT, l // T.NLT)),
                  _bs((2, TT, W), lambda l: (0, 1 + l % T.NLT, l // T.NLT)),
                  _bs((TT, RWC), lambda l: (tk(l), 0)), _bs((TT, CVC), lambda l: (tk(l), 0)),
                  _bs((TT, D), lambda l: (l, 0)),
                  _bs((1, 1, 6, D), lambda l: (l // T.NLT, 1, 0, 0)),
                  _row(W), _bs((GDW, W), lambda l: (0, 0)), _row(W), _row(W),
                  _bs((32, W), lambda l: (0, 0)), _row(W), _row(W), _row(W), _row(D),
                  _bs((D, D), lambda l: (0, 0))],
        out_specs=[_bs((TT, D), lat), _bs((TT, D), lat), _bs((TT, D), lat), _bs((TT, W), lat)],
        out_shape=[jax.ShapeDtypeStruct((T.NLAT, D), BF16), jax.ShapeDtypeStruct((T.NLAT, D), F32),
                   jax.ShapeDtypeStruct((T.NLAT, D), F32), jax.ShapeDtypeStruct((T.NLAT, W), F32)],
        compiler_params=_cp(("parallel",)),
    )(y, kd, rw, p_cv, x2, modrows, rk, gw, lnw, lnb, cw, cb, clw, clb, pg, w_out)


MT = 512
FC = 1024


def _late_weight_specs():
    assert FC == D
    return [_bs((1, D, FC), lambda t, f: (f, 0, 0)), _bs((1, FC, D), lambda t, f: (f, 1, 0))]


def _mlp_fwd(T, x1, modrows, g, wl):
    per_b = T.NLT * TT // MT

    def kern(x_ref, mod_ref, g_ref, w1_ref, w2_ref, m_o, h2_o, h2_s):
        f = pl.program_id(1)

        @pl.when(f == 0)
        def _():
            x = x_ref[...]
            s = lax.rsqrt(_rowmean(x * x) + EPS_RMS)
            h2 = (x * s * g_ref[...]) * (1.0 + mod_ref[0, 0, 4:5, :]) + mod_ref[0, 0, 3:4, :]
            h2_s[...] = h2.astype(BF16)
            h2_o[...] = h2.astype(BF16)
            m_o[...] = jnp.zeros_like(m_o)

        a = jnp.dot(h2_s[...], w1_ref[0], preferred_element_type=F32)
        rl = jnp.maximum(a, 0.0)
        m_o[...] += jnp.dot((rl * rl).astype(BF16), w2_ref[0], preferred_element_type=F32)

    tok = lambda t, f: (t, 0)
    return pl.pallas_call(
        kern, grid=(T.NLAT // MT, DFF // FC), name="mlp_fwd",
        in_specs=[_bs((MT, D), tok), _bs((1, 1, 6, D), lambda t, f: (t // per_b, 1, 0, 0)),
                  _bs((1, D), lambda t, f: (0, 0))] + _late_weight_specs(),
        out_specs=[_bs((MT, D), tok), _bs((MT, D), tok)],
        out_shape=[jax.ShapeDtypeStruct((T.NLAT, D), F32), jax.ShapeDtypeStruct((T.NLAT, D), BF16)],
        scratch_shapes=[pltpu.VMEM((MT, D), BF16)],
        compiler_params=_cp(("parallel", "arbitrary")),
    )(x1, modrows, g, wl, wl)


def _loss_head(T, m, x1, tgt, modrows, pg):
    def kern(m_ref, x1_ref, t_ref, mod_ref, pg_ref, loss_o, dm_o, dx2_o, dg2_o, dpg_o):
        l = pl.program_id(0)
        m_ = m_ref[...]
        sm = lax.rsqrt(_rowmean(m_ * m_) + EPS_RMS)
        mn = m_ * sm
        g2 = mod_ref[0, 0, 5:6, :]
        pgv = pg_ref[...]
        diff = x1_ref[...] + g2 * (mn * pgv) - t_ref[...]
        sq = jnp.sum(_colsum(diff * diff), axis=1, keepdims=True)
        _acc(loss_o, jnp.zeros((8, 128), F32) + (0.5 / D) * sq, l == 0)
        dx2 = diff * (1.0 / D)
        dx2_o[...] = dx2
        _acc(dg2_o.at[0], _colsum(dx2 * mn * pgv), l % T.NLT == 0)
        _acc(dpg_o, _colsum(dx2 * g2 * mn), l == 0)
        dmn = dx2 * g2 * pgv
        dm_o[...] = (sm * (dmn - mn * _rowmean(dmn * mn))).astype(BF16)

    lat = lambda l: (l, 0)
    return pl.pallas_call(
        kern, grid=(T.NL,), name="loss_head",
        in_specs=[_bs((TT, D), lat), _bs((TT, D), lat), _bs((TT, D), lat),
                  _bs((1, 1, 6, D), lambda l: (l // T.NLT, 1, 0, 0)), _row(D)],
        out_specs=[_bs((8, 128), lambda l: (0, 0)), _bs((TT, D), lat), _bs((TT, D), lat),
                   _bs((1, 1, D), lambda l: (l // T.NLT, 0, 0)), _row(D)],
        out_shape=[jax.ShapeDtypeStruct((8, 128), F32), jax.ShapeDtypeStruct((T.NLAT, D), BF16),
                   jax.ShapeDtypeStruct((T.NLAT, D), F32), jax.ShapeDtypeStruct((T.B, 1, D), F32),
                   jax.ShapeDtypeStruct((1, D), F32)],
        compiler_params=_cp(("arbitrary",)),
    )(m, x1, tgt, modrows, pg)


def _mlp_bwd(T, h2, dm, wl):
    def kern(h2_ref, dm_ref, w1_ref, w2_ref, f_o, da_o, dh2_o):
        f = pl.program_id(1)
        a = jnp.dot(h2_ref[...], w1_ref[0], preferred_element_type=F32)
        rl = jnp.maximum(a, 0.0)
        f_o[...] = (rl * rl).astype(BF16)
        df = lax.dot_general(dm_ref[...], w2_ref[0], _NT_DIMS, preferred_element_type=F32)
        dab = (df * (2.0 * rl)).astype(BF16)
        da_o[...] = dab
        _acc(dh2_o, lax.dot_general(dab, w1_ref[0], _NT_DIMS, preferred_element_type=F32), f == 0)

    tok = lambda t, f: (t, 0)
    return pl.pallas_call(
        kern, grid=(T.NLAT // MT, DFF // FC), name="mlp_bwd",
        in_specs=[_bs((MT, D), tok), _bs((MT, D), tok)] + _late_weight_specs(),
        out_specs=[_bs((MT, FC), lambda t, f: (t, f)), _bs((MT, FC), lambda t, f: (t, f)), _bs((MT, D), tok)],
        out_shape=[jax.ShapeDtypeStruct((T.NLAT, DFF), BF16), jax.ShapeDtypeStruct((T.NLAT, DFF), BF16),
                   jax.ShapeDtypeStruct((T.NLAT, D), F32)],
        compiler_params=_cp(("parallel", "arbitrary")),
    )(h2, dm, wl, wl)


def _mlp_in_bwd(T, dh2, x1, dx2, modrows, g):
    def kern(dh_ref, x1_ref, dx2_ref, mod_ref, g_ref, dx1_o, dmod_o, dg_o):
        i = pl.program_id(0)
        lat = (i % T.TPS != 0).astype(F32)
        x = x1_ref[...]
        s = lax.rsqrt(_rowmean(x * x) + EPS_RMS)
        xh = x * s
        gv = g_ref[...]
        dh = dh_ref[...] * lat
        n2 = xh * gv
        first_b = i % T.TPS == 0
        _acc(dmod_o.at[0, 0:1, :], _colsum(dh), first_b)
        _acc(dmod_o.at[0, 1:2, :], _colsum(dh * n2), first_b)
        dn2 = dh * (1.0 + mod_ref[0, 0, 4:5, :])
        _acc(dg_o, _colsum(dn2 * xh), i == 0)
        dxh = dn2 * gv
        dx1_o[...] = (dx2_ref[...] + s * (dxh - xh * _rowmean(dxh * xh))) * lat

    lat_i = lambda i: (T.lat(i), 0)
    return pl.pallas_call(
        kern, grid=(T.NT,), name="mlp_in_bwd",
        in_specs=[_bs((TT, D), lat_i), _bs((TT, D), lat_i), _bs((TT, D), lat_i),
                  _bs((1, 1, 6, D), lambda i: (i // T.TPS, 1, 0, 0)), _row(D)],
        out_specs=[_bs((TT, D), lambda i: (i, 0)), _bs((1, 2, D), lambda i: (i // T.TPS, 0, 0)), _row(D)],
        out_shape=[jax.ShapeDtypeStruct((T.NTOK, D), F32), jax.ShapeDtypeStruct((T.B, 2, D), F32),
                   jax.ShapeDtypeStruct((1, D), F32)],
        compiler_params=_cp(("arbitrary",)),
    )(dh2, x1, dx2, modrows, g)


def _mix_post_bwd(T, dx1, mix, modrows, pg, w_out):
    def kern(dx_ref, mix_ref, mod_ref, pg_ref, wo_ref, dmix_o, dcat_o, dg1_o, dpg_o):
        i = pl.program_id(0)
        lat = (i % T.TPS != 0).astype(F32)
        dx = dx_ref[...]
        mix = mix_ref[...]
        sm = lax.rsqrt(_rowmean(mix * mix) + EPS_RMS)
        mh = mix * sm
        g1 = mod_ref[0, 0, 2:3, :]
        pgv = pg_ref[...]
        _acc(dg1_o.at[0], _colsum(dx * mh * pgv), i % T.TPS == 0)
        _acc(dpg_o, _colsum(dx * g1 * mh), i == 0)
        dmh = dx * g1 * pgv
        dmix = ((sm * (dmh - mh * _rowmean(dmh * mh))) * lat).astype(BF16)
        dmix_o[...] = dmix
        dcat_o[...] = lax.dot_general(dmix, wo_ref[...], _NT_DIMS, preferred_element_type=F32)

    tok = lambda i: (i, 0)
    return pl.pallas_call(
        kern, grid=(T.NT,), name="mix_post_bwd",
        in_specs=[_bs((TT, D), tok), _bs((TT, D), lambda i: (T.lat(i), 0)),
                  _bs((1, 1, 6, D), lambda i: (i // T.TPS, 1, 0, 0)), _row(D), _bs((D, D), lambda i: (0, 0))],
        out_specs=[_bs((TT, D), tok), _bs((TT, D), tok), _bs((1, 1, D), lambda i: (i // T.TPS, 0, 0)), _row(D)],
        out_shape=[jax.ShapeDtypeStruct((T.NTOK, D), BF16), jax.ShapeDtypeStruct((T.NTOK, D), F32),
                   jax.ShapeDtypeStruct((T.B, 1, D), F32), jax.ShapeDtypeStruct((1, D), F32)],
        compiler_params=_cp(("arbitrary",)),
    )(dx1, mix, modrows, pg, w_out)


def _conv_bwd(T, dcat, p_cv, conv, cw, cb, clw, clb):
    def kern(dc_ref, pcv_ref, conv_ref, cw_ref, cb_ref, clw_ref, clb_ref, dp_o, dcw_o, dcb_o, dlw_o, dlb_o):
        i = pl.program_id(0)
        is_lat = i % T.TPS != 0

        @pl.when(i == 0)
        def _():
            for ref in (dcw_o, dcb_o, dlw_o, dlb_o):
                ref[...] = jnp.zeros(ref.shape, ref.dtype)

        @pl.when(jnp.logical_not(is_lat))
        def _():
            dp_o[...] = jnp.zeros(dp_o.shape, dp_o.dtype)

        @pl.when(is_lat)
        def _():
            cva = pcv_ref[:, 0:512]
            cv = _conv_fwd(cva, pcv_ref[:, 512:1024], cw_ref, cb_ref[...], clw_ref[...], clb_ref[...],
                           c=conv_ref[...])
            scn = cv["scn"]
            dcn = dc_ref[...] * (scn * (1.0 + cv["cn"] * (1.0 - scn)))
            chat = cv["chat"]
            dlw_o[...] += _colsum(dcn * chat)
            dlb_o[...] += _colsum(dcn)
            dchat = dcn * clw_ref[...]
            dc = cv["rstd"] * (dchat - _rowmean(dchat) - chat * _rowmean(dchat * chat))
            dcb_o[...] += _colsum(dc)
            pos = cv["pos"]
            u = cv["u"]
            du = jnp.zeros_like(u)
            for j in range(KCONV):
                s = j - KCONV // 2
                dcw_o[j:j + 1, :] += _colsum(dc * _shifted(u, s, pos))
                du = du + cw_ref[j:j + 1, :] * _shifted(dc, -s, pos)
            sgb = cv["sgb"]
            dp_o[...] = jnp.concatenate([du * sgb, du * cva * sgb * (1.0 - sgb)], axis=1).astype(BF16)

    return pl.pallas_call(
        kern, grid=(T.NT,), name="conv_bwd",
        in_specs=[_bs((TT, W), lambda i: (i, 1)), _bs((TT, CVC), lambda i: (i, 0)),
                  _bs((TT, W), lambda i: (T.lat(i), 0)),
                  _bs((32, W), lambda i: (0, 0)), _row(W), _row(W), _row(W)],
        out_specs=[_bs((TT, CVC), lambda i: (i, 0)), _bs((32, W), lambda i: (0, 0)), _row(W), _row(W), _row(W)],
        out_shape=[jax.ShapeDtypeStruct((T.NTOK, CVC), BF16), jax.ShapeDtypeStruct((32, W), F32),
                   jax.ShapeDtypeStruct((1, W), F32), jax.ShapeDtypeStruct((1, W), F32),
                   jax.ShapeDtypeStruct((1, W), F32)],
        compiler_params=_cp(("arbitrary",)),
    )(dcat, p_cv, conv, cw, cb, clw, clb)


def _readout_bwd(T, dcat, y, kd, rw, rk, gw, lnw, lnb):
    def kern(dc_ref, y_ref, kd_ref, rw_ref, rk_ref, gw_ref, lw_ref, lb_ref,
             dy_o, dr_o, dv_o, dkb_o, dgd_o, drk_o, dgw_o, dlw_o, dlb_o):
        i = pl.program_id(0)
        is_lat = i % T.TPS != 0

        @pl.when(i == 0)
        def _():
            for ref in (drk_o, dgw_o, dlw_o, dlb_o):
                ref[...] = jnp.zeros(ref.shape, ref.dtype)

        @pl.when(jnp.logical_not(is_lat))
        def _():
            for ref in (dy_o, dr_o, dv_o, dkb_o, dgd_o):
                ref[...] = jnp.zeros(ref.shape, ref.dtype)

        @pl.when(is_lat)
        def _():
            e = _e128(BF16)
            r = rw_ref[:, 0:512]
            v = rw_ref[:, 1024:1536]
            kbar = 0.5 * (kd_ref[0] + kd_ref[1])
            rk = rk_ref[...]
            ro = _readout_fwd(y_ref[0] + y_ref[1], r, v, rw_ref[:, 2048:2304], kbar, rk, gw_ref[...],
                              lw_ref[...], lb_ref[...], e)
            dout = dc_ref[...]
            dgg = dout * (ro["yn"] + ro["q"] * v)
            t1 = dout * ro["gg"]
            yhat = ro["yhat"]
            dlw_o[...] += _colsum(t1 * yhat)
            dlb_o[...] += _colsum(t1)
            dyh = t1 * lw_ref[...]
            dy_o[...] = ro["rstd"] * (dyh - _segsum(dyh, e) * (1.0 / HS) - yhat * (_segsum(dyh * yhat, e) * (1.0 / HS)))
            dq = _segsum(t1 * v, e)
            dv_o[...] = t1 * ro["q"]
            dr_o[...] = dq * kbar * rk
            dkb_o[...] = dq * r * rk
            drk_o[...] += _colsum(dq * r * kbar)
            sg = ro["sg"]
            dsg = _bdot(dgg, gw_ref[...], _NT_DIMS)
            dgd_o[...] = dsg * sg * (1.0 - sg)
            dgw_o[...] += _bdot(sg, dgg, _TN_DIMS)

    tok = lambda i: (i, 0)
    f32s = lambda *s: jax.ShapeDtypeStruct(s, F32)
    y_spec = _bs((2, TT, W), lambda i: (0, jnp.maximum(i % T.TPS, 1), i // T.TPS))
    return pl.pallas_call(
        kern, grid=(T.NT,), name="readout_bwd",
        in_specs=[_bs((TT, W), tok), y_spec, T.tm2_spec(), _bs((TT, RWC), tok),
                  _row(W), _bs((GDW, W), lambda i: (0, 0)), _row(W), _row(W)],
        out_specs=[T.tm_spec(), _bs((TT, W), tok), _bs((TT, W), tok), _bs((TT, W), tok), _bs((TT, GDW), tok),
                   _row(W), _bs((GDW, W), lambda i: (0, 0)), _row(W), _row(W)],
        out_shape=[f32s(T.TTOT, T.BW), f32s(T.NTOK, W), f32s(T.NTOK, W), f32s(T.NTOK, W), f32s(T.NTOK, GDW),
                   f32s(1, W), f32s(GDW, W), f32s(1, W), f32s(1, W)],
        compiler_params=_cp(("arbitrary",)),
    )(dcat, y, kd, rw, rk, gw, lnw, lnb)


def _scan_bwd(T, dy, r, v, kk, dec, kd, bb, hist, fin):
    NP = T.BW // 128
    R = NP * HS
    SB = SCAN_BSUB
    NS = T.TTOT // SB
    NSC = TT // SB

    def tmap(d, g):
        s = NS - 1 - g
        rev = jnp.where(s < NSC, NSC - 1 - s, NS - 1 - (s - NSC))
        return jnp.where(d == 0, s, rev)

    def kern(dy_ref, r_ref, v_ref, kk_ref, dec_ref, kd_ref, bb_ref, h_ref, fin_ref,
             dr_o, dw_o, dk_o, dv_o, da_o, db_o, ds_ref, snext):
        d = pl.program_id(0)
        g = pl.program_id(1)

        @pl.when(g == 0)
        def _():
            ds_ref[...] = jnp.zeros_like(ds_ref)
            snext[...] = fin_ref[0]

        e = _e128(BF16)

        def steps(with_dy):
            for t in range(SB - 1, -1, -1):
                row = jnp.where(d == 0, t, SB - 1 - t)
                sp = h_ref[0, t]
                a_row = -kk_ref[pl.ds(row, 1), :]
                a_ = _expand(a_row, NP)
                sa = _segp(sp.astype(BF16), a_row, NP, e)
                vc = _colb(v_ref[pl.ds(row, 1), :], NP, e)
                if with_dy:
                    st = snext[...] if t == SB - 1 else h_ref[0, t + 1]
                    dyc = _colb(dy_ref[pl.ds(row, 1), :], NP, e)
                    ds = ds_ref[...] + dyc * _expand(r_ref[pl.ds(row, 1), :], NP)
                    dr_o[0, pl.ds(row, 1), :] = _pair_colsum(st * dyc, NP)
                else:
                    ds = ds_ref[...]
                    dr_o[0, pl.ds(row, 1), :] = jnp.zeros((1, T.BW), F32)
                dsb = ds.astype(BF16)
                dsa = _segp(dsb, bb_ref[0, pl.ds(row, 1), :], NP, e)
                ds_ref[...] = ds * _expand(dec_ref[0, pl.ds(row, 1), :], NP) + dsa * a_
                dvb = _segp(dsb, kd_ref[0, pl.ds(row, 1), :], NP, e)
                dw_o[0, pl.ds(row, 1), :] = _pair_colsum(ds * sp, NP)
                db_o[0, pl.ds(row, 1), :] = _pair_colsum(ds * sa, NP)
                dv_o[0, pl.ds(row, 1), :] = _pair_colsum(_diag(dvb, NP), NP)
                dk_o[0, pl.ds(row, 1), :] = _pair_colsum(ds * vc, NP)
                da_o[0, pl.ds(row, 1), :] = _pair_colsum(sp * dsa, NP)

        @pl.when(g < NS - NSC)
        def _():
            steps(True)

        @pl.when(g >= NS - NSC)
        def _():
            steps(False)

        snext[...] = h_ref[0, 0]

    sh = _bs((SB, T.BW), lambda d, g: (tmap(d, g), 0))
    dr = _bs((1, SB, T.BW), lambda d, g: (d, tmap(d, g), 0))
    o2 = jax.ShapeDtypeStruct((2, T.TTOT, T.BW), F32)
    return pl.pallas_call(
        kern, grid=(2, NS), name="scan_bwd",
        in_specs=[sh, sh, sh, sh, dr, dr, dr, _bs((1, SB, R, 128), lambda d, g: (d, NS - 1 - g, 0, 0)),
                  _bs((1, R, 128), lambda d, g: (d, 0, 0))],
        out_specs=[dr] * 6,
        out_shape=[o2] * 6,
        scratch_shapes=[pltpu.VMEM((R, 128), F32), pltpu.VMEM((R, 128), F32)],
        compiler_params=_cp(("arbitrary", "arbitrary"), mb=48),
    )(dy, r, v, kk, dec, kd, bb, hist, fin)


def _prep_bwd(T, rw, dr_s, ddec, dkd, dv_s, da_s, dbb, dr_ro, dv_ro, dkbar, dgd, w0, w2, a0, a2, k_k, k_a):
    def kern(rw_ref, drs_ref, ddec_ref, dkd_ref, dvs_ref, das_ref, dbb_ref, drr_ref, dvr_ref, dkb_ref, dgd_ref,
             w0_ref, w2_ref, a0_ref, a2_ref, kk_ref, ka_ref,
             drw_o, dw0_o, dw2_o, da0_o, da2_o, dkk_o, dka_o):
        i = pl.program_id(0)
        first = i == 0
        e = _e128(BF16)
        w0v, w2v, a0v, a2v = _load_prep_params(w0_ref, w2_ref, a0_ref, a2_ref)
        k_k = kk_ref[...]
        k_a = ka_ref[...]
        o = _prep_math(rw_ref[...], w0v, w2v, a0v, a2v, k_k, k_a, e)
        k, kk = o["k"], o["kk"]
        dkbh = 0.5 * dkb_ref[...]
        dk = jnp.zeros_like(k)
        dkk = -(das_ref[0] + das_ref[1])
        dka = jnp.zeros((1, W), F32)
        dwd, dad = [], []
        for d in (0, 1):
            iclr = o["iclr"][d]
            dkd_d = dkd_ref[d] + dkbh
            dbb_d = dbb_ref[d]
            dk = dk + dkd_d * (1.0 + (iclr - 1.0) * k_a)
            dka = dka + _colsum(dkd_d * k * (iclr - 1.0))
            dkk = dkk + dbb_d * iclr
            dicl = dkd_d * k * k_a + dbb_d * kk
            dpa = dicl * iclr * (1.0 - iclr)
            _acc(da0_o.at[d:d + 1, :], _colsum(dpa), first)
            dad.append(_bdot(dpa, a2v[d], _NT_DIMS))
            _acc(da2_o.at[d], _bdot(o["ad"][d], dpa, _TN_DIMS), first)
            dpre = -ddec_ref[d] * o["dec"][d] * o["ex"][d] * _sigmoid(-o["pre"][d])
            _acc(dw0_o.at[d:d + 1, :], _colsum(dpre), first)
            th = o["th"][d]
            dth = _bdot(dpre, w2v[d], _NT_DIMS)
            _acc(dw2_o.at[d], _bdot(th, dpre, _TN_DIMS), first)
            dwd.append(dth * (1.0 - th * th))
        inv = o["inv"]
        kr = o["kr"]
        proj = _segsum(dkk * kr, e)
        dkr = dkk * inv - jnp.where(o["rt"] > 1e-12, kr * inv * inv * inv * proj, 0.0)
        dk = dk + dkr * k_k
        _acc(dkk_o, _colsum(dkr * k), first)
        _acc(dka_o, dka, first)
        dr = drs_ref[0] + drs_ref[1] + drr_ref[...]
        dv = dvs_ref[0] + dvs_ref[1] + dvr_ref[...]
        drw_o[...] = jnp.concatenate([dr, dk, dv, dwd[0], dwd[1], dad[0], dad[1], dgd_ref[...]], axis=1)

    tok = lambda i: (i, 0)
    f32s = lambda *s: jax.ShapeDtypeStruct(s, F32)
    p2 = lambda i: (0, 0)
    p3 = lambda i: (0, 0, 0)
    return pl.pallas_call(
        kern, grid=(T.NT,), name="prep_bwd",
        in_specs=[_bs((TT, RWC), tok)] + [T.tm2_spec()] * 6 + [_bs((TT, W), tok)] * 3 + [_bs((TT, GDW), tok)]
        + _prep_param_specs(),
        out_specs=[_bs((TT, RWC), tok), _bs((2, W), p2), _bs((2, LRW, W), p3), _bs((2, W), p2),
                   _bs((2, LRW, W), p3), _row(W), _row(W)],
        out_shape=[f32s(T.NTOK, RWC), f32s(2, W), f32s(2, LRW, W), f32s(2, W), f32s(2, LRW, W), f32s(1, W), f32s(1, W)],
        compiler_params=_cp(("arbitrary",), mb=56),
    )(rw, dr_s, ddec, dkd, dv_s, da_s, dbb, dr_ro, dv_ro, dkbar, dgd, w0, w2, a0, a2, k_k, k_a)


def _shift_bwd(T, drw, p_rw, mu_p, mu_n):
    def kern(d_ref, dp_ref, dn_ref, p_ref, pp_ref, pn_ref, mp_ref, mn_ref, dprw_o, dmp_o, dmn_o):
        i = pl.program_id(0)
        first = i == 0
        has_prev, has_next = _halo_masks(T, i)
        mp = mp_ref[...]
        mn = mn_ref[...]
        drw = d_ref[...]
        z = p_ref[...]
        zprev, znext = _neighbours(z, pp_ref[7:8, :] * has_prev, pn_ref[0:1, :] * has_next)
        _acc(dmp_o, _colsum(drw * (zprev - z)), first)
        _acc(dmn_o, _colsum(drw * (znext - z)), first)
        dprev, dnext = _neighbours(drw, dp_ref[7:8, :] * has_prev, dn_ref[0:1, :] * has_next)
        dprw_o[...] = (drw * (1.0 - mp - mn) + mp * dnext + mn * dprev).astype(BF16)

    tok = lambda i: (i, 0)
    prev, nxt = _halo_specs(T)
    f32s = lambda *s: jax.ShapeDtypeStruct(s, F32)
    return pl.pallas_call(
        kern, grid=(T.NT,), name="shift_bwd",
        in_specs=[_bs((TT, RWC), tok), prev, nxt, _bs((TT, RWC), tok), prev, nxt, _row(RWC), _row(RWC)],
        out_specs=[_bs((TT, RWC), tok), _row(RWC), _row(RWC)],
        out_shape=[jax.ShapeDtypeStruct((T.NTOK, RWC), BF16), f32s(1, RWC), f32s(1, RWC)],
        compiler_params=_cp(("arbitrary",), mb=56),
    )(drw, drw, drw, p_rw, p_rw, p_rw, mu_p, mu_n)


def _mix_in_bwd(T, dp_rw, dp_cv, x2, c2, dx1, modrows, g, w_rw, w_cv):
    def kern(drw_ref, dcv_ref, x_ref, c_ref, dx1_ref, mod_ref, g_ref, wr_ref, wc_ref, dxc_o, dmod_o, dg_o):
        i = pl.program_id(0)
        dh = (lax.dot_general(drw_ref[...], wr_ref[...], _NT_DIMS, preferred_element_type=F32)
              + lax.dot_general(dcv_ref[...], wc_ref[...], _NT_DIMS, preferred_element_type=F32))
        x = _tok_tile(T, x_ref, c_ref)
        s = lax.rsqrt(_rowmean(x * x) + EPS_RMS)
        xh = x * s
        gv = g_ref[...]
        q = i % T.TPS
        first_kind = jnp.logical_or(q == 0, q == 1)
        _acc(dmod_o.at[0, 0, 0:1, :], _colsum(dh), first_kind)
        _acc(dmod_o.at[0, 0, 1:2, :], _colsum(dh * (xh * gv)), first_kind)
        dn1 = dh * (1.0 + mod_ref[0, 0, 1:2, :])
        _acc(dg_o, _colsum(dn1 * xh), i == 0)
        dxh = dn1 * gv
        dxc_o[...] = dx1_ref[...] + s * (dxh - xh * _rowmean(dxh * xh))

    tok = lambda i: (i, 0)
    f32s = lambda *s: jax.ShapeDtypeStruct(s, F32)
    return pl.pallas_call(
        kern, grid=(T.NT,), name="mix_in_bwd",
        in_specs=[_bs((TT, RWC), tok), _bs((TT, CVC), tok)] + _tok_specs(T) + [
            _bs((TT, D), tok), T.mod_spec(), _row(D), _bs((D, RWC), lambda i: (0, 0)), _bs((D, CVC), lambda i: (0, 0))],
        out_specs=[_bs((TT, D), lambda i: (T.lat(i), 0)),
                   _bs((1, 1, 2, D), lambda i: (i // T.TPS, jnp.minimum(i % T.TPS, 1), 0, 0)), _row(D)],
        out_shape=[f32s(T.NLAT, D), f32s(T.B, 2, 2, D), f32s(1, D)],
        compiler_params=_cp(("arbitrary",)),
    )(dp_rw, dp_cv, x2, c2, dx1, modrows, g, w_rw, w_cv)


def _matmul_tn(a, b, name, tk, nk, tn, amap=None, bmap=None, tm=1024):
    M = a.shape[1]
    N = b.shape[1]
    amap = amap or (lambda k: k)
    bmap = bmap or (lambda k: k)

    def kern(a_ref, b_ref, o_ref):
        _acc(o_ref, lax.dot_general(a_ref[...], b_ref[...], _TN_DIMS, preferred_element_type=F32),
             pl.program_id(2) == 0)

    return pl.pallas_call(
        kern, grid=(M // tm, N // tn, nk), name=name,
        in_specs=[_bs((tk, tm), lambda i, j, k: (amap(k), i)), _bs((tk, tn), lambda i, j, k: (bmap(k), j))],
        out_specs=_bs((tm, tn), lambda i, j, k: (i, j)),
        out_shape=jax.ShapeDtypeStruct((M, N), F32),
        compiler_params=_cp(("parallel", "parallel", "arbitrary")),
    )(a, b)


def _silu(x):
    return x * _sigmoid(x)


def _ada_fwd(c_all, c_ctx, ada_w, ada_b_blk):
    nb = c_all.shape[0]
    R = nb + 8
    ncol = ada_w.shape[1]

    def kern(c_ref, cc_ref, w_ref, b_ref, o_ref):
        lhs = jnp.concatenate([_silu(c_ref[...]), _silu(cc_ref[...]), jnp.zeros((7, D), F32)], axis=0)
        o_ref[...] = jnp.dot(lhs, w_ref[...], precision=HI, preferred_element_type=F32) + b_ref[...]

    return pl.pallas_call(
        kern, name="ada_fwd", out_shape=jax.ShapeDtypeStruct((R, ncol), F32),
        compiler_params=_cp(None, 40),
    )(c_all, c_ctx, ada_w, ada_b_blk)


def _ada_bwd(c_all, c_ctx, ada_w, ex, cx, ex_blk, cx_blk):
    nb = c_all.shape[0]
    ncol = ada_w.shape[1]

    def kern(c_ref, cc_ref, w_ref, ex_ref, cx_ref, exb_ref, cxb_ref, gw_o, gb_o, ds_o):
        lhs = jnp.concatenate([_silu(c_ref[...]), _silu(cc_ref[...]), jnp.zeros((7, D), F32)], axis=0)
        dmc_blk = _colsum(cxb_ref[...])
        rhs = jnp.concatenate([exb_ref[...], dmc_blk, jnp.zeros((7, ncol), F32)], axis=0)
        gw_o[...] = lax.dot_general(lhs, rhs, _TN_DIMS, precision=HI, preferred_element_type=F32)
        gb_o[...] = _colsum(ex_ref[...]) + _colsum(cx_ref[...])
        ds_o[...] = lax.dot_general(jnp.concatenate([dmc_blk, jnp.zeros((7, ncol), F32)], axis=0), w_ref[...],
                                    _NT_DIMS, precision=HI, preferred_element_type=F32)

    return pl.pallas_call(
        kern, name="ada_bwd",
        out_shape=[jax.ShapeDtypeStruct((D, ncol), F32), jax.ShapeDtypeStruct((1, ex.shape[1]), F32),
                   jax.ShapeDtypeStruct((8, D), F32)],
        compiler_params=_cp(None, 48),
    )(c_all, c_ctx, ada_w, ex, cx, ex_blk, cx_blk)


def _cctx_final(parts, c_ctx):
    def kern(p_ref, c_ref, o_ref):
        tot = p_ref[0, 0:1, :]
        for j in range(1, parts.shape[0]):
            tot = tot + p_ref[j, 0:1, :]
        c = c_ref[...]
        sg = _sigmoid(c)
        o_ref[...] = tot * (sg * (1.0 + c * (1.0 - sg)))

    return pl.pallas_call(kern, name="cctx_final", out_shape=jax.ShapeDtypeStruct((1, D), F32))(parts, c_ctx)


def _peer(kind, p, ix, iy, ic):
    if kind == "chips":
        return (p // 2, p % 2, ic)
    if kind == "all":
        return (p // 4, (p // 2) % 2, p % 2)
    return (ix, iy, p)


def _exchange(x, kind, bcast, name, chunks=1):
    npeer = {"chips": 4, "all": 8, "sib": 2}[kind]
    slab = x.shape if bcast else x.shape[1:]
    assert chunks == 1 or slab[0] == chunks

    def kern(x_ref, o_ref, send_sems, recv_sems, lsem):
        ix, iy, ic = lax.axis_index("x"), lax.axis_index("y"), lax.axis_index("c")
        me = {"chips": 2 * ix + iy, "all": 4 * ix + 2 * iy + ic, "sib": ic}[kind]
        own = pltpu.make_async_copy(x_ref if bcast else x_ref.at[me], o_ref.at[me], lsem)
        own.start()

        def part(ref, k):
            return ref if chunks == 1 else ref.at[k]

        def copy(p, k):
            return pltpu.make_async_remote_copy(
                src_ref=part(x_ref if bcast else x_ref.at[p], k), dst_ref=part(o_ref.at[me], k),
                send_sem=send_sems.at[p, k], recv_sem=recv_sems.at[me, k],
                device_id=_peer(kind, p, ix, iy, ic), device_id_type=MESH)

        def arrival(p, k):
            return pltpu.make_async_remote_copy(
                src_ref=part(x_ref if bcast else x_ref.at[p], k), dst_ref=part(o_ref.at[p], k),
                send_sem=send_sems.at[p, k], recv_sem=recv_sems.at[p, k],
                device_id=_peer(kind, p, ix, iy, ic), device_id_type=MESH)

        for p in range(npeer):
            @pl.when(me != p)
            def _():
                for k in range(chunks):
                    copy(p, k).start()
        for p in range(npeer):
            @pl.when(me != p)
            def _():
                for k in range(chunks):
                    arrival(p, k).wait_recv()
        for p in range(npeer):
            @pl.when(me != p)
            def _():
                for k in range(chunks):
                    copy(p, k).wait_send()
        own.wait()

    any_spec = pl.BlockSpec(memory_space=pl.ANY)
    return pl.pallas_call(
        kern, name=name, in_specs=[any_spec], out_specs=any_spec,
        out_shape=jax.ShapeDtypeStruct((npeer,) + tuple(slab), x.dtype),
        scratch_shapes=[pltpu.SemaphoreType.DMA((npeer, chunks)), pltpu.SemaphoreType.DMA((npeer, chunks)),
                        pltpu.SemaphoreType.DMA],
    )(x)


_HBM = pl.BlockSpec(memory_space=pltpu.HBM)
_SEM = pl.BlockSpec(memory_space=pltpu.SEMAPHORE)
_FLOWS = pltpu.SideEffectType.DATAFLOW_SIDE_EFFECTING


def _other_chips(ix, iy, ic):
    return ((1 - ix, iy, ic), (ix, 1 - iy, ic), (1 - ix, 1 - iy, ic))


def _chip_index(dev):
    return 2 * dev[0] + dev[1]


def _gather_start(x, name, scatter=False):
    def kern(x_ref, land_ref, send_sems, recv_sems, x_thru, land_thru, token):
        ix, iy, ic = lax.axis_index("x"), lax.axis_index("y"), lax.axis_index("c")
        me = 2 * ix + iy
        for k, peer in enumerate(_other_chips(ix, iy, ic)):
            src = x_ref.at[_chip_index(peer)] if scatter else x_ref
            pltpu.make_async_remote_copy(src_ref=src, dst_ref=land_ref.at[me], send_sem=send_sems.at[k],
                                         recv_sem=recv_sems.at[k], device_id=peer, device_id_type=MESH).start()
        token[...] = jnp.zeros(token.shape, token.dtype)

    land = lax.empty(x.shape if scatter else (4,) + x.shape, x.dtype)
    return pl.pallas_call(
        kern, name=name,
        out_shape=(pltpu.SemaphoreType.DMA((3,)), pltpu.SemaphoreType.DMA((3,)), pltpu.HBM(x.shape, x.dtype),
                   pltpu.HBM(land.shape, land.dtype), jax.ShapeDtypeStruct((8, 128), F32)),
        in_specs=(_HBM, _HBM), out_specs=(_SEM, _SEM, _HBM, _HBM, pl.BlockSpec(memory_space=pltpu.VMEM)),
        input_output_aliases={0: 2, 1: 3},
        compiler_params=pltpu.CompilerParams(has_side_effects=_FLOWS),
    )(pltpu.with_memory_space_constraint(x, pltpu.HBM), pltpu.with_memory_space_constraint(land, pltpu.HBM))


def _gather_wait(send_sems, recv_sems, x_thru, land_thru, after, name, scatter=False):
    def kern(x_ref, land_ref, send_sems_ref, recv_sems_ref, after_ref, x_dead, land_out):
        ix, iy, ic = lax.axis_index("x"), lax.axis_index("y"), lax.axis_index("c")
        for k, peer in enumerate(_other_chips(ix, iy, ic)):
            src = x_ref.at[_chip_index(peer)] if scatter else x_ref
            copy = pltpu.make_async_remote_copy(src_ref=src, dst_ref=land_ref.at[_chip_index(peer)],
                                                send_sem=send_sems_ref.at[k], recv_sem=recv_sems_ref.at[k],
                                                device_id=peer, device_id_type=MESH)
            copy.wait_send()
            copy.wait_recv()

    return pl.pallas_call(
        kern, name=name,
        out_shape=(pltpu.HBM(x_thru.shape, x_thru.dtype), pltpu.HBM(land_thru.shape, land_thru.dtype)),
        in_specs=(_HBM, _HBM, _SEM, _SEM, pl.BlockSpec(memory_space=pl.ANY)), out_specs=(_HBM, _HBM),
        input_output_aliases={0: 0, 1: 1},
        compiler_params=pltpu.CompilerParams(has_side_effects=_FLOWS),
    )(x_thru, land_thru, send_sems, recv_sems, after)


def _sum_slots(x, name):
    n, R, C = x.shape
    budget = (8 << 20) // (n * C * x.dtype.itemsize)
    tr = max([t for t in range(8, R + 1, 8) if R % t == 0 and t <= max(budget, 8)], default=R)

    def kern(x_ref, o_ref):
        tot = x_ref[0]
        for s in range(1, n):
            tot = tot + x_ref[s]
        o_ref[...] = tot

    return pl.pallas_call(
        kern, grid=(R // tr,), name=name,
        in_specs=[_bs((n, tr, C), lambda i: (0, i, 0))], out_specs=_bs((tr, C), lambda i: (i, 0)),
        out_shape=jax.ShapeDtypeStruct((R, C), x.dtype), compiler_params=_cp(("parallel",)),
    )(x)


def _sib_stream(x, me, name, add, nck=1):
    if add:
        nslab, rows, C = x.shape
        R = rows // (2 * nck)
        assert R * 2 * nck == rows and R % 8 == 0
        K = nslab * nck
    else:
        K, R, C = x.shape

    def kern(me_ref, *refs):
        if add:
            own_ref, send_ref, o_ref, rbuf, ssem, rsem, credit = refs
        else:
            send_ref, o_ref, rbuf, ssem, rsem, credit = refs
        k = pl.program_id(0)
        slot = k % 2
        sib = (lax.axis_index("x"), lax.axis_index("y"), 1 - lax.axis_index("c"))

        @pl.when(k >= 2)
        def _():
            pl.semaphore_wait(credit.at[slot], 1)

        cp = pltpu.make_async_remote_copy(src_ref=send_ref.at[0], dst_ref=rbuf.at[slot], send_sem=ssem.at[slot],
                                          recv_sem=rsem.at[slot], device_id=sib, device_id_type=MESH)
        cp.start()
        cp.wait_recv()
        o_ref[0] = own_ref[0] + rbuf[slot] if add else rbuf[slot]
        cp.wait_send()

        @pl.when(k + 2 < K)
        def _():
            pl.semaphore_signal(credit.at[slot], 1, device_id=sib, device_id_type=MESH)

    if add:
        in_specs = [_bs((1, R, C), lambda k, me_ref: (k // nck, me_ref[0] * nck + k % nck, 0)),
                    _bs((1, R, C), lambda k, me_ref: (k // nck, (1 - me_ref[0]) * nck + k % nck, 0))]
        args = (x, x)
    else:
        in_specs = [_bs((1, R, C), lambda k, me_ref: (k, 0, 0))]
        args = (x,)
    return pl.pallas_call(
        kern, name=name,
        grid_spec=pltpu.PrefetchScalarGridSpec(
            num_scalar_prefetch=1, grid=(K,), in_specs=in_specs,
            out_specs=_bs((1, R, C), lambda k, me_ref: (k, 0, 0)),
            scratch_shapes=[pltpu.VMEM((2, R, C), x.dtype), pltpu.SemaphoreType.DMA((2,)),
                            pltpu.SemaphoreType.DMA((2,)), pltpu.SemaphoreType.REGULAR((2,))]),
        out_shape=jax.ShapeDtypeStruct((K, R, C), x.dtype),
        compiler_params=_cp(("arbitrary",)),
    )(me, *args)


def _adamw(w, g, m, v, name):
    shape = w.shape
    if len(shape) == 1:
        outs = _adamw(*(t.reshape(1, -1) for t in (w, g, m, v)), name)
        return tuple(t.reshape(shape) for t in outs)
    nd = len(shape)
    size = 1
    for s in shape:
        size *= s
    rows = shape[-2]
    tr = rows
    if size > (1 << 18) and all(s == 1 for s in shape[:-2]):
        tr = max(t for t in (256, 128, 64, 32, 16, 8) if rows % t == 0)
    c1 = 1.0 - ADAM_B1 ** ADAM_STEP
    c2 = 1.0 - ADAM_B2 ** ADAM_STEP

    def kern(w_ref, g_ref, m_ref, v_ref, d_o, m_o, v_o):
        gv = g_ref[...]
        mn = ADAM_B1 * m_ref[...] + (1.0 - ADAM_B1) * gv
        vn = ADAM_B2 * v_ref[...] + (1.0 - ADAM_B2) * (gv * gv)
        m_o[...] = mn
        v_o[...] = vn
        d_o[...] = -ADAM_LR * ((mn / c1) / (jnp.sqrt(vn / c2) + ADAM_EPS) + ADAM_WD * w_ref[...])

    spec = _bs(shape[:-2] + (tr, shape[-1]), lambda i: (0,) * (nd - 2) + (i, 0))
    o = jax.ShapeDtypeStruct(shape, F32)
    return tuple(pl.pallas_call(
        kern, grid=(rows // tr,), name=name, in_specs=[spec] * 4, out_specs=[spec] * 3, out_shape=[o, o, o],
        compiler_params=_cp(("parallel",)),
    )(w, g, m, v))


_WEIGHT_NAMES = ("c_ctx", "ada_w", "ada_b", "mix_pre_g", "mix_post_g", "mlp_pre_g", "mlp_post_g", "w_in", "mu_prev",
                 "mu_next", "decay_w0", "decay_w2", "iclr_a0", "iclr_a2", "k_k", "k_a", "r_k", "gate_w2", "lnx_w",
                 "lnx_b", "conv_w", "conv_b", "conv_ln_w", "conv_ln_b", "w_out", "mlp_w1", "mlp_w2")


def _pack_rows(parts, cols=512):
    flat = jnp.concatenate([p.reshape(-1) for p in parts])
    rows = -(-flat.shape[0] // cols)
    rows = -(-rows // 16) * 16
    flat = jnp.pad(flat, (0, rows * cols - flat.shape[0]))
    return flat.reshape(rows, cols)


def _unpack(flat, shapes):
    out = []
    off = 0
    for s in shapes:
        n = 1
        for d in s:
            n *= d
        out.append(flat[off:off + n].reshape(s))
        off += n
    return out


def _local_step(T, x2, c2, tgt, modrows, P, late_weights, early_grads):
    p_rw, p_cv, h = _mix_in(T, x2, c2, modrows, P["mix_pre_g"], P["w_rw"], P["w_cv"])
    prep_params = (P["w0"], P["w2"], P["a0"], P["a2"], P["k_k"], P["k_a"])
    r, v, kk, dec, kd, bb, rw = _rwkv_prep(T, p_rw, P["mu_p"], P["mu_n"], *prep_params)
    y, fin, hist = _scan_fwd(T, r, v, kk, dec, kd, bb)
    P = dict(P, **late_weights(fin))
    ro_params = (P["r_k"], P["gate_w2"], P["lnx_w"], P["lnx_b"])
    cv_params = (P["conv_w"], P["conv_b"], P["conv_ln_w"], P["conv_ln_b"])
    cat, mix, x1, conv = _mix_out(T, y, kd, rw, p_cv, x2, modrows, *ro_params, *cv_params, P["mix_post_g"], P["w_out"])
    m, h2 = _mlp_fwd(T, x1, modrows, P["mlp_pre_g"], P["wl"])
    loss_acc, dm, dx2, dg2, d_mlp_post = _loss_head(T, m, x1, tgt, modrows, P["mlp_post_g"])
    fact, da, dh2 = _mlp_bwd(T, h2, dm, P["wl"])
    dx1, dmod2, d_mlp_pre = _mlp_in_bwd(T, dh2, x1, dx2, modrows, P["mlp_pre_g"])
    dmix, dcat, dg1, d_mix_post = _mix_post_bwd(T, dx1, mix, modrows, P["mix_post_g"], P["w_out"])
    kl = max(t for t in (1024, 512, 256) if T.NLAT % t == 0)
    dw_out = _matmul_tn(cat, dmix, "dw_out", TT, T.NL, 1024, bmap=T.tok)
    dw1 = _matmul_tn(h2, da, "dw_mlp1", kl, T.NLAT // kl, 1024)
    dw2m = _matmul_tn(fact, dm, "dw_mlp2", kl, T.NLAT // kl, 1024)
    fin = fin + early_grads(dw1, dw2m, dw_out)
    dp_cv, d_conv_w, d_conv_b, d_cln_w, d_cln_b = _conv_bwd(T, dcat, p_cv, conv, *cv_params)
    dy, dr_ro, dv_ro, dkbar, dgd, d_r_k, d_gate, d_lnx_w, d_lnx_b = _readout_bwd(T, dcat, y, kd, rw, *ro_params)
    dr_s, ddec, dkd, dv_s, da_s, dbb = _scan_bwd(T, dy, r, v, kk, dec, kd, bb, hist, fin)
    drw, d_w0, d_w2, d_a0, d_a2, d_k_k, d_k_a = _prep_bwd(T, rw, dr_s, ddec, dkd, dv_s, da_s, dbb, dr_ro, dv_ro,
                                                          dkbar, dgd, *prep_params)
    dp_rw, d_mu_p, d_mu_n = _shift_bwd(T, drw, p_rw, P["mu_p"], P["mu_n"])
    dxc, dmod1, d_mix_pre = _mix_in_bwd(T, dp_rw, dp_cv, x2, c2, dx1, modrows, P["mix_pre_g"], P["w_rw"], P["w_cv"])
    kt = max(t for t in (1024, 768, 512, 256) if T.NTOK % t == 0)
    dw_rw = _matmul_tn(h, dp_rw, "dw_in_rw", kt, T.NTOK // kt, 768)
    dw_cv = _matmul_tn(h, dp_cv, "dw_in_cv", kt, T.NTOK // kt, 1024)
    small = dict(mix_pre_g=d_mix_pre, mix_post_g=d_mix_post, mlp_pre_g=d_mlp_pre, mlp_post_g=d_mlp_post,
                 mu_p=d_mu_p, mu_n=d_mu_n, w0=d_w0, w2=d_w2, a0=d_a0, a2=d_a2, k_k=d_k_k, k_a=d_k_a, r_k=d_r_k,
                 gate_w2=d_gate, lnx_w=d_lnx_w, lnx_b=d_lnx_b, conv_w=d_conv_w, conv_b=d_conv_b,
                 conv_ln_w=d_cln_w, conv_ln_b=d_cln_b)
    big = dict(w_rw=dw_rw, w_cv=dw_cv, w_out=dw_out, w1=dw1, w2m=dw2m, after_scan=dr_s)
    dmods = dict(dmod1=dmod1, dg1=dg1, dmod2=dmod2, dg2=dg2)
    return loss_acc[0, 0], dxc, small, big, dmods


_SMALL_ORDER = ("mix_pre_g", "mix_post_g", "mlp_pre_g", "mlp_post_g", "mu_p", "mu_n", "w0", "w2", "a0", "a2", "k_k",
                "k_a", "r_k", "gate_w2", "lnx_w", "lnx_b", "conv_w", "conv_b", "conv_ln_w", "conv_ln_b")


def kernel(x, c, ctx, c_ctx, ada_w, ada_b, mix_pre_g, mix_post_g, mlp_pre_g, mlp_post_g, w_in, mu_prev, mu_next, decay_w0, decay_w2, iclr_a0, iclr_a2, k_k, k_a, r_k, gate_w2, lnx_w, lnx_b, conv_w, conv_b, conv_ln_w, conv_ln_b, w_out, mlp_w1, mlp_w2, loss_target, m_c_ctx, m_ada_w, m_ada_b, m_mix_pre_g, m_mix_post_g, m_mlp_pre_g, m_mlp_post_g, m_w_in, m_mu_prev, m_mu_next, m_decay_w0, m_decay_w2, m_iclr_a0, m_iclr_a2, m_k_k, m_k_a, m_r_k, m_gate_w2, m_lnx_w, m_lnx_b, m_conv_w, m_conv_b, m_conv_ln_w, m_conv_ln_b, m_w_out, m_mlp_w1, m_mlp_w2, v_c_ctx, v_ada_w, v_ada_b, v_mix_pre_g, v_mix_post_g, v_mlp_pre_g, v_mlp_post_g, v_w_in, v_mu_prev, v_mu_next, v_decay_w0, v_decay_w2, v_iclr_a0, v_iclr_a2, v_k_k, v_k_a, v_r_k, v_gate_w2, v_lnx_w, v_lnx_b, v_conv_w, v_conv_b, v_conv_ln_w, v_conv_ln_b, v_w_out, v_mlp_w1, v_mlp_w2):
    weights = dict(zip(_WEIGHT_NAMES, (c_ctx, ada_w, ada_b, mix_pre_g, mix_post_g, mlp_pre_g, mlp_post_g, w_in, mu_prev, mu_next, decay_w0, decay_w2, iclr_a0, iclr_a2, k_k, k_a, r_k, gate_w2, lnx_w, lnx_b, conv_w, conv_b, conv_ln_w, conv_ln_b, w_out, mlp_w1, mlp_w2)))
    moms = dict(zip(_WEIGHT_NAMES, (m_c_ctx, m_ada_w, m_ada_b, m_mix_pre_g, m_mix_post_g, m_mlp_pre_g, m_mlp_post_g, m_w_in, m_mu_prev, m_mu_next, m_decay_w0, m_decay_w2, m_iclr_a0, m_iclr_a2, m_k_k, m_k_a, m_r_k, m_gate_w2, m_lnx_w, m_lnx_b, m_conv_w, m_conv_b, m_conv_ln_w, m_conv_ln_b, m_w_out, m_mlp_w1, m_mlp_w2)))
    vars_ = dict(zip(_WEIGHT_NAMES, (v_c_ctx, v_ada_w, v_ada_b, v_mix_pre_g, v_mix_post_g, v_mlp_pre_g, v_mlp_post_g, v_w_in, v_mu_prev, v_mu_next, v_decay_w0, v_decay_w2, v_iclr_a0, v_iclr_a2, v_k_k, v_k_a, v_r_k, v_gate_w2, v_lnx_w, v_lnx_b, v_conv_w, v_conv_b, v_conv_ln_w, v_conv_ln_b, v_w_out, v_mlp_w1, v_mlp_w2)))

    B, t_lat, _ = x.shape
    assert ctx.shape[1] == TT and t_lat % TT == 0 and (t_lat * B) % MT == 0
    T = _Tiles(B, t_lat)
    ix, iy, ic = lax.axis_index("x"), lax.axis_index("y"), lax.axis_index("c")
    chip = 2 * ix + iy
    dev = 4 * ix + 2 * iy + ic
    nsh = 4
    in_sh = w_in.shape[2]
    ada_sh = ada_w.shape[2]
    lane_sh = decay_w0.shape[2]

    in_sems_s, in_sems_r, in_x, in_land, in_token = _gather_start(w_in[0].astype(BF16), "gather_w_in_start")
    late_pack = jnp.concatenate([mlp_w1[0], mlp_w2[0], w_out[0]], axis=0).astype(BF16)
    late_sems_s, late_sems_r, late_x, late_land, late_token = _gather_start(late_pack, "gather_mlp_start")
    n_w1, n_w2 = mlp_w1.shape[1], mlp_w2.shape[1]

    def late_weights(after):
        own, land = _gather_wait(late_sems_s, late_sems_r, late_x, late_land, after, "gather_mlp_wait")
        wl = lax.dynamic_update_slice(land, own[None], (chip, 0, 0))
        return dict(wl=wl, w_out=jnp.concatenate([wl[j, n_w1 + n_w2:] for j in range(nsh)], axis=0))

    sm_parts = (decay_w0[0], decay_w2[0], iclr_a0[0], iclr_a2[0], gate_w2[0], conv_w[0])
    sm_shapes = [p.shape for p in sm_parts]
    sg = _exchange(_pack_rows(sm_parts), "chips", True, "gather_small_weights")
    pers = [_unpack(sg[j].reshape(-1), sm_shapes) for j in range(nsh)]
    w0_f, w2_f_, a0_f, a2_f, gate_f, convw_f = (jnp.concatenate([pers[j][t] for j in range(nsh)], axis=-1)
                                                for t in range(6))

    def pad_rows(a, n):
        return jnp.pad(a, [(0, 0)] * (a.ndim - 2) + [(0, n - a.shape[-2]), (0, 0)])

    c_ctx2 = c_ctx.reshape(1, D)
    c_all = _exchange(jnp.pad(c + in_token[0, 0], ((0, 8 - B), (0, 0))), "all", True, "gather_c")[:, :B]
    c_all = c_all.reshape(8 * B, D)
    ada_b_blk = lax.dynamic_slice(ada_b, (0, chip * ada_sh), (1, ada_sh))
    mod_blk = _ada_fwd(c_all, c_ctx2, ada_w[0], ada_b_blk)
    mod_g = _exchange(mod_blk, "chips", True, "gather_mod")
    mod_all = jnp.concatenate([mod_g[j] for j in range(nsh)], axis=1)
    mod_x = lax.dynamic_slice(mod_all, (dev * B, 0), (B, 6 * D)).reshape(B, 6, D)
    mod_c = jnp.broadcast_to(mod_all[8 * B].reshape(1, 6, D), (B, 6, D))
    modrows = jnp.stack([mod_c, mod_x], axis=1) + late_token[0, 0]

    own_in, land_in = _gather_wait(in_sems_s, in_sems_r, in_x, in_land, mod_all, "gather_w_in_wait")
    wg_in = lax.dynamic_update_slice(land_in, own_in[None], (chip, 0, 0))
    w_in_f = jnp.concatenate([wg_in[j] for j in range(nsh)], axis=1)
    w_in_p = _pad_cols(w_in_f, w_in_f.shape[1])
    P = dict(
        w_rw=w_in_p[:, :RWC], w_cv=w_in_p[:, RWC:],
        mix_pre_g=mix_pre_g, mix_post_g=mix_post_g, mlp_pre_g=mlp_pre_g, mlp_post_g=mlp_post_g,
        mu_p=_pad_cols(mu_prev, mu_prev.shape[1]), mu_n=_pad_cols(mu_next, mu_next.shape[1]),
        w0=w0_f, w2=pad_rows(w2_f_, LRW), a0=a0_f, a2=pad_rows(a2_f, LRW), k_k=k_k, k_a=k_a,
        r_k=r_k.reshape(1, W), gate_w2=pad_rows(gate_f, GDW), lnx_w=lnx_w, lnx_b=lnx_b,
        conv_w=pad_rows(convw_f, 32), conv_b=conv_b, conv_ln_w=conv_ln_w, conv_ln_b=conv_ln_b)

    x2 = x.reshape(T.NLAT, D)
    c2 = ctx.reshape(B * TT, D)
    tgt = loss_target.reshape(T.NLAT, D)
    me1 = ic.reshape(1).astype(jnp.int32)
    early = {}

    def early_grads(dw1, dw2, dw_out):
        c_1, n_o = mlp_w1.shape[2], w_out.shape[1]
        slabs = jnp.stack([jnp.concatenate([dw1[:, c_1 * j:c_1 * (j + 1)], dw2[n_w2 * j:n_w2 * (j + 1)],
                                            dw_out[n_o * j:n_o * (j + 1)]], axis=0) for j in range(nsh)])
        pair = _sib_stream(slabs, me1, "sib_reduce_mlp_grads", True, nck=3)
        send, recv, x_thru, land, token = _gather_start(pair.reshape(nsh, -1, slabs.shape[2]), "reduce_mlp_start",
                                                        scatter=True)
        early.update(send=send, recv=recv, x=x_thru, land=land)
        return token[0, 0]

    loss_loc, dxl, small, big, dm_ = _local_step(T, x2, c2, tgt, modrows, P, late_weights, early_grads)
    loss = lax.psum(loss_loc, ("x", "y", "c"))
    grad_x = dxl.reshape(x.shape)

    dw_in_f = _unpad_cols(jnp.concatenate([big["w_rw"], big["w_cv"]], axis=1), w_in_f.shape[1])
    slabs_in = jnp.stack([dw_in_f[:, in_sh * j:in_sh * (j + 1)] for j in range(nsh)])
    pair_in = _sib_stream(slabs_in, me1, "sib_reduce_w_in", True, nck=2).reshape(nsh, -1, in_sh)
    win_s, win_r, win_x, win_land, win_token = _gather_start(pair_in, "reduce_w_in_start", scatter=True)

    dmod_x = jnp.concatenate([dm_["dmod1"][:, 1], dm_["dg1"], dm_["dmod2"], dm_["dg2"]], axis=1)
    dmod_c = jnp.concatenate([dm_["dmod1"][:, 0], jnp.zeros((B, 4, D), F32)], axis=1)
    dpack = jnp.concatenate([dmod_x.reshape(B, 6 * D), dmod_c.reshape(B, 6 * D)], axis=0)
    dpack = dpack + win_token[0, 0]
    dg = _exchange(dpack, "all", True, "gather_dmod")
    ex = dg[:, :B].reshape(8 * B, 6 * D)
    cx = dg[:, B:].reshape(8 * B, 6 * D)
    ex_blk = lax.dynamic_slice(ex, (0, chip * ada_sh), (8 * B, ada_sh))
    cx_blk = lax.dynamic_slice(cx, (0, chip * ada_sh), (8 * B, ada_sh))
    g_ada_w, g_ada_b, dscc = _ada_bwd(c_all, c_ctx2, ada_w[0], ex, cx, ex_blk, cx_blk)
    dscc_g = _exchange(dscc, "chips", True, "gather_dcctx")
    g_c_ctx = _cctx_final(dscc_g, c_ctx2).reshape(D)

    small = dict(small, mu_p=_unpad_cols(small["mu_p"], mu_prev.shape[1]),
                 mu_n=_unpad_cols(small["mu_n"], mu_next.shape[1]),
                 w2=small["w2"][:, :decay_w2.shape[2]], a2=small["a2"][:, :iclr_a2.shape[2]],
                 gate_w2=small["gate_w2"][:gate_w2.shape[1]], conv_w=small["conv_w"][:KCONV])
    sm_list = [small[n] for n in _SMALL_ORDER]
    sm_shapes2 = [a.shape for a in sm_list]
    sm_pack = _pack_rows(sm_list) + win_token[0, 0]
    sm_pair = _sib_stream(jnp.concatenate([sm_pack, sm_pack], axis=0)[None], me1, "sib_small_grads", True)[0]
    sm_tot = _sum_slots(_exchange(sm_pair, "chips", True, "gather_small_grads"), "sum_small_grads")
    S = dict(zip(_SMALL_ORDER, _unpack(sm_tot.reshape(-1), sm_shapes2)))

    def shard_last(a):
        return lax.dynamic_slice_in_dim(a, chip * lane_sh, lane_sh, axis=a.ndim - 1)

    grads = dict(
        c_ctx=g_c_ctx, ada_w=g_ada_w[None], ada_b=g_ada_b,
        mix_pre_g=S["mix_pre_g"], mix_post_g=S["mix_post_g"], mlp_pre_g=S["mlp_pre_g"], mlp_post_g=S["mlp_post_g"],
        mu_prev=S["mu_p"], mu_next=S["mu_n"],
        decay_w0=shard_last(S["w0"])[None], decay_w2=shard_last(S["w2"])[None],
        iclr_a0=shard_last(S["a0"])[None], iclr_a2=shard_last(S["a2"])[None],
        k_k=S["k_k"], k_a=S["k_a"], r_k=S["r_k"].reshape(r_k.shape),
        gate_w2=shard_last(S["gate_w2"])[None], lnx_w=S["lnx_w"], lnx_b=S["lnx_b"],
        conv_w=shard_last(S["conv_w"])[None], conv_b=S["conv_b"], conv_ln_w=S["conv_ln_w"],
        conv_ln_b=S["conv_ln_b"])

    def both_halves(mine_, name, nck):
        chunks = mine_.reshape(nck, mine_.shape[0] // nck, mine_.shape[1])
        other = _sib_stream(chunks, me1, name, False)
        full = jnp.where(ic == 0, jnp.concatenate([chunks, other], axis=0), jnp.concatenate([other, chunks], axis=0))
        return full.reshape(2 * mine_.shape[0], mine_.shape[1])

    own, land = _gather_wait(early["send"], early["recv"], early["x"], early["land"], big["after_scan"],
                             "reduce_mlp_wait", scatter=True)
    land = lax.dynamic_update_slice(land, lax.dynamic_index_in_dim(own, chip, 0, keepdims=True), (chip, 0, 0))
    tot = both_halves(_sum_slots(land, "sum_mlp_grads"), "sib_swap_mlp_grads", 3)
    g_w1, g_w2, g_w_out = tot[:n_w1], tot[n_w1:n_w1 + n_w2], tot[n_w1 + n_w2:]

    own, land = _gather_wait(win_s, win_r, win_x, win_land, sm_tot, "reduce_w_in_wait", scatter=True)
    land = lax.dynamic_update_slice(land, lax.dynamic_index_in_dim(own, chip, 0, keepdims=True), (chip, 0, 0))
    g_w_in = both_halves(_sum_slots(land, "sum_w_in_grads"), "sib_swap_w_in", 2)
    grads.update(w_in=g_w_in[None], w_out=g_w_out[None], mlp_w1=g_w1[None], mlp_w2=g_w2[None])

    deltas, new_m, new_v = {}, {}, {}
    for n in _WEIGHT_NAMES:
        g = grads[n].reshape(weights[n].shape)
        grads[n] = g
        deltas[n], new_m[n], new_v[n] = _adamw(weights[n], g, moms[n], vars_[n], "adamw_" + n)

    return (loss, grad_x, *[grads[n] for n in _WEIGHT_NAMES], *[deltas[n] for n in _WEIGHT_NAMES],
            *[new_m[n] for n in _WEIGHT_NAMES], *[new_v[n] for n in _WEIGHT_NAMES])
```

```python
import functools

import jax
import jax.numpy as jnp
from jax import lax
from jax.experimental import pallas as pl
from jax.experimental.pallas import tpu as pltpu

F32 = jnp.float32
BF16 = jnp.bfloat16
HI = lax.Precision.HIGHEST

D = 1024
W = 512
HS = 64
RWC = 2304
CVC = 1024
GDW = 256
LRW = 128
DFF = 4096
TT = 256
LINE = 64
KCONV = 31
EPS_RMS = 1e-6
EPS_LN = 1e-5
EPS_GN = 64e-5
SCAN_CH = 128
SCAN_G = 8
SCAN_BSUB = 16

ADAM_LR = 0.001
ADAM_B1 = 0.9
ADAM_B2 = 0.999
ADAM_EPS = 1e-08
ADAM_WD = 0.01
ADAM_STEP = 10

_SEGS = ((0, 1536, 1536), (1536, 64, 128), (1600, 64, 128), (1664, 64, 128), (1728, 64, 128),
         (1792, 160, 256), (1952, 1024, 1024))

MESH = pl.DeviceIdType.MESH


def _bs(shape, imap):
    return pl.BlockSpec(shape, imap)


def _cp(sem=None, mb=48):
    return pltpu.CompilerParams(dimension_semantics=sem, vmem_limit_bytes=mb << 20)


def _pad_cols(a, ncols):
    out = []
    for s, w, pw in _SEGS:
        if s >= ncols:
            break
        piece = a[..., s:s + w]
        if pw > w:
            piece = jnp.pad(piece, [(0, 0)] * (a.ndim - 1) + [(0, pw - w)])
        out.append(piece)
    return jnp.concatenate(out, axis=-1)


def _unpad_cols(a, ncols):
    out = []
    off = 0
    for s, w, pw in _SEGS:
        if s >= ncols:
            break
        out.append(a[..., off:off + w])
        off += pw
    return jnp.concatenate(out, axis=-1)


def _sigmoid(x):
    return 1.0 / (1.0 + jnp.exp(-x))


def _softplus(x):
    return jnp.maximum(x, 0.0) + jnp.log(1.0 + jnp.exp(-jnp.abs(x)))


def _e128(dtype):
    r = lax.broadcasted_iota(jnp.int32, (128, 128), 0) >= HS
    c = lax.broadcasted_iota(jnp.int32, (128, 128), 1) >= HS
    return (r == c).astype(dtype)


def _segsum(x, e):
    hi = x.astype(BF16)
    lo = (x - hi.astype(F32)).astype(BF16)
    return jnp.concatenate(
        [jnp.dot(hi[:, 128 * g:128 * (g + 1)], e, preferred_element_type=F32)
         + jnp.dot(lo[:, 128 * g:128 * (g + 1)], e, preferred_element_type=F32) for g in range(4)], axis=1)


_NT_DIMS = (((1,), (1,)), ((), ()))
_TN_DIMS = (((0,), (0,)), ((), ()))


def _bdot(a, b, dims=None):
    a = a.astype(BF16)
    b = b.astype(BF16)
    if dims is None:
        return jnp.dot(a, b, preferred_element_type=F32)
    return lax.dot_general(a, b, dims, preferred_element_type=F32)


def _colsum(x):
    return jnp.sum(x, axis=0, keepdims=True)


def _rowmean(x):
    return jnp.mean(x, axis=-1, keepdims=True)


def _diag(x, npairs):
    row = lax.broadcasted_iota(jnp.int32, (HS, 128), 0)
    lane = lax.broadcasted_iota(jnp.int32, (HS, 128), 1) & (HS - 1)
    keep = jnp.broadcast_to((lane == row)[None], (npairs, HS, 128))
    return jnp.where(keep, x.reshape(npairs, HS, 128), 0.0).reshape(npairs * HS, 128)


def _segb(x, e):
    return jnp.dot(x.astype(BF16), e, preferred_element_type=F32)


_segb1 = _segb


def _segp(xb, row, npairs, e):
    return jnp.dot(xb * _expand(row.astype(BF16), npairs), e, preferred_element_type=F32)


def _expand(row, npairs):
    return jnp.concatenate([jnp.broadcast_to(row[:, 128 * j:128 * (j + 1)], (HS, 128)) for j in range(npairs)], axis=0)


def _colb(row, npairs, e):
    return _segb1(_diag(_expand(row, npairs), npairs), e)


def _pair_colsum(x, npairs):
    return jnp.concatenate([_colsum(x[HS * j:HS * (j + 1)]) for j in range(npairs)], axis=1)


def _conv_pos():
    return lax.broadcasted_iota(jnp.int32, (TT, W), 0) & (LINE - 1)


def _shifted(u, s, pos):
    if s == 0:
        return u
    sh = pltpu.roll(u, (-s) % TT, 0)
    valid = jnp.logical_and(pos + s >= 0, pos + s < LINE)
    return jnp.where(valid, sh, 0.0)


def _acc(ref, val, first):
    @pl.when(first)
    def _():
        ref[...] = jnp.zeros(ref.shape, ref.dtype)
    ref[...] += val


class _Tiles:
    def __init__(self, B, t_lat):
        self.B = B
        self.NLT = t_lat // TT
        self.TPS = self.NLT + 1
        self.NT = B * self.TPS
        self.NL = B * self.NLT
        self.NTOK = self.NT * TT
        self.NLAT = self.NL * TT
        self.TTOT = self.TPS * TT
        self.BW = B * W

    def b(self, i):
        return i // self.TPS

    def q(self, i):
        return i % self.TPS

    def lat(self, i):
        return (i // self.TPS) * self.NLT + jnp.maximum(i % self.TPS - 1, 0)

    def tok(self, l):
        return (l // self.NLT) * self.TPS + 1 + l % self.NLT

    def mod_spec(self):
        return _bs((1, 1, 6, D), lambda i: (i // self.TPS, jnp.minimum(i % self.TPS, 1), 0, 0))

    def tm_spec(self):
        return _bs((TT, W), lambda i: (i % self.TPS, i // self.TPS))

    def tm2_spec(self):
        return _bs((2, TT, W), lambda i: (0, i % self.TPS, i // self.TPS))


def _row(shape_last):
    return _bs((1, shape_last), lambda i: (0, 0))


def _tok_specs(T):
    return [_bs((TT, D), lambda i: (T.lat(i), 0)), _bs((TT, D), lambda i: (i // T.TPS, 0))]


def _tok_tile(T, x_ref, c_ref):
    is_ctx = (pl.program_id(0) % T.TPS == 0).astype(F32)
    return c_ref[...] * is_ctx + x_ref[...] * (1.0 - is_ctx)


def _mix_in(T, x2, c2, modrows, g, w_rw, w_cv):
    def kern(x_ref, c_ref, mod_ref, g_ref, wr_ref, wc_ref, prw_ref, pcv_ref, h_ref):
        x = _tok_tile(T, x_ref, c_ref)
        s = lax.rsqrt(_rowmean(x * x) + EPS_RMS)
        h = (x * s * g_ref[...]) * (1.0 + mod_ref[0, 0, 1:2, :]) + mod_ref[0, 0, 0:1, :]
        hb = h.astype(BF16)
        h_ref[...] = hb
        prw_ref[...] = jnp.dot(hb, wr_ref[...], preferred_element_type=F32)
        pcv_ref[...] = jnp.dot(hb, wc_ref[...], preferred_element_type=F32)

    return pl.pallas_call(
        kern, grid=(T.NT,), name="mix_in",
        in_specs=_tok_specs(T) + [T.mod_spec(), _row(D),
                                  _bs((D, RWC), lambda i: (0, 0)), _bs((D, CVC), lambda i: (0, 0))],
        out_specs=[_bs((TT, RWC), lambda i: (i, 0)), _bs((TT, CVC), lambda i: (i, 0)), _bs((TT, D), lambda i: (i, 0))],
        out_shape=[jax.ShapeDtypeStruct((T.NTOK, RWC), F32), jax.ShapeDtypeStruct((T.NTOK, CVC), F32),
                   jax.ShapeDtypeStruct((T.NTOK, D), BF16)],
        compiler_params=_cp(("parallel",)),
    )(x2, c2, modrows, g, w_rw, w_cv)


def _halo_specs(T):
    nb8 = T.NTOK // 8
    prev = _bs((8, RWC), lambda i: (jnp.maximum(i * (TT // 8) - 1, 0), 0))
    nxt = _bs((8, RWC), lambda i: (jnp.minimum((i + 1) * (TT // 8), nb8 - 1), 0))
    return prev, nxt


def _halo_masks(T, i):
    q = i % T.TPS
    has_prev = jnp.logical_and(q != 0, q != 1).astype(F32)
    has_next = jnp.logical_and(q != 0, q != T.TPS - 1).astype(F32)
    return has_prev, has_next


def _neighbours(z, prev_row, next_row):
    rowi = lax.broadcasted_iota(jnp.int32, z.shape, 0)
    zprev = jnp.where(rowi == 0, prev_row, pltpu.roll(z, 1, 0))
    znext = jnp.where(rowi == TT - 1, next_row, pltpu.roll(z, TT - 1, 0))
    return zprev, znext


def _prep_math(rw, w0, w2, a0, a2, k_k, k_a, e):
    r = rw[:, 0:512]
    k = rw[:, 512:1024]
    v = rw[:, 1024:1536]
    kr = k * k_k
    ss = _segsum(kr * kr, e)
    rt = jnp.sqrt(ss)
    inv = 1.0 / jnp.maximum(rt, 1e-12)
    kk = kr * inv
    o = dict(r=r, k=k, v=v, kr=kr, rt=rt, inv=inv, kk=kk, th=[], pre=[], ex=[], dec=[], iclr=[], kd=[], bb=[], ad=[])
    for d in (0, 1):
        wd = rw[:, 1536 + LRW * d:1536 + LRW * (d + 1)]
        ad = rw[:, 1792 + LRW * d:1792 + LRW * (d + 1)]
        th = jnp.tanh(wd)
        pre = w0[d] + _bdot(th, w2[d])
        ex = jnp.exp(-_softplus(-pre) - 0.5)
        dec = jnp.exp(-ex)
        iclr = _sigmoid(a0[d] + _bdot(ad, a2[d]))
        o["th"].append(th)
        o["pre"].append(pre)
        o["ex"].append(ex)
        o["dec"].append(dec)
        o["iclr"].append(iclr)
        o["ad"].append(ad)
        o["kd"].append(k * (1.0 + (iclr - 1.0) * k_a))
        o["bb"].append(kk * iclr)
    return o


def _load_prep_params(w0_ref, w2_ref, a0_ref, a2_ref):
    w0 = [w0_ref[0:1, :], w0_ref[1:2, :]]
    a0 = [a0_ref[0:1, :], a0_ref[1:2, :]]
    w2 = [w2_ref[0], w2_ref[1]]
    a2 = [a2_ref[0], a2_ref[1]]
    return w0, w2, a0, a2


def _prep_param_specs():
    return [_bs((2, W), lambda i: (0, 0)), _bs((2, LRW, W), lambda i: (0, 0, 0)),
            _bs((2, W), lambda i: (0, 0)), _bs((2, LRW, W), lambda i: (0, 0, 0)), _row(W), _row(W)]


def _rwkv_prep(T, p_rw, mu_p, mu_n, w0, w2, a0, a2, k_k, k_a):
    def kern(p_ref, pp_ref, pn_ref, mp_ref, mn_ref, w0_ref, w2_ref, a0_ref, a2_ref, kk_ref, ka_ref,
             r_o, v_o, kk_o, dec_o, kd_o, bb_o, rw_o):
        i = pl.program_id(0)
        has_prev, has_next = _halo_masks(T, i)
        z = p_ref[...]
        zprev, znext = _neighbours(z, pp_ref[7:8, :] * has_prev, pn_ref[0:1, :] * has_next)
        rw = z + mp_ref[...] * (zprev - z) + mn_ref[...] * (znext - z)
        rw_o[...] = rw
        w0v, w2v, a0v, a2v = _load_prep_params(w0_ref, w2_ref, a0_ref, a2_ref)
        o = _prep_math(rw, w0v, w2v, a0v, a2v, kk_ref[...], ka_ref[...], _e128(BF16))
        r_o[...] = o["r"]
        v_o[...] = o["v"]
        kk_o[...] = o["kk"]
        for d in (0, 1):
            dec_o[d] = o["dec"][d]
            kd_o[d] = o["kd"][d]
            bb_o[d] = o["bb"][d]

    prev, nxt = _halo_specs(T)
    tm = jax.ShapeDtypeStruct((T.TTOT, T.BW), F32)
    tm2 = jax.ShapeDtypeStruct((2, T.TTOT, T.BW), F32)
    return pl.pallas_call(
        kern, grid=(T.NT,), name="rwkv_prep",
        in_specs=[_bs((TT, RWC), lambda i: (i, 0)), prev, nxt, _row(RWC), _row(RWC)] + _prep_param_specs(),
        out_specs=[T.tm_spec(), T.tm_spec(), T.tm_spec(), T.tm2_spec(), T.tm2_spec(), T.tm2_spec(),
                   _bs((TT, RWC), lambda i: (i, 0))],
        out_shape=[tm, tm, tm, tm2, tm2, tm2, jax.ShapeDtypeStruct((T.NTOK, RWC), F32)],
        compiler_params=_cp(("parallel",)),
    )(p_rw, p_rw, p_rw, mu_p, mu_n, w0, w2, a0, a2, k_k, k_a)


def _scan_fwd(T, r, v, kk, dec, kd, bb):
    NP = T.BW // 128
    R = NP * HS
    NCH = T.TTOT // SCAN_CH
    NCC = TT // SCAN_CH
    G = SCAN_G
    NG = SCAN_CH // G
    NGRP = 4
    assert NG % NGRP == 0

    def tmap(d, i):
        rev = jnp.where(i < NCC, NCC - 1 - i, NCH - 1 - (i - NCC))
        return jnp.where(d == 0, i, rev)

    def kern(r_ref, v_ref, kk_ref, dec_ref, kd_ref, bb_ref, y_ref, fin_ref, hist_ref, ring, sems):
        d = pl.program_id(0)
        i = pl.program_id(1)

        @pl.when(i == 0)
        def _():
            ring[0] = jnp.zeros((R, 128), F32)

        e = _e128(BF16)

        def hist_copy(k):
            grp = k % NGRP
            return pltpu.make_async_copy(ring.at[pl.ds(grp * G, G)],
                                         hist_ref.at[d, pl.ds(i * SCAN_CH + k * G, G)], sems.at[grp])

        def make_body(with_y):
            def body(k, carry):
                @pl.when(k >= NGRP - 1)
                def _():
                    hist_copy(k - (NGRP - 1)).wait()

                base = (k % NGRP) * G
                s = ring[base]
                sb = s.astype(BF16)
                for u in range(G):
                    t = k * G + u
                    row = jnp.where(d == 0, t, SCAN_CH - 1 - t)
                    sa = _segp(sb, -kk_ref[pl.ds(row, 1), :], NP, e)
                    vc = _colb(v_ref[pl.ds(row, 1), :], NP, e)
                    s = (s * _expand(dec_ref[0, pl.ds(row, 1), :], NP) + sa * _expand(bb_ref[0, pl.ds(row, 1), :], NP)
                         + vc * _expand(kd_ref[0, pl.ds(row, 1), :], NP))
                    ring[(base + u + 1) if u < G - 1 else ((k + 1) % NGRP) * G] = s
                    sb = s.astype(BF16)
                    if with_y:
                        yb = _segp(sb, r_ref[pl.ds(row, 1), :], NP, e)
                        y_ref[0, pl.ds(row, 1), :] = _pair_colsum(_diag(yb, NP), NP)
                hist_copy(k).start()
                return carry
            return body

        @pl.when(i < NCC)
        def _():
            lax.fori_loop(0, NG, make_body(False), 0)

        @pl.when(i >= NCC)
        def _():
            lax.fori_loop(0, NG, make_body(True), 0)

        for k in range(NG - (NGRP - 1), NG):
            hist_copy(k).wait()

        @pl.when(i == NCH - 1)
        def _():
            fin_ref[0] = ring[0]

    sh = _bs((SCAN_CH, T.BW), lambda d, i: (tmap(d, i), 0))
    dr = _bs((1, SCAN_CH, T.BW), lambda d, i: (d, tmap(d, i), 0))
    return pl.pallas_call(
        kern, grid=(2, NCH), name="scan_fwd",
        in_specs=[sh, sh, sh, dr, dr, dr],
        out_specs=[dr, _bs((1, R, 128), lambda d, i: (d, 0, 0)), pl.BlockSpec(memory_space=pl.ANY)],
        out_shape=[jax.ShapeDtypeStruct((2, T.TTOT, T.BW), F32), jax.ShapeDtypeStruct((2, R, 128), F32),
                   jax.ShapeDtypeStruct((2, T.TTOT, R, 128), F32)],
        scratch_shapes=[pltpu.VMEM((NGRP * G, R, 128), F32), pltpu.SemaphoreType.DMA((NGRP,))],
        compiler_params=_cp(("arbitrary", "arbitrary")),
    )(r, v, kk, dec, kd, bb)


def _readout_fwd(y, r, v, gd, kbar, rk, gw, lw, lb, e):
    mu = _segsum(y, e) * (1.0 / HS)
    yc = y - mu
    var = _segsum(yc * yc, e) * (1.0 / HS)
    rstd = lax.rsqrt(var + EPS_GN)
    yhat = yc * rstd
    yn = yhat * lw + lb
    q = _segsum(r * kbar * rk, e)
    sg = _sigmoid(gd)
    gg = _bdot(sg, gw)
    return dict(yhat=yhat, rstd=rstd, yn=yn, q=q, sg=sg, gg=gg, out=(yn + q * v) * gg)


def _conv_fwd(cva, cvb, cw_ref, cb, lw, lb, c=None):
    pos = _conv_pos()
    sgb = _sigmoid(cvb)
    u = cva * sgb
    if c is None:
        c = jnp.zeros_like(u)
        for j in range(KCONV):
            c = c + cw_ref[j:j + 1, :] * _shifted(u, j - KCONV // 2, pos)
        c = c + cb
    mu = _rowmean(c)
    cc = c - mu
    rstd = lax.rsqrt(_rowmean(cc * cc) + EPS_LN)
    chat = cc * rstd
    cn = chat * lw + lb
    scn = _sigmoid(cn)
    return dict(sgb=sgb, u=u, c=c, chat=chat, rstd=rstd, cn=cn, scn=scn, out=cn * scn, pos=pos)


def _mix_out(T, y, kd, rw, p_cv, x2, modrows, rk, gw, lnw, lnb, cw, cb, clw, clb, pg, w_out):
    tk = T.tok

    def kern(y_ref, kd_ref, rw_ref, pcv_ref, x_ref, mod_ref, rk_ref, gw_ref, lw_ref, lb_ref, cw_ref, cb_ref,
             clw_ref, clb_ref, pg_ref, wo_ref, cat_o, mix_o, x1_o, conv_o):
        e = _e128(BF16)
        ro = _readout_fwd(y_ref[0] + y_ref[1], rw_ref[:, 0:512], rw_ref[:, 1024:1536], rw_ref[:, 2048:2304],
                          0.5 * (kd_ref[0] + kd_ref[1]), rk_ref[...], gw_ref[...], lw_ref[...], lb_ref[...], e)
        cv = _conv_fwd(pcv_ref[:, 0:512], pcv_ref[:, 512:1024], cw_ref, cb_ref[...], clw_ref[...], clb_ref[...])
        conv_o[...] = cv["c"]
        catb = jnp.concatenate([ro["out"], cv["out"]], axis=1).astype(BF16)
        cat_o[...] = catb
        mix = jnp.dot(catb, wo_ref[...], preferred_element_type=F32)
        mix_o[...] = mix
        sm = lax.rsqrt(_rowmean(mix * mix) + EPS_RMS)
        x1_o[...] = x_ref[...] + mod_ref[0, 0, 2:3, :] * (mix * sm * pg_ref[...])

    lat = lambda l: (l, 0)
    return pl.pallas_call(
        kern, grid=(T.NL,), name="mix_out",
        in_specs=[_bs((2, TT, W), lambda l: (0, 1 + l % T.NLT, l // T.NLT)),
                  _bs((2, TT, W), lambda l: (0, 1 + l % T.NLT, l // T.NLT)),
                  _bs((TT, RWC), lambda l: (tk(l), 0)), _bs((TT, CVC), lambda l: (tk(l), 0)),
                  _bs((TT, D), lambda l: (l, 0)),
                  _bs((1, 1, 6, D), lambda l: (l // T.NLT, 1, 0, 0)),
                  _row(W), _bs((GDW, W), lambda l: (0, 0)), _row(W), _row(W),
                  _bs((32, W), lambda l: (0, 0)), _row(W), _row(W), _row(W), _row(D),
                  _bs((D, D), lambda l: (0, 0))],
        out_specs=[_bs((TT, D), lat), _bs((TT, D), lat), _bs((TT, D), lat), _bs((TT, W), lat)],
        out_shape=[jax.ShapeDtypeStruct((T.NLAT, D), BF16), jax.ShapeDtypeStruct((T.NLAT, D), F32),
                   jax.ShapeDtypeStruct((T.NLAT, D), F32), jax.ShapeDtypeStruct((T.NLAT, W), F32)],
        compiler_params=_cp(("parallel",)),
    )(y, kd, rw, p_cv, x2, modrows, rk, gw, lnw, lnb, cw, cb, clw, clb, pg, w_out)


MT = 512
FC = 1024


def _late_weight_specs():
    assert FC == D
    return [_bs((1, D, FC), lambda t, f: (f, 0, 0)), _bs((1, FC, D), lambda t, f: (f, 1, 0))]


def _mlp_fwd(T, x1, tgt, modrows, g, pg, wl):
    per_b = T.NLT * TT // MT
    last = DFF // FC - 1

    def kern(x_ref, t_ref, mod_ref, g_ref, pg_ref, w1_ref, w2_ref, h2_o, loss_o, dm_o, dx2_o, dg2_o, dpg_o, h2_s, m_s):
        t = pl.program_id(0)
        f = pl.program_id(1)

        @pl.when(f == 0)
        def _():
            x = x_ref[...]
            s = lax.rsqrt(_rowmean(x * x) + EPS_RMS)
            h2 = (x * s * g_ref[...]) * (1.0 + mod_ref[0, 0, 4:5, :]) + mod_ref[0, 0, 3:4, :]
            h2_s[...] = h2.astype(BF16)
            h2_o[...] = h2.astype(BF16)
            m_s[...] = jnp.zeros_like(m_s)

        a = jnp.dot(h2_s[...], w1_ref[0], preferred_element_type=F32)
        rl = jnp.maximum(a, 0.0)
        m_s[...] += jnp.dot((rl * rl).astype(BF16), w2_ref[0], preferred_element_type=F32)

        @pl.when(f == last)
        def _():
            m_ = m_s[...]
            sm = lax.rsqrt(_rowmean(m_ * m_) + EPS_RMS)
            mn = m_ * sm
            g2 = mod_ref[0, 0, 5:6, :]
            pgv = pg_ref[...]
            diff = x_ref[...] + g2 * (mn * pgv) - t_ref[...]
            sq = jnp.sum(_colsum(diff * diff), axis=1, keepdims=True)
            _acc(loss_o, jnp.zeros((8, 128), F32) + (0.5 / D) * sq, t == 0)
            dx2 = diff * (1.0 / D)
            dx2_o[...] = dx2
            _acc(dg2_o.at[0], _colsum(dx2 * mn * pgv), t % per_b == 0)
            _acc(dpg_o, _colsum(dx2 * g2 * mn), t == 0)
            dmn = dx2 * g2 * pgv
            dm_o[...] = (sm * (dmn - mn * _rowmean(dmn * mn))).astype(BF16)

    tok = lambda t, f: (t, 0)
    const2 = lambda t, f: (0, 0)
    return pl.pallas_call(
        kern, grid=(T.NLAT // MT, DFF // FC), name="mlp_fwd",
        in_specs=[_bs((MT, D), tok), _bs((MT, D), tok), _bs((1, 1, 6, D), lambda t, f: (t // per_b, 1, 0, 0)),
                  _bs((1, D), const2), _bs((1, D), const2)] + _late_weight_specs(),
        out_specs=[_bs((MT, D), tok), _bs((8, 128), const2), _bs((MT, D), tok), _bs((MT, D), tok),
                   _bs((1, 1, D), lambda t, f: (t // per_b, 0, 0)), _bs((1, D), const2)],
        out_shape=[jax.ShapeDtypeStruct((T.NLAT, D), BF16), jax.ShapeDtypeStruct((8, 128), F32),
                   jax.ShapeDtypeStruct((T.NLAT, D), BF16), jax.ShapeDtypeStruct((T.NLAT, D), F32),
                   jax.ShapeDtypeStruct((T.B, 1, D), F32), jax.ShapeDtypeStruct((1, D), F32)],
        scratch_shapes=[pltpu.VMEM((MT, D), BF16), pltpu.VMEM((MT, D), F32)],
        compiler_params=_cp(("arbitrary", "arbitrary")),
    )(x1, tgt, modrows, g, pg, wl, wl)


def _mlp_bwd(T, h2, dm, x1, dx2, modrows, g, wl):
    per_b = T.NLT * TT // MT
    last = DFF // FC - 1

    def kern(h2_ref, dm_ref, x1_ref, dx2_ref, mod_ref, g_ref, w1_ref, w2_ref, f_o, da_o, dx1_o, dmod_o, dg_o, dh_s):
        t = pl.program_id(0)
        f = pl.program_id(1)
        a = jnp.dot(h2_ref[...], w1_ref[0], preferred_element_type=F32)
        rl = jnp.maximum(a, 0.0)
        f_o[...] = (rl * rl).astype(BF16)
        df = lax.dot_general(dm_ref[...], w2_ref[0], _NT_DIMS, preferred_element_type=F32)
        dab = (df * (2.0 * rl)).astype(BF16)
        da_o[...] = dab
        _acc(dh_s, lax.dot_general(dab, w1_ref[0], _NT_DIMS, preferred_element_type=F32), f == 0)

        @pl.when(f == last)
        def _():
            x = x1_ref[...]
            s = lax.rsqrt(_rowmean(x * x) + EPS_RMS)
            xh = x * s
            gv = g_ref[...]
            dh = dh_s[...]
            first_b = t % per_b == 0
            _acc(dmod_o.at[0, 0:1, :], _colsum(dh), first_b)
            _acc(dmod_o.at[0, 1:2, :], _colsum(dh * (xh * gv)), first_b)
            dn2 = dh * (1.0 + mod_ref[0, 0, 4:5, :])
            _acc(dg_o, _colsum(dn2 * xh), t == 0)
            dxh = dn2 * gv
            dx1_o[...] = dx2_ref[...] + s * (dxh - xh * _rowmean(dxh * xh))

    tok = lambda t, f: (t, 0)
    const2 = lambda t, f: (0, 0)
    return pl.pallas_call(
        kern, grid=(T.NLAT // MT, DFF // FC), name="mlp_bwd",
        in_specs=[_bs((MT, D), tok), _bs((MT, D), tok), _bs((MT, D), tok), _bs((MT, D), tok),
                  _bs((1, 1, 6, D), lambda t, f: (t // per_b, 1, 0, 0)), _bs((1, D), const2)] + _late_weight_specs(),
        out_specs=[_bs((MT, FC), lambda t, f: (t, f)), _bs((MT, FC), lambda t, f: (t, f)), _bs((MT, D), tok),
                   _bs((1, 2, D), lambda t, f: (t // per_b, 0, 0)), _bs((1, D), const2)],
        out_shape=[jax.ShapeDtypeStruct((T.NLAT, DFF), BF16), jax.ShapeDtypeStruct((T.NLAT, DFF), BF16),
                   jax.ShapeDtypeStruct((T.NLAT, D), F32), jax.ShapeDtypeStruct((T.B, 2, D), F32),
                   jax.ShapeDtypeStruct((1, D), F32)],
        scratch_shapes=[pltpu.VMEM((MT, D), F32)],
        compiler_params=_cp(("arbitrary", "arbitrary")),
    )(h2, dm, x1, dx2, modrows, g, wl, wl)


def _mix_post_bwd(T, dx1, mix, modrows, pg, w_out):
    def kern(dx_ref, mix_ref, mod_ref, pg_ref, wo_ref, dmix_o, dcat_o, dg1_o, dpg_o):
        i = pl.program_id(0)
        lat = (i % T.TPS != 0).astype(F32)
        dx = dx_ref[...] * lat
        mix = mix_ref[...]
        sm = lax.rsqrt(_rowmean(mix * mix) + EPS_RMS)
        mh = mix * sm
        g1 = mod_ref[0, 0, 2:3, :]
        pgv = pg_ref[...]
        _acc(dg1_o.at[0], _colsum(dx * mh * pgv), i % T.TPS == 0)
        _acc(dpg_o, _colsum(dx * g1 * mh), i == 0)
        dmh = dx * g1 * pgv
        dmix = ((sm * (dmh - mh * _rowmean(dmh * mh))) * lat).astype(BF16)
        dmix_o[...] = dmix
        dcat_o[...] = lax.dot_general(dmix, wo_ref[...], _NT_DIMS, preferred_element_type=F32)

    tok = lambda i: (i, 0)
    return pl.pallas_call(
        kern, grid=(T.NT,), name="mix_post_bwd",
        in_specs=[_bs((TT, D), lambda i: (T.lat(i), 0)), _bs((TT, D), lambda i: (T.lat(i), 0)),
                  _bs((1, 1, 6, D), lambda i: (i // T.TPS, 1, 0, 0)), _row(D), _bs((D, D), lambda i: (0, 0))],
        out_specs=[_bs((TT, D), tok), _bs((TT, D), tok), _bs((1, 1, D), lambda i: (i // T.TPS, 0, 0)), _row(D)],
        out_shape=[jax.ShapeDtypeStruct((T.NTOK, D), BF16), jax.ShapeDtypeStruct((T.NTOK, D), F32),
                   jax.ShapeDtypeStruct((T.B, 1, D), F32), jax.ShapeDtypeStruct((1, D), F32)],
        compiler_params=_cp(("arbitrary",)),
    )(dx1, mix, modrows, pg, w_out)


def _conv_bwd(T, dcat, p_cv, conv, cw, cb, clw, clb):
    def kern(dc_ref, pcv_ref, conv_ref, cw_ref, cb_ref, clw_ref, clb_ref, dp_o, dcw_o, dcb_o, dlw_o, dlb_o):
        i = pl.program_id(0)
        is_lat = i % T.TPS != 0

        @pl.when(i == 0)
        def _():
            for ref in (dcw_o, dcb_o, dlw_o, dlb_o):
                ref[...] = jnp.zeros(ref.shape, ref.dtype)

        @pl.when(jnp.logical_not(is_lat))
        def _():
            dp_o[...] = jnp.zeros(dp_o.shape, dp_o.dtype)

        @pl.when(is_lat)
        def _():
            cva = pcv_ref[:, 0:512]
            cv = _conv_fwd(cva, pcv_ref[:, 512:1024], cw_ref, cb_ref[...], clw_ref[...], clb_ref[...],
                           c=conv_ref[...])
            scn = cv["scn"]
            dcn = dc_ref[...] * (scn * (1.0 + cv["cn"] * (1.0 - scn)))
            chat = cv["chat"]
            dlw_o[...] += _colsum(dcn * chat)
            dlb_o[...] += _colsum(dcn)
            dchat = dcn * clw_ref[...]
            dc = cv["rstd"] * (dchat - _rowmean(dchat) - chat * _rowmean(dchat * chat))
            dcb_o[...] += _colsum(dc)
            pos = cv["pos"]
            u = cv["u"]
            du = jnp.zeros_like(u)
            for j in range(KCONV):
                s = j - KCONV // 2
                dcw_o[j:j + 1, :] += _colsum(dc * _shifted(u, s, pos))
                du = du + cw_ref[j:j + 1, :] * _shifted(dc, -s, pos)
            sgb = cv["sgb"]
            dp_o[...] = jnp.concatenate([du * sgb, du * cva * sgb * (1.0 - sgb)], axis=1).astype(BF16)

    return pl.pallas_call(
        kern, grid=(T.NT,), name="conv_bwd",
        in_specs=[_bs((TT, W), lambda i: (i, 1)), _bs((TT, CVC), lambda i: (i, 0)),
                  _bs((TT, W), lambda i: (T.lat(i), 0)),
                  _bs((32, W), lambda i: (0, 0)), _row(W), _row(W), _row(W)],
        out_specs=[_bs((TT, CVC), lambda i: (i, 0)), _bs((32, W), lambda i: (0, 0)), _row(W), _row(W), _row(W)],
        out_shape=[jax.ShapeDtypeStruct((T.NTOK, CVC), BF16), jax.ShapeDtypeStruct((32, W), F32),
                   jax.ShapeDtypeStruct((1, W), F32), jax.ShapeDtypeStruct((1, W), F32),
                   jax.ShapeDtypeStruct((1, W), F32)],
        compiler_params=_cp(("arbitrary",)),
    )(dcat, p_cv, conv, cw, cb, clw, clb)


def _readout_bwd(T, dcat, y, kd, rw, rk, gw, lnw, lnb):
    def kern(dc_ref, y_ref, kd_ref, rw_ref, rk_ref, gw_ref, lw_ref, lb_ref,
             dy_o, dr_o, dv_o, dkb_o, dgd_o, drk_o, dgw_o, dlw_o, dlb_o):
        i = pl.program_id(0)
        is_lat = i % T.TPS != 0

        @pl.when(i == 0)
        def _():
            for ref in (drk_o, dgw_o, dlw_o, dlb_o):
                ref[...] = jnp.zeros(ref.shape, ref.dtype)

        @pl.when(jnp.logical_not(is_lat))
        def _():
            for ref in (dy_o, dr_o, dv_o, dkb_o, dgd_o):
                ref[...] = jnp.zeros(ref.shape, ref.dtype)

        @pl.when(is_lat)
        def _():
            e = _e128(BF16)
            r = rw_ref[:, 0:512]
            v = rw_ref[:, 1024:1536]
            kbar = 0.5 * (kd_ref[0] + kd_ref[1])
            rk = rk_ref[...]
            ro = _readout_fwd(y_ref[0] + y_ref[1], r, v, rw_ref[:, 2048:2304], kbar, rk, gw_ref[...],
                              lw_ref[...], lb_ref[...], e)
            dout = dc_ref[...]
            dgg = dout * (ro["yn"] + ro["q"] * v)
            t1 = dout * ro["gg"]
            yhat = ro["yhat"]
            dlw_o[...] += _colsum(t1 * yhat)
            dlb_o[...] += _colsum(t1)
            dyh = t1 * lw_ref[...]
            dy_o[...] = ro["rstd"] * (dyh - _segsum(dyh, e) * (1.0 / HS) - yhat * (_segsum(dyh * yhat, e) * (1.0 / HS)))
            dq = _segsum(t1 * v, e)
            dv_o[...] = t1 * ro["q"]
            dr_o[...] = dq * kbar * rk
            dkb_o[...] = dq * r * rk
            drk_o[...] += _colsum(dq * r * kbar)
            sg = ro["sg"]
            dsg = _bdot(dgg, gw_ref[...], _NT_DIMS)
            dgd_o[...] = dsg * sg * (1.0 - sg)
            dgw_o[...] += _bdot(sg, dgg, _TN_DIMS)

    tok = lambda i: (i, 0)
    f32s = lambda *s: jax.ShapeDtypeStruct(s, F32)
    y_spec = _bs((2, TT, W), lambda i: (0, jnp.maximum(i % T.TPS, 1), i // T.TPS))
    return pl.pallas_call(
        kern, grid=(T.NT,), name="readout_bwd",
        in_specs=[_bs((TT, W), tok), y_spec, T.tm2_spec(), _bs((TT, RWC), tok),
                  _row(W), _bs((GDW, W), lambda i: (0, 0)), _row(W), _row(W)],
        out_specs=[T.tm_spec(), _bs((TT, W), tok), _bs((TT, W), tok), _bs((TT, W), tok), _bs((TT, GDW), tok),
                   _row(W), _bs((GDW, W), lambda i: (0, 0)), _row(W), _row(W)],
        out_shape=[f32s(T.TTOT, T.BW), f32s(T.NTOK, W), f32s(T.NTOK, W), f32s(T.NTOK, W), f32s(T.NTOK, GDW),
                   f32s(1, W), f32s(GDW, W), f32s(1, W), f32s(1, W)],
        compiler_params=_cp(("arbitrary",)),
    )(dcat, y, kd, rw, rk, gw, lnw, lnb)


def _scan_bwd(T, dy, r, v, kk, dec, kd, bb, hist, fin):
    NP = T.BW // 128
    R = NP * HS
    SB = SCAN_BSUB
    NS = T.TTOT // SB
    NSC = TT // SB

    def tmap(d, g):
        s = NS - 1 - g
        rev = jnp.where(s < NSC, NSC - 1 - s, NS - 1 - (s - NSC))
        return jnp.where(d == 0, s, rev)

    def kern(dy_ref, r_ref, v_ref, kk_ref, dec_ref, kd_ref, bb_ref, h_ref, fin_ref,
             dr_o, dw_o, dk_o, dv_o, da_o, db_o, ds_ref, snext):
        d = pl.program_id(0)
        g = pl.program_id(1)

        @pl.when(g == 0)
        def _():
            ds_ref[...] = jnp.zeros_like(ds_ref)
            snext[...] = fin_ref[0]

        e = _e128(BF16)

        def steps(with_dy):
            for t in range(SB - 1, -1, -1):
                row = jnp.where(d == 0, t, SB - 1 - t)
                sp = h_ref[0, t]
                a_row = -kk_ref[pl.ds(row, 1), :]
                a_ = _expand(a_row, NP)
                sa = _segp(sp.astype(BF16), a_row, NP, e)
                vc = _colb(v_ref[pl.ds(row, 1), :], NP, e)
                if with_dy:
                    st = snext[...] if t == SB - 1 else h_ref[0, t + 1]
                    dyc = _colb(dy_ref[pl.ds(row, 1), :], NP, e)
                    ds = ds_ref[...] + dyc * _expand(r_ref[pl.ds(row, 1), :], NP)
                    dr_o[0, pl.ds(row, 1), :] = _pair_colsum(st * dyc, NP)
                else:
                    ds = ds_ref[...]
                    dr_o[0, pl.ds(row, 1), :] = jnp.zeros((1, T.BW), F32)
                dsb = ds.astype(BF16)
                dsa = _segp(dsb, bb_ref[0, pl.ds(row, 1), :], NP, e)
                ds_ref[...] = ds * _expand(dec_ref[0, pl.ds(row, 1), :], NP) + dsa * a_
                dvb = _segp(dsb, kd_ref[0, pl.ds(row, 1), :], NP, e)
                dw_o[0, pl.ds(row, 1), :] = _pair_colsum(ds * sp, NP)
                db_o[0, pl.ds(row, 1), :] = _pair_colsum(ds * sa, NP)
                dv_o[0, pl.ds(row, 1), :] = _pair_colsum(_diag(dvb, NP), NP)
                dk_o[0, pl.ds(row, 1), :] = _pair_colsum(ds * vc, NP)
                da_o[0, pl.ds(row, 1), :] = _pair_colsum(sp * dsa, NP)

        @pl.when(g < NS - NSC)
        def _():
            steps(True)

        @pl.when(g >= NS - NSC)
        def _():
            steps(False)

        snext[...] = h_ref[0, 0]

    sh = _bs((SB, T.BW), lambda d, g: (tmap(d, g), 0))
    dr = _bs((1, SB, T.BW), lambda d, g: (d, tmap(d, g), 0))
    o2 = jax.ShapeDtypeStruct((2, T.TTOT, T.BW), F32)
    return pl.pallas_call(
        kern, grid=(2, NS), name="scan_bwd",
        in_specs=[sh, sh, sh, sh, dr, dr, dr, _bs((1, SB, R, 128), lambda d, g: (d, NS - 1 - g, 0, 0)),
                  _bs((1, R, 128), lambda d, g: (d, 0, 0))],
        out_specs=[dr] * 6,
        out_shape=[o2] * 6,
        scratch_shapes=[pltpu.VMEM((R, 128), F32), pltpu.VMEM((R, 128), F32)],
        compiler_params=_cp(("arbitrary", "arbitrary"), mb=48),
    )(dy, r, v, kk, dec, kd, bb, hist, fin)


def _prep_bwd(T, rw, dr_s, ddec, dkd, dv_s, da_s, dbb, dr_ro, dv_ro, dkbar, dgd, w0, w2, a0, a2, k_k, k_a):
    def kern(rw_ref, drs_ref, ddec_ref, dkd_ref, dvs_ref, das_ref, dbb_ref, drr_ref, dvr_ref, dkb_ref, dgd_ref,
             w0_ref, w2_ref, a0_ref, a2_ref, kk_ref, ka_ref,
             drw_o, dw0_o, dw2_o, da0_o, da2_o, dkk_o, dka_o):
        i = pl.program_id(0)
        first = i == 0
        e = _e128(BF16)
        w0v, w2v, a0v, a2v = _load_prep_params(w0_ref, w2_ref, a0_ref, a2_ref)
        k_k = kk_ref[...]
        k_a = ka_ref[...]
        o = _prep_math(rw_ref[...], w0v, w2v, a0v, a2v, k_k, k_a, e)
        k, kk = o["k"], o["kk"]
        dkbh = 0.5 * dkb_ref[...]
        dk = jnp.zeros_like(k)
        dkk = -(das_ref[0] + das_ref[1])
        dka = jnp.zeros((1, W), F32)
        dwd, dad = [], []
        for d in (0, 1):
            iclr = o["iclr"][d]
            dkd_d = dkd_ref[d] + dkbh
            dbb_d = dbb_ref[d]
            dk = dk + dkd_d * (1.0 + (iclr - 1.0) * k_a)
            dka = dka + _colsum(dkd_d * k * (iclr - 1.0))
            dkk = dkk + dbb_d * iclr
            dicl = dkd_d * k * k_a + dbb_d * kk
            dpa = dicl * iclr * (1.0 - iclr)
            _acc(da0_o.at[d:d + 1, :], _colsum(dpa), first)
            dad.append(_bdot(dpa, a2v[d], _NT_DIMS))
            _acc(da2_o.at[d], _bdot(o["ad"][d], dpa, _TN_DIMS), first)
            dpre = -ddec_ref[d] * o["dec"][d] * o["ex"][d] * _sigmoid(-o["pre"][d])
            _acc(dw0_o.at[d:d + 1, :], _colsum(dpre), first)
            th = o["th"][d]
            dth = _bdot(dpre, w2v[d], _NT_DIMS)
            _acc(dw2_o.at[d], _bdot(th, dpre, _TN_DIMS), first)
            dwd.append(dth * (1.0 - th * th))
        inv = o["inv"]
        kr = o["kr"]
        proj = _segsum(dkk * kr, e)
        dkr = dkk * inv - jnp.where(o["rt"] > 1e-12, kr * inv * inv * inv * proj, 0.0)
        dk = dk + dkr * k_k
        _acc(dkk_o, _colsum(dkr * k), first)
        _acc(dka_o, dka, first)
        dr = drs_ref[0] + drs_ref[1] + drr_ref[...]
        dv = dvs_ref[0] + dvs_ref[1] + dvr_ref[...]
        drw_o[...] = jnp.concatenate([dr, dk, dv, dwd[0], dwd[1], dad[0], dad[1], dgd_ref[...]], axis=1)

    tok = lambda i: (i, 0)
    f32s = lambda *s: jax.ShapeDtypeStruct(s, F32)
    p2 = lambda i: (0, 0)
    p3 = lambda i: (0, 0, 0)
    return pl.pallas_call(
        kern, grid=(T.NT,), name="prep_bwd",
        in_specs=[_bs((TT, RWC), tok)] + [T.tm2_spec()] * 6 + [_bs((TT, W), tok)] * 3 + [_bs((TT, GDW), tok)]
        + _prep_param_specs(),
        out_specs=[_bs((TT, RWC), tok), _bs((2, W), p2), _bs((2, LRW, W), p3), _bs((2, W), p2),
                   _bs((2, LRW, W), p3), _row(W), _row(W)],
        out_shape=[f32s(T.NTOK, RWC), f32s(2, W), f32s(2, LRW, W), f32s(2, W), f32s(2, LRW, W), f32s(1, W), f32s(1, W)],
        compiler_params=_cp(("arbitrary",), mb=56),
    )(rw, dr_s, ddec, dkd, dv_s, da_s, dbb, dr_ro, dv_ro, dkbar, dgd, w0, w2, a0, a2, k_k, k_a)


def _shift_bwd(T, drw, p_rw, mu_p, mu_n):
    def kern(d_ref, dp_ref, dn_ref, p_ref, pp_ref, pn_ref, mp_ref, mn_ref, dprw_o, dmp_o, dmn_o):
        i = pl.program_id(0)
        first = i == 0
        has_prev, has_next = _halo_masks(T, i)
        mp = mp_ref[...]
        mn = mn_ref[...]
        drw = d_ref[...]
        z = p_ref[...]
        zprev, znext = _neighbours(z, pp_ref[7:8, :] * has_prev, pn_ref[0:1, :] * has_next)
        _acc(dmp_o, _colsum(drw * (zprev - z)), first)
        _acc(dmn_o, _colsum(drw * (znext - z)), first)
        dprev, dnext = _neighbours(drw, dp_ref[7:8, :] * has_prev, dn_ref[0:1, :] * has_next)
        dprw_o[...] = (drw * (1.0 - mp - mn) + mp * dnext + mn * dprev).astype(BF16)

    tok = lambda i: (i, 0)
    prev, nxt = _halo_specs(T)
    f32s = lambda *s: jax.ShapeDtypeStruct(s, F32)
    return pl.pallas_call(
        kern, grid=(T.NT,), name="shift_bwd",
        in_specs=[_bs((TT, RWC), tok), prev, nxt, _bs((TT, RWC), tok), prev, nxt, _row(RWC), _row(RWC)],
        out_specs=[_bs((TT, RWC), tok), _row(RWC), _row(RWC)],
        out_shape=[jax.ShapeDtypeStruct((T.NTOK, RWC), BF16), f32s(1, RWC), f32s(1, RWC)],
        compiler_params=_cp(("arbitrary",), mb=56),
    )(drw, drw, drw, p_rw, p_rw, p_rw, mu_p, mu_n)


def _mix_in_bwd(T, dp_rw, dp_cv, x2, c2, dx1, modrows, g, w_rw, w_cv):
    def kern(drw_ref, dcv_ref, x_ref, c_ref, dx1_ref, mod_ref, g_ref, wr_ref, wc_ref, dxc_o, dmod_o, dg_o):
        i = pl.program_id(0)
        dh = (lax.dot_general(drw_ref[...], wr_ref[...], _NT_DIMS, preferred_element_type=F32)
              + lax.dot_general(dcv_ref[...], wc_ref[...], _NT_DIMS, preferred_element_type=F32))
        x = _tok_tile(T, x_ref, c_ref)
        s = lax.rsqrt(_rowmean(x * x) + EPS_RMS)
        xh = x * s
        gv = g_ref[...]
        q = i % T.TPS
        first_kind = jnp.logical_or(q == 0, q == 1)
        _acc(dmod_o.at[0, 0, 0:1, :], _colsum(dh), first_kind)
        _acc(dmod_o.at[0, 0, 1:2, :], _colsum(dh * (xh * gv)), first_kind)
        dn1 = dh * (1.0 + mod_ref[0, 0, 1:2, :])
        _acc(dg_o, _colsum(dn1 * xh), i == 0)
        dxh = dn1 * gv
        dxc_o[...] = dx1_ref[...] + s * (dxh - xh * _rowmean(dxh * xh))

    tok = lambda i: (i, 0)
    f32s = lambda *s: jax.ShapeDtypeStruct(s, F32)
    return pl.pallas_call(
        kern, grid=(T.NT,), name="mix_in_bwd",
        in_specs=[_bs((TT, RWC), tok), _bs((TT, CVC), tok)] + _tok_specs(T) + [
            _bs((TT, D), lambda i: (T.lat(i), 0)), T.mod_spec(), _row(D), _bs((D, RWC), lambda i: (0, 0)),
            _bs((D, CVC), lambda i: (0, 0))],
        out_specs=[_bs((TT, D), lambda i: (T.lat(i), 0)),
                   _bs((1, 1, 2, D), lambda i: (i // T.TPS, jnp.minimum(i % T.TPS, 1), 0, 0)), _row(D)],
        out_shape=[f32s(T.NLAT, D), f32s(T.B, 2, 2, D), f32s(1, D)],
        compiler_params=_cp(("arbitrary",)),
    )(dp_rw, dp_cv, x2, c2, dx1, modrows, g, w_rw, w_cv)


def _matmul_tn(a, b, name, tk, nk, tn, amap=None, bmap=None, tm=1024):
    M = a.shape[1]
    N = b.shape[1]
    amap = amap or (lambda k: k)
    bmap = bmap or (lambda k: k)

    def kern(a_ref, b_ref, o_ref):
        _acc(o_ref, lax.dot_general(a_ref[...], b_ref[...], _TN_DIMS, preferred_element_type=F32),
             pl.program_id(2) == 0)

    return pl.pallas_call(
        kern, grid=(M // tm, N // tn, nk), name=name,
        in_specs=[_bs((tk, tm), lambda i, j, k: (amap(k), i)), _bs((tk, tn), lambda i, j, k: (bmap(k), j))],
        out_specs=_bs((tm, tn), lambda i, j, k: (i, j)),
        out_shape=jax.ShapeDtypeStruct((M, N), F32),
        compiler_params=_cp(("parallel", "parallel", "arbitrary")),
    )(a, b)


def _silu(x):
    return x * _sigmoid(x)


def _ada_fwd(c_all, c_ctx, ada_w, ada_b_blk):
    nb = c_all.shape[0]
    R = nb + 8
    ncol = ada_w.shape[1]

    def kern(c_ref, cc_ref, w_ref, b_ref, o_ref):
        lhs = jnp.concatenate([_silu(c_ref[...]), _silu(cc_ref[...]), jnp.zeros((7, D), F32)], axis=0)
        o_ref[...] = jnp.dot(lhs, w_ref[...], precision=HI, preferred_element_type=F32) + b_ref[...]

    return pl.pallas_call(
        kern, name="ada_fwd", out_shape=jax.ShapeDtypeStruct((R, ncol), F32),
        compiler_params=_cp(None, 40),
    )(c_all, c_ctx, ada_w, ada_b_blk)


def _ada_bwd(c_all, c_ctx, ada_w, ex, cx, ex_blk, cx_blk):
    nb = c_all.shape[0]
    ncol = ada_w.shape[1]

    def kern(c_ref, cc_ref, w_ref, ex_ref, cx_ref, exb_ref, cxb_ref, gw_o, gb_o, ds_o):
        lhs = jnp.concatenate([_silu(c_ref[...]), _silu(cc_ref[...]), jnp.zeros((7, D), F32)], axis=0)
        dmc_blk = _colsum(cxb_ref[...])
        rhs = jnp.concatenate([exb_ref[...], dmc_blk, jnp.zeros((7, ncol), F32)], axis=0)
        gw_o[...] = lax.dot_general(lhs, rhs, _TN_DIMS, precision=HI, preferred_element_type=F32)
        gb_o[...] = _colsum(ex_ref[...]) + _colsum(cx_ref[...])
        ds_o[...] = lax.dot_general(jnp.concatenate([dmc_blk, jnp.zeros((7, ncol), F32)], axis=0), w_ref[...],
                                    _NT_DIMS, precision=HI, preferred_element_type=F32)

    return pl.pallas_call(
        kern, name="ada_bwd",
        out_shape=[jax.ShapeDtypeStruct((D, ncol), F32), jax.ShapeDtypeStruct((1, ex.shape[1]), F32),
                   jax.ShapeDtypeStruct((8, D), F32)],
        compiler_params=_cp(None, 48),
    )(c_all, c_ctx, ada_w, ex, cx, ex_blk, cx_blk)


def _cctx_final(parts, c_ctx):
    def kern(p_ref, c_ref, o_ref):
        tot = p_ref[0, 0:1, :]
        for j in range(1, parts.shape[0]):
            tot = tot + p_ref[j, 0:1, :]
        c = c_ref[...]
        sg = _sigmoid(c)
        o_ref[...] = tot * (sg * (1.0 + c * (1.0 - sg)))

    return pl.pallas_call(kern, name="cctx_final", out_shape=jax.ShapeDtypeStruct((1, D), F32))(parts, c_ctx)


def _peer(kind, p, ix, iy, ic):
    if kind == "chips":
        return (p // 2, p % 2, ic)
    if kind == "all":
        return (p // 4, (p // 2) % 2, p % 2)
    return (ix, iy, p)


def _exchange(x, kind, bcast, name, chunks=1):
    npeer = {"chips": 4, "all": 8, "sib": 2}[kind]
    slab = x.shape if bcast else x.shape[1:]
    assert chunks == 1 or slab[0] == chunks

    def kern(x_ref, o_ref, send_sems, recv_sems, lsem):
        ix, iy, ic = lax.axis_index("x"), lax.axis_index("y"), lax.axis_index("c")
        me = {"chips": 2 * ix + iy, "all": 4 * ix + 2 * iy + ic, "sib": ic}[kind]
        own = pltpu.make_async_copy(x_ref if bcast else x_ref.at[me], o_ref.at[me], lsem)
        own.start()

        def part(ref, k):
            return ref if chunks == 1 else ref.at[k]

        def copy(p, k):
            return pltpu.make_async_remote_copy(
                src_ref=part(x_ref if bcast else x_ref.at[p], k), dst_ref=part(o_ref.at[me], k),
                send_sem=send_sems.at[p, k], recv_sem=recv_sems.at[me, k],
                device_id=_peer(kind, p, ix, iy, ic), device_id_type=MESH)

        def arrival(p, k):
            return pltpu.make_async_remote_copy(
                src_ref=part(x_ref if bcast else x_ref.at[p], k), dst_ref=part(o_ref.at[p], k),
                send_sem=send_sems.at[p, k], recv_sem=recv_sems.at[p, k],
                device_id=_peer(kind, p, ix, iy, ic), device_id_type=MESH)

        for p in range(npeer):
            @pl.when(me != p)
            def _():
                for k in range(chunks):
                    copy(p, k).start()
        for p in range(npeer):
            @pl.when(me != p)
            def _():
                for k in range(chunks):
                    arrival(p, k).wait_recv()
        for p in range(npeer):
            @pl.when(me != p)
            def _():
                for k in range(chunks):
                    copy(p, k).wait_send()
        own.wait()

    any_spec = pl.BlockSpec(memory_space=pl.ANY)
    return pl.pallas_call(
        kern, name=name, in_specs=[any_spec], out_specs=any_spec,
        out_shape=jax.ShapeDtypeStruct((npeer,) + tuple(slab), x.dtype),
        scratch_shapes=[pltpu.SemaphoreType.DMA((npeer, chunks)), pltpu.SemaphoreType.DMA((npeer, chunks)),
                        pltpu.SemaphoreType.DMA],
    )(x)


_HBM = pl.BlockSpec(memory_space=pltpu.HBM)
_SEM = pl.BlockSpec(memory_space=pltpu.SEMAPHORE)
_FLOWS = pltpu.SideEffectType.DATAFLOW_SIDE_EFFECTING


def _other_chips(ix, iy, ic):
    return ((1 - ix, iy, ic), (ix, 1 - iy, ic), (1 - ix, 1 - iy, ic))


def _chip_index(dev):
    return 2 * dev[0] + dev[1]


def _gather_start(x, name, scatter=False):
    def kern(x_ref, land_ref, send_sems, recv_sems, x_thru, land_thru, token):
        ix, iy, ic = lax.axis_index("x"), lax.axis_index("y"), lax.axis_index("c")
        me = 2 * ix + iy
        for k, peer in enumerate(_other_chips(ix, iy, ic)):
            src = x_ref.at[_chip_index(peer)] if scatter else x_ref
            pltpu.make_async_remote_copy(src_ref=src, dst_ref=land_ref.at[me], send_sem=send_sems.at[k],
                                         recv_sem=recv_sems.at[k], device_id=peer, device_id_type=MESH).start()
        token[...] = jnp.zeros(token.shape, token.dtype)

    land = lax.empty(x.shape if scatter else (4,) + x.shape, x.dtype)
    return pl.pallas_call(
        kern, name=name,
        out_shape=(pltpu.SemaphoreType.DMA((3,)), pltpu.SemaphoreType.DMA((3,)), pltpu.HBM(x.shape, x.dtype),
                   pltpu.HBM(land.shape, land.dtype), jax.ShapeDtypeStruct((8, 128), F32)),
        in_specs=(_HBM, _HBM), out_specs=(_SEM, _SEM, _HBM, _HBM, pl.BlockSpec(memory_space=pltpu.VMEM)),
        input_output_aliases={0: 2, 1: 3},
        compiler_params=pltpu.CompilerParams(has_side_effects=_FLOWS),
    )(pltpu.with_memory_space_constraint(x, pltpu.HBM), pltpu.with_memory_space_constraint(land, pltpu.HBM))


def _gather_wait(send_sems, recv_sems, x_thru, land_thru, after, name, scatter=False):
    def kern(x_ref, land_ref, send_sems_ref, recv_sems_ref, after_ref, x_dead, land_out):
        ix, iy, ic = lax.axis_index("x"), lax.axis_index("y"), lax.axis_index("c")
        for k, peer in enumerate(_other_chips(ix, iy, ic)):
            src = x_ref.at[_chip_index(peer)] if scatter else x_ref
            copy = pltpu.make_async_remote_copy(src_ref=src, dst_ref=land_ref.at[_chip_index(peer)],
                                                send_sem=send_sems_ref.at[k], recv_sem=recv_sems_ref.at[k],
                                                device_id=peer, device_id_type=MESH)
            copy.wait_send()
            copy.wait_recv()

    return pl.pallas_call(
        kern, name=name,
        out_shape=(pltpu.HBM(x_thru.shape, x_thru.dtype), pltpu.HBM(land_thru.shape, land_thru.dtype)),
        in_specs=(_HBM, _HBM, _SEM, _SEM, pl.BlockSpec(memory_space=pl.ANY)), out_specs=(_HBM, _HBM),
        input_output_aliases={0: 0, 1: 1},
        compiler_params=pltpu.CompilerParams(has_side_effects=_FLOWS),
    )(x_thru, land_thru, send_sems, recv_sems, after)


def _sum_slots(x, name):
    n, R, C = x.shape
    budget = (8 << 20) // (n * C * x.dtype.itemsize)
    tr = max([t for t in range(8, R + 1, 8) if R % t == 0 and t <= max(budget, 8)], default=R)

    def kern(x_ref, o_ref):
        tot = x_ref[0]
        for s in range(1, n):
            tot = tot + x_ref[s]
        o_ref[...] = tot

    return pl.pallas_call(
        kern, grid=(R // tr,), name=name,
        in_specs=[_bs((n, tr, C), lambda i: (0, i, 0))], out_specs=_bs((tr, C), lambda i: (i, 0)),
        out_shape=jax.ShapeDtypeStruct((R, C), x.dtype), compiler_params=_cp(("parallel",)),
    )(x)


def _sib_stream(x, me, name, add, nck=1):
    if add:
        nslab, rows, C = x.shape
        R = rows // (2 * nck)
        assert R * 2 * nck == rows and R % 8 == 0
        K = nslab * nck
    else:
        K, R, C = x.shape

    def kern(me_ref, *refs):
        if add:
            own_ref, send_ref, o_ref, rbuf, ssem, rsem, credit = refs
        else:
            send_ref, o_ref, rbuf, ssem, rsem, credit = refs
        k = pl.program_id(0)
        slot = k % 2
        sib = (lax.axis_index("x"), lax.axis_index("y"), 1 - lax.axis_index("c"))

        @pl.when(k >= 2)
        def _():
            pl.semaphore_wait(credit.at[slot], 1)

        cp = pltpu.make_async_remote_copy(src_ref=send_ref.at[0], dst_ref=rbuf.at[slot], send_sem=ssem.at[slot],
                                          recv_sem=rsem.at[slot], device_id=sib, device_id_type=MESH)
        cp.start()
        cp.wait_recv()
        o_ref[0] = own_ref[0] + rbuf[slot] if add else rbuf[slot]
        cp.wait_send()

        @pl.when(k + 2 < K)
        def _():
            pl.semaphore_signal(credit.at[slot], 1, device_id=sib, device_id_type=MESH)

    if add:
        in_specs = [_bs((1, R, C), lambda k, me_ref: (k // nck, me_ref[0] * nck + k % nck, 0)),
                    _bs((1, R, C), lambda k, me_ref: (k // nck, (1 - me_ref[0]) * nck + k % nck, 0))]
        args = (x, x)
    else:
        in_specs = [_bs((1, R, C), lambda k, me_ref: (k, 0, 0))]
        args = (x,)
    return pl.pallas_call(
        kern, name=name,
        grid_spec=pltpu.PrefetchScalarGridSpec(
            num_scalar_prefetch=1, grid=(K,), in_specs=in_specs,
            out_specs=_bs((1, R, C), lambda k, me_ref: (k, 0, 0)),
            scratch_shapes=[pltpu.VMEM((2, R, C), x.dtype), pltpu.SemaphoreType.DMA((2,)),
                            pltpu.SemaphoreType.DMA((2,)), pltpu.SemaphoreType.REGULAR((2,))]),
        out_shape=jax.ShapeDtypeStruct((K, R, C), x.dtype),
        compiler_params=_cp(("arbitrary",)),
    )(me, *args)


def _adamw(w, g, m, v, name):
    shape = w.shape
    if len(shape) == 1:
        outs = _adamw(*(t.reshape(1, -1) for t in (w, g, m, v)), name)
        return tuple(t.reshape(shape) for t in outs)
    nd = len(shape)
    size = 1
    for s in shape:
        size *= s
    rows = shape[-2]
    tr = rows
    if size > (1 << 18) and all(s == 1 for s in shape[:-2]):
        tr = max(t for t in (256, 128, 64, 32, 16, 8) if rows % t == 0)
    c1 = 1.0 - ADAM_B1 ** ADAM_STEP
    c2 = 1.0 - ADAM_B2 ** ADAM_STEP

    def kern(w_ref, g_ref, m_ref, v_ref, d_o, m_o, v_o):
        gv = g_ref[...]
        mn = ADAM_B1 * m_ref[...] + (1.0 - ADAM_B1) * gv
        vn = ADAM_B2 * v_ref[...] + (1.0 - ADAM_B2) * (gv * gv)
        m_o[...] = mn
        v_o[...] = vn
        d_o[...] = -ADAM_LR * ((mn / c1) / (jnp.sqrt(vn / c2) + ADAM_EPS) + ADAM_WD * w_ref[...])

    spec = _bs(shape[:-2] + (tr, shape[-1]), lambda i: (0,) * (nd - 2) + (i, 0))
    o = jax.ShapeDtypeStruct(shape, F32)
    return tuple(pl.pallas_call(
        kern, grid=(rows // tr,), name=name, in_specs=[spec] * 4, out_specs=[spec] * 3, out_shape=[o, o, o],
        compiler_params=_cp(("parallel",)),
    )(w, g, m, v))


_WEIGHT_NAMES = ("c_ctx", "ada_w", "ada_b", "mix_pre_g", "mix_post_g", "mlp_pre_g", "mlp_post_g", "w_in", "mu_prev",
                 "mu_next", "decay_w0", "decay_w2", "iclr_a0", "iclr_a2", "k_k", "k_a", "r_k", "gate_w2", "lnx_w",
                 "lnx_b", "conv_w", "conv_b", "conv_ln_w", "conv_ln_b", "w_out", "mlp_w1", "mlp_w2")


def _pack_rows(parts, cols=512):
    flat = jnp.concatenate([p.reshape(-1) for p in parts])
    rows = -(-flat.shape[0] // cols)
    rows = -(-rows // 16) * 16
    flat = jnp.pad(flat, (0, rows * cols - flat.shape[0]))
    return flat.reshape(rows, cols)


def _unpack(flat, shapes):
    out = []
    off = 0
    for s in shapes:
        n = 1
        for d in s:
            n *= d
        out.append(flat[off:off + n].reshape(s))
        off += n
    return out


def _local_step(T, x2, c2, tgt, modrows, P, late_weights, early_grads):
    p_rw, p_cv, h = _mix_in(T, x2, c2, modrows, P["mix_pre_g"], P["w_rw"], P["w_cv"])
    prep_params = (P["w0"], P["w2"], P["a0"], P["a2"], P["k_k"], P["k_a"])
    r, v, kk, dec, kd, bb, rw = _rwkv_prep(T, p_rw, P["mu_p"], P["mu_n"], *prep_params)
    y, fin, hist = _scan_fwd(T, r, v, kk, dec, kd, bb)
    P = dict(P, **late_weights(fin))
    ro_params = (P["r_k"], P["gate_w2"], P["lnx_w"], P["lnx_b"])
    cv_params = (P["conv_w"], P["conv_b"], P["conv_ln_w"], P["conv_ln_b"])
    cat, mix, x1, conv = _mix_out(T, y, kd, rw, p_cv, x2, modrows, *ro_params, *cv_params, P["mix_post_g"], P["w_out"])
    h2, loss_acc, dm, dx2, dg2, d_mlp_post = _mlp_fwd(T, x1, tgt, modrows, P["mlp_pre_g"], P["mlp_post_g"], P["wl"])
    fact, da, dx1, dmod2, d_mlp_pre = _mlp_bwd(T, h2, dm, x1, dx2, modrows, P["mlp_pre_g"], P["wl"])
    dmix, dcat, dg1, d_mix_post = _mix_post_bwd(T, dx1, mix, modrows, P["mix_post_g"], P["w_out"])
    kl = max(t for t in (1024, 512, 256) if T.NLAT % t == 0)
    dw_out = _matmul_tn(cat, dmix, "dw_out", TT, T.NL, 1024, bmap=T.tok)
    dw1 = _matmul_tn(h2, da, "dw_mlp1", kl, T.NLAT // kl, 1024)
    dw2m = _matmul_tn(fact, dm, "dw_mlp2", kl, T.NLAT // kl, 1024)
    fin = fin + early_grads(dw1, dw2m, dw_out)
    dp_cv, d_conv_w, d_conv_b, d_cln_w, d_cln_b = _conv_bwd(T, dcat, p_cv, conv, *cv_params)
    dy, dr_ro, dv_ro, dkbar, dgd, d_r_k, d_gate, d_lnx_w, d_lnx_b = _readout_bwd(T, dcat, y, kd, rw, *ro_params)
    dr_s, ddec, dkd, dv_s, da_s, dbb = _scan_bwd(T, dy, r, v, kk, dec, kd, bb, hist, fin)
    drw, d_w0, d_w2, d_a0, d_a2, d_k_k, d_k_a = _prep_bwd(T, rw, dr_s, ddec, dkd, dv_s, da_s, dbb, dr_ro, dv_ro,
                                                          dkbar, dgd, *prep_params)
    dp_rw, d_mu_p, d_mu_n = _shift_bwd(T, drw, p_rw, P["mu_p"], P["mu_n"])
    dxc, dmod1, d_mix_pre = _mix_in_bwd(T, dp_rw, dp_cv, x2, c2, dx1, modrows, P["mix_pre_g"], P["w_rw"], P["w_cv"])
    kt = max(t for t in (1024, 768, 512, 256) if T.NTOK % t == 0)
    dw_rw = _matmul_tn(h, dp_rw, "dw_in_rw", kt, T.NTOK // kt, 768)
    dw_cv = _matmul_tn(h, dp_cv, "dw_in_cv", kt, T.NTOK // kt, 1024)
    small = dict(mix_pre_g=d_mix_pre, mix_post_g=d_mix_post, mlp_pre_g=d_mlp_pre, mlp_post_g=d_mlp_post,
                 mu_p=d_mu_p, mu_n=d_mu_n, w0=d_w0, w2=d_w2, a0=d_a0, a2=d_a2, k_k=d_k_k, k_a=d_k_a, r_k=d_r_k,
                 gate_w2=d_gate, lnx_w=d_lnx_w, lnx_b=d_lnx_b, conv_w=d_conv_w, conv_b=d_conv_b,
                 conv_ln_w=d_cln_w, conv_ln_b=d_cln_b)
    big = dict(w_rw=dw_rw, w_cv=dw_cv, w_out=dw_out, w1=dw1, w2m=dw2m, after_scan=dr_s)
    dmods = dict(dmod1=dmod1, dg1=dg1, dmod2=dmod2, dg2=dg2)
    return loss_acc[0, 0], dxc, small, big, dmods


_SMALL_ORDER = ("mix_pre_g", "mix_post_g", "mlp_pre_g", "mlp_post_g", "mu_p", "mu_n", "w0", "w2", "a0", "a2", "k_k",
                "k_a", "r_k", "gate_w2", "lnx_w", "lnx_b", "conv_w", "conv_b", "conv_ln_w", "conv_ln_b")


def kernel(x, c, ctx, c_ctx, ada_w, ada_b, mix_pre_g, mix_post_g, mlp_pre_g, mlp_post_g, w_in, mu_prev, mu_next, decay_w0, decay_w2, iclr_a0, iclr_a2, k_k, k_a, r_k, gate_w2, lnx_w, lnx_b, conv_w, conv_b, conv_ln_w, conv_ln_b, w_out, mlp_w1, mlp_w2, loss_target, m_c_ctx, m_ada_w, m_ada_b, m_mix_pre_g, m_mix_post_g, m_mlp_pre_g, m_mlp_post_g, m_w_in, m_mu_prev, m_mu_next, m_decay_w0, m_decay_w2, m_iclr_a0, m_iclr_a2, m_k_k, m_k_a, m_r_k, m_gate_w2, m_lnx_w, m_lnx_b, m_conv_w, m_conv_b, m_conv_ln_w, m_conv_ln_b, m_w_out, m_mlp_w1, m_mlp_w2, v_c_ctx, v_ada_w, v_ada_b, v_mix_pre_g, v_mix_post_g, v_mlp_pre_g, v_mlp_post_g, v_w_in, v_mu_prev, v_mu_next, v_decay_w0, v_decay_w2, v_iclr_a0, v_iclr_a2, v_k_k, v_k_a, v_r_k, v_gate_w2, v_lnx_w, v_lnx_b, v_conv_w, v_conv_b, v_conv_ln_w, v_conv_ln_b, v_w_out, v_mlp_w1, v_mlp_w2):
    weights = dict(zip(_WEIGHT_NAMES, (c_ctx, ada_w, ada_b, mix_pre_g, mix_post_g, mlp_pre_g, mlp_post_g, w_in, mu_prev, mu_next, decay_w0, decay_w2, iclr_a0, iclr_a2, k_k, k_a, r_k, gate_w2, lnx_w, lnx_b, conv_w, conv_b, conv_ln_w, conv_ln_b, w_out, mlp_w1, mlp_w2)))
    moms = dict(zip(_WEIGHT_NAMES, (m_c_ctx, m_ada_w, m_ada_b, m_mix_pre_g, m_mix_post_g, m_mlp_pre_g, m_mlp_post_g, m_w_in, m_mu_prev, m_mu_next, m_decay_w0, m_decay_w2, m_iclr_a0, m_iclr_a2, m_k_k, m_k_a, m_r_k, m_gate_w2, m_lnx_w, m_lnx_b, m_conv_w, m_conv_b, m_conv_ln_w, m_conv_ln_b, m_w_out, m_mlp_w1, m_mlp_w2)))
    vars_ = dict(zip(_WEIGHT_NAMES, (v_c_ctx, v_ada_w, v_ada_b, v_mix_pre_g, v_mix_post_g, v_mlp_pre_g, v_mlp_post_g, v_w_in, v_mu_prev, v_mu_next, v_decay_w0, v_decay_w2, v_iclr_a0, v_iclr_a2, v_k_k, v_k_a, v_r_k, v_gate_w2, v_lnx_w, v_lnx_b, v_conv_w, v_conv_b, v_conv_ln_w, v_conv_ln_b, v_w_out, v_mlp_w1, v_mlp_w2)))

    B, t_lat, _ = x.shape
    assert ctx.shape[1] == TT and t_lat % TT == 0 and (t_lat * B) % MT == 0
    T = _Tiles(B, t_lat)
    ix, iy, ic = lax.axis_index("x"), lax.axis_index("y"), lax.axis_index("c")
    chip = 2 * ix + iy
    dev = 4 * ix + 2 * iy + ic
    nsh = 4
    in_sh = w_in.shape[2]
    ada_sh = ada_w.shape[2]
    lane_sh = decay_w0.shape[2]

    in_sems_s, in_sems_r, in_x, in_land, in_token = _gather_start(w_in[0].astype(BF16), "gather_w_in_start")
    late_pack = jnp.concatenate([mlp_w1[0], mlp_w2[0], w_out[0]], axis=0).astype(BF16)
    late_sems_s, late_sems_r, late_x, late_land, late_token = _gather_start(late_pack, "gather_mlp_start")
    n_w1, n_w2 = mlp_w1.shape[1], mlp_w2.shape[1]

    def late_weights(after):
        own, land = _gather_wait(late_sems_s, late_sems_r, late_x, late_land, after, "gather_mlp_wait")
        wl = lax.dynamic_update_slice(land, own[None], (chip, 0, 0))
        return dict(wl=wl, w_out=jnp.concatenate([wl[j, n_w1 + n_w2:] for j in range(nsh)], axis=0))

    sm_parts = (decay_w0[0], decay_w2[0], iclr_a0[0], iclr_a2[0], gate_w2[0], conv_w[0])
    sm_shapes = [p.shape for p in sm_parts]
    sg = _exchange(_pack_rows(sm_parts), "chips", True, "gather_small_weights")
    pers = [_unpack(sg[j].reshape(-1), sm_shapes) for j in range(nsh)]
    w0_f, w2_f_, a0_f, a2_f, gate_f, convw_f = (jnp.concatenate([pers[j][t] for j in range(nsh)], axis=-1)
                                                for t in range(6))

    def pad_rows(a, n):
        return jnp.pad(a, [(0, 0)] * (a.ndim - 2) + [(0, n - a.shape[-2]), (0, 0)])

    c_ctx2 = c_ctx.reshape(1, D)
    c_all = _exchange(jnp.pad(c + in_token[0, 0], ((0, 8 - B), (0, 0))), "all", True, "gather_c")[:, :B]
    c_all = c_all.reshape(8 * B, D)
    ada_b_blk = lax.dynamic_slice(ada_b, (0, chip * ada_sh), (1, ada_sh))
    mod_blk = _ada_fwd(c_all, c_ctx2, ada_w[0], ada_b_blk)
    mod_g = _exchange(mod_blk, "chips", True, "gather_mod")
    mod_all = jnp.concatenate([mod_g[j] for j in range(nsh)], axis=1)
    mod_x = lax.dynamic_slice(mod_all, (dev * B, 0), (B, 6 * D)).reshape(B, 6, D)
    mod_c = jnp.broadcast_to(mod_all[8 * B].reshape(1, 6, D), (B, 6, D))
    modrows = jnp.stack([mod_c, mod_x], axis=1) + late_token[0, 0]

    own_in, land_in = _gather_wait(in_sems_s, in_sems_r, in_x, in_land, mod_all, "gather_w_in_wait")
    wg_in = lax.dynamic_update_slice(land_in, own_in[None], (chip, 0, 0))
    w_in_f = jnp.concatenate([wg_in[j] for j in range(nsh)], axis=1)
    w_in_p = _pad_cols(w_in_f, w_in_f.shape[1])
    P = dict(
        w_rw=w_in_p[:, :RWC], w_cv=w_in_p[:, RWC:],
        mix_pre_g=mix_pre_g, mix_post_g=mix_post_g, mlp_pre_g=mlp_pre_g, mlp_post_g=mlp_post_g,
        mu_p=_pad_cols(mu_prev, mu_prev.shape[1]), mu_n=_pad_cols(mu_next, mu_next.shape[1]),
        w0=w0_f, w2=pad_rows(w2_f_, LRW), a0=a0_f, a2=pad_rows(a2_f, LRW), k_k=k_k, k_a=k_a,
        r_k=r_k.reshape(1, W), gate_w2=pad_rows(gate_f, GDW), lnx_w=lnx_w, lnx_b=lnx_b,
        conv_w=pad_rows(convw_f, 32), conv_b=conv_b, conv_ln_w=conv_ln_w, conv_ln_b=conv_ln_b)

    x2 = x.reshape(T.NLAT, D)
    c2 = ctx.reshape(B * TT, D)
    tgt = loss_target.reshape(T.NLAT, D)
    me1 = ic.reshape(1).astype(jnp.int32)
    early = {}

    def early_grads(dw1, dw2, dw_out):
        c_1, n_o = mlp_w1.shape[2], w_out.shape[1]
        slabs = jnp.stack([jnp.concatenate([dw1[:, c_1 * j:c_1 * (j + 1)], dw2[n_w2 * j:n_w2 * (j + 1)],
                                            dw_out[n_o * j:n_o * (j + 1)]], axis=0) for j in range(nsh)])
        pair = _sib_stream(slabs, me1, "sib_reduce_mlp_grads", True, nck=3)
        send, recv, x_thru, land, token = _gather_start(pair.reshape(nsh, -1, slabs.shape[2]), "reduce_mlp_start",
                                                        scatter=True)
        early.update(send=send, recv=recv, x=x_thru, land=land)
        return token[0, 0]

    loss_loc, dxl, small, big, dm_ = _local_step(T, x2, c2, tgt, modrows, P, late_weights, early_grads)
    loss = lax.psum(loss_loc, ("x", "y", "c"))
    grad_x = dxl.reshape(x.shape)

    dw_in_f = _unpad_cols(jnp.concatenate([big["w_rw"], big["w_cv"]], axis=1), w_in_f.shape[1])
    slabs_in = jnp.stack([dw_in_f[:, in_sh * j:in_sh * (j + 1)] for j in range(nsh)])
    pair_in = _sib_stream(slabs_in, me1, "sib_reduce_w_in", True, nck=2).reshape(nsh, -1, in_sh)
    win_s, win_r, win_x, win_land, win_token = _gather_start(pair_in, "reduce_w_in_start", scatter=True)

    dmod_x = jnp.concatenate([dm_["dmod1"][:, 1], dm_["dg1"], dm_["dmod2"], dm_["dg2"]], axis=1)
    dmod_c = jnp.concatenate([dm_["dmod1"][:, 0], jnp.zeros((B, 4, D), F32)], axis=1)
    dpack = jnp.concatenate([dmod_x.reshape(B, 6 * D), dmod_c.reshape(B, 6 * D)], axis=0)
    dpack = dpack + win_token[0, 0]
    dg = _exchange(dpack, "all", True, "gather_dmod")
    ex = dg[:, :B].reshape(8 * B, 6 * D)
    cx = dg[:, B:].reshape(8 * B, 6 * D)
    ex_blk = lax.dynamic_slice(ex, (0, chip * ada_sh), (8 * B, ada_sh))
    cx_blk = lax.dynamic_slice(cx, (0, chip * ada_sh), (8 * B, ada_sh))
    g_ada_w, g_ada_b, dscc = _ada_bwd(c_all, c_ctx2, ada_w[0], ex, cx, ex_blk, cx_blk)
    dscc_g = _exchange(dscc, "chips", True, "gather_dcctx")
    g_c_ctx = _cctx_final(dscc_g, c_ctx2).reshape(D)

    small = dict(small, mu_p=_unpad_cols(small["mu_p"], mu_prev.shape[1]),
                 mu_n=_unpad_cols(small["mu_n"], mu_next.shape[1]),
                 w2=small["w2"][:, :decay_w2.shape[2]], a2=small["a2"][:, :iclr_a2.shape[2]],
                 gate_w2=small["gate_w2"][:gate_w2.shape[1]], conv_w=small["conv_w"][:KCONV])
    sm_list = [small[n] for n in _SMALL_ORDER]
    sm_shapes2 = [a.shape for a in sm_list]
    sm_pack = _pack_rows(sm_list) + win_token[0, 0]
    sm_pair = _sib_stream(jnp.concatenate([sm_pack, sm_pack], axis=0)[None], me1, "sib_small_grads", True)[0]
    sm_tot = _sum_slots(_exchange(sm_pair, "chips", True, "gather_small_grads"), "sum_small_grads")
    S = dict(zip(_SMALL_ORDER, _unpack(sm_tot.reshape(-1), sm_shapes2)))

    def shard_last(a):
        return lax.dynamic_slice_in_dim(a, chip * lane_sh, lane_sh, axis=a.ndim - 1)

    grads = dict(
        c_ctx=g_c_ctx, ada_w=g_ada_w[None], ada_b=g_ada_b,
        mix_pre_g=S["mix_pre_g"], mix_post_g=S["mix_post_g"], mlp_pre_g=S["mlp_pre_g"], mlp_post_g=S["mlp_post_g"],
        mu_prev=S["mu_p"], mu_next=S["mu_n"],
        decay_w0=shard_last(S["w0"])[None], decay_w2=shard_last(S["w2"])[None],
        iclr_a0=shard_last(S["a0"])[None], iclr_a2=shard_last(S["a2"])[None],
        k_k=S["k_k"], k_a=S["k_a"], r_k=S["r_k"].reshape(r_k.shape),
        gate_w2=shard_last(S["gate_w2"])[None], lnx_w=S["lnx_w"], lnx_b=S["lnx_b"],
        conv_w=shard_last(S["conv_w"])[None], conv_b=S["conv_b"], conv_ln_w=S["conv_ln_w"],
        conv_ln_b=S["conv_ln_b"])

    def both_halves(mine_, name, nck):
        chunks = mine_.reshape(nck, mine_.shape[0] // nck, mine_.shape[1])
        other = _sib_stream(chunks, me1, name, False)
        full = jnp.where(ic == 0, jnp.concatenate([chunks, other], axis=0), jnp.concatenate([other, chunks], axis=0))
        return full.reshape(2 * mine_.shape[0], mine_.shape[1])

    own, land = _gather_wait(early["send"], early["recv"], early["x"], early["land"], big["after_scan"],
                             "reduce_mlp_wait", scatter=True)
    land = lax.dynamic_update_slice(land, lax.dynamic_index_in_dim(own, chip, 0, keepdims=True), (chip, 0, 0))
    tot = both_halves(_sum_slots(land, "sum_mlp_grads"), "sib_swap_mlp_grads", 3)
    g_w1, g_w2, g_w_out = tot[:n_w1], tot[n_w1:n_w1 + n_w2], tot[n_w1 + n_w2:]

    own, land = _gather_wait(win_s, win_r, win_x, win_land, sm_tot, "reduce_w_in_wait", scatter=True)
    land = lax.dynamic_update_slice(land, lax.dynamic_index_in_dim(own, chip, 0, keepdims=True), (chip, 0, 0))
    g_w_in = both_halves(_sum_slots(land, "sum_w_in_grads"), "sib_swap_w_in", 2)
    grads.update(w_in=g_w_in[None], w_out=g_w_out[None], mlp_w1=g_w1[None], mlp_w2=g_w2[None])

    deltas, new_m, new_v = {}, {}, {}
    for n in _WEIGHT_NAMES:
        g = grads[n].reshape(weights[n].shape)
        grads[n] = g
        deltas[n], new_m[n], new_v[n] = _adamw(weights[n], g, moms[n], vars_[n], "adamw_" + n)

    return (loss, grad_x, *[grads[n] for n in _WEIGHT_NAMES], *[deltas[n] for n in _WEIGHT_NAMES],
            *[new_m[n] for n in _WEIGHT_NAMES], *[new_v[n] for n in _WEIGHT_NAMES])
```

```python
import functools

import jax
import jax.numpy as jnp
from jax import lax
from jax.experimental import pallas as pl
from jax.experimental.pallas import tpu as pltpu

F32 = jnp.float32
BF16 = jnp.bfloat16
HI = lax.Precision.HIGHEST

D = 1024
W = 512
HS = 64
RWC = 2304
CVC = 1024
GDW = 256
LRW = 128
DFF = 4096
TT = 256
LINE = 64
KCONV = 31
EPS_RMS = 1e-6
EPS_LN = 1e-5
EPS_GN = 64e-5
SCAN_CH = 128
SCAN_G = 8
SCAN_BSUB = 16

ADAM_LR = 0.001
ADAM_B1 = 0.9
ADAM_B2 = 0.999
ADAM_EPS = 1e-08
ADAM_WD = 0.01
ADAM_STEP = 10

_SEGS = ((0, 1536, 1536), (1536, 64, 128), (1600, 64, 128), (1664, 64, 128), (1728, 64, 128),
         (1792, 160, 256), (1952, 1024, 1024))

MESH = pl.DeviceIdType.MESH


def _bs(shape, imap):
    return pl.BlockSpec(shape, imap)


def _cp(sem=None, mb=48):
    return pltpu.CompilerParams(dimension_semantics=sem, vmem_limit_bytes=mb << 20)


def _pad_cols(a, ncols):
    out = []
    for s, w, pw in _SEGS:
        if s >= ncols:
            break
        piece = a[..., s:s + w]
        if pw > w:
            piece = jnp.pad(piece, [(0, 0)] * (a.ndim - 1) + [(0, pw - w)])
        out.append(piece)
    return jnp.concatenate(out, axis=-1)


def _unpad_cols(a, ncols):
    out = []
    off = 0
    for s, w, pw in _SEGS:
        if s >= ncols:
            break
        out.append(a[..., off:off + w])
        off += pw
    return jnp.concatenate(out, axis=-1)


def _sigmoid(x):
    return 1.0 / (1.0 + jnp.exp(-x))


def _softplus(x):
    return jnp.maximum(x, 0.0) + jnp.log(1.0 + jnp.exp(-jnp.abs(x)))


def _e128(dtype):
    r = lax.broadcasted_iota(jnp.int32, (128, 128), 0) >= HS
    c = lax.broadcasted_iota(jnp.int32, (128, 128), 1) >= HS
    return (r == c).astype(dtype)


def _segsum(x, e):
    hi = x.astype(BF16)
    lo = (x - hi.astype(F32)).astype(BF16)
    return jnp.concatenate(
        [jnp.dot(hi[:, 128 * g:128 * (g + 1)], e, preferred_element_type=F32)
         + jnp.dot(lo[:, 128 * g:128 * (g + 1)], e, preferred_element_type=F32) for g in range(4)], axis=1)


_NT_DIMS = (((1,), (1,)), ((), ()))
_TN_DIMS = (((0,), (0,)), ((), ()))


def _bdot(a, b, dims=None):
    a = a.astype(BF16)
    b = b.astype(BF16)
    if dims is None:
        return jnp.dot(a, b, preferred_element_type=F32)
    return lax.dot_general(a, b, dims, preferred_element_type=F32)


def _colsum(x):
    return jnp.sum(x, axis=0, keepdims=True)


def _rowmean(x):
    return jnp.mean(x, axis=-1, keepdims=True)


def _diag(x, npairs):
    row = lax.broadcasted_iota(jnp.int32, (HS, 128), 0)
    lane = lax.broadcasted_iota(jnp.int32, (HS, 128), 1) & (HS - 1)
    keep = jnp.broadcast_to((lane == row)[None], (npairs, HS, 128))
    return jnp.where(keep, x.reshape(npairs, HS, 128), 0.0).reshape(npairs * HS, 128)


def _segb(x, e):
    return jnp.dot(x.astype(BF16), e, preferred_element_type=F32)


_segb1 = _segb


def _segp(xb, row, npairs, e):
    return jnp.dot(xb * _expand(row.astype(BF16), npairs), e, preferred_element_type=F32)


def _expand(row, npairs):
    return jnp.concatenate([jnp.broadcast_to(row[:, 128 * j:128 * (j + 1)], (HS, 128)) for j in range(npairs)], axis=0)


def _colb(row, npairs, e):
    return _segb1(_diag(_expand(row, npairs), npairs), e)


def _pair_colsum(x, npairs):
    return jnp.concatenate([_colsum(x[HS * j:HS * (j + 1)]) for j in range(npairs)], axis=1)


def _conv_pos():
    return lax.broadcasted_iota(jnp.int32, (TT, W), 0) & (LINE - 1)


def _shifted(u, s, pos):
    if s == 0:
        return u
    sh = pltpu.roll(u, (-s) % TT, 0)
    valid = jnp.logical_and(pos + s >= 0, pos + s < LINE)
    return jnp.where(valid, sh, 0.0)


def _acc(ref, val, first):
    @pl.when(first)
    def _():
        ref[...] = jnp.zeros(ref.shape, ref.dtype)
    ref[...] += val


class _Tiles:
    def __init__(self, B, t_lat):
        self.B = B
        self.NLT = t_lat // TT
        self.TPS = self.NLT + 1
        self.NT = B * self.TPS
        self.NL = B * self.NLT
        self.NTOK = self.NT * TT
        self.NLAT = self.NL * TT
        self.TTOT = self.TPS * TT
        self.BW = B * W

    def b(self, i):
        return i // self.TPS

    def q(self, i):
        return i % self.TPS

    def lat(self, i):
        return (i // self.TPS) * self.NLT + jnp.maximum(i % self.TPS - 1, 0)

    def tok(self, l):
        return (l // self.NLT) * self.TPS + 1 + l % self.NLT

    def mod_spec(self):
        return _bs((1, 1, 6, D), lambda i: (i // self.TPS, jnp.minimum(i % self.TPS, 1), 0, 0))

    def tm_spec(self):
        return _bs((TT, W), lambda i: (i % self.TPS, i // self.TPS))

    def tm2_spec(self):
        return _bs((2, TT, W), lambda i: (0, i % self.TPS, i // self.TPS))


def _row(shape_last):
    return _bs((1, shape_last), lambda i: (0, 0))


def _tok_specs(T):
    return [_bs((TT, D), lambda i: (T.lat(i), 0)), _bs((TT, D), lambda i: (i // T.TPS, 0))]


def _tok_tile(T, x_ref, c_ref):
    is_ctx = (pl.program_id(0) % T.TPS == 0).astype(F32)
    return c_ref[...] * is_ctx + x_ref[...] * (1.0 - is_ctx)


def _mix_in(T, x2, c2, modrows, g, w_rw, w_cv):
    def kern(x_ref, c_ref, mod_ref, g_ref, wr_ref, wc_ref, prw_ref, pcv_ref, h_ref):
        x = _tok_tile(T, x_ref, c_ref)
        s = lax.rsqrt(_rowmean(x * x) + EPS_RMS)
        h = (x * s * g_ref[...]) * (1.0 + mod_ref[0, 0, 1:2, :]) + mod_ref[0, 0, 0:1, :]
        hb = h.astype(BF16)
        h_ref[...] = hb
        prw_ref[...] = jnp.dot(hb, wr_ref[...], preferred_element_type=F32)
        pcv_ref[...] = jnp.dot(hb, wc_ref[...], preferred_element_type=F32)

    return pl.pallas_call(
        kern, grid=(T.NT,), name="mix_in",
        in_specs=_tok_specs(T) + [T.mod_spec(), _row(D),
                                  _bs((D, RWC), lambda i: (0, 0)), _bs((D, CVC), lambda i: (0, 0))],
        out_specs=[_bs((TT, RWC), lambda i: (i, 0)), _bs((TT, CVC), lambda i: (i, 0)), _bs((TT, D), lambda i: (i, 0))],
        out_shape=[jax.ShapeDtypeStruct((T.NTOK, RWC), F32), jax.ShapeDtypeStruct((T.NTOK, CVC), F32),
                   jax.ShapeDtypeStruct((T.NTOK, D), BF16)],
        compiler_params=_cp(("parallel",)),
    )(x2, c2, modrows, g, w_rw, w_cv)


def _halo_specs(T):
    nb8 = T.NTOK // 8
    prev = _bs((8, RWC), lambda i: (jnp.maximum(i * (TT // 8) - 1, 0), 0))
    nxt = _bs((8, RWC), lambda i: (jnp.minimum((i + 1) * (TT // 8), nb8 - 1), 0))
    return prev, nxt


def _halo_masks(T, i):
    q = i % T.TPS
    has_prev = jnp.logical_and(q != 0, q != 1).astype(F32)
    has_next = jnp.logical_and(q != 0, q != T.TPS - 1).astype(F32)
    return has_prev, has_next


def _neighbours(z, prev_row, next_row):
    rowi = lax.broadcasted_iota(jnp.int32, z.shape, 0)
    zprev = jnp.where(rowi == 0, prev_row, pltpu.roll(z, 1, 0))
    znext = jnp.where(rowi == TT - 1, next_row, pltpu.roll(z, TT - 1, 0))
    return zprev, znext


def _prep_math(rw, w0, w2, a0, a2, k_k, k_a, e):
    r = rw[:, 0:512]
    k = rw[:, 512:1024]
    v = rw[:, 1024:1536]
    kr = k * k_k
    ss = _segsum(kr * kr, e)
    rt = jnp.sqrt(ss)
    inv = 1.0 / jnp.maximum(rt, 1e-12)
    kk = kr * inv
    o = dict(r=r, k=k, v=v, kr=kr, rt=rt, inv=inv, kk=kk, th=[], pre=[], ex=[], dec=[], iclr=[], kd=[], bb=[], ad=[])
    for d in (0, 1):
        wd = rw[:, 1536 + LRW * d:1536 + LRW * (d + 1)]
        ad = rw[:, 1792 + LRW * d:1792 + LRW * (d + 1)]
        th = jnp.tanh(wd)
        pre = w0[d] + _bdot(th, w2[d])
        ex = jnp.exp(-_softplus(-pre) - 0.5)
        dec = jnp.exp(-ex)
        iclr = _sigmoid(a0[d] + _bdot(ad, a2[d]))
        o["th"].append(th)
        o["pre"].append(pre)
        o["ex"].append(ex)
        o["dec"].append(dec)
        o["iclr"].append(iclr)
        o["ad"].append(ad)
        o["kd"].append(k * (1.0 + (iclr - 1.0) * k_a))
        o["bb"].append(kk * iclr)
    return o


def _load_prep_params(w0_ref, w2_ref, a0_ref, a2_ref):
    w0 = [w0_ref[0:1, :], w0_ref[1:2, :]]
    a0 = [a0_ref[0:1, :], a0_ref[1:2, :]]
    w2 = [w2_ref[0], w2_ref[1]]
    a2 = [a2_ref[0], a2_ref[1]]
    return w0, w2, a0, a2


def _prep_param_specs():
    return [_bs((2, W), lambda i: (0, 0)), _bs((2, LRW, W), lambda i: (0, 0, 0)),
            _bs((2, W), lambda i: (0, 0)), _bs((2, LRW, W), lambda i: (0, 0, 0)), _row(W), _row(W)]


def _rwkv_prep(T, p_rw, mu_p, mu_n, w0, w2, a0, a2, k_k, k_a):
    def kern(p_ref, pp_ref, pn_ref, mp_ref, mn_ref, w0_ref, w2_ref, a0_ref, a2_ref, kk_ref, ka_ref,
             r_o, v_o, kk_o, dec_o, kd_o, bb_o, rw_o):
        i = pl.program_id(0)
        has_prev, has_next = _halo_masks(T, i)
        z = p_ref[...]
        zprev, znext = _neighbours(z, pp_ref[7:8, :] * has_prev, pn_ref[0:1, :] * has_next)
        rw = z + mp_ref[...] * (zprev - z) + mn_ref[...] * (znext - z)
        rw_o[...] = rw
        w0v, w2v, a0v, a2v = _load_prep_params(w0_ref, w2_ref, a0_ref, a2_ref)
        o = _prep_math(rw, w0v, w2v, a0v, a2v, kk_ref[...], ka_ref[...], _e128(BF16))
        r_o[...] = o["r"]
        v_o[...] = o["v"]
        kk_o[...] = o["kk"]
        for d in (0, 1):
            dec_o[d] = o["dec"][d]
            kd_o[d] = o["kd"][d]
            bb_o[d] = o["bb"][d]

    prev, nxt = _halo_specs(T)
    tm = jax.ShapeDtypeStruct((T.TTOT, T.BW), F32)
    tm2 = jax.ShapeDtypeStruct((2, T.TTOT, T.BW), F32)
    return pl.pallas_call(
        kern, grid=(T.NT,), name="rwkv_prep",
        in_specs=[_bs((TT, RWC), lambda i: (i, 0)), prev, nxt, _row(RWC), _row(RWC)] + _prep_param_specs(),
        out_specs=[T.tm_spec(), T.tm_spec(), T.tm_spec(), T.tm2_spec(), T.tm2_spec(), T.tm2_spec(),
                   _bs((TT, RWC), lambda i: (i, 0))],
        out_shape=[tm, tm, tm, tm2, tm2, tm2, jax.ShapeDtypeStruct((T.NTOK, RWC), F32)],
        compiler_params=_cp(("parallel",)),
    )(p_rw, p_rw, p_rw, mu_p, mu_n, w0, w2, a0, a2, k_k, k_a)


def _scan_fwd(T, r, v, kk, dec, kd, bb):
    NP = T.BW // 128
    R = NP * HS
    NCH = T.TTOT // SCAN_CH
    NCC = TT // SCAN_CH
    G = SCAN_G
    NG = SCAN_CH // G
    NGRP = 4
    assert NG % NGRP == 0

    def tmap(d, i):
        rev = jnp.where(i < NCC, NCC - 1 - i, NCH - 1 - (i - NCC))
        return jnp.where(d == 0, i, rev)

    def kern(r_ref, v_ref, kk_ref, dec_ref, kd_ref, bb_ref, y_ref, fin_ref, hist_ref, ring, sems):
        d = pl.program_id(0)
        i = pl.program_id(1)

        @pl.when(i == 0)
        def _():
            ring[0] = jnp.zeros((R, 128), F32)

        e = _e128(BF16)

        def hist_copy(k):
            grp = k % NGRP
            return pltpu.make_async_copy(ring.at[pl.ds(grp * G, G)],
                                         hist_ref.at[d, pl.ds(i * SCAN_CH + k * G, G)], sems.at[grp])

        def make_body(with_y):
            def body(k, carry):
                @pl.when(k >= NGRP - 1)
                def _():
                    hist_copy(k - (NGRP - 1)).wait()

                base = (k % NGRP) * G
                s = ring[base]
                sb = s.astype(BF16)
                for u in range(G):
                    t = k * G + u
                    row = jnp.where(d == 0, t, SCAN_CH - 1 - t)
                    sa = _segp(sb, -kk_ref[pl.ds(row, 1), :], NP, e)
                    vc = _colb(v_ref[pl.ds(row, 1), :], NP, e)
                    s = (s * _expand(dec_ref[0, pl.ds(row, 1), :], NP) + sa * _expand(bb_ref[0, pl.ds(row, 1), :], NP)
                         + vc * _expand(kd_ref[0, pl.ds(row, 1), :], NP))
                    ring[(base + u + 1) if u < G - 1 else ((k + 1) % NGRP) * G] = s
                    sb = s.astype(BF16)
                    if with_y:
                        yb = _segp(sb, r_ref[pl.ds(row, 1), :], NP, e)
                        y_ref[0, pl.ds(row, 1), :] = _pair_colsum(_diag(yb, NP), NP)
                for prio in range(2):
                    @pl.when(k % 2 == prio)
                    def _():
                        hist_copy(k).start(priority=prio)
                return carry
            return body

        @pl.when(i < NCC)
        def _():
            lax.fori_loop(0, NG, make_body(False), 0)

        @pl.when(i >= NCC)
        def _():
            lax.fori_loop(0, NG, make_body(True), 0)

        for k in range(NG - (NGRP - 1), NG):
            hist_copy(k).wait()

        @pl.when(i == NCH - 1)
        def _():
            fin_ref[0] = ring[0]

    sh = _bs((SCAN_CH, T.BW), lambda d, i: (tmap(d, i), 0))
    dr = _bs((1, SCAN_CH, T.BW), lambda d, i: (d, tmap(d, i), 0))
    return pl.pallas_call(
        kern, grid=(2, NCH), name="scan_fwd",
        in_specs=[sh, sh, sh, dr, dr, dr],
        out_specs=[dr, _bs((1, R, 128), lambda d, i: (d, 0, 0)), pl.BlockSpec(memory_space=pl.ANY)],
        out_shape=[jax.ShapeDtypeStruct((2, T.TTOT, T.BW), F32), jax.ShapeDtypeStruct((2, R, 128), F32),
                   jax.ShapeDtypeStruct((2, T.TTOT, R, 128), F32)],
        scratch_shapes=[pltpu.VMEM((NGRP * G, R, 128), F32), pltpu.SemaphoreType.DMA((NGRP,))],
        compiler_params=_cp(("arbitrary", "arbitrary")),
    )(r, v, kk, dec, kd, bb)


def _readout_fwd(y, r, v, gd, kbar, rk, gw, lw, lb, e):
    mu = _segsum(y, e) * (1.0 / HS)
    yc = y - mu
    var = _segsum(yc * yc, e) * (1.0 / HS)
    rstd = lax.rsqrt(var + EPS_GN)
    yhat = yc * rstd
    yn = yhat * lw + lb
    q = _segsum(r * kbar * rk, e)
    sg = _sigmoid(gd)
    gg = _bdot(sg, gw)
    return dict(yhat=yhat, rstd=rstd, yn=yn, q=q, sg=sg, gg=gg, out=(yn + q * v) * gg)


def _conv_fwd(cva, cvb, cw_ref, cb, lw, lb, c=None):
    pos = _conv_pos()
    sgb = _sigmoid(cvb)
    u = cva * sgb
    if c is None:
        c = jnp.zeros_like(u)
        for j in range(KCONV):
            c = c + cw_ref[j:j + 1, :] * _shifted(u, j - KCONV // 2, pos)
        c = c + cb
    mu = _rowmean(c)
    cc = c - mu
    rstd = lax.rsqrt(_rowmean(cc * cc) + EPS_LN)
    chat = cc * rstd
    cn = chat * lw + lb
    scn = _sigmoid(cn)
    return dict(sgb=sgb, u=u, c=c, chat=chat, rstd=rstd, cn=cn, scn=scn, out=cn * scn, pos=pos)


def _mix_out(T, y, kd, rw, p_cv, x2, modrows, rk, gw, lnw, lnb, cw, cb, clw, clb, pg, w_out):
    tk = T.tok

    def kern(y_ref, kd_ref, rw_ref, pcv_ref, x_ref, mod_ref, rk_ref, gw_ref, lw_ref, lb_ref, cw_ref, cb_ref,
             clw_ref, clb_ref, pg_ref, wo_ref, cat_o, mix_o, x1_o, conv_o):
        e = _e128(BF16)
        ro = _readout_fwd(y_ref[0] + y_ref[1], rw_ref[:, 0:512], rw_ref[:, 1024:1536], rw_ref[:, 2048:2304],
                          0.5 * (kd_ref[0] + kd_ref[1]), rk_ref[...], gw_ref[...], lw_ref[...], lb_ref[...], e)
        cv = _conv_fwd(pcv_ref[:, 0:512], pcv_ref[:, 512:1024], cw_ref, cb_ref[...], clw_ref[...], clb_ref[...])
        conv_o[...] = cv["c"]
        catb = jnp.concatenate([ro["out"], cv["out"]], axis=1).astype(BF16)
        cat_o[...] = catb
        mix = jnp.dot(catb, wo_ref[...], preferred_element_type=F32)
        mix_o[...] = mix
        sm = lax.rsqrt(_rowmean(mix * mix) + EPS_RMS)
        x1_o[...] = x_ref[...] + mod_ref[0, 0, 2:3, :] * (mix * sm * pg_ref[...])

    lat = lambda l: (l, 0)
    return pl.pallas_call(
        kern, grid=(T.NL,), name="mix_out",
        in_specs=[_bs((2, TT, W), lambda l: (0, 1 + l % T.NLT, l // T.NLT)),
                  _bs((2, TT, W), lambda l: (0, 1 + l % T.NLT, l // T.NLT)),
                  _bs((TT, RWC), lambda l: (tk(l), 0)), _bs((TT, CVC), lambda l: (tk(l), 0)),
                  _bs((TT, D), lambda l: (l, 0)),
                  _bs((1, 1, 6, D), lambda l: (l // T.NLT, 1, 0, 0)),
                  _row(W), _bs((GDW, W), lambda l: (0, 0)), _row(W), _row(W),
                  _bs((32, W), lambda l: (0, 0)), _row(W), _row(W), _row(W), _row(D),
                  _bs((D, D), lambda l: (0, 0))],
        out_specs=[_bs((TT, D), lat), _bs((TT, D), lat), _bs((TT, D), lat), _bs((TT, W), lat)],
        out_shape=[jax.ShapeDtypeStruct((T.NLAT, D), BF16), jax.ShapeDtypeStruct((T.NLAT, D), F32),
                   jax.ShapeDtypeStruct((T.NLAT, D), F32), jax.ShapeDtypeStruct((T.NLAT, W), F32)],
        compiler_params=_cp(("parallel",)),
    )(y, kd, rw, p_cv, x2, modrows, rk, gw, lnw, lnb, cw, cb, clw, clb, pg, w_out)


MT = 512
FC = 1024


def _late_weight_specs():
    assert FC == D
    return [_bs((1, D, FC), lambda t, f: (f, 0, 0)), _bs((1, FC, D), lambda t, f: (f, 1, 0))]


def _mlp_fwd(T, x1, tgt, modrows, g, pg, wl):
    per_b = T.NLT * TT // MT
    last = DFF // FC - 1

    def kern(x_ref, t_ref, mod_ref, g_ref, pg_ref, w1_ref, w2_ref, h2_o, loss_o, dm_o, dx2_o, dg2_o, dpg_o, h2_s, m_s):
        t = pl.program_id(0)
        f = pl.program_id(1)

        @pl.when(f == 0)
        def _():
            x = x_ref[...]
            s = lax.rsqrt(_rowmean(x * x) + EPS_RMS)
            h2 = (x * s * g_ref[...]) * (1.0 + mod_ref[0, 0, 4:5, :]) + mod_ref[0, 0, 3:4, :]
            h2_s[...] = h2.astype(BF16)
            h2_o[...] = h2.astype(BF16)
            m_s[...] = jnp.zeros_like(m_s)

        a = jnp.dot(h2_s[...], w1_ref[0], preferred_element_type=F32)
        rl = jnp.maximum(a, 0.0)
        m_s[...] += jnp.dot((rl * rl).astype(BF16), w2_ref[0], preferred_element_type=F32)

        @pl.when(f == last)
        def _():
            m_ = m_s[...]
            sm = lax.rsqrt(_rowmean(m_ * m_) + EPS_RMS)
            mn = m_ * sm
            g2 = mod_ref[0, 0, 5:6, :]
            pgv = pg_ref[...]
            diff = x_ref[...] + g2 * (mn * pgv) - t_ref[...]
            sq = jnp.sum(_colsum(diff * diff), axis=1, keepdims=True)
            _acc(loss_o, jnp.zeros((8, 128), F32) + (0.5 / D) * sq, t == 0)
            dx2 = diff * (1.0 / D)
            dx2_o[...] = dx2
            _acc(dg2_o.at[0], _colsum(dx2 * mn * pgv), t % per_b == 0)
            _acc(dpg_o, _colsum(dx2 * g2 * mn), t == 0)
            dmn = dx2 * g2 * pgv
            dm_o[...] = (sm * (dmn - mn * _rowmean(dmn * mn))).astype(BF16)

    tok = lambda t, f: (t, 0)
    const2 = lambda t, f: (0, 0)
    return pl.pallas_call(
        kern, grid=(T.NLAT // MT, DFF // FC), name="mlp_fwd",
        in_specs=[_bs((MT, D), tok), _bs((MT, D), tok), _bs((1, 1, 6, D), lambda t, f: (t // per_b, 1, 0, 0)),
                  _bs((1, D), const2), _bs((1, D), const2)] + _late_weight_specs(),
        out_specs=[_bs((MT, D), tok), _bs((8, 128), const2), _bs((MT, D), tok), _bs((MT, D), tok),
                   _bs((1, 1, D), lambda t, f: (t // per_b, 0, 0)), _bs((1, D), const2)],
        out_shape=[jax.ShapeDtypeStruct((T.NLAT, D), BF16), jax.ShapeDtypeStruct((8, 128), F32),
                   jax.ShapeDtypeStruct((T.NLAT, D), BF16), jax.ShapeDtypeStruct((T.NLAT, D), F32),
                   jax.ShapeDtypeStruct((T.B, 1, D), F32), jax.ShapeDtypeStruct((1, D), F32)],
        scratch_shapes=[pltpu.VMEM((MT, D), BF16), pltpu.VMEM((MT, D), F32)],
        compiler_params=_cp(("arbitrary", "arbitrary")),
    )(x1, tgt, modrows, g, pg, wl, wl)


def _mlp_bwd(T, h2, dm, x1, dx2, modrows, g, wl):
    per_b = T.NLT * TT // MT
    last = DFF // FC - 1

    def kern(h2_ref, dm_ref, x1_ref, dx2_ref, mod_ref, g_ref, w1_ref, w2_ref, f_o, da_o, dx1_o, dmod_o, dg_o, dh_s):
        t = pl.program_id(0)
        f = pl.program_id(1)
        a = jnp.dot(h2_ref[...], w1_ref[0], preferred_element_type=F32)
        rl = jnp.maximum(a, 0.0)
        f_o[...] = (rl * rl).astype(BF16)
        df = lax.dot_general(dm_ref[...], w2_ref[0], _NT_DIMS, preferred_element_type=F32)
        dab = (df * (2.0 * rl)).astype(BF16)
        da_o[...] = dab
        _acc(dh_s, lax.dot_general(dab, w1_ref[0], _NT_DIMS, preferred_element_type=F32), f == 0)

        @pl.when(f == last)
        def _():
            x = x1_ref[...]
            s = lax.rsqrt(_rowmean(x * x) + EPS_RMS)
            xh = x * s
            gv = g_ref[...]
            dh = dh_s[...]
            first_b = t % per_b == 0
            _acc(dmod_o.at[0, 0:1, :], _colsum(dh), first_b)
            _acc(dmod_o.at[0, 1:2, :], _colsum(dh * (xh * gv)), first_b)
            dn2 = dh * (1.0 + mod_ref[0, 0, 4:5, :])
            _acc(dg_o, _colsum(dn2 * xh), t == 0)
            dxh = dn2 * gv
            dx1_o[...] = dx2_ref[...] + s * (dxh - xh * _rowmean(dxh * xh))

    tok = lambda t, f: (t, 0)
    const2 = lambda t, f: (0, 0)
    return pl.pallas_call(
        kern, grid=(T.NLAT // MT, DFF // FC), name="mlp_bwd",
        in_specs=[_bs((MT, D), tok), _bs((MT, D), tok), _bs((MT, D), tok), _bs((MT, D), tok),
                  _bs((1, 1, 6, D), lambda t, f: (t // per_b, 1, 0, 0)), _bs((1, D), const2)] + _late_weight_specs(),
        out_specs=[_bs((MT, FC), lambda t, f: (t, f)), _bs((MT, FC), lambda t, f: (t, f)), _bs((MT, D), tok),
                   _bs((1, 2, D), lambda t, f: (t // per_b, 0, 0)), _bs((1, D), const2)],
        out_shape=[jax.ShapeDtypeStruct((T.NLAT, DFF), BF16), jax.ShapeDtypeStruct((T.NLAT, DFF), BF16),
                   jax.ShapeDtypeStruct((T.NLAT, D), F32), jax.ShapeDtypeStruct((T.B, 2, D), F32),
                   jax.ShapeDtypeStruct((1, D), F32)],
        scratch_shapes=[pltpu.VMEM((MT, D), F32)],
        compiler_params=_cp(("arbitrary", "arbitrary")),
    )(h2, dm, x1, dx2, modrows, g, wl, wl)


def _mix_post_bwd(T, dx1, mix, modrows, pg, w_out):
    def kern(dx_ref, mix_ref, mod_ref, pg_ref, wo_ref, dmix_o, dcat_o, dg1_o, dpg_o):
        i = pl.program_id(0)
        lat = (i % T.TPS != 0).astype(F32)
        dx = dx_ref[...] * lat
        mix = mix_ref[...]
        sm = lax.rsqrt(_rowmean(mix * mix) + EPS_RMS)
        mh = mix * sm
        g1 = mod_ref[0, 0, 2:3, :]
        pgv = pg_ref[...]
        _acc(dg1_o.at[0], _colsum(dx * mh * pgv), i % T.TPS == 0)
        _acc(dpg_o, _colsum(dx * g1 * mh), i == 0)
        dmh = dx * g1 * pgv
        dmix = ((sm * (dmh - mh * _rowmean(dmh * mh))) * lat).astype(BF16)
        dmix_o[...] = dmix
        dcat_o[...] = lax.dot_general(dmix, wo_ref[...], _NT_DIMS, preferred_element_type=F32)

    tok = lambda i: (i, 0)
    return pl.pallas_call(
        kern, grid=(T.NT,), name="mix_post_bwd",
        in_specs=[_bs((TT, D), lambda i: (T.lat(i), 0)), _bs((TT, D), lambda i: (T.lat(i), 0)),
                  _bs((1, 1, 6, D), lambda i: (i // T.TPS, 1, 0, 0)), _row(D), _bs((D, D), lambda i: (0, 0))],
        out_specs=[_bs((TT, D), tok), _bs((TT, D), tok), _bs((1, 1, D), lambda i: (i // T.TPS, 0, 0)), _row(D)],
        out_shape=[jax.ShapeDtypeStruct((T.NTOK, D), BF16), jax.ShapeDtypeStruct((T.NTOK, D), F32),
                   jax.ShapeDtypeStruct((T.B, 1, D), F32), jax.ShapeDtypeStruct((1, D), F32)],
        compiler_params=_cp(("arbitrary",)),
    )(dx1, mix, modrows, pg, w_out)


def _conv_bwd(T, dcat, p_cv, conv, cw, cb, clw, clb):
    def kern(dc_ref, pcv_ref, conv_ref, cw_ref, cb_ref, clw_ref, clb_ref, dp_o, dcw_o, dcb_o, dlw_o, dlb_o):
        i = pl.program_id(0)
        is_lat = i % T.TPS != 0

        @pl.when(i == 0)
        def _():
            for ref in (dcw_o, dcb_o, dlw_o, dlb_o):
                ref[...] = jnp.zeros(ref.shape, ref.dtype)

        @pl.when(jnp.logical_not(is_lat))
        def _():
            dp_o[...] = jnp.zeros(dp_o.shape, dp_o.dtype)

        @pl.when(is_lat)
        def _():
            cva = pcv_ref[:, 0:512]
            cv = _conv_fwd(cva, pcv_ref[:, 512:1024], cw_ref, cb_ref[...], clw_ref[...], clb_ref[...],
                           c=conv_ref[...])
            scn = cv["scn"]
            dcn = dc_ref[...] * (scn * (1.0 + cv["cn"] * (1.0 - scn)))
            chat = cv["chat"]
            dlw_o[...] += _colsum(dcn * chat)
            dlb_o[...] += _colsum(dcn)
            dchat = dcn * clw_ref[...]
            dc = cv["rstd"] * (dchat - _rowmean(dchat) - chat * _rowmean(dchat * chat))
            dcb_o[...] += _colsum(dc)
            pos = cv["pos"]
            u = cv["u"]
            du = jnp.zeros_like(u)
            for j in range(KCONV):
                s = j - KCONV // 2
                dcw_o[j:j + 1, :] += _colsum(dc * _shifted(u, s, pos))
                du = du + cw_ref[j:j + 1, :] * _shifted(dc, -s, pos)
            sgb = cv["sgb"]
            dp_o[...] = jnp.concatenate([du * sgb, du * cva * sgb * (1.0 - sgb)], axis=1).astype(BF16)

    return pl.pallas_call(
        kern, grid=(T.NT,), name="conv_bwd",
        in_specs=[_bs((TT, W), lambda i: (i, 1)), _bs((TT, CVC), lambda i: (i, 0)),
                  _bs((TT, W), lambda i: (T.lat(i), 0)),
                  _bs((32, W), lambda i: (0, 0)), _row(W), _row(W), _row(W)],
        out_specs=[_bs((TT, CVC), lambda i: (i, 0)), _bs((32, W), lambda i: (0, 0)), _row(W), _row(W), _row(W)],
        out_shape=[jax.ShapeDtypeStruct((T.NTOK, CVC), BF16), jax.ShapeDtypeStruct((32, W), F32),
                   jax.ShapeDtypeStruct((1, W), F32), jax.ShapeDtypeStruct((1, W), F32),
                   jax.ShapeDtypeStruct((1, W), F32)],
        compiler_params=_cp(("arbitrary",)),
    )(dcat, p_cv, conv, cw, cb, clw, clb)


def _readout_bwd(T, dcat, y, kd, rw, rk, gw, lnw, lnb):
    def kern(dc_ref, y_ref, kd_ref, rw_ref, rk_ref, gw_ref, lw_ref, lb_ref,
             dy_o, dr_o, dv_o, dkb_o, dgd_o, drk_o, dgw_o, dlw_o, dlb_o):
        i = pl.program_id(0)
        is_lat = i % T.TPS != 0

        @pl.when(i == 0)
        def _():
            for ref in (drk_o, dgw_o, dlw_o, dlb_o):
                ref[...] = jnp.zeros(ref.shape, ref.dtype)

        @pl.when(jnp.logical_not(is_lat))
        def _():
            for ref in (dy_o, dr_o, dv_o, dkb_o, dgd_o):
                ref[...] = jnp.zeros(ref.shape, ref.dtype)

        @pl.when(is_lat)
        def _():
            e = _e128(BF16)
            r = rw_ref[:, 0:512]
            v = rw_ref[:, 1024:1536]
            kbar = 0.5 * (kd_ref[0] + kd_ref[1])
            rk = rk_ref[...]
            ro = _readout_fwd(y_ref[0] + y_ref[1], r, v, rw_ref[:, 2048:2304], kbar, rk, gw_ref[...],
                              lw_ref[...], lb_ref[...], e)
            dout = dc_ref[...]
            dgg = dout * (ro["yn"] + ro["q"] * v)
            t1 = dout * ro["gg"]
            yhat = ro["yhat"]
            dlw_o[...] += _colsum(t1 * yhat)
            dlb_o[...] += _colsum(t1)
            dyh = t1 * lw_ref[...]
            dy_o[...] = ro["rstd"] * (dyh - _segsum(dyh, e) * (1.0 / HS) - yhat * (_segsum(dyh * yhat, e) * (1.0 / HS)))
            dq = _segsum(t1 * v, e)
            dv_o[...] = t1 * ro["q"]
            dr_o[...] = dq * kbar * rk
            dkb_o[...] = dq * r * rk
            drk_o[...] += _colsum(dq * r * kbar)
            sg = ro["sg"]
            dsg = _bdot(dgg, gw_ref[...], _NT_DIMS)
            dgd_o[...] = dsg * sg * (1.0 - sg)
            dgw_o[...] += _bdot(sg, dgg, _TN_DIMS)

    tok = lambda i: (i, 0)
    f32s = lambda *s: jax.ShapeDtypeStruct(s, F32)
    y_spec = _bs((2, TT, W), lambda i: (0, jnp.maximum(i % T.TPS, 1), i // T.TPS))
    return pl.pallas_call(
        kern, grid=(T.NT,), name="readout_bwd",
        in_specs=[_bs((TT, W), tok), y_spec, T.tm2_spec(), _bs((TT, RWC), tok),
                  _row(W), _bs((GDW, W), lambda i: (0, 0)), _row(W), _row(W)],
        out_specs=[T.tm_spec(), _bs((TT, W), tok), _bs((TT, W), tok), _bs((TT, W), tok), _bs((TT, GDW), tok),
                   _row(W), _bs((GDW, W), lambda i: (0, 0)), _row(W), _row(W)],
        out_shape=[f32s(T.TTOT, T.BW), f32s(T.NTOK, W), f32s(T.NTOK, W), f32s(T.NTOK, W), f32s(T.NTOK, GDW),
                   f32s(1, W), f32s(GDW, W), f32s(1, W), f32s(1, W)],
        compiler_params=_cp(("arbitrary",)),
    )(dcat, y, kd, rw, rk, gw, lnw, lnb)


def _scan_bwd(T, dy, r, v, kk, dec, kd, bb, hist, fin):
    NP = T.BW // 128
    R = NP * HS
    SB = SCAN_BSUB
    NS = T.TTOT // SB
    NSC = TT // SB

    def tmap(d, g):
        s = NS - 1 - g
        rev = jnp.where(s < NSC, NSC - 1 - s, NS - 1 - (s - NSC))
        return jnp.where(d == 0, s, rev)

    def kern(dy_ref, r_ref, v_ref, kk_ref, dec_ref, kd_ref, bb_ref, h_ref, fin_ref,
             dr_o, dw_o, dk_o, dv_o, da_o, db_o, ds_ref, snext):
        d = pl.program_id(0)
        g = pl.program_id(1)

        @pl.when(g == 0)
        def _():
            ds_ref[...] = jnp.zeros_like(ds_ref)
            snext[...] = fin_ref[0]

        e = _e128(BF16)

        def steps(with_dy):
            for t in range(SB - 1, -1, -1):
                row = jnp.where(d == 0, t, SB - 1 - t)
                sp = h_ref[0, t]
                a_row = -kk_ref[pl.ds(row, 1), :]
                a_ = _expand(a_row, NP)
                sa = _segp(sp.astype(BF16), a_row, NP, e)
                vc = _colb(v_ref[pl.ds(row, 1), :], NP, e)
                if with_dy:
                    st = snext[...] if t == SB - 1 else h_ref[0, t + 1]
                    dyc = _colb(dy_ref[pl.ds(row, 1), :], NP, e)
                    ds = ds_ref[...] + dyc * _expand(r_ref[pl.ds(row, 1), :], NP)
                    dr_o[0, pl.ds(row, 1), :] = _pair_colsum(st * dyc, NP)
                else:
                    ds = ds_ref[...]
                    dr_o[0, pl.ds(row, 1), :] = jnp.zeros((1, T.BW), F32)
                dsb = ds.astype(BF16)
                dsa = _segp(dsb, bb_ref[0, pl.ds(row, 1), :], NP, e)
                ds_ref[...] = ds * _expand(dec_ref[0, pl.ds(row, 1), :], NP) + dsa * a_
                dvb = _segp(dsb, kd_ref[0, pl.ds(row, 1), :], NP, e)
                dw_o[0, pl.ds(row, 1), :] = _pair_colsum(ds * sp, NP)
                db_o[0, pl.ds(row, 1), :] = _pair_colsum(ds * sa, NP)
                dv_o[0, pl.ds(row, 1), :] = _pair_colsum(_diag(dvb, NP), NP)
                dk_o[0, pl.ds(row, 1), :] = _pair_colsum(ds * vc, NP)
                da_o[0, pl.ds(row, 1), :] = _pair_colsum(sp * dsa, NP)

        @pl.when(g < NS - NSC)
        def _():
            steps(True)

        @pl.when(g >= NS - NSC)
        def _():
            steps(False)

        snext[...] = h_ref[0, 0]

    sh = _bs((SB, T.BW), lambda d, g: (tmap(d, g), 0))
    dr = _bs((1, SB, T.BW), lambda d, g: (d, tmap(d, g), 0))
    o2 = jax.ShapeDtypeStruct((2, T.TTOT, T.BW), F32)
    return pl.pallas_call(
        kern, grid=(2, NS), name="scan_bwd",
        in_specs=[sh, sh, sh, sh, dr, dr, dr, _bs((1, SB, R, 128), lambda d, g: (d, NS - 1 - g, 0, 0)),
                  _bs((1, R, 128), lambda d, g: (d, 0, 0))],
        out_specs=[dr] * 6,
        out_shape=[o2] * 6,
        scratch_shapes=[pltpu.VMEM((R, 128), F32), pltpu.VMEM((R, 128), F32)],
        compiler_params=_cp(("arbitrary", "arbitrary"), mb=48),
    )(dy, r, v, kk, dec, kd, bb, hist, fin)


def _prep_bwd(T, rw, dr_s, ddec, dkd, dv_s, da_s, dbb, dr_ro, dv_ro, dkbar, dgd, w0, w2, a0, a2, k_k, k_a):
    def kern(rw_ref, drs_ref, ddec_ref, dkd_ref, dvs_ref, das_ref, dbb_ref, drr_ref, dvr_ref, dkb_ref, dgd_ref,
             w0_ref, w2_ref, a0_ref, a2_ref, kk_ref, ka_ref,
             drw_o, dw0_o, dw2_o, da0_o, da2_o, dkk_o, dka_o):
        i = pl.program_id(0)
        first = i == 0
        e = _e128(BF16)
        w0v, w2v, a0v, a2v = _load_prep_params(w0_ref, w2_ref, a0_ref, a2_ref)
        k_k = kk_ref[...]
        k_a = ka_ref[...]
        o = _prep_math(rw_ref[...], w0v, w2v, a0v, a2v, k_k, k_a, e)
        k, kk = o["k"], o["kk"]
        dkbh = 0.5 * dkb_ref[...]
        dk = jnp.zeros_like(k)
        dkk = -(das_ref[0] + das_ref[1])
        dka = jnp.zeros((1, W), F32)
        dwd, dad = [], []
        for d in (0, 1):
            iclr = o["iclr"][d]
            dkd_d = dkd_ref[d] + dkbh
            dbb_d = dbb_ref[d]
            dk = dk + dkd_d * (1.0 + (iclr - 1.0) * k_a)
            dka = dka + _colsum(dkd_d * k * (iclr - 1.0))
            dkk = dkk + dbb_d * iclr
            dicl = dkd_d * k * k_a + dbb_d * kk
            dpa = dicl * iclr * (1.0 - iclr)
            _acc(da0_o.at[d:d + 1, :], _colsum(dpa), first)
            dad.append(_bdot(dpa, a2v[d], _NT_DIMS))
            _acc(da2_o.at[d], _bdot(o["ad"][d], dpa, _TN_DIMS), first)
            dpre = -ddec_ref[d] * o["dec"][d] * o["ex"][d] * _sigmoid(-o["pre"][d])
            _acc(dw0_o.at[d:d + 1, :], _colsum(dpre), first)
            th = o["th"][d]
            dth = _bdot(dpre, w2v[d], _NT_DIMS)
            _acc(dw2_o.at[d], _bdot(th, dpre, _TN_DIMS), first)
            dwd.append(dth * (1.0 - th * th))
        inv = o["inv"]
        kr = o["kr"]
        proj = _segsum(dkk * kr, e)
        dkr = dkk * inv - jnp.where(o["rt"] > 1e-12, kr * inv * inv * inv * proj, 0.0)
        dk = dk + dkr * k_k
        _acc(dkk_o, _colsum(dkr * k), first)
        _acc(dka_o, dka, first)
        dr = drs_ref[0] + drs_ref[1] + drr_ref[...]
        dv = dvs_ref[0] + dvs_ref[1] + dvr_ref[...]
        drw_o[...] = jnp.concatenate([dr, dk, dv, dwd[0], dwd[1], dad[0], dad[1], dgd_ref[...]], axis=1)

    tok = lambda i: (i, 0)
    f32s = lambda *s: jax.ShapeDtypeStruct(s, F32)
    p2 = lambda i: (0, 0)
    p3 = lambda i: (0, 0, 0)
    return pl.pallas_call(
        kern, grid=(T.NT,), name="prep_bwd",
        in_specs=[_bs((TT, RWC), tok)] + [T.tm2_spec()] * 6 + [_bs((TT, W), tok)] * 3 + [_bs((TT, GDW), tok)]
        + _prep_param_specs(),
        out_specs=[_bs((TT, RWC), tok), _bs((2, W), p2), _bs((2, LRW, W), p3), _bs((2, W), p2),
                   _bs((2, LRW, W), p3), _row(W), _row(W)],
        out_shape=[f32s(T.NTOK, RWC), f32s(2, W), f32s(2, LRW, W), f32s(2, W), f32s(2, LRW, W), f32s(1, W), f32s(1, W)],
        compiler_params=_cp(("arbitrary",), mb=56),
    )(rw, dr_s, ddec, dkd, dv_s, da_s, dbb, dr_ro, dv_ro, dkbar, dgd, w0, w2, a0, a2, k_k, k_a)


def _shift_bwd(T, drw, p_rw, mu_p, mu_n):
    def kern(d_ref, dp_ref, dn_ref, p_ref, pp_ref, pn_ref, mp_ref, mn_ref, dprw_o, dmp_o, dmn_o):
        i = pl.program_id(0)
        first = i == 0
        has_prev, has_next = _halo_masks(T, i)
        mp = mp_ref[...]
        mn = mn_ref[...]
        drw = d_ref[...]
        z = p_ref[...]
        zprev, znext = _neighbours(z, pp_ref[7:8, :] * has_prev, pn_ref[0:1, :] * has_next)
        _acc(dmp_o, _colsum(drw * (zprev - z)), first)
        _acc(dmn_o, _colsum(drw * (znext - z)), first)
        dprev, dnext = _neighbours(drw, dp_ref[7:8, :] * has_prev, dn_ref[0:1, :] * has_next)
        dprw_o[...] = (drw * (1.0 - mp - mn) + mp * dnext + mn * dprev).astype(BF16)

    tok = lambda i: (i, 0)
    prev, nxt = _halo_specs(T)
    f32s = lambda *s: jax.ShapeDtypeStruct(s, F32)
    return pl.pallas_call(
        kern, grid=(T.NT,), name="shift_bwd",
        in_specs=[_bs((TT, RWC), tok), prev, nxt, _bs((TT, RWC), tok), prev, nxt, _row(RWC), _row(RWC)],
        out_specs=[_bs((TT, RWC), tok), _row(RWC), _row(RWC)],
        out_shape=[jax.ShapeDtypeStruct((T.NTOK, RWC), BF16), f32s(1, RWC), f32s(1, RWC)],
        compiler_params=_cp(("arbitrary",), mb=56),
    )(drw, drw, drw, p_rw, p_rw, p_rw, mu_p, mu_n)


def _mix_in_bwd(T, dp_rw, dp_cv, x2, c2, dx1, modrows, g, w_rw, w_cv):
    def kern(drw_ref, dcv_ref, x_ref, c_ref, dx1_ref, mod_ref, g_ref, wr_ref, wc_ref, dxc_o, dmod_o, dg_o):
        i = pl.program_id(0)
        dh = (lax.dot_general(drw_ref[...], wr_ref[...], _NT_DIMS, preferred_element_type=F32)
              + lax.dot_general(dcv_ref[...], wc_ref[...], _NT_DIMS, preferred_element_type=F32))
        x = _tok_tile(T, x_ref, c_ref)
        s = lax.rsqrt(_rowmean(x * x) + EPS_RMS)
        xh = x * s
        gv = g_ref[...]
        q = i % T.TPS
        first_kind = jnp.logical_or(q == 0, q == 1)
        _acc(dmod_o.at[0, 0, 0:1, :], _colsum(dh), first_kind)
        _acc(dmod_o.at[0, 0, 1:2, :], _colsum(dh * (xh * gv)), first_kind)
        dn1 = dh * (1.0 + mod_ref[0, 0, 1:2, :])
        _acc(dg_o, _colsum(dn1 * xh), i == 0)
        dxh = dn1 * gv
        dxc_o[...] = dx1_ref[...] + s * (dxh - xh * _rowmean(dxh * xh))

    tok = lambda i: (i, 0)
    f32s = lambda *s: jax.ShapeDtypeStruct(s, F32)
    return pl.pallas_call(
        kern, grid=(T.NT,), name="mix_in_bwd",
        in_specs=[_bs((TT, RWC), tok), _bs((TT, CVC), tok)] + _tok_specs(T) + [
            _bs((TT, D), lambda i: (T.lat(i), 0)), T.mod_spec(), _row(D), _bs((D, RWC), lambda i: (0, 0)),
            _bs((D, CVC), lambda i: (0, 0))],
        out_specs=[_bs((TT, D), lambda i: (T.lat(i), 0)),
                   _bs((1, 1, 2, D), lambda i: (i // T.TPS, jnp.minimum(i % T.TPS, 1), 0, 0)), _row(D)],
        out_shape=[f32s(T.NLAT, D), f32s(T.B, 2, 2, D), f32s(1, D)],
        compiler_params=_cp(("arbitrary",)),
    )(dp_rw, dp_cv, x2, c2, dx1, modrows, g, w_rw, w_cv)


def _matmul_tn(a, b, name, tk, nk, tn, amap=None, bmap=None, tm=1024):
    M = a.shape[1]
    N = b.shape[1]
    amap = amap or (lambda k: k)
    bmap = bmap or (lambda k: k)

    def kern(a_ref, b_ref, o_ref):
        _acc(o_ref, lax.dot_general(a_ref[...], b_ref[...], _TN_DIMS, preferred_element_type=F32),
             pl.program_id(2) == 0)

    return pl.pallas_call(
        kern, grid=(M // tm, N // tn, nk), name=name,
        in_specs=[_bs((tk, tm), lambda i, j, k: (amap(k), i)), _bs((tk, tn), lambda i, j, k: (bmap(k), j))],
        out_specs=_bs((tm, tn), lambda i, j, k: (i, j)),
        out_shape=jax.ShapeDtypeStruct((M, N), F32),
        compiler_params=_cp(("parallel", "parallel", "arbitrary")),
    )(a, b)


def _silu(x):
    return x * _sigmoid(x)


def _ada_fwd(c_all, c_ctx, ada_w, ada_b_blk):
    nb = c_all.shape[0]
    R = nb + 8
    ncol = ada_w.shape[1]

    def kern(c_ref, cc_ref, w_ref, b_ref, o_ref):
        lhs = jnp.concatenate([_silu(c_ref[...]), _silu(cc_ref[...]), jnp.zeros((7, D), F32)], axis=0)
        o_ref[...] = jnp.dot(lhs, w_ref[...], precision=HI, preferred_element_type=F32) + b_ref[...]

    return pl.pallas_call(
        kern, name="ada_fwd", out_shape=jax.ShapeDtypeStruct((R, ncol), F32),
        compiler_params=_cp(None, 40),
    )(c_all, c_ctx, ada_w, ada_b_blk)


def _ada_bwd(c_all, c_ctx, ada_w, ex, cx, ex_blk, cx_blk):
    nb = c_all.shape[0]
    ncol = ada_w.shape[1]

    def kern(c_ref, cc_ref, w_ref, ex_ref, cx_ref, exb_ref, cxb_ref, gw_o, gb_o, ds_o):
        lhs = jnp.concatenate([_silu(c_ref[...]), _silu(cc_ref[...]), jnp.zeros((7, D), F32)], axis=0)
        dmc_blk = _colsum(cxb_ref[...])
        rhs = jnp.concatenate([exb_ref[...], dmc_blk, jnp.zeros((7, ncol), F32)], axis=0)
        gw_o[...] = lax.dot_general(lhs, rhs, _TN_DIMS, precision=HI, preferred_element_type=F32)
        gb_o[...] = _colsum(ex_ref[...]) + _colsum(cx_ref[...])
        ds_o[...] = lax.dot_general(jnp.concatenate([dmc_blk, jnp.zeros((7, ncol), F32)], axis=0), w_ref[...],
                                    _NT_DIMS, precision=HI, preferred_element_type=F32)

    return pl.pallas_call(
        kern, name="ada_bwd",
        out_shape=[jax.ShapeDtypeStruct((D, ncol), F32), jax.ShapeDtypeStruct((1, ex.shape[1]), F32),
                   jax.ShapeDtypeStruct((8, D), F32)],
        compiler_params=_cp(None, 48),
    )(c_all, c_ctx, ada_w, ex, cx, ex_blk, cx_blk)


def _cctx_final(parts, c_ctx):
    def kern(p_ref, c_ref, o_ref):
        tot = p_ref[0, 0:1, :]
        for j in range(1, parts.shape[0]):
            tot = tot + p_ref[j, 0:1, :]
        c = c_ref[...]
        sg = _sigmoid(c)
        o_ref[...] = tot * (sg * (1.0 + c * (1.0 - sg)))

    return pl.pallas_call(kern, name="cctx_final", out_shape=jax.ShapeDtypeStruct((1, D), F32))(parts, c_ctx)


def _peer(kind, p, ix, iy, ic):
    if kind == "chips":
        return (p // 2, p % 2, ic)
    if kind == "all":
        return (p // 4, (p // 2) % 2, p % 2)
    return (ix, iy, p)


def _exchange(x, kind, bcast, name, chunks=1):
    npeer = {"chips": 4, "all": 8, "sib": 2}[kind]
    slab = x.shape if bcast else x.shape[1:]
    assert chunks == 1 or slab[0] == chunks

    def kern(x_ref, o_ref, send_sems, recv_sems, lsem):
        ix, iy, ic = lax.axis_index("x"), lax.axis_index("y"), lax.axis_index("c")
        me = {"chips": 2 * ix + iy, "all": 4 * ix + 2 * iy + ic, "sib": ic}[kind]
        own = pltpu.make_async_copy(x_ref if bcast else x_ref.at[me], o_ref.at[me], lsem)
        own.start()

        def part(ref, k):
            return ref if chunks == 1 else ref.at[k]

        def copy(p, k):
            return pltpu.make_async_remote_copy(
                src_ref=part(x_ref if bcast else x_ref.at[p], k), dst_ref=part(o_ref.at[me], k),
                send_sem=send_sems.at[p, k], recv_sem=recv_sems.at[me, k],
                device_id=_peer(kind, p, ix, iy, ic), device_id_type=MESH)

        def arrival(p, k):
            return pltpu.make_async_remote_copy(
                src_ref=part(x_ref if bcast else x_ref.at[p], k), dst_ref=part(o_ref.at[p], k),
                send_sem=send_sems.at[p, k], recv_sem=recv_sems.at[p, k],
                device_id=_peer(kind, p, ix, iy, ic), device_id_type=MESH)

        for p in range(npeer):
            @pl.when(me != p)
            def _():
                for k in range(chunks):
                    copy(p, k).start()
        for p in range(npeer):
            @pl.when(me != p)
            def _():
                for k in range(chunks):
                    arrival(p, k).wait_recv()
        for p in range(npeer):
            @pl.when(me != p)
            def _():
                for k in range(chunks):
                    copy(p, k).wait_send()
        own.wait()

    any_spec = pl.BlockSpec(memory_space=pl.ANY)
    return pl.pallas_call(
        kern, name=name, in_specs=[any_spec], out_specs=any_spec,
        out_shape=jax.ShapeDtypeStruct((npeer,) + tuple(slab), x.dtype),
        scratch_shapes=[pltpu.SemaphoreType.DMA((npeer, chunks)), pltpu.SemaphoreType.DMA((npeer, chunks)),
                        pltpu.SemaphoreType.DMA],
    )(x)


_HBM = pl.BlockSpec(memory_space=pltpu.HBM)
_SEM = pl.BlockSpec(memory_space=pltpu.SEMAPHORE)
_FLOWS = pltpu.SideEffectType.DATAFLOW_SIDE_EFFECTING


def _other_chips(ix, iy, ic):
    return ((1 - ix, iy, ic), (ix, 1 - iy, ic), (1 - ix, 1 - iy, ic))


def _chip_index(dev):
    return 2 * dev[0] + dev[1]


def _gather_start(x, name, scatter=False):
    def kern(x_ref, land_ref, send_sems, recv_sems, x_thru, land_thru, token):
        ix, iy, ic = lax.axis_index("x"), lax.axis_index("y"), lax.axis_index("c")
        me = 2 * ix + iy
        for k, peer in enumerate(_other_chips(ix, iy, ic)):
            src = x_ref.at[_chip_index(peer)] if scatter else x_ref
            pltpu.make_async_remote_copy(src_ref=src, dst_ref=land_ref.at[me], send_sem=send_sems.at[k],
                                         recv_sem=recv_sems.at[k], device_id=peer, device_id_type=MESH).start()
        token[...] = jnp.zeros(token.shape, token.dtype)

    land = lax.empty(x.shape if scatter else (4,) + x.shape, x.dtype)
    return pl.pallas_call(
        kern, name=name,
        out_shape=(pltpu.SemaphoreType.DMA((3,)), pltpu.SemaphoreType.DMA((3,)), pltpu.HBM(x.shape, x.dtype),
                   pltpu.HBM(land.shape, land.dtype), jax.ShapeDtypeStruct((8, 128), F32)),
        in_specs=(_HBM, _HBM), out_specs=(_SEM, _SEM, _HBM, _HBM, pl.BlockSpec(memory_space=pltpu.VMEM)),
        input_output_aliases={0: 2, 1: 3},
        compiler_params=pltpu.CompilerParams(has_side_effects=_FLOWS),
    )(pltpu.with_memory_space_constraint(x, pltpu.HBM), pltpu.with_memory_space_constraint(land, pltpu.HBM))


def _gather_wait(send_sems, recv_sems, x_thru, land_thru, after, name, scatter=False):
    def kern(x_ref, land_ref, send_sems_ref, recv_sems_ref, after_ref, x_dead, land_out):
        ix, iy, ic = lax.axis_index("x"), lax.axis_index("y"), lax.axis_index("c")
        for k, peer in enumerate(_other_chips(ix, iy, ic)):
            src = x_ref.at[_chip_index(peer)] if scatter else x_ref
            copy = pltpu.make_async_remote_copy(src_ref=src, dst_ref=land_ref.at[_chip_index(peer)],
                                                send_sem=send_sems_ref.at[k], recv_sem=recv_sems_ref.at[k],
                                                device_id=peer, device_id_type=MESH)
            copy.wait_send()
            copy.wait_recv()

    return pl.pallas_call(
        kern, name=name,
        out_shape=(pltpu.HBM(x_thru.shape, x_thru.dtype), pltpu.HBM(land_thru.shape, land_thru.dtype)),
        in_specs=(_HBM, _HBM, _SEM, _SEM, pl.BlockSpec(memory_space=pl.ANY)), out_specs=(_HBM, _HBM),
        input_output_aliases={0: 0, 1: 1},
        compiler_params=pltpu.CompilerParams(has_side_effects=_FLOWS),
    )(x_thru, land_thru, send_sems, recv_sems, after)


def _sum_slots(x, name):
    n, R, C = x.shape
    budget = (8 << 20) // (n * C * x.dtype.itemsize)
    tr = max([t for t in range(8, R + 1, 8) if R % t == 0 and t <= max(budget, 8)], default=R)

    def kern(x_ref, o_ref):
        tot = x_ref[0]
        for s in range(1, n):
            tot = tot + x_ref[s]
        o_ref[...] = tot

    return pl.pallas_call(
        kern, grid=(R // tr,), name=name,
        in_specs=[_bs((n, tr, C), lambda i: (0, i, 0))], out_specs=_bs((tr, C), lambda i: (i, 0)),
        out_shape=jax.ShapeDtypeStruct((R, C), x.dtype), compiler_params=_cp(("parallel",)),
    )(x)


def _sib_stream(x, me, name, add, nck=1):
    if add:
        nslab, rows, C = x.shape
        R = rows // (2 * nck)
        assert R * 2 * nck == rows and R % 8 == 0
        K = nslab * nck
    else:
        K, R, C = x.shape

    def kern(me_ref, *refs):
        if add:
            own_ref, send_ref, o_ref, rbuf, ssem, rsem, credit = refs
        else:
            send_ref, o_ref, rbuf, ssem, rsem, credit = refs
        k = pl.program_id(0)
        slot = k % 2
        sib = (lax.axis_index("x"), lax.axis_index("y"), 1 - lax.axis_index("c"))

        @pl.when(k >= 2)
        def _():
            pl.semaphore_wait(credit.at[slot], 1)

        cp = pltpu.make_async_remote_copy(src_ref=send_ref.at[0], dst_ref=rbuf.at[slot], send_sem=ssem.at[slot],
                                          recv_sem=rsem.at[slot], device_id=sib, device_id_type=MESH)
        cp.start()
        cp.wait_recv()
        o_ref[0] = own_ref[0] + rbuf[slot] if add else rbuf[slot]
        cp.wait_send()

        @pl.when(k + 2 < K)
        def _():
            pl.semaphore_signal(credit.at[slot], 1, device_id=sib, device_id_type=MESH)

    if add:
        in_specs = [_bs((1, R, C), lambda k, me_ref: (k // nck, me_ref[0] * nck + k % nck, 0)),
                    _bs((1, R, C), lambda k, me_ref: (k // nck, (1 - me_ref[0]) * nck + k % nck, 0))]
        args = (x, x)
    else:
        in_specs = [_bs((1, R, C), lambda k, me_ref: (k, 0, 0))]
        args = (x,)
    return pl.pallas_call(
        kern, name=name,
        grid_spec=pltpu.PrefetchScalarGridSpec(
            num_scalar_prefetch=1, grid=(K,), in_specs=in_specs,
            out_specs=_bs((1, R, C), lambda k, me_ref: (k, 0, 0)),
            scratch_shapes=[pltpu.VMEM((2, R, C), x.dtype), pltpu.SemaphoreType.DMA((2,)),
                            pltpu.SemaphoreType.DMA((2,)), pltpu.SemaphoreType.REGULAR((2,))]),
        out_shape=jax.ShapeDtypeStruct((K, R, C), x.dtype),
        compiler_params=_cp(("arbitrary",)),
    )(me, *args)


def _adamw(w, g, m, v, name):
    shape = w.shape
    if len(shape) == 1:
        outs = _adamw(*(t.reshape(1, -1) for t in (w, g, m, v)), name)
        return tuple(t.reshape(shape) for t in outs)
    nd = len(shape)
    size = 1
    for s in shape:
        size *= s
    rows = shape[-2]
    tr = rows
    if size > (1 << 18) and all(s == 1 for s in shape[:-2]):
        tr = max(t for t in (256, 128, 64, 32, 16, 8) if rows % t == 0)
    c1 = 1.0 - ADAM_B1 ** ADAM_STEP
    c2 = 1.0 - ADAM_B2 ** ADAM_STEP

    def kern(w_ref, g_ref, m_ref, v_ref, d_o, m_o, v_o):
        gv = g_ref[...]
        mn = ADAM_B1 * m_ref[...] + (1.0 - ADAM_B1) * gv
        vn = ADAM_B2 * v_ref[...] + (1.0 - ADAM_B2) * (gv * gv)
        m_o[...] = mn
        v_o[...] = vn
        d_o[...] = -ADAM_LR * ((mn / c1) / (jnp.sqrt(vn / c2) + ADAM_EPS) + ADAM_WD * w_ref[...])

    spec = _bs(shape[:-2] + (tr, shape[-1]), lambda i: (0,) * (nd - 2) + (i, 0))
    o = jax.ShapeDtypeStruct(shape, F32)
    return tuple(pl.pallas_call(
        kern, grid=(rows // tr,), name=name, in_specs=[spec] * 4, out_specs=[spec] * 3, out_shape=[o, o, o],
        compiler_params=_cp(("parallel",)),
    )(w, g, m, v))


_WEIGHT_NAMES = ("c_ctx", "ada_w", "ada_b", "mix_pre_g", "mix_post_g", "mlp_pre_g", "mlp_post_g", "w_in", "mu_prev",
                 "mu_next", "decay_w0", "decay_w2", "iclr_a0", "iclr_a2", "k_k", "k_a", "r_k", "gate_w2", "lnx_w",
                 "lnx_b", "conv_w", "conv_b", "conv_ln_w", "conv_ln_b", "w_out", "mlp_w1", "mlp_w2")


def _pack_rows(parts, cols=512):
    flat = jnp.concatenate([p.reshape(-1) for p in parts])
    rows = -(-flat.shape[0] // cols)
    rows = -(-rows // 16) * 16
    flat = jnp.pad(flat, (0, rows * cols - flat.shape[0]))
    return flat.reshape(rows, cols)


def _unpack(flat, shapes):
    out = []
    off = 0
    for s in shapes:
        n = 1
        for d in s:
            n *= d
        out.append(flat[off:off + n].reshape(s))
        off += n
    return out


def _local_step(T, x2, c2, tgt, modrows, P, late_weights, early_grads):
    p_rw, p_cv, h = _mix_in(T, x2, c2, modrows, P["mix_pre_g"], P["w_rw"], P["w_cv"])
    prep_params = (P["w0"], P["w2"], P["a0"], P["a2"], P["k_k"], P["k_a"])
    r, v, kk, dec, kd, bb, rw = _rwkv_prep(T, p_rw, P["mu_p"], P["mu_n"], *prep_params)
    y, fin, hist = _scan_fwd(T, r, v, kk, dec, kd, bb)
    P = dict(P, **late_weights(fin))
    ro_params = (P["r_k"], P["gate_w2"], P["lnx_w"], P["lnx_b"])
    cv_params = (P["conv_w"], P["conv_b"], P["conv_ln_w"], P["conv_ln_b"])
    cat, mix, x1, conv = _mix_out(T, y, kd, rw, p_cv, x2, modrows, *ro_params, *cv_params, P["mix_post_g"], P["w_out"])
    h2, loss_acc, dm, dx2, dg2, d_mlp_post = _mlp_fwd(T, x1, tgt, modrows, P["mlp_pre_g"], P["mlp_post_g"], P["wl"])
    fact, da, dx1, dmod2, d_mlp_pre = _mlp_bwd(T, h2, dm, x1, dx2, modrows, P["mlp_pre_g"], P["wl"])
    dmix, dcat, dg1, d_mix_post = _mix_post_bwd(T, dx1, mix, modrows, P["mix_post_g"], P["w_out"])
    kl = max(t for t in (1024, 512, 256) if T.NLAT % t == 0)
    dw_out = _matmul_tn(cat, dmix, "dw_out", TT, T.NL, 1024, bmap=T.tok)
    dw1 = _matmul_tn(h2, da, "dw_mlp1", kl, T.NLAT // kl, 1024)
    dw2m = _matmul_tn(fact, dm, "dw_mlp2", kl, T.NLAT // kl, 1024)
    fin = fin + early_grads(dw1, dw2m, dw_out)
    dp_cv, d_conv_w, d_conv_b, d_cln_w, d_cln_b = _conv_bwd(T, dcat, p_cv, conv, *cv_params)
    dy, dr_ro, dv_ro, dkbar, dgd, d_r_k, d_gate, d_lnx_w, d_lnx_b = _readout_bwd(T, dcat, y, kd, rw, *ro_params)
    dr_s, ddec, dkd, dv_s, da_s, dbb = _scan_bwd(T, dy, r, v, kk, dec, kd, bb, hist, fin)
    drw, d_w0, d_w2, d_a0, d_a2, d_k_k, d_k_a = _prep_bwd(T, rw, dr_s, ddec, dkd, dv_s, da_s, dbb, dr_ro, dv_ro,
                                                          dkbar, dgd, *prep_params)
    dp_rw, d_mu_p, d_mu_n = _shift_bwd(T, drw, p_rw, P["mu_p"], P["mu_n"])
    dxc, dmod1, d_mix_pre = _mix_in_bwd(T, dp_rw, dp_cv, x2, c2, dx1, modrows, P["mix_pre_g"], P["w_rw"], P["w_cv"])
    kt = max(t for t in (1024, 768, 512, 256) if T.NTOK % t == 0)
    dw_rw = _matmul_tn(h, dp_rw, "dw_in_rw", kt, T.NTOK // kt, 768)
    dw_cv = _matmul_tn(h, dp_cv, "dw_in_cv", kt, T.NTOK // kt, 1024)
    small = dict(mix_pre_g=d_mix_pre, mix_post_g=d_mix_post, mlp_pre_g=d_mlp_pre, mlp_post_g=d_mlp_post,
                 mu_p=d_mu_p, mu_n=d_mu_n, w0=d_w0, w2=d_w2, a0=d_a0, a2=d_a2, k_k=d_k_k, k_a=d_k_a, r_k=d_r_k,
                 gate_w2=d_gate, lnx_w=d_lnx_w, lnx_b=d_lnx_b, conv_w=d_conv_w, conv_b=d_conv_b,
                 conv_ln_w=d_cln_w, conv_ln_b=d_cln_b)
    big = dict(w_rw=dw_rw, w_cv=dw_cv, w_out=dw_out, w1=dw1, w2m=dw2m, after_scan=dr_s)
    dmods = dict(dmod1=dmod1, dg1=dg1, dmod2=dmod2, dg2=dg2)
    return loss_acc[0, 0], dxc, small, big, dmods


_SMALL_ORDER = ("mix_pre_g", "mix_post_g", "mlp_pre_g", "mlp_post_g", "mu_p", "mu_n", "w0", "w2", "a0", "a2", "k_k",
                "k_a", "r_k", "gate_w2", "lnx_w", "lnx_b", "conv_w", "conv_b", "conv_ln_w", "conv_ln_b")


def kernel(x, c, ctx, c_ctx, ada_w, ada_b, mix_pre_g, mix_post_g, mlp_pre_g, mlp_post_g, w_in, mu_prev, mu_next, decay_w0, decay_w2, iclr_a0, iclr_a2, k_k, k_a, r_k, gate_w2, lnx_w, lnx_b, conv_w, conv_b, conv_ln_w, conv_ln_b, w_out, mlp_w1, mlp_w2, loss_target, m_c_ctx, m_ada_w, m_ada_b, m_mix_pre_g, m_mix_post_g, m_mlp_pre_g, m_mlp_post_g, m_w_in, m_mu_prev, m_mu_next, m_decay_w0, m_decay_w2, m_iclr_a0, m_iclr_a2, m_k_k, m_k_a, m_r_k, m_gate_w2, m_lnx_w, m_lnx_b, m_conv_w, m_conv_b, m_conv_ln_w, m_conv_ln_b, m_w_out, m_mlp_w1, m_mlp_w2, v_c_ctx, v_ada_w, v_ada_b, v_mix_pre_g, v_mix_post_g, v_mlp_pre_g, v_mlp_post_g, v_w_in, v_mu_prev, v_mu_next, v_decay_w0, v_decay_w2, v_iclr_a0, v_iclr_a2, v_k_k, v_k_a, v_r_k, v_gate_w2, v_lnx_w, v_lnx_b, v_conv_w, v_conv_b, v_conv_ln_w, v_conv_ln_b, v_w_out, v_mlp_w1, v_mlp_w2):
    weights = dict(zip(_WEIGHT_NAMES, (c_ctx, ada_w, ada_b, mix_pre_g, mix_post_g, mlp_pre_g, mlp_post_g, w_in, mu_prev, mu_next, decay_w0, decay_w2, iclr_a0, iclr_a2, k_k, k_a, r_k, gate_w2, lnx_w, lnx_b, conv_w, conv_b, conv_ln_w, conv_ln_b, w_out, mlp_w1, mlp_w2)))
    moms = dict(zip(_WEIGHT_NAMES, (m_c_ctx, m_ada_w, m_ada_b, m_mix_pre_g, m_mix_post_g, m_mlp_pre_g, m_mlp_post_g, m_w_in, m_mu_prev, m_mu_next, m_decay_w0, m_decay_w2, m_iclr_a0, m_iclr_a2, m_k_k, m_k_a, m_r_k, m_gate_w2, m_lnx_w, m_lnx_b, m_conv_w, m_conv_b, m_conv_ln_w, m_conv_ln_b, m_w_out, m_mlp_w1, m_mlp_w2)))
    vars_ = dict(zip(_WEIGHT_NAMES, (v_c_ctx, v_ada_w, v_ada_b, v_mix_pre_g, v_mix_post_g, v_mlp_pre_g, v_mlp_post_g, v_w_in, v_mu_prev, v_mu_next, v_decay_w0, v_decay_w2, v_iclr_a0, v_iclr_a2, v_k_k, v_k_a, v_r_k, v_gate_w2, v_lnx_w, v_lnx_b, v_conv_w, v_conv_b, v_conv_ln_w, v_conv_ln_b, v_w_out, v_mlp_w1, v_mlp_w2)))

    B, t_lat, _ = x.shape
    assert ctx.shape[1] == TT and t_lat % TT == 0 and (t_lat * B) % MT == 0
    T = _Tiles(B, t_lat)
    ix, iy, ic = lax.axis_index("x"), lax.axis_index("y"), lax.axis_index("c")
    chip = 2 * ix + iy
    dev = 4 * ix + 2 * iy + ic
    nsh = 4
    in_sh = w_in.shape[2]
    ada_sh = ada_w.shape[2]
    lane_sh = decay_w0.shape[2]

    in_sems_s, in_sems_r, in_x, in_land, in_token = _gather_start(w_in[0].astype(BF16), "gather_w_in_start")
    late_pack = jnp.concatenate([mlp_w1[0], mlp_w2[0], w_out[0]], axis=0).astype(BF16)
    late_sems_s, late_sems_r, late_x, late_land, late_token = _gather_start(late_pack, "gather_mlp_start")
    n_w1, n_w2 = mlp_w1.shape[1], mlp_w2.shape[1]

    def late_weights(after):
        own, land = _gather_wait(late_sems_s, late_sems_r, late_x, late_land, after, "gather_mlp_wait")
        wl = lax.dynamic_update_slice(land, own[None], (chip, 0, 0))
        return dict(wl=wl, w_out=jnp.concatenate([wl[j, n_w1 + n_w2:] for j in range(nsh)], axis=0))

    sm_parts = (decay_w0[0], decay_w2[0], iclr_a0[0], iclr_a2[0], gate_w2[0], conv_w[0])
    sm_shapes = [p.shape for p in sm_parts]
    sg = _exchange(_pack_rows(sm_parts), "chips", True, "gather_small_weights")
    pers = [_unpack(sg[j].reshape(-1), sm_shapes) for j in range(nsh)]
    w0_f, w2_f_, a0_f, a2_f, gate_f, convw_f = (jnp.concatenate([pers[j][t] for j in range(nsh)], axis=-1)
                                                for t in range(6))

    def pad_rows(a, n):
        return jnp.pad(a, [(0, 0)] * (a.ndim - 2) + [(0, n - a.shape[-2]), (0, 0)])

    c_ctx2 = c_ctx.reshape(1, D)
    c_all = _exchange(jnp.pad(c + in_token[0, 0], ((0, 8 - B), (0, 0))), "all", True, "gather_c")[:, :B]
    c_all = c_all.reshape(8 * B, D)
    ada_b_blk = lax.dynamic_slice(ada_b, (0, chip * ada_sh), (1, ada_sh))
    mod_blk = _ada_fwd(c_all, c_ctx2, ada_w[0], ada_b_blk)
    mod_g = _exchange(mod_blk, "chips", True, "gather_mod")
    mod_all = jnp.concatenate([mod_g[j] for j in range(nsh)], axis=1)
    mod_x = lax.dynamic_slice(mod_all, (dev * B, 0), (B, 6 * D)).reshape(B, 6, D)
    mod_c = jnp.broadcast_to(mod_all[8 * B].reshape(1, 6, D), (B, 6, D))
    modrows = jnp.stack([mod_c, mod_x], axis=1) + late_token[0, 0]

    own_in, land_in = _gather_wait(in_sems_s, in_sems_r, in_x, in_land, mod_all, "gather_w_in_wait")
    wg_in = lax.dynamic_update_slice(land_in, own_in[None], (chip, 0, 0))
    w_in_f = jnp.concatenate([wg_in[j] for j in range(nsh)], axis=1)
    w_in_p = _pad_cols(w_in_f, w_in_f.shape[1])
    P = dict(
        w_rw=w_in_p[:, :RWC], w_cv=w_in_p[:, RWC:],
        mix_pre_g=mix_pre_g, mix_post_g=mix_post_g, mlp_pre_g=mlp_pre_g, mlp_post_g=mlp_post_g,
        mu_p=_pad_cols(mu_prev, mu_prev.shape[1]), mu_n=_pad_cols(mu_next, mu_next.shape[1]),
        w0=w0_f, w2=pad_rows(w2_f_, LRW), a0=a0_f, a2=pad_rows(a2_f, LRW), k_k=k_k, k_a=k_a,
        r_k=r_k.reshape(1, W), gate_w2=pad_rows(gate_f, GDW), lnx_w=lnx_w, lnx_b=lnx_b,
        conv_w=pad_rows(convw_f, 32), conv_b=conv_b, conv_ln_w=conv_ln_w, conv_ln_b=conv_ln_b)

    x2 = x.reshape(T.NLAT, D)
    c2 = ctx.reshape(B * TT, D)
    tgt = loss_target.reshape(T.NLAT, D)
    me1 = ic.reshape(1).astype(jnp.int32)
    early = {}

    def early_grads(dw1, dw2, dw_out):
        c_1, n_o = mlp_w1.shape[2], w_out.shape[1]
        slabs = jnp.stack([jnp.concatenate([dw1[:, c_1 * j:c_1 * (j + 1)], dw2[n_w2 * j:n_w2 * (j + 1)],
                                            dw_out[n_o * j:n_o * (j + 1)]], axis=0) for j in range(nsh)])
        pair = _sib_stream(slabs, me1, "sib_reduce_mlp_grads", True, nck=3)
        send, recv, x_thru, land, token = _gather_start(pair.reshape(nsh, -1, slabs.shape[2]), "reduce_mlp_start",
                                                        scatter=True)
        early.update(send=send, recv=recv, x=x_thru, land=land)
        return token[0, 0]

    loss_loc, dxl, small, big, dm_ = _local_step(T, x2, c2, tgt, modrows, P, late_weights, early_grads)
    loss = lax.psum(loss_loc, ("x", "y", "c"))
    grad_x = dxl.reshape(x.shape)

    dw_in_f = _unpad_cols(jnp.concatenate([big["w_rw"], big["w_cv"]], axis=1), w_in_f.shape[1])
    slabs_in = jnp.stack([dw_in_f[:, in_sh * j:in_sh * (j + 1)] for j in range(nsh)])
    pair_in = _sib_stream(slabs_in, me1, "sib_reduce_w_in", True, nck=2).reshape(nsh, -1, in_sh)
    win_s, win_r, win_x, win_land, win_token = _gather_start(pair_in, "reduce_w_in_start", scatter=True)

    dmod_x = jnp.concatenate([dm_["dmod1"][:, 1], dm_["dg1"], dm_["dmod2"], dm_["dg2"]], axis=1)
    dmod_c = jnp.concatenate([dm_["dmod1"][:, 0], jnp.zeros((B, 4, D), F32)], axis=1)
    dpack = jnp.concatenate([dmod_x.reshape(B, 6 * D), dmod_c.reshape(B, 6 * D)], axis=0)
    dpack = dpack + win_token[0, 0]
    dg = _exchange(dpack, "all", True, "gather_dmod")
    ex = dg[:, :B].reshape(8 * B, 6 * D)
    cx = dg[:, B:].reshape(8 * B, 6 * D)
    ex_blk = lax.dynamic_slice(ex, (0, chip * ada_sh), (8 * B, ada_sh))
    cx_blk = lax.dynamic_slice(cx, (0, chip * ada_sh), (8 * B, ada_sh))
    g_ada_w, g_ada_b, dscc = _ada_bwd(c_all, c_ctx2, ada_w[0], ex, cx, ex_blk, cx_blk)
    dscc_g = _exchange(dscc, "chips", True, "gather_dcctx")
    g_c_ctx = _cctx_final(dscc_g, c_ctx2).reshape(D)

    small = dict(small, mu_p=_unpad_cols(small["mu_p"], mu_prev.shape[1]),
                 mu_n=_unpad_cols(small["mu_n"], mu_next.shape[1]),
                 w2=small["w2"][:, :decay_w2.shape[2]], a2=small["a2"][:, :iclr_a2.shape[2]],
                 gate_w2=small["gate_w2"][:gate_w2.shape[1]], conv_w=small["conv_w"][:KCONV])
    sm_list = [small[n] for n in _SMALL_ORDER]
    sm_shapes2 = [a.shape for a in sm_list]
    sm_pack = _pack_rows(sm_list) + win_token[0, 0]
    sm_pair = _sib_stream(jnp.concatenate([sm_pack, sm_pack], axis=0)[None], me1, "sib_small_grads", True)[0]
    sm_tot = _sum_slots(_exchange(sm_pair, "chips", True, "gather_small_grads"), "sum_small_grads")
    S = dict(zip(_SMALL_ORDER, _unpack(sm_tot.reshape(-1), sm_shapes2)))

    def shard_last(a):
        return lax.dynamic_slice_in_dim(a, chip * lane_sh, lane_sh, axis=a.ndim - 1)

    grads = dict(
        c_ctx=g_c_ctx, ada_w=g_ada_w[None], ada_b=g_ada_b,
        mix_pre_g=S["mix_pre_g"], mix_post_g=S["mix_post_g"], mlp_pre_g=S["mlp_pre_g"], mlp_post_g=S["mlp_post_g"],
        mu_prev=S["mu_p"], mu_next=S["mu_n"],
        decay_w0=shard_last(S["w0"])[None], decay_w2=shard_last(S["w2"])[None],
        iclr_a0=shard_last(S["a0"])[None], iclr_a2=shard_last(S["a2"])[None],
        k_k=S["k_k"], k_a=S["k_a"], r_k=S["r_k"].reshape(r_k.shape),
        gate_w2=shard_last(S["gate_w2"])[None], lnx_w=S["lnx_w"], lnx_b=S["lnx_b"],
        conv_w=shard_last(S["conv_w"])[None], conv_b=S["conv_b"], conv_ln_w=S["conv_ln_w"],
        conv_ln_b=S["conv_ln_b"])

    def both_halves(mine_, name, nck):
        chunks = mine_.reshape(nck, mine_.shape[0] // nck, mine_.shape[1])
        other = _sib_stream(chunks, me1, name, False)
        full = jnp.where(ic == 0, jnp.concatenate([chunks, other], axis=0), jnp.concatenate([other, chunks], axis=0))
        return full.reshape(2 * mine_.shape[0], mine_.shape[1])

    own, land = _gather_wait(early["send"], early["recv"], early["x"], early["land"], big["after_scan"],
                             "reduce_mlp_wait", scatter=True)
    land = lax.dynamic_update_slice(land, lax.dynamic_index_in_dim(own, chip, 0, keepdims=True), (chip, 0, 0))
    tot = both_halves(_sum_slots(land, "sum_mlp_grads"), "sib_swap_mlp_grads", 3)
    g_w1, g_w2, g_w_out = tot[:n_w1], tot[n_w1:n_w1 + n_w2], tot[n_w1 + n_w2:]

    own, land = _gather_wait(win_s, win_r, win_x, win_land, sm_tot, "reduce_w_in_wait", scatter=True)
    land = lax.dynamic_update_slice(land, lax.dynamic_index_in_dim(own, chip, 0, keepdims=True), (chip, 0, 0))
    g_w_in = both_halves(_sum_slots(land, "sum_w_in_grads"), "sib_swap_w_in", 2)
    grads.update(w_in=g_w_in[None], w_out=g_w_out[None], mlp_w1=g_w1[None], mlp_w2=g_w2[None])

    deltas, new_m, new_v = {}, {}, {}
    for n in _WEIGHT_NAMES:
        g = grads[n].reshape(weights[n].shape)
        grads[n] = g
        deltas[n], new_m[n], new_v[n] = _adamw(weights[n], g, moms[n], vars_[n], "adamw_" + n)

    return (loss, grad_x, *[grads[n] for n in _WEIGHT_NAMES], *[deltas[n] for n in _WEIGHT_NAMES],
            *[new_m[n] for n in _WEIGHT_NAMES], *[new_v[n] for n in _WEIGHT_NAMES])
```
